```python
import jax, jax.numpy as jnp
from jax import lax
import numpy as np

D_MODEL = 1024
BATCH = 16
SEQ = 2048
DEPTH = 1

SB_HEADS = 8
SB_HEAD_DIM = 64
MLA_HEADS = 8
MLA_NOPE_DIM = 64
MLA_ROPE_DIM = 32
MLA_V_DIM = 64
Q_LORA_RANK = 384
KV_LORA_RANK = 256
D_FF = 2816
CONV_WIDTH = 3
BLOCK_Q = 128
ROPE_BASE = 10000.0
EPS = 1e-6

SB_WIDTH = SB_HEADS * SB_HEAD_DIM
MLA_WIDTH = MLA_HEADS * MLA_V_DIM
MIX_WIDTH = SB_WIDTH + MLA_WIDTH
IN_COLS = 3 * SB_WIDTH + Q_LORA_RANK + KV_LORA_RANK + MLA_ROPE_DIM
MLA_QK_DIM = MLA_NOPE_DIM + MLA_ROPE_DIM

kernel_name = "hymba_style_stickbreak_mla_convffn"


def rmsnorm(x, g):
    xf = x.astype(jnp.float32)
    y = xf * lax.rsqrt(jnp.mean(xf * xf, axis=-1, keepdims=True) + EPS)
    return (y * g.astype(jnp.float32)).astype(x.dtype)


def rope_tables(positions, dim):
    half = dim // 2
    inv_freq = 1.0 / (ROPE_BASE ** (jnp.arange(half, dtype=jnp.float32) * (2.0 / dim)))
    ang = positions.astype(jnp.float32)[..., None] * inv_freq
    return jnp.cos(ang), jnp.sin(ang)


def apply_rope(x, cos, sin):
    half = x.shape[-1] // 2
    xf = x.astype(jnp.float32)
    x1, x2 = xf[..., :half], xf[..., half:]
    out = jnp.concatenate([x1 * cos - x2 * sin, x2 * cos + x1 * sin], axis=-1)
    return out.astype(x.dtype)


def stick_breaking_attention(q, k, v):
    S, Dh = q.shape[1], q.shape[-1]
    scale = Dh ** -0.5
    outs = []
    for i in range(S // BLOCK_Q):
        q0 = i * BLOCK_Q
        kv_len = q0 + BLOCK_Q
        qb = q[:, q0:kv_len]
        kb = k[:, :kv_len]
        vb = v[:, :kv_len]
        z = jnp.einsum('bqhd,bkhd->bhqk', qb, kb, preferred_element_type=jnp.float32) * scale
        t_idx = q0 + jnp.arange(BLOCK_Q)[:, None]
        s_idx = jnp.arange(kv_len)[None, :]
        visible = s_idx < t_idx
        log_beta = jax.nn.log_sigmoid(z)
        log_keep = jnp.where(visible, jax.nn.log_sigmoid(-z), 0.0)
        tail = lax.cumsum(log_keep, axis=3, reverse=True) - log_keep
        a = jnp.where(visible, jnp.exp(log_beta + tail), 0.0)
        outs.append(jnp.einsum('bhqk,bkhd->bqhd', a.astype(v.dtype), vb))
    return jnp.concatenate(outs, axis=1)


def mla_attention(q_nope, q_rope, k_nope, k_rope, v):
    S = q_nope.shape[1]
    scale = MLA_QK_DIM ** -0.5
    outs = []
    for i in range(S // BLOCK_Q):
        q0 = i * BLOCK_Q
        kv_len = q0 + BLOCK_Q
        s = (jnp.einsum('bqhd,bkhd->bhqk', q_nope[:, q0:kv_len], k_nope[:, :kv_len],
                        preferred_element_type=jnp.float32)
             + jnp.einsum('bqhr,bkr->bhqk', q_rope[:, q0:kv_len], k_rope[:, :kv_len],
                          preferred_element_type=jnp.float32)) * scale
        t_idx = q0 + jnp.arange(BLOCK_Q)[:, None]
        s_idx = jnp.arange(kv_len)[None, :]
        s = jnp.where(s_idx <= t_idx, s, -jnp.inf)
        p = jax.nn.softmax(s, axis=-1)
        outs.append(jnp.einsum('bhqk,bkhd->bqhd', p.astype(v.dtype), v[:, :kv_len]))
    return jnp.concatenate(outs, axis=1)


def causal_depthwise_conv(h, w, b):
    C = h.shape[-1]
    y = lax.conv_general_dilated(
        h, w[:, None, :].astype(h.dtype), window_strides=(1,),
        padding=[(CONV_WIDTH - 1, 0)],
        dimension_numbers=('NWC', 'WIO', 'NWC'),
        feature_group_count=C)
    return y + b.astype(h.dtype)


def _fwd_setup_inputs(seed: int = 0) -> dict:
    key = jax.random.key(seed)
    ks = jax.random.split(key, 20)
    f32 = jnp.float32

    def nrm(k, shape, fan_in):
        return jax.random.normal(k, shape, f32) * (fan_in ** -0.5)

    def gain(k, shape):
        return 1.0 + 0.02 * jax.random.normal(k, shape, f32)

    x = jax.random.normal(ks[0], (BATCH, SEQ, D_MODEL), f32)
    offset = jax.random.randint(ks[1], (BATCH, 1), 0, 1024, dtype=jnp.int32)
    positions = (jnp.arange(SEQ, dtype=jnp.int32)[None, :] + offset).astype(jnp.int32)
    return {
        "x": x,
        "positions": positions,
        "g_mix": gain(ks[2], (DEPTH, D_MODEL)),
        "w_in": nrm(ks[3], (DEPTH, D_MODEL, IN_COLS), D_MODEL),
        "g_cq": gain(ks[4], (DEPTH, Q_LORA_RANK)),
        "w_uq": nrm(ks[5], (DEPTH, Q_LORA_RANK, MLA_HEADS * MLA_QK_DIM), Q_LORA_RANK),
        "g_ckv": gain(ks[6], (DEPTH, KV_LORA_RANK)),
        "w_ukv": nrm(ks[7], (DEPTH, KV_LORA_RANK, MLA_HEADS * (MLA_NOPE_DIM + MLA_V_DIM)), KV_LORA_RANK),
        "g_sb_out": gain(ks[8], (DEPTH, SB_WIDTH)),
        "g_mla_out": gain(ks[9], (DEPTH, MLA_WIDTH)),
        "w_out": nrm(ks[10], (DEPTH, MIX_WIDTH, D_MODEL), MIX_WIDTH),
        "g_ffn": gain(ks[11], (DEPTH, D_MODEL)),
        "w_up": nrm(ks[12], (DEPTH, D_MODEL, 2 * D_FF), D_MODEL),
        "conv_w": nrm(ks[13], (DEPTH, CONV_WIDTH, 2 * D_FF), CONV_WIDTH),
        "conv_b": 0.01 * jax.random.normal(ks[14], (DEPTH, 2 * D_FF), f32),
        "w_down": nrm(ks[15], (DEPTH, D_FF, D_MODEL), D_FF),
        "g_final": gain(ks[16], (D_MODEL,)),
    }


def _fwd_reference(x, positions, g_mix, w_in, g_cq, w_uq, g_ckv, w_ukv, g_sb_out, g_mla_out,
              w_out, g_ffn, w_up, conv_w, conv_b, w_down, g_final):
    B, S, _ = x.shape
    cos, sin = rope_tables(positions, MLA_ROPE_DIM)
    cos_h, sin_h = cos[:, :, None, :], sin[:, :, None, :]
    split_at = np.cumsum([SB_WIDTH, SB_WIDTH, SB_WIDTH, Q_LORA_RANK, KV_LORA_RANK])

    for l in range(DEPTH):
        h = rmsnorm(x, g_mix[l])
        p = h @ w_in[l]
        q_sb, k_sb, v_sb, c_q, c_kv, k_rope = jnp.split(p, split_at, axis=-1)

        o_sb = stick_breaking_attention(
            q_sb.reshape(B, S, SB_HEADS, SB_HEAD_DIM),
            k_sb.reshape(B, S, SB_HEADS, SB_HEAD_DIM),
            v_sb.reshape(B, S, SB_HEADS, SB_HEAD_DIM)).reshape(B, S, SB_WIDTH)

        q = (rmsnorm(c_q, g_cq[l]) @ w_uq[l]).reshape(B, S, MLA_HEADS, MLA_QK_DIM)
        q_nope, q_rope = q[..., :MLA_NOPE_DIM], q[..., MLA_NOPE_DIM:]
        q_rope = apply_rope(q_rope, cos_h, sin_h)
        kv = (rmsnorm(c_kv, g_ckv[l]) @ w_ukv[l]).reshape(B, S, MLA_HEADS, MLA_NOPE_DIM + MLA_V_DIM)
        k_nope, v_mla = kv[..., :MLA_NOPE_DIM], kv[..., MLA_NOPE_DIM:]
        k_rope = apply_rope(k_rope, cos, sin)
        o_mla = mla_attention(q_nope, q_rope, k_nope, k_rope, v_mla).reshape(B, S, MLA_WIDTH)

        o = jnp.concatenate([rmsnorm(o_sb, g_sb_out[l]), rmsnorm(o_mla, g_mla_out[l])], axis=-1)
        x = x + o @ w_out[l]

        u = rmsnorm(x, g_ffn[l]) @ w_up[l]
        u = causal_depthwise_conv(u, conv_w[l], conv_b[l])
        gate, val = u[..., :D_FF], u[..., D_FF:]
        x = x + (jax.nn.silu(gate) * val) @ w_down[l]

    return rmsnorm(x, g_final)


import jax as _jax
import jax.numpy as _jnp

TWIN_FORMAT = 'train_step'
FWD_PARAMS = ['x', 'positions', 'g_mix', 'w_in', 'g_cq', 'w_uq', 'g_ckv', 'w_ukv', 'g_sb_out', 'g_mla_out', 'w_out', 'g_ffn', 'w_up', 'conv_w', 'conv_b', 'w_down', 'g_final']
TWIN_WEIGHTS = ['g_mix', 'w_in', 'g_cq', 'w_uq', 'g_ckv', 'w_ukv', 'g_sb_out', 'g_mla_out', 'w_out', 'g_ffn', 'w_up', 'conv_w', 'conv_b', 'w_down', 'g_final']
TWIN_DIFF_INPUT = 'x'
TWIN_INPUTS = ['x', 'positions', 'g_mix', 'w_in', 'g_cq', 'w_uq', 'g_ckv', 'w_ukv', 'g_sb_out', 'g_mla_out', 'w_out', 'g_ffn', 'w_up', 'conv_w', 'conv_b', 'w_down', 'g_final', 'loss_target', 'm_g_mix', 'm_w_in', 'm_g_cq', 'm_w_uq', 'm_g_ckv', 'm_w_ukv', 'm_g_sb_out', 'm_g_mla_out', 'm_w_out', 'm_g_ffn', 'm_w_up', 'm_conv_w', 'm_conv_b', 'm_w_down', 'm_g_final', 'v_g_mix', 'v_w_in', 'v_g_cq', 'v_w_uq', 'v_g_ckv', 'v_w_ukv', 'v_g_sb_out', 'v_g_mla_out', 'v_w_out', 'v_g_ffn', 'v_w_up', 'v_conv_w', 'v_conv_b', 'v_w_down', 'v_g_final']
TWIN_OUTPUTS = ['loss', 'grad_x', 'grad_g_mix', 'grad_w_in', 'grad_g_cq', 'grad_w_uq', 'grad_g_ckv', 'grad_w_ukv', 'grad_g_sb_out', 'grad_g_mla_out', 'grad_w_out', 'grad_g_ffn', 'grad_w_up', 'grad_conv_w', 'grad_conv_b', 'grad_w_down', 'grad_g_final', 'delta_g_mix', 'delta_w_in', 'delta_g_cq', 'delta_w_uq', 'delta_g_ckv', 'delta_w_ukv', 'delta_g_sb_out', 'delta_g_mla_out', 'delta_w_out', 'delta_g_ffn', 'delta_w_up', 'delta_conv_w', 'delta_conv_b', 'delta_w_down', 'delta_g_final', 'new_m_g_mix', 'new_m_w_in', 'new_m_g_cq', 'new_m_w_uq', 'new_m_g_ckv', 'new_m_w_ukv', 'new_m_g_sb_out', 'new_m_g_mla_out', 'new_m_w_out', 'new_m_g_ffn', 'new_m_w_up', 'new_m_conv_w', 'new_m_conv_b', 'new_m_w_down', 'new_m_g_final', 'new_v_g_mix', 'new_v_w_in', 'new_v_g_cq', 'new_v_w_uq', 'new_v_g_ckv', 'new_v_w_ukv', 'new_v_g_sb_out', 'new_v_g_mla_out', 'new_v_w_out', 'new_v_g_ffn', 'new_v_w_up', 'new_v_conv_w', 'new_v_conv_b', 'new_v_w_down', 'new_v_g_final']
TWIN_LEAF_KINDS = {'loss': 'loss', 'grad_x': 'grad_x', 'grad_g_mix': 'grad_w', 'grad_w_in': 'grad_w', 'grad_g_cq': 'grad_w', 'grad_w_uq': 'grad_w', 'grad_g_ckv': 'grad_w', 'grad_w_ukv': 'grad_w', 'grad_g_sb_out': 'grad_w', 'grad_g_mla_out': 'grad_w', 'grad_w_out': 'grad_w', 'grad_g_ffn': 'grad_w', 'grad_w_up': 'grad_w', 'grad_conv_w': 'grad_w', 'grad_conv_b': 'grad_w', 'grad_w_down': 'grad_w', 'grad_g_final': 'grad_w', 'delta_g_mix': 'delta_w', 'delta_w_in': 'delta_w', 'delta_g_cq': 'delta_w', 'delta_w_uq': 'delta_w', 'delta_g_ckv': 'delta_w', 'delta_w_ukv': 'delta_w', 'delta_g_sb_out': 'delta_w', 'delta_g_mla_out': 'delta_w', 'delta_w_out': 'delta_w', 'delta_g_ffn': 'delta_w', 'delta_w_up': 'delta_w', 'delta_conv_w': 'delta_w', 'delta_conv_b': 'delta_w', 'delta_w_down': 'delta_w', 'delta_g_final': 'delta_w', 'new_m_g_mix': 'new_m', 'new_m_w_in': 'new_m', 'new_m_g_cq': 'new_m', 'new_m_w_uq': 'new_m', 'new_m_g_ckv': 'new_m', 'new_m_w_ukv': 'new_m', 'new_m_g_sb_out': 'new_m', 'new_m_g_mla_out': 'new_m', 'new_m_w_out': 'new_m', 'new_m_g_ffn': 'new_m', 'new_m_w_up': 'new_m', 'new_m_conv_w': 'new_m', 'new_m_conv_b': 'new_m', 'new_m_w_down': 'new_m', 'new_m_g_final': 'new_m', 'new_v_g_mix': 'new_v', 'new_v_w_in': 'new_v', 'new_v_g_cq': 'new_v', 'new_v_w_uq': 'new_v', 'new_v_g_ckv': 'new_v', 'new_v_w_ukv': 'new_v', 'new_v_g_sb_out': 'new_v', 'new_v_g_mla_out': 'new_v', 'new_v_w_out': 'new_v', 'new_v_g_ffn': 'new_v', 'new_v_w_up': 'new_v', 'new_v_conv_w': 'new_v', 'new_v_conv_b': 'new_v', 'new_v_w_down': 'new_v', 'new_v_g_final': 'new_v'}


def _forward(args):
    return _fwd_reference(*[args[k] for k in FWD_PARAMS])


def _output_shape():
    out = _jax.eval_shape(lambda: _forward(_fwd_setup_inputs(0)))
    return out.shape, out.dtype

N_MICROBATCH = 1
ADAM_LR = 0.001
ADAM_B1 = 0.9
ADAM_B2 = 0.999
ADAM_EPS = 1e-08
ADAM_WD = 0.01
ADAM_STEP = 10
PER_EXAMPLE_BATCH_AXIS = {'x': 0, 'positions': 0, 'loss_target': 0}
SHARED_INPUTS = []
_WEIGHT_DTYPES = {'g_mix': _jnp.float32, 'w_in': _jnp.float32, 'g_cq': _jnp.float32, 'w_uq': _jnp.float32, 'g_ckv': _jnp.float32, 'w_ukv': _jnp.float32, 'g_sb_out': _jnp.float32, 'g_mla_out': _jnp.float32, 'w_out': _jnp.float32, 'g_ffn': _jnp.float32, 'w_up': _jnp.float32, 'conv_w': _jnp.float32, 'conv_b': _jnp.float32, 'w_down': _jnp.float32, 'g_final': _jnp.float32}
MOMENT_SCALE = {'g_mix': 1.953102e-01, 'w_in': 1.310937e-01, 'g_cq': 1.533172e-01, 'w_uq': 1.177702e-01, 'g_ckv': 3.148763e-01, 'w_ukv': 1.357948e-01, 'g_sb_out': 1.346455e-01, 'g_mla_out': 1.502823e-01, 'w_out': 1.378121e-01, 'g_ffn': 1.088714e-01, 'w_up': 4.447767e-02, 'conv_w': 4.404851e-02, 'conv_b': 4.325552e-02, 'w_down': 7.304401e-02, 'g_final': 3.204555e+01}


def _to_microbatches(a, axis):
    t = _jnp.moveaxis(a, axis, 0)
    t = t.reshape((N_MICROBATCH, t.shape[0] // N_MICROBATCH) + t.shape[1:])
    return _jnp.moveaxis(t, 1, axis + 1)


def setup_inputs(seed: int = 0) -> dict:
    inp = _fwd_setup_inputs(seed)
    key = _jax.random.fold_in(_jax.random.key(seed), 7919)
    shape, _ = _output_shape()
    out = dict(inp)
    out["loss_target"] = _jax.random.normal(_jax.random.fold_in(key, 0), shape, _jnp.float32)
    for i, name in enumerate(TWIN_WEIGHTS):
        w = inp[name].astype(_jnp.float32)
        if MOMENT_SCALE is None:
            s = _jnp.sqrt(_jnp.mean(_jnp.square(w)) + 1e-30)
        else:
            s = MOMENT_SCALE[name]
        km, kv = _jax.random.split(_jax.random.fold_in(key, i + 1))
        out[name] = w
        out["m_" + name] = s * _jax.random.normal(km, w.shape, _jnp.float32)
        out["v_" + name] = (s * s) * _jax.random.uniform(kv, w.shape, _jnp.float32, 0.5, 1.5)
    if N_MICROBATCH > 1:
        for name, axis in PER_EXAMPLE_BATCH_AXIS.items():
            out[name] = _to_microbatches(out[name], axis)
    return {'x': out['x'], 'positions': out['positions'], 'g_mix': out['g_mix'], 'w_in': out['w_in'], 'g_cq': out['g_cq'], 'w_uq': out['w_uq'], 'g_ckv': out['g_ckv'], 'w_ukv': out['w_ukv'], 'g_sb_out': out['g_sb_out'], 'g_mla_out': out['g_mla_out'], 'w_out': out['w_out'], 'g_ffn': out['g_ffn'], 'w_up': out['w_up'], 'conv_w': out['conv_w'], 'conv_b': out['conv_b'], 'w_down': out['w_down'], 'g_final': out['g_final'], 'loss_target': out['loss_target'], 'm_g_mix': out['m_g_mix'], 'm_w_in': out['m_w_in'], 'm_g_cq': out['m_g_cq'], 'm_w_uq': out['m_w_uq'], 'm_g_ckv': out['m_g_ckv'], 'm_w_ukv': out['m_w_ukv'], 'm_g_sb_out': out['m_g_sb_out'], 'm_g_mla_out': out['m_g_mla_out'], 'm_w_out': out['m_w_out'], 'm_g_ffn': out['m_g_ffn'], 'm_w_up': out['m_w_up'], 'm_conv_w': out['m_conv_w'], 'm_conv_b': out['m_conv_b'], 'm_w_down': out['m_w_down'], 'm_g_final': out['m_g_final'], 'v_g_mix': out['v_g_mix'], 'v_w_in': out['v_w_in'], 'v_g_cq': out['v_g_cq'], 'v_w_uq': out['v_w_uq'], 'v_g_ckv': out['v_g_ckv'], 'v_w_ukv': out['v_w_ukv'], 'v_g_sb_out': out['v_g_sb_out'], 'v_g_mla_out': out['v_g_mla_out'], 'v_w_out': out['v_w_out'], 'v_g_ffn': out['v_g_ffn'], 'v_w_up': out['v_w_up'], 'v_conv_w': out['v_conv_w'], 'v_conv_b': out['v_conv_b'], 'v_w_down': out['v_w_down'], 'v_g_final': out['v_g_final']}


def _loss(weights, diff, rest, loss_target):
    with _jax.named_scope("forward"):
        args = {**rest, TWIN_DIFF_INPUT: diff, **{k: w.astype(_WEIGHT_DTYPES[k]) for k, w in weights.items()}}
        y = _forward(args)
    with _jax.named_scope("loss_head"):
        err = _jnp.square(y.astype(_jnp.float32) - loss_target)
        return 0.5 * _jnp.sum(_jnp.mean(err, axis=-1)) if err.ndim else 0.5 * err


def _adamw(w, g, m, v):
    m = ADAM_B1 * m + (1.0 - ADAM_B1) * g
    v = ADAM_B2 * v + (1.0 - ADAM_B2) * _jnp.square(g)
    m_hat = m / (1.0 - ADAM_B1 ** ADAM_STEP)
    v_hat = v / (1.0 - ADAM_B2 ** ADAM_STEP)
    delta = -ADAM_LR * (m_hat / (_jnp.sqrt(v_hat) + ADAM_EPS) + ADAM_WD * w)
    return delta, m, v


def reference(x, positions, g_mix, w_in, g_cq, w_uq, g_ckv, w_ukv, g_sb_out, g_mla_out, w_out, g_ffn, w_up, conv_w, conv_b, w_down, g_final, loss_target, m_g_mix, m_w_in, m_g_cq, m_w_uq, m_g_ckv, m_w_ukv, m_g_sb_out, m_g_mla_out, m_w_out, m_g_ffn, m_w_up, m_conv_w, m_conv_b, m_w_down, m_g_final, v_g_mix, v_w_in, v_g_cq, v_w_uq, v_g_ckv, v_w_ukv, v_g_sb_out, v_g_mla_out, v_w_out, v_g_ffn, v_w_up, v_conv_w, v_conv_b, v_w_down, v_g_final):
    given = dict(x=x, positions=positions, g_mix=g_mix, w_in=w_in, g_cq=g_cq, w_uq=w_uq, g_ckv=g_ckv, w_ukv=w_ukv, g_sb_out=g_sb_out, g_mla_out=g_mla_out, w_out=w_out, g_ffn=g_ffn, w_up=w_up, conv_w=conv_w, conv_b=conv_b, w_down=w_down, g_final=g_final, loss_target=loss_target, m_g_mix=m_g_mix, m_w_in=m_w_in, m_g_cq=m_g_cq, m_w_uq=m_w_uq, m_g_ckv=m_g_ckv, m_w_ukv=m_w_ukv, m_g_sb_out=m_g_sb_out, m_g_mla_out=m_g_mla_out, m_w_out=m_w_out, m_g_ffn=m_g_ffn, m_w_up=m_w_up, m_conv_w=m_conv_w, m_conv_b=m_conv_b, m_w_down=m_w_down, m_g_final=m_g_final, v_g_mix=v_g_mix, v_w_in=v_w_in, v_g_cq=v_g_cq, v_w_uq=v_w_uq, v_g_ckv=v_g_ckv, v_w_ukv=v_w_ukv, v_g_sb_out=v_g_sb_out, v_g_mla_out=v_g_mla_out, v_w_out=v_w_out, v_g_ffn=v_g_ffn, v_w_up=v_w_up, v_conv_w=v_conv_w, v_conv_b=v_conv_b, v_w_down=v_w_down, v_g_final=v_g_final)
    weights = {n: given[n] for n in TWIN_WEIGHTS}
    shared = {n: given[n] for n in SHARED_INPUTS}
    per_example = {n: given[n] for n in ['x', 'positions']}
    grad_fn = _jax.value_and_grad(_loss, argnums=(0, 1))

    def one_microbatch(ex, loss_target):
        ex = dict(ex)
        diff = ex.pop(TWIN_DIFF_INPUT)
        return grad_fn(weights, diff, {**shared, **ex}, loss_target)

    if N_MICROBATCH == 1:
        loss, (grad_w, grad_x) = one_microbatch(per_example, given["loss_target"])
    else:
        def body(carry, xs):
            loss_sum, grad_sum = carry
            l_k, (gw_k, gx_k) = one_microbatch(xs[0], xs[1])
            with _jax.named_scope("update"):
                return (loss_sum + l_k, _jax.tree.map(_jnp.add, grad_sum, gw_k)), gx_k

        init = (_jnp.zeros((), _jnp.float32), _jax.tree.map(_jnp.zeros_like, weights))
        (loss, grad_w), grad_x = _jax.lax.scan(body, init, (per_example, given["loss_target"]))
    with _jax.named_scope("update"):
        delta_w, new_m, new_v = {}, {}, {}
        for n in TWIN_WEIGHTS:
            delta_w[n], new_m[n], new_v[n] = _adamw(weights[n], grad_w[n], given["m_" + n], given["v_" + n])
    return (loss, grad_x, *[grad_w[n] for n in TWIN_WEIGHTS], *[delta_w[n] for n in TWIN_WEIGHTS],
            *[new_m[n] for n in TWIN_WEIGHTS], *[new_v[n] for n in TWIN_WEIGHTS])
```

```python
import jax
import jax.numpy as jnp
from jax import lax
from jax.experimental import pallas as pl
from jax.experimental.pallas import tpu as pltpu

F32 = jnp.float32
BF16 = jnp.bfloat16

D_MODEL = 1024
SB_HEADS = 8
SB_HEAD_DIM = 64
MLA_HEADS = 8
MLA_NOPE = 64
MLA_ROPE = 32
MLA_V = 64
Q_LORA = 384
KV_LORA = 256
D_FF = 2816
ROPE_BASE = 10000.0
EPS = 1e-6
SB_W = SB_HEADS * SB_HEAD_DIM
MLA_W = MLA_HEADS * MLA_V
MLA_QK = MLA_NOPE + MLA_ROPE
IN_COLS = 3 * SB_W + Q_LORA + KV_LORA + MLA_ROPE

ADAM_LR = 0.001
ADAM_B1 = 0.9
ADAM_B2 = 0.999
ADAM_EPS = 1e-08
ADAM_WD = 0.01
ADAM_STEP = 10

N_DEV = 8
MESH_AXES = ("x", "y", "c")
LANES = 128
V7X_VMEM_LIMIT = 56 * 1024 * 1024
FF_BLK = 256
N_FF_BLK = D_FF // FF_BLK

P_Q, P_K, P_V = 0, SB_W, 2 * SB_W
P_CKV = 3 * SB_W
P_KRT = P_CKV + KV_LORA
P_CQ = P_KRT + LANES
P_COLS = P_CQ + Q_LORA

MESH = pl.DeviceIdType.MESH
ANY = pl.BlockSpec(memory_space=pl.ANY)


def _cparams(sem=None, vmem=V7X_VMEM_LIMIT):
    return pltpu.CompilerParams(dimension_semantics=sem, vmem_limit_bytes=vmem)


def _matmul_nn(a, b, *, tm, tn, out_dtype, name, residual=None):
    M, K = a.shape
    N = b.shape[1]
    in_specs = [pl.BlockSpec((tm, K), lambda i, j: (i, 0)), pl.BlockSpec((K, tn), lambda i, j: (0, j))]
    args = [a, b]
    if residual is not None:
        in_specs.append(pl.BlockSpec((tm, tn), lambda i, j: (i, j)))
        args.append(residual)

    def body(*refs):
        a_ref, b_ref = refs[0], refs[1]
        o_ref = refs[-1]
        acc = jnp.dot(a_ref[...].astype(BF16), b_ref[...], preferred_element_type=F32)
        if residual is not None:
            acc = acc + refs[2][...]
        o_ref[...] = acc.astype(out_dtype)

    return pl.pallas_call(
        body, name=name, grid=(M // tm, N // tn), in_specs=in_specs,
        out_specs=pl.BlockSpec((tm, tn), lambda i, j: (i, j)),
        out_shape=jax.ShapeDtypeStruct((M, N), out_dtype),
        compiler_params=_cparams(("parallel", "parallel")),
    )(*args)


def _matmul_nt(a, b, *, tm, tn, out_dtype, name):
    M, K = a.shape
    N = b.shape[0]

    def body(a_ref, b_ref, o_ref):
        acc = lax.dot_general(a_ref[...].astype(BF16), b_ref[...], (((1,), (1,)), ((), ())),
                              preferred_element_type=F32)
        o_ref[...] = acc.astype(out_dtype)

    return pl.pallas_call(
        body, name=name, grid=(M // tm, N // tn),
        in_specs=[pl.BlockSpec((tm, K), lambda i, j: (i, 0)), pl.BlockSpec((tn, K), lambda i, j: (j, 0))],
        out_specs=pl.BlockSpec((tm, tn), lambda i, j: (i, j)),
        out_shape=jax.ShapeDtypeStruct((M, N), out_dtype),
        compiler_params=_cparams(("parallel", "parallel")),
    )(a, b)


def _matmul_tn(a, b, *, tm, tn, tk, name):
    K, M = a.shape
    N = b.shape[1]

    def body(a_ref, b_ref, o_ref):
        k = pl.program_id(2)
        part = lax.dot_general(a_ref[...].astype(BF16), b_ref[...].astype(BF16), (((0,), (0,)), ((), ())),
                               preferred_element_type=F32)

        @pl.when(k == 0)
        def _():
            o_ref[...] = part

        @pl.when(k > 0)
        def _():
            o_ref[...] += part

    return pl.pallas_call(
        body, name=name, grid=(M // tm, N // tn, K // tk),
        in_specs=[pl.BlockSpec((tk, tm), lambda i, j, k: (k, i)), pl.BlockSpec((tk, tn), lambda i, j, k: (k, j))],
        out_specs=pl.BlockSpec((tm, tn), lambda i, j, k: (i, j)),
        out_shape=jax.ShapeDtypeStruct((M, N), F32),
        compiler_params=_cparams(("parallel", "parallel", "arbitrary")),
    )(a, b)


def _rms(xf, g):
    r = lax.rsqrt(jnp.mean(xf * xf, axis=1, keepdims=True) + EPS)
    return (xf * r) * g


def _rms_grad(dyf, xf, g):
    r = lax.rsqrt(jnp.mean(xf * xf, axis=1, keepdims=True) + EPS)
    xh = xf * r
    dyg = dyf * g
    dx = r * (dyg - xh * jnp.mean(dyg * xh, axis=1, keepdims=True))
    return dx, jnp.sum(dyf * xh, axis=0, keepdims=True)


def _accumulate(ref, part):
    @pl.when(pl.program_id(0) == 0)
    def _():
        ref[...] = part

    @pl.when(pl.program_id(0) > 0)
    def _():
        ref[...] += part


def _rms_fwd(x, g, *, tm, name, col_block=0):
    T = x.shape[0]
    C = g.shape[1]

    def body(x_ref, g_ref, o_ref):
        o_ref[...] = _rms(x_ref[...], g_ref[...]).astype(BF16)

    return pl.pallas_call(
        body, name=name, grid=(T // tm,),
        in_specs=[pl.BlockSpec((tm, C), lambda i: (i, col_block)), pl.BlockSpec((1, C), lambda i: (0, 0))],
        out_specs=pl.BlockSpec((tm, C), lambda i: (i, 0)),
        out_shape=jax.ShapeDtypeStruct((T, C), BF16),
        compiler_params=_cparams(("parallel",)),
    )(x, g)


def _rms_bwd(dy, x, g, *, tm, name, residual=None, col_block=0, out_dtype=F32):
    T = dy.shape[0]
    C = g.shape[1]
    in_specs = [pl.BlockSpec((tm, C), lambda i: (i, 0)), pl.BlockSpec((tm, C), lambda i: (i, col_block)),
                pl.BlockSpec((1, C), lambda i: (0, 0))]
    args = [dy, x, g]
    if residual is not None:
        in_specs.append(pl.BlockSpec((tm, C), lambda i: (i, 0)))
        args.append(residual)

    def body(*refs):
        dy_ref, x_ref, g_ref = refs[:3]
        dx_ref, dg_ref = refs[-2:]
        dx, part = _rms_grad(dy_ref[...].astype(F32), x_ref[...], g_ref[...])
        if residual is not None:
            dx = dx + refs[3][...]
        dx_ref[...] = dx.astype(out_dtype)
        _accumulate(dg_ref, part)

    return pl.pallas_call(
        body, name=name, grid=(T // tm,), in_specs=in_specs,
        out_specs=[pl.BlockSpec((tm, C), lambda i: (i, 0)), pl.BlockSpec((1, C), lambda i: (0, 0))],
        out_shape=[jax.ShapeDtypeStruct((T, C), out_dtype), jax.ShapeDtypeStruct((1, C), F32)],
        compiler_params=_cparams(("arbitrary",)),
    )(*args)


def _rms2_fwd(xa, xb, ga, gb, *, tm, name):
    T, C = xa.shape

    def body(xa_ref, xb_ref, ga_ref, gb_ref, o_ref):
        o_ref[:, :C] = _rms(xa_ref[...], ga_ref[...]).astype(BF16)
        o_ref[:, C:] = _rms(xb_ref[...], gb_ref[...]).astype(BF16)

    row = pl.BlockSpec((tm, C), lambda i: (i, 0))
    gsp = pl.BlockSpec((1, C), lambda i: (0, 0))
    return pl.pallas_call(
        body, name=name, grid=(T // tm,), in_specs=[row, row, gsp, gsp],
        out_specs=pl.BlockSpec((tm, 2 * C), lambda i: (i, 0)),
        out_shape=jax.ShapeDtypeStruct((T, 2 * C), BF16),
        compiler_params=_cparams(("parallel",)),
    )(xa, xb, ga, gb)


def _rms2_bwd(dy, xa, xb, ga, gb, *, tm, name):
    T, C = xa.shape

    def body(dy_ref, xa_ref, xb_ref, ga_ref, gb_ref, dxa_ref, dxb_ref, dga_ref, dgb_ref):
        dxa, pa = _rms_grad(dy_ref[:, :C], xa_ref[...], ga_ref[...])
        dxb, pb = _rms_grad(dy_ref[:, C:], xb_ref[...], gb_ref[...])
        dxa_ref[...] = dxa
        dxb_ref[...] = dxb
        _accumulate(dga_ref, pa)
        _accumulate(dgb_ref, pb)

    row = pl.BlockSpec((tm, C), lambda i: (i, 0))
    gsp = pl.BlockSpec((1, C), lambda i: (0, 0))
    return pl.pallas_call(
        body, name=name, grid=(T // tm,),
        in_specs=[pl.BlockSpec((tm, 2 * C), lambda i: (i, 0)), row, row, gsp, gsp],
        out_specs=[row, row, gsp, gsp],
        out_shape=[jax.ShapeDtypeStruct((T, C), F32), jax.ShapeDtypeStruct((T, C), F32),
                   jax.ShapeDtypeStruct((1, C), F32), jax.ShapeDtypeStruct((1, C), F32)],
        compiler_params=_cparams(("arbitrary",)),
    )(dy, xa, xb, ga, gb)


def _final_loss(x2, g, tgt, *, tm, name):
    T, C = x2.shape

    def body(x_ref, g_ref, t_ref, dx_ref, dg_ref, loss_ref):
        xf = x_ref[...]
        gf = g_ref[...]
        err = _rms(xf, gf) - t_ref[...]
        lpart = 0.5 * jnp.sum(jnp.mean(err * err, axis=1, keepdims=True), axis=0, keepdims=True)
        dx, gpart = _rms_grad(err * (1.0 / C), xf, gf)
        dx_ref[...] = dx
        _accumulate(dg_ref, gpart)
        _accumulate(loss_ref, jnp.broadcast_to(lpart, (1, LANES)))

    return pl.pallas_call(
        body, name=name, grid=(T // tm,),
        in_specs=[pl.BlockSpec((tm, C), lambda i: (i, 0)), pl.BlockSpec((1, C), lambda i: (0, 0)),
                  pl.BlockSpec((tm, C), lambda i: (i, 0))],
        out_specs=[pl.BlockSpec((tm, C), lambda i: (i, 0)), pl.BlockSpec((1, C), lambda i: (0, 0)),
                   pl.BlockSpec((1, LANES), lambda i: (0, 0))],
        out_shape=[jax.ShapeDtypeStruct((T, C), F32), jax.ShapeDtypeStruct((1, C), F32),
                   jax.ShapeDtypeStruct((1, LANES), F32)],
        compiler_params=_cparams(("arbitrary",)),
    )(x2, g, tgt)


ATT_T = 256
NEG_BIG = -1e30


def _lane_iota():
    return lax.broadcasted_iota(jnp.int32, (1, LANES), 1)


def _head_masks():
    first = _lane_iota() < SB_HEAD_DIM
    return first, jnp.logical_not(first)


def _pick(mask, x):
    return jnp.where(mask, x, jnp.zeros_like(x))


def _lane_value(t, lane):
    return jnp.sum(jnp.where(_lane_iota() == lane, t, 0.0), axis=1, keepdims=True)


def _split_hi_lo(x):
    hi = x.astype(BF16)
    lo = (x - hi.astype(F32)).astype(BF16)
    return jnp.concatenate([hi, lo], axis=1)


def _tri(n, kind):
    r = lax.broadcasted_iota(jnp.int32, (n, n), 0)
    c = lax.broadcasted_iota(jnp.int32, (n, n), 1)
    u = {"suffix_excl": r > c, "prefix_incl": r <= c, "prefix_excl": r < c}[kind].astype(BF16)
    return jnp.concatenate([u, u], axis=0)


def _dot_nt(a, b):
    return lax.dot_general(a, b, (((1,), (1,)), ((), ())), preferred_element_type=F32)


def _dot_tn(a, b):
    return lax.dot_general(a, b, (((0,), (0,)), ((), ())), preferred_element_type=F32)


def _dot(a, b):
    return jnp.dot(a, b, preferred_element_type=F32)


def _causal_mask(n, strict):
    r = lax.broadcasted_iota(jnp.int32, (n, n), 0)
    c = lax.broadcasted_iota(jnp.int32, (n, n), 1)
    return (c < r) if strict else (c <= r)


def _sb_logs(qh, kj, vis):
    z = _dot_nt(qh, kj)
    sp = jnp.log(1.0 + jnp.exp(-jnp.abs(z)))
    lb = jnp.minimum(z, 0.0) - sp
    lk = jnp.minimum(-z, 0.0) - sp
    if vis is not None:
        lk = jnp.where(vis, lk, 0.0)
    return lb, lk


def _sb_fwd(p, *, seq, name):
    T = p.shape[0]
    B = T // seq
    TQ = ATT_T
    nq = seq // TQ
    npair = SB_W // LANES

    def body(q_ref, k_ref, v_ref, o_ref, lt_ref, qa_s, qb_s, k_s, va_s, vb_s):
        ma, mb = _head_masks()
        q = q_ref[...] * (SB_HEAD_DIM ** -0.5)
        qa_s[...] = _pick(ma, q).astype(BF16)
        qb_s[...] = _pick(mb, q).astype(BF16)
        k_s[...] = k_ref[...].astype(BF16)
        v = v_ref[...]
        va_s[...] = _pick(ma, v).astype(BF16)
        vb_s[...] = _pick(mb, v).astype(BF16)
        u_suf = _tri(TQ, "suffix_excl")
        vis = _causal_mask(TQ, True)

        def q_block(i, carry):
            q0 = pl.multiple_of(i * TQ, TQ)
            qa = qa_s[pl.ds(q0, TQ), :]
            qb = qb_s[pl.ds(q0, TQ), :]

            def head(qh, kj, vj, r_run, mask):
                lb, lk = _sb_logs(qh, kj, mask)
                a = jnp.exp(lb + _dot(_split_hi_lo(lk), u_suf) + r_run)
                if mask is not None:
                    a = jnp.where(mask, a, 0.0)
                return _dot(a.astype(BF16), vj), r_run + jnp.sum(lk, axis=1, keepdims=True)

            def pair(k0, c, mask):
                ra, rb, acc = c
                kj = k_s[pl.ds(k0, TQ), :]
                oa, ra = head(qa, kj, va_s[pl.ds(k0, TQ), :], ra, mask)
                ob, rb = head(qb, kj, vb_s[pl.ds(k0, TQ), :], rb, mask)
                return ra, rb, acc + oa + ob

            zero = jnp.zeros((TQ, 1), F32)
            c = pair(q0, (zero, zero, jnp.zeros((TQ, LANES), F32)), vis)

            def k_block(jj, c):
                return pair(pl.multiple_of((i - 1 - jj) * TQ, TQ), c, None)

            ra, rb, acc = lax.fori_loop(0, i, k_block, c)
            o_ref[pl.ds(q0, TQ), :] = acc
            lt_ref[pl.ds(q0, TQ), :] = jnp.where(ma, ra, rb)
            return carry

        lax.fori_loop(0, nq, q_block, 0)

    blk = lambda off: pl.BlockSpec((seq, LANES), lambda b, g: (b, off + g))
    out_blk = pl.BlockSpec((seq, LANES), lambda b, g: (b, g))
    return pl.pallas_call(
        body, name=name, grid=(B, npair),
        in_specs=[blk(P_Q // LANES), blk(P_K // LANES), blk(P_V // LANES)],
        out_specs=[out_blk, out_blk],
        out_shape=[jax.ShapeDtypeStruct((T, SB_W), F32), jax.ShapeDtypeStruct((T, SB_W), F32)],
        scratch_shapes=[pltpu.VMEM((seq, LANES), BF16) for _ in range(5)],
        compiler_params=_cparams(("parallel", "parallel")),
    )(p, p, p)


def _sb_bwd(p, ltot, do, *, seq, name):
    T = p.shape[0]
    B = T // seq
    TQ = ATT_T
    nq = seq // TQ
    npair = SB_W // LANES
    scale = SB_HEAD_DIM ** -0.5

    def body(q_ref, k_ref, v_ref, lt_ref, do_ref, dq_ref, dk_ref, dv_ref,
             qa_s, qb_s, k_s, v_s, doa_s, dob_s, dk_s, dv_s):
        ma, mb = _head_masks()
        q = q_ref[...] * scale
        qa_s[...] = _pick(ma, q).astype(BF16)
        qb_s[...] = _pick(mb, q).astype(BF16)
        k_s[...] = k_ref[...].astype(BF16)
        v_s[...] = v_ref[...].astype(BF16)
        dof = do_ref[...]
        doa_s[...] = _pick(ma, dof).astype(BF16)
        dob_s[...] = _pick(mb, dof).astype(BF16)
        dk_s[...] = jnp.zeros_like(dk_s)
        dv_s[...] = jnp.zeros_like(dv_s)
        u_pin = _tri(TQ, "prefix_incl")
        u_pex = _tri(TQ, "prefix_excl")
        vis = _causal_mask(TQ, True)

        def q_block(i, carry):
            q0 = pl.multiple_of(i * TQ, TQ)
            qa = qa_s[pl.ds(q0, TQ), :]
            qb = qb_s[pl.ds(q0, TQ), :]
            doa = doa_s[pl.ds(q0, TQ), :]
            dob = dob_s[pl.ds(q0, TQ), :]
            lt = lt_ref[pl.ds(q0, TQ), :]
            lt_a = _lane_value(lt, 0)
            lt_b = _lane_value(lt, SB_HEAD_DIM)

            def head(qh, doh, lt_h, kj, vj, c_run, g_run, mask):
                lb, lk = _sb_logs(qh, kj, mask)
                tail = (lt_h - c_run) - _dot(_split_hi_lo(lk), u_pin)
                a = jnp.exp(lb + tail)
                if mask is not None:
                    a = jnp.where(mask, a, 0.0)
                g = _dot_nt(doh, vj) * a
                pre = _dot(_split_hi_lo(g), u_pex) + g_run
                dz = g - jnp.exp(lb) * (g + pre)
                if mask is not None:
                    dz = jnp.where(mask, dz, 0.0)
                return (dz.astype(BF16), a.astype(BF16), c_run + jnp.sum(lk, axis=1, keepdims=True),
                        g_run + jnp.sum(g, axis=1, keepdims=True))

            def pair(k0, c, mask):
                ca, cb, ga, gb, acc_a, acc_b = c
                kj = k_s[pl.ds(k0, TQ), :]
                vj = v_s[pl.ds(k0, TQ), :]
                dz_a, a_a, ca, ga = head(qa, doa, lt_a, kj, vj, ca, ga, mask)
                dz_b, a_b, cb, gb = head(qb, dob, lt_b, kj, vj, cb, gb, mask)
                acc_a = acc_a + _dot(dz_a, kj)
                acc_b = acc_b + _dot(dz_b, kj)
                dk_s[pl.ds(k0, TQ), :] += _dot_tn(dz_a, qa) + _dot_tn(dz_b, qb)
                dv_s[pl.ds(k0, TQ), :] += _dot_tn(a_a, doa) + _dot_tn(a_b, dob)
                return ca, cb, ga, gb, acc_a, acc_b

            z1 = jnp.zeros((TQ, 1), F32)
            zl = jnp.zeros((TQ, LANES), F32)

            def k_block(j, c):
                return pair(pl.multiple_of(j * TQ, TQ), c, None)

            c = lax.fori_loop(0, i, k_block, (z1, z1, z1, z1, zl, zl))
            c = pair(q0, c, vis)
            dq_ref[pl.ds(q0, TQ), :] = (jnp.where(ma, c[4], c[5]) * scale).astype(BF16)
            return carry

        lax.fori_loop(0, nq, q_block, 0)
        dk_ref[...] = dk_s[...].astype(BF16)
        dv_ref[...] = dv_s[...].astype(BF16)

    blk = lambda off: pl.BlockSpec((seq, LANES), lambda b, g: (b, off + g))
    out_blk = pl.BlockSpec((seq, LANES), lambda b, g: (b, g))
    return pl.pallas_call(
        body, name=name, grid=(B, npair),
        in_specs=[blk(P_Q // LANES), blk(P_K // LANES), blk(P_V // LANES), out_blk, out_blk],
        out_specs=[out_blk, out_blk, out_blk],
        out_shape=[jax.ShapeDtypeStruct((T, SB_W), BF16) for _ in range(3)],
        scratch_shapes=[pltpu.VMEM((seq, LANES), BF16) for _ in range(6)] + [pltpu.VMEM((seq, LANES), F32) for _ in range(2)],
        compiler_params=_cparams(("parallel", "parallel")),
    )(p, p, p, ltot, do)


def _mla_masks():
    lane = lax.broadcasted_iota(jnp.int32, (1, 2 * LANES), 1)
    ma = (lane < MLA_NOPE) | ((lane >= LANES) & (lane < LANES + MLA_ROPE))
    mb = ((lane >= MLA_NOPE) & (lane < LANES)) | ((lane >= LANES + MLA_ROPE) & (lane < LANES + 2 * MLA_ROPE))
    return ma, mb


def _mla_fwd(qm, kvm, krt, *, seq, name):
    T = qm.shape[0]
    B = T // seq
    TQ = ATT_T
    nq = seq // TQ
    npair = MLA_W // LANES
    scale = MLA_QK ** -0.5

    def body(qn_ref, qr_ref, kn_ref, v_ref, kr_ref, o_ref, lse_ref, qa_s, qb_s, kc_s, va_s, vb_s):
        ha, hb = _head_masks()
        ma, mb = _mla_masks()
        qc = jnp.concatenate([qn_ref[...], qr_ref[...]], axis=1)
        qa_s[...] = _pick(ma, qc)
        qb_s[...] = _pick(mb, qc)
        kc_s[...] = jnp.concatenate([kn_ref[...], kr_ref[...]], axis=1)
        v = v_ref[...]
        va_s[...] = _pick(ha, v)
        vb_s[...] = _pick(hb, v)
        vis = _causal_mask(TQ, False)

        def q_block(i, carry):
            q0 = pl.multiple_of(i * TQ, TQ)
            qa = qa_s[pl.ds(q0, TQ), :]
            qb = qb_s[pl.ds(q0, TQ), :]

            def head(qh, kc, vj, m, l, mask):
                s = _dot_nt(qh, kc) * scale
                if mask is not None:
                    s = jnp.where(mask, s, NEG_BIG)
                m_new = jnp.maximum(m, jnp.max(s, axis=1, keepdims=True))
                alpha = jnp.exp(m - m_new)
                pr = jnp.exp(s - m_new)
                l = alpha * l + jnp.sum(pr, axis=1, keepdims=True)
                return _dot(pr.astype(BF16), vj), alpha, m_new, l

            def pair(k0, c, mask):
                m_a, l_a, m_b, l_b, acc = c
                kc = kc_s[pl.ds(k0, TQ), :]
                oa, al_a, m_a, l_a = head(qa, kc, va_s[pl.ds(k0, TQ), :], m_a, l_a, mask)
                ob, al_b, m_b, l_b = head(qb, kc, vb_s[pl.ds(k0, TQ), :], m_b, l_b, mask)
                acc = acc * jnp.where(ha, al_a, al_b) + oa + ob
                return m_a, l_a, m_b, l_b, acc

            neg = jnp.full((TQ, 1), NEG_BIG, F32)
            z1 = jnp.zeros((TQ, 1), F32)

            def k_block(j, c):
                return pair(pl.multiple_of(j * TQ, TQ), c, None)

            c = lax.fori_loop(0, i, k_block, (neg, z1, neg, z1, jnp.zeros((TQ, LANES), F32)))
            m_a, l_a, m_b, l_b, acc = pair(q0, c, vis)
            o_ref[pl.ds(q0, TQ), :] = acc / jnp.where(ha, l_a, l_b)
            lse_ref[pl.ds(q0, TQ), :] = jnp.where(ha, m_a + jnp.log(l_a), m_b + jnp.log(l_b))
            return carry

        lax.fori_loop(0, nq, q_block, 0)

    blk = lambda off: pl.BlockSpec((seq, LANES), lambda b, g: (b, off + g))
    out_blk = pl.BlockSpec((seq, LANES), lambda b, g: (b, g))
    return pl.pallas_call(
        body, name=name, grid=(B, npair),
        in_specs=[blk(0), blk(npair), blk(0), blk(npair), pl.BlockSpec((seq, LANES), lambda b, g: (b, 0))],
        out_specs=[out_blk, out_blk],
        out_shape=[jax.ShapeDtypeStruct((T, MLA_W), F32), jax.ShapeDtypeStruct((T, MLA_W), F32)],
        scratch_shapes=[pltpu.VMEM((seq, 2 * LANES), BF16) for _ in range(3)] + [pltpu.VMEM((seq, LANES), BF16) for _ in range(2)],
        compiler_params=_cparams(("parallel", "parallel")),
    )(qm, qm, kvm, kvm, krt)


def _mla_bwd(qm, kvm, krt, o, lse, do, *, seq, name):
    T = qm.shape[0]
    B = T // seq
    TQ = ATT_T
    nq = seq // TQ
    npair = MLA_W // LANES
    scale = MLA_QK ** -0.5

    def body(qn_ref, qr_ref, kn_ref, v_ref, kr_ref, o_ref, lse_ref, do_ref,
             dqn_ref, dqr_ref, dkn_ref, dv_ref, dkr_ref,
             qa_s, qb_s, kc_s, doa_s, dob_s, dkc_s, dv_s):
        ha, hb = _head_masks()
        ma, mb = _mla_masks()
        qc = jnp.concatenate([qn_ref[...], qr_ref[...]], axis=1)
        qa_s[...] = _pick(ma, qc)
        qb_s[...] = _pick(mb, qc)
        kc_s[...] = jnp.concatenate([kn_ref[...], kr_ref[...]], axis=1)
        dof = do_ref[...]
        doa_s[...] = _pick(ha, dof).astype(BF16)
        dob_s[...] = _pick(hb, dof).astype(BF16)
        dkc_s[...] = jnp.zeros_like(dkc_s)
        dv_s[...] = jnp.zeros_like(dv_s)
        vis = _causal_mask(TQ, False)

        def q_block(i, carry):
            q0 = pl.multiple_of(i * TQ, TQ)
            qa = qa_s[pl.ds(q0, TQ), :]
            qb = qb_s[pl.ds(q0, TQ), :]
            doa = doa_s[pl.ds(q0, TQ), :]
            dob = dob_s[pl.ds(q0, TQ), :]
            lse_t = lse_ref[pl.ds(q0, TQ), :]
            dd = do_ref[pl.ds(q0, TQ), :] * o_ref[pl.ds(q0, TQ), :]
            d_a = jnp.sum(_pick(ha, dd), axis=1, keepdims=True)
            d_b = jnp.sum(_pick(hb, dd), axis=1, keepdims=True)
            lse_a = _lane_value(lse_t, 0)
            lse_b = _lane_value(lse_t, MLA_V)

            def head(qh, doh, lse_h, d_h, kc, vj, mask):
                s = _dot_nt(qh, kc) * scale
                pr = jnp.exp(s - lse_h)
                if mask is not None:
                    pr = jnp.where(mask, pr, 0.0)
                ds = pr * (_dot_nt(doh, vj) - d_h) * scale
                return ds.astype(BF16), pr.astype(BF16)

            def pair(k0, c, mask):
                acc_a, acc_b = c
                kc = kc_s[pl.ds(k0, TQ), :]
                vj = v_ref[pl.ds(k0, TQ), :]
                ds_a, p_a = head(qa, doa, lse_a, d_a, kc, vj, mask)
                ds_b, p_b = head(qb, dob, lse_b, d_b, kc, vj, mask)
                acc_a = acc_a + _dot(ds_a, kc)
                acc_b = acc_b + _dot(ds_b, kc)
                dkc_s[pl.ds(k0, TQ), :] += _dot_tn(ds_a, qa) + _dot_tn(ds_b, qb)
                dv_s[pl.ds(k0, TQ), :] += _dot_tn(p_a, doa) + _dot_tn(p_b, dob)
                return acc_a, acc_b

            zc = jnp.zeros((TQ, 2 * LANES), F32)

            def k_block(j, c):
                return pair(pl.multiple_of(j * TQ, TQ), c, None)

            c = lax.fori_loop(0, i, k_block, (zc, zc))
            acc_a, acc_b = pair(q0, c, vis)
            dq = _pick(ma, acc_a) + _pick(mb, acc_b)
            dqn_ref[pl.ds(q0, TQ), :] = dq[:, :LANES].astype(BF16)
            dqr_ref[pl.ds(q0, TQ), :] = dq[:, LANES:]
            return carry

        lax.fori_loop(0, nq, q_block, 0)
        dkn_ref[...] = dkc_s[:, :LANES].astype(BF16)
        dv_ref[...] = dv_s[...].astype(BF16)
        g = pl.program_id(1)

        @pl.when(g == 0)
        def _():
            dkr_ref[...] = dkc_s[:, LANES:]

        @pl.when(g > 0)
        def _():
            dkr_ref[...] += dkc_s[:, LANES:]

    blk = lambda off: pl.BlockSpec((seq, LANES), lambda b, g: (b, off + g))
    out_blk = pl.BlockSpec((seq, LANES), lambda b, g: (b, g))
    one_blk = pl.BlockSpec((seq, LANES), lambda b, g: (b, 0))
    return pl.pallas_call(
        body, name=name, grid=(B, npair),
        in_specs=[blk(0), blk(npair), blk(0), blk(npair), one_blk, out_blk, out_blk, out_blk],
        out_specs=[out_blk, out_blk, out_blk, out_blk, one_blk],
        out_shape=[jax.ShapeDtypeStruct((T, MLA_W), BF16), jax.ShapeDtypeStruct((T, MLA_W), F32),
                   jax.ShapeDtypeStruct((T, MLA_W), BF16), jax.ShapeDtypeStruct((T, MLA_W), BF16),
                   jax.ShapeDtypeStruct((T, LANES), F32)],
        scratch_shapes=[pltpu.VMEM((seq, 2 * LANES), BF16) for _ in range(3)]
        + [pltpu.VMEM((seq, LANES), BF16) for _ in range(2)]
        + [pltpu.VMEM((seq, 2 * LANES), F32), pltpu.VMEM((seq, LANES), F32)],
        compiler_params=_cparams(("parallel", "arbitrary")),
    )(qm, qm, kvm, kvm, krt, o, lse, do)


def _rope_tables(pos_ref, invf_ref):
    ang = pos_ref[...].astype(F32) * invf_ref[...]
    first = (_lane_iota() % MLA_ROPE) < (MLA_ROPE // 2)
    return jnp.cos(ang), jnp.sin(ang), first


def _rope_apply(x, cos, sin, first):
    rot = jnp.where(first, -pltpu.roll(x, LANES - MLA_ROPE // 2, 1), pltpu.roll(x, MLA_ROPE // 2, 1))
    return x * cos + rot * sin


def _rope_apply_t(dy, cos, sin, first):
    dys = dy * sin
    rot_t = jnp.where(first, pltpu.roll(dys, LANES - MLA_ROPE // 2, 1), -pltpu.roll(dys, MLA_ROPE // 2, 1))
    return dy * cos + rot_t


def _rope_fwd(qfull, p, pos, invf, *, tm, name):
    T = qfull.shape[0]
    ntile = MLA_W // LANES

    def body(q_ref, kr_ref, pos_ref, invf_ref, qm_ref, krt_ref):
        cos, sin, first = _rope_tables(pos_ref, invf_ref)
        qm_ref[:, :MLA_W] = q_ref[:, :MLA_W].astype(BF16)
        for t in range(ntile):
            sl = slice(MLA_W + t * LANES, MLA_W + (t + 1) * LANES)
            qm_ref[:, sl] = _rope_apply(q_ref[:, sl], cos, sin, first).astype(BF16)
        krt_ref[...] = _rope_apply(kr_ref[...], cos, sin, first).astype(BF16)

    return pl.pallas_call(
        body, name=name, grid=(T // tm,),
        in_specs=[pl.BlockSpec((tm, 2 * MLA_W), lambda i: (i, 0)), pl.BlockSpec((tm, LANES), lambda i: (i, P_KRT // LANES)),
                  pl.BlockSpec((tm, 1), lambda i: (i, 0)), pl.BlockSpec((1, LANES), lambda i: (0, 0))],
        out_specs=[pl.BlockSpec((tm, 2 * MLA_W), lambda i: (i, 0)), pl.BlockSpec((tm, LANES), lambda i: (i, 0))],
        out_shape=[jax.ShapeDtypeStruct((T, 2 * MLA_W), BF16), jax.ShapeDtypeStruct((T, LANES), BF16)],
        compiler_params=_cparams(("parallel",)),
    )(qfull, p, pos, invf)


def _rope_bwd(dqr, dkr, pos, invf, *, tm, name):
    T = dqr.shape[0]
    ntile = MLA_W // LANES

    def body(dq_ref, dk_ref, pos_ref, invf_ref, oq_ref, ok_ref):
        cos, sin, first = _rope_tables(pos_ref, invf_ref)
        for t in range(ntile):
            sl = slice(t * LANES, (t + 1) * LANES)
            oq_ref[:, sl] = _rope_apply_t(dq_ref[:, sl], cos, sin, first).astype(BF16)
        ok_ref[...] = _rope_apply_t(dk_ref[...], cos, sin, first).astype(BF16)

    return pl.pallas_call(
        body, name=name, grid=(T // tm,),
        in_specs=[pl.BlockSpec((tm, MLA_W), lambda i: (i, 0)), pl.BlockSpec((tm, LANES), lambda i: (i, 0)),
                  pl.BlockSpec((tm, 1), lambda i: (i, 0)), pl.BlockSpec((1, LANES), lambda i: (0, 0))],
        out_specs=[pl.BlockSpec((tm, MLA_W), lambda i: (i, 0)), pl.BlockSpec((tm, LANES), lambda i: (i, 0))],
        out_shape=[jax.ShapeDtypeStruct((T, MLA_W), BF16), jax.ShapeDtypeStruct((T, LANES), BF16)],
        compiler_params=_cparams(("parallel",)),
    )(dqr, dkr, pos, invf)


CONV_ROWS = 256
HALO = 8


def _conv_taps(w_ref):
    return w_ref[0:1, :], w_ref[1:2, :], w_ref[2:3, :]


def _conv_rows(cur, prev, w, bias):
    ext = jnp.concatenate([prev, cur], axis=0)
    u1 = pltpu.roll(ext, 1, 0)[HALO:]
    u2 = pltpu.roll(ext, 2, 0)[HALO:]
    return w[2] * cur + w[1] * u1 + w[0] * u2 + bias, u1, u2


def _conv_fwd(u, w, bias, *, seq, name):
    T = u.shape[0]
    B = T // seq
    W2 = 2 * FF_BLK

    def body(u_ref, w_ref, b_ref, a_ref):
        wv = _conv_taps(w_ref)
        bv = b_ref[...]
        for c in range(seq // CONV_ROWS):
            r0 = c * CONV_ROWS
            cur = u_ref[r0:r0 + CONV_ROWS, :]
            prev = u_ref[r0 - HALO:r0, :] if c > 0 else jnp.zeros((HALO, W2), F32)
            y, _, _ = _conv_rows(cur, prev, wv, bv)
            gc = y[:, :FF_BLK]
            a_ref[r0:r0 + CONV_ROWS, :] = (gc * (1.0 / (1.0 + jnp.exp(-gc))) * y[:, FF_BLK:]).astype(BF16)

    return pl.pallas_call(
        body, name=name, grid=(B, N_FF_BLK),
        in_specs=[pl.BlockSpec((seq, W2), lambda b, j: (b, j)), pl.BlockSpec((3, W2), lambda b, j: (0, j)),
                  pl.BlockSpec((1, W2), lambda b, j: (0, j))],
        out_specs=pl.BlockSpec((seq, FF_BLK), lambda b, j: (b, j)),
        out_shape=jax.ShapeDtypeStruct((T, D_FF), BF16),
        compiler_params=_cparams(("parallel", "parallel")),
    )(u, w, bias)


def _conv_bwd(u, da, w, bias, *, seq, name):
    T = u.shape[0]
    B = T // seq
    W2 = 2 * FF_BLK
    nchunk = seq // CONV_ROWS

    def body(u_ref, da_ref, w_ref, b_ref, du_ref, dw_ref, db_ref, duc_s):
        wv = _conv_taps(w_ref)
        bv = b_ref[...]
        zrow = jnp.zeros((1, W2), F32)
        dw0, dw1, dw2, dbs = zrow, zrow, zrow, zrow
        for c in range(nchunk):
            r0 = c * CONV_ROWS
            cur = u_ref[r0:r0 + CONV_ROWS, :]
            prev = u_ref[r0 - HALO:r0, :] if c > 0 else jnp.zeros((HALO, W2), F32)
            y, u1, u2 = _conv_rows(cur, prev, wv, bv)
            gc = y[:, :FF_BLK]
            vc = y[:, FF_BLK:]
            sg = 1.0 / (1.0 + jnp.exp(-gc))
            dav = da_ref[r0:r0 + CONV_ROWS, :]
            duc = jnp.concatenate([dav * vc * (sg * (1.0 + gc * (1.0 - sg))), dav * (gc * sg)], axis=1)
            duc_s[r0:r0 + CONV_ROWS, :] = duc
            dw0 = dw0 + jnp.sum(duc * u2, axis=0, keepdims=True)
            dw1 = dw1 + jnp.sum(duc * u1, axis=0, keepdims=True)
            dw2 = dw2 + jnp.sum(duc * cur, axis=0, keepdims=True)
            dbs = dbs + jnp.sum(duc, axis=0, keepdims=True)
        duc_s[seq:seq + HALO, :] = jnp.zeros((HALO, W2), F32)
        n_ext = CONV_ROWS + HALO
        for c in range(nchunk):
            r0 = c * CONV_ROWS
            ext = duc_s[r0:r0 + n_ext, :]
            s1 = pltpu.roll(ext, n_ext - 1, 0)[:CONV_ROWS]
            s2 = pltpu.roll(ext, n_ext - 2, 0)[:CONV_ROWS]
            du_ref[r0:r0 + CONV_ROWS, :] = (wv[2] * ext[:CONV_ROWS] + wv[1] * s1 + wv[0] * s2).astype(BF16)

        first = pl.program_id(1) == 0

        @pl.when(first)
        def _():
            dw_ref[0:1, :] = dw0
            dw_ref[1:2, :] = dw1
            dw_ref[2:3, :] = dw2
            db_ref[...] = dbs

        @pl.when(jnp.logical_not(first))
        def _():
            dw_ref[0:1, :] += dw0
            dw_ref[1:2, :] += dw1
            dw_ref[2:3, :] += dw2
            db_ref[...] += dbs

    return pl.pallas_call(
        body, name=name, grid=(N_FF_BLK, B),
        in_specs=[pl.BlockSpec((seq, W2), lambda j, b: (b, j)), pl.BlockSpec((seq, FF_BLK), lambda j, b: (b, j)),
                  pl.BlockSpec((3, W2), lambda j, b: (0, j)), pl.BlockSpec((1, W2), lambda j, b: (0, j))],
        out_specs=[pl.BlockSpec((seq, W2), lambda j, b: (b, j)), pl.BlockSpec((3, W2), lambda j, b: (0, j)),
                   pl.BlockSpec((1, W2), lambda j, b: (0, j))],
        out_shape=[jax.ShapeDtypeStruct((T, 2 * D_FF), BF16), jax.ShapeDtypeStruct((3, 2 * D_FF), F32),
                   jax.ShapeDtypeStruct((1, 2 * D_FF), F32)],
        scratch_shapes=[pltpu.VMEM((seq + HALO, W2), F32)],
        compiler_params=_cparams(("parallel", "arbitrary")),
    )(u, da, w, bias)


def _place():
    return lax.axis_index("x"), lax.axis_index("y"), lax.axis_index("c")


def _other_chips(x, y):
    return [(1 - x, y), (x, 1 - y), (1 - x, 1 - y)]


def _all_gather(v, *, name):
    R, C = v.shape

    def body(v_ref, out_ref, send_sems, recv_sems, local_sem):
        x, y, c = _place()
        me, sibling = (x, y, c), (x, y, 1 - c)
        chips = _other_chips(x, y)

        def slab(px, py, pc):
            return out_ref.at[4 * px + 2 * py + pc]

        def copy(k, block, to, src=None):
            return pltpu.make_async_remote_copy(
                src_ref=slab(*block) if src is None else src, dst_ref=slab(*block),
                send_sem=send_sems.at[k], recv_sem=recv_sems.at[k], device_id=to, device_id_type=MESH)

        mine = pltpu.make_async_copy(v_ref, slab(*me), local_sem)
        mine.start()
        first = [copy(0, me, sibling, src=v_ref)]
        first += [copy(1 + j, me, (*chip, c), src=v_ref) for j, chip in enumerate(chips)]
        for cp in first:
            cp.start()
        passed = [copy(4 + j, (*chip, c), sibling) for j, chip in enumerate(chips)]
        for j, chip in enumerate(chips):
            copy(1 + j, (*chip, c), me).wait_recv()
            passed[j].start()
        copy(0, sibling, me).wait_recv()
        for j, chip in enumerate(chips):
            copy(4 + j, (*chip, 1 - c), me).wait_recv()
        for cp in first + passed:
            cp.wait_send()
        mine.wait()

    return pl.pallas_call(
        body, name=name, in_specs=[ANY], out_specs=ANY,
        out_shape=jax.ShapeDtypeStruct((N_DEV, R, C), v.dtype),
        scratch_shapes=[pltpu.SemaphoreType.DMA((7,)), pltpu.SemaphoreType.DMA((7,)), pltpu.SemaphoreType.DMA],
    )(v)


def _rs_sibling(g8, *, name):
    _, R, C = g8.shape

    def body(g_ref, out_ref, send_sems, recv_sems):
        x, y, c = _place()
        copies = [
            pltpu.make_async_remote_copy(
                src_ref=g_ref.at[2 * k + 1 - c], dst_ref=out_ref.at[k],
                send_sem=send_sems.at[k], recv_sem=recv_sems.at[k], device_id=(x, y, 1 - c), device_id_type=MESH)
            for k in range(4)]
        for cp in copies:
            cp.start()
        for cp in copies:
            cp.wait()

    return pl.pallas_call(
        body, name=name, in_specs=[ANY], out_specs=ANY,
        out_shape=jax.ShapeDtypeStruct((4, R, C), g8.dtype),
        scratch_shapes=[pltpu.SemaphoreType.DMA((4,)), pltpu.SemaphoreType.DMA((4,))],
    )(g8)


def _rs_chip_sum(g8, from_sibling, c_idx, *, tr, name):
    _, R, C = g8.shape

    def body(c_ref, a_ref, b_ref, f_ref, h_ref):
        s = a_ref[...] + b_ref[...]
        f_ref[...] = s
        h_ref[...] = s.astype(BF16)

    blk = pl.BlockSpec((None, tr, C), lambda k, r, c_ref: (k, r, 0))
    return pl.pallas_call(
        body, name=name,
        grid_spec=pltpu.PrefetchScalarGridSpec(
            num_scalar_prefetch=1, grid=(4, R // tr),
            in_specs=[pl.BlockSpec((None, tr, C), lambda k, r, c_ref: (2 * k + c_ref[0], r, 0)), blk],
            out_specs=[blk, blk]),
        out_shape=[jax.ShapeDtypeStruct((4, R, C), F32), jax.ShapeDtypeStruct((4, R, C), BF16)],
        compiler_params=_cparams(("parallel", "parallel")),
    )(c_idx, g8, from_sibling)


def _rs_chips(h4, *, name):
    _, R, C = h4.shape

    def body(h_ref, out_ref, send_sems, recv_sems):
        x, y, c = _place()
        copies = [
            pltpu.make_async_remote_copy(
                src_ref=h_ref.at[2 * cx + cy], dst_ref=out_ref.at[j],
                send_sem=send_sems.at[j], recv_sem=recv_sems.at[j], device_id=(cx, cy, c), device_id_type=MESH)
            for j, (cx, cy) in enumerate(_other_chips(x, y))]
        for cp in copies:
            cp.start()
        for cp in copies:
            cp.wait()

    return pl.pallas_call(
        body, name=name, in_specs=[ANY], out_specs=ANY,
        out_shape=jax.ShapeDtypeStruct((3, R, C), h4.dtype),
        scratch_shapes=[pltpu.SemaphoreType.DMA((3,)), pltpu.SemaphoreType.DMA((3,))],
    )(h4)


def _rs_final(f4, r3, chip_idx, *, tr, name):
    _, R, C = f4.shape

    def body(k_ref, f_ref, r_ref, o_ref):
        o_ref[...] = ((f_ref[...] + r_ref[0].astype(F32)) + r_ref[1].astype(F32)) + r_ref[2].astype(F32)

    return pl.pallas_call(
        body, name=name,
        grid_spec=pltpu.PrefetchScalarGridSpec(
            num_scalar_prefetch=1, grid=(R // tr,),
            in_specs=[pl.BlockSpec((None, tr, C), lambda r, k_ref: (k_ref[0], r, 0)),
                      pl.BlockSpec((3, tr, C), lambda r, k_ref: (0, r, 0))],
            out_specs=pl.BlockSpec((tr, C), lambda r, k_ref: (r, 0))),
        out_shape=jax.ShapeDtypeStruct((R, C), F32),
        compiler_params=_cparams(("parallel",)),
    )(chip_idx, f4, r3)


def _sum8(g, *, name):
    _, R, C = g.shape

    def body(g_ref, o_ref):
        acc = g_ref[0]
        for k in range(1, N_DEV):
            acc = acc + g_ref[k]
        o_ref[...] = acc

    return pl.pallas_call(
        body, name=name, out_shape=jax.ShapeDtypeStruct((R, C), F32),
    )(g)


def _adamw(w, g, m, v, *, name):
    R, C = w.shape
    tr = R if R <= 512 else 256
    c1 = 1.0 / (1.0 - ADAM_B1 ** ADAM_STEP)
    c2 = 1.0 / (1.0 - ADAM_B2 ** ADAM_STEP)

    def body(w_ref, g_ref, m_ref, v_ref, d_ref, mo_ref, vo_ref):
        gf = g_ref[...]
        mn = ADAM_B1 * m_ref[...] + (1.0 - ADAM_B1) * gf
        vn = ADAM_B2 * v_ref[...] + (1.0 - ADAM_B2) * (gf * gf)
        d_ref[...] = -ADAM_LR * ((mn * c1) / (jnp.sqrt(vn * c2) + ADAM_EPS) + ADAM_WD * w_ref[...])
        mo_ref[...] = mn
        vo_ref[...] = vn

    blk = pl.BlockSpec((tr, C), lambda i: (i, 0))
    shp = jax.ShapeDtypeStruct((R, C), F32)
    return pl.pallas_call(
        body, name=name, grid=(R // tr,), in_specs=[blk] * 4, out_specs=[blk] * 3,
        out_shape=[shp, shp, shp], compiler_params=_cparams(("parallel",)),
    )(w, g, m, v)


PACK_W = 1024
_SLAB = (("w_in", D_MODEL * (IN_COLS // N_DEV) // PACK_W), ("w_uq", Q_LORA * (MLA_HEADS * MLA_QK // N_DEV) // PACK_W),
         ("w_ukv", KV_LORA * (MLA_HEADS * (MLA_NOPE + MLA_V) // N_DEV) // PACK_W), ("w_out", D_MODEL // N_DEV),
         ("w_up", D_MODEL * (2 * D_FF // N_DEV) // PACK_W), ("w_down", D_FF // N_DEV))
ROW_ALIGN = 16


def _aligned(rows):
    return -(-rows // ROW_ALIGN) * ROW_ALIGN


_SLAB_OFF = {}
_off = 0
for _n, _r in _SLAB:
    _SLAB_OFF[_n] = (_off, _r)
    _off += _aligned(_r)
CONV_SHARD = 3 * (2 * D_FF // N_DEV)
CONV_ROWS_BF16 = 5
CONV_ROWS_F32 = 3
CONV_OFF = _off
SLAB_ROWS = CONV_OFF + ROW_ALIGN
RS_TILE = SLAB_ROWS // 7
assert RS_TILE * 7 == SLAB_ROWS and RS_TILE % ROW_ALIGN == 0


def _pad_rows(a, rows):
    pad = [(0, 0)] * a.ndim
    pad[-2] = (0, rows - a.shape[-2])
    return jnp.pad(a, pad)

SHARD_SHAPES = {"w_in": (D_MODEL, IN_COLS // N_DEV), "w_uq": (Q_LORA, MLA_HEADS * MLA_QK // N_DEV),
                "w_ukv": (KV_LORA, MLA_HEADS * (MLA_NOPE + MLA_V) // N_DEV), "w_out": (D_MODEL // N_DEV, D_MODEL),
                "w_up": (D_MODEL, 2 * D_FF // N_DEV), "w_down": (D_FF // N_DEV, D_MODEL)}
COL_SHARDED = ("w_in", "w_uq", "w_ukv", "w_up")


def _pack_slab(shards, conv_rows, dtype):
    parts = [_pad_rows(shards[n].astype(dtype).reshape(-1, PACK_W), _aligned(r)) for n, r in _SLAB]
    parts.append(_pad_rows(conv_rows, ROW_ALIGN))
    return jnp.concatenate(parts, axis=0)


def _unstack(g8, name):
    off, rows = _SLAB_OFF[name]
    r, c = SHARD_SHAPES[name]
    s = g8[:, off:off + rows].reshape(N_DEV, r, c)
    if name in COL_SHARDED:
        return s.transpose(1, 0, 2).reshape(r, N_DEV * c)
    return s.reshape(N_DEV * r, c)


def _stack(full, name):
    r, c = SHARD_SHAPES[name]
    if name in COL_SHARDED:
        s = full.reshape(r, N_DEV, c).transpose(1, 0, 2)
    else:
        s = full.reshape(N_DEV, r, c)
    rows = _SLAB_OFF[name][1]
    return _pad_rows(s.reshape(N_DEV, rows, PACK_W), _aligned(rows))


def _ff_interleave(a):
    lead = a.shape[:-1]
    return a.reshape(*lead, 2, N_FF_BLK, FF_BLK).swapaxes(-3, -2).reshape(*lead, 2 * D_FF)


def _ff_deinterleave(a):
    lead = a.shape[:-1]
    return a.reshape(*lead, N_FF_BLK, 2, FF_BLK).swapaxes(-3, -2).reshape(*lead, 2 * D_FF)


def _w_in_layout(w):
    cq = w[:, 3 * SB_W:3 * SB_W + Q_LORA]
    ckv = w[:, 3 * SB_W + Q_LORA:3 * SB_W + Q_LORA + KV_LORA]
    kr = w[:, IN_COLS - MLA_ROPE:]
    pad = jnp.zeros((w.shape[0], LANES - 2 * MLA_ROPE), w.dtype)
    return jnp.concatenate([w[:, :3 * SB_W], ckv, kr, kr, pad, cq], axis=1)


def _w_in_layout_t(dw):
    kr = dw[:, P_KRT:P_KRT + MLA_ROPE] + dw[:, P_KRT + MLA_ROPE:P_KRT + 2 * MLA_ROPE]
    return jnp.concatenate([dw[:, :3 * SB_W], dw[:, P_CQ:P_CQ + Q_LORA], dw[:, P_CKV:P_CKV + KV_LORA], kr], axis=1)


def _w_uq_layout(w):
    h = w.reshape(Q_LORA, MLA_HEADS, MLA_QK)
    qn = h[:, :, :MLA_NOPE].reshape(Q_LORA, MLA_W)
    qr = h[:, :, MLA_NOPE:].reshape(Q_LORA, MLA_HEADS // 2, 2 * MLA_ROPE)
    qr = jnp.pad(qr, ((0, 0), (0, 0), (0, LANES - 2 * MLA_ROPE))).reshape(Q_LORA, MLA_W)
    return jnp.concatenate([qn, qr], axis=1)


def _w_uq_layout_t(dw):
    qn = dw[:, :MLA_W].reshape(Q_LORA, MLA_HEADS, MLA_NOPE)
    qr = dw[:, MLA_W:].reshape(Q_LORA, MLA_HEADS // 2, LANES)[:, :, :2 * MLA_ROPE].reshape(Q_LORA, MLA_HEADS, MLA_ROPE)
    return jnp.concatenate([qn, qr], axis=2).reshape(Q_LORA, MLA_HEADS * MLA_QK)


def _w_ukv_layout(w):
    h = w.reshape(KV_LORA, MLA_HEADS, MLA_NOPE + MLA_V)
    return jnp.concatenate([h[:, :, :MLA_NOPE].reshape(KV_LORA, MLA_W), h[:, :, MLA_NOPE:].reshape(KV_LORA, MLA_W)], axis=1)


def _w_ukv_layout_t(dw):
    kn = dw[:, :MLA_W].reshape(KV_LORA, MLA_HEADS, MLA_NOPE)
    vv = dw[:, MLA_W:].reshape(KV_LORA, MLA_HEADS, MLA_V)
    return jnp.concatenate([kn, vv], axis=2).reshape(KV_LORA, MLA_HEADS * (MLA_NOPE + MLA_V))


SMALL = (("g_mix", D_MODEL), ("g_cq", Q_LORA), ("g_ckv", KV_LORA), ("g_sb_out", SB_W), ("g_mla_out", MLA_W),
         ("g_ffn", D_MODEL), ("conv_b", 2 * D_FF), ("g_final", D_MODEL))
SMALL_ROWS = 88


def _pack_small(d):
    flat = jnp.concatenate([d[n].reshape(-1) for n, _ in SMALL])
    flat = jnp.pad(flat, (0, SMALL_ROWS * LANES - flat.shape[0]))
    return flat.reshape(SMALL_ROWS, LANES)


def _unpack_small(a):
    flat = a.reshape(-1)
    out, off = {}, 0
    for n, size in SMALL:
        out[n] = flat[off:off + size]
        off += size
    return out


def kernel(x, positions, g_mix, w_in, g_cq, w_uq, g_ckv, w_ukv, g_sb_out, g_mla_out, w_out, g_ffn, w_up, conv_w, conv_b, w_down, g_final, loss_target, m_g_mix, m_w_in, m_g_cq, m_w_uq, m_g_ckv, m_w_ukv, m_g_sb_out, m_g_mla_out, m_w_out, m_g_ffn, m_w_up, m_conv_w, m_conv_b, m_w_down, m_g_final, v_g_mix, v_w_in, v_g_cq, v_w_uq, v_g_ckv, v_w_ukv, v_g_sb_out, v_g_mla_out, v_w_out, v_g_ffn, v_w_up, v_conv_w, v_conv_b, v_w_down, v_g_final):
    B, S, D = x.shape
    T = B * S
    xf = x.reshape(T, D)
    tgt = loss_target.reshape(T, D)
    pos = positions.reshape(T, 1)
    half = MLA_ROPE // 2
    inv_freq = 1.0 / (ROPE_BASE ** (jnp.arange(half, dtype=F32) * (2.0 / MLA_ROPE)))
    invf = jnp.tile(inv_freq, LANES // half).reshape(1, LANES)
    c_idx = lax.axis_index("c").astype(jnp.int32).reshape(1)
    chip_idx = (2 * lax.axis_index("x") + lax.axis_index("y")).astype(jnp.int32).reshape(1)

    shards = {"w_in": w_in[0], "w_uq": w_uq[0], "w_ukv": w_ukv[0], "w_out": w_out[0], "w_up": w_up[0], "w_down": w_down[0]}
    conv_bits = lax.bitcast_convert_type(conv_w[0], BF16).reshape(-1)
    conv_bits = jnp.pad(conv_bits, (0, CONV_ROWS_BF16 * PACK_W - conv_bits.shape[0])).reshape(CONV_ROWS_BF16, PACK_W)
    gathered = _all_gather(_pack_slab(shards, conv_bits, BF16), name="ag_weights")
    full = {n: _unstack(gathered, n) for n, _ in _SLAB}
    cw = gathered[:, CONV_OFF:CONV_OFF + CONV_ROWS_BF16].reshape(N_DEV, -1)[:, :2 * CONV_SHARD]
    cw = lax.bitcast_convert_type(cw.reshape(N_DEV, 3, 2 * D_FF // N_DEV, 2), F32)
    conv_w_full = cw.transpose(1, 0, 2).reshape(3, 2 * D_FF)

    wi = _w_in_layout(full["w_in"])
    wuq = _w_uq_layout(full["w_uq"])
    wukv = _w_ukv_layout(full["w_ukv"])
    wo = full["w_out"]
    wup = _ff_interleave(full["w_up"])
    wdn = full["w_down"]
    cwi = _ff_interleave(conv_w_full)
    cbi = _ff_interleave(conv_b)

    h = _rms_fwd(xf, g_mix, tm=512, name="rms_mix")
    p = _matmul_nn(h, wi, tm=512, tn=768, out_dtype=F32, name="proj_in")
    o_sb, ltot = _sb_fwd(p, seq=S, name="sb_fwd")
    cq = _rms_fwd(p, g_cq, tm=512, name="rms_cq", col_block=P_CQ // Q_LORA)
    ckv = _rms_fwd(p, g_ckv, tm=512, name="rms_ckv", col_block=P_CKV // KV_LORA)
    qfull = _matmul_nn(cq, wuq, tm=512, tn=1024, out_dtype=F32, name="proj_uq")
    kvm = _matmul_nn(ckv, wukv, tm=512, tn=1024, out_dtype=BF16, name="proj_ukv")
    qm, krt = _rope_fwd(qfull, p, pos, invf, tm=512, name="rope_fwd")
    o_mla, lse = _mla_fwd(qm, kvm, krt, seq=S, name="mla_fwd")
    ocat = _rms2_fwd(o_sb, o_mla, g_sb_out, g_mla_out, tm=512, name="rms_heads")
    x1 = _matmul_nn(ocat, wo, tm=512, tn=1024, out_dtype=F32, name="proj_out", residual=xf)
    hf = _rms_fwd(x1, g_ffn, tm=512, name="rms_ffn")
    u = _matmul_nn(hf, wup, tm=512, tn=512, out_dtype=F32, name="ffn_up")
    a = _conv_fwd(u, cwi, cbi, seq=S, name="conv_fwd")
    x2 = _matmul_nn(a, wdn, tm=512, tn=1024, out_dtype=F32, name="ffn_down", residual=x1)
    dx2, dg_final, loss_row = _final_loss(x2, g_final.reshape(1, D), tgt, tm=512, name="final_loss")

    da = _matmul_nt(dx2, wdn, tm=512, tn=D_FF // 2, out_dtype=F32, name="d_ffn_down")
    dw_down = _matmul_tn(a, dx2, tm=D_FF // 2, tn=1024, tk=512, name="dw_down")
    du, dcw, dcb = _conv_bwd(u, da, cwi, cbi, seq=S, name="conv_bwd")
    dhf = _matmul_nt(du, wup, tm=512, tn=512, out_dtype=F32, name="d_ffn_up")
    dw_up = _matmul_tn(hf, du, tm=1024, tn=512, tk=512, name="dw_up")
    dx1, dg_ffn = _rms_bwd(dhf, x1, g_ffn, tm=512, name="rms_ffn_bwd", residual=dx2)
    docat = _matmul_nt(dx1, wo, tm=512, tn=1024, out_dtype=F32, name="d_proj_out")
    dw_out = _matmul_tn(ocat, dx1, tm=1024, tn=1024, tk=512, name="dw_out")
    do_sb, do_mla, dg_sb, dg_mla = _rms2_bwd(docat, o_sb, o_mla, g_sb_out, g_mla_out, tm=512, name="rms_heads_bwd")
    dq_sb, dk_sb, dv_sb = _sb_bwd(p, ltot, do_sb, seq=S, name="sb_bwd")
    dqn, dqr, dkn, dvm, dkr = _mla_bwd(qm, kvm, krt, o_mla, lse, do_mla, seq=S, name="mla_bwd")
    dqr_u, dkr_u = _rope_bwd(dqr, dkr, pos, invf, tm=512, name="rope_bwd")
    dqm = jnp.concatenate([dqn, dqr_u], axis=1)
    dkvm = jnp.concatenate([dkn, dvm], axis=1)
    dcq_n = _matmul_nt(dqm, wuq, tm=512, tn=Q_LORA, out_dtype=F32, name="d_proj_uq")
    dw_uq = _matmul_tn(cq, dqm, tm=Q_LORA, tn=1024, tk=512, name="dw_uq")
    dckv_n = _matmul_nt(dkvm, wukv, tm=512, tn=KV_LORA, out_dtype=F32, name="d_proj_ukv")
    dw_ukv = _matmul_tn(ckv, dkvm, tm=KV_LORA, tn=1024, tk=512, name="dw_ukv")
    dcq, dg_cq = _rms_bwd(dcq_n, p, g_cq, tm=512, name="rms_cq_bwd", col_block=P_CQ // Q_LORA, out_dtype=BF16)
    dckv, dg_ckv = _rms_bwd(dckv_n, p, g_ckv, tm=512, name="rms_ckv_bwd", col_block=P_CKV // KV_LORA, out_dtype=BF16)
    dp = jnp.concatenate([dq_sb, dk_sb, dv_sb, dckv, dkr_u, dcq], axis=1)
    dh = _matmul_nt(dp, wi, tm=512, tn=1024, out_dtype=F32, name="d_proj_in")
    dw_in = _matmul_tn(h, dp, tm=1024, tn=768, tk=512, name="dw_in")
    dx, dg_mix = _rms_bwd(dh, xf, g_mix, tm=512, name="rms_mix_bwd", residual=dx1)

    grads_full = {"w_in": _w_in_layout_t(dw_in), "w_uq": _w_uq_layout_t(dw_uq), "w_ukv": _w_ukv_layout_t(dw_ukv),
                  "w_out": dw_out, "w_up": _ff_deinterleave(dw_up), "w_down": dw_down}
    dcw_full = _ff_deinterleave(dcw)
    dcw_rows = dcw_full.reshape(3, N_DEV, 2 * D_FF // N_DEV).transpose(1, 0, 2).reshape(N_DEV, CONV_SHARD)
    dcw_rows = jnp.pad(dcw_rows, ((0, 0), (0, CONV_ROWS_F32 * PACK_W - CONV_SHARD))).reshape(N_DEV, CONV_ROWS_F32, PACK_W)
    g8 = jnp.concatenate([_stack(grads_full[n], n) for n, _ in _SLAB] + [_pad_rows(dcw_rows, ROW_ALIGN)], axis=1)
    from_sib = _rs_sibling(g8, name="rs_sibling")
    f4, h4 = _rs_chip_sum(g8, from_sib, c_idx, tr=RS_TILE, name="rs_chip_sum")
    r3 = _rs_chips(h4, name="rs_chips")
    gsl = _rs_final(f4, r3, chip_idx, tr=RS_TILE, name="rs_final")

    grad = {}
    for n, _ in _SLAB:
        off, rows = _SLAB_OFF[n]
        grad[n] = gsl[off:off + rows].reshape(SHARD_SHAPES[n])
    grad["conv_w"] = gsl[CONV_OFF:CONV_OFF + CONV_ROWS_F32].reshape(-1)[:CONV_SHARD].reshape(3, 2 * D_FF // N_DEV)

    small_part = {"g_mix": dg_mix, "g_cq": dg_cq, "g_ckv": dg_ckv, "g_sb_out": dg_sb, "g_mla_out": dg_mla,
                  "g_ffn": dg_ffn, "conv_b": _ff_deinterleave(dcb), "g_final": dg_final}
    small_all = _all_gather(_pack_small(small_part), name="ag_small_grads")
    gsmall = _sum8(small_all, name="sum_small_grads")

    params = {"w_in": (w_in, m_w_in, v_w_in), "w_uq": (w_uq, m_w_uq, v_w_uq), "w_ukv": (w_ukv, m_w_ukv, v_w_ukv),
              "w_out": (w_out, m_w_out, v_w_out), "w_up": (w_up, m_w_up, v_w_up), "conv_w": (conv_w, m_conv_w, v_conv_w),
              "w_down": (w_down, m_w_down, v_w_down)}
    delta, new_m, new_v = {}, {}, {}
    for n, (w_, m_, v_) in params.items():
        d_, mn_, vn_ = _adamw(w_[0], grad[n], m_[0], v_[0], name="adamw_" + n)
        delta[n], new_m[n], new_v[n] = d_[None], mn_[None], vn_[None]
        grad[n] = grad[n][None]
    small_w = {"g_mix": g_mix, "g_cq": g_cq, "g_ckv": g_ckv, "g_sb_out": g_sb_out, "g_mla_out": g_mla_out,
               "g_ffn": g_ffn, "conv_b": conv_b, "g_final": g_final}
    small_m = {"g_mix": m_g_mix, "g_cq": m_g_cq, "g_ckv": m_g_ckv, "g_sb_out": m_g_sb_out, "g_mla_out": m_g_mla_out,
               "g_ffn": m_g_ffn, "conv_b": m_conv_b, "g_final": m_g_final}
    small_v = {"g_mix": v_g_mix, "g_cq": v_g_cq, "g_ckv": v_g_ckv, "g_sb_out": v_g_sb_out, "g_mla_out": v_g_mla_out,
               "g_ffn": v_g_ffn, "conv_b": v_conv_b, "g_final": v_g_final}
    ds_, ms_, vs_ = _adamw(_pack_small(small_w), gsmall, _pack_small(small_m), _pack_small(small_v), name="adamw_small")
    for src, dst in ((_unpack_small(gsmall), grad), (_unpack_small(ds_), delta), (_unpack_small(ms_), new_m), (_unpack_small(vs_), new_v)):
        for n, _ in SMALL:
            dst[n] = src[n].reshape(small_w[n].shape)

    loss = lax.psum(loss_row[0, 0], MESH_AXES)
    order = ("g_mix", "w_in", "g_cq", "w_uq", "g_ckv", "w_ukv", "g_sb_out", "g_mla_out", "w_out", "g_ffn", "w_up",
             "conv_w", "conv_b", "w_down", "g_final")
    return (loss, dx.reshape(B, S, D), *[grad[n] for n in order], *[delta[n] for n in order],
            *[new_m[n] for n in order], *[new_v[n] for n in order])
```

```python
import jax
import jax.numpy as jnp
from jax import lax
from jax.experimental import pallas as pl
from jax.experimental.pallas import tpu as pltpu

F32 = jnp.float32
BF16 = jnp.bfloat16

D_MODEL = 1024
SB_HEADS = 8
SB_HEAD_DIM = 64
MLA_HEADS = 8
MLA_NOPE = 64
MLA_ROPE = 32
MLA_V = 64
Q_LORA = 384
KV_LORA = 256
D_FF = 2816
ROPE_BASE = 10000.0
EPS = 1e-6
SB_W = SB_HEADS * SB_HEAD_DIM
MLA_W = MLA_HEADS * MLA_V
MLA_QK = MLA_NOPE + MLA_ROPE
IN_COLS = 3 * SB_W + Q_LORA + KV_LORA + MLA_ROPE

ADAM_LR = 0.001
ADAM_B1 = 0.9
ADAM_B2 = 0.999
ADAM_EPS = 1e-08
ADAM_WD = 0.01
ADAM_STEP = 10

N_DEV = 8
MESH_AXES = ("x", "y", "c")
LANES = 128
V7X_VMEM_LIMIT = 56 * 1024 * 1024
FF_BLK = 256
N_FF_BLK = D_FF // FF_BLK

P_Q, P_K, P_V = 0, SB_W, 2 * SB_W
P_CKV = 3 * SB_W
P_KRT = P_CKV + KV_LORA
P_CQ = P_KRT + LANES
P_COLS = P_CQ + Q_LORA

MESH = pl.DeviceIdType.MESH
ANY = pl.BlockSpec(memory_space=pl.ANY)


def _cparams(sem=None, vmem=V7X_VMEM_LIMIT):
    return pltpu.CompilerParams(dimension_semantics=sem, vmem_limit_bytes=vmem)


def _matmul_nn(a, b, *, tm, tn, out_dtype, name, residual=None):
    M, K = a.shape
    N = b.shape[1]
    in_specs = [pl.BlockSpec((tm, K), lambda i, j: (i, 0)), pl.BlockSpec((K, tn), lambda i, j: (0, j))]
    args = [a, b]
    if residual is not None:
        in_specs.append(pl.BlockSpec((tm, tn), lambda i, j: (i, j)))
        args.append(residual)

    def body(*refs):
        a_ref, b_ref = refs[0], refs[1]
        o_ref = refs[-1]
        acc = jnp.dot(a_ref[...].astype(BF16), b_ref[...], preferred_element_type=F32)
        if residual is not None:
            acc = acc + refs[2][...]
        o_ref[...] = acc.astype(out_dtype)

    return pl.pallas_call(
        body, name=name, grid=(M // tm, N // tn), in_specs=in_specs,
        out_specs=pl.BlockSpec((tm, tn), lambda i, j: (i, j)),
        out_shape=jax.ShapeDtypeStruct((M, N), out_dtype),
        compiler_params=_cparams(("parallel", "parallel")),
    )(*args)


def _matmul_nt(a, b, *, tm, tn, out_dtype, name):
    M, K = a.shape
    N = b.shape[0]

    def body(a_ref, b_ref, o_ref):
        acc = lax.dot_general(a_ref[...].astype(BF16), b_ref[...], (((1,), (1,)), ((), ())),
                              preferred_element_type=F32)
        o_ref[...] = acc.astype(out_dtype)

    return pl.pallas_call(
        body, name=name, grid=(M // tm, N // tn),
        in_specs=[pl.BlockSpec((tm, K), lambda i, j: (i, 0)), pl.BlockSpec((tn, K), lambda i, j: (j, 0))],
        out_specs=pl.BlockSpec((tm, tn), lambda i, j: (i, j)),
        out_shape=jax.ShapeDtypeStruct((M, N), out_dtype),
        compiler_params=_cparams(("parallel", "parallel")),
    )(a, b)


def _matmul_tn(a, b, *, tm, tn, tk, name):
    K, M = a.shape
    N = b.shape[1]

    def body(a_ref, b_ref, o_ref):
        k = pl.program_id(2)
        part = lax.dot_general(a_ref[...].astype(BF16), b_ref[...].astype(BF16), (((0,), (0,)), ((), ())),
                               preferred_element_type=F32)

        @pl.when(k == 0)
        def _():
            o_ref[...] = part

        @pl.when(k > 0)
        def _():
            o_ref[...] += part

    return pl.pallas_call(
        body, name=name, grid=(M // tm, N // tn, K // tk),
        in_specs=[pl.BlockSpec((tk, tm), lambda i, j, k: (k, i)), pl.BlockSpec((tk, tn), lambda i, j, k: (k, j))],
        out_specs=pl.BlockSpec((tm, tn), lambda i, j, k: (i, j)),
        out_shape=jax.ShapeDtypeStruct((M, N), F32),
        compiler_params=_cparams(("parallel", "parallel", "arbitrary")),
    )(a, b)


def _rms(xf, g):
    r = lax.rsqrt(jnp.mean(xf * xf, axis=1, keepdims=True) + EPS)
    return (xf * r) * g


def _rms_grad(dyf, xf, g):
    r = lax.rsqrt(jnp.mean(xf * xf, axis=1, keepdims=True) + EPS)
    xh = xf * r
    dyg = dyf * g
    dx = r * (dyg - xh * jnp.mean(dyg * xh, axis=1, keepdims=True))
    return dx, jnp.sum(dyf * xh, axis=0, keepdims=True)


def _accumulate(ref, part):
    @pl.when(pl.program_id(0) == 0)
    def _():
        ref[...] = part

    @pl.when(pl.program_id(0) > 0)
    def _():
        ref[...] += part


def _rms_fwd(x, g, *, tm, name, col_block=0):
    T = x.shape[0]
    C = g.shape[1]

    def body(x_ref, g_ref, o_ref):
        o_ref[...] = _rms(x_ref[...], g_ref[...]).astype(BF16)

    return pl.pallas_call(
        body, name=name, grid=(T // tm,),
        in_specs=[pl.BlockSpec((tm, C), lambda i: (i, col_block)), pl.BlockSpec((1, C), lambda i: (0, 0))],
        out_specs=pl.BlockSpec((tm, C), lambda i: (i, 0)),
        out_shape=jax.ShapeDtypeStruct((T, C), BF16),
        compiler_params=_cparams(("parallel",)),
    )(x, g)


def _rms_bwd(dy, x, g, *, tm, name, residual=None, col_block=0, out_dtype=F32):
    T = dy.shape[0]
    C = g.shape[1]
    in_specs = [pl.BlockSpec((tm, C), lambda i: (i, 0)), pl.BlockSpec((tm, C), lambda i: (i, col_block)),
                pl.BlockSpec((1, C), lambda i: (0, 0))]
    args = [dy, x, g]
    if residual is not None:
        in_specs.append(pl.BlockSpec((tm, C), lambda i: (i, 0)))
        args.append(residual)

    def body(*refs):
        dy_ref, x_ref, g_ref = refs[:3]
        dx_ref, dg_ref = refs[-2:]
        dx, part = _rms_grad(dy_ref[...].astype(F32), x_ref[...], g_ref[...])
        if residual is not None:
            dx = dx + refs[3][...]
        dx_ref[...] = dx.astype(out_dtype)
        _accumulate(dg_ref, part)

    return pl.pallas_call(
        body, name=name, grid=(T // tm,), in_specs=in_specs,
        out_specs=[pl.BlockSpec((tm, C), lambda i: (i, 0)), pl.BlockSpec((1, C), lambda i: (0, 0))],
        out_shape=[jax.ShapeDtypeStruct((T, C), out_dtype), jax.ShapeDtypeStruct((1, C), F32)],
        compiler_params=_cparams(("arbitrary",)),
    )(*args)


def _rms2_fwd(xa, xb, ga, gb, *, tm, name):
    T, C = xa.shape

    def body(xa_ref, xb_ref, ga_ref, gb_ref, o_ref):
        o_ref[:, :C] = _rms(xa_ref[...], ga_ref[...]).astype(BF16)
        o_ref[:, C:] = _rms(xb_ref[...], gb_ref[...]).astype(BF16)

    row = pl.BlockSpec((tm, C), lambda i: (i, 0))
    gsp = pl.BlockSpec((1, C), lambda i: (0, 0))
    return pl.pallas_call(
        body, name=name, grid=(T // tm,), in_specs=[row, row, gsp, gsp],
        out_specs=pl.BlockSpec((tm, 2 * C), lambda i: (i, 0)),
        out_shape=jax.ShapeDtypeStruct((T, 2 * C), BF16),
        compiler_params=_cparams(("parallel",)),
    )(xa, xb, ga, gb)


def _rms2_bwd(dy, xa, xb, ga, gb, *, tm, name):
    T, C = xa.shape

    def body(dy_ref, xa_ref, xb_ref, ga_ref, gb_ref, dxa_ref, dxb_ref, dga_ref, dgb_ref):
        dxa, pa = _rms_grad(dy_ref[:, :C], xa_ref[...], ga_ref[...])
        dxb, pb = _rms_grad(dy_ref[:, C:], xb_ref[...], gb_ref[...])
        dxa_ref[...] = dxa
        dxb_ref[...] = dxb
        _accumulate(dga_ref, pa)
        _accumulate(dgb_ref, pb)

    row = pl.BlockSpec((tm, C), lambda i: (i, 0))
    gsp = pl.BlockSpec((1, C), lambda i: (0, 0))
    return pl.pallas_call(
        body, name=name, grid=(T // tm,),
        in_specs=[pl.BlockSpec((tm, 2 * C), lambda i: (i, 0)), row, row, gsp, gsp],
        out_specs=[row, row, gsp, gsp],
        out_shape=[jax.ShapeDtypeStruct((T, C), F32), jax.ShapeDtypeStruct((T, C), F32),
                   jax.ShapeDtypeStruct((1, C), F32), jax.ShapeDtypeStruct((1, C), F32)],
        compiler_params=_cparams(("arbitrary",)),
    )(dy, xa, xb, ga, gb)


def _final_loss(x2, g, tgt, *, tm, name):
    T, C = x2.shape

    def body(x_ref, g_ref, t_ref, dx_ref, dg_ref, loss_ref):
        xf = x_ref[...]
        gf = g_ref[...]
        err = _rms(xf, gf) - t_ref[...]
        lpart = 0.5 * jnp.sum(jnp.mean(err * err, axis=1, keepdims=True), axis=0, keepdims=True)
        dx, gpart = _rms_grad(err * (1.0 / C), xf, gf)
        dx_ref[...] = dx
        _accumulate(dg_ref, gpart)
        _accumulate(loss_ref, jnp.broadcast_to(lpart, (1, LANES)))

    return pl.pallas_call(
        body, name=name, grid=(T // tm,),
        in_specs=[pl.BlockSpec((tm, C), lambda i: (i, 0)), pl.BlockSpec((1, C), lambda i: (0, 0)),
                  pl.BlockSpec((tm, C), lambda i: (i, 0))],
        out_specs=[pl.BlockSpec((tm, C), lambda i: (i, 0)), pl.BlockSpec((1, C), lambda i: (0, 0)),
                   pl.BlockSpec((1, LANES), lambda i: (0, 0))],
        out_shape=[jax.ShapeDtypeStruct((T, C), F32), jax.ShapeDtypeStruct((1, C), F32),
                   jax.ShapeDtypeStruct((1, LANES), F32)],
        compiler_params=_cparams(("arbitrary",)),
    )(x2, g, tgt)


ATT_T = 256
ATT_PAIRS = 2
NEG_BIG = -1e30


def _lane_iota():
    return lax.broadcasted_iota(jnp.int32, (1, LANES), 1)


def _head_masks():
    first = _lane_iota() < SB_HEAD_DIM
    return first, jnp.logical_not(first)


def _pick(mask, x):
    return jnp.where(mask, x, jnp.zeros_like(x))


def _lane_value(t, lane):
    return jnp.sum(jnp.where(_lane_iota() == lane, t, 0.0), axis=1, keepdims=True)


def _split_hi_lo(x):
    hi = x.astype(BF16)
    lo = (x - hi.astype(F32)).astype(BF16)
    return jnp.concatenate([hi, lo], axis=1)


def _tri(n, kind):
    r = lax.broadcasted_iota(jnp.int32, (n, n), 0)
    c = lax.broadcasted_iota(jnp.int32, (n, n), 1)
    u = {"suffix_excl": r > c, "prefix_incl": r <= c, "prefix_excl": r < c}[kind].astype(BF16)
    return jnp.concatenate([u, u], axis=0)


def _dot_nt(a, b):
    return lax.dot_general(a, b, (((1,), (1,)), ((), ())), preferred_element_type=F32)


def _dot_tn(a, b):
    return lax.dot_general(a, b, (((0,), (0,)), ((), ())), preferred_element_type=F32)


def _dot(a, b):
    return jnp.dot(a, b, preferred_element_type=F32)


def _causal_mask(n, strict):
    r = lax.broadcasted_iota(jnp.int32, (n, n), 0)
    c = lax.broadcasted_iota(jnp.int32, (n, n), 1)
    return (c < r) if strict else (c <= r)


def _sb_logs(qh, kj, vis):
    z = _dot_nt(qh, kj)
    sp = jnp.log(1.0 + jnp.exp(-jnp.abs(z)))
    lb = jnp.minimum(z, 0.0) - sp
    lk = jnp.minimum(-z, 0.0) - sp
    if vis is not None:
        lk = jnp.where(vis, lk, 0.0)
    return lb, lk


def _sb_fwd(p, *, seq, name):
    T = p.shape[0]
    B = T // seq
    TQ = ATT_T
    nq = seq // TQ
    PP = ATT_PAIRS
    W = PP * LANES
    nstep = SB_W // W
    NH = 2 * PP

    def body(q_ref, k_ref, v_ref, o_ref, lt_ref, q_s, k_s, v_s):
        masks = _head_masks()
        q = q_ref[...] * (SB_HEAD_DIM ** -0.5)
        v = v_ref[...]
        k_s[...] = k_ref[...].astype(BF16)
        for h in range(NH):
            ps = slice((h // 2) * LANES, (h // 2 + 1) * LANES)
            hs = slice(h * LANES, (h + 1) * LANES)
            q_s[:, hs] = _pick(masks[h % 2], q[:, ps]).astype(BF16)
            v_s[:, hs] = _pick(masks[h % 2], v[:, ps]).astype(BF16)
        u_suf = _tri(TQ, "suffix_excl")
        vis = _causal_mask(TQ, True)

        def q_block(i, carry):
            q0 = pl.multiple_of(i * TQ, TQ)
            qs = [q_s[pl.ds(q0, TQ), h * LANES:(h + 1) * LANES] for h in range(NH)]

            def head(qh, kj, vj, r_run, mask):
                lb, lk = _sb_logs(qh, kj, mask)
                a = jnp.exp(lb + _dot(_split_hi_lo(lk), u_suf) + r_run)
                if mask is not None:
                    a = jnp.where(mask, a, 0.0)
                return _dot(a.astype(BF16), vj), r_run + jnp.sum(lk, axis=1, keepdims=True)

            def tile(k0, c, mask):
                rs, accs = list(c[:NH]), list(c[NH:])
                logs = [_sb_logs(qs[h], k_s[pl.ds(k0, TQ), (h // 2) * LANES:(h // 2 + 1) * LANES], mask) for h in range(NH)]
                sums = [_dot(_split_hi_lo(lk), u_suf) for _, lk in logs]
                for h in range(NH):
                    a = jnp.exp(logs[h][0] + sums[h] + rs[h])
                    if mask is not None:
                        a = jnp.where(mask, a, 0.0)
                    accs[h // 2] = accs[h // 2] + _dot(a.astype(BF16), v_s[pl.ds(k0, TQ), h * LANES:(h + 1) * LANES])
                    rs[h] = rs[h] + jnp.sum(logs[h][1], axis=1, keepdims=True)
                return tuple(rs) + tuple(accs)

            zero = jnp.zeros((TQ, 1), F32)
            c = tile(q0, (zero,) * NH + (jnp.zeros((TQ, LANES), F32),) * PP, vis)

            def k_block(jj, c):
                return tile(pl.multiple_of((i - 1 - jj) * TQ, TQ), c, None)

            c = lax.fori_loop(0, i, k_block, c)
            for pr in range(PP):
                ps = slice(pr * LANES, (pr + 1) * LANES)
                o_ref[pl.ds(q0, TQ), ps] = c[NH + pr]
                lt_ref[pl.ds(q0, TQ), ps] = jnp.where(masks[0], c[2 * pr], c[2 * pr + 1])
            return carry

        lax.fori_loop(0, nq, q_block, 0)

    blk = lambda off: pl.BlockSpec((seq, W), lambda b, g: (b, off + g))
    out_blk = pl.BlockSpec((seq, W), lambda b, g: (b, g))
    return pl.pallas_call(
        body, name=name, grid=(B, nstep),
        in_specs=[blk(P_Q // W), blk(P_K // W), blk(P_V // W)],
        out_specs=[out_blk, out_blk],
        out_shape=[jax.ShapeDtypeStruct((T, SB_W), F32), jax.ShapeDtypeStruct((T, SB_W), F32)],
        scratch_shapes=[pltpu.VMEM((seq, NH * LANES), BF16), pltpu.VMEM((seq, W), BF16), pltpu.VMEM((seq, NH * LANES), BF16)],
        compiler_params=_cparams(("parallel", "parallel")),
    )(p, p, p)


def _sb_bwd(p, ltot, do, *, seq, name):
    T = p.shape[0]
    B = T // seq
    TQ = ATT_T
    nq = seq // TQ
    PP = ATT_PAIRS
    W = PP * LANES
    nstep = SB_W // W
    NH = 2 * PP
    scale = SB_HEAD_DIM ** -0.5

    def body(q_ref, k_ref, v_ref, lt_ref, do_ref, dq_ref, dk_ref, dv_ref, q_s, k_s, v_s, do_s, dk_s, dv_s):
        masks = _head_masks()
        q = q_ref[...] * scale
        dof = do_ref[...]
        k_s[...] = k_ref[...].astype(BF16)
        v_s[...] = v_ref[...].astype(BF16)
        for h in range(NH):
            ps = slice((h // 2) * LANES, (h // 2 + 1) * LANES)
            hs = slice(h * LANES, (h + 1) * LANES)
            q_s[:, hs] = _pick(masks[h % 2], q[:, ps]).astype(BF16)
            do_s[:, hs] = _pick(masks[h % 2], dof[:, ps]).astype(BF16)
        dk_s[...] = jnp.zeros_like(dk_s)
        dv_s[...] = jnp.zeros_like(dv_s)
        u_pin = _tri(TQ, "prefix_incl")
        u_pex = _tri(TQ, "prefix_excl")
        vis = _causal_mask(TQ, True)

        def q_block(i, carry):
            q0 = pl.multiple_of(i * TQ, TQ)
            qs = [q_s[pl.ds(q0, TQ), h * LANES:(h + 1) * LANES] for h in range(NH)]
            dos = [do_s[pl.ds(q0, TQ), h * LANES:(h + 1) * LANES] for h in range(NH)]
            lt = lt_ref[pl.ds(q0, TQ), :]
            lts = [_lane_value(lt[:, (h // 2) * LANES:(h // 2 + 1) * LANES], (h % 2) * SB_HEAD_DIM) for h in range(NH)]

            def tile(k0, c, mask):
                cs, gs, accs = list(c[:NH]), list(c[NH:2 * NH]), list(c[2 * NH:])
                kjs = [k_s[pl.ds(k0, TQ), pr * LANES:(pr + 1) * LANES] for pr in range(PP)]
                vjs = [v_s[pl.ds(k0, TQ), pr * LANES:(pr + 1) * LANES] for pr in range(PP)]
                logs = [_sb_logs(qs[h], kjs[h // 2], mask) for h in range(NH)]
                pins = [_dot(_split_hi_lo(lk), u_pin) for _, lk in logs]
                das = [_dot_nt(dos[h], vjs[h // 2]) for h in range(NH)]
                a_l, g_l = [], []
                for h in range(NH):
                    a = jnp.exp(logs[h][0] + ((lts[h] - cs[h]) - pins[h]))
                    if mask is not None:
                        a = jnp.where(mask, a, 0.0)
                    a_l.append(a)
                    g_l.append(das[h] * a)
                pres = [_dot(_split_hi_lo(g), u_pex) for g in g_l]
                dz_l = []
                for h in range(NH):
                    dz = g_l[h] - jnp.exp(logs[h][0]) * (g_l[h] + (pres[h] + gs[h]))
                    if mask is not None:
                        dz = jnp.where(mask, dz, 0.0)
                    dz_l.append(dz.astype(BF16))
                for h in range(NH):
                    accs[h] = accs[h] + _dot(dz_l[h], kjs[h // 2])
                for pr in range(PP):
                    ps = slice(pr * LANES, (pr + 1) * LANES)
                    ha, hb = 2 * pr, 2 * pr + 1
                    dk_s[pl.ds(k0, TQ), ps] += _dot_tn(dz_l[ha], qs[ha]) + _dot_tn(dz_l[hb], qs[hb])
                    dv_s[pl.ds(k0, TQ), ps] += _dot_tn(a_l[ha].astype(BF16), dos[ha]) + _dot_tn(a_l[hb].astype(BF16), dos[hb])
                for h in range(NH):
                    cs[h] = cs[h] + jnp.sum(logs[h][1], axis=1, keepdims=True)
                    gs[h] = gs[h] + jnp.sum(g_l[h], axis=1, keepdims=True)
                return tuple(cs) + tuple(gs) + tuple(accs)

            z1 = jnp.zeros((TQ, 1), F32)
            zl = jnp.zeros((TQ, LANES), F32)

            def k_block(j, c):
                return tile(pl.multiple_of(j * TQ, TQ), c, None)

            c = lax.fori_loop(0, i, k_block, (z1,) * (2 * NH) + (zl,) * NH)
            c = tile(q0, c, vis)
            for pr in range(PP):
                dq = jnp.where(masks[0], c[2 * NH + 2 * pr], c[2 * NH + 2 * pr + 1]) * scale
                dq_ref[pl.ds(q0, TQ), pr * LANES:(pr + 1) * LANES] = dq.astype(BF16)
            return carry

        lax.fori_loop(0, nq, q_block, 0)
        dk_ref[...] = dk_s[...].astype(BF16)
        dv_ref[...] = dv_s[...].astype(BF16)

    blk = lambda off: pl.BlockSpec((seq, W), lambda b, g: (b, off + g))
    out_blk = pl.BlockSpec((seq, W), lambda b, g: (b, g))
    return pl.pallas_call(
        body, name=name, grid=(B, nstep),
        in_specs=[blk(P_Q // W), blk(P_K // W), blk(P_V // W), out_blk, out_blk],
        out_specs=[out_blk, out_blk, out_blk],
        out_shape=[jax.ShapeDtypeStruct((T, SB_W), BF16) for _ in range(3)],
        scratch_shapes=[pltpu.VMEM((seq, NH * LANES), BF16), pltpu.VMEM((seq, W), BF16), pltpu.VMEM((seq, W), BF16),
                        pltpu.VMEM((seq, NH * LANES), BF16), pltpu.VMEM((seq, W), F32), pltpu.VMEM((seq, W), F32)],
        compiler_params=_cparams(("parallel", "parallel")),
    )(p, p, p, ltot, do)


def _mla_masks():
    lane = lax.broadcasted_iota(jnp.int32, (1, 2 * LANES), 1)
    ma = (lane < MLA_NOPE) | ((lane >= LANES) & (lane < LANES + MLA_ROPE))
    mb = ((lane >= MLA_NOPE) & (lane < LANES)) | ((lane >= LANES + MLA_ROPE) & (lane < LANES + 2 * MLA_ROPE))
    return ma, mb


def _mla_fwd(qm, kvm, krt, *, seq, name):
    T = qm.shape[0]
    B = T // seq
    TQ = ATT_T
    nq = seq // TQ
    PP = ATT_PAIRS
    W = PP * LANES
    nstep = MLA_W // W
    NH = 2 * PP
    CW = 2 * LANES
    scale = MLA_QK ** -0.5

    def body(qn_ref, qr_ref, kn_ref, v_ref, kr_ref, o_ref, lse_ref, q_s, kc_s, v_s):
        hm = _head_masks()
        mm = _mla_masks()
        v = v_ref[...]
        for pr in range(PP):
            ps = slice(pr * LANES, (pr + 1) * LANES)
            qc = jnp.concatenate([qn_ref[:, ps], qr_ref[:, ps]], axis=1)
            kc_s[:, pr * CW:(pr + 1) * CW] = jnp.concatenate([kn_ref[:, ps], kr_ref[...]], axis=1)
            for e in range(2):
                h = 2 * pr + e
                q_s[:, h * CW:(h + 1) * CW] = _pick(mm[e], qc)
                v_s[:, h * LANES:(h + 1) * LANES] = _pick(hm[e], v[:, ps])
        vis = _causal_mask(TQ, False)

        def q_block(i, carry):
            q0 = pl.multiple_of(i * TQ, TQ)
            qs = [q_s[pl.ds(q0, TQ), h * CW:(h + 1) * CW] for h in range(NH)]

            def tile(k0, c, mask):
                ms, ls, accs = list(c[:NH]), list(c[NH:2 * NH]), list(c[2 * NH:])
                ss = [_dot_nt(qs[h], kc_s[pl.ds(k0, TQ), (h // 2) * CW:(h // 2 + 1) * CW]) * scale for h in range(NH)]
                if mask is not None:
                    ss = [jnp.where(mask, s, NEG_BIG) for s in ss]
                m_new = [jnp.maximum(ms[h], jnp.max(ss[h], axis=1, keepdims=True)) for h in range(NH)]
                alphas = [jnp.exp(ms[h] - m_new[h]) for h in range(NH)]
                prs = [jnp.exp(ss[h] - m_new[h]) for h in range(NH)]
                outs = [_dot(prs[h].astype(BF16), v_s[pl.ds(k0, TQ), h * LANES:(h + 1) * LANES]) for h in range(NH)]
                ls = [alphas[h] * ls[h] + jnp.sum(prs[h], axis=1, keepdims=True) for h in range(NH)]
                for pr in range(PP):
                    accs[pr] = accs[pr] * jnp.where(hm[0], alphas[2 * pr], alphas[2 * pr + 1]) + outs[2 * pr] + outs[2 * pr + 1]
                return tuple(m_new) + tuple(ls) + tuple(accs)

            neg = jnp.full((TQ, 1), NEG_BIG, F32)
            z1 = jnp.zeros((TQ, 1), F32)

            def k_block(j, c):
                return tile(pl.multiple_of(j * TQ, TQ), c, None)

            c = lax.fori_loop(0, i, k_block, (neg,) * NH + (z1,) * NH + (jnp.zeros((TQ, LANES), F32),) * PP)
            c = tile(q0, c, vis)
            for pr in range(PP):
                ps = slice(pr * LANES, (pr + 1) * LANES)
                m_a, m_b, l_a, l_b = c[2 * pr], c[2 * pr + 1], c[NH + 2 * pr], c[NH + 2 * pr + 1]
                o_ref[pl.ds(q0, TQ), ps] = c[2 * NH + pr] / jnp.where(hm[0], l_a, l_b)
                lse_ref[pl.ds(q0, TQ), ps] = jnp.where(hm[0], m_a + jnp.log(l_a), m_b + jnp.log(l_b))
            return carry

        lax.fori_loop(0, nq, q_block, 0)

    blk = lambda off: pl.BlockSpec((seq, W), lambda b, g: (b, off + g))
    out_blk = pl.BlockSpec((seq, W), lambda b, g: (b, g))
    return pl.pallas_call(
        body, name=name, grid=(B, nstep),
        in_specs=[blk(0), blk(nstep), blk(0), blk(nstep), pl.BlockSpec((seq, LANES), lambda b, g: (b, 0))],
        out_specs=[out_blk, out_blk],
        out_shape=[jax.ShapeDtypeStruct((T, MLA_W), F32), jax.ShapeDtypeStruct((T, MLA_W), F32)],
        scratch_shapes=[pltpu.VMEM((seq, NH * CW), BF16), pltpu.VMEM((seq, PP * CW), BF16), pltpu.VMEM((seq, NH * LANES), BF16)],
        compiler_params=_cparams(("parallel", "parallel")),
    )(qm, qm, kvm, kvm, krt)


def _mla_bwd(qm, kvm, krt, o, lse, do, *, seq, name):
    T = qm.shape[0]
    B = T // seq
    TQ = ATT_T
    nq = seq // TQ
    PP = ATT_PAIRS
    W = PP * LANES
    nstep = MLA_W // W
    NH = 2 * PP
    CW = 2 * LANES
    scale = MLA_QK ** -0.5

    def body(qn_ref, qr_ref, kn_ref, v_ref, kr_ref, o_ref, lse_ref, do_ref,
             dqn_ref, dqr_ref, dkn_ref, dv_ref, dkr_ref, q_s, kc_s, do_s, dkc_s, dv_s):
        hm = _head_masks()
        mm = _mla_masks()
        dof = do_ref[...]
        for pr in range(PP):
            ps = slice(pr * LANES, (pr + 1) * LANES)
            qc = jnp.concatenate([qn_ref[:, ps], qr_ref[:, ps]], axis=1)
            kc_s[:, pr * CW:(pr + 1) * CW] = jnp.concatenate([kn_ref[:, ps], kr_ref[...]], axis=1)
            for e in range(2):
                h = 2 * pr + e
                q_s[:, h * CW:(h + 1) * CW] = _pick(mm[e], qc)
                do_s[:, h * LANES:(h + 1) * LANES] = _pick(hm[e], dof[:, ps]).astype(BF16)
        dkc_s[...] = jnp.zeros_like(dkc_s)
        dv_s[...] = jnp.zeros_like(dv_s)
        vis = _causal_mask(TQ, False)

        def q_block(i, carry):
            q0 = pl.multiple_of(i * TQ, TQ)
            qs = [q_s[pl.ds(q0, TQ), h * CW:(h + 1) * CW] for h in range(NH)]
            dos = [do_s[pl.ds(q0, TQ), h * LANES:(h + 1) * LANES] for h in range(NH)]
            lse_t = lse_ref[pl.ds(q0, TQ), :]
            dd = do_ref[pl.ds(q0, TQ), :] * o_ref[pl.ds(q0, TQ), :]
            lses, ds_ = [], []
            for h in range(NH):
                ps = slice((h // 2) * LANES, (h // 2 + 1) * LANES)
                lses.append(_lane_value(lse_t[:, ps], (h % 2) * MLA_V))
                ds_.append(jnp.sum(_pick(hm[h % 2], dd[:, ps]), axis=1, keepdims=True))

            def tile(k0, c, mask):
                accs = list(c)
                kcs = [kc_s[pl.ds(k0, TQ), pr * CW:(pr + 1) * CW] for pr in range(PP)]
                vjs = [v_ref[pl.ds(k0, TQ), pr * LANES:(pr + 1) * LANES] for pr in range(PP)]
                ss = [_dot_nt(qs[h], kcs[h // 2]) * scale for h in range(NH)]
                dps = [_dot_nt(dos[h], vjs[h // 2]) for h in range(NH)]
                p_l, ds_l = [], []
                for h in range(NH):
                    pr_ = jnp.exp(ss[h] - lses[h])
                    if mask is not None:
                        pr_ = jnp.where(mask, pr_, 0.0)
                    p_l.append(pr_.astype(BF16))
                    ds_l.append((pr_ * (dps[h] - ds_[h]) * scale).astype(BF16))
                for h in range(NH):
                    accs[h] = accs[h] + _dot(ds_l[h], kcs[h // 2])
                for pr in range(PP):
                    ha, hb = 2 * pr, 2 * pr + 1
                    dkc_s[pl.ds(k0, TQ), pr * CW:(pr + 1) * CW] += _dot_tn(ds_l[ha], qs[ha]) + _dot_tn(ds_l[hb], qs[hb])
                    dv_s[pl.ds(k0, TQ), pr * LANES:(pr + 1) * LANES] += _dot_tn(p_l[ha], dos[ha]) + _dot_tn(p_l[hb], dos[hb])
                return tuple(accs)

            zc = jnp.zeros((TQ, CW), F32)

            def k_block(j, c):
                return tile(pl.multiple_of(j * TQ, TQ), c, None)

            c = lax.fori_loop(0, i, k_block, (zc,) * NH)
            c = tile(q0, c, vis)
            for pr in range(PP):
                ps = slice(pr * LANES, (pr + 1) * LANES)
                dq = _pick(mm[0], c[2 * pr]) + _pick(mm[1], c[2 * pr + 1])
                dqn_ref[pl.ds(q0, TQ), ps] = dq[:, :LANES].astype(BF16)
                dqr_ref[pl.ds(q0, TQ), ps] = dq[:, LANES:]
            return carry

        lax.fori_loop(0, nq, q_block, 0)
        dkr = dkc_s[:, LANES:CW]
        for pr in range(PP):
            dkn_ref[:, pr * LANES:(pr + 1) * LANES] = dkc_s[:, pr * CW:pr * CW + LANES].astype(BF16)
            if pr > 0:
                dkr = dkr + dkc_s[:, pr * CW + LANES:(pr + 1) * CW]
        dv_ref[...] = dv_s[...].astype(BF16)
        g = pl.program_id(1)

        @pl.when(g == 0)
        def _():
            dkr_ref[...] = dkr

        @pl.when(g > 0)
        def _():
            dkr_ref[...] += dkr

    blk = lambda off: pl.BlockSpec((seq, W), lambda b, g: (b, off + g))
    out_blk = pl.BlockSpec((seq, W), lambda b, g: (b, g))
    one_blk = pl.BlockSpec((seq, LANES), lambda b, g: (b, 0))
    return pl.pallas_call(
        body, name=name, grid=(B, nstep),
        in_specs=[blk(0), blk(nstep), blk(0), blk(nstep), one_blk, out_blk, out_blk, out_blk],
        out_specs=[out_blk, out_blk, out_blk, out_blk, one_blk],
        out_shape=[jax.ShapeDtypeStruct((T, MLA_W), BF16), jax.ShapeDtypeStruct((T, MLA_W), F32),
                   jax.ShapeDtypeStruct((T, MLA_W), BF16), jax.ShapeDtypeStruct((T, MLA_W), BF16),
                   jax.ShapeDtypeStruct((T, LANES), F32)],
        scratch_shapes=[pltpu.VMEM((seq, NH * CW), BF16), pltpu.VMEM((seq, PP * CW), BF16), pltpu.VMEM((seq, NH * LANES), BF16),
                        pltpu.VMEM((seq, PP * CW), F32), pltpu.VMEM((seq, W), F32)],
        compiler_params=_cparams(("parallel", "arbitrary")),
    )(qm, qm, kvm, kvm, krt, o, lse, do)


def _rope_tables(pos_ref, invf_ref):
    ang = pos_ref[...].astype(F32) * invf_ref[...]
    first = (_lane_iota() % MLA_ROPE) < (MLA_ROPE // 2)
    return jnp.cos(ang), jnp.sin(ang), first


def _rope_apply(x, cos, sin, first):
    rot = jnp.where(first, -pltpu.roll(x, LANES - MLA_ROPE // 2, 1), pltpu.roll(x, MLA_ROPE // 2, 1))
    return x * cos + rot * sin


def _rope_apply_t(dy, cos, sin, first):
    dys = dy * sin
    rot_t = jnp.where(first, pltpu.roll(dys, LANES - MLA_ROPE // 2, 1), -pltpu.roll(dys, MLA_ROPE // 2, 1))
    return dy * cos + rot_t


def _rope_fwd(qfull, p, pos, invf, *, tm, name):
    T = qfull.shape[0]
    ntile = MLA_W // LANES

    def body(q_ref, kr_ref, pos_ref, invf_ref, qm_ref, krt_ref):
        cos, sin, first = _rope_tables(pos_ref, invf_ref)
        qm_ref[:, :MLA_W] = q_ref[:, :MLA_W].astype(BF16)
        for t in range(ntile):
            sl = slice(MLA_W + t * LANES, MLA_W + (t + 1) * LANES)
            qm_ref[:, sl] = _rope_apply(q_ref[:, sl], cos, sin, first).astype(BF16)
        krt_ref[...] = _rope_apply(kr_ref[...], cos, sin, first).astype(BF16)

    return pl.pallas_call(
        body, name=name, grid=(T // tm,),
        in_specs=[pl.BlockSpec((tm, 2 * MLA_W), lambda i: (i, 0)), pl.BlockSpec((tm, LANES), lambda i: (i, P_KRT // LANES)),
                  pl.BlockSpec((tm, 1), lambda i: (i, 0)), pl.BlockSpec((1, LANES), lambda i: (0, 0))],
        out_specs=[pl.BlockSpec((tm, 2 * MLA_W), lambda i: (i, 0)), pl.BlockSpec((tm, LANES), lambda i: (i, 0))],
        out_shape=[jax.ShapeDtypeStruct((T, 2 * MLA_W), BF16), jax.ShapeDtypeStruct((T, LANES), BF16)],
        compiler_params=_cparams(("parallel",)),
    )(qfull, p, pos, invf)


def _rope_bwd(dqr, dkr, pos, invf, *, tm, name):
    T = dqr.shape[0]
    ntile = MLA_W // LANES

    def body(dq_ref, dk_ref, pos_ref, invf_ref, oq_ref, ok_ref):
        cos, sin, first = _rope_tables(pos_ref, invf_ref)
        for t in range(ntile):
            sl = slice(t * LANES, (t + 1) * LANES)
            oq_ref[:, sl] = _rope_apply_t(dq_ref[:, sl], cos, sin, first).astype(BF16)
        ok_ref[...] = _rope_apply_t(dk_ref[...], cos, sin, first).astype(BF16)

    return pl.pallas_call(
        body, name=name, grid=(T // tm,),
        in_specs=[pl.BlockSpec((tm, MLA_W), lambda i: (i, 0)), pl.BlockSpec((tm, LANES), lambda i: (i, 0)),
                  pl.BlockSpec((tm, 1), lambda i: (i, 0)), pl.BlockSpec((1, LANES), lambda i: (0, 0))],
        out_specs=[pl.BlockSpec((tm, MLA_W), lambda i: (i, 0)), pl.BlockSpec((tm, LANES), lambda i: (i, 0))],
        out_shape=[jax.ShapeDtypeStruct((T, MLA_W), BF16), jax.ShapeDtypeStruct((T, LANES), BF16)],
        compiler_params=_cparams(("parallel",)),
    )(dqr, dkr, pos, invf)


CONV_ROWS = 256
HALO = 8


def _conv_taps(w_ref):
    return w_ref[0:1, :], w_ref[1:2, :], w_ref[2:3, :]


def _conv_rows(cur, prev, w, bias):
    ext = jnp.concatenate([prev, cur], axis=0)
    u1 = pltpu.roll(ext, 1, 0)[HALO:]
    u2 = pltpu.roll(ext, 2, 0)[HALO:]
    return w[2] * cur + w[1] * u1 + w[0] * u2 + bias, u1, u2


def _conv_fwd(u, w, bias, *, seq, name):
    T = u.shape[0]
    B = T // seq
    W2 = 2 * FF_BLK

    def body(u_ref, w_ref, b_ref, a_ref):
        wv = _conv_taps(w_ref)
        bv = b_ref[...]
        for c in range(seq // CONV_ROWS):
            r0 = c * CONV_ROWS
            cur = u_ref[r0:r0 + CONV_ROWS, :]
            prev = u_ref[r0 - HALO:r0, :] if c > 0 else jnp.zeros((HALO, W2), F32)
            y, _, _ = _conv_rows(cur, prev, wv, bv)
            gc = y[:, :FF_BLK]
            a_ref[r0:r0 + CONV_ROWS, :] = (gc * (1.0 / (1.0 + jnp.exp(-gc))) * y[:, FF_BLK:]).astype(BF16)

    return pl.pallas_call(
        body, name=name, grid=(B, N_FF_BLK),
        in_specs=[pl.BlockSpec((seq, W2), lambda b, j: (b, j)), pl.BlockSpec((3, W2), lambda b, j: (0, j)),
                  pl.BlockSpec((1, W2), lambda b, j: (0, j))],
        out_specs=pl.BlockSpec((seq, FF_BLK), lambda b, j: (b, j)),
        out_shape=jax.ShapeDtypeStruct((T, D_FF), BF16),
        compiler_params=_cparams(("parallel", "parallel")),
    )(u, w, bias)


def _conv_bwd(u, da, w, bias, *, seq, name):
    T = u.shape[0]
    B = T // seq
    W2 = 2 * FF_BLK
    nchunk = seq // CONV_ROWS

    def body(u_ref, da_ref, w_ref, b_ref, du_ref, dw_ref, db_ref, duc_s):
        wv = _conv_taps(w_ref)
        bv = b_ref[...]
        zrow = jnp.zeros((1, W2), F32)
        dw0, dw1, dw2, dbs = zrow, zrow, zrow, zrow
        for c in range(nchunk):
            r0 = c * CONV_ROWS
            cur = u_ref[r0:r0 + CONV_ROWS, :]
            prev = u_ref[r0 - HALO:r0, :] if c > 0 else jnp.zeros((HALO, W2), F32)
            y, u1, u2 = _conv_rows(cur, prev, wv, bv)
            gc = y[:, :FF_BLK]
            vc = y[:, FF_BLK:]
            sg = 1.0 / (1.0 + jnp.exp(-gc))
            dav = da_ref[r0:r0 + CONV_ROWS, :]
            duc = jnp.concatenate([dav * vc * (sg * (1.0 + gc * (1.0 - sg))), dav * (gc * sg)], axis=1)
            duc_s[r0:r0 + CONV_ROWS, :] = duc
            dw0 = dw0 + jnp.sum(duc * u2, axis=0, keepdims=True)
            dw1 = dw1 + jnp.sum(duc * u1, axis=0, keepdims=True)
            dw2 = dw2 + jnp.sum(duc * cur, axis=0, keepdims=True)
            dbs = dbs + jnp.sum(duc, axis=0, keepdims=True)
        duc_s[seq:seq + HALO, :] = jnp.zeros((HALO, W2), F32)
        n_ext = CONV_ROWS + HALO
        for c in range(nchunk):
            r0 = c * CONV_ROWS
            ext = duc_s[r0:r0 + n_ext, :]
            s1 = pltpu.roll(ext, n_ext - 1, 0)[:CONV_ROWS]
            s2 = pltpu.roll(ext, n_ext - 2, 0)[:CONV_ROWS]
            du_ref[r0:r0 + CONV_ROWS, :] = (wv[2] * ext[:CONV_ROWS] + wv[1] * s1 + wv[0] * s2).astype(BF16)

        first = pl.program_id(1) == 0

        @pl.when(first)
        def _():
            dw_ref[0:1, :] = dw0
            dw_ref[1:2, :] = dw1
            dw_ref[2:3, :] = dw2
            db_ref[...] = dbs

        @pl.when(jnp.logical_not(first))
        def _():
            dw_ref[0:1, :] += dw0
            dw_ref[1:2, :] += dw1
            dw_ref[2:3, :] += dw2
            db_ref[...] += dbs

    return pl.pallas_call(
        body, name=name, grid=(N_FF_BLK, B),
        in_specs=[pl.BlockSpec((seq, W2), lambda j, b: (b, j)), pl.BlockSpec((seq, FF_BLK), lambda j, b: (b, j)),
                  pl.BlockSpec((3, W2), lambda j, b: (0, j)), pl.BlockSpec((1, W2), lambda j, b: (0, j))],
        out_specs=[pl.BlockSpec((seq, W2), lambda j, b: (b, j)), pl.BlockSpec((3, W2), lambda j, b: (0, j)),
                   pl.BlockSpec((1, W2), lambda j, b: (0, j))],
        out_shape=[jax.ShapeDtypeStruct((T, 2 * D_FF), BF16), jax.ShapeDtypeStruct((3, 2 * D_FF), F32),
                   jax.ShapeDtypeStruct((1, 2 * D_FF), F32)],
        scratch_shapes=[pltpu.VMEM((seq + HALO, W2), F32)],
        compiler_params=_cparams(("parallel", "arbitrary")),
    )(u, da, w, bias)


def _place():
    return lax.axis_index("x"), lax.axis_index("y"), lax.axis_index("c")


def _other_chips(x, y):
    return [(1 - x, y), (x, 1 - y), (1 - x, 1 - y)]


def _all_gather(v, *, name):
    R, C = v.shape

    def body(v_ref, out_ref, send_sems, recv_sems, local_sem):
        x, y, c = _place()
        me, sibling = (x, y, c), (x, y, 1 - c)
        chips = _other_chips(x, y)

        def slab(px, py, pc):
            return out_ref.at[4 * px + 2 * py + pc]

        def copy(k, block, to, src=None):
            return pltpu.make_async_remote_copy(
                src_ref=slab(*block) if src is None else src, dst_ref=slab(*block),
                send_sem=send_sems.at[k], recv_sem=recv_sems.at[k], device_id=to, device_id_type=MESH)

        mine = pltpu.make_async_copy(v_ref, slab(*me), local_sem)
        mine.start()
        first = [copy(0, me, sibling, src=v_ref)]
        first += [copy(1 + j, me, (*chip, c), src=v_ref) for j, chip in enumerate(chips)]
        for cp in first:
            cp.start()
        passed = [copy(4 + j, (*chip, c), sibling) for j, chip in enumerate(chips)]
        for j, chip in enumerate(chips):
            copy(1 + j, (*chip, c), me).wait_recv()
            passed[j].start()
        copy(0, sibling, me).wait_recv()
        for j, chip in enumerate(chips):
            copy(4 + j, (*chip, 1 - c), me).wait_recv()
        for cp in first + passed:
            cp.wait_send()
        mine.wait()

    return pl.pallas_call(
        body, name=name, in_specs=[ANY], out_specs=ANY,
        out_shape=jax.ShapeDtypeStruct((N_DEV, R, C), v.dtype),
        scratch_shapes=[pltpu.SemaphoreType.DMA((7,)), pltpu.SemaphoreType.DMA((7,)), pltpu.SemaphoreType.DMA],
    )(v)


def _rs_sibling(g8, *, name):
    _, R, C = g8.shape

    def body(g_ref, out_ref, send_sems, recv_sems):
        x, y, c = _place()
        copies = [
            pltpu.make_async_remote_copy(
                src_ref=g_ref.at[2 * k + 1 - c], dst_ref=out_ref.at[k],
                send_sem=send_sems.at[k], recv_sem=recv_sems.at[k], device_id=(x, y, 1 - c), device_id_type=MESH)
            for k in range(4)]
        for cp in copies:
            cp.start()
        for cp in copies:
            cp.wait()

    return pl.pallas_call(
        body, name=name, in_specs=[ANY], out_specs=ANY,
        out_shape=jax.ShapeDtypeStruct((4, R, C), g8.dtype),
        scratch_shapes=[pltpu.SemaphoreType.DMA((4,)), pltpu.SemaphoreType.DMA((4,))],
    )(g8)


def _rs_chip_sum(g8, from_sibling, c_idx, *, tr, name):
    _, R, C = g8.shape

    def body(c_ref, a_ref, b_ref, f_ref, h_ref):
        s = a_ref[...] + b_ref[...]
        f_ref[...] = s
        h_ref[...] = s.astype(BF16)

    blk = pl.BlockSpec((None, tr, C), lambda k, r, c_ref: (k, r, 0))
    return pl.pallas_call(
        body, name=name,
        grid_spec=pltpu.PrefetchScalarGridSpec(
            num_scalar_prefetch=1, grid=(4, R // tr),
            in_specs=[pl.BlockSpec((None, tr, C), lambda k, r, c_ref: (2 * k + c_ref[0], r, 0)), blk],
            out_specs=[blk, blk]),
        out_shape=[jax.ShapeDtypeStruct((4, R, C), F32), jax.ShapeDtypeStruct((4, R, C), BF16)],
        compiler_params=_cparams(("parallel", "parallel")),
    )(c_idx, g8, from_sibling)


def _rs_chips(h4, *, name):
    _, R, C = h4.shape

    def body(h_ref, out_ref, send_sems, recv_sems):
        x, y, c = _place()
        copies = [
            pltpu.make_async_remote_copy(
                src_ref=h_ref.at[2 * cx + cy], dst_ref=out_ref.at[j],
                send_sem=send_sems.at[j], recv_sem=recv_sems.at[j], device_id=(cx, cy, c), device_id_type=MESH)
            for j, (cx, cy) in enumerate(_other_chips(x, y))]
        for cp in copies:
            cp.start()
        for cp in copies:
            cp.wait()

    return pl.pallas_call(
        body, name=name, in_specs=[ANY], out_specs=ANY,
        out_shape=jax.ShapeDtypeStruct((3, R, C), h4.dtype),
        scratch_shapes=[pltpu.SemaphoreType.DMA((3,)), pltpu.SemaphoreType.DMA((3,))],
    )(h4)


def _rs_final(f4, r3, chip_idx, *, tr, name):
    _, R, C = f4.shape

    def body(k_ref, f_ref, r_ref, o_ref):
        o_ref[...] = ((f_ref[...] + r_ref[0].astype(F32)) + r_ref[1].astype(F32)) + r_ref[2].astype(F32)

    return pl.pallas_call(
        body, name=name,
        grid_spec=pltpu.PrefetchScalarGridSpec(
            num_scalar_prefetch=1, grid=(R // tr,),
            in_specs=[pl.BlockSpec((None, tr, C), lambda r, k_ref: (k_ref[0], r, 0)),
                      pl.BlockSpec((3, tr, C), lambda r, k_ref: (0, r, 0))],
            out_specs=pl.BlockSpec((tr, C), lambda r, k_ref: (r, 0))),
        out_shape=jax.ShapeDtypeStruct((R, C), F32),
        compiler_params=_cparams(("parallel",)),
    )(chip_idx, f4, r3)


def _sum8(g, *, name):
    _, R, C = g.shape

    def body(g_ref, o_ref):
        acc = g_ref[0]
        for k in range(1, N_DEV):
            acc = acc + g_ref[k]
        o_ref[...] = acc

    return pl.pallas_call(
        body, name=name, out_shape=jax.ShapeDtypeStruct((R, C), F32),
    )(g)


def _adamw(w, g, m, v, *, name):
    R, C = w.shape
    tr = R if R <= 512 else 256
    c1 = 1.0 / (1.0 - ADAM_B1 ** ADAM_STEP)
    c2 = 1.0 / (1.0 - ADAM_B2 ** ADAM_STEP)

    def body(w_ref, g_ref, m_ref, v_ref, d_ref, mo_ref, vo_ref):
        gf = g_ref[...]
        mn = ADAM_B1 * m_ref[...] + (1.0 - ADAM_B1) * gf
        vn = ADAM_B2 * v_ref[...] + (1.0 - ADAM_B2) * (gf * gf)
        d_ref[...] = -ADAM_LR * ((mn * c1) / (jnp.sqrt(vn * c2) + ADAM_EPS) + ADAM_WD * w_ref[...])
        mo_ref[...] = mn
        vo_ref[...] = vn

    blk = pl.BlockSpec((tr, C), lambda i: (i, 0))
    shp = jax.ShapeDtypeStruct((R, C), F32)
    return pl.pallas_call(
        body, name=name, grid=(R // tr,), in_specs=[blk] * 4, out_specs=[blk] * 3,
        out_shape=[shp, shp, shp], compiler_params=_cparams(("parallel",)),
    )(w, g, m, v)


PACK_W = 1024
_SLAB = (("w_in", D_MODEL * (IN_COLS // N_DEV) // PACK_W), ("w_uq", Q_LORA * (MLA_HEADS * MLA_QK // N_DEV) // PACK_W),
         ("w_ukv", KV_LORA * (MLA_HEADS * (MLA_NOPE + MLA_V) // N_DEV) // PACK_W), ("w_out", D_MODEL // N_DEV),
         ("w_up", D_MODEL * (2 * D_FF // N_DEV) // PACK_W), ("w_down", D_FF // N_DEV))
ROW_ALIGN = 16


def _aligned(rows):
    return -(-rows // ROW_ALIGN) * ROW_ALIGN


_SLAB_OFF = {}
_off = 0
for _n, _r in _SLAB:
    _SLAB_OFF[_n] = (_off, _r)
    _off += _aligned(_r)
CONV_SHARD = 3 * (2 * D_FF // N_DEV)
CONV_ROWS_BF16 = 5
CONV_ROWS_F32 = 3
CONV_OFF = _off
SLAB_ROWS = CONV_OFF + ROW_ALIGN
RS_TILE = SLAB_ROWS // 7
assert RS_TILE * 7 == SLAB_ROWS and RS_TILE % ROW_ALIGN == 0


def _pad_rows(a, rows):
    pad = [(0, 0)] * a.ndim
    pad[-2] = (0, rows - a.shape[-2])
    return jnp.pad(a, pad)

SHARD_SHAPES = {"w_in": (D_MODEL, IN_COLS // N_DEV), "w_uq": (Q_LORA, MLA_HEADS * MLA_QK // N_DEV),
                "w_ukv": (KV_LORA, MLA_HEADS * (MLA_NOPE + MLA_V) // N_DEV), "w_out": (D_MODEL // N_DEV, D_MODEL),
                "w_up": (D_MODEL, 2 * D_FF // N_DEV), "w_down": (D_FF // N_DEV, D_MODEL)}
COL_SHARDED = ("w_in", "w_uq", "w_ukv", "w_up")


def _pack_slab(shards, conv_rows, dtype):
    parts = [_pad_rows(shards[n].astype(dtype).reshape(-1, PACK_W), _aligned(r)) for n, r in _SLAB]
    parts.append(_pad_rows(conv_rows, ROW_ALIGN))
    return jnp.concatenate(parts, axis=0)


def _unstack(g8, name):
    off, rows = _SLAB_OFF[name]
    r, c = SHARD_SHAPES[name]
    s = g8[:, off:off + rows].reshape(N_DEV, r, c)
    if name in COL_SHARDED:
        return s.transpose(1, 0, 2).reshape(r, N_DEV * c)
    return s.reshape(N_DEV * r, c)


def _stack(full, name):
    r, c = SHARD_SHAPES[name]
    if name in COL_SHARDED:
        s = full.reshape(r, N_DEV, c).transpose(1, 0, 2)
    else:
        s = full.reshape(N_DEV, r, c)
    rows = _SLAB_OFF[name][1]
    return _pad_rows(s.reshape(N_DEV, rows, PACK_W), _aligned(rows))


def _ff_interleave(a):
    lead = a.shape[:-1]
    return a.reshape(*lead, 2, N_FF_BLK, FF_BLK).swapaxes(-3, -2).reshape(*lead, 2 * D_FF)


def _ff_deinterleave(a):
    lead = a.shape[:-1]
    return a.reshape(*lead, N_FF_BLK, 2, FF_BLK).swapaxes(-3, -2).reshape(*lead, 2 * D_FF)


def _w_in_layout(w):
    cq = w[:, 3 * SB_W:3 * SB_W + Q_LORA]
    ckv = w[:, 3 * SB_W + Q_LORA:3 * SB_W + Q_LORA + KV_LORA]
    kr = w[:, IN_COLS - MLA_ROPE:]
    pad = jnp.zeros((w.shape[0], LANES - 2 * MLA_ROPE), w.dtype)
    return jnp.concatenate([w[:, :3 * SB_W], ckv, kr, kr, pad, cq], axis=1)


def _w_in_layout_t(dw):
    kr = dw[:, P_KRT:P_KRT + MLA_ROPE] + dw[:, P_KRT + MLA_ROPE:P_KRT + 2 * MLA_ROPE]
    return jnp.concatenate([dw[:, :3 * SB_W], dw[:, P_CQ:P_CQ + Q_LORA], dw[:, P_CKV:P_CKV + KV_LORA], kr], axis=1)


def _w_uq_layout(w):
    h = w.reshape(Q_LORA, MLA_HEADS, MLA_QK)
    qn = h[:, :, :MLA_NOPE].reshape(Q_LORA, MLA_W)
    qr = h[:, :, MLA_NOPE:].reshape(Q_LORA, MLA_HEADS // 2, 2 * MLA_ROPE)
    qr = jnp.pad(qr, ((0, 0), (0, 0), (0, LANES - 2 * MLA_ROPE))).reshape(Q_LORA, MLA_W)
    return jnp.concatenate([qn, qr], axis=1)


def _w_uq_layout_t(dw):
    qn = dw[:, :MLA_W].reshape(Q_LORA, MLA_HEADS, MLA_NOPE)
    qr = dw[:, MLA_W:].reshape(Q_LORA, MLA_HEADS // 2, LANES)[:, :, :2 * MLA_ROPE].reshape(Q_LORA, MLA_HEADS, MLA_ROPE)
    return jnp.concatenate([qn, qr], axis=2).reshape(Q_LORA, MLA_HEADS * MLA_QK)


def _w_ukv_layout(w):
    h = w.reshape(KV_LORA, MLA_HEADS, MLA_NOPE + MLA_V)
    return jnp.concatenate([h[:, :, :MLA_NOPE].reshape(KV_LORA, MLA_W), h[:, :, MLA_NOPE:].reshape(KV_LORA, MLA_W)], axis=1)


def _w_ukv_layout_t(dw):
    kn = dw[:, :MLA_W].reshape(KV_LORA, MLA_HEADS, MLA_NOPE)
    vv = dw[:, MLA_W:].reshape(KV_LORA, MLA_HEADS, MLA_V)
    return jnp.concatenate([kn, vv], axis=2).reshape(KV_LORA, MLA_HEADS * (MLA_NOPE + MLA_V))


SMALL = (("g_mix", D_MODEL), ("g_cq", Q_LORA), ("g_ckv", KV_LORA), ("g_sb_out", SB_W), ("g_mla_out", MLA_W),
         ("g_ffn", D_MODEL), ("conv_b", 2 * D_FF), ("g_final", D_MODEL))
SMALL_ROWS = 88


def _pack_small(d):
    flat = jnp.concatenate([d[n].reshape(-1) for n, _ in SMALL])
    flat = jnp.pad(flat, (0, SMALL_ROWS * LANES - flat.shape[0]))
    return flat.reshape(SMALL_ROWS, LANES)


def _unpack_small(a):
    flat = a.reshape(-1)
    out, off = {}, 0
    for n, size in SMALL:
        out[n] = flat[off:off + size]
        off += size
    return out


def kernel(x, positions, g_mix, w_in, g_cq, w_uq, g_ckv, w_ukv, g_sb_out, g_mla_out, w_out, g_ffn, w_up, conv_w, conv_b, w_down, g_final, loss_target, m_g_mix, m_w_in, m_g_cq, m_w_uq, m_g_ckv, m_w_ukv, m_g_sb_out, m_g_mla_out, m_w_out, m_g_ffn, m_w_up, m_conv_w, m_conv_b, m_w_down, m_g_final, v_g_mix, v_w_in, v_g_cq, v_w_uq, v_g_ckv, v_w_ukv, v_g_sb_out, v_g_mla_out, v_w_out, v_g_ffn, v_w_up, v_conv_w, v_conv_b, v_w_down, v_g_final):
    B, S, D = x.shape
    T = B * S
    xf = x.reshape(T, D)
    tgt = loss_target.reshape(T, D)
    pos = positions.reshape(T, 1)
    half = MLA_ROPE // 2
    inv_freq = 1.0 / (ROPE_BASE ** (jnp.arange(half, dtype=F32) * (2.0 / MLA_ROPE)))
    invf = jnp.tile(inv_freq, LANES // half).reshape(1, LANES)
    c_idx = lax.axis_index("c").astype(jnp.int32).reshape(1)
    chip_idx = (2 * lax.axis_index("x") + lax.axis_index("y")).astype(jnp.int32).reshape(1)

    shards = {"w_in": w_in[0], "w_uq": w_uq[0], "w_ukv": w_ukv[0], "w_out": w_out[0], "w_up": w_up[0], "w_down": w_down[0]}
    conv_bits = lax.bitcast_convert_type(conv_w[0], BF16).reshape(-1)
    conv_bits = jnp.pad(conv_bits, (0, CONV_ROWS_BF16 * PACK_W - conv_bits.shape[0])).reshape(CONV_ROWS_BF16, PACK_W)
    gathered = _all_gather(_pack_slab(shards, conv_bits, BF16), name="ag_weights")
    full = {n: _unstack(gathered, n) for n, _ in _SLAB}
    cw = gathered[:, CONV_OFF:CONV_OFF + CONV_ROWS_BF16].reshape(N_DEV, -1)[:, :2 * CONV_SHARD]
    cw = lax.bitcast_convert_type(cw.reshape(N_DEV, 3, 2 * D_FF // N_DEV, 2), F32)
    conv_w_full = cw.transpose(1, 0, 2).reshape(3, 2 * D_FF)

    wi = _w_in_layout(full["w_in"])
    wuq = _w_uq_layout(full["w_uq"])
    wukv = _w_ukv_layout(full["w_ukv"])
    wo = full["w_out"]
    wup = _ff_interleave(full["w_up"])
    wdn = full["w_down"]
    cwi = _ff_interleave(conv_w_full)
    cbi = _ff_interleave(conv_b)

    h = _rms_fwd(xf, g_mix, tm=512, name="rms_mix")
    p = _matmul_nn(h, wi, tm=512, tn=768, out_dtype=F32, name="proj_in")
    o_sb, ltot = _sb_fwd(p, seq=S, name="sb_fwd")
    cq = _rms_fwd(p, g_cq, tm=512, name="rms_cq", col_block=P_CQ // Q_LORA)
    ckv = _rms_fwd(p, g_ckv, tm=512, name="rms_ckv", col_block=P_CKV // KV_LORA)
    qfull = _matmul_nn(cq, wuq, tm=512, tn=1024, out_dtype=F32, name="proj_uq")
    kvm = _matmul_nn(ckv, wukv, tm=512, tn=1024, out_dtype=BF16, name="proj_ukv")
    qm, krt = _rope_fwd(qfull, p, pos, invf, tm=512, name="rope_fwd")
    o_mla, lse = _mla_fwd(qm, kvm, krt, seq=S, name="mla_fwd")
    ocat = _rms2_fwd(o_sb, o_mla, g_sb_out, g_mla_out, tm=512, name="rms_heads")
    x1 = _matmul_nn(ocat, wo, tm=512, tn=1024, out_dtype=F32, name="proj_out", residual=xf)
    hf = _rms_fwd(x1, g_ffn, tm=512, name="rms_ffn")
    u = _matmul_nn(hf, wup, tm=512, tn=512, out_dtype=F32, name="ffn_up")
    a = _conv_fwd(u, cwi, cbi, seq=S, name="conv_fwd")
    x2 = _matmul_nn(a, wdn, tm=512, tn=1024, out_dtype=F32, name="ffn_down", residual=x1)
    dx2, dg_final, loss_row = _final_loss(x2, g_final.reshape(1, D), tgt, tm=512, name="final_loss")

    da = _matmul_nt(dx2, wdn, tm=512, tn=D_FF // 2, out_dtype=F32, name="d_ffn_down")
    dw_down = _matmul_tn(a, dx2, tm=D_FF // 2, tn=1024, tk=512, name="dw_down")
    du, dcw, dcb = _conv_bwd(u, da, cwi, cbi, seq=S, name="conv_bwd")
    dhf = _matmul_nt(du, wup, tm=512, tn=512, out_dtype=F32, name="d_ffn_up")
    dw_up = _matmul_tn(hf, du, tm=1024, tn=512, tk=512, name="dw_up")
    dx1, dg_ffn = _rms_bwd(dhf, x1, g_ffn, tm=512, name="rms_ffn_bwd", residual=dx2)
    docat = _matmul_nt(dx1, wo, tm=512, tn=1024, out_dtype=F32, name="d_proj_out")
    dw_out = _matmul_tn(ocat, dx1, tm=1024, tn=1024, tk=512, name="dw_out")
    do_sb, do_mla, dg_sb, dg_mla = _rms2_bwd(docat, o_sb, o_mla, g_sb_out, g_mla_out, tm=512, name="rms_heads_bwd")
    dq_sb, dk_sb, dv_sb = _sb_bwd(p, ltot, do_sb, seq=S, name="sb_bwd")
    dqn, dqr, dkn, dvm, dkr = _mla_bwd(qm, kvm, krt, o_mla, lse, do_mla, seq=S, name="mla_bwd")
    dqr_u, dkr_u = _rope_bwd(dqr, dkr, pos, invf, tm=512, name="rope_bwd")
    dqm = jnp.concatenate([dqn, dqr_u], axis=1)
    dkvm = jnp.concatenate([dkn, dvm], axis=1)
    dcq_n = _matmul_nt(dqm, wuq, tm=512, tn=Q_LORA, out_dtype=F32, name="d_proj_uq")
    dw_uq = _matmul_tn(cq, dqm, tm=Q_LORA, tn=1024, tk=512, name="dw_uq")
    dckv_n = _matmul_nt(dkvm, wukv, tm=512, tn=KV_LORA, out_dtype=F32, name="d_proj_ukv")
    dw_ukv = _matmul_tn(ckv, dkvm, tm=KV_LORA, tn=1024, tk=512, name="dw_ukv")
    dcq, dg_cq = _rms_bwd(dcq_n, p, g_cq, tm=512, name="rms_cq_bwd", col_block=P_CQ // Q_LORA, out_dtype=BF16)
    dckv, dg_ckv = _rms_bwd(dckv_n, p, g_ckv, tm=512, name="rms_ckv_bwd", col_block=P_CKV // KV_LORA, out_dtype=BF16)
    dp = jnp.concatenate([dq_sb, dk_sb, dv_sb, dckv, dkr_u, dcq], axis=1)
    dh = _matmul_nt(dp, wi, tm=512, tn=1024, out_dtype=F32, name="d_proj_in")
    dw_in = _matmul_tn(h, dp, tm=1024, tn=768, tk=512, name="dw_in")
    dx, dg_mix = _rms_bwd(dh, xf, g_mix, tm=512, name="rms_mix_bwd", residual=dx1)

    grads_full = {"w_in": _w_in_layout_t(dw_in), "w_uq": _w_uq_layout_t(dw_uq), "w_ukv": _w_ukv_layout_t(dw_ukv),
                  "w_out": dw_out, "w_up": _ff_deinterleave(dw_up), "w_down": dw_down}
    dcw_full = _ff_deinterleave(dcw)
    dcw_rows = dcw_full.reshape(3, N_DEV, 2 * D_FF // N_DEV).transpose(1, 0, 2).reshape(N_DEV, CONV_SHARD)
    dcw_rows = jnp.pad(dcw_rows, ((0, 0), (0, CONV_ROWS_F32 * PACK_W - CONV_SHARD))).reshape(N_DEV, CONV_ROWS_F32, PACK_W)
    g8 = jnp.concatenate([_stack(grads_full[n], n) for n, _ in _SLAB] + [_pad_rows(dcw_rows, ROW_ALIGN)], axis=1)
    from_sib = _rs_sibling(g8, name="rs_sibling")
    f4, h4 = _rs_chip_sum(g8, from_sib, c_idx, tr=RS_TILE, name="rs_chip_sum")
    r3 = _rs_chips(h4, name="rs_chips")
    gsl = _rs_final(f4, r3, chip_idx, tr=RS_TILE, name="rs_final")

    grad = {}
    for n, _ in _SLAB:
        off, rows = _SLAB_OFF[n]
        grad[n] = gsl[off:off + rows].reshape(SHARD_SHAPES[n])
    grad["conv_w"] = gsl[CONV_OFF:CONV_OFF + CONV_ROWS_F32].reshape(-1)[:CONV_SHARD].reshape(3, 2 * D_FF // N_DEV)

    small_part = {"g_mix": dg_mix, "g_cq": dg_cq, "g_ckv": dg_ckv, "g_sb_out": dg_sb, "g_mla_out": dg_mla,
                  "g_ffn": dg_ffn, "conv_b": _ff_deinterleave(dcb), "g_final": dg_final}
    small_all = _all_gather(_pack_small(small_part), name="ag_small_grads")
    gsmall = _sum8(small_all, name="sum_small_grads")

    params = {"w_in": (w_in, m_w_in, v_w_in), "w_uq": (w_uq, m_w_uq, v_w_uq), "w_ukv": (w_ukv, m_w_ukv, v_w_ukv),
              "w_out": (w_out, m_w_out, v_w_out), "w_up": (w_up, m_w_up, v_w_up), "conv_w": (conv_w, m_conv_w, v_conv_w),
              "w_down": (w_down, m_w_down, v_w_down)}
    delta, new_m, new_v = {}, {}, {}
    for n, (w_, m_, v_) in params.items():
        d_, mn_, vn_ = _adamw(w_[0], grad[n], m_[0], v_[0], name="adamw_" + n)
        delta[n], new_m[n], new_v[n] = d_[None], mn_[None], vn_[None]
        grad[n] = grad[n][None]
    small_w = {"g_mix": g_mix, "g_cq": g_cq, "g_ckv": g_ckv, "g_sb_out": g_sb_out, "g_mla_out": g_mla_out,
               "g_ffn": g_ffn, "conv_b": conv_b, "g_final": g_final}
    small_m = {"g_mix": m_g_mix, "g_cq": m_g_cq, "g_ckv": m_g_ckv, "g_sb_out": m_g_sb_out, "g_mla_out": m_g_mla_out,
               "g_ffn": m_g_ffn, "conv_b": m_conv_b, "g_final": m_g_final}
    small_v = {"g_mix": v_g_mix, "g_cq": v_g_cq, "g_ckv": v_g_ckv, "g_sb_out": v_g_sb_out, "g_mla_out": v_g_mla_out,
               "g_ffn": v_g_ffn, "conv_b": v_conv_b, "g_final": v_g_final}
    ds_, ms_, vs_ = _adamw(_pack_small(small_w), gsmall, _pack_small(small_m), _pack_small(small_v), name="adamw_small")
    for src, dst in ((_unpack_small(gsmall), grad), (_unpack_small(ds_), delta), (_unpack_small(ms_), new_m), (_unpack_small(vs_), new_v)):
        for n, _ in SMALL:
            dst[n] = src[n].reshape(small_w[n].shape)

    loss = lax.psum(loss_row[0, 0], MESH_AXES)
    order = ("g_mix", "w_in", "g_cq", "w_uq", "g_ckv", "w_ukv", "g_sb_out", "g_mla_out", "w_out", "g_ffn", "w_up",
             "conv_w", "conv_b", "w_down", "g_final")
    return (loss, dx.reshape(B, S, D), *[grad[n] for n in order], *[delta[n] for n in order],
            *[new_m[n] for n in order], *[new_v[n] for n in order])
```

```python
import jax
import jax.numpy as jnp
from jax import lax
from jax.experimental import pallas as pl
from jax.experimental.pallas import tpu as pltpu

F32 = jnp.float32
BF16 = jnp.bfloat16

D_MODEL = 1024
SB_HEADS = 8
SB_HEAD_DIM = 64
MLA_HEADS = 8
MLA_NOPE = 64
MLA_ROPE = 32
MLA_V = 64
Q_LORA = 384
KV_LORA = 256
D_FF = 2816
ROPE_BASE = 10000.0
EPS = 1e-6
SB_W = SB_HEADS * SB_HEAD_DIM
MLA_W = MLA_HEADS * MLA_V
MLA_QK = MLA_NOPE + MLA_ROPE
IN_COLS = 3 * SB_W + Q_LORA + KV_LORA + MLA_ROPE

ADAM_LR = 0.001
ADAM_B1 = 0.9
ADAM_B2 = 0.999
ADAM_EPS = 1e-08
ADAM_WD = 0.01
ADAM_STEP = 10

N_DEV = 8
MESH_AXES = ("x", "y", "c")
LANES = 128
V7X_VMEM_LIMIT = 56 * 1024 * 1024
FF_BLK = 256
N_FF_BLK = D_FF // FF_BLK

P_Q, P_K, P_V = 0, SB_W, 2 * SB_W
P_CKV = 3 * SB_W
P_KRT = P_CKV + KV_LORA
P_CQ = P_KRT + LANES
P_COLS = P_CQ + Q_LORA

MESH = pl.DeviceIdType.MESH
ANY = pl.BlockSpec(memory_space=pl.ANY)


def _cparams(sem=None, vmem=V7X_VMEM_LIMIT):
    return pltpu.CompilerParams(dimension_semantics=sem, vmem_limit_bytes=vmem)


def _matmul_nn(a, b, *, tm, tn, out_dtype, name, residual=None):
    M, K = a.shape
    N = b.shape[1]
    in_specs = [pl.BlockSpec((tm, K), lambda i, j: (i, 0)), pl.BlockSpec((K, tn), lambda i, j: (0, j))]
    args = [a, b]
    if residual is not None:
        in_specs.append(pl.BlockSpec((tm, tn), lambda i, j: (i, j)))
        args.append(residual)

    def body(*refs):
        a_ref, b_ref = refs[0], refs[1]
        o_ref = refs[-1]
        acc = jnp.dot(a_ref[...].astype(BF16), b_ref[...], preferred_element_type=F32)
        if residual is not None:
            acc = acc + refs[2][...]
        o_ref[...] = acc.astype(out_dtype)

    return pl.pallas_call(
        body, name=name, grid=(M // tm, N // tn), in_specs=in_specs,
        out_specs=pl.BlockSpec((tm, tn), lambda i, j: (i, j)),
        out_shape=jax.ShapeDtypeStruct((M, N), out_dtype),
        compiler_params=_cparams(("parallel", "parallel")),
    )(*args)


def _matmul_nt(a, b, *, tm, tn, out_dtype, name):
    M, K = a.shape
    N = b.shape[0]

    def body(a_ref, b_ref, o_ref):
        acc = lax.dot_general(a_ref[...].astype(BF16), b_ref[...], (((1,), (1,)), ((), ())),
                              preferred_element_type=F32)
        o_ref[...] = acc.astype(out_dtype)

    return pl.pallas_call(
        body, name=name, grid=(M // tm, N // tn),
        in_specs=[pl.BlockSpec((tm, K), lambda i, j: (i, 0)), pl.BlockSpec((tn, K), lambda i, j: (j, 0))],
        out_specs=pl.BlockSpec((tm, tn), lambda i, j: (i, j)),
        out_shape=jax.ShapeDtypeStruct((M, N), out_dtype),
        compiler_params=_cparams(("parallel", "parallel")),
    )(a, b)


def _matmul_tn(a, b, *, tm, tn, tk, name):
    K, M = a.shape
    N = b.shape[1]

    def body(a_ref, b_ref, o_ref):
        k = pl.program_id(2)
        part = lax.dot_general(a_ref[...].astype(BF16), b_ref[...].astype(BF16), (((0,), (0,)), ((), ())),
                               preferred_element_type=F32)

        @pl.when(k == 0)
        def _():
            o_ref[...] = part

        @pl.when(k > 0)
        def _():
            o_ref[...] += part

    return pl.pallas_call(
        body, name=name, grid=(M // tm, N // tn, K // tk),
        in_specs=[pl.BlockSpec((tk, tm), lambda i, j, k: (k, i)), pl.BlockSpec((tk, tn), lambda i, j, k: (k, j))],
        out_specs=pl.BlockSpec((tm, tn), lambda i, j, k: (i, j)),
        out_shape=jax.ShapeDtypeStruct((M, N), F32),
        compiler_params=_cparams(("parallel", "parallel", "arbitrary")),
    )(a, b)


def _rms(xf, g):
    r = lax.rsqrt(jnp.mean(xf * xf, axis=1, keepdims=True) + EPS)
    return (xf * r) * g


def _rms_grad(dyf, xf, g):
    r = lax.rsqrt(jnp.mean(xf * xf, axis=1, keepdims=True) + EPS)
    xh = xf * r
    dyg = dyf * g
    dx = r * (dyg - xh * jnp.mean(dyg * xh, axis=1, keepdims=True))
    return dx, jnp.sum(dyf * xh, axis=0, keepdims=True)


def _accumulate(ref, part):
    @pl.when(pl.program_id(0) == 0)
    def _():
        ref[...] = part

    @pl.when(pl.program_id(0) > 0)
    def _():
        ref[...] += part


def _rms_fwd(x, g, *, tm, name, col_block=0):
    T = x.shape[0]
    C = g.shape[1]

    def body(x_ref, g_ref, o_ref):
        o_ref[...] = _rms(x_ref[...], g_ref[...]).astype(BF16)

    return pl.pallas_call(
        body, name=name, grid=(T // tm,),
        in_specs=[pl.BlockSpec((tm, C), lambda i: (i, col_block)), pl.BlockSpec((1, C), lambda i: (0, 0))],
        out_specs=pl.BlockSpec((tm, C), lambda i: (i, 0)),
        out_shape=jax.ShapeDtypeStruct((T, C), BF16),
        compiler_params=_cparams(("parallel",)),
    )(x, g)


def _rms_bwd(dy, x, g, *, tm, name, residual=None, col_block=0, out_dtype=F32):
    T = dy.shape[0]
    C = g.shape[1]
    in_specs = [pl.BlockSpec((tm, C), lambda i: (i, 0)), pl.BlockSpec((tm, C), lambda i: (i, col_block)),
                pl.BlockSpec((1, C), lambda i: (0, 0))]
    args = [dy, x, g]
    if residual is not None:
        in_specs.append(pl.BlockSpec((tm, C), lambda i: (i, 0)))
        args.append(residual)

    def body(*refs):
        dy_ref, x_ref, g_ref = refs[:3]
        dx_ref, dg_ref = refs[-2:]
        dx, part = _rms_grad(dy_ref[...].astype(F32), x_ref[...], g_ref[...])
        if residual is not None:
            dx = dx + refs[3][...]
        dx_ref[...] = dx.astype(out_dtype)
        _accumulate(dg_ref, part)

    return pl.pallas_call(
        body, name=name, grid=(T // tm,), in_specs=in_specs,
        out_specs=[pl.BlockSpec((tm, C), lambda i: (i, 0)), pl.BlockSpec((1, C), lambda i: (0, 0))],
        out_shape=[jax.ShapeDtypeStruct((T, C), out_dtype), jax.ShapeDtypeStruct((1, C), F32)],
        compiler_params=_cparams(("arbitrary",)),
    )(*args)


def _rms2_fwd(xa, xb, ga, gb, *, tm, name):
    T, C = xa.shape

    def body(xa_ref, xb_ref, ga_ref, gb_ref, o_ref):
        o_ref[:, :C] = _rms(xa_ref[...], ga_ref[...]).astype(BF16)
        o_ref[:, C:] = _rms(xb_ref[...], gb_ref[...]).astype(BF16)

    row = pl.BlockSpec((tm, C), lambda i: (i, 0))
    gsp = pl.BlockSpec((1, C), lambda i: (0, 0))
    return pl.pallas_call(
        body, name=name, grid=(T // tm,), in_specs=[row, row, gsp, gsp],
        out_specs=pl.BlockSpec((tm, 2 * C), lambda i: (i, 0)),
        out_shape=jax.ShapeDtypeStruct((T, 2 * C), BF16),
        compiler_params=_cparams(("parallel",)),
    )(xa, xb, ga, gb)


def _rms2_bwd(dy, xa, xb, ga, gb, *, tm, name):
    T, C = xa.shape

    def body(dy_ref, xa_ref, xb_ref, ga_ref, gb_ref, dxa_ref, dxb_ref, dga_ref, dgb_ref):
        dxa, pa = _rms_grad(dy_ref[:, :C], xa_ref[...], ga_ref[...])
        dxb, pb = _rms_grad(dy_ref[:, C:], xb_ref[...], gb_ref[...])
        dxa_ref[...] = dxa
        dxb_ref[...] = dxb
        _accumulate(dga_ref, pa)
        _accumulate(dgb_ref, pb)

    row = pl.BlockSpec((tm, C), lambda i: (i, 0))
    gsp = pl.BlockSpec((1, C), lambda i: (0, 0))
    return pl.pallas_call(
        body, name=name, grid=(T // tm,),
        in_specs=[pl.BlockSpec((tm, 2 * C), lambda i: (i, 0)), row, row, gsp, gsp],
        out_specs=[row, row, gsp, gsp],
        out_shape=[jax.ShapeDtypeStruct((T, C), F32), jax.ShapeDtypeStruct((T, C), F32),
                   jax.ShapeDtypeStruct((1, C), F32), jax.ShapeDtypeStruct((1, C), F32)],
        compiler_params=_cparams(("arbitrary",)),
    )(dy, xa, xb, ga, gb)


def _final_loss(x2, g, tgt, *, tm, name):
    T, C = x2.shape

    def body(x_ref, g_ref, t_ref, dx_ref, dg_ref, loss_ref):
        xf = x_ref[...]
        gf = g_ref[...]
        err = _rms(xf, gf) - t_ref[...]
        lpart = 0.5 * jnp.sum(jnp.mean(err * err, axis=1, keepdims=True), axis=0, keepdims=True)
        dx, gpart = _rms_grad(err * (1.0 / C), xf, gf)
        dx_ref[...] = dx
        _accumulate(dg_ref, gpart)
        _accumulate(loss_ref, jnp.broadcast_to(lpart, (1, LANES)))

    return pl.pallas_call(
        body, name=name, grid=(T // tm,),
        in_specs=[pl.BlockSpec((tm, C), lambda i: (i, 0)), pl.BlockSpec((1, C), lambda i: (0, 0)),
                  pl.BlockSpec((tm, C), lambda i: (i, 0))],
        out_specs=[pl.BlockSpec((tm, C), lambda i: (i, 0)), pl.BlockSpec((1, C), lambda i: (0, 0)),
                   pl.BlockSpec((1, LANES), lambda i: (0, 0))],
        out_shape=[jax.ShapeDtypeStruct((T, C), F32), jax.ShapeDtypeStruct((1, C), F32),
                   jax.ShapeDtypeStruct((1, LANES), F32)],
        compiler_params=_cparams(("arbitrary",)),
    )(x2, g, tgt)


ATT_T = 256
ATT_PAIRS = 2
NEG_BIG = -1e30


def _lane_iota():
    return lax.broadcasted_iota(jnp.int32, (1, LANES), 1)


def _head_masks():
    first = _lane_iota() < SB_HEAD_DIM
    return first, jnp.logical_not(first)


def _pick(mask, x):
    return jnp.where(mask, x, jnp.zeros_like(x))


def _lane_value(t, lane):
    return jnp.sum(jnp.where(_lane_iota() == lane, t, 0.0), axis=1, keepdims=True)


def _split_hi_lo(x):
    hi = x.astype(BF16)
    lo = (x - hi.astype(F32)).astype(BF16)
    return jnp.concatenate([hi, lo], axis=1)


def _tri(n, kind):
    r = lax.broadcasted_iota(jnp.int32, (n, n), 0)
    c = lax.broadcasted_iota(jnp.int32, (n, n), 1)
    u = {"suffix_excl": r > c, "prefix_incl": r <= c, "prefix_excl": r < c}[kind].astype(BF16)
    return jnp.concatenate([u, u], axis=0)


def _dot_nt(a, b):
    return lax.dot_general(a, b, (((1,), (1,)), ((), ())), preferred_element_type=F32)


def _dot_tn(a, b):
    return lax.dot_general(a, b, (((0,), (0,)), ((), ())), preferred_element_type=F32)


def _dot(a, b):
    return jnp.dot(a, b, preferred_element_type=F32)


def _causal_mask(n, strict):
    r = lax.broadcasted_iota(jnp.int32, (n, n), 0)
    c = lax.broadcasted_iota(jnp.int32, (n, n), 1)
    return (c < r) if strict else (c <= r)


def _sb_logs(qh, kj, vis):
    z = _dot_nt(qh, kj)
    sp = jnp.log(1.0 + jnp.exp(-jnp.abs(z)))
    lb = jnp.minimum(z, 0.0) - sp
    lk = jnp.minimum(-z, 0.0) - sp
    if vis is not None:
        lk = jnp.where(vis, lk, 0.0)
    return lb, lk


def _sb_fwd(p, *, seq, name):
    T = p.shape[0]
    B = T // seq
    TQ = ATT_T
    nq = seq // TQ
    PP = ATT_PAIRS
    W = PP * LANES
    nstep = SB_W // W
    NH = 2 * PP

    def body(q_ref, k_ref, v_ref, o_ref, lt_ref, q_s, k_s, v_s):
        masks = _head_masks()
        q = q_ref[...] * (SB_HEAD_DIM ** -0.5)
        v = v_ref[...]
        k_s[...] = k_ref[...].astype(BF16)
        for h in range(NH):
            ps = slice((h // 2) * LANES, (h // 2 + 1) * LANES)
            hs = slice(h * LANES, (h + 1) * LANES)
            q_s[:, hs] = _pick(masks[h % 2], q[:, ps]).astype(BF16)
            v_s[:, hs] = _pick(masks[h % 2], v[:, ps]).astype(BF16)
        u_suf = _tri(TQ, "suffix_excl")
        vis = _causal_mask(TQ, True)

        def q_block(i, carry):
            q0 = pl.multiple_of(i * TQ, TQ)
            qs = [q_s[pl.ds(q0, TQ), h * LANES:(h + 1) * LANES] for h in range(NH)]

            def head(qh, kj, vj, r_run, mask):
                lb, lk = _sb_logs(qh, kj, mask)
                a = jnp.exp(lb + _dot(_split_hi_lo(lk), u_suf) + r_run)
                if mask is not None:
                    a = jnp.where(mask, a, 0.0)
                return _dot(a.astype(BF16), vj), r_run + jnp.sum(lk, axis=1, keepdims=True)

            def tile(k0, c, mask):
                rs, accs = list(c[:NH]), list(c[NH:])
                logs = [_sb_logs(qs[h], k_s[pl.ds(k0, TQ), (h // 2) * LANES:(h // 2 + 1) * LANES], mask) for h in range(NH)]
                sums = [_dot(_split_hi_lo(lk), u_suf) for _, lk in logs]
                for h in range(NH):
                    a = jnp.exp(logs[h][0] + sums[h] + rs[h])
                    if mask is not None:
                        a = jnp.where(mask, a, 0.0)
                    accs[h // 2] = accs[h // 2] + _dot(a.astype(BF16), v_s[pl.ds(k0, TQ), h * LANES:(h + 1) * LANES])
                    rs[h] = rs[h] + jnp.sum(logs[h][1], axis=1, keepdims=True)
                return tuple(rs) + tuple(accs)

            zero = jnp.zeros((TQ, 1), F32)
            c = tile(q0, (zero,) * NH + (jnp.zeros((TQ, LANES), F32),) * PP, vis)

            def k_block(jj, c):
                return tile(pl.multiple_of((i - 1 - jj) * TQ, TQ), c, None)

            c = lax.fori_loop(0, i, k_block, c)
            for pr in range(PP):
                ps = slice(pr * LANES, (pr + 1) * LANES)
                o_ref[pl.ds(q0, TQ), ps] = c[NH + pr]
                lt_ref[pl.ds(q0, TQ), ps] = jnp.where(masks[0], c[2 * pr], c[2 * pr + 1])
            return carry

        lax.fori_loop(0, nq, q_block, 0)

    blk = lambda off: pl.BlockSpec((seq, W), lambda b, g: (b, off + g))
    out_blk = pl.BlockSpec((seq, W), lambda b, g: (b, g))
    return pl.pallas_call(
        body, name=name, grid=(B, nstep),
        in_specs=[blk(P_Q // W), blk(P_K // W), blk(P_V // W)],
        out_specs=[out_blk, out_blk],
        out_shape=[jax.ShapeDtypeStruct((T, SB_W), F32), jax.ShapeDtypeStruct((T, SB_W), F32)],
        scratch_shapes=[pltpu.VMEM((seq, NH * LANES), BF16), pltpu.VMEM((seq, W), BF16), pltpu.VMEM((seq, NH * LANES), BF16)],
        compiler_params=_cparams(("parallel", "parallel")),
    )(p, p, p)


def _sb_bwd(p, ltot, do, *, seq, name):
    T = p.shape[0]
    B = T // seq
    TQ = ATT_T
    nq = seq // TQ
    PP = ATT_PAIRS
    W = PP * LANES
    nstep = SB_W // W
    NH = 2 * PP
    scale = SB_HEAD_DIM ** -0.5

    def body(q_ref, k_ref, v_ref, lt_ref, do_ref, dq_ref, dk_ref, dv_ref, q_s, k_s, v_s, do_s, dk_s, dv_s):
        masks = _head_masks()
        q = q_ref[...] * scale
        dof = do_ref[...]
        k_s[...] = k_ref[...].astype(BF16)
        v_s[...] = v_ref[...].astype(BF16)
        for h in range(NH):
            ps = slice((h // 2) * LANES, (h // 2 + 1) * LANES)
            hs = slice(h * LANES, (h + 1) * LANES)
            q_s[:, hs] = _pick(masks[h % 2], q[:, ps]).astype(BF16)
            do_s[:, hs] = _pick(masks[h % 2], dof[:, ps]).astype(BF16)
        dk_s[...] = jnp.zeros_like(dk_s)
        dv_s[...] = jnp.zeros_like(dv_s)
        u_pin = _tri(TQ, "prefix_incl")
        u_pex = _tri(TQ, "prefix_excl")
        vis = _causal_mask(TQ, True)

        def q_block(i, carry):
            q0 = pl.multiple_of(i * TQ, TQ)
            qs = [q_s[pl.ds(q0, TQ), h * LANES:(h + 1) * LANES] for h in range(NH)]
            dos = [do_s[pl.ds(q0, TQ), h * LANES:(h + 1) * LANES] for h in range(NH)]
            lt = lt_ref[pl.ds(q0, TQ), :]
            lts = [_lane_value(lt[:, (h // 2) * LANES:(h // 2 + 1) * LANES], (h % 2) * SB_HEAD_DIM) for h in range(NH)]

            def tile(k0, c, mask):
                cs, gs, accs = list(c[:NH]), list(c[NH:2 * NH]), list(c[2 * NH:])
                kjs = [k_s[pl.ds(k0, TQ), pr * LANES:(pr + 1) * LANES] for pr in range(PP)]
                vjs = [v_s[pl.ds(k0, TQ), pr * LANES:(pr + 1) * LANES] for pr in range(PP)]
                logs = [_sb_logs(qs[h], kjs[h // 2], mask) for h in range(NH)]
                pins = [_dot(_split_hi_lo(lk), u_pin) for _, lk in logs]
                das = [_dot_nt(dos[h], vjs[h // 2]) for h in range(NH)]
                a_l, g_l = [], []
                for h in range(NH):
                    a = jnp.exp(logs[h][0] + ((lts[h] - cs[h]) - pins[h]))
                    if mask is not None:
                        a = jnp.where(mask, a, 0.0)
                    a_l.append(a)
                    g_l.append(das[h] * a)
                pres = [_dot(_split_hi_lo(g), u_pex) for g in g_l]
                dz_l = []
                for h in range(NH):
                    dz = g_l[h] - jnp.exp(logs[h][0]) * (g_l[h] + (pres[h] + gs[h]))
                    if mask is not None:
                        dz = jnp.where(mask, dz, 0.0)
                    dz_l.append(dz.astype(BF16))
                for h in range(NH):
                    accs[h] = accs[h] + _dot(dz_l[h], kjs[h // 2])
                for pr in range(PP):
                    ps = slice(pr * LANES, (pr + 1) * LANES)
                    ha, hb = 2 * pr, 2 * pr + 1
                    dk_s[pl.ds(k0, TQ), ps] += _dot_tn(dz_l[ha], qs[ha]) + _dot_tn(dz_l[hb], qs[hb])
                    dv_s[pl.ds(k0, TQ), ps] += _dot_tn(a_l[ha].astype(BF16), dos[ha]) + _dot_tn(a_l[hb].astype(BF16), dos[hb])
                for h in range(NH):
                    cs[h] = cs[h] + jnp.sum(logs[h][1], axis=1, keepdims=True)
                    gs[h] = gs[h] + jnp.sum(g_l[h], axis=1, keepdims=True)
                return tuple(cs) + tuple(gs) + tuple(accs)

            z1 = jnp.zeros((TQ, 1), F32)
            zl = jnp.zeros((TQ, LANES), F32)

            def k_block(j, c):
                return tile(pl.multiple_of(j * TQ, TQ), c, None)

            c = lax.fori_loop(0, i, k_block, (z1,) * (2 * NH) + (zl,) * NH)
            c = tile(q0, c, vis)
            for pr in range(PP):
                dq = jnp.where(masks[0], c[2 * NH + 2 * pr], c[2 * NH + 2 * pr + 1]) * scale
                dq_ref[pl.ds(q0, TQ), pr * LANES:(pr + 1) * LANES] = dq.astype(BF16)
            return carry

        lax.fori_loop(0, nq, q_block, 0)
        dk_ref[...] = dk_s[...].astype(BF16)
        dv_ref[...] = dv_s[...].astype(BF16)

    blk = lambda off: pl.BlockSpec((seq, W), lambda b, g: (b, off + g))
    out_blk = pl.BlockSpec((seq, W), lambda b, g: (b, g))
    return pl.pallas_call(
        body, name=name, grid=(B, nstep),
        in_specs=[blk(P_Q // W), blk(P_K // W), blk(P_V // W), out_blk, out_blk],
        out_specs=[out_blk, out_blk, out_blk],
        out_shape=[jax.ShapeDtypeStruct((T, SB_W), BF16) for _ in range(3)],
        scratch_shapes=[pltpu.VMEM((seq, NH * LANES), BF16), pltpu.VMEM((seq, W), BF16), pltpu.VMEM((seq, W), BF16),
                        pltpu.VMEM((seq, NH * LANES), BF16), pltpu.VMEM((seq, W), F32), pltpu.VMEM((seq, W), F32)],
        compiler_params=_cparams(("parallel", "parallel")),
    )(p, p, p, ltot, do)


def _mla_masks():
    lane = lax.broadcasted_iota(jnp.int32, (1, 2 * LANES), 1)
    ma = (lane < MLA_NOPE) | ((lane >= LANES) & (lane < LANES + MLA_ROPE))
    mb = ((lane >= MLA_NOPE) & (lane < LANES)) | ((lane >= LANES + MLA_ROPE) & (lane < LANES + 2 * MLA_ROPE))
    return ma, mb


def _mla_fwd(qm, kvm, krt, *, seq, name):
    T = qm.shape[0]
    B = T // seq
    TQ = ATT_T
    nq = seq // TQ
    PP = ATT_PAIRS
    W = PP * LANES
    nstep = MLA_W // W
    NH = 2 * PP
    CW = 2 * LANES
    scale = MLA_QK ** -0.5

    def body(qn_ref, qr_ref, kn_ref, v_ref, kr_ref, o_ref, lse_ref, q_s, kc_s, v_s):
        hm = _head_masks()
        mm = _mla_masks()
        v = v_ref[...]
        for pr in range(PP):
            ps = slice(pr * LANES, (pr + 1) * LANES)
            qc = jnp.concatenate([qn_ref[:, ps], qr_ref[:, ps]], axis=1)
            kc_s[:, pr * CW:(pr + 1) * CW] = jnp.concatenate([kn_ref[:, ps], kr_ref[...]], axis=1)
            for e in range(2):
                h = 2 * pr + e
                q_s[:, h * CW:(h + 1) * CW] = _pick(mm[e], qc)
                v_s[:, h * LANES:(h + 1) * LANES] = _pick(hm[e], v[:, ps])
        vis = _causal_mask(TQ, False)

        def q_block(i, carry):
            q0 = pl.multiple_of(i * TQ, TQ)
            qs = [q_s[pl.ds(q0, TQ), h * CW:(h + 1) * CW] for h in range(NH)]

            def tile(k0, c, mask):
                ms, ls, accs = list(c[:NH]), list(c[NH:2 * NH]), list(c[2 * NH:])
                ss = [_dot_nt(qs[h], kc_s[pl.ds(k0, TQ), (h // 2) * CW:(h // 2 + 1) * CW]) * scale for h in range(NH)]
                if mask is not None:
                    ss = [jnp.where(mask, s, NEG_BIG) for s in ss]
                m_new = [jnp.maximum(ms[h], jnp.max(ss[h], axis=1, keepdims=True)) for h in range(NH)]
                alphas = [jnp.exp(ms[h] - m_new[h]) for h in range(NH)]
                prs = [jnp.exp(ss[h] - m_new[h]) for h in range(NH)]
                outs = [_dot(prs[h].astype(BF16), v_s[pl.ds(k0, TQ), h * LANES:(h + 1) * LANES]) for h in range(NH)]
                ls = [alphas[h] * ls[h] + jnp.sum(prs[h], axis=1, keepdims=True) for h in range(NH)]
                for pr in range(PP):
                    accs[pr] = accs[pr] * jnp.where(hm[0], alphas[2 * pr], alphas[2 * pr + 1]) + outs[2 * pr] + outs[2 * pr + 1]
                return tuple(m_new) + tuple(ls) + tuple(accs)

            neg = jnp.full((TQ, 1), NEG_BIG, F32)
            z1 = jnp.zeros((TQ, 1), F32)

            def k_block(j, c):
                return tile(pl.multiple_of(j * TQ, TQ), c, None)

            c = lax.fori_loop(0, i, k_block, (neg,) * NH + (z1,) * NH + (jnp.zeros((TQ, LANES), F32),) * PP)
            c = tile(q0, c, vis)
            for pr in range(PP):
                ps = slice(pr * LANES, (pr + 1) * LANES)
                m_a, m_b, l_a, l_b = c[2 * pr], c[2 * pr + 1], c[NH + 2 * pr], c[NH + 2 * pr + 1]
                o_ref[pl.ds(q0, TQ), ps] = c[2 * NH + pr] / jnp.where(hm[0], l_a, l_b)
                lse_ref[pl.ds(q0, TQ), ps] = jnp.where(hm[0], m_a + jnp.log(l_a), m_b + jnp.log(l_b))
            return carry

        lax.fori_loop(0, nq, q_block, 0)

    blk = lambda off: pl.BlockSpec((seq, W), lambda b, g: (b, off + g))
    out_blk = pl.BlockSpec((seq, W), lambda b, g: (b, g))
    return pl.pallas_call(
        body, name=name, grid=(B, nstep),
        in_specs=[blk(0), blk(nstep), blk(0), blk(nstep), pl.BlockSpec((seq, LANES), lambda b, g: (b, 0))],
        out_specs=[out_blk, out_blk],
        out_shape=[jax.ShapeDtypeStruct((T, MLA_W), F32), jax.ShapeDtypeStruct((T, MLA_W), F32)],
        scratch_shapes=[pltpu.VMEM((seq, NH * CW), BF16), pltpu.VMEM((seq, PP * CW), BF16), pltpu.VMEM((seq, NH * LANES), BF16)],
        compiler_params=_cparams(("parallel", "parallel")),
    )(qm, qm, kvm, kvm, krt)


def _mla_bwd(qm, kvm, krt, o, lse, do, *, seq, name):
    T = qm.shape[0]
    B = T // seq
    TQ = ATT_T
    nq = seq // TQ
    PP = ATT_PAIRS
    W = PP * LANES
    nstep = MLA_W // W
    NH = 2 * PP
    CW = 2 * LANES
    scale = MLA_QK ** -0.5

    def body(qn_ref, qr_ref, kn_ref, v_ref, kr_ref, o_ref, lse_ref, do_ref,
             dqn_ref, dqr_ref, dkn_ref, dv_ref, dkr_ref, q_s, kc_s, do_s, dkc_s, dv_s):
        hm = _head_masks()
        mm = _mla_masks()
        dof = do_ref[...]
        for pr in range(PP):
            ps = slice(pr * LANES, (pr + 1) * LANES)
            qc = jnp.concatenate([qn_ref[:, ps], qr_ref[:, ps]], axis=1)
            kc_s[:, pr * CW:(pr + 1) * CW] = jnp.concatenate([kn_ref[:, ps], kr_ref[...]], axis=1)
            for e in range(2):
                h = 2 * pr + e
                q_s[:, h * CW:(h + 1) * CW] = _pick(mm[e], qc)
                do_s[:, h * LANES:(h + 1) * LANES] = _pick(hm[e], dof[:, ps]).astype(BF16)
        dkc_s[...] = jnp.zeros_like(dkc_s)
        dv_s[...] = jnp.zeros_like(dv_s)
        vis = _causal_mask(TQ, False)

        def q_block(i, carry):
            q0 = pl.multiple_of(i * TQ, TQ)
            qs = [q_s[pl.ds(q0, TQ), h * CW:(h + 1) * CW] for h in range(NH)]
            dos = [do_s[pl.ds(q0, TQ), h * LANES:(h + 1) * LANES] for h in range(NH)]
            lse_t = lse_ref[pl.ds(q0, TQ), :]
            dd = do_ref[pl.ds(q0, TQ), :] * o_ref[pl.ds(q0, TQ), :]
            lses, ds_ = [], []
            for h in range(NH):
                ps = slice((h // 2) * LANES, (h // 2 + 1) * LANES)
                lses.append(_lane_value(lse_t[:, ps], (h % 2) * MLA_V))
                ds_.append(jnp.sum(_pick(hm[h % 2], dd[:, ps]), axis=1, keepdims=True))

            def tile(k0, c, mask):
                accs = list(c)
                kcs = [kc_s[pl.ds(k0, TQ), pr * CW:(pr + 1) * CW] for pr in range(PP)]
                vjs = [v_ref[pl.ds(k0, TQ), pr * LANES:(pr + 1) * LANES] for pr in range(PP)]
                ss = [_dot_nt(qs[h], kcs[h // 2]) * scale for h in range(NH)]
                dps = [_dot_nt(dos[h], vjs[h // 2]) for h in range(NH)]
                p_l, ds_l = [], []
                for h in range(NH):
                    pr_ = jnp.exp(ss[h] - lses[h])
                    if mask is not None:
                        pr_ = jnp.where(mask, pr_, 0.0)
                    p_l.append(pr_.astype(BF16))
                    ds_l.append((pr_ * (dps[h] - ds_[h]) * scale).astype(BF16))
                for h in range(NH):
                    accs[h] = accs[h] + _dot(ds_l[h], kcs[h // 2])
                for pr in range(PP):
                    ha, hb = 2 * pr, 2 * pr + 1
                    dkc_s[pl.ds(k0, TQ), pr * CW:(pr + 1) * CW] += _dot_tn(ds_l[ha], qs[ha]) + _dot_tn(ds_l[hb], qs[hb])
                    dv_s[pl.ds(k0, TQ), pr * LANES:(pr + 1) * LANES] += _dot_tn(p_l[ha], dos[ha]) + _dot_tn(p_l[hb], dos[hb])
                return tuple(accs)

            zc = jnp.zeros((TQ, CW), F32)

            def k_block(j, c):
                return tile(pl.multiple_of(j * TQ, TQ), c, None)

            c = lax.fori_loop(0, i, k_block, (zc,) * NH)
            c = tile(q0, c, vis)
            for pr in range(PP):
                ps = slice(pr * LANES, (pr + 1) * LANES)
                dq = _pick(mm[0], c[2 * pr]) + _pick(mm[1], c[2 * pr + 1])
                dqn_ref[pl.ds(q0, TQ), ps] = dq[:, :LANES].astype(BF16)
                dqr_ref[pl.ds(q0, TQ), ps] = dq[:, LANES:]
            return carry

        lax.fori_loop(0, nq, q_block, 0)
        dkr = dkc_s[:, LANES:CW]
        for pr in range(PP):
            dkn_ref[:, pr * LANES:(pr + 1) * LANES] = dkc_s[:, pr * CW:pr * CW + LANES].astype(BF16)
            if pr > 0:
                dkr = dkr + dkc_s[:, pr * CW + LANES:(pr + 1) * CW]
        dv_ref[...] = dv_s[...].astype(BF16)
        g = pl.program_id(1)

        @pl.when(g == 0)
        def _():
            dkr_ref[...] = dkr

        @pl.when(g > 0)
        def _():
            dkr_ref[...] += dkr

    blk = lambda off: pl.BlockSpec((seq, W), lambda b, g: (b, off + g))
    out_blk = pl.BlockSpec((seq, W), lambda b, g: (b, g))
    one_blk = pl.BlockSpec((seq, LANES), lambda b, g: (b, 0))
    return pl.pallas_call(
        body, name=name, grid=(B, nstep),
        in_specs=[blk(0), blk(nstep), blk(0), blk(nstep), one_blk, out_blk, out_blk, out_blk],
        out_specs=[out_blk, out_blk, out_blk, out_blk, one_blk],
        out_shape=[jax.ShapeDtypeStruct((T, MLA_W), BF16), jax.ShapeDtypeStruct((T, MLA_W), F32),
                   jax.ShapeDtypeStruct((T, MLA_W), BF16), jax.ShapeDtypeStruct((T, MLA_W), BF16),
                   jax.ShapeDtypeStruct((T, LANES), F32)],
        scratch_shapes=[pltpu.VMEM((seq, NH * CW), BF16), pltpu.VMEM((seq, PP * CW), BF16), pltpu.VMEM((seq, NH * LANES), BF16),
                        pltpu.VMEM((seq, PP * CW), F32), pltpu.VMEM((seq, W), F32)],
        compiler_params=_cparams(("parallel", "arbitrary")),
    )(qm, qm, kvm, kvm, krt, o, lse, do)


def _rope_tables(pos_ref, invf_ref):
    ang = pos_ref[...].astype(F32) * invf_ref[...]
    first = (_lane_iota() % MLA_ROPE) < (MLA_ROPE // 2)
    return jnp.cos(ang), jnp.sin(ang), first


def _rope_apply(x, cos, sin, first):
    rot = jnp.where(first, -pltpu.roll(x, LANES - MLA_ROPE // 2, 1), pltpu.roll(x, MLA_ROPE // 2, 1))
    return x * cos + rot * sin


def _rope_apply_t(dy, cos, sin, first):
    dys = dy * sin
    rot_t = jnp.where(first, pltpu.roll(dys, LANES - MLA_ROPE // 2, 1), -pltpu.roll(dys, MLA_ROPE // 2, 1))
    return dy * cos + rot_t


def _rope_fwd(qfull, p, pos, invf, *, tm, name):
    T = qfull.shape[0]
    ntile = MLA_W // LANES

    def body(q_ref, kr_ref, pos_ref, invf_ref, qm_ref, krt_ref):
        cos, sin, first = _rope_tables(pos_ref, invf_ref)
        qm_ref[:, :MLA_W] = q_ref[:, :MLA_W].astype(BF16)
        for t in range(ntile):
            sl = slice(MLA_W + t * LANES, MLA_W + (t + 1) * LANES)
            qm_ref[:, sl] = _rope_apply(q_ref[:, sl], cos, sin, first).astype(BF16)
        krt_ref[...] = _rope_apply(kr_ref[...], cos, sin, first).astype(BF16)

    return pl.pallas_call(
        body, name=name, grid=(T // tm,),
        in_specs=[pl.BlockSpec((tm, 2 * MLA_W), lambda i: (i, 0)), pl.BlockSpec((tm, LANES), lambda i: (i, P_KRT // LANES)),
                  pl.BlockSpec((tm, 1), lambda i: (i, 0)), pl.BlockSpec((1, LANES), lambda i: (0, 0))],
        out_specs=[pl.BlockSpec((tm, 2 * MLA_W), lambda i: (i, 0)), pl.BlockSpec((tm, LANES), lambda i: (i, 0))],
        out_shape=[jax.ShapeDtypeStruct((T, 2 * MLA_W), BF16), jax.ShapeDtypeStruct((T, LANES), BF16)],
        compiler_params=_cparams(("parallel",)),
    )(qfull, p, pos, invf)


def _rope_bwd(dqr, dkr, pos, invf, *, tm, name):
    T = dqr.shape[0]
    ntile = MLA_W // LANES

    def body(dq_ref, dk_ref, pos_ref, invf_ref, oq_ref, ok_ref):
        cos, sin, first = _rope_tables(pos_ref, invf_ref)
        for t in range(ntile):
            sl = slice(t * LANES, (t + 1) * LANES)
            oq_ref[:, sl] = _rope_apply_t(dq_ref[:, sl], cos, sin, first).astype(BF16)
        ok_ref[...] = _rope_apply_t(dk_ref[...], cos, sin, first).astype(BF16)

    return pl.pallas_call(
        body, name=name, grid=(T // tm,),
        in_specs=[pl.BlockSpec((tm, MLA_W), lambda i: (i, 0)), pl.BlockSpec((tm, LANES), lambda i: (i, 0)),
                  pl.BlockSpec((tm, 1), lambda i: (i, 0)), pl.BlockSpec((1, LANES), lambda i: (0, 0))],
        out_specs=[pl.BlockSpec((tm, MLA_W), lambda i: (i, 0)), pl.BlockSpec((tm, LANES), lambda i: (i, 0))],
        out_shape=[jax.ShapeDtypeStruct((T, MLA_W), BF16), jax.ShapeDtypeStruct((T, LANES), BF16)],
        compiler_params=_cparams(("parallel",)),
    )(dqr, dkr, pos, invf)


CONV_ROWS = 256
HALO = 8


def _conv_taps(w_ref):
    return w_ref[0:1, :], w_ref[1:2, :], w_ref[2:3, :]


def _conv_rows(cur, prev, w, bias):
    ext = jnp.concatenate([prev, cur], axis=0)
    u1 = pltpu.roll(ext, 1, 0)[HALO:]
    u2 = pltpu.roll(ext, 2, 0)[HALO:]
    return w[2] * cur + w[1] * u1 + w[0] * u2 + bias, u1, u2


def _conv_fwd(u, w, bias, *, seq, name):
    T = u.shape[0]
    B = T // seq
    W2 = 2 * FF_BLK

    def body(u_ref, w_ref, b_ref, a_ref):
        wv = _conv_taps(w_ref)
        bv = b_ref[...]
        for c in range(seq // CONV_ROWS):
            r0 = c * CONV_ROWS
            cur = u_ref[r0:r0 + CONV_ROWS, :]
            prev = u_ref[r0 - HALO:r0, :] if c > 0 else jnp.zeros((HALO, W2), F32)
            y, _, _ = _conv_rows(cur, prev, wv, bv)
            gc = y[:, :FF_BLK]
            a_ref[r0:r0 + CONV_ROWS, :] = (gc * (1.0 / (1.0 + jnp.exp(-gc))) * y[:, FF_BLK:]).astype(BF16)

    return pl.pallas_call(
        body, name=name, grid=(B, N_FF_BLK),
        in_specs=[pl.BlockSpec((seq, W2), lambda b, j: (b, j)), pl.BlockSpec((3, W2), lambda b, j: (0, j)),
                  pl.BlockSpec((1, W2), lambda b, j: (0, j))],
        out_specs=pl.BlockSpec((seq, FF_BLK), lambda b, j: (b, j)),
        out_shape=jax.ShapeDtypeStruct((T, D_FF), BF16),
        compiler_params=_cparams(("parallel", "parallel")),
    )(u, w, bias)


def _conv_bwd(u, da, w, bias, *, seq, name):
    T = u.shape[0]
    B = T // seq
    W2 = 2 * FF_BLK
    nchunk = seq // CONV_ROWS

    def body(u_ref, da_ref, w_ref, b_ref, du_ref, dw_ref, db_ref, duc_s):
        wv = _conv_taps(w_ref)
        bv = b_ref[...]
        zrow = jnp.zeros((1, W2), F32)
        dw0, dw1, dw2, dbs = zrow, zrow, zrow, zrow
        for c in range(nchunk):
            r0 = c * CONV_ROWS
            cur = u_ref[r0:r0 + CONV_ROWS, :]
            prev = u_ref[r0 - HALO:r0, :] if c > 0 else jnp.zeros((HALO, W2), F32)
            y, u1, u2 = _conv_rows(cur, prev, wv, bv)
            gc = y[:, :FF_BLK]
            vc = y[:, FF_BLK:]
            sg = 1.0 / (1.0 + jnp.exp(-gc))
            dav = da_ref[r0:r0 + CONV_ROWS, :]
            duc = jnp.concatenate([dav * vc * (sg * (1.0 + gc * (1.0 - sg))), dav * (gc * sg)], axis=1)
            duc_s[r0:r0 + CONV_ROWS, :] = duc
            dw0 = dw0 + jnp.sum(duc * u2, axis=0, keepdims=True)
            dw1 = dw1 + jnp.sum(duc * u1, axis=0, keepdims=True)
            dw2 = dw2 + jnp.sum(duc * cur, axis=0, keepdims=True)
            dbs = dbs + jnp.sum(duc, axis=0, keepdims=True)
        duc_s[seq:seq + HALO, :] = jnp.zeros((HALO, W2), F32)
        n_ext = CONV_ROWS + HALO
        for c in range(nchunk):
            r0 = c * CONV_ROWS
            ext = duc_s[r0:r0 + n_ext, :]
            s1 = pltpu.roll(ext, n_ext - 1, 0)[:CONV_ROWS]
            s2 = pltpu.roll(ext, n_ext - 2, 0)[:CONV_ROWS]
            du_ref[r0:r0 + CONV_ROWS, :] = (wv[2] * ext[:CONV_ROWS] + wv[1] * s1 + wv[0] * s2).astype(BF16)

        first = pl.program_id(1) == 0

        @pl.when(first)
        def _():
            dw_ref[0:1, :] = dw0
            dw_ref[1:2, :] = dw1
            dw_ref[2:3, :] = dw2
            db_ref[...] = dbs

        @pl.when(jnp.logical_not(first))
        def _():
            dw_ref[0:1, :] += dw0
            dw_ref[1:2, :] += dw1
            dw_ref[2:3, :] += dw2
            db_ref[...] += dbs

    return pl.pallas_call(
        body, name=name, grid=(N_FF_BLK, B),
        in_specs=[pl.BlockSpec((seq, W2), lambda j, b: (b, j)), pl.BlockSpec((seq, FF_BLK), lambda j, b: (b, j)),
                  pl.BlockSpec((3, W2), lambda j, b: (0, j)), pl.BlockSpec((1, W2), lambda j, b: (0, j))],
        out_specs=[pl.BlockSpec((seq, W2), lambda j, b: (b, j)), pl.BlockSpec((3, W2), lambda j, b: (0, j)),
                   pl.BlockSpec((1, W2), lambda j, b: (0, j))],
        out_shape=[jax.ShapeDtypeStruct((T, 2 * D_FF), BF16), jax.ShapeDtypeStruct((3, 2 * D_FF), F32),
                   jax.ShapeDtypeStruct((1, 2 * D_FF), F32)],
        scratch_shapes=[pltpu.VMEM((seq + HALO, W2), F32)],
        compiler_params=_cparams(("parallel", "arbitrary")),
    )(u, da, w, bias)


def _place():
    return lax.axis_index("x"), lax.axis_index("y"), lax.axis_index("c")


def _other_chips(x, y):
    return [(1 - x, y), (x, 1 - y), (1 - x, 1 - y)]


def _all_gather(vs, *, name):
    n = len(vs)

    def body(*refs):
        v_refs, out_refs = refs[:n], refs[n:2 * n]
        send_sems, recv_sems, local_sems = refs[2 * n:]
        x, y, c = _place()
        me, sibling = (x, y, c), (x, y, 1 - c)
        chips = _other_chips(x, y)

        def slab(a, px, py, pc):
            return out_refs[a].at[4 * px + 2 * py + pc]

        def copy(a, k, block, to, src=None):
            return pltpu.make_async_remote_copy(
                src_ref=slab(a, *block) if src is None else src, dst_ref=slab(a, *block),
                send_sem=send_sems.at[7 * a + k], recv_sem=recv_sems.at[7 * a + k], device_id=to, device_id_type=MESH)

        mine = [pltpu.make_async_copy(v_refs[a], slab(a, *me), local_sems.at[a]) for a in range(n)]
        for cp in mine:
            cp.start()
        first = []
        for a in range(n):
            first.append(copy(a, 0, me, sibling, src=v_refs[a]))
            first += [copy(a, 1 + j, me, (*chip, c), src=v_refs[a]) for j, chip in enumerate(chips)]
        for cp in first:
            cp.start()
        passed = []
        for j, chip in enumerate(chips):
            for a in range(n):
                copy(a, 1 + j, (*chip, c), me).wait_recv()
                cp = copy(a, 4 + j, (*chip, c), sibling)
                cp.start()
                passed.append(cp)
        for a in range(n):
            copy(a, 0, sibling, me).wait_recv()
            for j, chip in enumerate(chips):
                copy(a, 4 + j, (*chip, 1 - c), me).wait_recv()
        for cp in first + passed:
            cp.wait_send()
        for cp in mine:
            cp.wait()

    return pl.pallas_call(
        body, name=name, in_specs=[ANY] * n, out_specs=[ANY] * n,
        out_shape=[jax.ShapeDtypeStruct((N_DEV,) + v.shape, v.dtype) for v in vs],
        scratch_shapes=[pltpu.SemaphoreType.DMA((7 * n,)), pltpu.SemaphoreType.DMA((7 * n,)), pltpu.SemaphoreType.DMA((n,))],
    )(*vs)


def _rs_sibling(g8s, *, name):
    n = len(g8s)

    def body(*refs):
        g_refs, out_refs = refs[:n], refs[n:2 * n]
        send_sems, recv_sems = refs[2 * n:]
        x, y, c = _place()
        copies = [
            pltpu.make_async_remote_copy(
                src_ref=g_refs[a].at[2 * k + 1 - c], dst_ref=out_refs[a].at[k],
                send_sem=send_sems.at[4 * a + k], recv_sem=recv_sems.at[4 * a + k],
                device_id=(x, y, 1 - c), device_id_type=MESH)
            for a in range(n) for k in range(4)]
        for cp in copies:
            cp.start()
        for cp in copies:
            cp.wait()

    return pl.pallas_call(
        body, name=name, in_specs=[ANY] * n, out_specs=[ANY] * n,
        out_shape=[jax.ShapeDtypeStruct((4,) + g.shape[1:], g.dtype) for g in g8s],
        scratch_shapes=[pltpu.SemaphoreType.DMA((4 * n,)), pltpu.SemaphoreType.DMA((4 * n,))],
    )(*g8s)


def _row_tile(rows):
    return rows if rows <= 512 else 256


def _rs_chip_sum(g8, from_sibling, place_idx, *, name):
    _, R, C = g8.shape
    tr = _row_tile(R)

    def body(pi_ref, a_ref, b_ref, f_ref, h_ref):
        s = a_ref[...] + b_ref[...]
        h_ref[...] = s.astype(BF16)

        @pl.when(pl.program_id(1) == pi_ref[1])
        def _():
            f_ref[...] = s

    blk = pl.BlockSpec((None, tr, C), lambda r, k, pi_ref: (k, r, 0))
    return pl.pallas_call(
        body, name=name,
        grid_spec=pltpu.PrefetchScalarGridSpec(
            num_scalar_prefetch=1, grid=(R // tr, 4),
            in_specs=[pl.BlockSpec((None, tr, C), lambda r, k, pi_ref: (2 * k + pi_ref[0], r, 0)), blk],
            out_specs=[pl.BlockSpec((tr, C), lambda r, k, pi_ref: (r, 0)), blk]),
        out_shape=[jax.ShapeDtypeStruct((R, C), F32), jax.ShapeDtypeStruct((4, R, C), BF16)],
        compiler_params=_cparams(("parallel", "arbitrary")),
    )(place_idx, g8, from_sibling)


def _rs_chips(h4s, *, name):
    n = len(h4s)

    def body(*refs):
        h_refs, out_refs = refs[:n], refs[n:2 * n]
        send_sems, recv_sems = refs[2 * n:]
        x, y, c = _place()
        copies = [
            pltpu.make_async_remote_copy(
                src_ref=h_refs[a].at[2 * cx + cy], dst_ref=out_refs[a].at[j],
                send_sem=send_sems.at[3 * a + j], recv_sem=recv_sems.at[3 * a + j],
                device_id=(cx, cy, c), device_id_type=MESH)
            for a in range(n) for j, (cx, cy) in enumerate(_other_chips(x, y))]
        for cp in copies:
            cp.start()
        for cp in copies:
            cp.wait()

    return pl.pallas_call(
        body, name=name, in_specs=[ANY] * n, out_specs=[ANY] * n,
        out_shape=[jax.ShapeDtypeStruct((3,) + h.shape[1:], h.dtype) for h in h4s],
        scratch_shapes=[pltpu.SemaphoreType.DMA((3 * n,)), pltpu.SemaphoreType.DMA((3 * n,))],
    )(*h4s)


def _split_moves(segments, chunk):
    moves = []
    for dst, src, length in segments:
        while length > 0:
            dev, off = divmod(src, chunk)
            take = min(length, chunk - off)
            moves.append((dst, dev, off, take))
            dst, src, length = dst + take, src + take, length - take
    return moves


def _assemble(stacked, segments, zero_spans, out_cols, *, name):
    _, R, c = stacked.shape
    tr = _row_tile(R)
    moves = _split_moves(segments, c)

    def body(x_ref, o_ref):
        for dst, dev, off, take in moves:
            o_ref[:, dst:dst + take] = x_ref[dev, :, off:off + take]
        for a, b in zero_spans:
            o_ref[:, a:b] = jnp.zeros((tr, b - a), o_ref.dtype)

    return pl.pallas_call(
        body, name=name, grid=(R // tr,),
        in_specs=[pl.BlockSpec((N_DEV, tr, c), lambda i: (0, i, 0))],
        out_specs=pl.BlockSpec((tr, out_cols), lambda i: (i, 0)),
        out_shape=jax.ShapeDtypeStruct((R, out_cols), stacked.dtype),
        compiler_params=_cparams(("parallel",)),
    )(stacked)


def _disassemble(full, segments, chunk, *, name):
    R = full.shape[0]
    tr = _row_tile(R)
    moves = _split_moves(segments, chunk)

    def body(x_ref, o_ref):
        seen = set()
        for dst, dev, off, take in moves:
            piece = x_ref[:, dst:dst + take]
            if (dev, off) in seen:
                piece = piece + o_ref[dev, :, off:off + take]
            seen.add((dev, off))
            o_ref[dev, :, off:off + take] = piece

    return pl.pallas_call(
        body, name=name, grid=(R // tr,),
        in_specs=[pl.BlockSpec((tr, full.shape[1]), lambda i: (i, 0))],
        out_specs=pl.BlockSpec((N_DEV, tr, chunk), lambda i: (0, i, 0)),
        out_shape=jax.ShapeDtypeStruct((N_DEV, R, chunk), F32),
        compiler_params=_cparams(("parallel",)),
    )(full)


_O_CQ = 3 * SB_W
_O_CKV = _O_CQ + Q_LORA
_O_KR = _O_CKV + KV_LORA
SEG_W_IN = ((0, 0, 3 * SB_W), (P_CKV, _O_CKV, KV_LORA), (P_KRT, _O_KR, MLA_ROPE), (P_KRT + MLA_ROPE, _O_KR, MLA_ROPE),
            (P_CQ, _O_CQ, Q_LORA))
ZERO_W_IN = ((P_KRT + 2 * MLA_ROPE, P_CQ),)
SEG_W_UQ = tuple((MLA_NOPE * h, MLA_QK * h, MLA_NOPE) for h in range(MLA_HEADS)) + tuple(
    (MLA_W + LANES * (h // 2) + MLA_ROPE * (h % 2), MLA_QK * h + MLA_NOPE, MLA_ROPE) for h in range(MLA_HEADS))
ZERO_W_UQ = tuple((MLA_W + LANES * g + 2 * MLA_ROPE, MLA_W + LANES * (g + 1)) for g in range(MLA_HEADS // 2))
SEG_W_UKV = tuple((MLA_NOPE * h, (MLA_NOPE + MLA_V) * h, MLA_NOPE) for h in range(MLA_HEADS)) + tuple(
    (MLA_W + MLA_V * h, (MLA_NOPE + MLA_V) * h + MLA_NOPE, MLA_V) for h in range(MLA_HEADS))
SEG_W_UP = tuple((2 * FF_BLK * blk + FF_BLK * half, D_FF * half + FF_BLK * blk, FF_BLK)
                 for half in range(2) for blk in range(N_FF_BLK))


def _sum8(g, *, name):
    _, R, C = g.shape

    def body(g_ref, o_ref):
        acc = g_ref[0]
        for k in range(1, N_DEV):
            acc = acc + g_ref[k]
        o_ref[...] = acc

    return pl.pallas_call(
        body, name=name, out_shape=jax.ShapeDtypeStruct((R, C), F32),
    )(g)


def _adamw_math(w, gf, m, v):
    c1 = 1.0 / (1.0 - ADAM_B1 ** ADAM_STEP)
    c2 = 1.0 / (1.0 - ADAM_B2 ** ADAM_STEP)
    mn = ADAM_B1 * m + (1.0 - ADAM_B1) * gf
    vn = ADAM_B2 * v + (1.0 - ADAM_B2) * (gf * gf)
    return -ADAM_LR * ((mn * c1) / (jnp.sqrt(vn * c2) + ADAM_EPS) + ADAM_WD * w), mn, vn


def _adamw(w, g, m, v, *, name):
    R, C = w.shape
    tr = _row_tile(R)

    def body(w_ref, g_ref, m_ref, v_ref, d_ref, mo_ref, vo_ref):
        d_ref[...], mo_ref[...], vo_ref[...] = _adamw_math(w_ref[...], g_ref[...], m_ref[...], v_ref[...])

    blk = pl.BlockSpec((tr, C), lambda i: (i, 0))
    shp = jax.ShapeDtypeStruct((R, C), F32)
    return pl.pallas_call(
        body, name=name, grid=(R // tr,), in_specs=[blk] * 4, out_specs=[blk] * 3,
        out_shape=[shp, shp, shp], compiler_params=_cparams(("parallel",)),
    )(w, g, m, v)


def _adamw_rs(own, r3, w, m, v, *, name):
    R, C = w.shape
    tr = _row_tile(R)

    def body(f_ref, r_ref, w_ref, m_ref, v_ref, g_ref, d_ref, mo_ref, vo_ref):
        gf = ((f_ref[...] + r_ref[0].astype(F32)) + r_ref[1].astype(F32)) + r_ref[2].astype(F32)
        g_ref[...] = gf
        d_ref[...], mo_ref[...], vo_ref[...] = _adamw_math(w_ref[...], gf, m_ref[...], v_ref[...])

    blk = pl.BlockSpec((tr, C), lambda i: (i, 0))
    shp = jax.ShapeDtypeStruct((R, C), F32)
    return pl.pallas_call(
        body, name=name, grid=(R // tr,),
        in_specs=[blk, pl.BlockSpec((3, tr, C), lambda i: (0, i, 0)), blk, blk, blk], out_specs=[blk] * 4,
        out_shape=[shp] * 4, compiler_params=_cparams(("parallel",)),
    )(own, r3, w, m, v)


def _ff_interleave(a):
    lead = a.shape[:-1]
    return a.reshape(*lead, 2, N_FF_BLK, FF_BLK).swapaxes(-3, -2).reshape(*lead, 2 * D_FF)


def _ff_deinterleave(a):
    lead = a.shape[:-1]
    return a.reshape(*lead, N_FF_BLK, 2, FF_BLK).swapaxes(-3, -2).reshape(*lead, 2 * D_FF)


SMALL =(("g_mix", D_MODEL), ("g_cq", Q_LORA), ("g_ckv", KV_LORA), ("g_sb_out", SB_W), ("g_mla_out", MLA_W),
         ("g_ffn", D_MODEL), ("conv_b", 2 * D_FF), ("g_final", D_MODEL))
SMALL_ROWS = 88


def _pack_small(d):
    flat = jnp.concatenate([d[n].reshape(-1) for n, _ in SMALL])
    flat = jnp.pad(flat, (0, SMALL_ROWS * LANES - flat.shape[0]))
    return flat.reshape(SMALL_ROWS, LANES)


def _unpack_small(a):
    flat = a.reshape(-1)
    out, off = {}, 0
    for n, size in SMALL:
        out[n] = flat[off:off + size]
        off += size
    return out


def kernel(x, positions, g_mix, w_in, g_cq, w_uq, g_ckv, w_ukv, g_sb_out, g_mla_out, w_out, g_ffn, w_up, conv_w, conv_b, w_down, g_final, loss_target, m_g_mix, m_w_in, m_g_cq, m_w_uq, m_g_ckv, m_w_ukv, m_g_sb_out, m_g_mla_out, m_w_out, m_g_ffn, m_w_up, m_conv_w, m_conv_b, m_w_down, m_g_final, v_g_mix, v_w_in, v_g_cq, v_w_uq, v_g_ckv, v_w_ukv, v_g_sb_out, v_g_mla_out, v_w_out, v_g_ffn, v_w_up, v_conv_w, v_conv_b, v_w_down, v_g_final):
    B, S, D = x.shape
    T = B * S
    xf = x.reshape(T, D)
    tgt = loss_target.reshape(T, D)
    pos = positions.reshape(T, 1)
    half = MLA_ROPE // 2
    inv_freq = 1.0 / (ROPE_BASE ** (jnp.arange(half, dtype=F32) * (2.0 / MLA_ROPE)))
    invf = jnp.tile(inv_freq, LANES // half).reshape(1, LANES)
    place_idx = jnp.stack([lax.axis_index("c"), 2 * lax.axis_index("x") + lax.axis_index("y")]).astype(jnp.int32)

    names = ("w_in", "w_uq", "w_ukv", "w_out", "w_up", "w_down", "conv_w")
    shard = {"w_in": w_in[0], "w_uq": w_uq[0], "w_ukv": w_ukv[0], "w_out": w_out[0], "w_up": w_up[0],
             "w_down": w_down[0], "conv_w": conv_w[0]}
    sent = [shard[n] if n == "conv_w" else shard[n].astype(BF16) for n in names]
    got = dict(zip(names, _all_gather(sent, name="ag_weights")))
    wi = _assemble(got["w_in"], SEG_W_IN, ZERO_W_IN, P_COLS, name="asm_w_in")
    wuq = _assemble(got["w_uq"], SEG_W_UQ, ZERO_W_UQ, 2 * MLA_W, name="asm_w_uq")
    wukv = _assemble(got["w_ukv"], SEG_W_UKV, (), 2 * MLA_W, name="asm_w_ukv")
    wup = _assemble(got["w_up"], SEG_W_UP, (), 2 * D_FF, name="asm_w_up")
    cwi = _assemble(got["conv_w"], SEG_W_UP, (), 2 * D_FF, name="asm_conv_w")
    wo = got["w_out"].reshape(D, D)
    wdn = got["w_down"].reshape(D_FF, D)
    cbi = _ff_interleave(conv_b)

    h = _rms_fwd(xf, g_mix, tm=512, name="rms_mix")
    p = _matmul_nn(h, wi, tm=512, tn=768, out_dtype=F32, name="proj_in")
    o_sb, ltot = _sb_fwd(p, seq=S, name="sb_fwd")
    cq = _rms_fwd(p, g_cq, tm=512, name="rms_cq", col_block=P_CQ // Q_LORA)
    ckv = _rms_fwd(p, g_ckv, tm=512, name="rms_ckv", col_block=P_CKV // KV_LORA)
    qfull = _matmul_nn(cq, wuq, tm=512, tn=1024, out_dtype=F32, name="proj_uq")
    kvm = _matmul_nn(ckv, wukv, tm=512, tn=1024, out_dtype=BF16, name="proj_ukv")
    qm, krt = _rope_fwd(qfull, p, pos, invf, tm=512, name="rope_fwd")
    o_mla, lse = _mla_fwd(qm, kvm, krt, seq=S, name="mla_fwd")
    ocat = _rms2_fwd(o_sb, o_mla, g_sb_out, g_mla_out, tm=512, name="rms_heads")
    x1 = _matmul_nn(ocat, wo, tm=512, tn=1024, out_dtype=F32, name="proj_out", residual=xf)
    hf = _rms_fwd(x1, g_ffn, tm=512, name="rms_ffn")
    u = _matmul_nn(hf, wup, tm=512, tn=512, out_dtype=F32, name="ffn_up")
    a = _conv_fwd(u, cwi, cbi, seq=S, name="conv_fwd")
    x2 = _matmul_nn(a, wdn, tm=512, tn=1024, out_dtype=F32, name="ffn_down", residual=x1)
    dx2, dg_final, loss_row = _final_loss(x2, g_final.reshape(1, D), tgt, tm=512, name="final_loss")

    da = _matmul_nt(dx2, wdn, tm=512, tn=D_FF // 2, out_dtype=F32, name="d_ffn_down")
    dw_down = _matmul_tn(a, dx2, tm=D_FF // 2, tn=1024, tk=512, name="dw_down")
    du, dcw, dcb = _conv_bwd(u, da, cwi, cbi, seq=S, name="conv_bwd")
    dhf = _matmul_nt(du, wup, tm=512, tn=512, out_dtype=F32, name="d_ffn_up")
    dw_up = _matmul_tn(hf, du, tm=1024, tn=512, tk=512, name="dw_up")
    dx1, dg_ffn = _rms_bwd(dhf, x1, g_ffn, tm=512, name="rms_ffn_bwd", residual=dx2)
    docat = _matmul_nt(dx1, wo, tm=512, tn=1024, out_dtype=F32, name="d_proj_out")
    dw_out = _matmul_tn(ocat, dx1, tm=1024, tn=1024, tk=512, name="dw_out")
    do_sb, do_mla, dg_sb, dg_mla = _rms2_bwd(docat, o_sb, o_mla, g_sb_out, g_mla_out, tm=512, name="rms_heads_bwd")
    dq_sb, dk_sb, dv_sb = _sb_bwd(p, ltot, do_sb, seq=S, name="sb_bwd")
    dqn, dqr, dkn, dvm, dkr = _mla_bwd(qm, kvm, krt, o_mla, lse, do_mla, seq=S, name="mla_bwd")
    dqr_u, dkr_u = _rope_bwd(dqr, dkr, pos, invf, tm=512, name="rope_bwd")
    dqm = jnp.concatenate([dqn, dqr_u], axis=1)
    dkvm = jnp.concatenate([dkn, dvm], axis=1)
    dcq_n = _matmul_nt(dqm, wuq, tm=512, tn=Q_LORA, out_dtype=F32, name="d_proj_uq")
    dw_uq = _matmul_tn(cq, dqm, tm=Q_LORA, tn=1024, tk=512, name="dw_uq")
    dckv_n = _matmul_nt(dkvm, wukv, tm=512, tn=KV_LORA, out_dtype=F32, name="d_proj_ukv")
    dw_ukv = _matmul_tn(ckv, dkvm, tm=KV_LORA, tn=1024, tk=512, name="dw_ukv")
    dcq, dg_cq = _rms_bwd(dcq_n, p, g_cq, tm=512, name="rms_cq_bwd", col_block=P_CQ // Q_LORA, out_dtype=BF16)
    dckv, dg_ckv = _rms_bwd(dckv_n, p, g_ckv, tm=512, name="rms_ckv_bwd", col_block=P_CKV // KV_LORA, out_dtype=BF16)
    dp = jnp.concatenate([dq_sb, dk_sb, dv_sb, dckv, dkr_u, dcq], axis=1)
    dh = _matmul_nt(dp, wi, tm=512, tn=1024, out_dtype=F32, name="d_proj_in")
    dw_in = _matmul_tn(h, dp, tm=1024, tn=768, tk=512, name="dw_in")
    dx, dg_mix = _rms_bwd(dh, xf, g_mix, tm=512, name="rms_mix_bwd", residual=dx1)

    g8 = {"w_in": _disassemble(dw_in, SEG_W_IN, shard["w_in"].shape[1], name="split_dw_in"),
          "w_uq": _disassemble(dw_uq, SEG_W_UQ, shard["w_uq"].shape[1], name="split_dw_uq"),
          "w_ukv": _disassemble(dw_ukv, SEG_W_UKV, shard["w_ukv"].shape[1], name="split_dw_ukv"),
          "w_up": _disassemble(dw_up, SEG_W_UP, shard["w_up"].shape[1], name="split_dw_up"),
          "conv_w": _disassemble(dcw, SEG_W_UP, shard["conv_w"].shape[1], name="split_dconv_w"),
          "w_out": dw_out.reshape((N_DEV,) + shard["w_out"].shape),
          "w_down": dw_down.reshape((N_DEV,) + shard["w_down"].shape)}
    from_sib = _rs_sibling([g8[n] for n in names], name="rs_sibling")
    sums = [_rs_chip_sum(g8[n], fs, place_idx, name="rs_chip_sum_" + n) for n, fs in zip(names, from_sib)]
    r3 = dict(zip(names, _rs_chips([h4 for _, h4 in sums], name="rs_chips")))
    own = {n: f for n, (f, _) in zip(names, sums)}

    small_part = {"g_mix": dg_mix, "g_cq": dg_cq, "g_ckv": dg_ckv, "g_sb_out": dg_sb, "g_mla_out": dg_mla,
                  "g_ffn": dg_ffn, "conv_b": _ff_deinterleave(dcb), "g_final": dg_final}
    small_all, = _all_gather([_pack_small(small_part)], name="ag_small_grads")
    gsmall = _sum8(small_all, name="sum_small_grads")

    params = {"w_in": (w_in, m_w_in, v_w_in), "w_uq": (w_uq, m_w_uq, v_w_uq), "w_ukv": (w_ukv, m_w_ukv, v_w_ukv),
              "w_out": (w_out, m_w_out, v_w_out), "w_up": (w_up, m_w_up, v_w_up), "conv_w": (conv_w, m_conv_w, v_conv_w),
              "w_down": (w_down, m_w_down, v_w_down)}
    grad, delta, new_m, new_v = {}, {}, {}, {}
    for n, (w_, m_, v_) in params.items():
        g_, d_, mn_, vn_ = _adamw_rs(own[n], r3[n], w_[0], m_[0], v_[0], name="adamw_" + n)
        grad[n], delta[n], new_m[n], new_v[n] = g_[None], d_[None], mn_[None], vn_[None]
    small_w = {"g_mix": g_mix, "g_cq": g_cq, "g_ckv": g_ckv, "g_sb_out": g_sb_out, "g_mla_out": g_mla_out,
               "g_ffn": g_ffn, "conv_b": conv_b, "g_final": g_final}
    small_m = {"g_mix": m_g_mix, "g_cq": m_g_cq, "g_ckv": m_g_ckv, "g_sb_out": m_g_sb_out, "g_mla_out": m_g_mla_out,
               "g_ffn": m_g_ffn, "conv_b": m_conv_b, "g_final": m_g_final}
    small_v = {"g_mix": v_g_mix, "g_cq": v_g_cq, "g_ckv": v_g_ckv, "g_sb_out": v_g_sb_out, "g_mla_out": v_g_mla_out,
               "g_ffn": v_g_ffn, "conv_b": v_conv_b, "g_final": v_g_final}
    ds_, ms_, vs_ = _adamw(_pack_small(small_w), gsmall, _pack_small(small_m), _pack_small(small_v), name="adamw_small")
    for src, dst in ((_unpack_small(gsmall), grad), (_unpack_small(ds_), delta), (_unpack_small(ms_), new_m), (_unpack_small(vs_), new_v)):
        for n, _ in SMALL:
            dst[n] = src[n].reshape(small_w[n].shape)

    loss = lax.psum(loss_row[0, 0], MESH_AXES)
    order = ("g_mix", "w_in", "g_cq", "w_uq", "g_ckv", "w_ukv", "g_sb_out", "g_mla_out", "w_out", "g_ffn", "w_up",
             "conv_w", "conv_b", "w_down", "g_final")
    return (loss, dx.reshape(B, S, D), *[grad[n] for n in order], *[delta[n] for n in order],
            *[new_m[n] for n in order], *[new_v[n] for n in order])
```

```python
import jax
import jax.numpy as jnp
from jax import lax
from jax.experimental import pallas as pl
from jax.experimental.pallas import tpu as pltpu
from jax.experimental.pallas import tpu_sc as plsc

F32 = jnp.float32
BF16 = jnp.bfloat16

D_MODEL = 1024
SB_HEADS = 8
SB_HEAD_DIM = 64
MLA_HEADS = 8
MLA_NOPE = 64
MLA_ROPE = 32
MLA_V = 64
Q_LORA = 384
KV_LORA = 256
D_FF = 2816
ROPE_BASE = 10000.0
EPS = 1e-6
SB_W = SB_HEADS * SB_HEAD_DIM
MLA_W = MLA_HEADS * MLA_V
MLA_QK = MLA_NOPE + MLA_ROPE
IN_COLS = 3 * SB_W + Q_LORA + KV_LORA + MLA_ROPE

ADAM_LR = 0.001
ADAM_B1 = 0.9
ADAM_B2 = 0.999
ADAM_EPS = 1e-08
ADAM_WD = 0.01
ADAM_STEP = 10

N_DEV = 8
MESH_AXES = ("x", "y", "c")
LANES = 128
V7X_VMEM_LIMIT = 56 * 1024 * 1024
FF_BLK = 256
N_FF_BLK = D_FF // FF_BLK

P_Q, P_K, P_V = 0, SB_W, 2 * SB_W
P_CKV = 3 * SB_W
P_KRT = P_CKV + KV_LORA
P_CQ = P_KRT + LANES
P_COLS = P_CQ + Q_LORA

MESH = pl.DeviceIdType.MESH
ANY = pl.BlockSpec(memory_space=pl.ANY)


def _cparams(sem=None, vmem=V7X_VMEM_LIMIT):
    return pltpu.CompilerParams(dimension_semantics=sem, vmem_limit_bytes=vmem)


def _matmul_nn(a, b, *, tm, tn, out_dtype, name, residual=None):
    M, K = a.shape
    N = b.shape[1]
    in_specs = [pl.BlockSpec((tm, K), lambda i, j: (i, 0)), pl.BlockSpec((K, tn), lambda i, j: (0, j))]
    args = [a, b]
    if residual is not None:
        in_specs.append(pl.BlockSpec((tm, tn), lambda i, j: (i, j)))
        args.append(residual)

    def body(*refs):
        a_ref, b_ref = refs[0], refs[1]
        o_ref = refs[-1]
        acc = jnp.dot(a_ref[...].astype(BF16), b_ref[...], preferred_element_type=F32)
        if residual is not None:
            acc = acc + refs[2][...]
        o_ref[...] = acc.astype(out_dtype)

    return pl.pallas_call(
        body, name=name, grid=(M // tm, N // tn), in_specs=in_specs,
        out_specs=pl.BlockSpec((tm, tn), lambda i, j: (i, j)),
        out_shape=jax.ShapeDtypeStruct((M, N), out_dtype),
        compiler_params=_cparams(("parallel", "parallel")),
    )(*args)


def _matmul_nt(a, b, *, tm, tn, out_dtype, name):
    M, K = a.shape
    N = b.shape[0]

    def body(a_ref, b_ref, o_ref):
        acc = lax.dot_general(a_ref[...].astype(BF16), b_ref[...], (((1,), (1,)), ((), ())),
                              preferred_element_type=F32)
        o_ref[...] = acc.astype(out_dtype)

    return pl.pallas_call(
        body, name=name, grid=(M // tm, N // tn),
        in_specs=[pl.BlockSpec((tm, K), lambda i, j: (i, 0)), pl.BlockSpec((tn, K), lambda i, j: (j, 0))],
        out_specs=pl.BlockSpec((tm, tn), lambda i, j: (i, j)),
        out_shape=jax.ShapeDtypeStruct((M, N), out_dtype),
        compiler_params=_cparams(("parallel", "parallel")),
    )(a, b)


def _matmul_tn(a, b, *, tm, tn, tk, name):
    K, M = a.shape
    N = b.shape[1]

    def body(a_ref, b_ref, o_ref):
        k = pl.program_id(2)
        part = lax.dot_general(a_ref[...].astype(BF16), b_ref[...].astype(BF16), (((0,), (0,)), ((), ())),
                               preferred_element_type=F32)

        @pl.when(k == 0)
        def _():
            o_ref[...] = part

        @pl.when(k > 0)
        def _():
            o_ref[...] += part

    return pl.pallas_call(
        body, name=name, grid=(M // tm, N // tn, K // tk),
        in_specs=[pl.BlockSpec((tk, tm), lambda i, j, k: (k, i)), pl.BlockSpec((tk, tn), lambda i, j, k: (k, j))],
        out_specs=pl.BlockSpec((tm, tn), lambda i, j, k: (i, j)),
        out_shape=jax.ShapeDtypeStruct((M, N), F32),
        compiler_params=_cparams(("parallel", "parallel", "arbitrary")),
    )(a, b)


def _rms(xf, g):
    r = lax.rsqrt(jnp.mean(xf * xf, axis=1, keepdims=True) + EPS)
    return (xf * r) * g


def _rms_grad(dyf, xf, g):
    r = lax.rsqrt(jnp.mean(xf * xf, axis=1, keepdims=True) + EPS)
    xh = xf * r
    dyg = dyf * g
    dx = r * (dyg - xh * jnp.mean(dyg * xh, axis=1, keepdims=True))
    return dx, jnp.sum(dyf * xh, axis=0, keepdims=True)


def _accumulate(ref, part):
    @pl.when(pl.program_id(0) == 0)
    def _():
        ref[...] = part

    @pl.when(pl.program_id(0) > 0)
    def _():
        ref[...] += part


def _rms_fwd(x, g, *, tm, name, col_block=0):
    T = x.shape[0]
    C = g.shape[1]

    def body(x_ref, g_ref, o_ref):
        o_ref[...] = _rms(x_ref[...], g_ref[...]).astype(BF16)

    return pl.pallas_call(
        body, name=name, grid=(T // tm,),
        in_specs=[pl.BlockSpec((tm, C), lambda i: (i, col_block)), pl.BlockSpec((1, C), lambda i: (0, 0))],
        out_specs=pl.BlockSpec((tm, C), lambda i: (i, 0)),
        out_shape=jax.ShapeDtypeStruct((T, C), BF16),
        compiler_params=_cparams(("parallel",)),
    )(x, g)


def _rms_bwd(dy, x, g, *, tm, name, residual=None, col_block=0, out_dtype=F32):
    T = dy.shape[0]
    C = g.shape[1]
    in_specs = [pl.BlockSpec((tm, C), lambda i: (i, 0)), pl.BlockSpec((tm, C), lambda i: (i, col_block)),
                pl.BlockSpec((1, C), lambda i: (0, 0))]
    args = [dy, x, g]
    if residual is not None:
        in_specs.append(pl.BlockSpec((tm, C), lambda i: (i, 0)))
        args.append(residual)

    def body(*refs):
        dy_ref, x_ref, g_ref = refs[:3]
        dx_ref, dg_ref = refs[-2:]
        dx, part = _rms_grad(dy_ref[...].astype(F32), x_ref[...], g_ref[...])
        if residual is not None:
            dx = dx + refs[3][...]
        dx_ref[...] = dx.astype(out_dtype)
        _accumulate(dg_ref, part)

    return pl.pallas_call(
        body, name=name, grid=(T // tm,), in_specs=in_specs,
        out_specs=[pl.BlockSpec((tm, C), lambda i: (i, 0)), pl.BlockSpec((1, C), lambda i: (0, 0))],
        out_shape=[jax.ShapeDtypeStruct((T, C), out_dtype), jax.ShapeDtypeStruct((1, C), F32)],
        compiler_params=_cparams(("arbitrary",)),
    )(*args)


def _rms2_fwd(xa, xb, ga, gb, *, tm, name):
    T, C = xa.shape

    def body(xa_ref, xb_ref, ga_ref, gb_ref, o_ref):
        o_ref[:, :C] = _rms(xa_ref[...], ga_ref[...]).astype(BF16)
        o_ref[:, C:] = _rms(xb_ref[...], gb_ref[...]).astype(BF16)

    row = pl.BlockSpec((tm, C), lambda i: (i, 0))
    gsp = pl.BlockSpec((1, C), lambda i: (0, 0))
    return pl.pallas_call(
        body, name=name, grid=(T // tm,), in_specs=[row, row, gsp, gsp],
        out_specs=pl.BlockSpec((tm, 2 * C), lambda i: (i, 0)),
        out_shape=jax.ShapeDtypeStruct((T, 2 * C), BF16),
        compiler_params=_cparams(("parallel",)),
    )(xa, xb, ga, gb)


def _rms2_bwd(dy, xa, xb, ga, gb, *, tm, name):
    T, C = xa.shape

    def body(dy_ref, xa_ref, xb_ref, ga_ref, gb_ref, dxa_ref, dxb_ref, dga_ref, dgb_ref):
        dxa, pa = _rms_grad(dy_ref[:, :C], xa_ref[...], ga_ref[...])
        dxb, pb = _rms_grad(dy_ref[:, C:], xb_ref[...], gb_ref[...])
        dxa_ref[...] = dxa
        dxb_ref[...] = dxb
        _accumulate(dga_ref, pa)
        _accumulate(dgb_ref, pb)

    row = pl.BlockSpec((tm, C), lambda i: (i, 0))
    gsp = pl.BlockSpec((1, C), lambda i: (0, 0))
    return pl.pallas_call(
        body, name=name, grid=(T // tm,),
        in_specs=[pl.BlockSpec((tm, 2 * C), lambda i: (i, 0)), row, row, gsp, gsp],
        out_specs=[row, row, gsp, gsp],
        out_shape=[jax.ShapeDtypeStruct((T, C), F32), jax.ShapeDtypeStruct((T, C), F32),
                   jax.ShapeDtypeStruct((1, C), F32), jax.ShapeDtypeStruct((1, C), F32)],
        compiler_params=_cparams(("arbitrary",)),
    )(dy, xa, xb, ga, gb)


def _final_loss(x2, g, tgt, *, tm, name):
    T, C = x2.shape

    def body(x_ref, g_ref, t_ref, dx_ref, dg_ref, loss_ref):
        xf = x_ref[...]
        gf = g_ref[...]
        err = _rms(xf, gf) - t_ref[...]
        lpart = 0.5 * jnp.sum(jnp.mean(err * err, axis=1, keepdims=True), axis=0, keepdims=True)
        dx, gpart = _rms_grad(err * (1.0 / C), xf, gf)
        dx_ref[...] = dx
        _accumulate(dg_ref, gpart)
        _accumulate(loss_ref, jnp.broadcast_to(lpart, (1, LANES)))

    return pl.pallas_call(
        body, name=name, grid=(T // tm,),
        in_specs=[pl.BlockSpec((tm, C), lambda i: (i, 0)), pl.BlockSpec((1, C), lambda i: (0, 0)),
                  pl.BlockSpec((tm, C), lambda i: (i, 0))],
        out_specs=[pl.BlockSpec((tm, C), lambda i: (i, 0)), pl.BlockSpec((1, C), lambda i: (0, 0)),
                   pl.BlockSpec((1, LANES), lambda i: (0, 0))],
        out_shape=[jax.ShapeDtypeStruct((T, C), F32), jax.ShapeDtypeStruct((1, C), F32),
                   jax.ShapeDtypeStruct((1, LANES), F32)],
        compiler_params=_cparams(("arbitrary",)),
    )(x2, g, tgt)


ATT_T = 256
ATT_PAIRS = 2
NEG_BIG = -1e30


def _lane_iota():
    return lax.broadcasted_iota(jnp.int32, (1, LANES), 1)


def _head_masks():
    first = _lane_iota() < SB_HEAD_DIM
    return first, jnp.logical_not(first)


def _pick(mask, x):
    return jnp.where(mask, x, jnp.zeros_like(x))


def _lane_value(t, lane):
    return jnp.sum(jnp.where(_lane_iota() == lane, t, 0.0), axis=1, keepdims=True)


def _split_hi_lo(x):
    hi = x.astype(BF16)
    lo = (x - hi.astype(F32)).astype(BF16)
    return jnp.concatenate([hi, lo], axis=1)


def _tri(n, kind):
    r = lax.broadcasted_iota(jnp.int32, (n, n), 0)
    c = lax.broadcasted_iota(jnp.int32, (n, n), 1)
    u = {"suffix_excl": r > c, "prefix_incl": r <= c, "prefix_excl": r < c}[kind].astype(BF16)
    return jnp.concatenate([u, u], axis=0)


def _dot_nt(a, b):
    return lax.dot_general(a, b, (((1,), (1,)), ((), ())), preferred_element_type=F32)


def _dot_tn(a, b):
    return lax.dot_general(a, b, (((0,), (0,)), ((), ())), preferred_element_type=F32)


def _dot(a, b):
    return jnp.dot(a, b, preferred_element_type=F32)


def _causal_mask(n, strict):
    r = lax.broadcasted_iota(jnp.int32, (n, n), 0)
    c = lax.broadcasted_iota(jnp.int32, (n, n), 1)
    return (c < r) if strict else (c <= r)


def _sb_logs(qh, kj, vis):
    z = _dot_nt(qh, kj)
    sp = jnp.log(1.0 + jnp.exp(-jnp.abs(z)))
    lb = jnp.minimum(z, 0.0) - sp
    lk = jnp.minimum(-z, 0.0) - sp
    if vis is not None:
        lk = jnp.where(vis, lk, 0.0)
    return lb, lk


def _sb_fwd(p, *, seq, name):
    T = p.shape[0]
    B = T // seq
    TQ = ATT_T
    nq = seq // TQ
    PP = ATT_PAIRS
    W = PP * LANES
    nstep = SB_W // W
    NH = 2 * PP

    def body(q_ref, k_ref, v_ref, o_ref, lt_ref, q_s, k_s, v_s):
        masks = _head_masks()
        q = q_ref[...] * (SB_HEAD_DIM ** -0.5)
        v = v_ref[...]
        k_s[...] = k_ref[...].astype(BF16)
        for h in range(NH):
            ps = slice((h // 2) * LANES, (h // 2 + 1) * LANES)
            hs = slice(h * LANES, (h + 1) * LANES)
            q_s[:, hs] = _pick(masks[h % 2], q[:, ps]).astype(BF16)
            v_s[:, hs] = _pick(masks[h % 2], v[:, ps]).astype(BF16)
        u_suf = _tri(TQ, "suffix_excl")
        vis = _causal_mask(TQ, True)

        def q_block(i, carry):
            q0 = pl.multiple_of(i * TQ, TQ)
            qs = [q_s[pl.ds(q0, TQ), h * LANES:(h + 1) * LANES] for h in range(NH)]

            def head(qh, kj, vj, r_run, mask):
                lb, lk = _sb_logs(qh, kj, mask)
                a = jnp.exp(lb + _dot(_split_hi_lo(lk), u_suf) + r_run)
                if mask is not None:
                    a = jnp.where(mask, a, 0.0)
                return _dot(a.astype(BF16), vj), r_run + jnp.sum(lk, axis=1, keepdims=True)

            def tile(k0, c, mask):
                rs, accs = list(c[:NH]), list(c[NH:])
                logs = [_sb_logs(qs[h], k_s[pl.ds(k0, TQ), (h // 2) * LANES:(h // 2 + 1) * LANES], mask) for h in range(NH)]
                sums = [_dot(_split_hi_lo(lk), u_suf) for _, lk in logs]
                for h in range(NH):
                    a = jnp.exp(logs[h][0] + sums[h] + rs[h])
                    if mask is not None:
                        a = jnp.where(mask, a, 0.0)
                    accs[h // 2] = accs[h // 2] + _dot(a.astype(BF16), v_s[pl.ds(k0, TQ), h * LANES:(h + 1) * LANES])
                    rs[h] = rs[h] + jnp.sum(logs[h][1], axis=1, keepdims=True)
                return tuple(rs) + tuple(accs)

            zero = jnp.zeros((TQ, 1), F32)
            c = tile(q0, (zero,) * NH + (jnp.zeros((TQ, LANES), F32),) * PP, vis)

            def k_block(jj, c):
                return tile(pl.multiple_of((i - 1 - jj) * TQ, TQ), c, None)

            c = lax.fori_loop(0, i, k_block, c)
            for pr in range(PP):
                ps = slice(pr * LANES, (pr + 1) * LANES)
                o_ref[pl.ds(q0, TQ), ps] = c[NH + pr]
                lt_ref[pl.ds(q0, TQ), ps] = jnp.where(masks[0], c[2 * pr], c[2 * pr + 1])
            return carry

        lax.fori_loop(0, nq, q_block, 0)

    blk = lambda off: pl.BlockSpec((seq, W), lambda b, g: (b, off + g))
    out_blk = pl.BlockSpec((seq, W), lambda b, g: (b, g))
    return pl.pallas_call(
        body, name=name, grid=(B, nstep),
        in_specs=[blk(P_Q // W), blk(P_K // W), blk(P_V // W)],
        out_specs=[out_blk, out_blk],
        out_shape=[jax.ShapeDtypeStruct((T, SB_W), F32), jax.ShapeDtypeStruct((T, SB_W), F32)],
        scratch_shapes=[pltpu.VMEM((seq, NH * LANES), BF16), pltpu.VMEM((seq, W), BF16), pltpu.VMEM((seq, NH * LANES), BF16)],
        compiler_params=_cparams(("parallel", "parallel")),
    )(p, p, p)


def _sb_bwd(p, ltot, do, *, seq, name):
    T = p.shape[0]
    B = T // seq
    TQ = ATT_T
    nq = seq // TQ
    PP = ATT_PAIRS
    W = PP * LANES
    nstep = SB_W // W
    NH = 2 * PP
    scale = SB_HEAD_DIM ** -0.5

    def body(q_ref, k_ref, v_ref, lt_ref, do_ref, dq_ref, dk_ref, dv_ref, q_s, k_s, v_s, do_s, dk_s, dv_s):
        masks = _head_masks()
        q = q_ref[...] * scale
        dof = do_ref[...]
        k_s[...] = k_ref[...].astype(BF16)
        v_s[...] = v_ref[...].astype(BF16)
        for h in range(NH):
            ps = slice((h // 2) * LANES, (h // 2 + 1) * LANES)
            hs = slice(h * LANES, (h + 1) * LANES)
            q_s[:, hs] = _pick(masks[h % 2], q[:, ps]).astype(BF16)
            do_s[:, hs] = _pick(masks[h % 2], dof[:, ps]).astype(BF16)
        dk_s[...] = jnp.zeros_like(dk_s)
        dv_s[...] = jnp.zeros_like(dv_s)
        u_pin = _tri(TQ, "prefix_incl")
        u_pex = _tri(TQ, "prefix_excl")
        vis = _causal_mask(TQ, True)

        def q_block(i, carry):
            q0 = pl.multiple_of(i * TQ, TQ)
            qs = [q_s[pl.ds(q0, TQ), h * LANES:(h + 1) * LANES] for h in range(NH)]
            dos = [do_s[pl.ds(q0, TQ), h * LANES:(h + 1) * LANES] for h in range(NH)]
            lt = lt_ref[pl.ds(q0, TQ), :]
            lts = [_lane_value(lt[:, (h // 2) * LANES:(h // 2 + 1) * LANES], (h % 2) * SB_HEAD_DIM) for h in range(NH)]

            def tile(k0, c, mask):
                cs, gs, accs = list(c[:NH]), list(c[NH:2 * NH]), list(c[2 * NH:])
                kjs = [k_s[pl.ds(k0, TQ), pr * LANES:(pr + 1) * LANES] for pr in range(PP)]
                vjs = [v_s[pl.ds(k0, TQ), pr * LANES:(pr + 1) * LANES] for pr in range(PP)]
                logs = [_sb_logs(qs[h], kjs[h // 2], mask) for h in range(NH)]
                pins = [_dot(_split_hi_lo(lk), u_pin) for _, lk in logs]
                das = [_dot_nt(dos[h], vjs[h // 2]) for h in range(NH)]
                a_l, g_l = [], []
                for h in range(NH):
                    a = jnp.exp(logs[h][0] + ((lts[h] - cs[h]) - pins[h]))
                    if mask is not None:
                        a = jnp.where(mask, a, 0.0)
                    a_l.append(a)
                    g_l.append(das[h] * a)
                pres = [_dot(_split_hi_lo(g), u_pex) for g in g_l]
                dz_l = []
                for h in range(NH):
                    dz = g_l[h] - jnp.exp(logs[h][0]) * (g_l[h] + (pres[h] + gs[h]))
                    if mask is not None:
                        dz = jnp.where(mask, dz, 0.0)
                    dz_l.append(dz.astype(BF16))
                for h in range(NH):
                    accs[h] = accs[h] + _dot(dz_l[h], kjs[h // 2])
                for pr in range(PP):
                    ps = slice(pr * LANES, (pr + 1) * LANES)
                    ha, hb = 2 * pr, 2 * pr + 1
                    dk_s[pl.ds(k0, TQ), ps] += _dot_tn(dz_l[ha], qs[ha]) + _dot_tn(dz_l[hb], qs[hb])
                    dv_s[pl.ds(k0, TQ), ps] += _dot_tn(a_l[ha].astype(BF16), dos[ha]) + _dot_tn(a_l[hb].astype(BF16), dos[hb])
                for h in range(NH):
                    cs[h] = cs[h] + jnp.sum(logs[h][1], axis=1, keepdims=True)
                    gs[h] = gs[h] + jnp.sum(g_l[h], axis=1, keepdims=True)
                return tuple(cs) + tuple(gs) + tuple(accs)

            z1 = jnp.zeros((TQ, 1), F32)
            zl = jnp.zeros((TQ, LANES), F32)

            def k_block(j, c):
                return tile(pl.multiple_of(j * TQ, TQ), c, None)

            c = lax.fori_loop(0, i, k_block, (z1,) * (2 * NH) + (zl,) * NH)
            c = tile(q0, c, vis)
            for pr in range(PP):
                dq = jnp.where(masks[0], c[2 * NH + 2 * pr], c[2 * NH + 2 * pr + 1]) * scale
                dq_ref[pl.ds(q0, TQ), pr * LANES:(pr + 1) * LANES] = dq.astype(BF16)
            return carry

        lax.fori_loop(0, nq, q_block, 0)
        dk_ref[...] = dk_s[...].astype(BF16)
        dv_ref[...] = dv_s[...].astype(BF16)

    blk = lambda off: pl.BlockSpec((seq, W), lambda b, g: (b, off + g))
    out_blk = pl.BlockSpec((seq, W), lambda b, g: (b, g))
    return pl.pallas_call(
        body, name=name, grid=(B, nstep),
        in_specs=[blk(P_Q // W), blk(P_K // W), blk(P_V // W), out_blk, out_blk],
        out_specs=[out_blk, out_blk, out_blk],
        out_shape=[jax.ShapeDtypeStruct((T, SB_W), BF16) for _ in range(3)],
        scratch_shapes=[pltpu.VMEM((seq, NH * LANES), BF16), pltpu.VMEM((seq, W), BF16), pltpu.VMEM((seq, W), BF16),
                        pltpu.VMEM((seq, NH * LANES), BF16), pltpu.VMEM((seq, W), F32), pltpu.VMEM((seq, W), F32)],
        compiler_params=_cparams(("parallel", "parallel")),
    )(p, p, p, ltot, do)


def _mla_masks():
    lane = lax.broadcasted_iota(jnp.int32, (1, 2 * LANES), 1)
    ma = (lane < MLA_NOPE) | ((lane >= LANES) & (lane < LANES + MLA_ROPE))
    mb = ((lane >= MLA_NOPE) & (lane < LANES)) | ((lane >= LANES + MLA_ROPE) & (lane < LANES + 2 * MLA_ROPE))
    return ma, mb


def _mla_fwd(qm, kvm, krt, *, seq, name):
    T = qm.shape[0]
    B = T // seq
    TQ = ATT_T
    nq = seq // TQ
    PP = ATT_PAIRS
    W = PP * LANES
    nstep = MLA_W // W
    NH = 2 * PP
    CW = 2 * LANES
    scale = MLA_QK ** -0.5

    def body(qn_ref, qr_ref, kn_ref, v_ref, kr_ref, o_ref, lse_ref, q_s, kc_s, v_s):
        hm = _head_masks()
        mm = _mla_masks()
        v = v_ref[...]
        for pr in range(PP):
            ps = slice(pr * LANES, (pr + 1) * LANES)
            qc = jnp.concatenate([qn_ref[:, ps], qr_ref[:, ps]], axis=1)
            kc_s[:, pr * CW:(pr + 1) * CW] = jnp.concatenate([kn_ref[:, ps], kr_ref[...]], axis=1)
            for e in range(2):
                h = 2 * pr + e
                q_s[:, h * CW:(h + 1) * CW] = _pick(mm[e], qc)
                v_s[:, h * LANES:(h + 1) * LANES] = _pick(hm[e], v[:, ps])
        vis = _causal_mask(TQ, False)

        def q_block(i, carry):
            q0 = pl.multiple_of(i * TQ, TQ)
            qs = [q_s[pl.ds(q0, TQ), h * CW:(h + 1) * CW] for h in range(NH)]

            def tile(k0, c, mask):
                ms, ls, accs = list(c[:NH]), list(c[NH:2 * NH]), list(c[2 * NH:])
                ss = [_dot_nt(qs[h], kc_s[pl.ds(k0, TQ), (h // 2) * CW:(h // 2 + 1) * CW]) * scale for h in range(NH)]
                if mask is not None:
                    ss = [jnp.where(mask, s, NEG_BIG) for s in ss]
                m_new = [jnp.maximum(ms[h], jnp.max(ss[h], axis=1, keepdims=True)) for h in range(NH)]
                alphas = [jnp.exp(ms[h] - m_new[h]) for h in range(NH)]
                prs = [jnp.exp(ss[h] - m_new[h]) for h in range(NH)]
                outs = [_dot(prs[h].astype(BF16), v_s[pl.ds(k0, TQ), h * LANES:(h + 1) * LANES]) for h in range(NH)]
                ls = [alphas[h] * ls[h] + jnp.sum(prs[h], axis=1, keepdims=True) for h in range(NH)]
                for pr in range(PP):
                    accs[pr] = accs[pr] * jnp.where(hm[0], alphas[2 * pr], alphas[2 * pr + 1]) + outs[2 * pr] + outs[2 * pr + 1]
                return tuple(m_new) + tuple(ls) + tuple(accs)

            neg = jnp.full((TQ, 1), NEG_BIG, F32)
            z1 = jnp.zeros((TQ, 1), F32)

            def k_block(j, c):
                return tile(pl.multiple_of(j * TQ, TQ), c, None)

            c = lax.fori_loop(0, i, k_block, (neg,) * NH + (z1,) * NH + (jnp.zeros((TQ, LANES), F32),) * PP)
            c = tile(q0, c, vis)
            for pr in range(PP):
                ps = slice(pr * LANES, (pr + 1) * LANES)
                m_a, m_b, l_a, l_b = c[2 * pr], c[2 * pr + 1], c[NH + 2 * pr], c[NH + 2 * pr + 1]
                o_ref[pl.ds(q0, TQ), ps] = c[2 * NH + pr] / jnp.where(hm[0], l_a, l_b)
                lse_ref[pl.ds(q0, TQ), ps] = jnp.where(hm[0], m_a + jnp.log(l_a), m_b + jnp.log(l_b))
            return carry

        lax.fori_loop(0, nq, q_block, 0)

    blk = lambda off: pl.BlockSpec((seq, W), lambda b, g: (b, off + g))
    out_blk = pl.BlockSpec((seq, W), lambda b, g: (b, g))
    return pl.pallas_call(
        body, name=name, grid=(B, nstep),
        in_specs=[blk(0), blk(nstep), blk(0), blk(nstep), pl.BlockSpec((seq, LANES), lambda b, g: (b, 0))],
        out_specs=[out_blk, out_blk],
        out_shape=[jax.ShapeDtypeStruct((T, MLA_W), F32), jax.ShapeDtypeStruct((T, MLA_W), F32)],
        scratch_shapes=[pltpu.VMEM((seq, NH * CW), BF16), pltpu.VMEM((seq, PP * CW), BF16), pltpu.VMEM((seq, NH * LANES), BF16)],
        compiler_params=_cparams(("parallel", "parallel")),
    )(qm, qm, kvm, kvm, krt)


def _mla_bwd(qm, kvm, krt, o, lse, do, *, seq, name):
    T = qm.shape[0]
    B = T // seq
    TQ = ATT_T
    nq = seq // TQ
    PP = ATT_PAIRS
    W = PP * LANES
    nstep = MLA_W // W
    NH = 2 * PP
    CW = 2 * LANES
    scale = MLA_QK ** -0.5

    def body(qn_ref, qr_ref, kn_ref, v_ref, kr_ref, o_ref, lse_ref, do_ref,
             dqn_ref, dqr_ref, dkn_ref, dv_ref, dkr_ref, q_s, kc_s, do_s, dkc_s, dv_s):
        hm = _head_masks()
        mm = _mla_masks()
        dof = do_ref[...]
        for pr in range(PP):
            ps = slice(pr * LANES, (pr + 1) * LANES)
            qc = jnp.concatenate([qn_ref[:, ps], qr_ref[:, ps]], axis=1)
            kc_s[:, pr * CW:(pr + 1) * CW] = jnp.concatenate([kn_ref[:, ps], kr_ref[...]], axis=1)
            for e in range(2):
                h = 2 * pr + e
                q_s[:, h * CW:(h + 1) * CW] = _pick(mm[e], qc)
                do_s[:, h * LANES:(h + 1) * LANES] = _pick(hm[e], dof[:, ps]).astype(BF16)
        dkc_s[...] = jnp.zeros_like(dkc_s)
        dv_s[...] = jnp.zeros_like(dv_s)
        vis = _causal_mask(TQ, False)

        def q_block(i, carry):
            q0 = pl.multiple_of(i * TQ, TQ)
            qs = [q_s[pl.ds(q0, TQ), h * CW:(h + 1) * CW] for h in range(NH)]
            dos = [do_s[pl.ds(q0, TQ), h * LANES:(h + 1) * LANES] for h in range(NH)]
            lse_t = lse_ref[pl.ds(q0, TQ), :]
            dd = do_ref[pl.ds(q0, TQ), :] * o_ref[pl.ds(q0, TQ), :]
            lses, ds_ = [], []
            for h in range(NH):
                ps = slice((h // 2) * LANES, (h // 2 + 1) * LANES)
                lses.append(_lane_value(lse_t[:, ps], (h % 2) * MLA_V))
                ds_.append(jnp.sum(_pick(hm[h % 2], dd[:, ps]), axis=1, keepdims=True))

            def tile(k0, c, mask):
                accs = list(c)
                kcs = [kc_s[pl.ds(k0, TQ), pr * CW:(pr + 1) * CW] for pr in range(PP)]
                vjs = [v_ref[pl.ds(k0, TQ), pr * LANES:(pr + 1) * LANES] for pr in range(PP)]
                ss = [_dot_nt(qs[h], kcs[h // 2]) * scale for h in range(NH)]
                dps = [_dot_nt(dos[h], vjs[h // 2]) for h in range(NH)]
                p_l, ds_l = [], []
                for h in range(NH):
                    pr_ = jnp.exp(ss[h] - lses[h])
                    if mask is not None:
                        pr_ = jnp.where(mask, pr_, 0.0)
                    p_l.append(pr_.astype(BF16))
                    ds_l.append((pr_ * (dps[h] - ds_[h]) * scale).astype(BF16))
                for h in range(NH):
                    accs[h] = accs[h] + _dot(ds_l[h], kcs[h // 2])
                for pr in range(PP):
                    ha, hb = 2 * pr, 2 * pr + 1
                    dkc_s[pl.ds(k0, TQ), pr * CW:(pr + 1) * CW] += _dot_tn(ds_l[ha], qs[ha]) + _dot_tn(ds_l[hb], qs[hb])
                    dv_s[pl.ds(k0, TQ), pr * LANES:(pr + 1) * LANES] += _dot_tn(p_l[ha], dos[ha]) + _dot_tn(p_l[hb], dos[hb])
                return tuple(accs)

            zc = jnp.zeros((TQ, CW), F32)

            def k_block(j, c):
                return tile(pl.multiple_of(j * TQ, TQ), c, None)

            c = lax.fori_loop(0, i, k_block, (zc,) * NH)
            c = tile(q0, c, vis)
            for pr in range(PP):
                ps = slice(pr * LANES, (pr + 1) * LANES)
                dq = _pick(mm[0], c[2 * pr]) + _pick(mm[1], c[2 * pr + 1])
                dqn_ref[pl.ds(q0, TQ), ps] = dq[:, :LANES].astype(BF16)
                dqr_ref[pl.ds(q0, TQ), ps] = dq[:, LANES:]
            return carry

        lax.fori_loop(0, nq, q_block, 0)
        dkr = dkc_s[:, LANES:CW]
        for pr in range(PP):
            dkn_ref[:, pr * LANES:(pr + 1) * LANES] = dkc_s[:, pr * CW:pr * CW + LANES].astype(BF16)
            if pr > 0:
                dkr = dkr + dkc_s[:, pr * CW + LANES:(pr + 1) * CW]
        dv_ref[...] = dv_s[...].astype(BF16)
        g = pl.program_id(1)

        @pl.when(g == 0)
        def _():
            dkr_ref[...] = dkr

        @pl.when(g > 0)
        def _():
            dkr_ref[...] += dkr

    blk = lambda off: pl.BlockSpec((seq, W), lambda b, g: (b, off + g))
    out_blk = pl.BlockSpec((seq, W), lambda b, g: (b, g))
    one_blk = pl.BlockSpec((seq, LANES), lambda b, g: (b, 0))
    return pl.pallas_call(
        body, name=name, grid=(B, nstep),
        in_specs=[blk(0), blk(nstep), blk(0), blk(nstep), one_blk, out_blk, out_blk, out_blk],
        out_specs=[out_blk, out_blk, out_blk, out_blk, one_blk],
        out_shape=[jax.ShapeDtypeStruct((T, MLA_W), BF16), jax.ShapeDtypeStruct((T, MLA_W), F32),
                   jax.ShapeDtypeStruct((T, MLA_W), BF16), jax.ShapeDtypeStruct((T, MLA_W), BF16),
                   jax.ShapeDtypeStruct((T, LANES), F32)],
        scratch_shapes=[pltpu.VMEM((seq, NH * CW), BF16), pltpu.VMEM((seq, PP * CW), BF16), pltpu.VMEM((seq, NH * LANES), BF16),
                        pltpu.VMEM((seq, PP * CW), F32), pltpu.VMEM((seq, W), F32)],
        compiler_params=_cparams(("parallel", "arbitrary")),
    )(qm, qm, kvm, kvm, krt, o, lse, do)


def _rope_tables(pos_ref, invf_ref):
    ang = pos_ref[...].astype(F32) * invf_ref[...]
    first = (_lane_iota() % MLA_ROPE) < (MLA_ROPE // 2)
    return jnp.cos(ang), jnp.sin(ang), first


def _rope_apply(x, cos, sin, first):
    rot = jnp.where(first, -pltpu.roll(x, LANES - MLA_ROPE // 2, 1), pltpu.roll(x, MLA_ROPE // 2, 1))
    return x * cos + rot * sin


def _rope_apply_t(dy, cos, sin, first):
    dys = dy * sin
    rot_t = jnp.where(first, pltpu.roll(dys, LANES - MLA_ROPE // 2, 1), -pltpu.roll(dys, MLA_ROPE // 2, 1))
    return dy * cos + rot_t


def _rope_fwd(qfull, p, pos, invf, *, tm, name):
    T = qfull.shape[0]
    ntile = MLA_W // LANES

    def body(q_ref, kr_ref, pos_ref, invf_ref, qm_ref, krt_ref):
        cos, sin, first = _rope_tables(pos_ref, invf_ref)
        qm_ref[:, :MLA_W] = q_ref[:, :MLA_W].astype(BF16)
        for t in range(ntile):
            sl = slice(MLA_W + t * LANES, MLA_W + (t + 1) * LANES)
            qm_ref[:, sl] = _rope_apply(q_ref[:, sl], cos, sin, first).astype(BF16)
        krt_ref[...] = _rope_apply(kr_ref[...], cos, sin, first).astype(BF16)

    return pl.pallas_call(
        body, name=name, grid=(T // tm,),
        in_specs=[pl.BlockSpec((tm, 2 * MLA_W), lambda i: (i, 0)), pl.BlockSpec((tm, LANES), lambda i: (i, P_KRT // LANES)),
                  pl.BlockSpec((tm, 1), lambda i: (i, 0)), pl.BlockSpec((1, LANES), lambda i: (0, 0))],
        out_specs=[pl.BlockSpec((tm, 2 * MLA_W), lambda i: (i, 0)), pl.BlockSpec((tm, LANES), lambda i: (i, 0))],
        out_shape=[jax.ShapeDtypeStruct((T, 2 * MLA_W), BF16), jax.ShapeDtypeStruct((T, LANES), BF16)],
        compiler_params=_cparams(("parallel",)),
    )(qfull, p, pos, invf)


def _rope_bwd(dqr, dkr, pos, invf, *, tm, name):
    T = dqr.shape[0]
    ntile = MLA_W // LANES

    def body(dq_ref, dk_ref, pos_ref, invf_ref, oq_ref, ok_ref):
        cos, sin, first = _rope_tables(pos_ref, invf_ref)
        for t in range(ntile):
            sl = slice(t * LANES, (t + 1) * LANES)
            oq_ref[:, sl] = _rope_apply_t(dq_ref[:, sl], cos, sin, first).astype(BF16)
        ok_ref[...] = _rope_apply_t(dk_ref[...], cos, sin, first).astype(BF16)

    return pl.pallas_call(
        body, name=name, grid=(T // tm,),
        in_specs=[pl.BlockSpec((tm, MLA_W), lambda i: (i, 0)), pl.BlockSpec((tm, LANES), lambda i: (i, 0)),
                  pl.BlockSpec((tm, 1), lambda i: (i, 0)), pl.BlockSpec((1, LANES), lambda i: (0, 0))],
        out_specs=[pl.BlockSpec((tm, MLA_W), lambda i: (i, 0)), pl.BlockSpec((tm, LANES), lambda i: (i, 0))],
        out_shape=[jax.ShapeDtypeStruct((T, MLA_W), BF16), jax.ShapeDtypeStruct((T, LANES), BF16)],
        compiler_params=_cparams(("parallel",)),
    )(dqr, dkr, pos, invf)


CONV_ROWS = 256
HALO = 8


def _conv_taps(w_ref):
    return w_ref[0:1, :], w_ref[1:2, :], w_ref[2:3, :]


def _conv_rows(cur, prev, w, bias):
    ext = jnp.concatenate([prev, cur], axis=0)
    u1 = pltpu.roll(ext, 1, 0)[HALO:]
    u2 = pltpu.roll(ext, 2, 0)[HALO:]
    return w[2] * cur + w[1] * u1 + w[0] * u2 + bias, u1, u2


def _conv_fwd(u, w, bias, *, seq, name):
    T = u.shape[0]
    B = T // seq
    W2 = 2 * FF_BLK

    def body(u_ref, w_ref, b_ref, a_ref):
        wv = _conv_taps(w_ref)
        bv = b_ref[...]
        for c in range(seq // CONV_ROWS):
            r0 = c * CONV_ROWS
            cur = u_ref[r0:r0 + CONV_ROWS, :]
            prev = u_ref[r0 - HALO:r0, :] if c > 0 else jnp.zeros((HALO, W2), F32)
            y, _, _ = _conv_rows(cur, prev, wv, bv)
            gc = y[:, :FF_BLK]
            a_ref[r0:r0 + CONV_ROWS, :] = (gc * (1.0 / (1.0 + jnp.exp(-gc))) * y[:, FF_BLK:]).astype(BF16)

    return pl.pallas_call(
        body, name=name, grid=(B, N_FF_BLK),
        in_specs=[pl.BlockSpec((seq, W2), lambda b, j: (b, j)), pl.BlockSpec((3, W2), lambda b, j: (0, j)),
                  pl.BlockSpec((1, W2), lambda b, j: (0, j))],
        out_specs=pl.BlockSpec((seq, FF_BLK), lambda b, j: (b, j)),
        out_shape=jax.ShapeDtypeStruct((T, D_FF), BF16),
        compiler_params=_cparams(("parallel", "parallel")),
    )(u, w, bias)


def _conv_bwd(u, da, w, bias, *, seq, name):
    T = u.shape[0]
    B = T // seq
    W2 = 2 * FF_BLK
    nchunk = seq // CONV_ROWS

    def body(u_ref, da_ref, w_ref, b_ref, du_ref, dw_ref, db_ref, duc_s):
        wv = _conv_taps(w_ref)
        bv = b_ref[...]
        zrow = jnp.zeros((1, W2), F32)
        dw0, dw1, dw2, dbs = zrow, zrow, zrow, zrow
        for c in range(nchunk):
            r0 = c * CONV_ROWS
            cur = u_ref[r0:r0 + CONV_ROWS, :]
            prev = u_ref[r0 - HALO:r0, :] if c > 0 else jnp.zeros((HALO, W2), F32)
            y, u1, u2 = _conv_rows(cur, prev, wv, bv)
            gc = y[:, :FF_BLK]
            vc = y[:, FF_BLK:]
            sg = 1.0 / (1.0 + jnp.exp(-gc))
            dav = da_ref[r0:r0 + CONV_ROWS, :]
            duc = jnp.concatenate([dav * vc * (sg * (1.0 + gc * (1.0 - sg))), dav * (gc * sg)], axis=1)
            duc_s[r0:r0 + CONV_ROWS, :] = duc
            dw0 = dw0 + jnp.sum(duc * u2, axis=0, keepdims=True)
            dw1 = dw1 + jnp.sum(duc * u1, axis=0, keepdims=True)
            dw2 = dw2 + jnp.sum(duc * cur, axis=0, keepdims=True)
            dbs = dbs + jnp.sum(duc, axis=0, keepdims=True)
        duc_s[seq:seq + HALO, :] = jnp.zeros((HALO, W2), F32)
        n_ext = CONV_ROWS + HALO
        for c in range(nchunk):
            r0 = c * CONV_ROWS
            ext = duc_s[r0:r0 + n_ext, :]
            s1 = pltpu.roll(ext, n_ext - 1, 0)[:CONV_ROWS]
            s2 = pltpu.roll(ext, n_ext - 2, 0)[:CONV_ROWS]
            du_ref[r0:r0 + CONV_ROWS, :] = (wv[2] * ext[:CONV_ROWS] + wv[1] * s1 + wv[0] * s2).astype(BF16)

        first = pl.program_id(1) == 0

        @pl.when(first)
        def _():
            dw_ref[0:1, :] = dw0
            dw_ref[1:2, :] = dw1
            dw_ref[2:3, :] = dw2
            db_ref[...] = dbs

        @pl.when(jnp.logical_not(first))
        def _():
            dw_ref[0:1, :] += dw0
            dw_ref[1:2, :] += dw1
            dw_ref[2:3, :] += dw2
            db_ref[...] += dbs

    return pl.pallas_call(
        body, name=name, grid=(N_FF_BLK, B),
        in_specs=[pl.BlockSpec((seq, W2), lambda j, b: (b, j)), pl.BlockSpec((seq, FF_BLK), lambda j, b: (b, j)),
                  pl.BlockSpec((3, W2), lambda j, b: (0, j)), pl.BlockSpec((1, W2), lambda j, b: (0, j))],
        out_specs=[pl.BlockSpec((seq, W2), lambda j, b: (b, j)), pl.BlockSpec((3, W2), lambda j, b: (0, j)),
                   pl.BlockSpec((1, W2), lambda j, b: (0, j))],
        out_shape=[jax.ShapeDtypeStruct((T, 2 * D_FF), BF16), jax.ShapeDtypeStruct((3, 2 * D_FF), F32),
                   jax.ShapeDtypeStruct((1, 2 * D_FF), F32)],
        scratch_shapes=[pltpu.VMEM((seq + HALO, W2), F32)],
        compiler_params=_cparams(("parallel", "arbitrary")),
    )(u, da, w, bias)


def _place():
    return lax.axis_index("x"), lax.axis_index("y"), lax.axis_index("c")


def _other_chips(x, y):
    return [(1 - x, y), (x, 1 - y), (1 - x, 1 - y)]


def _all_gather(vs, *, name):
    n = len(vs)

    def body(*refs):
        v_refs, out_refs = refs[:n], refs[n:2 * n]
        send_sems, recv_sems, local_sems = refs[2 * n:]
        x, y, c = _place()
        me, sibling = (x, y, c), (x, y, 1 - c)
        chips = _other_chips(x, y)

        def slab(a, px, py, pc):
            return out_refs[a].at[4 * px + 2 * py + pc]

        def copy(a, k, block, to, src=None):
            return pltpu.make_async_remote_copy(
                src_ref=slab(a, *block) if src is None else src, dst_ref=slab(a, *block),
                send_sem=send_sems.at[7 * a + k], recv_sem=recv_sems.at[7 * a + k], device_id=to, device_id_type=MESH)

        mine = [pltpu.make_async_copy(v_refs[a], slab(a, *me), local_sems.at[a]) for a in range(n)]
        for cp in mine:
            cp.start()
        first = []
        for a in range(n):
            first.append(copy(a, 0, me, sibling, src=v_refs[a]))
            first += [copy(a, 1 + j, me, (*chip, c), src=v_refs[a]) for j, chip in enumerate(chips)]
        for cp in first:
            cp.start()
        passed = []
        for j, chip in enumerate(chips):
            for a in range(n):
                copy(a, 1 + j, (*chip, c), me).wait_recv()
                cp = copy(a, 4 + j, (*chip, c), sibling)
                cp.start()
                passed.append(cp)
        for a in range(n):
            copy(a, 0, sibling, me).wait_recv()
            for j, chip in enumerate(chips):
                copy(a, 4 + j, (*chip, 1 - c), me).wait_recv()
        for cp in first + passed:
            cp.wait_send()
        for cp in mine:
            cp.wait()

    return pl.pallas_call(
        body, name=name, in_specs=[ANY] * n, out_specs=[ANY] * n,
        out_shape=[jax.ShapeDtypeStruct((N_DEV,) + v.shape, v.dtype) for v in vs],
        scratch_shapes=[pltpu.SemaphoreType.DMA((7 * n,)), pltpu.SemaphoreType.DMA((7 * n,)), pltpu.SemaphoreType.DMA((n,))],
    )(*vs)


def _all_gather_async(vs, *, name, collective_id):
    n = len(vs)
    v_refs = [jax.new_ref(v, memory_space=pltpu.MemorySpace.HBM) for v in vs]
    out_refs = [jax.empty_ref(jax.ShapeDtypeStruct((N_DEV,) + v.shape, v.dtype), memory_space=pltpu.MemorySpace.HBM)
                for v in vs]

    @pl.kernel(mesh=plsc.ScalarSubcoreMesh(axis_name="seq", num_cores=1), name=name,
               scratch_types=(pltpu.SemaphoreType.DMA((7 * n,)), pltpu.SemaphoreType.DMA((7 * n,)),
                              pltpu.SemaphoreType.DMA((n,))),
               compiler_params=pltpu.CompilerParams(collective_id=collective_id))
    def launch(send_sems, recv_sems, local_sems):
        x, y, c = _place()
        me, sibling = (x, y, c), (x, y, 1 - c)
        chips = _other_chips(x, y)
        peers = [sibling] + [(*chip, c) for chip in chips]
        barrier = pltpu.get_barrier_semaphore()
        for peer in peers:
            pl.semaphore_signal(barrier, inc=1, device_id=peer, device_id_type=MESH)
        pl.semaphore_wait(barrier, len(peers))

        def slab(a, px, py, pc):
            return out_refs[a].at[4 * px + 2 * py + pc]

        def copy(a, k, block, to, src=None):
            return pltpu.make_async_remote_copy(
                src_ref=slab(a, *block) if src is None else src, dst_ref=slab(a, *block),
                send_sem=send_sems.at[7 * a + k], recv_sem=recv_sems.at[7 * a + k], device_id=to, device_id_type=MESH)

        mine = [pltpu.make_async_copy(v_refs[a], slab(a, *me), local_sems.at[a]) for a in range(n)]
        for cp in mine:
            cp.start()
        first = []
        for a in range(n):
            first.append(copy(a, 0, me, sibling, src=v_refs[a]))
            first += [copy(a, 1 + j, me, (*chip, c), src=v_refs[a]) for j, chip in enumerate(chips)]
        for cp in first:
            cp.start()
        passed = []
        for j, chip in enumerate(chips):
            for a in range(n):
                copy(a, 1 + j, (*chip, c), me).wait_recv()
                cp = copy(a, 4 + j, (*chip, c), sibling)
                cp.start()
                passed.append(cp)
        for a in range(n):
            copy(a, 0, sibling, me).wait_recv()
            for j, chip in enumerate(chips):
                copy(a, 4 + j, (*chip, 1 - c), me).wait_recv()
        for cp in first + passed:
            cp.wait_send()
        for cp in mine:
            cp.wait()

    launch()
    return [r[...] for r in out_refs]


def _rs_sibling(g8s, *, name):
    n = len(g8s)

    def body(*refs):
        g_refs, out_refs = refs[:n], refs[n:2 * n]
        send_sems, recv_sems = refs[2 * n:]
        x, y, c = _place()
        copies = [
            pltpu.make_async_remote_copy(
                src_ref=g_refs[a].at[2 * k + 1 - c], dst_ref=out_refs[a].at[k],
                send_sem=send_sems.at[4 * a + k], recv_sem=recv_sems.at[4 * a + k],
                device_id=(x, y, 1 - c), device_id_type=MESH)
            for a in range(n) for k in range(4)]
        for cp in copies:
            cp.start()
        for cp in copies:
            cp.wait()

    return pl.pallas_call(
        body, name=name, in_specs=[ANY] * n, out_specs=[ANY] * n,
        out_shape=[jax.ShapeDtypeStruct((4,) + g.shape[1:], g.dtype) for g in g8s],
        scratch_shapes=[pltpu.SemaphoreType.DMA((4 * n,)), pltpu.SemaphoreType.DMA((4 * n,))],
    )(*g8s)


def _row_tile(rows):
    return rows if rows <= 512 else 256


def _rs_chip_sum(g8, from_sibling, place_idx, *, name):
    _, R, C = g8.shape
    tr = _row_tile(R)

    def body(pi_ref, a_ref, b_ref, f_ref, h_ref):
        s = a_ref[...] + b_ref[...]
        h_ref[...] = s.astype(BF16)

        @pl.when(pl.program_id(1) == pi_ref[1])
        def _():
            f_ref[...] = s

    blk = pl.BlockSpec((None, tr, C), lambda r, k, pi_ref: (k, r, 0))
    return pl.pallas_call(
        body, name=name,
        grid_spec=pltpu.PrefetchScalarGridSpec(
            num_scalar_prefetch=1, grid=(R // tr, 4),
            in_specs=[pl.BlockSpec((None, tr, C), lambda r, k, pi_ref: (2 * k + pi_ref[0], r, 0)), blk],
            out_specs=[pl.BlockSpec((tr, C), lambda r, k, pi_ref: (r, 0)), blk]),
        out_shape=[jax.ShapeDtypeStruct((R, C), F32), jax.ShapeDtypeStruct((4, R, C), BF16)],
        compiler_params=_cparams(("parallel", "arbitrary")),
    )(place_idx, g8, from_sibling)


def _rs_chips(h4s, *, name):
    n = len(h4s)

    def body(*refs):
        h_refs, out_refs = refs[:n], refs[n:2 * n]
        send_sems, recv_sems = refs[2 * n:]
        x, y, c = _place()
        copies = [
            pltpu.make_async_remote_copy(
                src_ref=h_refs[a].at[2 * cx + cy], dst_ref=out_refs[a].at[j],
                send_sem=send_sems.at[3 * a + j], recv_sem=recv_sems.at[3 * a + j],
                device_id=(cx, cy, c), device_id_type=MESH)
            for a in range(n) for j, (cx, cy) in enumerate(_other_chips(x, y))]
        for cp in copies:
            cp.start()
        for cp in copies:
            cp.wait()

    return pl.pallas_call(
        body, name=name, in_specs=[ANY] * n, out_specs=[ANY] * n,
        out_shape=[jax.ShapeDtypeStruct((3,) + h.shape[1:], h.dtype) for h in h4s],
        scratch_shapes=[pltpu.SemaphoreType.DMA((3 * n,)), pltpu.SemaphoreType.DMA((3 * n,))],
    )(*h4s)


def _split_moves(segments, chunk):
    moves = []
    for dst, src, length in segments:
        while length > 0:
            dev, off = divmod(src, chunk)
            take = min(length, chunk - off)
            moves.append((dst, dev, off, take))
            dst, src, length = dst + take, src + take, length - take
    return moves


def _assemble(stacked, segments, zero_spans, out_cols, *, name):
    _, R, c = stacked.shape
    tr = _row_tile(R)
    moves = _split_moves(segments, c)

    def body(x_ref, o_ref):
        for dst, dev, off, take in moves:
            o_ref[:, dst:dst + take] = x_ref[dev, :, off:off + take]
        for a, b in zero_spans:
            o_ref[:, a:b] = jnp.zeros((tr, b - a), o_ref.dtype)

    return pl.pallas_call(
        body, name=name, grid=(R // tr,),
        in_specs=[pl.BlockSpec((N_DEV, tr, c), lambda i: (0, i, 0))],
        out_specs=pl.BlockSpec((tr, out_cols), lambda i: (i, 0)),
        out_shape=jax.ShapeDtypeStruct((R, out_cols), stacked.dtype),
        compiler_params=_cparams(("parallel",)),
    )(stacked)


def _disassemble(full, segments, chunk, *, name):
    R = full.shape[0]
    tr = _row_tile(R)
    moves = _split_moves(segments, chunk)

    def body(x_ref, o_ref):
        seen = set()
        for dst, dev, off, take in moves:
            piece = x_ref[:, dst:dst + take]
            if (dev, off) in seen:
                piece = piece + o_ref[dev, :, off:off + take]
            seen.add((dev, off))
            o_ref[dev, :, off:off + take] = piece

    return pl.pallas_call(
        body, name=name, grid=(R // tr,),
        in_specs=[pl.BlockSpec((tr, full.shape[1]), lambda i: (i, 0))],
        out_specs=pl.BlockSpec((N_DEV, tr, chunk), lambda i: (0, i, 0)),
        out_shape=jax.ShapeDtypeStruct((N_DEV, R, chunk), F32),
        compiler_params=_cparams(("parallel",)),
    )(full)


_O_CQ = 3 * SB_W
_O_CKV = _O_CQ + Q_LORA
_O_KR = _O_CKV + KV_LORA
SEG_W_IN = ((0, 0, 3 * SB_W), (P_CKV, _O_CKV, KV_LORA), (P_KRT, _O_KR, MLA_ROPE), (P_KRT + MLA_ROPE, _O_KR, MLA_ROPE),
            (P_CQ, _O_CQ, Q_LORA))
ZERO_W_IN = ((P_KRT + 2 * MLA_ROPE, P_CQ),)
SEG_W_UQ = tuple((MLA_NOPE * h, MLA_QK * h, MLA_NOPE) for h in range(MLA_HEADS)) + tuple(
    (MLA_W + LANES * (h // 2) + MLA_ROPE * (h % 2), MLA_QK * h + MLA_NOPE, MLA_ROPE) for h in range(MLA_HEADS))
ZERO_W_UQ = tuple((MLA_W + LANES * g + 2 * MLA_ROPE, MLA_W + LANES * (g + 1)) for g in range(MLA_HEADS // 2))
SEG_W_UKV = tuple((MLA_NOPE * h, (MLA_NOPE + MLA_V) * h, MLA_NOPE) for h in range(MLA_HEADS)) + tuple(
    (MLA_W + MLA_V * h, (MLA_NOPE + MLA_V) * h + MLA_NOPE, MLA_V) for h in range(MLA_HEADS))
SEG_W_UP = tuple((2 * FF_BLK * blk + FF_BLK * half, D_FF * half + FF_BLK * blk, FF_BLK)
                 for half in range(2) for blk in range(N_FF_BLK))


def _sum8(g, *, name):
    _, R, C = g.shape

    def body(g_ref, o_ref):
        acc = g_ref[0]
        for k in range(1, N_DEV):
            acc = acc + g_ref[k]
        o_ref[...] = acc

    return pl.pallas_call(
        body, name=name, out_shape=jax.ShapeDtypeStruct((R, C), F32),
    )(g)


def _adamw_math(w, gf, m, v):
    c1 = 1.0 / (1.0 - ADAM_B1 ** ADAM_STEP)
    c2 = 1.0 / (1.0 - ADAM_B2 ** ADAM_STEP)
    mn = ADAM_B1 * m + (1.0 - ADAM_B1) * gf
    vn = ADAM_B2 * v + (1.0 - ADAM_B2) * (gf * gf)
    return -ADAM_LR * ((mn * c1) / (jnp.sqrt(vn * c2) + ADAM_EPS) + ADAM_WD * w), mn, vn


def _adamw(w, g, m, v, *, name):
    R, C = w.shape
    tr = _row_tile(R)

    def body(w_ref, g_ref, m_ref, v_ref, d_ref, mo_ref, vo_ref):
        d_ref[...], mo_ref[...], vo_ref[...] = _adamw_math(w_ref[...], g_ref[...], m_ref[...], v_ref[...])

    blk = pl.BlockSpec((tr, C), lambda i: (i, 0))
    shp = jax.ShapeDtypeStruct((R, C), F32)
    return pl.pallas_call(
        body, name=name, grid=(R // tr,), in_specs=[blk] * 4, out_specs=[blk] * 3,
        out_shape=[shp, shp, shp], compiler_params=_cparams(("parallel",)),
    )(w, g, m, v)


def _adamw_rs(own, r3, w, m, v, *, name):
    R, C = w.shape
    tr = _row_tile(R)

    def body(f_ref, r_ref, w_ref, m_ref, v_ref, g_ref, d_ref, mo_ref, vo_ref):
        gf = ((f_ref[...] + r_ref[0].astype(F32)) + r_ref[1].astype(F32)) + r_ref[2].astype(F32)
        g_ref[...] = gf
        d_ref[...], mo_ref[...], vo_ref[...] = _adamw_math(w_ref[...], gf, m_ref[...], v_ref[...])

    blk = pl.BlockSpec((tr, C), lambda i: (i, 0))
    shp = jax.ShapeDtypeStruct((R, C), F32)
    return pl.pallas_call(
        body, name=name, grid=(R // tr,),
        in_specs=[blk, pl.BlockSpec((3, tr, C), lambda i: (0, i, 0)), blk, blk, blk], out_specs=[blk] * 4,
        out_shape=[shp] * 4, compiler_params=_cparams(("parallel",)),
    )(own, r3, w, m, v)


def _ff_interleave(a):
    lead = a.shape[:-1]
    return a.reshape(*lead, 2, N_FF_BLK, FF_BLK).swapaxes(-3, -2).reshape(*lead, 2 * D_FF)


def _ff_deinterleave(a):
    lead = a.shape[:-1]
    return a.reshape(*lead, N_FF_BLK, 2, FF_BLK).swapaxes(-3, -2).reshape(*lead, 2 * D_FF)


SMALL =(("g_mix", D_MODEL), ("g_cq", Q_LORA), ("g_ckv", KV_LORA), ("g_sb_out", SB_W), ("g_mla_out", MLA_W),
         ("g_ffn", D_MODEL), ("conv_b", 2 * D_FF), ("g_final", D_MODEL))
SMALL_ROWS = 88


def _pack_small(d):
    flat = jnp.concatenate([d[n].reshape(-1) for n, _ in SMALL])
    flat = jnp.pad(flat, (0, SMALL_ROWS * LANES - flat.shape[0]))
    return flat.reshape(SMALL_ROWS, LANES)


def _unpack_small(a):
    flat = a.reshape(-1)
    out, off = {}, 0
    for n, size in SMALL:
        out[n] = flat[off:off + size]
        off += size
    return out


def kernel(x, positions, g_mix, w_in, g_cq, w_uq, g_ckv, w_ukv, g_sb_out, g_mla_out, w_out, g_ffn, w_up, conv_w, conv_b, w_down, g_final, loss_target, m_g_mix, m_w_in, m_g_cq, m_w_uq, m_g_ckv, m_w_ukv, m_g_sb_out, m_g_mla_out, m_w_out, m_g_ffn, m_w_up, m_conv_w, m_conv_b, m_w_down, m_g_final, v_g_mix, v_w_in, v_g_cq, v_w_uq, v_g_ckv, v_w_ukv, v_g_sb_out, v_g_mla_out, v_w_out, v_g_ffn, v_w_up, v_conv_w, v_conv_b, v_w_down, v_g_final):
    B, S, D = x.shape
    T = B * S
    xf = x.reshape(T, D)
    tgt = loss_target.reshape(T, D)
    pos = positions.reshape(T, 1)
    half = MLA_ROPE // 2
    inv_freq = 1.0 / (ROPE_BASE ** (jnp.arange(half, dtype=F32) * (2.0 / MLA_ROPE)))
    invf = jnp.tile(inv_freq, LANES // half).reshape(1, LANES)
    place_idx = jnp.stack([lax.axis_index("c"), 2 * lax.axis_index("x") + lax.axis_index("y")]).astype(jnp.int32)

    names = ("w_in", "w_uq", "w_ukv", "w_out", "w_up", "w_down", "conv_w")
    shard = {"w_in": w_in[0], "w_uq": w_uq[0], "w_ukv": w_ukv[0], "w_out": w_out[0], "w_up": w_up[0],
             "w_down": w_down[0], "conv_w": conv_w[0]}
    sent = {n: shard[n] if n == "conv_w" else shard[n].astype(BF16) for n in names}
    got = {"w_in": _all_gather([sent["w_in"]], name="ag_w_in")[0]}
    later = names[1:]
    got.update(zip(later, _all_gather_async([sent[n] for n in later], name="ag_weights_async", collective_id=0)))
    wi = _assemble(got["w_in"], SEG_W_IN, ZERO_W_IN, P_COLS, name="asm_w_in")
    wuq = _assemble(got["w_uq"], SEG_W_UQ, ZERO_W_UQ, 2 * MLA_W, name="asm_w_uq")
    wukv = _assemble(got["w_ukv"], SEG_W_UKV, (), 2 * MLA_W, name="asm_w_ukv")
    wup = _assemble(got["w_up"], SEG_W_UP, (), 2 * D_FF, name="asm_w_up")
    cwi = _assemble(got["conv_w"], SEG_W_UP, (), 2 * D_FF, name="asm_conv_w")
    wo = got["w_out"].reshape(D, D)
    wdn = got["w_down"].reshape(D_FF, D)
    cbi = _ff_interleave(conv_b)

    h = _rms_fwd(xf, g_mix, tm=512, name="rms_mix")
    p = _matmul_nn(h, wi, tm=1024, tn=P_COLS // 2, out_dtype=F32, name="proj_in")
    o_sb, ltot = _sb_fwd(p, seq=S, name="sb_fwd")
    cq = _rms_fwd(p, g_cq, tm=512, name="rms_cq", col_block=P_CQ // Q_LORA)
    ckv = _rms_fwd(p, g_ckv, tm=512, name="rms_ckv", col_block=P_CKV // KV_LORA)
    qfull = _matmul_nn(cq, wuq, tm=512, tn=1024, out_dtype=F32, name="proj_uq")
    kvm = _matmul_nn(ckv, wukv, tm=512, tn=1024, out_dtype=BF16, name="proj_ukv")
    qm, krt = _rope_fwd(qfull, p, pos, invf, tm=512, name="rope_fwd")
    o_mla, lse = _mla_fwd(qm, kvm, krt, seq=S, name="mla_fwd")
    ocat = _rms2_fwd(o_sb, o_mla, g_sb_out, g_mla_out, tm=512, name="rms_heads")
    x1 = _matmul_nn(ocat, wo, tm=512, tn=1024, out_dtype=F32, name="proj_out", residual=xf)
    hf = _rms_fwd(x1, g_ffn, tm=512, name="rms_ffn")
    u = _matmul_nn(hf, wup, tm=1024, tn=512, out_dtype=F32, name="ffn_up")
    a = _conv_fwd(u, cwi, cbi, seq=S, name="conv_fwd")
    x2 = _matmul_nn(a, wdn, tm=512, tn=1024, out_dtype=F32, name="ffn_down", residual=x1)
    dx2, dg_final, loss_row = _final_loss(x2, g_final.reshape(1, D), tgt, tm=512, name="final_loss")

    da = _matmul_nt(dx2, wdn, tm=1024, tn=D_FF // 2, out_dtype=F32, name="d_ffn_down")
    dw_down = _matmul_tn(a, dx2, tm=D_FF // 2, tn=1024, tk=1024, name="dw_down")
    du, dcw, dcb = _conv_bwd(u, da, cwi, cbi, seq=S, name="conv_bwd")
    dhf = _matmul_nt(du, wup, tm=512, tn=512, out_dtype=F32, name="d_ffn_up")
    dw_up = _matmul_tn(hf, du, tm=1024, tn=D_FF, tk=1024, name="dw_up")
    dx1, dg_ffn = _rms_bwd(dhf, x1, g_ffn, tm=512, name="rms_ffn_bwd", residual=dx2)
    docat = _matmul_nt(dx1, wo, tm=512, tn=1024, out_dtype=F32, name="d_proj_out")
    dw_out = _matmul_tn(ocat, dx1, tm=1024, tn=1024, tk=1024, name="dw_out")
    do_sb, do_mla, dg_sb, dg_mla = _rms2_bwd(docat, o_sb, o_mla, g_sb_out, g_mla_out, tm=512, name="rms_heads_bwd")
    dq_sb, dk_sb, dv_sb = _sb_bwd(p, ltot, do_sb, seq=S, name="sb_bwd")
    dqn, dqr, dkn, dvm, dkr = _mla_bwd(qm, kvm, krt, o_mla, lse, do_mla, seq=S, name="mla_bwd")
    dqr_u, dkr_u = _rope_bwd(dqr, dkr, pos, invf, tm=512, name="rope_bwd")
    dqm = jnp.concatenate([dqn, dqr_u], axis=1)
    dkvm = jnp.concatenate([dkn, dvm], axis=1)
    dcq_n = _matmul_nt(dqm, wuq, tm=512, tn=Q_LORA, out_dtype=F32, name="d_proj_uq")
    dw_uq = _matmul_tn(cq, dqm, tm=Q_LORA, tn=1024, tk=1024, name="dw_uq")
    dckv_n = _matmul_nt(dkvm, wukv, tm=512, tn=KV_LORA, out_dtype=F32, name="d_proj_ukv")
    dw_ukv = _matmul_tn(ckv, dkvm, tm=KV_LORA, tn=1024, tk=1024, name="dw_ukv")
    dcq, dg_cq = _rms_bwd(dcq_n, p, g_cq, tm=512, name="rms_cq_bwd", col_block=P_CQ // Q_LORA, out_dtype=BF16)
    dckv, dg_ckv = _rms_bwd(dckv_n, p, g_ckv, tm=512, name="rms_ckv_bwd", col_block=P_CKV // KV_LORA, out_dtype=BF16)
    dp = jnp.concatenate([dq_sb, dk_sb, dv_sb, dckv, dkr_u, dcq], axis=1)
    dh = _matmul_nt(dp, wi, tm=512, tn=1024, out_dtype=F32, name="d_proj_in")
    dw_in = _matmul_tn(h, dp, tm=1024, tn=P_COLS, tk=1024, name="dw_in")
    dx, dg_mix = _rms_bwd(dh, xf, g_mix, tm=512, name="rms_mix_bwd", residual=dx1)

    g8 = {"w_in": _disassemble(dw_in, SEG_W_IN, shard["w_in"].shape[1], name="split_dw_in"),
          "w_uq": _disassemble(dw_uq, SEG_W_UQ, shard["w_uq"].shape[1], name="split_dw_uq"),
          "w_ukv": _disassemble(dw_ukv, SEG_W_UKV, shard["w_ukv"].shape[1], name="split_dw_ukv"),
          "w_up": _disassemble(dw_up, SEG_W_UP, shard["w_up"].shape[1], name="split_dw_up"),
          "conv_w": _disassemble(dcw, SEG_W_UP, shard["conv_w"].shape[1], name="split_dconv_w"),
          "w_out": dw_out.reshape((N_DEV,) + shard["w_out"].shape),
          "w_down": dw_down.reshape((N_DEV,) + shard["w_down"].shape)}
    from_sib = _rs_sibling([g8[n] for n in names], name="rs_sibling")
    sums = [_rs_chip_sum(g8[n], fs, place_idx, name="rs_chip_sum_" + n) for n, fs in zip(names, from_sib)]
    r3 = dict(zip(names, _rs_chips([h4 for _, h4 in sums], name="rs_chips")))
    own = {n: f for n, (f, _) in zip(names, sums)}

    small_part = {"g_mix": dg_mix, "g_cq": dg_cq, "g_ckv": dg_ckv, "g_sb_out": dg_sb, "g_mla_out": dg_mla,
                  "g_ffn": dg_ffn, "conv_b": _ff_deinterleave(dcb), "g_final": dg_final}
    small_all, = _all_gather([_pack_small(small_part)], name="ag_small_grads")
    gsmall = _sum8(small_all, name="sum_small_grads")

    params = {"w_in": (w_in, m_w_in, v_w_in), "w_uq": (w_uq, m_w_uq, v_w_uq), "w_ukv": (w_ukv, m_w_ukv, v_w_ukv),
              "w_out": (w_out, m_w_out, v_w_out), "w_up": (w_up, m_w_up, v_w_up), "conv_w": (conv_w, m_conv_w, v_conv_w),
              "w_down": (w_down, m_w_down, v_w_down)}
    grad, delta, new_m, new_v = {}, {}, {}, {}
    for n, (w_, m_, v_) in params.items():
        g_, d_, mn_, vn_ = _adamw_rs(own[n], r3[n], w_[0], m_[0], v_[0], name="adamw_" + n)
        grad[n], delta[n], new_m[n], new_v[n] = g_[None], d_[None], mn_[None], vn_[None]
    small_w = {"g_mix": g_mix, "g_cq": g_cq, "g_ckv": g_ckv, "g_sb_out": g_sb_out, "g_mla_out": g_mla_out,
               "g_ffn": g_ffn, "conv_b": conv_b, "g_final": g_final}
    small_m = {"g_mix": m_g_mix, "g_cq": m_g_cq, "g_ckv": m_g_ckv, "g_sb_out": m_g_sb_out, "g_mla_out": m_g_mla_out,
               "g_ffn": m_g_ffn, "conv_b": m_conv_b, "g_final": m_g_final}
    small_v = {"g_mix": v_g_mix, "g_cq": v_g_cq, "g_ckv": v_g_ckv, "g_sb_out": v_g_sb_out, "g_mla_out": v_g_mla_out,
               "g_ffn": v_g_ffn, "conv_b": v_conv_b, "g_final": v_g_final}
    ds_, ms_, vs_ = _adamw(_pack_small(small_w), gsmall, _pack_small(small_m), _pack_small(small_v), name="adamw_small")
    for src, dst in ((_unpack_small(gsmall), grad), (_unpack_small(ds_), delta), (_unpack_small(ms_), new_m), (_unpack_small(vs_), new_v)):
        for n, _ in SMALL:
            dst[n] = src[n].reshape(small_w[n].shape)

    loss = lax.psum(loss_row[0, 0], MESH_AXES)
    order = ("g_mix", "w_in", "g_cq", "w_uq", "g_ckv", "w_ukv", "g_sb_out", "g_mla_out", "w_out", "g_ffn", "w_up",
             "conv_w", "conv_b", "w_down", "g_final")
    return (loss, dx.reshape(B, S, D), *[grad[n] for n in order], *[delta[n] for n in order],
            *[new_m[n] for n in order], *[new_v[n] for n in order])
```

```python
import jax
import jax.numpy as jnp
from jax import lax
from jax.experimental import pallas as pl
from jax.experimental.pallas import tpu as pltpu
from jax.experimental.pallas import tpu_sc as plsc

F32 = jnp.float32
BF16 = jnp.bfloat16

D_MODEL = 1024
SB_HEADS = 8
SB_HEAD_DIM = 64
MLA_HEADS = 8
MLA_NOPE = 64
MLA_ROPE = 32
MLA_V = 64
Q_LORA = 384
KV_LORA = 256
D_FF = 2816
ROPE_BASE = 10000.0
EPS = 1e-6
SB_W = SB_HEADS * SB_HEAD_DIM
MLA_W = MLA_HEADS * MLA_V
MLA_QK = MLA_NOPE + MLA_ROPE
IN_COLS = 3 * SB_W + Q_LORA + KV_LORA + MLA_ROPE

ADAM_LR = 0.001
ADAM_B1 = 0.9
ADAM_B2 = 0.999
ADAM_EPS = 1e-08
ADAM_WD = 0.01
ADAM_STEP = 10

N_DEV = 8
MESH_AXES = ("x", "y", "c")
LANES = 128
V7X_VMEM_LIMIT = 56 * 1024 * 1024
FF_BLK = 256
N_FF_BLK = D_FF // FF_BLK

P_Q, P_K, P_V = 0, SB_W, 2 * SB_W
P_CKV = 3 * SB_W
P_KRT = P_CKV + KV_LORA
P_CQ = P_KRT + LANES
P_COLS = P_CQ + Q_LORA

MESH = pl.DeviceIdType.MESH
ANY = pl.BlockSpec(memory_space=pl.ANY)


def _cparams(sem=None, vmem=V7X_VMEM_LIMIT):
    return pltpu.CompilerParams(dimension_semantics=sem, vmem_limit_bytes=vmem)


def _matmul_nn(a, b, *, tm, tn, out_dtype, name, residual=None):
    M, K = a.shape
    N = b.shape[1]
    in_specs = [pl.BlockSpec((tm, K), lambda i, j: (i, 0)), pl.BlockSpec((K, tn), lambda i, j: (0, j))]
    args = [a, b]
    if residual is not None:
        in_specs.append(pl.BlockSpec((tm, tn), lambda i, j: (i, j)))
        args.append(residual)

    def body(*refs):
        a_ref, b_ref = refs[0], refs[1]
        o_ref = refs[-1]
        acc = jnp.dot(a_ref[...].astype(BF16), b_ref[...], preferred_element_type=F32)
        if residual is not None:
            acc = acc + refs[2][...]
        o_ref[...] = acc.astype(out_dtype)

    return pl.pallas_call(
        body, name=name, grid=(M // tm, N // tn), in_specs=in_specs,
        out_specs=pl.BlockSpec((tm, tn), lambda i, j: (i, j)),
        out_shape=jax.ShapeDtypeStruct((M, N), out_dtype),
        compiler_params=_cparams(("parallel", "parallel")),
    )(*args)


def _matmul_nt(a, b, *, tm, tn, out_dtype, name):
    M, K = a.shape
    N = b.shape[0]

    def body(a_ref, b_ref, o_ref):
        acc = lax.dot_general(a_ref[...].astype(BF16), b_ref[...], (((1,), (1,)), ((), ())),
                              preferred_element_type=F32)
        o_ref[...] = acc.astype(out_dtype)

    return pl.pallas_call(
        body, name=name, grid=(M // tm, N // tn),
        in_specs=[pl.BlockSpec((tm, K), lambda i, j: (i, 0)), pl.BlockSpec((tn, K), lambda i, j: (j, 0))],
        out_specs=pl.BlockSpec((tm, tn), lambda i, j: (i, j)),
        out_shape=jax.ShapeDtypeStruct((M, N), out_dtype),
        compiler_params=_cparams(("parallel", "parallel")),
    )(a, b)


def _matmul_tn(a, b, *, tm, tn, tk, name):
    K, M = a.shape
    N = b.shape[1]

    def body(a_ref, b_ref, o_ref):
        k = pl.program_id(2)
        part = lax.dot_general(a_ref[...].astype(BF16), b_ref[...].astype(BF16), (((0,), (0,)), ((), ())),
                               preferred_element_type=F32)

        @pl.when(k == 0)
        def _():
            o_ref[...] = part

        @pl.when(k > 0)
        def _():
            o_ref[...] += part

    return pl.pallas_call(
        body, name=name, grid=(M // tm, N // tn, K // tk),
        in_specs=[pl.BlockSpec((tk, tm), lambda i, j, k: (k, i)), pl.BlockSpec((tk, tn), lambda i, j, k: (k, j))],
        out_specs=pl.BlockSpec((tm, tn), lambda i, j, k: (i, j)),
        out_shape=jax.ShapeDtypeStruct((M, N), F32),
        compiler_params=_cparams(("parallel", "parallel", "arbitrary")),
    )(a, b)


def _rms(xf, g):
    r = lax.rsqrt(jnp.mean(xf * xf, axis=1, keepdims=True) + EPS)
    return (xf * r) * g


def _rms_grad(dyf, xf, g):
    r = lax.rsqrt(jnp.mean(xf * xf, axis=1, keepdims=True) + EPS)
    xh = xf * r
    dyg = dyf * g
    dx = r * (dyg - xh * jnp.mean(dyg * xh, axis=1, keepdims=True))
    return dx, jnp.sum(dyf * xh, axis=0, keepdims=True)


def _accumulate(ref, part):
    @pl.when(pl.program_id(0) == 0)
    def _():
        ref[...] = part

    @pl.when(pl.program_id(0) > 0)
    def _():
        ref[...] += part


def _rms_fwd(x, g, *, tm, name, col_block=0):
    T = x.shape[0]
    C = g.shape[1]

    def body(x_ref, g_ref, o_ref):
        o_ref[...] = _rms(x_ref[...], g_ref[...]).astype(BF16)

    return pl.pallas_call(
        body, name=name, grid=(T // tm,),
        in_specs=[pl.BlockSpec((tm, C), lambda i: (i, col_block)), pl.BlockSpec((1, C), lambda i: (0, 0))],
        out_specs=pl.BlockSpec((tm, C), lambda i: (i, 0)),
        out_shape=jax.ShapeDtypeStruct((T, C), BF16),
        compiler_params=_cparams(("parallel",)),
    )(x, g)


def _rms_bwd(dy, x, g, *, tm, name, residual=None, col_block=0, out_dtype=F32):
    T = dy.shape[0]
    C = g.shape[1]
    in_specs = [pl.BlockSpec((tm, C), lambda i: (i, 0)), pl.BlockSpec((tm, C), lambda i: (i, col_block)),
                pl.BlockSpec((1, C), lambda i: (0, 0))]
    args = [dy, x, g]
    if residual is not None:
        in_specs.append(pl.BlockSpec((tm, C), lambda i: (i, 0)))
        args.append(residual)

    def body(*refs):
        dy_ref, x_ref, g_ref = refs[:3]
        dx_ref, dg_ref = refs[-2:]
        dx, part = _rms_grad(dy_ref[...].astype(F32), x_ref[...], g_ref[...])
        if residual is not None:
            dx = dx + refs[3][...]
        dx_ref[...] = dx.astype(out_dtype)
        _accumulate(dg_ref, part)

    return pl.pallas_call(
        body, name=name, grid=(T // tm,), in_specs=in_specs,
        out_specs=[pl.BlockSpec((tm, C), lambda i: (i, 0)), pl.BlockSpec((1, C), lambda i: (0, 0))],
        out_shape=[jax.ShapeDtypeStruct((T, C), out_dtype), jax.ShapeDtypeStruct((1, C), F32)],
        compiler_params=_cparams(("arbitrary",)),
    )(*args)


def _rms2_fwd(xa, xb, ga, gb, *, tm, name):
    T, C = xa.shape

    def body(xa_ref, xb_ref, ga_ref, gb_ref, o_ref):
        o_ref[:, :C] = _rms(xa_ref[...], ga_ref[...]).astype(BF16)
        o_ref[:, C:] = _rms(xb_ref[...], gb_ref[...]).astype(BF16)

    row = pl.BlockSpec((tm, C), lambda i: (i, 0))
    gsp = pl.BlockSpec((1, C), lambda i: (0, 0))
    return pl.pallas_call(
        body, name=name, grid=(T // tm,), in_specs=[row, row, gsp, gsp],
        out_specs=pl.BlockSpec((tm, 2 * C), lambda i: (i, 0)),
        out_shape=jax.ShapeDtypeStruct((T, 2 * C), BF16),
        compiler_params=_cparams(("parallel",)),
    )(xa, xb, ga, gb)


def _rms2_bwd(dy, xa, xb, ga, gb, *, tm, name):
    T, C = xa.shape

    def body(dy_ref, xa_ref, xb_ref, ga_ref, gb_ref, dxa_ref, dxb_ref, dga_ref, dgb_ref):
        dxa, pa = _rms_grad(dy_ref[:, :C], xa_ref[...], ga_ref[...])
        dxb, pb = _rms_grad(dy_ref[:, C:], xb_ref[...], gb_ref[...])
        dxa_ref[...] = dxa
        dxb_ref[...] = dxb
        _accumulate(dga_ref, pa)
        _accumulate(dgb_ref, pb)

    row = pl.BlockSpec((tm, C), lambda i: (i, 0))
    gsp = pl.BlockSpec((1, C), lambda i: (0, 0))
    return pl.pallas_call(
        body, name=name, grid=(T // tm,),
        in_specs=[pl.BlockSpec((tm, 2 * C), lambda i: (i, 0)), row, row, gsp, gsp],
        out_specs=[row, row, gsp, gsp],
        out_shape=[jax.ShapeDtypeStruct((T, C), F32), jax.ShapeDtypeStruct((T, C), F32),
                   jax.ShapeDtypeStruct((1, C), F32), jax.ShapeDtypeStruct((1, C), F32)],
        compiler_params=_cparams(("arbitrary",)),
    )(dy, xa, xb, ga, gb)


def _final_loss(x2, g, tgt, *, tm, name):
    T, C = x2.shape

    def body(x_ref, g_ref, t_ref, dx_ref, dg_ref, loss_ref):
        xf = x_ref[...]
        gf = g_ref[...]
        err = _rms(xf, gf) - t_ref[...]
        lpart = 0.5 * jnp.sum(jnp.mean(err * err, axis=1, keepdims=True), axis=0, keepdims=True)
        dx, gpart = _rms_grad(err * (1.0 / C), xf, gf)
        dx_ref[...] = dx
        _accumulate(dg_ref, gpart)
        _accumulate(loss_ref, jnp.broadcast_to(lpart, (1, LANES)))

    return pl.pallas_call(
        body, name=name, grid=(T // tm,),
        in_specs=[pl.BlockSpec((tm, C), lambda i: (i, 0)), pl.BlockSpec((1, C), lambda i: (0, 0)),
                  pl.BlockSpec((tm, C), lambda i: (i, 0))],
        out_specs=[pl.BlockSpec((tm, C), lambda i: (i, 0)), pl.BlockSpec((1, C), lambda i: (0, 0)),
                   pl.BlockSpec((1, LANES), lambda i: (0, 0))],
        out_shape=[jax.ShapeDtypeStruct((T, C), F32), jax.ShapeDtypeStruct((1, C), F32),
                   jax.ShapeDtypeStruct((1, LANES), F32)],
        compiler_params=_cparams(("arbitrary",)),
    )(x2, g, tgt)


ATT_T = 256
ATT_PAIRS = 2
NEG_BIG = -1e30


def _lane_iota():
    return lax.broadcasted_iota(jnp.int32, (1, LANES), 1)


def _head_masks():
    first = _lane_iota() < SB_HEAD_DIM
    return first, jnp.logical_not(first)


def _pick(mask, x):
    return jnp.where(mask, x, jnp.zeros_like(x))


def _lane_value(t, lane):
    return jnp.sum(jnp.where(_lane_iota() == lane, t, 0.0), axis=1, keepdims=True)


def _split_hi_lo(x):
    hi = x.astype(BF16)
    lo = (x - hi.astype(F32)).astype(BF16)
    return jnp.concatenate([hi, lo], axis=1)


def _tri(n, kind):
    r = lax.broadcasted_iota(jnp.int32, (n, n), 0)
    c = lax.broadcasted_iota(jnp.int32, (n, n), 1)
    u = {"suffix_excl": r > c, "prefix_incl": r <= c, "prefix_excl": r < c}[kind].astype(BF16)
    return jnp.concatenate([u, u], axis=0)


def _dot_nt(a, b):
    return lax.dot_general(a, b, (((1,), (1,)), ((), ())), preferred_element_type=F32)


def _dot_tn(a, b):
    return lax.dot_general(a, b, (((0,), (0,)), ((), ())), preferred_element_type=F32)


def _dot(a, b):
    return jnp.dot(a, b, preferred_element_type=F32)


def _causal_mask(n, strict):
    r = lax.broadcasted_iota(jnp.int32, (n, n), 0)
    c = lax.broadcasted_iota(jnp.int32, (n, n), 1)
    return (c < r) if strict else (c <= r)


def _sb_logs(qh, kj, vis):
    z = _dot_nt(qh, kj)
    sp = jnp.log(1.0 + jnp.exp(-jnp.abs(z)))
    lb = jnp.minimum(z, 0.0) - sp
    lk = jnp.minimum(-z, 0.0) - sp
    if vis is not None:
        lk = jnp.where(vis, lk, 0.0)
    return lb, lk


def _sb_fwd(p, *, seq, name):
    T = p.shape[0]
    B = T // seq
    TQ = ATT_T
    nq = seq // TQ
    PP = ATT_PAIRS
    W = PP * LANES
    nstep = SB_W // W
    NH = 2 * PP

    def body(q_ref, k_ref, v_ref, o_ref, lt_ref, q_s, k_s, v_s):
        masks = _head_masks()
        q = q_ref[...] * (SB_HEAD_DIM ** -0.5)
        v = v_ref[...]
        k_s[...] = k_ref[...].astype(BF16)
        for h in range(NH):
            ps = slice((h // 2) * LANES, (h // 2 + 1) * LANES)
            hs = slice(h * LANES, (h + 1) * LANES)
            q_s[:, hs] = _pick(masks[h % 2], q[:, ps]).astype(BF16)
            v_s[:, hs] = _pick(masks[h % 2], v[:, ps]).astype(BF16)
        u_suf = _tri(TQ, "suffix_excl")
        vis = _causal_mask(TQ, True)

        def q_block(i, carry):
            q0 = pl.multiple_of(i * TQ, TQ)
            qs = [q_s[pl.ds(q0, TQ), h * LANES:(h + 1) * LANES] for h in range(NH)]

            def head(qh, kj, vj, r_run, mask):
                lb, lk = _sb_logs(qh, kj, mask)
                a = jnp.exp(lb + _dot(_split_hi_lo(lk), u_suf) + r_run)
                if mask is not None:
                    a = jnp.where(mask, a, 0.0)
                return _dot(a.astype(BF16), vj), r_run + jnp.sum(lk, axis=1, keepdims=True)

            def tile(k0, c, mask):
                rs, accs = list(c[:NH]), list(c[NH:])
                logs = [_sb_logs(qs[h], k_s[pl.ds(k0, TQ), (h // 2) * LANES:(h // 2 + 1) * LANES], mask) for h in range(NH)]
                sums = [_dot(_split_hi_lo(lk), u_suf) for _, lk in logs]
                for h in range(NH):
                    a = jnp.exp(logs[h][0] + sums[h] + rs[h])
                    if mask is not None:
                        a = jnp.where(mask, a, 0.0)
                    accs[h // 2] = accs[h // 2] + _dot(a.astype(BF16), v_s[pl.ds(k0, TQ), h * LANES:(h + 1) * LANES])
                    rs[h] = rs[h] + jnp.sum(logs[h][1], axis=1, keepdims=True)
                return tuple(rs) + tuple(accs)

            zero = jnp.zeros((TQ, 1), F32)
            c = tile(q0, (zero,) * NH + (jnp.zeros((TQ, LANES), F32),) * PP, vis)

            def k_block(jj, c):
                return tile(pl.multiple_of((i - 1 - jj) * TQ, TQ), c, None)

            c = lax.fori_loop(0, i, k_block, c)
            for pr in range(PP):
                ps = slice(pr * LANES, (pr + 1) * LANES)
                o_ref[pl.ds(q0, TQ), ps] = c[NH + pr]
                lt_ref[pl.ds(q0, TQ), ps] = jnp.where(masks[0], c[2 * pr], c[2 * pr + 1])
            return carry

        lax.fori_loop(0, nq, q_block, 0)

    blk = lambda off: pl.BlockSpec((seq, W), lambda b, g: (b, off + g))
    out_blk = pl.BlockSpec((seq, W), lambda b, g: (b, g))
    return pl.pallas_call(
        body, name=name, grid=(B, nstep),
        in_specs=[blk(P_Q // W), blk(P_K // W), blk(P_V // W)],
        out_specs=[out_blk, out_blk],
        out_shape=[jax.ShapeDtypeStruct((T, SB_W), F32), jax.ShapeDtypeStruct((T, SB_W), F32)],
        scratch_shapes=[pltpu.VMEM((seq, NH * LANES), BF16), pltpu.VMEM((seq, W), BF16), pltpu.VMEM((seq, NH * LANES), BF16)],
        compiler_params=_cparams(("parallel", "parallel")),
    )(p, p, p)


def _sb_bwd(p, ltot, do, *, seq, name):
    T = p.shape[0]
    B = T // seq
    TQ = ATT_T
    nq = seq // TQ
    PP = ATT_PAIRS
    W = PP * LANES
    nstep = SB_W // W
    NH = 2 * PP
    scale = SB_HEAD_DIM ** -0.5

    def body(q_ref, k_ref, v_ref, lt_ref, do_ref, dq_ref, dk_ref, dv_ref, q_s, k_s, v_s, do_s, dk_s, dv_s):
        masks = _head_masks()
        q = q_ref[...] * scale
        dof = do_ref[...]
        k_s[...] = k_ref[...].astype(BF16)
        v_s[...] = v_ref[...].astype(BF16)
        for h in range(NH):
            ps = slice((h // 2) * LANES, (h // 2 + 1) * LANES)
            hs = slice(h * LANES, (h + 1) * LANES)
            q_s[:, hs] = _pick(masks[h % 2], q[:, ps]).astype(BF16)
            do_s[:, hs] = _pick(masks[h % 2], dof[:, ps]).astype(BF16)
        dk_s[...] = jnp.zeros_like(dk_s)
        dv_s[...] = jnp.zeros_like(dv_s)
        u_pin = _tri(TQ, "prefix_incl")
        u_pex = _tri(TQ, "prefix_excl")
        vis = _causal_mask(TQ, True)

        def q_block(i, carry):
            q0 = pl.multiple_of(i * TQ, TQ)
            qs = [q_s[pl.ds(q0, TQ), h * LANES:(h + 1) * LANES] for h in range(NH)]
            dos = [do_s[pl.ds(q0, TQ), h * LANES:(h + 1) * LANES] for h in range(NH)]
            lt = lt_ref[pl.ds(q0, TQ), :]
            lts = [_lane_value(lt[:, (h // 2) * LANES:(h // 2 + 1) * LANES], (h % 2) * SB_HEAD_DIM) for h in range(NH)]

            def tile(k0, c, mask):
                cs, gs, accs = list(c[:NH]), list(c[NH:2 * NH]), list(c[2 * NH:])
                kjs = [k_s[pl.ds(k0, TQ), pr * LANES:(pr + 1) * LANES] for pr in range(PP)]
                vjs = [v_s[pl.ds(k0, TQ), pr * LANES:(pr + 1) * LANES] for pr in range(PP)]
                logs = [_sb_logs(qs[h], kjs[h // 2], mask) for h in range(NH)]
                pins = [_dot(_split_hi_lo(lk), u_pin) for _, lk in logs]
                das = [_dot_nt(dos[h], vjs[h // 2]) for h in range(NH)]
                a_l, g_l = [], []
                for h in range(NH):
                    a = jnp.exp(logs[h][0] + ((lts[h] - cs[h]) - pins[h]))
                    if mask is not None:
                        a = jnp.where(mask, a, 0.0)
                    a_l.append(a)
                    g_l.append(das[h] * a)
                pres = [_dot(_split_hi_lo(g), u_pex) for g in g_l]
                dz_l = []
                for h in range(NH):
                    dz = g_l[h] - jnp.exp(logs[h][0]) * (g_l[h] + (pres[h] + gs[h]))
                    if mask is not None:
                        dz = jnp.where(mask, dz, 0.0)
                    dz_l.append(dz.astype(BF16))
                for h in range(NH):
                    accs[h] = accs[h] + _dot(dz_l[h], kjs[h // 2])
                for pr in range(PP):
                    ps = slice(pr * LANES, (pr + 1) * LANES)
                    ha, hb = 2 * pr, 2 * pr + 1
                    dk_s[pl.ds(k0, TQ), ps] += _dot_tn(dz_l[ha], qs[ha]) + _dot_tn(dz_l[hb], qs[hb])
                    dv_s[pl.ds(k0, TQ), ps] += _dot_tn(a_l[ha].astype(BF16), dos[ha]) + _dot_tn(a_l[hb].astype(BF16), dos[hb])
                for h in range(NH):
                    cs[h] = cs[h] + jnp.sum(logs[h][1], axis=1, keepdims=True)
                    gs[h] = gs[h] + jnp.sum(g_l[h], axis=1, keepdims=True)
                return tuple(cs) + tuple(gs) + tuple(accs)

            z1 = jnp.zeros((TQ, 1), F32)
            zl = jnp.zeros((TQ, LANES), F32)

            def k_block(j, c):
                return tile(pl.multiple_of(j * TQ, TQ), c, None)

            c = lax.fori_loop(0, i, k_block, (z1,) * (2 * NH) + (zl,) * NH)
            c = tile(q0, c, vis)
            for pr in range(PP):
                dq = jnp.where(masks[0], c[2 * NH + 2 * pr], c[2 * NH + 2 * pr + 1]) * scale
                dq_ref[pl.ds(q0, TQ), pr * LANES:(pr + 1) * LANES] = dq.astype(BF16)
            return carry

        lax.fori_loop(0, nq, q_block, 0)
        dk_ref[...] = dk_s[...].astype(BF16)
        dv_ref[...] = dv_s[...].astype(BF16)

    blk = lambda off: pl.BlockSpec((seq, W), lambda b, g: (b, off + g))
    out_blk = pl.BlockSpec((seq, W), lambda b, g: (b, g))
    return pl.pallas_call(
        body, name=name, grid=(B, nstep),
        in_specs=[blk(P_Q // W), blk(P_K // W), blk(P_V // W), out_blk, out_blk],
        out_specs=[out_blk, out_blk, out_blk],
        out_shape=[jax.ShapeDtypeStruct((T, SB_W), BF16) for _ in range(3)],
        scratch_shapes=[pltpu.VMEM((seq, NH * LANES), BF16), pltpu.VMEM((seq, W), BF16), pltpu.VMEM((seq, W), BF16),
                        pltpu.VMEM((seq, NH * LANES), BF16), pltpu.VMEM((seq, W), F32), pltpu.VMEM((seq, W), F32)],
        compiler_params=_cparams(("parallel", "parallel")),
    )(p, p, p, ltot, do)


def _mla_masks():
    lane = lax.broadcasted_iota(jnp.int32, (1, 2 * LANES), 1)
    ma = (lane < MLA_NOPE) | ((lane >= LANES) & (lane < LANES + MLA_ROPE))
    mb = ((lane >= MLA_NOPE) & (lane < LANES)) | ((lane >= LANES + MLA_ROPE) & (lane < LANES + 2 * MLA_ROPE))
    return ma, mb


def _mla_fwd(qm, kvm, krt, *, seq, name):
    T = qm.shape[0]
    B = T // seq
    TQ = ATT_T
    nq = seq // TQ
    PP = ATT_PAIRS
    W = PP * LANES
    nstep = MLA_W // W
    NH = 2 * PP
    CW = 2 * LANES
    scale = MLA_QK ** -0.5

    def body(qn_ref, qr_ref, kn_ref, v_ref, kr_ref, o_ref, lse_ref, q_s, kc_s, v_s):
        hm = _head_masks()
        mm = _mla_masks()
        v = v_ref[...]
        for pr in range(PP):
            ps = slice(pr * LANES, (pr + 1) * LANES)
            qc = jnp.concatenate([qn_ref[:, ps], qr_ref[:, ps]], axis=1)
            kc_s[:, pr * CW:(pr + 1) * CW] = jnp.concatenate([kn_ref[:, ps], kr_ref[...]], axis=1)
            for e in range(2):
                h = 2 * pr + e
                q_s[:, h * CW:(h + 1) * CW] = _pick(mm[e], qc)
                v_s[:, h * LANES:(h + 1) * LANES] = _pick(hm[e], v[:, ps])
        vis = _causal_mask(TQ, False)

        def q_block(i, carry):
            q0 = pl.multiple_of(i * TQ, TQ)
            qs = [q_s[pl.ds(q0, TQ), h * CW:(h + 1) * CW] for h in range(NH)]

            def tile(k0, c, mask):
                ms, ls, accs = list(c[:NH]), list(c[NH:2 * NH]), list(c[2 * NH:])
                ss = [_dot_nt(qs[h], kc_s[pl.ds(k0, TQ), (h // 2) * CW:(h // 2 + 1) * CW]) * scale for h in range(NH)]
                if mask is not None:
                    ss = [jnp.where(mask, s, NEG_BIG) for s in ss]
                m_new = [jnp.maximum(ms[h], jnp.max(ss[h], axis=1, keepdims=True)) for h in range(NH)]
                alphas = [jnp.exp(ms[h] - m_new[h]) for h in range(NH)]
                prs = [jnp.exp(ss[h] - m_new[h]) for h in range(NH)]
                outs = [_dot(prs[h].astype(BF16), v_s[pl.ds(k0, TQ), h * LANES:(h + 1) * LANES]) for h in range(NH)]
                ls = [alphas[h] * ls[h] + jnp.sum(prs[h], axis=1, keepdims=True) for h in range(NH)]
                for pr in range(PP):
                    accs[pr] = accs[pr] * jnp.where(hm[0], alphas[2 * pr], alphas[2 * pr + 1]) + outs[2 * pr] + outs[2 * pr + 1]
                return tuple(m_new) + tuple(ls) + tuple(accs)

            neg = jnp.full((TQ, 1), NEG_BIG, F32)
            z1 = jnp.zeros((TQ, 1), F32)

            def k_block(j, c):
                return tile(pl.multiple_of(j * TQ, TQ), c, None)

            c = lax.fori_loop(0, i, k_block, (neg,) * NH + (z1,) * NH + (jnp.zeros((TQ, LANES), F32),) * PP)
            c = tile(q0, c, vis)
            for pr in range(PP):
                ps = slice(pr * LANES, (pr + 1) * LANES)
                m_a, m_b, l_a, l_b = c[2 * pr], c[2 * pr + 1], c[NH + 2 * pr], c[NH + 2 * pr + 1]
                o_ref[pl.ds(q0, TQ), ps] = c[2 * NH + pr] / jnp.where(hm[0], l_a, l_b)
                lse_ref[pl.ds(q0, TQ), ps] = jnp.where(hm[0], m_a + jnp.log(l_a), m_b + jnp.log(l_b))
            return carry

        lax.fori_loop(0, nq, q_block, 0)

    blk = lambda off: pl.BlockSpec((seq, W), lambda b, g: (b, off + g))
    out_blk = pl.BlockSpec((seq, W), lambda b, g: (b, g))
    return pl.pallas_call(
        body, name=name, grid=(B, nstep),
        in_specs=[blk(0), blk(nstep), blk(0), blk(nstep), pl.BlockSpec((seq, LANES), lambda b, g: (b, 0))],
        out_specs=[out_blk, out_blk],
        out_shape=[jax.ShapeDtypeStruct((T, MLA_W), F32), jax.ShapeDtypeStruct((T, MLA_W), F32)],
        scratch_shapes=[pltpu.VMEM((seq, NH * CW), BF16), pltpu.VMEM((seq, PP * CW), BF16), pltpu.VMEM((seq, NH * LANES), BF16)],
        compiler_params=_cparams(("parallel", "parallel")),
    )(qm, qm, kvm, kvm, krt)


def _mla_bwd(qm, kvm, krt, o, lse, do, *, seq, name):
    T = qm.shape[0]
    B = T // seq
    TQ = ATT_T
    nq = seq // TQ
    PP = ATT_PAIRS
    W = PP * LANES
    nstep = MLA_W // W
    NH = 2 * PP
    CW = 2 * LANES
    scale = MLA_QK ** -0.5

    def body(qn_ref, qr_ref, kn_ref, v_ref, kr_ref, o_ref, lse_ref, do_ref,
             dqn_ref, dqr_ref, dkn_ref, dv_ref, dkr_ref, q_s, kc_s, do_s, dkc_s, dv_s):
        hm = _head_masks()
        mm = _mla_masks()
        dof = do_ref[...]
        for pr in range(PP):
            ps = slice(pr * LANES, (pr + 1) * LANES)
            qc = jnp.concatenate([qn_ref[:, ps], qr_ref[:, ps]], axis=1)
            kc_s[:, pr * CW:(pr + 1) * CW] = jnp.concatenate([kn_ref[:, ps], kr_ref[...]], axis=1)
            for e in range(2):
                h = 2 * pr + e
                q_s[:, h * CW:(h + 1) * CW] = _pick(mm[e], qc)
                do_s[:, h * LANES:(h + 1) * LANES] = _pick(hm[e], dof[:, ps]).astype(BF16)
        dkc_s[...] = jnp.zeros_like(dkc_s)
        dv_s[...] = jnp.zeros_like(dv_s)
        vis = _causal_mask(TQ, False)

        def q_block(i, carry):
            q0 = pl.multiple_of(i * TQ, TQ)
            qs = [q_s[pl.ds(q0, TQ), h * CW:(h + 1) * CW] for h in range(NH)]
            dos = [do_s[pl.ds(q0, TQ), h * LANES:(h + 1) * LANES] for h in range(NH)]
            lse_t = lse_ref[pl.ds(q0, TQ), :]
            dd = do_ref[pl.ds(q0, TQ), :] * o_ref[pl.ds(q0, TQ), :]
            lses, ds_ = [], []
            for h in range(NH):
                ps = slice((h // 2) * LANES, (h // 2 + 1) * LANES)
                lses.append(_lane_value(lse_t[:, ps], (h % 2) * MLA_V))
                ds_.append(jnp.sum(_pick(hm[h % 2], dd[:, ps]), axis=1, keepdims=True))

            def tile(k0, c, mask):
                accs = list(c)
                kcs = [kc_s[pl.ds(k0, TQ), pr * CW:(pr + 1) * CW] for pr in range(PP)]
                vjs = [v_ref[pl.ds(k0, TQ), pr * LANES:(pr + 1) * LANES] for pr in range(PP)]
                ss = [_dot_nt(qs[h], kcs[h // 2]) * scale for h in range(NH)]
                dps = [_dot_nt(dos[h], vjs[h // 2]) for h in range(NH)]
                p_l, ds_l = [], []
                for h in range(NH):
                    pr_ = jnp.exp(ss[h] - lses[h])
                    if mask is not None:
                        pr_ = jnp.where(mask, pr_, 0.0)
                    p_l.append(pr_.astype(BF16))
                    ds_l.append((pr_ * (dps[h] - ds_[h]) * scale).astype(BF16))
                for h in range(NH):
                    accs[h] = accs[h] + _dot(ds_l[h], kcs[h // 2])
                for pr in range(PP):
                    ha, hb = 2 * pr, 2 * pr + 1
                    dkc_s[pl.ds(k0, TQ), pr * CW:(pr + 1) * CW] += _dot_tn(ds_l[ha], qs[ha]) + _dot_tn(ds_l[hb], qs[hb])
                    dv_s[pl.ds(k0, TQ), pr * LANES:(pr + 1) * LANES] += _dot_tn(p_l[ha], dos[ha]) + _dot_tn(p_l[hb], dos[hb])
                return tuple(accs)

            zc = jnp.zeros((TQ, CW), F32)

            def k_block(j, c):
                return tile(pl.multiple_of(j * TQ, TQ), c, None)

            c = lax.fori_loop(0, i, k_block, (zc,) * NH)
            c = tile(q0, c, vis)
            for pr in range(PP):
                ps = slice(pr * LANES, (pr + 1) * LANES)
                dq = _pick(mm[0], c[2 * pr]) + _pick(mm[1], c[2 * pr + 1])
                dqn_ref[pl.ds(q0, TQ), ps] = dq[:, :LANES].astype(BF16)
                dqr_ref[pl.ds(q0, TQ), ps] = dq[:, LANES:]
            return carry

        lax.fori_loop(0, nq, q_block, 0)
        dkr = dkc_s[:, LANES:CW]
        for pr in range(PP):
            dkn_ref[:, pr * LANES:(pr + 1) * LANES] = dkc_s[:, pr * CW:pr * CW + LANES].astype(BF16)
            if pr > 0:
                dkr = dkr + dkc_s[:, pr * CW + LANES:(pr + 1) * CW]
        dv_ref[...] = dv_s[...].astype(BF16)
        g = pl.program_id(1)

        @pl.when(g == 0)
        def _():
            dkr_ref[...] = dkr

        @pl.when(g > 0)
        def _():
            dkr_ref[...] += dkr

    blk = lambda off: pl.BlockSpec((seq, W), lambda b, g: (b, off + g))
    out_blk = pl.BlockSpec((seq, W), lambda b, g: (b, g))
    one_blk = pl.BlockSpec((seq, LANES), lambda b, g: (b, 0))
    return pl.pallas_call(
        body, name=name, grid=(B, nstep),
        in_specs=[blk(0), blk(nstep), blk(0), blk(nstep), one_blk, out_blk, out_blk, out_blk],
        out_specs=[out_blk, out_blk, out_blk, out_blk, one_blk],
        out_shape=[jax.ShapeDtypeStruct((T, MLA_W), BF16), jax.ShapeDtypeStruct((T, MLA_W), F32),
                   jax.ShapeDtypeStruct((T, MLA_W), BF16), jax.ShapeDtypeStruct((T, MLA_W), BF16),
                   jax.ShapeDtypeStruct((T, LANES), F32)],
        scratch_shapes=[pltpu.VMEM((seq, NH * CW), BF16), pltpu.VMEM((seq, PP * CW), BF16), pltpu.VMEM((seq, NH * LANES), BF16),
                        pltpu.VMEM((seq, PP * CW), F32), pltpu.VMEM((seq, W), F32)],
        compiler_params=_cparams(("parallel", "arbitrary")),
    )(qm, qm, kvm, kvm, krt, o, lse, do)


def _rope_tables(pos_ref, invf_ref):
    ang = pos_ref[...].astype(F32) * invf_ref[...]
    first = (_lane_iota() % MLA_ROPE) < (MLA_ROPE // 2)
    return jnp.cos(ang), jnp.sin(ang), first


def _rope_apply(x, cos, sin, first):
    rot = jnp.where(first, -pltpu.roll(x, LANES - MLA_ROPE // 2, 1), pltpu.roll(x, MLA_ROPE // 2, 1))
    return x * cos + rot * sin


def _rope_apply_t(dy, cos, sin, first):
    dys = dy * sin
    rot_t = jnp.where(first, pltpu.roll(dys, LANES - MLA_ROPE // 2, 1), -pltpu.roll(dys, MLA_ROPE // 2, 1))
    return dy * cos + rot_t


def _rope_fwd(qfull, p, pos, invf, *, tm, name):
    T = qfull.shape[0]
    ntile = MLA_W // LANES

    def body(q_ref, kr_ref, pos_ref, invf_ref, qm_ref, krt_ref):
        cos, sin, first = _rope_tables(pos_ref, invf_ref)
        qm_ref[:, :MLA_W] = q_ref[:, :MLA_W].astype(BF16)
        for t in range(ntile):
            sl = slice(MLA_W + t * LANES, MLA_W + (t + 1) * LANES)
            qm_ref[:, sl] = _rope_apply(q_ref[:, sl], cos, sin, first).astype(BF16)
        krt_ref[...] = _rope_apply(kr_ref[...], cos, sin, first).astype(BF16)

    return pl.pallas_call(
        body, name=name, grid=(T // tm,),
        in_specs=[pl.BlockSpec((tm, 2 * MLA_W), lambda i: (i, 0)), pl.BlockSpec((tm, LANES), lambda i: (i, P_KRT // LANES)),
                  pl.BlockSpec((tm, 1), lambda i: (i, 0)), pl.BlockSpec((1, LANES), lambda i: (0, 0))],
        out_specs=[pl.BlockSpec((tm, 2 * MLA_W), lambda i: (i, 0)), pl.BlockSpec((tm, LANES), lambda i: (i, 0))],
        out_shape=[jax.ShapeDtypeStruct((T, 2 * MLA_W), BF16), jax.ShapeDtypeStruct((T, LANES), BF16)],
        compiler_params=_cparams(("parallel",)),
    )(qfull, p, pos, invf)


def _rope_bwd(dqr, dkr, pos, invf, *, tm, name):
    T = dqr.shape[0]
    ntile = MLA_W // LANES

    def body(dq_ref, dk_ref, pos_ref, invf_ref, oq_ref, ok_ref):
        cos, sin, first = _rope_tables(pos_ref, invf_ref)
        for t in range(ntile):
            sl = slice(t * LANES, (t + 1) * LANES)
            oq_ref[:, sl] = _rope_apply_t(dq_ref[:, sl], cos, sin, first).astype(BF16)
        ok_ref[...] = _rope_apply_t(dk_ref[...], cos, sin, first).astype(BF16)

    return pl.pallas_call(
        body, name=name, grid=(T // tm,),
        in_specs=[pl.BlockSpec((tm, MLA_W), lambda i: (i, 0)), pl.BlockSpec((tm, LANES), lambda i: (i, 0)),
                  pl.BlockSpec((tm, 1), lambda i: (i, 0)), pl.BlockSpec((1, LANES), lambda i: (0, 0))],
        out_specs=[pl.BlockSpec((tm, MLA_W), lambda i: (i, 0)), pl.BlockSpec((tm, LANES), lambda i: (i, 0))],
        out_shape=[jax.ShapeDtypeStruct((T, MLA_W), BF16), jax.ShapeDtypeStruct((T, LANES), BF16)],
        compiler_params=_cparams(("parallel",)),
    )(dqr, dkr, pos, invf)


CONV_ROWS = 256
HALO = 8


def _conv_taps(w_ref):
    return w_ref[0:1, :], w_ref[1:2, :], w_ref[2:3, :]


def _conv_rows(cur, prev, w, bias):
    ext = jnp.concatenate([prev, cur], axis=0)
    u1 = pltpu.roll(ext, 1, 0)[HALO:]
    u2 = pltpu.roll(ext, 2, 0)[HALO:]
    return w[2] * cur + w[1] * u1 + w[0] * u2 + bias, u1, u2


def _conv_fwd(u, w, bias, *, seq, name):
    T = u.shape[0]
    B = T // seq
    W2 = 2 * FF_BLK

    def body(u_ref, w_ref, b_ref, a_ref):
        wv = _conv_taps(w_ref)
        bv = b_ref[...]
        for c in range(seq // CONV_ROWS):
            r0 = c * CONV_ROWS
            cur = u_ref[r0:r0 + CONV_ROWS, :]
            prev = u_ref[r0 - HALO:r0, :] if c > 0 else jnp.zeros((HALO, W2), F32)
            y, _, _ = _conv_rows(cur, prev, wv, bv)
            gc = y[:, :FF_BLK]
            a_ref[r0:r0 + CONV_ROWS, :] = (gc * (1.0 / (1.0 + jnp.exp(-gc))) * y[:, FF_BLK:]).astype(BF16)

    return pl.pallas_call(
        body, name=name, grid=(B, N_FF_BLK),
        in_specs=[pl.BlockSpec((seq, W2), lambda b, j: (b, j)), pl.BlockSpec((3, W2), lambda b, j: (0, j)),
                  pl.BlockSpec((1, W2), lambda b, j: (0, j))],
        out_specs=pl.BlockSpec((seq, FF_BLK), lambda b, j: (b, j)),
        out_shape=jax.ShapeDtypeStruct((T, D_FF), BF16),
        compiler_params=_cparams(("parallel", "parallel")),
    )(u, w, bias)


def _conv_bwd(u, da, w, bias, *, seq, name):
    T = u.shape[0]
    B = T // seq
    W2 = 2 * FF_BLK
    nchunk = seq // CONV_ROWS

    def body(u_ref, da_ref, w_ref, b_ref, du_ref, dw_ref, db_ref, duc_s):
        wv = _conv_taps(w_ref)
        bv = b_ref[...]
        zrow = jnp.zeros((1, W2), F32)
        dw0, dw1, dw2, dbs = zrow, zrow, zrow, zrow
        for c in range(nchunk):
            r0 = c * CONV_ROWS
            cur = u_ref[r0:r0 + CONV_ROWS, :]
            prev = u_ref[r0 - HALO:r0, :] if c > 0 else jnp.zeros((HALO, W2), F32)
            y, u1, u2 = _conv_rows(cur, prev, wv, bv)
            gc = y[:, :FF_BLK]
            vc = y[:, FF_BLK:]
            sg = 1.0 / (1.0 + jnp.exp(-gc))
            dav = da_ref[r0:r0 + CONV_ROWS, :]
            duc = jnp.concatenate([dav * vc * (sg * (1.0 + gc * (1.0 - sg))), dav * (gc * sg)], axis=1)
            duc_s[r0:r0 + CONV_ROWS, :] = duc
            dw0 = dw0 + jnp.sum(duc * u2, axis=0, keepdims=True)
            dw1 = dw1 + jnp.sum(duc * u1, axis=0, keepdims=True)
            dw2 = dw2 + jnp.sum(duc * cur, axis=0, keepdims=True)
            dbs = dbs + jnp.sum(duc, axis=0, keepdims=True)
        duc_s[seq:seq + HALO, :] = jnp.zeros((HALO, W2), F32)
        n_ext = CONV_ROWS + HALO
        for c in range(nchunk):
            r0 = c * CONV_ROWS
            ext = duc_s[r0:r0 + n_ext, :]
            s1 = pltpu.roll(ext, n_ext - 1, 0)[:CONV_ROWS]
            s2 = pltpu.roll(ext, n_ext - 2, 0)[:CONV_ROWS]
            du_ref[r0:r0 + CONV_ROWS, :] = (wv[2] * ext[:CONV_ROWS] + wv[1] * s1 + wv[0] * s2).astype(BF16)

        first = pl.program_id(1) == 0

        @pl.when(first)
        def _():
            dw_ref[0:1, :] = dw0
            dw_ref[1:2, :] = dw1
            dw_ref[2:3, :] = dw2
            db_ref[...] = dbs

        @pl.when(jnp.logical_not(first))
        def _():
            dw_ref[0:1, :] += dw0
            dw_ref[1:2, :] += dw1
            dw_ref[2:3, :] += dw2
            db_ref[...] += dbs

    return pl.pallas_call(
        body, name=name, grid=(N_FF_BLK, B),
        in_specs=[pl.BlockSpec((seq, W2), lambda j, b: (b, j)), pl.BlockSpec((seq, FF_BLK), lambda j, b: (b, j)),
                  pl.BlockSpec((3, W2), lambda j, b: (0, j)), pl.BlockSpec((1, W2), lambda j, b: (0, j))],
        out_specs=[pl.BlockSpec((seq, W2), lambda j, b: (b, j)), pl.BlockSpec((3, W2), lambda j, b: (0, j)),
                   pl.BlockSpec((1, W2), lambda j, b: (0, j))],
        out_shape=[jax.ShapeDtypeStruct((T, 2 * D_FF), BF16), jax.ShapeDtypeStruct((3, 2 * D_FF), F32),
                   jax.ShapeDtypeStruct((1, 2 * D_FF), F32)],
        scratch_shapes=[pltpu.VMEM((seq + HALO, W2), F32)],
        compiler_params=_cparams(("parallel", "arbitrary")),
    )(u, da, w, bias)


def _place():
    return lax.axis_index("x"), lax.axis_index("y"), lax.axis_index("c")


def _other_chips(x, y):
    return [(1 - x, y), (x, 1 - y), (1 - x, 1 - y)]


def _all_gather(vs, *, name):
    n = len(vs)

    def body(*refs):
        v_refs, out_refs = refs[:n], refs[n:2 * n]
        send_sems, recv_sems, local_sems = refs[2 * n:]
        x, y, c = _place()
        me, sibling = (x, y, c), (x, y, 1 - c)
        chips = _other_chips(x, y)

        def slab(a, px, py, pc):
            return out_refs[a].at[4 * px + 2 * py + pc]

        def copy(a, k, block, to, src=None):
            return pltpu.make_async_remote_copy(
                src_ref=slab(a, *block) if src is None else src, dst_ref=slab(a, *block),
                send_sem=send_sems.at[7 * a + k], recv_sem=recv_sems.at[7 * a + k], device_id=to, device_id_type=MESH)

        mine = [pltpu.make_async_copy(v_refs[a], slab(a, *me), local_sems.at[a]) for a in range(n)]
        for cp in mine:
            cp.start()
        first = []
        for a in range(n):
            first.append(copy(a, 0, me, sibling, src=v_refs[a]))
            first += [copy(a, 1 + j, me, (*chip, c), src=v_refs[a]) for j, chip in enumerate(chips)]
        for cp in first:
            cp.start()
        passed = []
        for j, chip in enumerate(chips):
            for a in range(n):
                copy(a, 1 + j, (*chip, c), me).wait_recv()
                cp = copy(a, 4 + j, (*chip, c), sibling)
                cp.start()
                passed.append(cp)
        for a in range(n):
            copy(a, 0, sibling, me).wait_recv()
            for j, chip in enumerate(chips):
                copy(a, 4 + j, (*chip, 1 - c), me).wait_recv()
        for cp in first + passed:
            cp.wait_send()
        for cp in mine:
            cp.wait()

    return pl.pallas_call(
        body, name=name, in_specs=[ANY] * n, out_specs=[ANY] * n,
        out_shape=[jax.ShapeDtypeStruct((N_DEV,) + v.shape, v.dtype) for v in vs],
        scratch_shapes=[pltpu.SemaphoreType.DMA((7 * n,)), pltpu.SemaphoreType.DMA((7 * n,)), pltpu.SemaphoreType.DMA((n,))],
    )(*vs)


def _all_gather_async(vs, *, name, collective_id):
    n = len(vs)
    v_refs = [jax.new_ref(v, memory_space=pltpu.MemorySpace.HBM) for v in vs]
    out_refs = [jax.empty_ref(jax.ShapeDtypeStruct((N_DEV,) + v.shape, v.dtype), memory_space=pltpu.MemorySpace.HBM)
                for v in vs]

    @pl.kernel(mesh=plsc.ScalarSubcoreMesh(axis_name="seq", num_cores=1), name=name,
               scratch_types=(pltpu.SemaphoreType.DMA((7 * n,)), pltpu.SemaphoreType.DMA((7 * n,)),
                              pltpu.SemaphoreType.DMA((n,))),
               compiler_params=pltpu.CompilerParams(collective_id=collective_id))
    def launch(send_sems, recv_sems, local_sems):
        x, y, c = _place()
        me, sibling = (x, y, c), (x, y, 1 - c)
        chips = _other_chips(x, y)
        peers = [sibling] + [(*chip, c) for chip in chips]
        barrier = pltpu.get_barrier_semaphore()
        for peer in peers:
            pl.semaphore_signal(barrier, inc=1, device_id=peer, device_id_type=MESH)
        pl.semaphore_wait(barrier, len(peers))

        def slab(a, px, py, pc):
            return out_refs[a].at[4 * px + 2 * py + pc]

        def copy(a, k, block, to, src=None):
            return pltpu.make_async_remote_copy(
                src_ref=slab(a, *block) if src is None else src, dst_ref=slab(a, *block),
                send_sem=send_sems.at[7 * a + k], recv_sem=recv_sems.at[7 * a + k], device_id=to, device_id_type=MESH)

        mine = [pltpu.make_async_copy(v_refs[a], slab(a, *me), local_sems.at[a]) for a in range(n)]
        for cp in mine:
            cp.start()
        first = []
        for a in range(n):
            first.append(copy(a, 0, me, sibling, src=v_refs[a]))
            first += [copy(a, 1 + j, me, (*chip, c), src=v_refs[a]) for j, chip in enumerate(chips)]
        for cp in first:
            cp.start()
        passed = []
        for j, chip in enumerate(chips):
            for a in range(n):
                copy(a, 1 + j, (*chip, c), me).wait_recv()
                cp = copy(a, 4 + j, (*chip, c), sibling)
                cp.start()
                passed.append(cp)
        for a in range(n):
            copy(a, 0, sibling, me).wait_recv()
            for j, chip in enumerate(chips):
                copy(a, 4 + j, (*chip, 1 - c), me).wait_recv()
        for cp in first + passed:
            cp.wait_send()
        for cp in mine:
            cp.wait()

    launch()
    return [r[...] for r in out_refs]


def _rs_sibling(g8s, *, name):
    n = len(g8s)

    def body(*refs):
        g_refs, out_refs = refs[:n], refs[n:2 * n]
        send_sems, recv_sems = refs[2 * n:]
        x, y, c = _place()
        copies = [
            pltpu.make_async_remote_copy(
                src_ref=g_refs[a].at[2 * k + 1 - c], dst_ref=out_refs[a].at[k],
                send_sem=send_sems.at[4 * a + k], recv_sem=recv_sems.at[4 * a + k],
                device_id=(x, y, 1 - c), device_id_type=MESH)
            for a in range(n) for k in range(4)]
        for cp in copies:
            cp.start()
        for cp in copies:
            cp.wait()

    return pl.pallas_call(
        body, name=name, in_specs=[ANY] * n, out_specs=[ANY] * n,
        out_shape=[jax.ShapeDtypeStruct((4,) + g.shape[1:], g.dtype) for g in g8s],
        scratch_shapes=[pltpu.SemaphoreType.DMA((4 * n,)), pltpu.SemaphoreType.DMA((4 * n,))],
    )(*g8s)


def _handshake(peers):
    barrier = pltpu.get_barrier_semaphore()
    for peer in peers:
        pl.semaphore_signal(barrier, inc=1, device_id=peer, device_id_type=MESH)
    pl.semaphore_wait(barrier, len(peers))


def _hbm_refs(arrays, lead):
    src = [jax.new_ref(a, memory_space=pltpu.MemorySpace.HBM) for a in arrays]
    dst = [jax.empty_ref(jax.ShapeDtypeStruct((lead,) + a.shape[1:], a.dtype), memory_space=pltpu.MemorySpace.HBM)
           for a in arrays]
    return src, dst


def _rs_sibling_async(g8s, *, name, collective_id):
    n = len(g8s)
    g_refs, out_refs = _hbm_refs(g8s, 4)

    @pl.kernel(mesh=plsc.ScalarSubcoreMesh(axis_name="seq", num_cores=1), name=name,
               scratch_types=(pltpu.SemaphoreType.DMA((4 * n,)), pltpu.SemaphoreType.DMA((4 * n,))),
               compiler_params=pltpu.CompilerParams(collective_id=collective_id))
    def launch(send_sems, recv_sems):
        x, y, c = _place()
        _handshake([(x, y, 1 - c)])
        copies = [
            pltpu.make_async_remote_copy(
                src_ref=g_refs[a].at[2 * k + 1 - c], dst_ref=out_refs[a].at[k],
                send_sem=send_sems.at[4 * a + k], recv_sem=recv_sems.at[4 * a + k],
                device_id=(x, y, 1 - c), device_id_type=MESH)
            for a in range(n) for k in range(4)]
        for cp in copies:
            cp.start()
        for cp in copies:
            cp.wait()

    launch()
    return [r[...] for r in out_refs]


def _rs_chips_async(h4s, *, name, collective_id):
    n = len(h4s)
    h_refs, out_refs = _hbm_refs(h4s, 3)

    @pl.kernel(mesh=plsc.ScalarSubcoreMesh(axis_name="seq", num_cores=1), name=name,
               scratch_types=(pltpu.SemaphoreType.DMA((3 * n,)), pltpu.SemaphoreType.DMA((3 * n,))),
               compiler_params=pltpu.CompilerParams(collective_id=collective_id))
    def launch(send_sems, recv_sems):
        x, y, c = _place()
        chips = _other_chips(x, y)
        _handshake([(cx, cy, c) for cx, cy in chips])
        copies = [
            pltpu.make_async_remote_copy(
                src_ref=h_refs[a].at[2 * cx + cy], dst_ref=out_refs[a].at[j],
                send_sem=send_sems.at[3 * a + j], recv_sem=recv_sems.at[3 * a + j],
                device_id=(cx, cy, c), device_id_type=MESH)
            for a in range(n) for j, (cx, cy) in enumerate(chips)]
        for cp in copies:
            cp.start()
        for cp in copies:
            cp.wait()

    launch()
    return [r[...] for r in out_refs]


def _row_tile(rows):
    return rows if rows <= 512 else 256


def _rs_chip_sum(g8, from_sibling, place_idx, *, name):
    _, R, C = g8.shape
    tr = _row_tile(R)

    def body(pi_ref, a_ref, b_ref, f_ref, h_ref):
        s = a_ref[...] + b_ref[...]
        h_ref[...] = s.astype(BF16)

        @pl.when(pl.program_id(1) == pi_ref[1])
        def _():
            f_ref[...] = s

    blk = pl.BlockSpec((None, tr, C), lambda r, k, pi_ref: (k, r, 0))
    return pl.pallas_call(
        body, name=name,
        grid_spec=pltpu.PrefetchScalarGridSpec(
            num_scalar_prefetch=1, grid=(R // tr, 4),
            in_specs=[pl.BlockSpec((None, tr, C), lambda r, k, pi_ref: (2 * k + pi_ref[0], r, 0)), blk],
            out_specs=[pl.BlockSpec((tr, C), lambda r, k, pi_ref: (r, 0)), blk]),
        out_shape=[jax.ShapeDtypeStruct((R, C), F32), jax.ShapeDtypeStruct((4, R, C), BF16)],
        compiler_params=_cparams(("parallel", "arbitrary")),
    )(place_idx, g8, from_sibling)


def _rs_chips(h4s, *, name):
    n = len(h4s)

    def body(*refs):
        h_refs, out_refs = refs[:n], refs[n:2 * n]
        send_sems, recv_sems = refs[2 * n:]
        x, y, c = _place()
        copies = [
            pltpu.make_async_remote_copy(
                src_ref=h_refs[a].at[2 * cx + cy], dst_ref=out_refs[a].at[j],
                send_sem=send_sems.at[3 * a + j], recv_sem=recv_sems.at[3 * a + j],
                device_id=(cx, cy, c), device_id_type=MESH)
            for a in range(n) for j, (cx, cy) in enumerate(_other_chips(x, y))]
        for cp in copies:
            cp.start()
        for cp in copies:
            cp.wait()

    return pl.pallas_call(
        body, name=name, in_specs=[ANY] * n, out_specs=[ANY] * n,
        out_shape=[jax.ShapeDtypeStruct((3,) + h.shape[1:], h.dtype) for h in h4s],
        scratch_shapes=[pltpu.SemaphoreType.DMA((3 * n,)), pltpu.SemaphoreType.DMA((3 * n,))],
    )(*h4s)


def _split_moves(segments, chunk):
    moves = []
    for dst, src, length in segments:
        while length > 0:
            dev, off = divmod(src, chunk)
            take = min(length, chunk - off)
            moves.append((dst, dev, off, take))
            dst, src, length = dst + take, src + take, length - take
    return moves


def _assemble(stacked, segments, zero_spans, out_cols, *, name):
    _, R, c = stacked.shape
    tr = _row_tile(R)
    moves = _split_moves(segments, c)

    def body(x_ref, o_ref):
        for dst, dev, off, take in moves:
            o_ref[:, dst:dst + take] = x_ref[dev, :, off:off + take]
        for a, b in zero_spans:
            o_ref[:, a:b] = jnp.zeros((tr, b - a), o_ref.dtype)

    return pl.pallas_call(
        body, name=name, grid=(R // tr,),
        in_specs=[pl.BlockSpec((N_DEV, tr, c), lambda i: (0, i, 0))],
        out_specs=pl.BlockSpec((tr, out_cols), lambda i: (i, 0)),
        out_shape=jax.ShapeDtypeStruct((R, out_cols), stacked.dtype),
        compiler_params=_cparams(("parallel",)),
    )(stacked)


def _disassemble(full, segments, chunk, *, name):
    R = full.shape[0]
    tr = _row_tile(R)
    moves = _split_moves(segments, chunk)

    def body(x_ref, o_ref):
        seen = set()
        for dst, dev, off, take in moves:
            piece = x_ref[:, dst:dst + take]
            if (dev, off) in seen:
                piece = piece + o_ref[dev, :, off:off + take]
            seen.add((dev, off))
            o_ref[dev, :, off:off + take] = piece

    return pl.pallas_call(
        body, name=name, grid=(R // tr,),
        in_specs=[pl.BlockSpec((tr, full.shape[1]), lambda i: (i, 0))],
        out_specs=pl.BlockSpec((N_DEV, tr, chunk), lambda i: (0, i, 0)),
        out_shape=jax.ShapeDtypeStruct((N_DEV, R, chunk), F32),
        compiler_params=_cparams(("parallel",)),
    )(full)


_O_CQ = 3 * SB_W
_O_CKV = _O_CQ + Q_LORA
_O_KR = _O_CKV + KV_LORA
SEG_W_IN = ((0, 0, 3 * SB_W), (P_CKV, _O_CKV, KV_LORA), (P_KRT, _O_KR, MLA_ROPE), (P_KRT + MLA_ROPE, _O_KR, MLA_ROPE),
            (P_CQ, _O_CQ, Q_LORA))
ZERO_W_IN = ((P_KRT + 2 * MLA_ROPE, P_CQ),)
SEG_W_UQ = tuple((MLA_NOPE * h, MLA_QK * h, MLA_NOPE) for h in range(MLA_HEADS)) + tuple(
    (MLA_W + LANES * (h // 2) + MLA_ROPE * (h % 2), MLA_QK * h + MLA_NOPE, MLA_ROPE) for h in range(MLA_HEADS))
ZERO_W_UQ = tuple((MLA_W + LANES * g + 2 * MLA_ROPE, MLA_W + LANES * (g + 1)) for g in range(MLA_HEADS // 2))
SEG_W_UKV = tuple((MLA_NOPE * h, (MLA_NOPE + MLA_V) * h, MLA_NOPE) for h in range(MLA_HEADS)) + tuple(
    (MLA_W + MLA_V * h, (MLA_NOPE + MLA_V) * h + MLA_NOPE, MLA_V) for h in range(MLA_HEADS))
SEG_W_UP = tuple((2 * FF_BLK * blk + FF_BLK * half, D_FF * half + FF_BLK * blk, FF_BLK)
                 for half in range(2) for blk in range(N_FF_BLK))


def _sum8(g, *, name):
    _, R, C = g.shape

    def body(g_ref, o_ref):
        acc = g_ref[0]
        for k in range(1, N_DEV):
            acc = acc + g_ref[k]
        o_ref[...] = acc

    return pl.pallas_call(
        body, name=name, out_shape=jax.ShapeDtypeStruct((R, C), F32),
    )(g)


def _adamw_math(w, gf, m, v):
    c1 = 1.0 / (1.0 - ADAM_B1 ** ADAM_STEP)
    c2 = 1.0 / (1.0 - ADAM_B2 ** ADAM_STEP)
    mn = ADAM_B1 * m + (1.0 - ADAM_B1) * gf
    vn = ADAM_B2 * v + (1.0 - ADAM_B2) * (gf * gf)
    return -ADAM_LR * ((mn * c1) / (jnp.sqrt(vn * c2) + ADAM_EPS) + ADAM_WD * w), mn, vn


def _adamw(w, g, m, v, *, name):
    R, C = w.shape
    tr = _row_tile(R)

    def body(w_ref, g_ref, m_ref, v_ref, d_ref, mo_ref, vo_ref):
        d_ref[...], mo_ref[...], vo_ref[...] = _adamw_math(w_ref[...], g_ref[...], m_ref[...], v_ref[...])

    blk = pl.BlockSpec((tr, C), lambda i: (i, 0))
    shp = jax.ShapeDtypeStruct((R, C), F32)
    return pl.pallas_call(
        body, name=name, grid=(R // tr,), in_specs=[blk] * 4, out_specs=[blk] * 3,
        out_shape=[shp, shp, shp], compiler_params=_cparams(("parallel",)),
    )(w, g, m, v)


def _adamw_rs(own, r3, w, m, v, *, name):
    R, C = w.shape
    tr = _row_tile(R)

    def body(f_ref, r_ref, w_ref, m_ref, v_ref, g_ref, d_ref, mo_ref, vo_ref):
        gf = ((f_ref[...] + r_ref[0].astype(F32)) + r_ref[1].astype(F32)) + r_ref[2].astype(F32)
        g_ref[...] = gf
        d_ref[...], mo_ref[...], vo_ref[...] = _adamw_math(w_ref[...], gf, m_ref[...], v_ref[...])

    blk = pl.BlockSpec((tr, C), lambda i: (i, 0))
    shp = jax.ShapeDtypeStruct((R, C), F32)
    return pl.pallas_call(
        body, name=name, grid=(R // tr,),
        in_specs=[blk, pl.BlockSpec((3, tr, C), lambda i: (0, i, 0)), blk, blk, blk], out_specs=[blk] * 4,
        out_shape=[shp] * 4, compiler_params=_cparams(("parallel",)),
    )(own, r3, w, m, v)


def _ff_interleave(a):
    lead = a.shape[:-1]
    return a.reshape(*lead, 2, N_FF_BLK, FF_BLK).swapaxes(-3, -2).reshape(*lead, 2 * D_FF)


def _ff_deinterleave(a):
    lead = a.shape[:-1]
    return a.reshape(*lead, N_FF_BLK, 2, FF_BLK).swapaxes(-3, -2).reshape(*lead, 2 * D_FF)


SMALL =(("g_mix", D_MODEL), ("g_cq", Q_LORA), ("g_ckv", KV_LORA), ("g_sb_out", SB_W), ("g_mla_out", MLA_W),
         ("g_ffn", D_MODEL), ("conv_b", 2 * D_FF), ("g_final", D_MODEL))
SMALL_ROWS = 88


def _pack_small(d):
    flat = jnp.concatenate([d[n].reshape(-1) for n, _ in SMALL])
    flat = jnp.pad(flat, (0, SMALL_ROWS * LANES - flat.shape[0]))
    return flat.reshape(SMALL_ROWS, LANES)


def _unpack_small(a):
    flat = a.reshape(-1)
    out, off = {}, 0
    for n, size in SMALL:
        out[n] = flat[off:off + size]
        off += size
    return out


def kernel(x, positions, g_mix, w_in, g_cq, w_uq, g_ckv, w_ukv, g_sb_out, g_mla_out, w_out, g_ffn, w_up, conv_w, conv_b, w_down, g_final, loss_target, m_g_mix, m_w_in, m_g_cq, m_w_uq, m_g_ckv, m_w_ukv, m_g_sb_out, m_g_mla_out, m_w_out, m_g_ffn, m_w_up, m_conv_w, m_conv_b, m_w_down, m_g_final, v_g_mix, v_w_in, v_g_cq, v_w_uq, v_g_ckv, v_w_ukv, v_g_sb_out, v_g_mla_out, v_w_out, v_g_ffn, v_w_up, v_conv_w, v_conv_b, v_w_down, v_g_final):
    B, S, D = x.shape
    T = B * S
    xf = x.reshape(T, D)
    tgt = loss_target.reshape(T, D)
    pos = positions.reshape(T, 1)
    half = MLA_ROPE // 2
    inv_freq = 1.0 / (ROPE_BASE ** (jnp.arange(half, dtype=F32) * (2.0 / MLA_ROPE)))
    invf = jnp.tile(inv_freq, LANES // half).reshape(1, LANES)
    place_idx = jnp.stack([lax.axis_index("c"), 2 * lax.axis_index("x") + lax.axis_index("y")]).astype(jnp.int32)

    names = ("w_in", "w_uq", "w_ukv", "w_out", "w_up", "w_down", "conv_w")
    shard = {"w_in": w_in[0], "w_uq": w_uq[0], "w_ukv": w_ukv[0], "w_out": w_out[0], "w_up": w_up[0],
             "w_down": w_down[0], "conv_w": conv_w[0]}
    sent = {n: shard[n] if n == "conv_w" else shard[n].astype(BF16) for n in names}
    later = names[1:]
    w_in_all = _all_gather([sent["w_in"]], name="ag_w_in")[0]
    w_in_all, rest = lax.optimization_barrier((w_in_all, [sent[n] for n in later]))
    got = {"w_in": w_in_all}
    got.update(zip(later, _all_gather_async(rest, name="ag_weights_async", collective_id=0)))
    wi = _assemble(got["w_in"], SEG_W_IN, ZERO_W_IN, P_COLS, name="asm_w_in")
    wuq = _assemble(got["w_uq"], SEG_W_UQ, ZERO_W_UQ, 2 * MLA_W, name="asm_w_uq")
    wukv = _assemble(got["w_ukv"], SEG_W_UKV, (), 2 * MLA_W, name="asm_w_ukv")
    wup = _assemble(got["w_up"], SEG_W_UP, (), 2 * D_FF, name="asm_w_up")
    cwi = _assemble(got["conv_w"], SEG_W_UP, (), 2 * D_FF, name="asm_conv_w")
    wo = got["w_out"].reshape(D, D)
    wdn = got["w_down"].reshape(D_FF, D)
    cbi = _ff_interleave(conv_b)

    h = _rms_fwd(xf, g_mix, tm=512, name="rms_mix")
    p = _matmul_nn(h, wi, tm=1024, tn=P_COLS // 2, out_dtype=F32, name="proj_in")
    o_sb, ltot = _sb_fwd(p, seq=S, name="sb_fwd")
    cq = _rms_fwd(p, g_cq, tm=512, name="rms_cq", col_block=P_CQ // Q_LORA)
    ckv = _rms_fwd(p, g_ckv, tm=512, name="rms_ckv", col_block=P_CKV // KV_LORA)
    qfull = _matmul_nn(cq, wuq, tm=512, tn=1024, out_dtype=F32, name="proj_uq")
    kvm = _matmul_nn(ckv, wukv, tm=512, tn=1024, out_dtype=BF16, name="proj_ukv")
    qm, krt = _rope_fwd(qfull, p, pos, invf, tm=512, name="rope_fwd")
    o_mla, lse = _mla_fwd(qm, kvm, krt, seq=S, name="mla_fwd")
    ocat = _rms2_fwd(o_sb, o_mla, g_sb_out, g_mla_out, tm=512, name="rms_heads")
    x1 = _matmul_nn(ocat, wo, tm=512, tn=1024, out_dtype=F32, name="proj_out", residual=xf)
    hf = _rms_fwd(x1, g_ffn, tm=512, name="rms_ffn")
    u = _matmul_nn(hf, wup, tm=1024, tn=512, out_dtype=F32, name="ffn_up")
    a = _conv_fwd(u, cwi, cbi, seq=S, name="conv_fwd")
    x2 = _matmul_nn(a, wdn, tm=512, tn=1024, out_dtype=F32, name="ffn_down", residual=x1)
    dx2, dg_final, loss_row = _final_loss(x2, g_final.reshape(1, D), tgt, tm=512, name="final_loss")

    da = _matmul_nt(dx2, wdn, tm=1024, tn=D_FF // 2, out_dtype=F32, name="d_ffn_down")
    dw_down = _matmul_tn(a, dx2, tm=D_FF // 2, tn=1024, tk=1024, name="dw_down")
    du, dcw, dcb = _conv_bwd(u, da, cwi, cbi, seq=S, name="conv_bwd")
    dhf = _matmul_nt(du, wup, tm=512, tn=512, out_dtype=F32, name="d_ffn_up")
    dw_up = _matmul_tn(hf, du, tm=1024, tn=D_FF, tk=1024, name="dw_up")
    dx1, dg_ffn = _rms_bwd(dhf, x1, g_ffn, tm=512, name="rms_ffn_bwd", residual=dx2)
    docat = _matmul_nt(dx1, wo, tm=512, tn=1024, out_dtype=F32, name="d_proj_out")
    dw_out = _matmul_tn(ocat, dx1, tm=1024, tn=1024, tk=1024, name="dw_out")
    do_sb, do_mla, dg_sb, dg_mla = _rms2_bwd(docat, o_sb, o_mla, g_sb_out, g_mla_out, tm=512, name="rms_heads_bwd")
    dq_sb, dk_sb, dv_sb = _sb_bwd(p, ltot, do_sb, seq=S, name="sb_bwd")
    dqn, dqr, dkn, dvm, dkr = _mla_bwd(qm, kvm, krt, o_mla, lse, do_mla, seq=S, name="mla_bwd")
    dqr_u, dkr_u = _rope_bwd(dqr, dkr, pos, invf, tm=512, name="rope_bwd")
    dqm = jnp.concatenate([dqn, dqr_u], axis=1)
    dkvm = jnp.concatenate([dkn, dvm], axis=1)
    dcq_n = _matmul_nt(dqm, wuq, tm=512, tn=Q_LORA, out_dtype=F32, name="d_proj_uq")
    dw_uq = _matmul_tn(cq, dqm, tm=Q_LORA, tn=1024, tk=1024, name="dw_uq")
    dckv_n = _matmul_nt(dkvm, wukv, tm=512, tn=KV_LORA, out_dtype=F32, name="d_proj_ukv")
    dw_ukv = _matmul_tn(ckv, dkvm, tm=KV_LORA, tn=1024, tk=1024, name="dw_ukv")
    dcq, dg_cq = _rms_bwd(dcq_n, p, g_cq, tm=512, name="rms_cq_bwd", col_block=P_CQ // Q_LORA, out_dtype=BF16)
    dckv, dg_ckv = _rms_bwd(dckv_n, p, g_ckv, tm=512, name="rms_ckv_bwd", col_block=P_CKV // KV_LORA, out_dtype=BF16)
    dp = jnp.concatenate([dq_sb, dk_sb, dv_sb, dckv, dkr_u, dcq], axis=1)
    dh = _matmul_nt(dp, wi, tm=512, tn=1024, out_dtype=F32, name="d_proj_in")
    dw_in = _matmul_tn(h, dp, tm=1024, tn=P_COLS, tk=1024, name="dw_in")
    dx, dg_mix = _rms_bwd(dh, xf, g_mix, tm=512, name="rms_mix_bwd", residual=dx1)

    g8 = {"w_in": _disassemble(dw_in, SEG_W_IN, shard["w_in"].shape[1], name="split_dw_in"),
          "w_uq": _disassemble(dw_uq, SEG_W_UQ, shard["w_uq"].shape[1], name="split_dw_uq"),
          "w_ukv": _disassemble(dw_ukv, SEG_W_UKV, shard["w_ukv"].shape[1], name="split_dw_ukv"),
          "w_up": _disassemble(dw_up, SEG_W_UP, shard["w_up"].shape[1], name="split_dw_up"),
          "conv_w": _disassemble(dcw, SEG_W_UP, shard["conv_w"].shape[1], name="split_dconv_w"),
          "w_out": dw_out.reshape((N_DEV,) + shard["w_out"].shape),
          "w_down": dw_down.reshape((N_DEV,) + shard["w_down"].shape)}
    early = ("w_down", "w_up", "conv_w", "w_out")
    late = ("w_in", "w_uq", "w_ukv")
    own, r3 = {}, {}
    sib_e = _rs_sibling_async([g8[n] for n in early], name="rs_sibling_async", collective_id=1)
    sums_e = [_rs_chip_sum(g8[n], fs, place_idx, name="rs_chip_sum_" + n) for n, fs in zip(early, sib_e)]
    r3.update(zip(early, _rs_chips_async([h4 for _, h4 in sums_e], name="rs_chips_async", collective_id=2)))
    own.update({n: f for n, (f, _) in zip(early, sums_e)})
    sib_l = _rs_sibling([g8[n] for n in late], name="rs_sibling")
    sums_l = [_rs_chip_sum(g8[n], fs, place_idx, name="rs_chip_sum_" + n) for n, fs in zip(late, sib_l)]
    r3.update(zip(late, _rs_chips([h4 for _, h4 in sums_l], name="rs_chips")))
    own.update({n: f for n, (f, _) in zip(late, sums_l)})

    small_part = {"g_mix": dg_mix, "g_cq": dg_cq, "g_ckv": dg_ckv, "g_sb_out": dg_sb, "g_mla_out": dg_mla,
                  "g_ffn": dg_ffn, "conv_b": _ff_deinterleave(dcb), "g_final": dg_final}
    small_all, = _all_gather([_pack_small(small_part)], name="ag_small_grads")
    gsmall = _sum8(small_all, name="sum_small_grads")

    params = {"w_in": (w_in, m_w_in, v_w_in), "w_uq": (w_uq, m_w_uq, v_w_uq), "w_ukv": (w_ukv, m_w_ukv, v_w_ukv),
              "w_out": (w_out, m_w_out, v_w_out), "w_up": (w_up, m_w_up, v_w_up), "conv_w": (conv_w, m_conv_w, v_conv_w),
              "w_down": (w_down, m_w_down, v_w_down)}
    grad, delta, new_m, new_v = {}, {}, {}, {}
    for n, (w_, m_, v_) in params.items():
        g_, d_, mn_, vn_ = _adamw_rs(own[n], r3[n], w_[0], m_[0], v_[0], name="adamw_" + n)
        grad[n], delta[n], new_m[n], new_v[n] = g_[None], d_[None], mn_[None], vn_[None]
    small_w = {"g_mix": g_mix, "g_cq": g_cq, "g_ckv": g_ckv, "g_sb_out": g_sb_out, "g_mla_out": g_mla_out,
               "g_ffn": g_ffn, "conv_b": conv_b, "g_final": g_final}
    small_m = {"g_mix": m_g_mix, "g_cq": m_g_cq, "g_ckv": m_g_ckv, "g_sb_out": m_g_sb_out, "g_mla_out": m_g_mla_out,
               "g_ffn": m_g_ffn, "conv_b": m_conv_b, "g_final": m_g_final}
    small_v = {"g_mix": v_g_mix, "g_cq": v_g_cq, "g_ckv": v_g_ckv, "g_sb_out": v_g_sb_out, "g_mla_out": v_g_mla_out,
               "g_ffn": v_g_ffn, "conv_b": v_conv_b, "g_final": v_g_final}
    ds_, ms_, vs_ = _adamw(_pack_small(small_w), gsmall, _pack_small(small_m), _pack_small(small_v), name="adamw_small")
    for src, dst in ((_unpack_small(gsmall), grad), (_unpack_small(ds_), delta), (_unpack_small(ms_), new_m), (_unpack_small(vs_), new_v)):
        for n, _ in SMALL:
            dst[n] = src[n].reshape(small_w[n].shape)

    loss = lax.psum(loss_row[0, 0], MESH_AXES)
    order = ("g_mix", "w_in", "g_cq", "w_uq", "g_ckv", "w_ukv", "g_sb_out", "g_mla_out", "w_out", "g_ffn", "w_up",
             "conv_w", "conv_b", "w_down", "g_final")
    return (loss, dx.reshape(B, S, D), *[grad[n] for n in order], *[delta[n] for n in order],
            *[new_m[n] for n in order], *[new_v[n] for n in order])
```

```python
import jax
import jax.numpy as jnp
from jax import lax
from jax.experimental import pallas as pl
from jax.experimental.pallas import tpu as pltpu
from jax.experimental.pallas import tpu_sc as plsc

F32 = jnp.float32
BF16 = jnp.bfloat16

D_MODEL = 1024
SB_HEADS = 8
SB_HEAD_DIM = 64
MLA_HEADS = 8
MLA_NOPE = 64
MLA_ROPE = 32
MLA_V = 64
Q_LORA = 384
KV_LORA = 256
D_FF = 2816
ROPE_BASE = 10000.0
EPS = 1e-6
SB_W = SB_HEADS * SB_HEAD_DIM
MLA_W = MLA_HEADS * MLA_V
MLA_QK = MLA_NOPE + MLA_ROPE
IN_COLS = 3 * SB_W + Q_LORA + KV_LORA + MLA_ROPE

ADAM_LR = 0.001
ADAM_B1 = 0.9
ADAM_B2 = 0.999
ADAM_EPS = 1e-08
ADAM_WD = 0.01
ADAM_STEP = 10

N_DEV = 8
MESH_AXES = ("x", "y", "c")
LANES = 128
V7X_VMEM_LIMIT = 56 * 1024 * 1024
FF_BLK = 256
N_FF_BLK = D_FF // FF_BLK

P_Q, P_K, P_V = 0, SB_W, 2 * SB_W
P_CKV = 3 * SB_W
P_KRT = P_CKV + KV_LORA
P_CQ = P_KRT + LANES
P_COLS = P_CQ + Q_LORA

MESH = pl.DeviceIdType.MESH
ANY = pl.BlockSpec(memory_space=pl.ANY)


def _cparams(sem=None, vmem=V7X_VMEM_LIMIT):
    return pltpu.CompilerParams(dimension_semantics=sem, vmem_limit_bytes=vmem)


def _matmul_nn(a, b, *, tm, tn, out_dtype, name, residual=None):
    M, K = a.shape
    N = b.shape[1]
    assert M % tm == 0 and N % tn == 0, (name, a.shape, b.shape)
    in_specs = [pl.BlockSpec((tm, K), lambda i, j: (i, 0)), pl.BlockSpec((K, tn), lambda i, j: (0, j))]
    args = [a, b]
    if residual is not None:
        in_specs.append(pl.BlockSpec((tm, tn), lambda i, j: (i, j)))
        args.append(residual)

    def body(*refs):
        a_ref, b_ref = refs[0], refs[1]
        o_ref = refs[-1]
        acc = jnp.dot(a_ref[...].astype(BF16), b_ref[...], preferred_element_type=F32)
        if residual is not None:
            acc = acc + refs[2][...]
        o_ref[...] = acc.astype(out_dtype)

    return pl.pallas_call(
        body, name=name, grid=(M // tm, N // tn), in_specs=in_specs,
        out_specs=pl.BlockSpec((tm, tn), lambda i, j: (i, j)),
        out_shape=jax.ShapeDtypeStruct((M, N), out_dtype),
        compiler_params=_cparams(("parallel", "parallel")),
    )(*args)


def _matmul_nt(a, b, *, tm, tn, out_dtype, name):
    M, K = a.shape
    N = b.shape[0]
    assert M % tm == 0 and N % tn == 0, (name, a.shape, b.shape)

    def body(a_ref, b_ref, o_ref):
        acc = lax.dot_general(a_ref[...].astype(BF16), b_ref[...], (((1,), (1,)), ((), ())),
                              preferred_element_type=F32)
        o_ref[...] = acc.astype(out_dtype)

    return pl.pallas_call(
        body, name=name, grid=(M // tm, N // tn),
        in_specs=[pl.BlockSpec((tm, K), lambda i, j: (i, 0)), pl.BlockSpec((tn, K), lambda i, j: (j, 0))],
        out_specs=pl.BlockSpec((tm, tn), lambda i, j: (i, j)),
        out_shape=jax.ShapeDtypeStruct((M, N), out_dtype),
        compiler_params=_cparams(("parallel", "parallel")),
    )(a, b)


def _matmul_tn(a, b, *, tm, tn, tk, name):
    K, M = a.shape
    N = b.shape[1]
    assert M % tm == 0 and N % tn == 0 and K % tk == 0, (name, a.shape, b.shape)

    def body(a_ref, b_ref, o_ref):
        k = pl.program_id(2)
        part = lax.dot_general(a_ref[...].astype(BF16), b_ref[...].astype(BF16), (((0,), (0,)), ((), ())),
                               preferred_element_type=F32)

        @pl.when(k == 0)
        def _():
            o_ref[...] = part

        @pl.when(k > 0)
        def _():
            o_ref[...] += part

    return pl.pallas_call(
        body, name=name, grid=(M // tm, N // tn, K // tk),
        in_specs=[pl.BlockSpec((tk, tm), lambda i, j, k: (k, i)), pl.BlockSpec((tk, tn), lambda i, j, k: (k, j))],
        out_specs=pl.BlockSpec((tm, tn), lambda i, j, k: (i, j)),
        out_shape=jax.ShapeDtypeStruct((M, N), F32),
        compiler_params=_cparams(("parallel", "parallel", "arbitrary")),
    )(a, b)


def _rms(xf, g):
    r = lax.rsqrt(jnp.mean(xf * xf, axis=1, keepdims=True) + EPS)
    return (xf * r) * g


def _rms_grad(dyf, xf, g):
    r = lax.rsqrt(jnp.mean(xf * xf, axis=1, keepdims=True) + EPS)
    xh = xf * r
    dyg = dyf * g
    dx = r * (dyg - xh * jnp.mean(dyg * xh, axis=1, keepdims=True))
    return dx, jnp.sum(dyf * xh, axis=0, keepdims=True)


def _accumulate(ref, part):
    @pl.when(pl.program_id(0) == 0)
    def _():
        ref[...] = part

    @pl.when(pl.program_id(0) > 0)
    def _():
        ref[...] += part


def _rms_fwd(x, g, *, tm, name, col_block=0):
    T = x.shape[0]
    C = g.shape[1]

    def body(x_ref, g_ref, o_ref):
        o_ref[...] = _rms(x_ref[...], g_ref[...]).astype(BF16)

    return pl.pallas_call(
        body, name=name, grid=(T // tm,),
        in_specs=[pl.BlockSpec((tm, C), lambda i: (i, col_block)), pl.BlockSpec((1, C), lambda i: (0, 0))],
        out_specs=pl.BlockSpec((tm, C), lambda i: (i, 0)),
        out_shape=jax.ShapeDtypeStruct((T, C), BF16),
        compiler_params=_cparams(("parallel",)),
    )(x, g)


def _rms_bwd(dy, x, g, *, tm, name, residual=None, col_block=0, out_dtype=F32):
    T = dy.shape[0]
    C = g.shape[1]
    in_specs = [pl.BlockSpec((tm, C), lambda i: (i, 0)), pl.BlockSpec((tm, C), lambda i: (i, col_block)),
                pl.BlockSpec((1, C), lambda i: (0, 0))]
    args = [dy, x, g]
    if residual is not None:
        in_specs.append(pl.BlockSpec((tm, C), lambda i: (i, 0)))
        args.append(residual)

    def body(*refs):
        dy_ref, x_ref, g_ref = refs[:3]
        dx_ref, dg_ref = refs[-2:]
        dx, part = _rms_grad(dy_ref[...].astype(F32), x_ref[...], g_ref[...])
        if residual is not None:
            dx = dx + refs[3][...]
        dx_ref[...] = dx.astype(out_dtype)
        _accumulate(dg_ref, part)

    return pl.pallas_call(
        body, name=name, grid=(T // tm,), in_specs=in_specs,
        out_specs=[pl.BlockSpec((tm, C), lambda i: (i, 0)), pl.BlockSpec((1, C), lambda i: (0, 0))],
        out_shape=[jax.ShapeDtypeStruct((T, C), out_dtype), jax.ShapeDtypeStruct((1, C), F32)],
        compiler_params=_cparams(("arbitrary",)),
    )(*args)


def _rms2_fwd(xa, xb, ga, gb, *, tm, name):
    T, C = xa.shape

    def body(xa_ref, xb_ref, ga_ref, gb_ref, o_ref):
        o_ref[:, :C] = _rms(xa_ref[...], ga_ref[...]).astype(BF16)
        o_ref[:, C:] = _rms(xb_ref[...], gb_ref[...]).astype(BF16)

    row = pl.BlockSpec((tm, C), lambda i: (i, 0))
    gsp = pl.BlockSpec((1, C), lambda i: (0, 0))
    return pl.pallas_call(
        body, name=name, grid=(T // tm,), in_specs=[row, row, gsp, gsp],
        out_specs=pl.BlockSpec((tm, 2 * C), lambda i: (i, 0)),
        out_shape=jax.ShapeDtypeStruct((T, 2 * C), BF16),
        compiler_params=_cparams(("parallel",)),
    )(xa, xb, ga, gb)


def _rms2_bwd(dy, xa, xb, ga, gb, *, tm, name):
    T, C = xa.shape

    def body(dy_ref, xa_ref, xb_ref, ga_ref, gb_ref, dxa_ref, dxb_ref, dga_ref, dgb_ref):
        dxa, pa = _rms_grad(dy_ref[:, :C], xa_ref[...], ga_ref[...])
        dxb, pb = _rms_grad(dy_ref[:, C:], xb_ref[...], gb_ref[...])
        dxa_ref[...] = dxa
        dxb_ref[...] = dxb
        _accumulate(dga_ref, pa)
        _accumulate(dgb_ref, pb)

    row = pl.BlockSpec((tm, C), lambda i: (i, 0))
    gsp = pl.BlockSpec((1, C), lambda i: (0, 0))
    return pl.pallas_call(
        body, name=name, grid=(T // tm,),
        in_specs=[pl.BlockSpec((tm, 2 * C), lambda i: (i, 0)), row, row, gsp, gsp],
        out_specs=[row, row, gsp, gsp],
        out_shape=[jax.ShapeDtypeStruct((T, C), F32), jax.ShapeDtypeStruct((T, C), F32),
                   jax.ShapeDtypeStruct((1, C), F32), jax.ShapeDtypeStruct((1, C), F32)],
        compiler_params=_cparams(("arbitrary",)),
    )(dy, xa, xb, ga, gb)


def _final_loss(x2, g, tgt, *, tm, name):
    T, C = x2.shape

    def body(x_ref, g_ref, t_ref, dx_ref, dg_ref, loss_ref):
        xf = x_ref[...]
        gf = g_ref[...]
        err = _rms(xf, gf) - t_ref[...]
        lpart = 0.5 * jnp.sum(jnp.mean(err * err, axis=1, keepdims=True), axis=0, keepdims=True)
        dx, gpart = _rms_grad(err * (1.0 / C), xf, gf)
        dx_ref[...] = dx
        _accumulate(dg_ref, gpart)
        _accumulate(loss_ref, jnp.broadcast_to(lpart, (1, LANES)))

    return pl.pallas_call(
        body, name=name, grid=(T // tm,),
        in_specs=[pl.BlockSpec((tm, C), lambda i: (i, 0)), pl.BlockSpec((1, C), lambda i: (0, 0)),
                  pl.BlockSpec((tm, C), lambda i: (i, 0))],
        out_specs=[pl.BlockSpec((tm, C), lambda i: (i, 0)), pl.BlockSpec((1, C), lambda i: (0, 0)),
                   pl.BlockSpec((1, LANES), lambda i: (0, 0))],
        out_shape=[jax.ShapeDtypeStruct((T, C), F32), jax.ShapeDtypeStruct((1, C), F32),
                   jax.ShapeDtypeStruct((1, LANES), F32)],
        compiler_params=_cparams(("arbitrary",)),
    )(x2, g, tgt)


ATT_T = 256
ATT_PAIRS = 2
NEG_BIG = -1e30


def _lane_iota():
    return lax.broadcasted_iota(jnp.int32, (1, LANES), 1)


def _head_masks():
    first = _lane_iota() < SB_HEAD_DIM
    return first, jnp.logical_not(first)


def _pick(mask, x):
    return jnp.where(mask, x, jnp.zeros_like(x))


def _lane_value(t, lane):
    return jnp.sum(jnp.where(_lane_iota() == lane, t, 0.0), axis=1, keepdims=True)


def _split_hi_lo(x):
    hi = x.astype(BF16)
    lo = (x - hi.astype(F32)).astype(BF16)
    return jnp.concatenate([hi, lo], axis=1)


def _tri(n, kind):
    r = lax.broadcasted_iota(jnp.int32, (n, n), 0)
    c = lax.broadcasted_iota(jnp.int32, (n, n), 1)
    u = {"suffix_excl": r > c, "prefix_incl": r <= c, "prefix_excl": r < c}[kind].astype(BF16)
    return jnp.concatenate([u, u], axis=0)


def _dot_nt(a, b):
    return lax.dot_general(a, b, (((1,), (1,)), ((), ())), preferred_element_type=F32)


def _dot_tn(a, b):
    return lax.dot_general(a, b, (((0,), (0,)), ((), ())), preferred_element_type=F32)


def _dot(a, b):
    return jnp.dot(a, b, preferred_element_type=F32)


def _causal_mask(n, strict):
    r = lax.broadcasted_iota(jnp.int32, (n, n), 0)
    c = lax.broadcasted_iota(jnp.int32, (n, n), 1)
    return (c < r) if strict else (c <= r)


LOG2E = 1.4426950408889634


def _sb_logs(qh, kj, vis):
    z2 = _dot_nt(qh, kj) * LOG2E
    nk = jnp.maximum(z2, 0.0) + jnp.log2(1.0 + jnp.exp2(-jnp.abs(z2)))
    lb = z2 - nk
    if vis is not None:
        nk = jnp.where(vis, nk, 0.0)
    return lb, nk


def _sb_fwd(p, *, seq, name):
    T = p.shape[0]
    B = T // seq
    TQ = ATT_T
    nq = seq // TQ
    PP = ATT_PAIRS
    W = PP * LANES
    nstep = SB_W // W
    NH = 2 * PP

    def body(q_ref, k_ref, v_ref, o_ref, lt_ref, q_s, k_s, v_s):
        masks = _head_masks()
        q = q_ref[...] * (SB_HEAD_DIM ** -0.5)
        v = v_ref[...]
        k_s[...] = k_ref[...].astype(BF16)
        for h in range(NH):
            ps = slice((h // 2) * LANES, (h // 2 + 1) * LANES)
            hs = slice(h * LANES, (h + 1) * LANES)
            q_s[:, hs] = _pick(masks[h % 2], q[:, ps]).astype(BF16)
            v_s[:, hs] = _pick(masks[h % 2], v[:, ps]).astype(BF16)
        u_suf = _tri(TQ, "suffix_excl")
        vis = _causal_mask(TQ, True)

        def q_block(i, carry):
            q0 = pl.multiple_of(i * TQ, TQ)
            qs = [q_s[pl.ds(q0, TQ), h * LANES:(h + 1) * LANES] for h in range(NH)]

            def tile(k0, c, mask):
                rs, accs = list(c[:NH]), list(c[NH:])
                logs = [_sb_logs(qs[h], k_s[pl.ds(k0, TQ), (h // 2) * LANES:(h // 2 + 1) * LANES], mask) for h in range(NH)]
                sums = [_dot(_split_hi_lo(nk), u_suf) for _, nk in logs]
                for h in range(NH):
                    a = jnp.exp2(logs[h][0] - sums[h] - rs[h])
                    if mask is not None:
                        a = jnp.where(mask, a, 0.0)
                    accs[h // 2] = accs[h // 2] + _dot(a.astype(BF16), v_s[pl.ds(k0, TQ), h * LANES:(h + 1) * LANES])
                    rs[h] = rs[h] + jnp.sum(logs[h][1], axis=1, keepdims=True)
                return tuple(rs) + tuple(accs)

            zero = jnp.zeros((TQ, 1), F32)
            c = tile(q0, (zero,) * NH + (jnp.zeros((TQ, LANES), F32),) * PP, vis)

            def k_block(jj, c):
                return tile(pl.multiple_of((i - 1 - jj) * TQ, TQ), c, None)

            c = lax.fori_loop(0, i, k_block, c)
            for pr in range(PP):
                ps = slice(pr * LANES, (pr + 1) * LANES)
                o_ref[pl.ds(q0, TQ), ps] = c[NH + pr]
                lt_ref[pl.ds(q0, TQ), ps] = jnp.where(masks[0], c[2 * pr], c[2 * pr + 1])
            return carry

        lax.fori_loop(0, nq, q_block, 0)

    blk = lambda off: pl.BlockSpec((seq, W), lambda b, g: (b, off + g))
    out_blk = pl.BlockSpec((seq, W), lambda b, g: (b, g))
    return pl.pallas_call(
        body, name=name, grid=(B, nstep),
        in_specs=[blk(P_Q // W), blk(P_K // W), blk(P_V // W)],
        out_specs=[out_blk, out_blk],
        out_shape=[jax.ShapeDtypeStruct((T, SB_W), F32), jax.ShapeDtypeStruct((T, SB_W), F32)],
        scratch_shapes=[pltpu.VMEM((seq, NH * LANES), BF16), pltpu.VMEM((seq, W), BF16), pltpu.VMEM((seq, NH * LANES), BF16)],
        compiler_params=_cparams(("parallel", "parallel")),
    )(p, p, p)


def _sb_bwd(p, ltot, do, *, seq, name):
    T = p.shape[0]
    B = T // seq
    TQ = ATT_T
    nq = seq // TQ
    PP = ATT_PAIRS
    W = PP * LANES
    nstep = SB_W // W
    NH = 2 * PP
    scale = SB_HEAD_DIM ** -0.5

    def body(q_ref, k_ref, v_ref, lt_ref, do_ref, dq_ref, dk_ref, dv_ref, q_s, k_s, v_s, do_s, dk_s, dv_s):
        masks = _head_masks()
        q = q_ref[...] * scale
        dof = do_ref[...]
        k_s[...] = k_ref[...].astype(BF16)
        v_s[...] = v_ref[...].astype(BF16)
        for h in range(NH):
            ps = slice((h // 2) * LANES, (h // 2 + 1) * LANES)
            hs = slice(h * LANES, (h + 1) * LANES)
            q_s[:, hs] = _pick(masks[h % 2], q[:, ps]).astype(BF16)
            do_s[:, hs] = _pick(masks[h % 2], dof[:, ps]).astype(BF16)
        dk_s[...] = jnp.zeros_like(dk_s)
        dv_s[...] = jnp.zeros_like(dv_s)
        u_pin = _tri(TQ, "prefix_incl")
        u_pex = _tri(TQ, "prefix_excl")
        vis = _causal_mask(TQ, True)

        def q_block(i, carry):
            q0 = pl.multiple_of(i * TQ, TQ)
            qs = [q_s[pl.ds(q0, TQ), h * LANES:(h + 1) * LANES] for h in range(NH)]
            dos = [do_s[pl.ds(q0, TQ), h * LANES:(h + 1) * LANES] for h in range(NH)]
            lt = lt_ref[pl.ds(q0, TQ), :]
            lts = [_lane_value(lt[:, (h // 2) * LANES:(h // 2 + 1) * LANES], (h % 2) * SB_HEAD_DIM) for h in range(NH)]

            def tile(k0, c, mask):
                cs, gs, accs = list(c[:NH]), list(c[NH:2 * NH]), list(c[2 * NH:])
                kjs = [k_s[pl.ds(k0, TQ), pr * LANES:(pr + 1) * LANES] for pr in range(PP)]
                vjs = [v_s[pl.ds(k0, TQ), pr * LANES:(pr + 1) * LANES] for pr in range(PP)]
                logs = [_sb_logs(qs[h], kjs[h // 2], mask) for h in range(NH)]
                pins = [_dot(_split_hi_lo(nk), u_pin) for _, nk in logs]
                das = [_dot_nt(dos[h], vjs[h // 2]) for h in range(NH)]
                a_l, g_l = [], []
                for h in range(NH):
                    a = jnp.exp2(logs[h][0] - ((lts[h] - cs[h]) - pins[h]))
                    if mask is not None:
                        a = jnp.where(mask, a, 0.0)
                    a_l.append(a)
                    g_l.append(das[h] * a)
                pres = [_dot(_split_hi_lo(g), u_pex) for g in g_l]
                dz_l = []
                for h in range(NH):
                    dz = g_l[h] - jnp.exp2(logs[h][0]) * (g_l[h] + (pres[h] + gs[h]))
                    if mask is not None:
                        dz = jnp.where(mask, dz, 0.0)
                    dz_l.append(dz.astype(BF16))
                for h in range(NH):
                    accs[h] = accs[h] + _dot(dz_l[h], kjs[h // 2])
                for pr in range(PP):
                    ps = slice(pr * LANES, (pr + 1) * LANES)
                    ha, hb = 2 * pr, 2 * pr + 1
                    dk_s[pl.ds(k0, TQ), ps] += _dot_tn(dz_l[ha], qs[ha]) + _dot_tn(dz_l[hb], qs[hb])
                    dv_s[pl.ds(k0, TQ), ps] += _dot_tn(a_l[ha].astype(BF16), dos[ha]) + _dot_tn(a_l[hb].astype(BF16), dos[hb])
                for h in range(NH):
                    cs[h] = cs[h] + jnp.sum(logs[h][1], axis=1, keepdims=True)
                    gs[h] = gs[h] + jnp.sum(g_l[h], axis=1, keepdims=True)
                return tuple(cs) + tuple(gs) + tuple(accs)

            z1 = jnp.zeros((TQ, 1), F32)
            zl = jnp.zeros((TQ, LANES), F32)

            def k_block(j, c):
                return tile(pl.multiple_of(j * TQ, TQ), c, None)

            c = lax.fori_loop(0, i, k_block, (z1,) * (2 * NH) + (zl,) * NH)
            c = tile(q0, c, vis)
            for pr in range(PP):
                dq = jnp.where(masks[0], c[2 * NH + 2 * pr], c[2 * NH + 2 * pr + 1]) * scale
                dq_ref[pl.ds(q0, TQ), pr * LANES:(pr + 1) * LANES] = dq.astype(BF16)
            return carry

        lax.fori_loop(0, nq, q_block, 0)
        dk_ref[...] = dk_s[...].astype(BF16)
        dv_ref[...] = dv_s[...].astype(BF16)

    blk = lambda off: pl.BlockSpec((seq, W), lambda b, g: (b, off + g))
    out_blk = pl.BlockSpec((seq, W), lambda b, g: (b, g))
    return pl.pallas_call(
        body, name=name, grid=(B, nstep),
        in_specs=[blk(P_Q // W), blk(P_K // W), blk(P_V // W), out_blk, out_blk],
        out_specs=[out_blk, out_blk, out_blk],
        out_shape=[jax.ShapeDtypeStruct((T, SB_W), BF16) for _ in range(3)],
        scratch_shapes=[pltpu.VMEM((seq, NH * LANES), BF16), pltpu.VMEM((seq, W), BF16), pltpu.VMEM((seq, W), BF16),
                        pltpu.VMEM((seq, NH * LANES), BF16), pltpu.VMEM((seq, W), F32), pltpu.VMEM((seq, W), F32)],
        compiler_params=_cparams(("parallel", "parallel")),
    )(p, p, p, ltot, do)


def _mla_masks():
    lane = lax.broadcasted_iota(jnp.int32, (1, 2 * LANES), 1)
    ma = (lane < MLA_NOPE) | ((lane >= LANES) & (lane < LANES + MLA_ROPE))
    mb = ((lane >= MLA_NOPE) & (lane < LANES)) | ((lane >= LANES + MLA_ROPE) & (lane < LANES + 2 * MLA_ROPE))
    return ma, mb


def _mla_fwd(qm, kvm, krt, *, seq, name):
    T = qm.shape[0]
    B = T // seq
    TQ = ATT_T
    nq = seq // TQ
    PP = ATT_PAIRS
    W = PP * LANES
    nstep = MLA_W // W
    NH = 2 * PP
    CW = 2 * LANES
    scale = MLA_QK ** -0.5

    def body(qn_ref, qr_ref, kn_ref, v_ref, kr_ref, o_ref, lse_ref, q_s, kc_s, v_s):
        hm = _head_masks()
        mm = _mla_masks()
        v = v_ref[...]
        for pr in range(PP):
            ps = slice(pr * LANES, (pr + 1) * LANES)
            qc = jnp.concatenate([qn_ref[:, ps], qr_ref[:, ps]], axis=1)
            kc_s[:, pr * CW:(pr + 1) * CW] = jnp.concatenate([kn_ref[:, ps], kr_ref[...]], axis=1)
            for e in range(2):
                h = 2 * pr + e
                q_s[:, h * CW:(h + 1) * CW] = _pick(mm[e], qc)
                v_s[:, h * LANES:(h + 1) * LANES] = _pick(hm[e], v[:, ps])
        vis = _causal_mask(TQ, False)

        def q_block(i, carry):
            q0 = pl.multiple_of(i * TQ, TQ)
            qs = [q_s[pl.ds(q0, TQ), h * CW:(h + 1) * CW] for h in range(NH)]

            def tile(k0, c, mask):
                ms, ls, accs = list(c[:NH]), list(c[NH:2 * NH]), list(c[2 * NH:])
                ss = [_dot_nt(qs[h], kc_s[pl.ds(k0, TQ), (h // 2) * CW:(h // 2 + 1) * CW]) * scale for h in range(NH)]
                if mask is not None:
                    ss = [jnp.where(mask, s, NEG_BIG) for s in ss]
                m_new = [jnp.maximum(ms[h], jnp.max(ss[h], axis=1, keepdims=True)) for h in range(NH)]
                alphas = [jnp.exp(ms[h] - m_new[h]) for h in range(NH)]
                prs = [jnp.exp(ss[h] - m_new[h]) for h in range(NH)]
                outs = [_dot(prs[h].astype(BF16), v_s[pl.ds(k0, TQ), h * LANES:(h + 1) * LANES]) for h in range(NH)]
                ls = [alphas[h] * ls[h] + jnp.sum(prs[h], axis=1, keepdims=True) for h in range(NH)]
                for pr in range(PP):
                    accs[pr] = accs[pr] * jnp.where(hm[0], alphas[2 * pr], alphas[2 * pr + 1]) + outs[2 * pr] + outs[2 * pr + 1]
                return tuple(m_new) + tuple(ls) + tuple(accs)

            neg = jnp.full((TQ, 1), NEG_BIG, F32)
            z1 = jnp.zeros((TQ, 1), F32)

            def k_block(j, c):
                return tile(pl.multiple_of(j * TQ, TQ), c, None)

            c = lax.fori_loop(0, i, k_block, (neg,) * NH + (z1,) * NH + (jnp.zeros((TQ, LANES), F32),) * PP)
            c = tile(q0, c, vis)
            for pr in range(PP):
                ps = slice(pr * LANES, (pr + 1) * LANES)
                m_a, m_b, l_a, l_b = c[2 * pr], c[2 * pr + 1], c[NH + 2 * pr], c[NH + 2 * pr + 1]
                o_ref[pl.ds(q0, TQ), ps] = c[2 * NH + pr] / jnp.where(hm[0], l_a, l_b)
                lse_ref[pl.ds(q0, TQ), ps] = jnp.where(hm[0], m_a + jnp.log(l_a), m_b + jnp.log(l_b))
            return carry

        lax.fori_loop(0, nq, q_block, 0)

    blk = lambda off: pl.BlockSpec((seq, W), lambda b, g: (b, off + g))
    out_blk = pl.BlockSpec((seq, W), lambda b, g: (b, g))
    return pl.pallas_call(
        body, name=name, grid=(B, nstep),
        in_specs=[blk(0), blk(nstep), blk(0), blk(nstep), pl.BlockSpec((seq, LANES), lambda b, g: (b, 0))],
        out_specs=[out_blk, out_blk],
        out_shape=[jax.ShapeDtypeStruct((T, MLA_W), F32), jax.ShapeDtypeStruct((T, MLA_W), F32)],
        scratch_shapes=[pltpu.VMEM((seq, NH * CW), BF16), pltpu.VMEM((seq, PP * CW), BF16), pltpu.VMEM((seq, NH * LANES), BF16)],
        compiler_params=_cparams(("parallel", "parallel")),
    )(qm, qm, kvm, kvm, krt)


def _mla_bwd(qm, kvm, krt, o, lse, do, *, seq, name):
    T = qm.shape[0]
    B = T // seq
    TQ = ATT_T
    nq = seq // TQ
    PP = ATT_PAIRS
    W = PP * LANES
    nstep = MLA_W // W
    NH = 2 * PP
    CW = 2 * LANES
    scale = MLA_QK ** -0.5

    def body(qn_ref, qr_ref, kn_ref, v_ref, kr_ref, o_ref, lse_ref, do_ref,
             dqn_ref, dqr_ref, dkn_ref, dv_ref, dkr_ref, q_s, kc_s, do_s, dkc_s, dv_s):
        hm = _head_masks()
        mm = _mla_masks()
        dof = do_ref[...]
        for pr in range(PP):
            ps = slice(pr * LANES, (pr + 1) * LANES)
            qc = jnp.concatenate([qn_ref[:, ps], qr_ref[:, ps]], axis=1)
            kc_s[:, pr * CW:(pr + 1) * CW] = jnp.concatenate([kn_ref[:, ps], kr_ref[...]], axis=1)
            for e in range(2):
                h = 2 * pr + e
                q_s[:, h * CW:(h + 1) * CW] = _pick(mm[e], qc)
                do_s[:, h * LANES:(h + 1) * LANES] = _pick(hm[e], dof[:, ps]).astype(BF16)
        dkc_s[...] = jnp.zeros_like(dkc_s)
        dv_s[...] = jnp.zeros_like(dv_s)
        vis = _causal_mask(TQ, False)

        def q_block(i, carry):
            q0 = pl.multiple_of(i * TQ, TQ)
            qs = [q_s[pl.ds(q0, TQ), h * CW:(h + 1) * CW] for h in range(NH)]
            dos = [do_s[pl.ds(q0, TQ), h * LANES:(h + 1) * LANES] for h in range(NH)]
            lse_t = lse_ref[pl.ds(q0, TQ), :]
            dd = do_ref[pl.ds(q0, TQ), :] * o_ref[pl.ds(q0, TQ), :]
            lses, ds_ = [], []
            for h in range(NH):
                ps = slice((h // 2) * LANES, (h // 2 + 1) * LANES)
                lses.append(_lane_value(lse_t[:, ps], (h % 2) * MLA_V))
                ds_.append(jnp.sum(_pick(hm[h % 2], dd[:, ps]), axis=1, keepdims=True))

            def tile(k0, c, mask):
                accs = list(c)
                kcs = [kc_s[pl.ds(k0, TQ), pr * CW:(pr + 1) * CW] for pr in range(PP)]
                vjs = [v_ref[pl.ds(k0, TQ), pr * LANES:(pr + 1) * LANES] for pr in range(PP)]
                ss = [_dot_nt(qs[h], kcs[h // 2]) * scale for h in range(NH)]
                dps = [_dot_nt(dos[h], vjs[h // 2]) for h in range(NH)]
                p_l, ds_l = [], []
                for h in range(NH):
                    pr_ = jnp.exp(ss[h] - lses[h])
                    if mask is not None:
                        pr_ = jnp.where(mask, pr_, 0.0)
                    p_l.append(pr_.astype(BF16))
                    ds_l.append((pr_ * (dps[h] - ds_[h]) * scale).astype(BF16))
                for h in range(NH):
                    accs[h] = accs[h] + _dot(ds_l[h], kcs[h // 2])
                for pr in range(PP):
                    ha, hb = 2 * pr, 2 * pr + 1
                    dkc_s[pl.ds(k0, TQ), pr * CW:(pr + 1) * CW] += _dot_tn(ds_l[ha], qs[ha]) + _dot_tn(ds_l[hb], qs[hb])
                    dv_s[pl.ds(k0, TQ), pr * LANES:(pr + 1) * LANES] += _dot_tn(p_l[ha], dos[ha]) + _dot_tn(p_l[hb], dos[hb])
                return tuple(accs)

            zc = jnp.zeros((TQ, CW), F32)

            def k_block(j, c):
                return tile(pl.multiple_of(j * TQ, TQ), c, None)

            c = lax.fori_loop(0, i, k_block, (zc,) * NH)
            c = tile(q0, c, vis)
            for pr in range(PP):
                ps = slice(pr * LANES, (pr + 1) * LANES)
                dq = _pick(mm[0], c[2 * pr]) + _pick(mm[1], c[2 * pr + 1])
                dqn_ref[pl.ds(q0, TQ), ps] = dq[:, :LANES].astype(BF16)
                dqr_ref[pl.ds(q0, TQ), ps] = dq[:, LANES:]
            return carry

        lax.fori_loop(0, nq, q_block, 0)
        dkr = dkc_s[:, LANES:CW]
        for pr in range(PP):
            dkn_ref[:, pr * LANES:(pr + 1) * LANES] = dkc_s[:, pr * CW:pr * CW + LANES].astype(BF16)
            if pr > 0:
                dkr = dkr + dkc_s[:, pr * CW + LANES:(pr + 1) * CW]
        dv_ref[...] = dv_s[...].astype(BF16)
        g = pl.program_id(1)

        @pl.when(g == 0)
        def _():
            dkr_ref[...] = dkr

        @pl.when(g > 0)
        def _():
            dkr_ref[...] += dkr

    blk = lambda off: pl.BlockSpec((seq, W), lambda b, g: (b, off + g))
    out_blk = pl.BlockSpec((seq, W), lambda b, g: (b, g))
    one_blk = pl.BlockSpec((seq, LANES), lambda b, g: (b, 0))
    return pl.pallas_call(
        body, name=name, grid=(B, nstep),
        in_specs=[blk(0), blk(nstep), blk(0), blk(nstep), one_blk, out_blk, out_blk, out_blk],
        out_specs=[out_blk, out_blk, out_blk, out_blk, one_blk],
        out_shape=[jax.ShapeDtypeStruct((T, MLA_W), BF16), jax.ShapeDtypeStruct((T, MLA_W), F32),
                   jax.ShapeDtypeStruct((T, MLA_W), BF16), jax.ShapeDtypeStruct((T, MLA_W), BF16),
                   jax.ShapeDtypeStruct((T, LANES), F32)],
        scratch_shapes=[pltpu.VMEM((seq, NH * CW), BF16), pltpu.VMEM((seq, PP * CW), BF16), pltpu.VMEM((seq, NH * LANES), BF16),
                        pltpu.VMEM((seq, PP * CW), F32), pltpu.VMEM((seq, W), F32)],
        compiler_params=_cparams(("parallel", "arbitrary")),
    )(qm, qm, kvm, kvm, krt, o, lse, do)


def _rope_tables(pos_ref, invf_ref):
    ang = pos_ref[...].astype(F32) * invf_ref[...]
    first = (_lane_iota() % MLA_ROPE) < (MLA_ROPE // 2)
    return jnp.cos(ang), jnp.sin(ang), first


def _rope_apply(x, cos, sin, first):
    rot = jnp.where(first, -pltpu.roll(x, LANES - MLA_ROPE // 2, 1), pltpu.roll(x, MLA_ROPE // 2, 1))
    return x * cos + rot * sin


def _rope_apply_t(dy, cos, sin, first):
    dys = dy * sin
    rot_t = jnp.where(first, pltpu.roll(dys, LANES - MLA_ROPE // 2, 1), -pltpu.roll(dys, MLA_ROPE // 2, 1))
    return dy * cos + rot_t


def _rope_fwd(qfull, p, pos, invf, *, tm, name):
    T = qfull.shape[0]
    ntile = MLA_W // LANES

    def body(q_ref, kr_ref, pos_ref, invf_ref, qm_ref, krt_ref):
        cos, sin, first = _rope_tables(pos_ref, invf_ref)
        qm_ref[:, :MLA_W] = q_ref[:, :MLA_W].astype(BF16)
        for t in range(ntile):
            sl = slice(MLA_W + t * LANES, MLA_W + (t + 1) * LANES)
            qm_ref[:, sl] = _rope_apply(q_ref[:, sl], cos, sin, first).astype(BF16)
        krt_ref[...] = _rope_apply(kr_ref[...], cos, sin, first).astype(BF16)

    return pl.pallas_call(
        body, name=name, grid=(T // tm,),
        in_specs=[pl.BlockSpec((tm, 2 * MLA_W), lambda i: (i, 0)), pl.BlockSpec((tm, LANES), lambda i: (i, P_KRT // LANES)),
                  pl.BlockSpec((tm, 1), lambda i: (i, 0)), pl.BlockSpec((1, LANES), lambda i: (0, 0))],
        out_specs=[pl.BlockSpec((tm, 2 * MLA_W), lambda i: (i, 0)), pl.BlockSpec((tm, LANES), lambda i: (i, 0))],
        out_shape=[jax.ShapeDtypeStruct((T, 2 * MLA_W), BF16), jax.ShapeDtypeStruct((T, LANES), BF16)],
        compiler_params=_cparams(("parallel",)),
    )(qfull, p, pos, invf)


def _rope_bwd(dqr, dkr, pos, invf, *, tm, name):
    T = dqr.shape[0]
    ntile = MLA_W // LANES

    def body(dq_ref, dk_ref, pos_ref, invf_ref, oq_ref, ok_ref):
        cos, sin, first = _rope_tables(pos_ref, invf_ref)
        for t in range(ntile):
            sl = slice(t * LANES, (t + 1) * LANES)
            oq_ref[:, sl] = _rope_apply_t(dq_ref[:, sl], cos, sin, first).astype(BF16)
        ok_ref[...] = _rope_apply_t(dk_ref[...], cos, sin, first).astype(BF16)

    return pl.pallas_call(
        body, name=name, grid=(T // tm,),
        in_specs=[pl.BlockSpec((tm, MLA_W), lambda i: (i, 0)), pl.BlockSpec((tm, LANES), lambda i: (i, 0)),
                  pl.BlockSpec((tm, 1), lambda i: (i, 0)), pl.BlockSpec((1, LANES), lambda i: (0, 0))],
        out_specs=[pl.BlockSpec((tm, MLA_W), lambda i: (i, 0)), pl.BlockSpec((tm, LANES), lambda i: (i, 0))],
        out_shape=[jax.ShapeDtypeStruct((T, MLA_W), BF16), jax.ShapeDtypeStruct((T, LANES), BF16)],
        compiler_params=_cparams(("parallel",)),
    )(dqr, dkr, pos, invf)


CONV_ROWS = 256
HALO = 8


def _conv_taps(w_ref):
    return w_ref[0:1, :], w_ref[1:2, :], w_ref[2:3, :]


def _conv_rows(cur, prev, w, bias):
    ext = jnp.concatenate([prev, cur], axis=0)
    u1 = pltpu.roll(ext, 1, 0)[HALO:]
    u2 = pltpu.roll(ext, 2, 0)[HALO:]
    return w[2] * cur + w[1] * u1 + w[0] * u2 + bias, u1, u2


def _conv_fwd(u, w, bias, *, seq, name):
    T = u.shape[0]
    B = T // seq
    W2 = 2 * FF_BLK

    def body(u_ref, w_ref, b_ref, a_ref):
        wv = _conv_taps(w_ref)
        bv = b_ref[...]
        for c in range(seq // CONV_ROWS):
            r0 = c * CONV_ROWS
            cur = u_ref[r0:r0 + CONV_ROWS, :]
            prev = u_ref[r0 - HALO:r0, :] if c > 0 else jnp.zeros((HALO, W2), F32)
            y, _, _ = _conv_rows(cur, prev, wv, bv)
            gc = y[:, :FF_BLK]
            a_ref[r0:r0 + CONV_ROWS, :] = (gc * (1.0 / (1.0 + jnp.exp(-gc))) * y[:, FF_BLK:]).astype(BF16)

    return pl.pallas_call(
        body, name=name, grid=(B, N_FF_BLK),
        in_specs=[pl.BlockSpec((seq, W2), lambda b, j: (b, j)), pl.BlockSpec((3, W2), lambda b, j: (0, j)),
                  pl.BlockSpec((1, W2), lambda b, j: (0, j))],
        out_specs=pl.BlockSpec((seq, FF_BLK), lambda b, j: (b, j)),
        out_shape=jax.ShapeDtypeStruct((T, D_FF), BF16),
        compiler_params=_cparams(("parallel", "parallel")),
    )(u, w, bias)


def _conv_bwd(u, da, w, bias, *, seq, name):
    T = u.shape[0]
    B = T // seq
    W2 = 2 * FF_BLK
    nchunk = seq // CONV_ROWS

    def body(u_ref, da_ref, w_ref, b_ref, du_ref, dw_ref, db_ref, duc_s):
        wv = _conv_taps(w_ref)
        bv = b_ref[...]
        zrow = jnp.zeros((1, W2), F32)
        dw0, dw1, dw2, dbs = zrow, zrow, zrow, zrow
        for c in range(nchunk):
            r0 = c * CONV_ROWS
            cur = u_ref[r0:r0 + CONV_ROWS, :]
            prev = u_ref[r0 - HALO:r0, :] if c > 0 else jnp.zeros((HALO, W2), F32)
            y, u1, u2 = _conv_rows(cur, prev, wv, bv)
            gc = y[:, :FF_BLK]
            vc = y[:, FF_BLK:]
            sg = 1.0 / (1.0 + jnp.exp(-gc))
            dav = da_ref[r0:r0 + CONV_ROWS, :]
            duc = jnp.concatenate([dav * vc * (sg * (1.0 + gc * (1.0 - sg))), dav * (gc * sg)], axis=1)
            duc_s[r0:r0 + CONV_ROWS, :] = duc
            dw0 = dw0 + jnp.sum(duc * u2, axis=0, keepdims=True)
            dw1 = dw1 + jnp.sum(duc * u1, axis=0, keepdims=True)
            dw2 = dw2 + jnp.sum(duc * cur, axis=0, keepdims=True)
            dbs = dbs + jnp.sum(duc, axis=0, keepdims=True)
        duc_s[seq:seq + HALO, :] = jnp.zeros((HALO, W2), F32)
        n_ext = CONV_ROWS + HALO
        for c in range(nchunk):
            r0 = c * CONV_ROWS
            ext = duc_s[r0:r0 + n_ext, :]
            s1 = pltpu.roll(ext, n_ext - 1, 0)[:CONV_ROWS]
            s2 = pltpu.roll(ext, n_ext - 2, 0)[:CONV_ROWS]
            du_ref[r0:r0 + CONV_ROWS, :] = (wv[2] * ext[:CONV_ROWS] + wv[1] * s1 + wv[0] * s2).astype(BF16)

        first = pl.program_id(1) == 0

        @pl.when(first)
        def _():
            dw_ref[0:1, :] = dw0
            dw_ref[1:2, :] = dw1
            dw_ref[2:3, :] = dw2
            db_ref[...] = dbs

        @pl.when(jnp.logical_not(first))
        def _():
            dw_ref[0:1, :] += dw0
            dw_ref[1:2, :] += dw1
            dw_ref[2:3, :] += dw2
            db_ref[...] += dbs

    return pl.pallas_call(
        body, name=name, grid=(N_FF_BLK, B),
        in_specs=[pl.BlockSpec((seq, W2), lambda j, b: (b, j)), pl.BlockSpec((seq, FF_BLK), lambda j, b: (b, j)),
                  pl.BlockSpec((3, W2), lambda j, b: (0, j)), pl.BlockSpec((1, W2), lambda j, b: (0, j))],
        out_specs=[pl.BlockSpec((seq, W2), lambda j, b: (b, j)), pl.BlockSpec((3, W2), lambda j, b: (0, j)),
                   pl.BlockSpec((1, W2), lambda j, b: (0, j))],
        out_shape=[jax.ShapeDtypeStruct((T, 2 * D_FF), BF16), jax.ShapeDtypeStruct((3, 2 * D_FF), F32),
                   jax.ShapeDtypeStruct((1, 2 * D_FF), F32)],
        scratch_shapes=[pltpu.VMEM((seq + HALO, W2), F32)],
        compiler_params=_cparams(("parallel", "arbitrary")),
    )(u, da, w, bias)


def _place():
    return lax.axis_index("x"), lax.axis_index("y"), lax.axis_index("c")


def _other_chips(x, y):
    return [(1 - x, y), (x, 1 - y), (1 - x, 1 - y)]


def _all_gather(vs, *, name):
    n = len(vs)

    def body(*refs):
        v_refs, out_refs = refs[:n], refs[n:2 * n]
        send_sems, recv_sems, local_sems = refs[2 * n:]
        x, y, c = _place()
        me, sibling = (x, y, c), (x, y, 1 - c)
        chips = _other_chips(x, y)

        def slab(a, px, py, pc):
            return out_refs[a].at[4 * px + 2 * py + pc]

        def copy(a, k, block, to, src=None):
            return pltpu.make_async_remote_copy(
                src_ref=slab(a, *block) if src is None else src, dst_ref=slab(a, *block),
                send_sem=send_sems.at[7 * a + k], recv_sem=recv_sems.at[7 * a + k], device_id=to, device_id_type=MESH)

        mine = [pltpu.make_async_copy(v_refs[a], slab(a, *me), local_sems.at[a]) for a in range(n)]
        for cp in mine:
            cp.start()
        first = []
        for a in range(n):
            first.append(copy(a, 0, me, sibling, src=v_refs[a]))
            first += [copy(a, 1 + j, me, (*chip, c), src=v_refs[a]) for j, chip in enumerate(chips)]
        for cp in first:
            cp.start()
        passed = []
        for j, chip in enumerate(chips):
            for a in range(n):
                copy(a, 1 + j, (*chip, c), me).wait_recv()
                cp = copy(a, 4 + j, (*chip, c), sibling)
                cp.start()
                passed.append(cp)
        for a in range(n):
            copy(a, 0, sibling, me).wait_recv()
            for j, chip in enumerate(chips):
                copy(a, 4 + j, (*chip, 1 - c), me).wait_recv()
        for cp in first + passed:
            cp.wait_send()
        for cp in mine:
            cp.wait()

    return pl.pallas_call(
        body, name=name, in_specs=[ANY] * n, out_specs=[ANY] * n,
        out_shape=[jax.ShapeDtypeStruct((N_DEV,) + v.shape, v.dtype) for v in vs],
        scratch_shapes=[pltpu.SemaphoreType.DMA((7 * n,)), pltpu.SemaphoreType.DMA((7 * n,)), pltpu.SemaphoreType.DMA((n,))],
    )(*vs)


def _all_gather_async(vs, *, name, collective_id):
    n = len(vs)
    v_refs = [jax.new_ref(v, memory_space=pltpu.MemorySpace.HBM) for v in vs]
    out_refs = [jax.empty_ref(jax.ShapeDtypeStruct((N_DEV,) + v.shape, v.dtype), memory_space=pltpu.MemorySpace.HBM)
                for v in vs]

    @pl.kernel(mesh=plsc.ScalarSubcoreMesh(axis_name="seq", num_cores=1), name=name,
               scratch_types=(pltpu.SemaphoreType.DMA((7 * n,)), pltpu.SemaphoreType.DMA((7 * n,)),
                              pltpu.SemaphoreType.DMA((n,))),
               compiler_params=pltpu.CompilerParams(collective_id=collective_id))
    def launch(send_sems, recv_sems, local_sems):
        x, y, c = _place()
        me, sibling = (x, y, c), (x, y, 1 - c)
        chips = _other_chips(x, y)
        peers = [sibling] + [(*chip, c) for chip in chips]
        barrier = pltpu.get_barrier_semaphore()
        for peer in peers:
            pl.semaphore_signal(barrier, inc=1, device_id=peer, device_id_type=MESH)
        pl.semaphore_wait(barrier, len(peers))

        def slab(a, px, py, pc):
            return out_refs[a].at[4 * px + 2 * py + pc]

        def copy(a, k, block, to, src=None):
            return pltpu.make_async_remote_copy(
                src_ref=slab(a, *block) if src is None else src, dst_ref=slab(a, *block),
                send_sem=send_sems.at[7 * a + k], recv_sem=recv_sems.at[7 * a + k], device_id=to, device_id_type=MESH)

        mine = [pltpu.make_async_copy(v_refs[a], slab(a, *me), local_sems.at[a]) for a in range(n)]
        for cp in mine:
            cp.start()
        first = []
        for a in range(n):
            first.append(copy(a, 0, me, sibling, src=v_refs[a]))
            first += [copy(a, 1 + j, me, (*chip, c), src=v_refs[a]) for j, chip in enumerate(chips)]
        for cp in first:
            cp.start()
        passed = []
        for j, chip in enumerate(chips):
            for a in range(n):
                copy(a, 1 + j, (*chip, c), me).wait_recv()
                cp = copy(a, 4 + j, (*chip, c), sibling)
                cp.start()
                passed.append(cp)
        for a in range(n):
            copy(a, 0, sibling, me).wait_recv()
            for j, chip in enumerate(chips):
                copy(a, 4 + j, (*chip, 1 - c), me).wait_recv()
        for cp in first + passed:
            cp.wait_send()
        for cp in mine:
            cp.wait()

    launch()
    return [r[...] for r in out_refs]


def _rs_sibling(g8s, *, name):
    n = len(g8s)

    def body(*refs):
        g_refs, out_refs = refs[:n], refs[n:2 * n]
        send_sems, recv_sems = refs[2 * n:]
        x, y, c = _place()
        copies = [
            pltpu.make_async_remote_copy(
                src_ref=g_refs[a].at[2 * k + 1 - c], dst_ref=out_refs[a].at[k],
                send_sem=send_sems.at[4 * a + k], recv_sem=recv_sems.at[4 * a + k],
                device_id=(x, y, 1 - c), device_id_type=MESH)
            for a in range(n) for k in range(4)]
        for cp in copies:
            cp.start()
        for cp in copies:
            cp.wait()

    return pl.pallas_call(
        body, name=name, in_specs=[ANY] * n, out_specs=[ANY] * n,
        out_shape=[jax.ShapeDtypeStruct((4,) + g.shape[1:], g.dtype) for g in g8s],
        scratch_shapes=[pltpu.SemaphoreType.DMA((4 * n,)), pltpu.SemaphoreType.DMA((4 * n,))],
    )(*g8s)


def _handshake(peers):
    barrier = pltpu.get_barrier_semaphore()
    for peer in peers:
        pl.semaphore_signal(barrier, inc=1, device_id=peer, device_id_type=MESH)
    pl.semaphore_wait(barrier, len(peers))


def _hbm_refs(arrays, lead):
    src = [jax.new_ref(a, memory_space=pltpu.MemorySpace.HBM) for a in arrays]
    dst = [jax.empty_ref(jax.ShapeDtypeStruct((lead,) + a.shape[1:], a.dtype), memory_space=pltpu.MemorySpace.HBM)
           for a in arrays]
    return src, dst


def _rs_sibling_async(g8s, *, name, collective_id):
    n = len(g8s)
    g_refs, out_refs = _hbm_refs(g8s, 4)

    @pl.kernel(mesh=plsc.ScalarSubcoreMesh(axis_name="seq", num_cores=1), name=name,
               scratch_types=(pltpu.SemaphoreType.DMA((4 * n,)), pltpu.SemaphoreType.DMA((4 * n,))),
               compiler_params=pltpu.CompilerParams(collective_id=collective_id))
    def launch(send_sems, recv_sems):
        x, y, c = _place()
        _handshake([(x, y, 1 - c)])
        copies = [
            pltpu.make_async_remote_copy(
                src_ref=g_refs[a].at[2 * k + 1 - c], dst_ref=out_refs[a].at[k],
                send_sem=send_sems.at[4 * a + k], recv_sem=recv_sems.at[4 * a + k],
                device_id=(x, y, 1 - c), device_id_type=MESH)
            for a in range(n) for k in range(4)]
        for cp in copies:
            cp.start()
        for cp in copies:
            cp.wait()

    launch()
    return [r[...] for r in out_refs]


def _rs_chips_async(h4s, *, name, collective_id):
    n = len(h4s)
    h_refs, out_refs = _hbm_refs(h4s, 3)

    @pl.kernel(mesh=plsc.ScalarSubcoreMesh(axis_name="seq", num_cores=1), name=name,
               scratch_types=(pltpu.SemaphoreType.DMA((3 * n,)), pltpu.SemaphoreType.DMA((3 * n,))),
               compiler_params=pltpu.CompilerParams(collective_id=collective_id))
    def launch(send_sems, recv_sems):
        x, y, c = _place()
        chips = _other_chips(x, y)
        _handshake([(cx, cy, c) for cx, cy in chips])
        copies = [
            pltpu.make_async_remote_copy(
                src_ref=h_refs[a].at[2 * cx + cy], dst_ref=out_refs[a].at[j],
                send_sem=send_sems.at[3 * a + j], recv_sem=recv_sems.at[3 * a + j],
                device_id=(cx, cy, c), device_id_type=MESH)
            for a in range(n) for j, (cx, cy) in enumerate(chips)]
        for cp in copies:
            cp.start()
        for cp in copies:
            cp.wait()

    launch()
    return [r[...] for r in out_refs]


def _row_tile(rows):
    return rows if rows <= 512 else 256


def _rs_chip_sum(g8, from_sibling, place_idx, *, name):
    _, R, C = g8.shape
    tr = _row_tile(R)

    def body(pi_ref, a_ref, b_ref, f_ref, h_ref):
        s = a_ref[...] + b_ref[...]
        h_ref[...] = s.astype(BF16)

        @pl.when(pl.program_id(1) == pi_ref[1])
        def _():
            f_ref[...] = s

    blk = pl.BlockSpec((None, tr, C), lambda r, k, pi_ref: (k, r, 0))
    return pl.pallas_call(
        body, name=name,
        grid_spec=pltpu.PrefetchScalarGridSpec(
            num_scalar_prefetch=1, grid=(R // tr, 4),
            in_specs=[pl.BlockSpec((None, tr, C), lambda r, k, pi_ref: (2 * k + pi_ref[0], r, 0)), blk],
            out_specs=[pl.BlockSpec((tr, C), lambda r, k, pi_ref: (r, 0)), blk]),
        out_shape=[jax.ShapeDtypeStruct((R, C), F32), jax.ShapeDtypeStruct((4, R, C), BF16)],
        compiler_params=_cparams(("parallel", "arbitrary")),
    )(place_idx, g8, from_sibling)


def _rs_chips(h4s, *, name):
    n = len(h4s)

    def body(*refs):
        h_refs, out_refs = refs[:n], refs[n:2 * n]
        send_sems, recv_sems = refs[2 * n:]
        x, y, c = _place()
        copies = [
            pltpu.make_async_remote_copy(
                src_ref=h_refs[a].at[2 * cx + cy], dst_ref=out_refs[a].at[j],
                send_sem=send_sems.at[3 * a + j], recv_sem=recv_sems.at[3 * a + j],
                device_id=(cx, cy, c), device_id_type=MESH)
            for a in range(n) for j, (cx, cy) in enumerate(_other_chips(x, y))]
        for cp in copies:
            cp.start()
        for cp in copies:
            cp.wait()

    return pl.pallas_call(
        body, name=name, in_specs=[ANY] * n, out_specs=[ANY] * n,
        out_shape=[jax.ShapeDtypeStruct((3,) + h.shape[1:], h.dtype) for h in h4s],
        scratch_shapes=[pltpu.SemaphoreType.DMA((3 * n,)), pltpu.SemaphoreType.DMA((3 * n,))],
    )(*h4s)


def _split_moves(segments, chunk):
    moves = []
    for dst, src, length in segments:
        while length > 0:
            dev, off = divmod(src, chunk)
            take = min(length, chunk - off)
            moves.append((dst, dev, off, take))
            dst, src, length = dst + take, src + take, length - take
    return moves


def _assemble(stacked, segments, zero_spans, out_cols, *, name):
    _, R, c = stacked.shape
    tr = _row_tile(R)
    moves = _split_moves(segments, c)

    def body(x_ref, o_ref):
        for dst, dev, off, take in moves:
            o_ref[:, dst:dst + take] = x_ref[dev, :, off:off + take]
        for a, b in zero_spans:
            o_ref[:, a:b] = jnp.zeros((tr, b - a), o_ref.dtype)

    return pl.pallas_call(
        body, name=name, grid=(R // tr,),
        in_specs=[pl.BlockSpec((N_DEV, tr, c), lambda i: (0, i, 0))],
        out_specs=pl.BlockSpec((tr, out_cols), lambda i: (i, 0)),
        out_shape=jax.ShapeDtypeStruct((R, out_cols), stacked.dtype),
        compiler_params=_cparams(("parallel",)),
    )(stacked)


def _disassemble(full, segments, chunk, *, name):
    R = full.shape[0]
    tr = _row_tile(R)
    moves = _split_moves(segments, chunk)

    def body(x_ref, o_ref):
        seen = set()
        for dst, dev, off, take in moves:
            piece = x_ref[:, dst:dst + take]
            if (dev, off) in seen:
                piece = piece + o_ref[dev, :, off:off + take]
            seen.add((dev, off))
            o_ref[dev, :, off:off + take] = piece

    return pl.pallas_call(
        body, name=name, grid=(R // tr,),
        in_specs=[pl.BlockSpec((tr, full.shape[1]), lambda i: (i, 0))],
        out_specs=pl.BlockSpec((N_DEV, tr, chunk), lambda i: (0, i, 0)),
        out_shape=jax.ShapeDtypeStruct((N_DEV, R, chunk), F32),
        compiler_params=_cparams(("parallel",)),
    )(full)


_O_CQ = 3 * SB_W
_O_CKV = _O_CQ + Q_LORA
_O_KR = _O_CKV + KV_LORA
SEG_W_IN = ((0, 0, 3 * SB_W), (P_CKV, _O_CKV, KV_LORA), (P_KRT, _O_KR, MLA_ROPE), (P_KRT + MLA_ROPE, _O_KR, MLA_ROPE),
            (P_CQ, _O_CQ, Q_LORA))
ZERO_W_IN = ((P_KRT + 2 * MLA_ROPE, P_CQ),)
SEG_W_UQ = tuple((MLA_NOPE * h, MLA_QK * h, MLA_NOPE) for h in range(MLA_HEADS)) + tuple(
    (MLA_W + LANES * (h // 2) + MLA_ROPE * (h % 2), MLA_QK * h + MLA_NOPE, MLA_ROPE) for h in range(MLA_HEADS))
ZERO_W_UQ = tuple((MLA_W + LANES * g + 2 * MLA_ROPE, MLA_W + LANES * (g + 1)) for g in range(MLA_HEADS // 2))
SEG_W_UKV = tuple((MLA_NOPE * h, (MLA_NOPE + MLA_V) * h, MLA_NOPE) for h in range(MLA_HEADS)) + tuple(
    (MLA_W + MLA_V * h, (MLA_NOPE + MLA_V) * h + MLA_NOPE, MLA_V) for h in range(MLA_HEADS))
SEG_W_UP = tuple((2 * FF_BLK * blk + FF_BLK * half, D_FF * half + FF_BLK * blk, FF_BLK)
                 for half in range(2) for blk in range(N_FF_BLK))


def _sum8(g, *, name):
    _, R, C = g.shape

    def body(g_ref, o_ref):
        acc = g_ref[0]
        for k in range(1, N_DEV):
            acc = acc + g_ref[k]
        o_ref[...] = acc

    return pl.pallas_call(
        body, name=name, out_shape=jax.ShapeDtypeStruct((R, C), F32),
    )(g)


def _adamw_math(w, gf, m, v):
    c1 = 1.0 / (1.0 - ADAM_B1 ** ADAM_STEP)
    c2 = 1.0 / (1.0 - ADAM_B2 ** ADAM_STEP)
    mn = ADAM_B1 * m + (1.0 - ADAM_B1) * gf
    vn = ADAM_B2 * v + (1.0 - ADAM_B2) * (gf * gf)
    return -ADAM_LR * ((mn * c1) / (jnp.sqrt(vn * c2) + ADAM_EPS) + ADAM_WD * w), mn, vn


def _adamw(w, g, m, v, *, name):
    R, C = w.shape
    tr = _row_tile(R)

    def body(w_ref, g_ref, m_ref, v_ref, d_ref, mo_ref, vo_ref):
        d_ref[...], mo_ref[...], vo_ref[...] = _adamw_math(w_ref[...], g_ref[...], m_ref[...], v_ref[...])

    blk = pl.BlockSpec((tr, C), lambda i: (i, 0))
    shp = jax.ShapeDtypeStruct((R, C), F32)
    return pl.pallas_call(
        body, name=name, grid=(R // tr,), in_specs=[blk] * 4, out_specs=[blk] * 3,
        out_shape=[shp, shp, shp], compiler_params=_cparams(("parallel",)),
    )(w, g, m, v)


def _adamw_rs(own, r3, w, m, v, *, name):
    R, C = w.shape
    tr = _row_tile(R)

    def body(f_ref, r_ref, w_ref, m_ref, v_ref, g_ref, d_ref, mo_ref, vo_ref):
        gf = ((f_ref[...] + r_ref[0].astype(F32)) + r_ref[1].astype(F32)) + r_ref[2].astype(F32)
        g_ref[...] = gf
        d_ref[...], mo_ref[...], vo_ref[...] = _adamw_math(w_ref[...], gf, m_ref[...], v_ref[...])

    blk = pl.BlockSpec((tr, C), lambda i: (i, 0))
    shp = jax.ShapeDtypeStruct((R, C), F32)
    return pl.pallas_call(
        body, name=name, grid=(R // tr,),
        in_specs=[blk, pl.BlockSpec((3, tr, C), lambda i: (0, i, 0)), blk, blk, blk], out_specs=[blk] * 4,
        out_shape=[shp] * 4, compiler_params=_cparams(("parallel",)),
    )(own, r3, w, m, v)


def _ff_interleave(a):
    lead = a.shape[:-1]
    return a.reshape(*lead, 2, N_FF_BLK, FF_BLK).swapaxes(-3, -2).reshape(*lead, 2 * D_FF)


def _ff_deinterleave(a):
    lead = a.shape[:-1]
    return a.reshape(*lead, N_FF_BLK, 2, FF_BLK).swapaxes(-3, -2).reshape(*lead, 2 * D_FF)


SMALL =(("g_mix", D_MODEL), ("g_cq", Q_LORA), ("g_ckv", KV_LORA), ("g_sb_out", SB_W), ("g_mla_out", MLA_W),
         ("g_ffn", D_MODEL), ("conv_b", 2 * D_FF), ("g_final", D_MODEL))
SMALL_ROWS = 88


SMALL_USED = sum(size for _, size in SMALL)


def _pack_small(d, tail=None):
    parts = [d[n].reshape(-1) for n, _ in SMALL] + ([] if tail is None else [tail])
    flat = jnp.concatenate(parts)
    flat = jnp.pad(flat, (0, SMALL_ROWS * LANES - flat.shape[0]))
    return flat.reshape(SMALL_ROWS, LANES)


def _unpack_small(a):
    flat = a.reshape(-1)
    out, off = {}, 0
    for n, size in SMALL:
        out[n] = flat[off:off + size]
        off += size
    return out


def kernel(x, positions, g_mix, w_in, g_cq, w_uq, g_ckv, w_ukv, g_sb_out, g_mla_out, w_out, g_ffn, w_up, conv_w, conv_b, w_down, g_final, loss_target, m_g_mix, m_w_in, m_g_cq, m_w_uq, m_g_ckv, m_w_ukv, m_g_sb_out, m_g_mla_out, m_w_out, m_g_ffn, m_w_up, m_conv_w, m_conv_b, m_w_down, m_g_final, v_g_mix, v_w_in, v_g_cq, v_w_uq, v_g_ckv, v_w_ukv, v_g_sb_out, v_g_mla_out, v_w_out, v_g_ffn, v_w_up, v_conv_w, v_conv_b, v_w_down, v_g_final):
    B, S, D = x.shape
    T = B * S
    xf = x.reshape(T, D)
    tgt = loss_target.reshape(T, D)
    pos = positions.reshape(T, 1)
    half = MLA_ROPE // 2
    inv_freq = 1.0 / (ROPE_BASE ** (jnp.arange(half, dtype=F32) * (2.0 / MLA_ROPE)))
    invf = jnp.tile(inv_freq, LANES // half).reshape(1, LANES)
    place_idx = jnp.stack([lax.axis_index("c"), 2 * lax.axis_index("x") + lax.axis_index("y")]).astype(jnp.int32)

    names = ("w_in", "w_uq", "w_ukv", "w_out", "w_up", "w_down", "conv_w")
    shard = {"w_in": w_in[0], "w_uq": w_uq[0], "w_ukv": w_ukv[0], "w_out": w_out[0], "w_up": w_up[0],
             "w_down": w_down[0], "conv_w": conv_w[0]}
    sent = {n: shard[n] if n == "conv_w" else shard[n].astype(BF16) for n in names}
    later = names[1:]
    w_in_all = _all_gather([sent["w_in"]], name="ag_w_in")[0]
    w_in_all, rest = lax.optimization_barrier((w_in_all, [sent[n] for n in later]))
    got = {"w_in": w_in_all}
    got.update(zip(later, _all_gather_async(rest, name="ag_weights_async", collective_id=0)))
    wi = _assemble(got["w_in"], SEG_W_IN, ZERO_W_IN, P_COLS, name="asm_w_in")
    wuq = _assemble(got["w_uq"], SEG_W_UQ, ZERO_W_UQ, 2 * MLA_W, name="asm_w_uq")
    wukv = _assemble(got["w_ukv"], SEG_W_UKV, (), 2 * MLA_W, name="asm_w_ukv")
    wup = _assemble(got["w_up"], SEG_W_UP, (), 2 * D_FF, name="asm_w_up")
    cwi = _assemble(got["conv_w"], SEG_W_UP, (), 2 * D_FF, name="asm_conv_w")
    wo = got["w_out"].reshape(D, D)
    wdn = got["w_down"].reshape(D_FF, D)
    cbi = _ff_interleave(conv_b)

    h = _rms_fwd(xf, g_mix, tm=512, name="rms_mix")
    p = _matmul_nn(h, wi, tm=1024, tn=P_COLS // 2, out_dtype=F32, name="proj_in")
    o_sb, ltot = _sb_fwd(p, seq=S, name="sb_fwd")
    cq = _rms_fwd(p, g_cq, tm=512, name="rms_cq", col_block=P_CQ // Q_LORA)
    ckv = _rms_fwd(p, g_ckv, tm=512, name="rms_ckv", col_block=P_CKV // KV_LORA)
    qfull = _matmul_nn(cq, wuq, tm=512, tn=1024, out_dtype=F32, name="proj_uq")
    kvm = _matmul_nn(ckv, wukv, tm=512, tn=1024, out_dtype=BF16, name="proj_ukv")
    qm, krt = _rope_fwd(qfull, p, pos, invf, tm=512, name="rope_fwd")
    o_mla, lse = _mla_fwd(qm, kvm, krt, seq=S, name="mla_fwd")
    ocat = _rms2_fwd(o_sb, o_mla, g_sb_out, g_mla_out, tm=512, name="rms_heads")
    x1 = _matmul_nn(ocat, wo, tm=512, tn=1024, out_dtype=F32, name="proj_out", residual=xf)
    hf = _rms_fwd(x1, g_ffn, tm=512, name="rms_ffn")
    u = _matmul_nn(hf, wup, tm=1024, tn=512, out_dtype=F32, name="ffn_up")
    a = _conv_fwd(u, cwi, cbi, seq=S, name="conv_fwd")
    x2 = _matmul_nn(a, wdn, tm=512, tn=1024, out_dtype=F32, name="ffn_down", residual=x1)
    dx2, dg_final, loss_row = _final_loss(x2, g_final.reshape(1, D), tgt, tm=512, name="final_loss")

    da = _matmul_nt(dx2, wdn, tm=1024, tn=D_FF // 2, out_dtype=F32, name="d_ffn_down")
    dw_down = _matmul_tn(a, dx2, tm=D_FF // 2, tn=1024, tk=1024, name="dw_down")
    du, dcw, dcb = _conv_bwd(u, da, cwi, cbi, seq=S, name="conv_bwd")
    dhf = _matmul_nt(du, wup, tm=512, tn=512, out_dtype=F32, name="d_ffn_up")
    dw_up = _matmul_tn(hf, du, tm=1024, tn=D_FF, tk=1024, name="dw_up")
    dx1, dg_ffn = _rms_bwd(dhf, x1, g_ffn, tm=512, name="rms_ffn_bwd", residual=dx2)
    docat = _matmul_nt(dx1, wo, tm=512, tn=1024, out_dtype=F32, name="d_proj_out")
    dw_out = _matmul_tn(ocat, dx1, tm=1024, tn=1024, tk=1024, name="dw_out")
    do_sb, do_mla, dg_sb, dg_mla = _rms2_bwd(docat, o_sb, o_mla, g_sb_out, g_mla_out, tm=512, name="rms_heads_bwd")

    early = ("w_down", "w_up", "conv_w", "w_out")
    g8 = {"w_up": _disassemble(dw_up, SEG_W_UP, shard["w_up"].shape[1], name="split_dw_up"),
          "conv_w": _disassemble(dcw, SEG_W_UP, shard["conv_w"].shape[1], name="split_dconv_w"),
          "w_out": dw_out.reshape((N_DEV,) + shard["w_out"].shape),
          "w_down": dw_down.reshape((N_DEV,) + shard["w_down"].shape)}
    sib_e = _rs_sibling_async([g8[n] for n in early], name="rs_sibling_async", collective_id=1)

    dq_sb, dk_sb, dv_sb = _sb_bwd(p, ltot, do_sb, seq=S, name="sb_bwd")
    sib_e, dq_sb = lax.optimization_barrier((sib_e, dq_sb))
    own, r3 = {}, {}
    sums_e = [_rs_chip_sum(g8[n], fs, place_idx, name="rs_chip_sum_" + n) for n, fs in zip(early, sib_e)]
    r3.update(zip(early, _rs_chips_async([h4 for _, h4 in sums_e], name="rs_chips_async", collective_id=2)))
    own.update({n: f for n, (f, _) in zip(early, sums_e)})

    dqn, dqr, dkn, dvm, dkr = _mla_bwd(qm, kvm, krt, o_mla, lse, do_mla, seq=S, name="mla_bwd")
    dqr_u, dkr_u = _rope_bwd(dqr, dkr, pos, invf, tm=512, name="rope_bwd")
    dqm = jnp.concatenate([dqn, dqr_u], axis=1)
    dkvm = jnp.concatenate([dkn, dvm], axis=1)
    dcq_n = _matmul_nt(dqm, wuq, tm=512, tn=Q_LORA, out_dtype=F32, name="d_proj_uq")
    dw_uq = _matmul_tn(cq, dqm, tm=Q_LORA, tn=1024, tk=1024, name="dw_uq")
    dckv_n = _matmul_nt(dkvm, wukv, tm=512, tn=KV_LORA, out_dtype=F32, name="d_proj_ukv")
    dw_ukv = _matmul_tn(ckv, dkvm, tm=KV_LORA, tn=1024, tk=1024, name="dw_ukv")
    dcq, dg_cq = _rms_bwd(dcq_n, p, g_cq, tm=512, name="rms_cq_bwd", col_block=P_CQ // Q_LORA, out_dtype=BF16)
    dckv, dg_ckv = _rms_bwd(dckv_n, p, g_ckv, tm=512, name="rms_ckv_bwd", col_block=P_CKV // KV_LORA, out_dtype=BF16)
    dp = jnp.concatenate([dq_sb, dk_sb, dv_sb, dckv, dkr_u, dcq], axis=1)
    dw_in = _matmul_tn(h, dp, tm=1024, tn=P_COLS, tk=1024, name="dw_in")

    late = ("w_in", "w_uq", "w_ukv")
    g8.update({"w_in": _disassemble(dw_in, SEG_W_IN, shard["w_in"].shape[1], name="split_dw_in"),
               "w_uq": _disassemble(dw_uq, SEG_W_UQ, shard["w_uq"].shape[1], name="split_dw_uq"),
               "w_ukv": _disassemble(dw_ukv, SEG_W_UKV, shard["w_ukv"].shape[1], name="split_dw_ukv")})
    sib_l = _rs_sibling_async([g8[n] for n in late], name="rs_sibling_late", collective_id=3)
    dh = _matmul_nt(dp, wi, tm=512, tn=1024, out_dtype=F32, name="d_proj_in")
    dx, dg_mix = _rms_bwd(dh, xf, g_mix, tm=512, name="rms_mix_bwd", residual=dx1)
    sib_l, dx = lax.optimization_barrier((sib_l, dx))
    sums_l = [_rs_chip_sum(g8[n], fs, place_idx, name="rs_chip_sum_" + n) for n, fs in zip(late, sib_l)]
    r3.update(zip(late, _rs_chips_async([h4 for _, h4 in sums_l], name="rs_chips_late", collective_id=4)))
    own.update({n: f for n, (f, _) in zip(late, sums_l)})

    small_part = {"g_mix": dg_mix, "g_cq": dg_cq, "g_ckv": dg_ckv, "g_sb_out": dg_sb, "g_mla_out": dg_mla,
                  "g_ffn": dg_ffn, "conv_b": _ff_deinterleave(dcb), "g_final": dg_final}
    small_all, = _all_gather([_pack_small(small_part, tail=loss_row[0, 0:1])], name="ag_small_grads")
    gsmall = _sum8(small_all, name="sum_small_grads")

    params = {"w_in": (w_in, m_w_in, v_w_in), "w_uq": (w_uq, m_w_uq, v_w_uq), "w_ukv": (w_ukv, m_w_ukv, v_w_ukv),
              "w_out": (w_out, m_w_out, v_w_out), "w_up": (w_up, m_w_up, v_w_up), "conv_w": (conv_w, m_conv_w, v_conv_w),
              "w_down": (w_down, m_w_down, v_w_down)}
    grad, delta, new_m, new_v = {}, {}, {}, {}
    for n, (w_, m_, v_) in params.items():
        g_, d_, mn_, vn_ = _adamw_rs(own[n], r3[n], w_[0], m_[0], v_[0], name="adamw_" + n)
        grad[n], delta[n], new_m[n], new_v[n] = g_[None], d_[None], mn_[None], vn_[None]
    small_w = {"g_mix": g_mix, "g_cq": g_cq, "g_ckv": g_ckv, "g_sb_out": g_sb_out, "g_mla_out": g_mla_out,
               "g_ffn": g_ffn, "conv_b": conv_b, "g_final": g_final}
    small_m = {"g_mix": m_g_mix, "g_cq": m_g_cq, "g_ckv": m_g_ckv, "g_sb_out": m_g_sb_out, "g_mla_out": m_g_mla_out,
               "g_ffn": m_g_ffn, "conv_b": m_conv_b, "g_final": m_g_final}
    small_v = {"g_mix": v_g_mix, "g_cq": v_g_cq, "g_ckv": v_g_ckv, "g_sb_out": v_g_sb_out, "g_mla_out": v_g_mla_out,
               "g_ffn": v_g_ffn, "conv_b": v_conv_b, "g_final": v_g_final}
    ds_, ms_, vs_ = _adamw(_pack_small(small_w), gsmall, _pack_small(small_m), _pack_small(small_v), name="adamw_small")
    for src, dst in ((_unpack_small(gsmall), grad), (_unpack_small(ds_), delta), (_unpack_small(ms_), new_m), (_unpack_small(vs_), new_v)):
        for n, _ in SMALL:
            dst[n] = src[n].reshape(small_w[n].shape)

    loss = gsmall.reshape(-1)[SMALL_USED]
    order = ("g_mix", "w_in", "g_cq", "w_uq", "g_ckv", "w_ukv", "g_sb_out", "g_mla_out", "w_out", "g_ffn", "w_up",
             "conv_w", "conv_b", "w_down", "g_final")
    return (loss, dx.reshape(B, S, D), *[grad[n] for n in order], *[delta[n] for n in order],
            *[new_m[n] for n in order], *[new_v[n] for n in order])
```

```python
import jax
import jax.numpy as jnp
from jax import lax
from jax.experimental import pallas as pl
from jax.experimental.pallas import tpu as pltpu
from jax.experimental.pallas import tpu_sc as plsc

F32 = jnp.float32
BF16 = jnp.bfloat16

D_MODEL = 1024
SB_HEADS = 8
SB_HEAD_DIM = 64
MLA_HEADS = 8
MLA_NOPE = 64
MLA_ROPE = 32
MLA_V = 64
Q_LORA = 384
KV_LORA = 256
D_FF = 2816
ROPE_BASE = 10000.0
EPS = 1e-6
SB_W = SB_HEADS * SB_HEAD_DIM
MLA_W = MLA_HEADS * MLA_V
MLA_QK = MLA_NOPE + MLA_ROPE
IN_COLS = 3 * SB_W + Q_LORA + KV_LORA + MLA_ROPE

ADAM_LR = 0.001
ADAM_B1 = 0.9
ADAM_B2 = 0.999
ADAM_EPS = 1e-08
ADAM_WD = 0.01
ADAM_STEP = 10

N_DEV = 8
MESH_AXES = ("x", "y", "c")
LANES = 128
V7X_VMEM_LIMIT = 56 * 1024 * 1024
FF_BLK = 256
N_FF_BLK = D_FF // FF_BLK

P_Q, P_K, P_V = 0, SB_W, 2 * SB_W
P_CKV = 3 * SB_W
P_KRT = P_CKV + KV_LORA
P_CQ = P_KRT + LANES
P_COLS = P_CQ + Q_LORA

MESH = pl.DeviceIdType.MESH
ANY = pl.BlockSpec(memory_space=pl.ANY)


def _cparams(sem=None, vmem=V7X_VMEM_LIMIT):
    return pltpu.CompilerParams(dimension_semantics=sem, vmem_limit_bytes=vmem)


def _matmul_nn(a, b, *, tm, tn, out_dtype, name, residual=None):
    M, K = a.shape
    N = b.shape[1]
    assert M % tm == 0 and N % tn == 0, (name, a.shape, b.shape)
    in_specs = [pl.BlockSpec((tm, K), lambda i, j: (i, 0)), pl.BlockSpec((K, tn), lambda i, j: (0, j))]
    args = [a, b]
    if residual is not None:
        in_specs.append(pl.BlockSpec((tm, tn), lambda i, j: (i, j)))
        args.append(residual)

    def body(*refs):
        a_ref, b_ref = refs[0], refs[1]
        o_ref = refs[-1]
        acc = jnp.dot(a_ref[...].astype(BF16), b_ref[...], preferred_element_type=F32)
        if residual is not None:
            acc = acc + refs[2][...]
        o_ref[...] = acc.astype(out_dtype)

    return pl.pallas_call(
        body, name=name, grid=(M // tm, N // tn), in_specs=in_specs,
        out_specs=pl.BlockSpec((tm, tn), lambda i, j: (i, j)),
        out_shape=jax.ShapeDtypeStruct((M, N), out_dtype),
        compiler_params=_cparams(("parallel", "parallel")),
    )(*args)


def _matmul_nt(a, b, *, tm, tn, out_dtype, name):
    M, K = a.shape
    N = b.shape[0]
    assert M % tm == 0 and N % tn == 0, (name, a.shape, b.shape)

    def body(a_ref, b_ref, o_ref):
        acc = lax.dot_general(a_ref[...].astype(BF16), b_ref[...], (((1,), (1,)), ((), ())),
                              preferred_element_type=F32)
        o_ref[...] = acc.astype(out_dtype)

    return pl.pallas_call(
        body, name=name, grid=(M // tm, N // tn),
        in_specs=[pl.BlockSpec((tm, K), lambda i, j: (i, 0)), pl.BlockSpec((tn, K), lambda i, j: (j, 0))],
        out_specs=pl.BlockSpec((tm, tn), lambda i, j: (i, j)),
        out_shape=jax.ShapeDtypeStruct((M, N), out_dtype),
        compiler_params=_cparams(("parallel", "parallel")),
    )(a, b)


def _matmul_tn(a, b, *, tm, tn, tk, name):
    K, M = a.shape
    N = b.shape[1]
    assert M % tm == 0 and N % tn == 0 and K % tk == 0, (name, a.shape, b.shape)

    def body(a_ref, b_ref, o_ref):
        k = pl.program_id(2)
        part = lax.dot_general(a_ref[...].astype(BF16), b_ref[...].astype(BF16), (((0,), (0,)), ((), ())),
                               preferred_element_type=F32)

        @pl.when(k == 0)
        def _():
            o_ref[...] = part

        @pl.when(k > 0)
        def _():
            o_ref[...] += part

    return pl.pallas_call(
        body, name=name, grid=(M // tm, N // tn, K // tk),
        in_specs=[pl.BlockSpec((tk, tm), lambda i, j, k: (k, i)), pl.BlockSpec((tk, tn), lambda i, j, k: (k, j))],
        out_specs=pl.BlockSpec((tm, tn), lambda i, j, k: (i, j)),
        out_shape=jax.ShapeDtypeStruct((M, N), F32),
        compiler_params=_cparams(("parallel", "parallel", "arbitrary")),
    )(a, b)


def _rms(xf, g):
    r = lax.rsqrt(jnp.mean(xf * xf, axis=1, keepdims=True) + EPS)
    return (xf * r) * g


def _rms_grad(dyf, xf, g):
    r = lax.rsqrt(jnp.mean(xf * xf, axis=1, keepdims=True) + EPS)
    xh = xf * r
    dyg = dyf * g
    dx = r * (dyg - xh * jnp.mean(dyg * xh, axis=1, keepdims=True))
    return dx, jnp.sum(dyf * xh, axis=0, keepdims=True)


def _accumulate(ref, part):
    @pl.when(pl.program_id(0) == 0)
    def _():
        ref[...] = part

    @pl.when(pl.program_id(0) > 0)
    def _():
        ref[...] += part


def _rms_fwd(x, g, *, tm, name, col_block=0):
    T = x.shape[0]
    C = g.shape[1]

    def body(x_ref, g_ref, o_ref):
        o_ref[...] = _rms(x_ref[...], g_ref[...]).astype(BF16)

    return pl.pallas_call(
        body, name=name, grid=(T // tm,),
        in_specs=[pl.BlockSpec((tm, C), lambda i: (i, col_block)), pl.BlockSpec((1, C), lambda i: (0, 0))],
        out_specs=pl.BlockSpec((tm, C), lambda i: (i, 0)),
        out_shape=jax.ShapeDtypeStruct((T, C), BF16),
        compiler_params=_cparams(("parallel",)),
    )(x, g)


def _rms_bwd(dy, x, g, *, tm, name, residual=None, col_block=0, out_dtype=F32):
    T = dy.shape[0]
    C = g.shape[1]
    in_specs = [pl.BlockSpec((tm, C), lambda i: (i, 0)), pl.BlockSpec((tm, C), lambda i: (i, col_block)),
                pl.BlockSpec((1, C), lambda i: (0, 0))]
    args = [dy, x, g]
    if residual is not None:
        in_specs.append(pl.BlockSpec((tm, C), lambda i: (i, 0)))
        args.append(residual)

    def body(*refs):
        dy_ref, x_ref, g_ref = refs[:3]
        dx_ref, dg_ref = refs[-2:]
        dx, part = _rms_grad(dy_ref[...].astype(F32), x_ref[...], g_ref[...])
        if residual is not None:
            dx = dx + refs[3][...]
        dx_ref[...] = dx.astype(out_dtype)
        _accumulate(dg_ref, part)

    return pl.pallas_call(
        body, name=name, grid=(T // tm,), in_specs=in_specs,
        out_specs=[pl.BlockSpec((tm, C), lambda i: (i, 0)), pl.BlockSpec((1, C), lambda i: (0, 0))],
        out_shape=[jax.ShapeDtypeStruct((T, C), out_dtype), jax.ShapeDtypeStruct((1, C), F32)],
        compiler_params=_cparams(("arbitrary",)),
    )(*args)


def _rms_matmul_nn(x, g, w, *, tm, name, col_block=0, out_dtype=F32):
    T = x.shape[0]
    C, N = w.shape
    assert T % tm == 0, (name, x.shape)

    def body(x_ref, g_ref, w_ref, h_ref, o_ref):
        hb = _rms(x_ref[...], g_ref[...]).astype(BF16)
        h_ref[...] = hb
        o_ref[...] = jnp.dot(hb, w_ref[...], preferred_element_type=F32).astype(out_dtype)

    return pl.pallas_call(
        body, name=name, grid=(T // tm,),
        in_specs=[pl.BlockSpec((tm, C), lambda i: (i, col_block)), pl.BlockSpec((1, C), lambda i: (0, 0)),
                  pl.BlockSpec((C, N), lambda i: (0, 0))],
        out_specs=[pl.BlockSpec((tm, C), lambda i: (i, 0)), pl.BlockSpec((tm, N), lambda i: (i, 0))],
        out_shape=[jax.ShapeDtypeStruct((T, C), BF16), jax.ShapeDtypeStruct((T, N), out_dtype)],
        compiler_params=_cparams(("parallel",)),
    )(x, g, w)


def _matmul_nt_rms_bwd(a, b, x, g, *, tm, name, residual=None, col_block=0, out_dtype=F32):
    M, K = a.shape
    C = b.shape[0]
    assert M % tm == 0, (name, a.shape)
    in_specs = [pl.BlockSpec((tm, K), lambda i: (i, 0)), pl.BlockSpec((C, K), lambda i: (0, 0)),
                pl.BlockSpec((tm, C), lambda i: (i, col_block)), pl.BlockSpec((1, C), lambda i: (0, 0))]
    args = [a, b, x, g]
    if residual is not None:
        in_specs.append(pl.BlockSpec((tm, C), lambda i: (i, 0)))
        args.append(residual)

    def body(*refs):
        a_ref, b_ref, x_ref, g_ref = refs[:4]
        dx_ref, dg_ref = refs[-2:]
        dy = lax.dot_general(a_ref[...].astype(BF16), b_ref[...], (((1,), (1,)), ((), ())), preferred_element_type=F32)
        dx, part = _rms_grad(dy, x_ref[...], g_ref[...])
        if residual is not None:
            dx = dx + refs[4][...]
        dx_ref[...] = dx.astype(out_dtype)
        _accumulate(dg_ref, part)

    return pl.pallas_call(
        body, name=name, grid=(M // tm,), in_specs=in_specs,
        out_specs=[pl.BlockSpec((tm, C), lambda i: (i, 0)), pl.BlockSpec((1, C), lambda i: (0, 0))],
        out_shape=[jax.ShapeDtypeStruct((M, C), out_dtype), jax.ShapeDtypeStruct((1, C), F32)],
        compiler_params=_cparams(("arbitrary",)),
    )(*args)


def _matmul_nn_loss(a, w, x1, g, tgt, *, tm, name):
    M, K = a.shape
    C = w.shape[1]
    assert M % tm == 0, (name, a.shape)

    def body(a_ref, w_ref, x_ref, g_ref, t_ref, dx_ref, dg_ref, loss_ref):
        xf = x_ref[...] + jnp.dot(a_ref[...], w_ref[...], preferred_element_type=F32)
        gf = g_ref[...]
        err = _rms(xf, gf) - t_ref[...]
        lpart = 0.5 * jnp.sum(jnp.mean(err * err, axis=1, keepdims=True), axis=0, keepdims=True)
        dx, gpart = _rms_grad(err * (1.0 / C), xf, gf)
        dx_ref[...] = dx
        _accumulate(dg_ref, gpart)
        _accumulate(loss_ref, jnp.broadcast_to(lpart, (1, LANES)))

    row = pl.BlockSpec((tm, C), lambda i: (i, 0))
    return pl.pallas_call(
        body, name=name, grid=(M // tm,),
        in_specs=[pl.BlockSpec((tm, K), lambda i: (i, 0)), pl.BlockSpec((K, C), lambda i: (0, 0)), row,
                  pl.BlockSpec((1, C), lambda i: (0, 0)), row],
        out_specs=[row, pl.BlockSpec((1, C), lambda i: (0, 0)), pl.BlockSpec((1, LANES), lambda i: (0, 0))],
        out_shape=[jax.ShapeDtypeStruct((M, C), F32), jax.ShapeDtypeStruct((1, C), F32),
                   jax.ShapeDtypeStruct((1, LANES), F32)],
        compiler_params=_cparams(("arbitrary",)),
    )(a, w, x1, g, tgt)


def _rms2_fwd(xa, xb, ga, gb, *, tm, name):
    T, C = xa.shape

    def body(xa_ref, xb_ref, ga_ref, gb_ref, o_ref):
        o_ref[:, :C] = _rms(xa_ref[...], ga_ref[...]).astype(BF16)
        o_ref[:, C:] = _rms(xb_ref[...], gb_ref[...]).astype(BF16)

    row = pl.BlockSpec((tm, C), lambda i: (i, 0))
    gsp = pl.BlockSpec((1, C), lambda i: (0, 0))
    return pl.pallas_call(
        body, name=name, grid=(T // tm,), in_specs=[row, row, gsp, gsp],
        out_specs=pl.BlockSpec((tm, 2 * C), lambda i: (i, 0)),
        out_shape=jax.ShapeDtypeStruct((T, 2 * C), BF16),
        compiler_params=_cparams(("parallel",)),
    )(xa, xb, ga, gb)


def _rms2_bwd(dy, xa, xb, ga, gb, *, tm, name):
    T, C = xa.shape

    def body(dy_ref, xa_ref, xb_ref, ga_ref, gb_ref, dxa_ref, dxb_ref, dga_ref, dgb_ref):
        dxa, pa = _rms_grad(dy_ref[:, :C], xa_ref[...], ga_ref[...])
        dxb, pb = _rms_grad(dy_ref[:, C:], xb_ref[...], gb_ref[...])
        dxa_ref[...] = dxa
        dxb_ref[...] = dxb
        _accumulate(dga_ref, pa)
        _accumulate(dgb_ref, pb)

    row = pl.BlockSpec((tm, C), lambda i: (i, 0))
    gsp = pl.BlockSpec((1, C), lambda i: (0, 0))
    return pl.pallas_call(
        body, name=name, grid=(T // tm,),
        in_specs=[pl.BlockSpec((tm, 2 * C), lambda i: (i, 0)), row, row, gsp, gsp],
        out_specs=[row, row, gsp, gsp],
        out_shape=[jax.ShapeDtypeStruct((T, C), F32), jax.ShapeDtypeStruct((T, C), F32),
                   jax.ShapeDtypeStruct((1, C), F32), jax.ShapeDtypeStruct((1, C), F32)],
        compiler_params=_cparams(("arbitrary",)),
    )(dy, xa, xb, ga, gb)


def _final_loss(x2, g, tgt, *, tm, name):
    T, C = x2.shape

    def body(x_ref, g_ref, t_ref, dx_ref, dg_ref, loss_ref):
        xf = x_ref[...]
        gf = g_ref[...]
        err = _rms(xf, gf) - t_ref[...]
        lpart = 0.5 * jnp.sum(jnp.mean(err * err, axis=1, keepdims=True), axis=0, keepdims=True)
        dx, gpart = _rms_grad(err * (1.0 / C), xf, gf)
        dx_ref[...] = dx
        _accumulate(dg_ref, gpart)
        _accumulate(loss_ref, jnp.broadcast_to(lpart, (1, LANES)))

    return pl.pallas_call(
        body, name=name, grid=(T // tm,),
        in_specs=[pl.BlockSpec((tm, C), lambda i: (i, 0)), pl.BlockSpec((1, C), lambda i: (0, 0)),
                  pl.BlockSpec((tm, C), lambda i: (i, 0))],
        out_specs=[pl.BlockSpec((tm, C), lambda i: (i, 0)), pl.BlockSpec((1, C), lambda i: (0, 0)),
                   pl.BlockSpec((1, LANES), lambda i: (0, 0))],
        out_shape=[jax.ShapeDtypeStruct((T, C), F32), jax.ShapeDtypeStruct((1, C), F32),
                   jax.ShapeDtypeStruct((1, LANES), F32)],
        compiler_params=_cparams(("arbitrary",)),
    )(x2, g, tgt)


ATT_T = 256
ATT_PAIRS = 2
NEG_BIG = -1e30


def _lane_iota():
    return lax.broadcasted_iota(jnp.int32, (1, LANES), 1)


def _head_masks():
    first = _lane_iota() < SB_HEAD_DIM
    return first, jnp.logical_not(first)


def _pick(mask, x):
    return jnp.where(mask, x, jnp.zeros_like(x))


def _lane_value(t, lane):
    return jnp.sum(jnp.where(_lane_iota() == lane, t, 0.0), axis=1, keepdims=True)


def _split_hi_lo(x):
    hi = x.astype(BF16)
    lo = (x - hi.astype(F32)).astype(BF16)
    return jnp.concatenate([hi, lo], axis=1)


def _tri(n, kind):
    r = lax.broadcasted_iota(jnp.int32, (n, n), 0)
    c = lax.broadcasted_iota(jnp.int32, (n, n), 1)
    u = {"suffix_excl": r > c, "prefix_incl": r <= c, "prefix_excl": r < c}[kind].astype(BF16)
    return jnp.concatenate([u, u], axis=0)


def _dot_nt(a, b):
    return lax.dot_general(a, b, (((1,), (1,)), ((), ())), preferred_element_type=F32)


def _dot_tn(a, b):
    return lax.dot_general(a, b, (((0,), (0,)), ((), ())), preferred_element_type=F32)


def _dot(a, b):
    return jnp.dot(a, b, preferred_element_type=F32)


def _causal_mask(n, strict):
    r = lax.broadcasted_iota(jnp.int32, (n, n), 0)
    c = lax.broadcasted_iota(jnp.int32, (n, n), 1)
    return (c < r) if strict else (c <= r)


LOG2E = 1.4426950408889634


def _sb_logs(qh, kj, vis):
    z2 = _dot_nt(qh, kj) * LOG2E
    nk = jnp.maximum(z2, 0.0) + jnp.log2(1.0 + jnp.exp2(-jnp.abs(z2)))
    lb = z2 - nk
    if vis is not None:
        nk = jnp.where(vis, nk, 0.0)
    return lb, nk


def _sb_fwd(p, *, seq, name):
    T = p.shape[0]
    B = T // seq
    TQ = ATT_T
    nq = seq // TQ
    PP = ATT_PAIRS
    W = PP * LANES
    nstep = SB_W // W
    NH = 2 * PP

    def body(q_ref, k_ref, v_ref, o_ref, lt_ref, q_s, k_s, v_s):
        masks = _head_masks()
        q = q_ref[...] * (SB_HEAD_DIM ** -0.5)
        v = v_ref[...]
        k_s[...] = k_ref[...].astype(BF16)
        for h in range(NH):
            ps = slice((h // 2) * LANES, (h // 2 + 1) * LANES)
            hs = slice(h * LANES, (h + 1) * LANES)
            q_s[:, hs] = _pick(masks[h % 2], q[:, ps]).astype(BF16)
            v_s[:, hs] = _pick(masks[h % 2], v[:, ps]).astype(BF16)
        u_suf = _tri(TQ, "suffix_excl")
        vis = _causal_mask(TQ, True)

        def q_block(i, carry):
            q0 = pl.multiple_of(i * TQ, TQ)
            qs = [q_s[pl.ds(q0, TQ), h * LANES:(h + 1) * LANES] for h in range(NH)]

            def tile(k0, c, mask):
                rs, accs = list(c[:NH]), list(c[NH:])
                logs = [_sb_logs(qs[h], k_s[pl.ds(k0, TQ), (h // 2) * LANES:(h // 2 + 1) * LANES], mask) for h in range(NH)]
                sums = [_dot(_split_hi_lo(nk), u_suf) for _, nk in logs]
                for h in range(NH):
                    a = jnp.exp2(logs[h][0] - sums[h] - rs[h])
                    if mask is not None:
                        a = jnp.where(mask, a, 0.0)
                    accs[h // 2] = accs[h // 2] + _dot(a.astype(BF16), v_s[pl.ds(k0, TQ), h * LANES:(h + 1) * LANES])
                    rs[h] = rs[h] + jnp.sum(logs[h][1], axis=1, keepdims=True)
                return tuple(rs) + tuple(accs)

            zero = jnp.zeros((TQ, 1), F32)
            c = tile(q0, (zero,) * NH + (jnp.zeros((TQ, LANES), F32),) * PP, vis)

            def k_block(jj, c):
                return tile(pl.multiple_of((i - 1 - jj) * TQ, TQ), c, None)

            c = lax.fori_loop(0, i, k_block, c)
            for pr in range(PP):
                ps = slice(pr * LANES, (pr + 1) * LANES)
                o_ref[pl.ds(q0, TQ), ps] = c[NH + pr]
                lt_ref[pl.ds(q0, TQ), ps] = jnp.where(masks[0], c[2 * pr], c[2 * pr + 1])
            return carry

        lax.fori_loop(0, nq, q_block, 0)

    blk = lambda off: pl.BlockSpec((seq, W), lambda b, g: (b, off + g))
    out_blk = pl.BlockSpec((seq, W), lambda b, g: (b, g))
    return pl.pallas_call(
        body, name=name, grid=(B, nstep),
        in_specs=[blk(P_Q // W), blk(P_K // W), blk(P_V // W)],
        out_specs=[out_blk, out_blk],
        out_shape=[jax.ShapeDtypeStruct((T, SB_W), F32), jax.ShapeDtypeStruct((T, SB_W), F32)],
        scratch_shapes=[pltpu.VMEM((seq, NH * LANES), BF16), pltpu.VMEM((seq, W), BF16), pltpu.VMEM((seq, NH * LANES), BF16)],
        compiler_params=_cparams(("parallel", "parallel")),
    )(p, p, p)


def _sb_bwd(p, ltot, do, *, seq, name):
    T = p.shape[0]
    B = T // seq
    TQ = ATT_T
    nq = seq // TQ
    PP = ATT_PAIRS
    W = PP * LANES
    nstep = SB_W // W
    NH = 2 * PP
    scale = SB_HEAD_DIM ** -0.5

    def body(q_ref, k_ref, v_ref, lt_ref, do_ref, dq_ref, dk_ref, dv_ref, q_s, k_s, v_s, do_s, dk_s, dv_s):
        masks = _head_masks()
        q = q_ref[...] * scale
        dof = do_ref[...]
        k_s[...] = k_ref[...].astype(BF16)
        v_s[...] = v_ref[...].astype(BF16)
        for h in range(NH):
            ps = slice((h // 2) * LANES, (h // 2 + 1) * LANES)
            hs = slice(h * LANES, (h + 1) * LANES)
            q_s[:, hs] = _pick(masks[h % 2], q[:, ps]).astype(BF16)
            do_s[:, hs] = _pick(masks[h % 2], dof[:, ps]).astype(BF16)
        dk_s[...] = jnp.zeros_like(dk_s)
        dv_s[...] = jnp.zeros_like(dv_s)
        u_pin = _tri(TQ, "prefix_incl")
        u_pex = _tri(TQ, "prefix_excl")[:TQ]
        vis = _causal_mask(TQ, True)

        def q_block(i, carry):
            q0 = pl.multiple_of(i * TQ, TQ)
            qs = [q_s[pl.ds(q0, TQ), h * LANES:(h + 1) * LANES] for h in range(NH)]
            dos = [do_s[pl.ds(q0, TQ), h * LANES:(h + 1) * LANES] for h in range(NH)]
            lt = lt_ref[pl.ds(q0, TQ), :]
            lts = [_lane_value(lt[:, (h // 2) * LANES:(h // 2 + 1) * LANES], (h % 2) * SB_HEAD_DIM) for h in range(NH)]

            def tile(k0, c, mask):
                cs, gs, accs = list(c[:NH]), list(c[NH:2 * NH]), list(c[2 * NH:])
                kjs = [k_s[pl.ds(k0, TQ), pr * LANES:(pr + 1) * LANES] for pr in range(PP)]
                vjs = [v_s[pl.ds(k0, TQ), pr * LANES:(pr + 1) * LANES] for pr in range(PP)]
                logs = [_sb_logs(qs[h], kjs[h // 2], mask) for h in range(NH)]
                pins = [_dot(_split_hi_lo(nk), u_pin) for _, nk in logs]
                das = [_dot_nt(dos[h], vjs[h // 2]) for h in range(NH)]
                a_l, g_l = [], []
                for h in range(NH):
                    a = jnp.exp2(logs[h][0] - ((lts[h] - cs[h]) - pins[h]))
                    if mask is not None:
                        a = jnp.where(mask, a, 0.0)
                    a_l.append(a)
                    g_l.append(das[h] * a)
                pres = [_dot(g.astype(BF16), u_pex) for g in g_l]
                dz_l = []
                for h in range(NH):
                    dz = g_l[h] - jnp.exp2(logs[h][0]) * (g_l[h] + (pres[h] + gs[h]))
                    if mask is not None:
                        dz = jnp.where(mask, dz, 0.0)
                    dz_l.append(dz.astype(BF16))
                for h in range(NH):
                    accs[h] = accs[h] + _dot(dz_l[h], kjs[h // 2])
                for pr in range(PP):
                    ps = slice(pr * LANES, (pr + 1) * LANES)
                    ha, hb = 2 * pr, 2 * pr + 1
                    dk_s[pl.ds(k0, TQ), ps] += _dot_tn(dz_l[ha], qs[ha]) + _dot_tn(dz_l[hb], qs[hb])
                    dv_s[pl.ds(k0, TQ), ps] += _dot_tn(a_l[ha].astype(BF16), dos[ha]) + _dot_tn(a_l[hb].astype(BF16), dos[hb])
                for h in range(NH):
                    cs[h] = cs[h] + jnp.sum(logs[h][1], axis=1, keepdims=True)
                    gs[h] = gs[h] + jnp.sum(g_l[h], axis=1, keepdims=True)
                return tuple(cs) + tuple(gs) + tuple(accs)

            z1 = jnp.zeros((TQ, 1), F32)
            zl = jnp.zeros((TQ, LANES), F32)

            def k_block(j, c):
                return tile(pl.multiple_of(j * TQ, TQ), c, None)

            c = lax.fori_loop(0, i, k_block, (z1,) * (2 * NH) + (zl,) * NH)
            c = tile(q0, c, vis)
            for pr in range(PP):
                dq = jnp.where(masks[0], c[2 * NH + 2 * pr], c[2 * NH + 2 * pr + 1]) * scale
                dq_ref[pl.ds(q0, TQ), pr * LANES:(pr + 1) * LANES] = dq.astype(BF16)
            return carry

        lax.fori_loop(0, nq, q_block, 0)
        dk_ref[...] = dk_s[...].astype(BF16)
        dv_ref[...] = dv_s[...].astype(BF16)

    blk = lambda off: pl.BlockSpec((seq, W), lambda b, g: (b, off + g))
    out_blk = pl.BlockSpec((seq, W), lambda b, g: (b, g))
    return pl.pallas_call(
        body, name=name, grid=(B, nstep),
        in_specs=[blk(P_Q // W), blk(P_K // W), blk(P_V // W), out_blk, out_blk],
        out_specs=[out_blk, out_blk, out_blk],
        out_shape=[jax.ShapeDtypeStruct((T, SB_W), BF16) for _ in range(3)],
        scratch_shapes=[pltpu.VMEM((seq, NH * LANES), BF16), pltpu.VMEM((seq, W), BF16), pltpu.VMEM((seq, W), BF16),
                        pltpu.VMEM((seq, NH * LANES), BF16), pltpu.VMEM((seq, W), F32), pltpu.VMEM((seq, W), F32)],
        compiler_params=_cparams(("parallel", "parallel")),
    )(p, p, p, ltot, do)


def _mla_masks():
    lane = lax.broadcasted_iota(jnp.int32, (1, 2 * LANES), 1)
    ma = (lane < MLA_NOPE) | ((lane >= LANES) & (lane < LANES + MLA_ROPE))
    mb = ((lane >= MLA_NOPE) & (lane < LANES)) | ((lane >= LANES + MLA_ROPE) & (lane < LANES + 2 * MLA_ROPE))
    return ma, mb


def _mla_fwd(qm, kvm, krt, *, seq, name):
    T = qm.shape[0]
    B = T // seq
    TQ = ATT_T
    nq = seq // TQ
    PP = ATT_PAIRS
    W = PP * LANES
    nstep = MLA_W // W
    NH = 2 * PP
    CW = 2 * LANES
    scale = MLA_QK ** -0.5

    def body(qn_ref, qr_ref, kn_ref, v_ref, kr_ref, o_ref, lse_ref, q_s, kc_s, v_s):
        hm = _head_masks()
        mm = _mla_masks()
        v = v_ref[...]
        for pr in range(PP):
            ps = slice(pr * LANES, (pr + 1) * LANES)
            qc = jnp.concatenate([qn_ref[:, ps], qr_ref[:, ps]], axis=1)
            kc_s[:, pr * CW:(pr + 1) * CW] = jnp.concatenate([kn_ref[:, ps], kr_ref[...]], axis=1)
            for e in range(2):
                h = 2 * pr + e
                q_s[:, h * CW:(h + 1) * CW] = _pick(mm[e], qc)
                v_s[:, h * LANES:(h + 1) * LANES] = _pick(hm[e], v[:, ps])
        vis = _causal_mask(TQ, False)

        def q_block(i, carry):
            q0 = pl.multiple_of(i * TQ, TQ)
            qs = [q_s[pl.ds(q0, TQ), h * CW:(h + 1) * CW] for h in range(NH)]

            def tile(k0, c, mask):
                ms, ls, accs = list(c[:NH]), list(c[NH:2 * NH]), list(c[2 * NH:])
                ss = [_dot_nt(qs[h], kc_s[pl.ds(k0, TQ), (h // 2) * CW:(h // 2 + 1) * CW]) * scale for h in range(NH)]
                if mask is not None:
                    ss = [jnp.where(mask, s, NEG_BIG) for s in ss]
                m_new = [jnp.maximum(ms[h], jnp.max(ss[h], axis=1, keepdims=True)) for h in range(NH)]
                alphas = [jnp.exp(ms[h] - m_new[h]) for h in range(NH)]
                prs = [jnp.exp(ss[h] - m_new[h]) for h in range(NH)]
                outs = [_dot(prs[h].astype(BF16), v_s[pl.ds(k0, TQ), h * LANES:(h + 1) * LANES]) for h in range(NH)]
                ls = [alphas[h] * ls[h] + jnp.sum(prs[h], axis=1, keepdims=True) for h in range(NH)]
                for pr in range(PP):
                    accs[pr] = accs[pr] * jnp.where(hm[0], alphas[2 * pr], alphas[2 * pr + 1]) + outs[2 * pr] + outs[2 * pr + 1]
                return tuple(m_new) + tuple(ls) + tuple(accs)

            neg = jnp.full((TQ, 1), NEG_BIG, F32)
            z1 = jnp.zeros((TQ, 1), F32)

            def k_block(j, c):
                return tile(pl.multiple_of(j * TQ, TQ), c, None)

            c = lax.fori_loop(0, i, k_block, (neg,) * NH + (z1,) * NH + (jnp.zeros((TQ, LANES), F32),) * PP)
            c = tile(q0, c, vis)
            for pr in range(PP):
                ps = slice(pr * LANES, (pr + 1) * LANES)
                m_a, m_b, l_a, l_b = c[2 * pr], c[2 * pr + 1], c[NH + 2 * pr], c[NH + 2 * pr + 1]
                o_ref[pl.ds(q0, TQ), ps] = c[2 * NH + pr] / jnp.where(hm[0], l_a, l_b)
                lse_ref[pl.ds(q0, TQ), ps] = jnp.where(hm[0], m_a + jnp.log(l_a), m_b + jnp.log(l_b))
            return carry

        lax.fori_loop(0, nq, q_block, 0)

    blk = lambda off: pl.BlockSpec((seq, W), lambda b, g: (b, off + g))
    out_blk = pl.BlockSpec((seq, W), lambda b, g: (b, g))
    return pl.pallas_call(
        body, name=name, grid=(B, nstep),
        in_specs=[blk(0), blk(nstep), blk(0), blk(nstep), pl.BlockSpec((seq, LANES), lambda b, g: (b, 0))],
        out_specs=[out_blk, out_blk],
        out_shape=[jax.ShapeDtypeStruct((T, MLA_W), F32), jax.ShapeDtypeStruct((T, MLA_W), F32)],
        scratch_shapes=[pltpu.VMEM((seq, NH * CW), BF16), pltpu.VMEM((seq, PP * CW), BF16), pltpu.VMEM((seq, NH * LANES), BF16)],
        compiler_params=_cparams(("parallel", "parallel")),
    )(qm, qm, kvm, kvm, krt)


def _mla_bwd(qm, kvm, krt, o, lse, do, *, seq, name):
    T = qm.shape[0]
    B = T // seq
    TQ = ATT_T
    nq = seq // TQ
    PP = ATT_PAIRS
    W = PP * LANES
    nstep = MLA_W // W
    NH = 2 * PP
    CW = 2 * LANES
    scale = MLA_QK ** -0.5

    def body(qn_ref, qr_ref, kn_ref, v_ref, kr_ref, o_ref, lse_ref, do_ref,
             dqn_ref, dqr_ref, dkn_ref, dv_ref, dkr_ref, q_s, kc_s, do_s, dkc_s, dv_s):
        hm = _head_masks()
        mm = _mla_masks()
        dof = do_ref[...]
        for pr in range(PP):
            ps = slice(pr * LANES, (pr + 1) * LANES)
            qc = jnp.concatenate([qn_ref[:, ps], qr_ref[:, ps]], axis=1)
            kc_s[:, pr * CW:(pr + 1) * CW] = jnp.concatenate([kn_ref[:, ps], kr_ref[...]], axis=1)
            for e in range(2):
                h = 2 * pr + e
                q_s[:, h * CW:(h + 1) * CW] = _pick(mm[e], qc)
                do_s[:, h * LANES:(h + 1) * LANES] = _pick(hm[e], dof[:, ps]).astype(BF16)
        dkc_s[...] = jnp.zeros_like(dkc_s)
        dv_s[...] = jnp.zeros_like(dv_s)
        vis = _causal_mask(TQ, False)

        def q_block(i, carry):
            q0 = pl.multiple_of(i * TQ, TQ)
            qs = [q_s[pl.ds(q0, TQ), h * CW:(h + 1) * CW] for h in range(NH)]
            dos = [do_s[pl.ds(q0, TQ), h * LANES:(h + 1) * LANES] for h in range(NH)]
            lse_t = lse_ref[pl.ds(q0, TQ), :]
            dd = do_ref[pl.ds(q0, TQ), :] * o_ref[pl.ds(q0, TQ), :]
            lses, ds_ = [], []
            for h in range(NH):
                ps = slice((h // 2) * LANES, (h // 2 + 1) * LANES)
                lses.append(_lane_value(lse_t[:, ps], (h % 2) * MLA_V))
                ds_.append(jnp.sum(_pick(hm[h % 2], dd[:, ps]), axis=1, keepdims=True))

            def tile(k0, c, mask):
                accs = list(c)
                kcs = [kc_s[pl.ds(k0, TQ), pr * CW:(pr + 1) * CW] for pr in range(PP)]
                vjs = [v_ref[pl.ds(k0, TQ), pr * LANES:(pr + 1) * LANES] for pr in range(PP)]
                ss = [_dot_nt(qs[h], kcs[h // 2]) * scale for h in range(NH)]
                dps = [_dot_nt(dos[h], vjs[h // 2]) for h in range(NH)]
                p_l, ds_l = [], []
                for h in range(NH):
                    pr_ = jnp.exp(ss[h] - lses[h])
                    if mask is not None:
                        pr_ = jnp.where(mask, pr_, 0.0)
                    p_l.append(pr_.astype(BF16))
                    ds_l.append((pr_ * (dps[h] - ds_[h]) * scale).astype(BF16))
                for h in range(NH):
                    accs[h] = accs[h] + _dot(ds_l[h], kcs[h // 2])
                for pr in range(PP):
                    ha, hb = 2 * pr, 2 * pr + 1
                    dkc_s[pl.ds(k0, TQ), pr * CW:(pr + 1) * CW] += _dot_tn(ds_l[ha], qs[ha]) + _dot_tn(ds_l[hb], qs[hb])
                    dv_s[pl.ds(k0, TQ), pr * LANES:(pr + 1) * LANES] += _dot_tn(p_l[ha], dos[ha]) + _dot_tn(p_l[hb], dos[hb])
                return tuple(accs)

            zc = jnp.zeros((TQ, CW), F32)

            def k_block(j, c):
                return tile(pl.multiple_of(j * TQ, TQ), c, None)

            c = lax.fori_loop(0, i, k_block, (zc,) * NH)
            c = tile(q0, c, vis)
            for pr in range(PP):
                ps = slice(pr * LANES, (pr + 1) * LANES)
                dq = _pick(mm[0], c[2 * pr]) + _pick(mm[1], c[2 * pr + 1])
                dqn_ref[pl.ds(q0, TQ), ps] = dq[:, :LANES].astype(BF16)
                dqr_ref[pl.ds(q0, TQ), ps] = dq[:, LANES:]
            return carry

        lax.fori_loop(0, nq, q_block, 0)
        dkr = dkc_s[:, LANES:CW]
        for pr in range(PP):
            dkn_ref[:, pr * LANES:(pr + 1) * LANES] = dkc_s[:, pr * CW:pr * CW + LANES].astype(BF16)
            if pr > 0:
                dkr = dkr + dkc_s[:, pr * CW + LANES:(pr + 1) * CW]
        dv_ref[...] = dv_s[...].astype(BF16)
        g = pl.program_id(1)

        @pl.when(g == 0)
        def _():
            dkr_ref[...] = dkr

        @pl.when(g > 0)
        def _():
            dkr_ref[...] += dkr

    blk = lambda off: pl.BlockSpec((seq, W), lambda b, g: (b, off + g))
    out_blk = pl.BlockSpec((seq, W), lambda b, g: (b, g))
    one_blk = pl.BlockSpec((seq, LANES), lambda b, g: (b, 0))
    return pl.pallas_call(
        body, name=name, grid=(B, nstep),
        in_specs=[blk(0), blk(nstep), blk(0), blk(nstep), one_blk, out_blk, out_blk, out_blk],
        out_specs=[out_blk, out_blk, out_blk, out_blk, one_blk],
        out_shape=[jax.ShapeDtypeStruct((T, MLA_W), BF16), jax.ShapeDtypeStruct((T, MLA_W), F32),
                   jax.ShapeDtypeStruct((T, MLA_W), BF16), jax.ShapeDtypeStruct((T, MLA_W), BF16),
                   jax.ShapeDtypeStruct((T, LANES), F32)],
        scratch_shapes=[pltpu.VMEM((seq, NH * CW), BF16), pltpu.VMEM((seq, PP * CW), BF16), pltpu.VMEM((seq, NH * LANES), BF16),
                        pltpu.VMEM((seq, PP * CW), F32), pltpu.VMEM((seq, W), F32)],
        compiler_params=_cparams(("parallel", "arbitrary")),
    )(qm, qm, kvm, kvm, krt, o, lse, do)


def _rope_tables(pos_ref, invf_ref):
    ang = pos_ref[...].astype(F32) * invf_ref[...]
    first = (_lane_iota() % MLA_ROPE) < (MLA_ROPE // 2)
    return jnp.cos(ang), jnp.sin(ang), first


def _rope_apply(x, cos, sin, first):
    rot = jnp.where(first, -pltpu.roll(x, LANES - MLA_ROPE // 2, 1), pltpu.roll(x, MLA_ROPE // 2, 1))
    return x * cos + rot * sin


def _rope_apply_t(dy, cos, sin, first):
    dys = dy * sin
    rot_t = jnp.where(first, pltpu.roll(dys, LANES - MLA_ROPE // 2, 1), -pltpu.roll(dys, MLA_ROPE // 2, 1))
    return dy * cos + rot_t


def _rope_fwd(qfull, p, pos, invf, *, tm, name):
    T = qfull.shape[0]
    ntile = MLA_W // LANES

    def body(q_ref, kr_ref, pos_ref, invf_ref, qm_ref, krt_ref):
        cos, sin, first = _rope_tables(pos_ref, invf_ref)
        qm_ref[:, :MLA_W] = q_ref[:, :MLA_W].astype(BF16)
        for t in range(ntile):
            sl = slice(MLA_W + t * LANES, MLA_W + (t + 1) * LANES)
            qm_ref[:, sl] = _rope_apply(q_ref[:, sl], cos, sin, first).astype(BF16)
        krt_ref[...] = _rope_apply(kr_ref[...], cos, sin, first).astype(BF16)

    return pl.pallas_call(
        body, name=name, grid=(T // tm,),
        in_specs=[pl.BlockSpec((tm, 2 * MLA_W), lambda i: (i, 0)), pl.BlockSpec((tm, LANES), lambda i: (i, P_KRT // LANES)),
                  pl.BlockSpec((tm, 1), lambda i: (i, 0)), pl.BlockSpec((1, LANES), lambda i: (0, 0))],
        out_specs=[pl.BlockSpec((tm, 2 * MLA_W), lambda i: (i, 0)), pl.BlockSpec((tm, LANES), lambda i: (i, 0))],
        out_shape=[jax.ShapeDtypeStruct((T, 2 * MLA_W), BF16), jax.ShapeDtypeStruct((T, LANES), BF16)],
        compiler_params=_cparams(("parallel",)),
    )(qfull, p, pos, invf)


def _rope_bwd(dqr, dkr, pos, invf, *, tm, name):
    T = dqr.shape[0]
    ntile = MLA_W // LANES

    def body(dq_ref, dk_ref, pos_ref, invf_ref, oq_ref, ok_ref):
        cos, sin, first = _rope_tables(pos_ref, invf_ref)
        for t in range(ntile):
            sl = slice(t * LANES, (t + 1) * LANES)
            oq_ref[:, sl] = _rope_apply_t(dq_ref[:, sl], cos, sin, first).astype(BF16)
        ok_ref[...] = _rope_apply_t(dk_ref[...], cos, sin, first).astype(BF16)

    return pl.pallas_call(
        body, name=name, grid=(T // tm,),
        in_specs=[pl.BlockSpec((tm, MLA_W), lambda i: (i, 0)), pl.BlockSpec((tm, LANES), lambda i: (i, 0)),
                  pl.BlockSpec((tm, 1), lambda i: (i, 0)), pl.BlockSpec((1, LANES), lambda i: (0, 0))],
        out_specs=[pl.BlockSpec((tm, MLA_W), lambda i: (i, 0)), pl.BlockSpec((tm, LANES), lambda i: (i, 0))],
        out_shape=[jax.ShapeDtypeStruct((T, MLA_W), BF16), jax.ShapeDtypeStruct((T, LANES), BF16)],
        compiler_params=_cparams(("parallel",)),
    )(dqr, dkr, pos, invf)


CONV_ROWS = 256
HALO = 8


def _conv_taps(w_ref):
    return w_ref[0:1, :], w_ref[1:2, :], w_ref[2:3, :]


def _conv_rows(cur, prev, w, bias):
    ext = jnp.concatenate([prev, cur], axis=0)
    u1 = pltpu.roll(ext, 1, 0)[HALO:]
    u2 = pltpu.roll(ext, 2, 0)[HALO:]
    return w[2] * cur + w[1] * u1 + w[0] * u2 + bias, u1, u2


def _conv_fwd(u, w, bias, *, seq, name):
    T = u.shape[0]
    B = T // seq
    W2 = 2 * FF_BLK

    def body(u_ref, w_ref, b_ref, a_ref):
        wv = _conv_taps(w_ref)
        bv = b_ref[...]
        for c in range(seq // CONV_ROWS):
            r0 = c * CONV_ROWS
            cur = u_ref[r0:r0 + CONV_ROWS, :]
            prev = u_ref[r0 - HALO:r0, :] if c > 0 else jnp.zeros((HALO, W2), F32)
            y, _, _ = _conv_rows(cur, prev, wv, bv)
            gc = y[:, :FF_BLK]
            a_ref[r0:r0 + CONV_ROWS, :] = (gc * (1.0 / (1.0 + jnp.exp(-gc))) * y[:, FF_BLK:]).astype(BF16)

    return pl.pallas_call(
        body, name=name, grid=(B, N_FF_BLK),
        in_specs=[pl.BlockSpec((seq, W2), lambda b, j: (b, j)), pl.BlockSpec((3, W2), lambda b, j: (0, j)),
                  pl.BlockSpec((1, W2), lambda b, j: (0, j))],
        out_specs=pl.BlockSpec((seq, FF_BLK), lambda b, j: (b, j)),
        out_shape=jax.ShapeDtypeStruct((T, D_FF), BF16),
        compiler_params=_cparams(("parallel", "parallel")),
    )(u, w, bias)


def _conv_bwd(u, da, w, bias, *, seq, name):
    T = u.shape[0]
    B = T // seq
    W2 = 2 * FF_BLK
    nchunk = seq // CONV_ROWS

    def body(u_ref, da_ref, w_ref, b_ref, du_ref, dw_ref, db_ref, duc_s):
        wv = _conv_taps(w_ref)
        bv = b_ref[...]
        zrow = jnp.zeros((1, W2), F32)
        dw0, dw1, dw2, dbs = zrow, zrow, zrow, zrow
        for c in range(nchunk):
            r0 = c * CONV_ROWS
            cur = u_ref[r0:r0 + CONV_ROWS, :]
            prev = u_ref[r0 - HALO:r0, :] if c > 0 else jnp.zeros((HALO, W2), F32)
            y, u1, u2 = _conv_rows(cur, prev, wv, bv)
            gc = y[:, :FF_BLK]
            vc = y[:, FF_BLK:]
            sg = 1.0 / (1.0 + jnp.exp(-gc))
            dav = da_ref[r0:r0 + CONV_ROWS, :]
            duc = jnp.concatenate([dav * vc * (sg * (1.0 + gc * (1.0 - sg))), dav * (gc * sg)], axis=1)
            duc_s[r0:r0 + CONV_ROWS, :] = duc
            dw0 = dw0 + jnp.sum(duc * u2, axis=0, keepdims=True)
            dw1 = dw1 + jnp.sum(duc * u1, axis=0, keepdims=True)
            dw2 = dw2 + jnp.sum(duc * cur, axis=0, keepdims=True)
            dbs = dbs + jnp.sum(duc, axis=0, keepdims=True)
        duc_s[seq:seq + HALO, :] = jnp.zeros((HALO, W2), F32)
        n_ext = CONV_ROWS + HALO
        for c in range(nchunk):
            r0 = c * CONV_ROWS
            ext = duc_s[r0:r0 + n_ext, :]
            s1 = pltpu.roll(ext, n_ext - 1, 0)[:CONV_ROWS]
            s2 = pltpu.roll(ext, n_ext - 2, 0)[:CONV_ROWS]
            du_ref[r0:r0 + CONV_ROWS, :] = (wv[2] * ext[:CONV_ROWS] + wv[1] * s1 + wv[0] * s2).astype(BF16)

        first = pl.program_id(1) == 0

        @pl.when(first)
        def _():
            dw_ref[0:1, :] = dw0
            dw_ref[1:2, :] = dw1
            dw_ref[2:3, :] = dw2
            db_ref[...] = dbs

        @pl.when(jnp.logical_not(first))
        def _():
            dw_ref[0:1, :] += dw0
            dw_ref[1:2, :] += dw1
            dw_ref[2:3, :] += dw2
            db_ref[...] += dbs

    return pl.pallas_call(
        body, name=name, grid=(N_FF_BLK, B),
        in_specs=[pl.BlockSpec((seq, W2), lambda j, b: (b, j)), pl.BlockSpec((seq, FF_BLK), lambda j, b: (b, j)),
                  pl.BlockSpec((3, W2), lambda j, b: (0, j)), pl.BlockSpec((1, W2), lambda j, b: (0, j))],
        out_specs=[pl.BlockSpec((seq, W2), lambda j, b: (b, j)), pl.BlockSpec((3, W2), lambda j, b: (0, j)),
                   pl.BlockSpec((1, W2), lambda j, b: (0, j))],
        out_shape=[jax.ShapeDtypeStruct((T, 2 * D_FF), BF16), jax.ShapeDtypeStruct((3, 2 * D_FF), F32),
                   jax.ShapeDtypeStruct((1, 2 * D_FF), F32)],
        scratch_shapes=[pltpu.VMEM((seq + HALO, W2), F32)],
        compiler_params=_cparams(("parallel", "arbitrary")),
    )(u, da, w, bias)


def _place():
    return lax.axis_index("x"), lax.axis_index("y"), lax.axis_index("c")


def _other_chips(x, y):
    return [(1 - x, y), (x, 1 - y), (1 - x, 1 - y)]


def _all_gather(vs, *, name):
    n = len(vs)

    def body(*refs):
        v_refs, out_refs = refs[:n], refs[n:2 * n]
        send_sems, recv_sems, local_sems = refs[2 * n:]
        x, y, c = _place()
        me, sibling = (x, y, c), (x, y, 1 - c)
        chips = _other_chips(x, y)

        def slab(a, px, py, pc):
            return out_refs[a].at[4 * px + 2 * py + pc]

        def copy(a, k, block, to, src=None):
            return pltpu.make_async_remote_copy(
                src_ref=slab(a, *block) if src is None else src, dst_ref=slab(a, *block),
                send_sem=send_sems.at[7 * a + k], recv_sem=recv_sems.at[7 * a + k], device_id=to, device_id_type=MESH)

        mine = [pltpu.make_async_copy(v_refs[a], slab(a, *me), local_sems.at[a]) for a in range(n)]
        for cp in mine:
            cp.start()
        first = []
        for a in range(n):
            first.append(copy(a, 0, me, sibling, src=v_refs[a]))
            first += [copy(a, 1 + j, me, (*chip, c), src=v_refs[a]) for j, chip in enumerate(chips)]
        for cp in first:
            cp.start()
        passed = []
        for j, chip in enumerate(chips):
            for a in range(n):
                copy(a, 1 + j, (*chip, c), me).wait_recv()
                cp = copy(a, 4 + j, (*chip, c), sibling)
                cp.start()
                passed.append(cp)
        for a in range(n):
            copy(a, 0, sibling, me).wait_recv()
            for j, chip in enumerate(chips):
                copy(a, 4 + j, (*chip, 1 - c), me).wait_recv()
        for cp in first + passed:
            cp.wait_send()
        for cp in mine:
            cp.wait()

    return pl.pallas_call(
        body, name=name, in_specs=[ANY] * n, out_specs=[ANY] * n,
        out_shape=[jax.ShapeDtypeStruct((N_DEV,) + v.shape, v.dtype) for v in vs],
        scratch_shapes=[pltpu.SemaphoreType.DMA((7 * n,)), pltpu.SemaphoreType.DMA((7 * n,)), pltpu.SemaphoreType.DMA((n,))],
    )(*vs)


def _all_gather_async(vs, *, name, collective_id):
    n = len(vs)
    v_refs = [jax.new_ref(v, memory_space=pltpu.MemorySpace.HBM) for v in vs]
    out_refs = [jax.empty_ref(jax.ShapeDtypeStruct((N_DEV,) + v.shape, v.dtype), memory_space=pltpu.MemorySpace.HBM)
                for v in vs]

    @pl.kernel(mesh=plsc.ScalarSubcoreMesh(axis_name="seq", num_cores=1), name=name,
               scratch_types=(pltpu.SemaphoreType.DMA((7 * n,)), pltpu.SemaphoreType.DMA((7 * n,)),
                              pltpu.SemaphoreType.DMA((n,))),
               compiler_params=pltpu.CompilerParams(collective_id=collective_id))
    def launch(send_sems, recv_sems, local_sems):
        x, y, c = _place()
        me, sibling = (x, y, c), (x, y, 1 - c)
        chips = _other_chips(x, y)
        peers = [sibling] + [(*chip, c) for chip in chips]
        barrier = pltpu.get_barrier_semaphore()
        for peer in peers:
            pl.semaphore_signal(barrier, inc=1, device_id=peer, device_id_type=MESH)
        pl.semaphore_wait(barrier, len(peers))

        def slab(a, px, py, pc):
            return out_refs[a].at[4 * px + 2 * py + pc]

        def copy(a, k, block, to, src=None):
            return pltpu.make_async_remote_copy(
                src_ref=slab(a, *block) if src is None else src, dst_ref=slab(a, *block),
                send_sem=send_sems.at[7 * a + k], recv_sem=recv_sems.at[7 * a + k], device_id=to, device_id_type=MESH)

        mine = [pltpu.make_async_copy(v_refs[a], slab(a, *me), local_sems.at[a]) for a in range(n)]
        for cp in mine:
            cp.start()
        first = []
        for a in range(n):
            first.append(copy(a, 0, me, sibling, src=v_refs[a]))
            first += [copy(a, 1 + j, me, (*chip, c), src=v_refs[a]) for j, chip in enumerate(chips)]
        for cp in first:
            cp.start()
        passed = []
        for j, chip in enumerate(chips):
            for a in range(n):
                copy(a, 1 + j, (*chip, c), me).wait_recv()
                cp = copy(a, 4 + j, (*chip, c), sibling)
                cp.start()
                passed.append(cp)
        for a in range(n):
            copy(a, 0, sibling, me).wait_recv()
            for j, chip in enumerate(chips):
                copy(a, 4 + j, (*chip, 1 - c), me).wait_recv()
        for cp in first + passed:
            cp.wait_send()
        for cp in mine:
            cp.wait()

    launch()
    return [r[...] for r in out_refs]


def _rs_sibling(g8s, *, name):
    n = len(g8s)

    def body(*refs):
        g_refs, out_refs = refs[:n], refs[n:2 * n]
        send_sems, recv_sems = refs[2 * n:]
        x, y, c = _place()
        copies = [
            pltpu.make_async_remote_copy(
                src_ref=g_refs[a].at[2 * k + 1 - c], dst_ref=out_refs[a].at[k],
                send_sem=send_sems.at[4 * a + k], recv_sem=recv_sems.at[4 * a + k],
                device_id=(x, y, 1 - c), device_id_type=MESH)
            for a in range(n) for k in range(4)]
        for cp in copies:
            cp.start()
        for cp in copies:
            cp.wait()

    return pl.pallas_call(
        body, name=name, in_specs=[ANY] * n, out_specs=[ANY] * n,
        out_shape=[jax.ShapeDtypeStruct((4,) + g.shape[1:], g.dtype) for g in g8s],
        scratch_shapes=[pltpu.SemaphoreType.DMA((4 * n,)), pltpu.SemaphoreType.DMA((4 * n,))],
    )(*g8s)


def _handshake(peers):
    barrier = pltpu.get_barrier_semaphore()
    for peer in peers:
        pl.semaphore_signal(barrier, inc=1, device_id=peer, device_id_type=MESH)
    pl.semaphore_wait(barrier, len(peers))


def _hbm_refs(arrays, lead):
    src = [jax.new_ref(a, memory_space=pltpu.MemorySpace.HBM) for a in arrays]
    dst = [jax.empty_ref(jax.ShapeDtypeStruct((lead,) + a.shape[1:], a.dtype), memory_space=pltpu.MemorySpace.HBM)
           for a in arrays]
    return src, dst


def _rs_sibling_async(g8s, *, name, collective_id):
    n = len(g8s)
    g_refs, out_refs = _hbm_refs(g8s, 4)

    @pl.kernel(mesh=plsc.ScalarSubcoreMesh(axis_name="seq", num_cores=1), name=name,
               scratch_types=(pltpu.SemaphoreType.DMA((4 * n,)), pltpu.SemaphoreType.DMA((4 * n,))),
               compiler_params=pltpu.CompilerParams(collective_id=collective_id))
    def launch(send_sems, recv_sems):
        x, y, c = _place()
        _handshake([(x, y, 1 - c)])
        copies = [
            pltpu.make_async_remote_copy(
                src_ref=g_refs[a].at[2 * k + 1 - c], dst_ref=out_refs[a].at[k],
                send_sem=send_sems.at[4 * a + k], recv_sem=recv_sems.at[4 * a + k],
                device_id=(x, y, 1 - c), device_id_type=MESH)
            for a in range(n) for k in range(4)]
        for cp in copies:
            cp.start()
        for cp in copies:
            cp.wait()

    launch()
    return [r[...] for r in out_refs]


def _rs_chips_async(h4s, *, name, collective_id):
    n = len(h4s)
    h_refs, out_refs = _hbm_refs(h4s, 3)

    @pl.kernel(mesh=plsc.ScalarSubcoreMesh(axis_name="seq", num_cores=1), name=name,
               scratch_types=(pltpu.SemaphoreType.DMA((3 * n,)), pltpu.SemaphoreType.DMA((3 * n,))),
               compiler_params=pltpu.CompilerParams(collective_id=collective_id))
    def launch(send_sems, recv_sems):
        x, y, c = _place()
        chips = _other_chips(x, y)
        _handshake([(cx, cy, c) for cx, cy in chips])
        copies = [
            pltpu.make_async_remote_copy(
                src_ref=h_refs[a].at[2 * cx + cy], dst_ref=out_refs[a].at[j],
                send_sem=send_sems.at[3 * a + j], recv_sem=recv_sems.at[3 * a + j],
                device_id=(cx, cy, c), device_id_type=MESH)
            for a in range(n) for j, (cx, cy) in enumerate(chips)]
        for cp in copies:
            cp.start()
        for cp in copies:
            cp.wait()

    launch()
    return [r[...] for r in out_refs]


def _row_tile(rows):
    return rows if rows <= 512 else 256


def _rs_chip_sum(g8, from_sibling, place_idx, *, name):
    _, R, C = g8.shape
    tr = _row_tile(R)

    def body(pi_ref, a_ref, b_ref, f_ref, h_ref):
        s = a_ref[...] + b_ref[...]
        h_ref[...] = s.astype(BF16)

        @pl.when(pl.program_id(1) == pi_ref[1])
        def _():
            f_ref[...] = s

    blk = pl.BlockSpec((None, tr, C), lambda r, k, pi_ref: (k, r, 0))
    return pl.pallas_call(
        body, name=name,
        grid_spec=pltpu.PrefetchScalarGridSpec(
            num_scalar_prefetch=1, grid=(R // tr, 4),
            in_specs=[pl.BlockSpec((None, tr, C), lambda r, k, pi_ref: (2 * k + pi_ref[0], r, 0)), blk],
            out_specs=[pl.BlockSpec((tr, C), lambda r, k, pi_ref: (r, 0)), blk]),
        out_shape=[jax.ShapeDtypeStruct((R, C), F32), jax.ShapeDtypeStruct((4, R, C), BF16)],
        compiler_params=_cparams(("parallel", "arbitrary")),
    )(place_idx, g8, from_sibling)


def _rs_chips(h4s, *, name):
    n = len(h4s)

    def body(*refs):
        h_refs, out_refs = refs[:n], refs[n:2 * n]
        send_sems, recv_sems = refs[2 * n:]
        x, y, c = _place()
        copies = [
            pltpu.make_async_remote_copy(
                src_ref=h_refs[a].at[2 * cx + cy], dst_ref=out_refs[a].at[j],
                send_sem=send_sems.at[3 * a + j], recv_sem=recv_sems.at[3 * a + j],
                device_id=(cx, cy, c), device_id_type=MESH)
            for a in range(n) for j, (cx, cy) in enumerate(_other_chips(x, y))]
        for cp in copies:
            cp.start()
        for cp in copies:
            cp.wait()

    return pl.pallas_call(
        body, name=name, in_specs=[ANY] * n, out_specs=[ANY] * n,
        out_shape=[jax.ShapeDtypeStruct((3,) + h.shape[1:], h.dtype) for h in h4s],
        scratch_shapes=[pltpu.SemaphoreType.DMA((3 * n,)), pltpu.SemaphoreType.DMA((3 * n,))],
    )(*h4s)


def _split_moves(segments, chunk):
    moves = []
    for dst, src, length in segments:
        while length > 0:
            dev, off = divmod(src, chunk)
            take = min(length, chunk - off)
            moves.append((dst, dev, off, take))
            dst, src, length = dst + take, src + take, length - take
    return moves


def _assemble(stacked, segments, zero_spans, out_cols, *, name):
    _, R, c = stacked.shape
    tr = _row_tile(R)
    moves = _split_moves(segments, c)

    def body(x_ref, o_ref):
        for dst, dev, off, take in moves:
            o_ref[:, dst:dst + take] = x_ref[dev, :, off:off + take]
        for a, b in zero_spans:
            o_ref[:, a:b] = jnp.zeros((tr, b - a), o_ref.dtype)

    return pl.pallas_call(
        body, name=name, grid=(R // tr,),
        in_specs=[pl.BlockSpec((N_DEV, tr, c), lambda i: (0, i, 0))],
        out_specs=pl.BlockSpec((tr, out_cols), lambda i: (i, 0)),
        out_shape=jax.ShapeDtypeStruct((R, out_cols), stacked.dtype),
        compiler_params=_cparams(("parallel",)),
    )(stacked)


def _disassemble(full, segments, chunk, *, name):
    R = full.shape[0]
    tr = _row_tile(R)
    moves = _split_moves(segments, chunk)

    def body(x_ref, o_ref):
        seen = set()
        for dst, dev, off, take in moves:
            piece = x_ref[:, dst:dst + take]
            if (dev, off) in seen:
                piece = piece + o_ref[dev, :, off:off + take]
            seen.add((dev, off))
            o_ref[dev, :, off:off + take] = piece

    return pl.pallas_call(
        body, name=name, grid=(R // tr,),
        in_specs=[pl.BlockSpec((tr, full.shape[1]), lambda i: (i, 0))],
        out_specs=pl.BlockSpec((N_DEV, tr, chunk), lambda i: (0, i, 0)),
        out_shape=jax.ShapeDtypeStruct((N_DEV, R, chunk), F32),
        compiler_params=_cparams(("parallel",)),
    )(full)


_O_CQ = 3 * SB_W
_O_CKV = _O_CQ + Q_LORA
_O_KR = _O_CKV + KV_LORA
SEG_W_IN = ((0, 0, 3 * SB_W), (P_CKV, _O_CKV, KV_LORA), (P_KRT, _O_KR, MLA_ROPE), (P_KRT + MLA_ROPE, _O_KR, MLA_ROPE),
            (P_CQ, _O_CQ, Q_LORA))
ZERO_W_IN = ((P_KRT + 2 * MLA_ROPE, P_CQ),)
SEG_W_UQ = tuple((MLA_NOPE * h, MLA_QK * h, MLA_NOPE) for h in range(MLA_HEADS)) + tuple(
    (MLA_W + LANES * (h // 2) + MLA_ROPE * (h % 2), MLA_QK * h + MLA_NOPE, MLA_ROPE) for h in range(MLA_HEADS))
ZERO_W_UQ = tuple((MLA_W + LANES * g + 2 * MLA_ROPE, MLA_W + LANES * (g + 1)) for g in range(MLA_HEADS // 2))
SEG_W_UKV = tuple((MLA_NOPE * h, (MLA_NOPE + MLA_V) * h, MLA_NOPE) for h in range(MLA_HEADS)) + tuple(
    (MLA_W + MLA_V * h, (MLA_NOPE + MLA_V) * h + MLA_NOPE, MLA_V) for h in range(MLA_HEADS))
SEG_W_UP = tuple((2 * FF_BLK * blk + FF_BLK * half, D_FF * half + FF_BLK * blk, FF_BLK)
                 for half in range(2) for blk in range(N_FF_BLK))


def _sum8(g, *, name):
    _, R, C = g.shape

    def body(g_ref, o_ref):
        acc = g_ref[0]
        for k in range(1, N_DEV):
            acc = acc + g_ref[k]
        o_ref[...] = acc

    return pl.pallas_call(
        body, name=name, out_shape=jax.ShapeDtypeStruct((R, C), F32),
    )(g)


def _adamw_math(w, gf, m, v):
    c1 = 1.0 / (1.0 - ADAM_B1 ** ADAM_STEP)
    c2 = 1.0 / (1.0 - ADAM_B2 ** ADAM_STEP)
    mn = ADAM_B1 * m + (1.0 - ADAM_B1) * gf
    vn = ADAM_B2 * v + (1.0 - ADAM_B2) * (gf * gf)
    return -ADAM_LR * ((mn * c1) / (jnp.sqrt(vn * c2) + ADAM_EPS) + ADAM_WD * w), mn, vn


def _adamw(w, g, m, v, *, name):
    R, C = w.shape
    tr = _row_tile(R)

    def body(w_ref, g_ref, m_ref, v_ref, d_ref, mo_ref, vo_ref):
        d_ref[...], mo_ref[...], vo_ref[...] = _adamw_math(w_ref[...], g_ref[...], m_ref[...], v_ref[...])

    blk = pl.BlockSpec((tr, C), lambda i: (i, 0))
    shp = jax.ShapeDtypeStruct((R, C), F32)
    return pl.pallas_call(
        body, name=name, grid=(R // tr,), in_specs=[blk] * 4, out_specs=[blk] * 3,
        out_shape=[shp, shp, shp], compiler_params=_cparams(("parallel",)),
    )(w, g, m, v)


def _adamw_rs(own, r3, w, m, v, *, name):
    R, C = w.shape
    tr = _row_tile(R)

    def body(f_ref, r_ref, w_ref, m_ref, v_ref, g_ref, d_ref, mo_ref, vo_ref):
        gf = ((f_ref[...] + r_ref[0].astype(F32)) + r_ref[1].astype(F32)) + r_ref[2].astype(F32)
        g_ref[...] = gf
        d_ref[...], mo_ref[...], vo_ref[...] = _adamw_math(w_ref[...], gf, m_ref[...], v_ref[...])

    blk = pl.BlockSpec((tr, C), lambda i: (i, 0))
    shp = jax.ShapeDtypeStruct((R, C), F32)
    return pl.pallas_call(
        body, name=name, grid=(R // tr,),
        in_specs=[blk, pl.BlockSpec((3, tr, C), lambda i: (0, i, 0)), blk, blk, blk], out_specs=[blk] * 4,
        out_shape=[shp] * 4, compiler_params=_cparams(("parallel",)),
    )(own, r3, w, m, v)


def _ff_interleave(a):
    lead = a.shape[:-1]
    return a.reshape(*lead, 2, N_FF_BLK, FF_BLK).swapaxes(-3, -2).reshape(*lead, 2 * D_FF)


def _ff_deinterleave(a):
    lead = a.shape[:-1]
    return a.reshape(*lead, N_FF_BLK, 2, FF_BLK).swapaxes(-3, -2).reshape(*lead, 2 * D_FF)


SMALL =(("g_mix", D_MODEL), ("g_cq", Q_LORA), ("g_ckv", KV_LORA), ("g_sb_out", SB_W), ("g_mla_out", MLA_W),
         ("g_ffn", D_MODEL), ("conv_b", 2 * D_FF), ("g_final", D_MODEL))
SMALL_ROWS = 88


SMALL_USED = sum(size for _, size in SMALL)


def _pack_small(d, tail=None):
    parts = [d[n].reshape(-1) for n, _ in SMALL] + ([] if tail is None else [tail])
    flat = jnp.concatenate(parts)
    flat = jnp.pad(flat, (0, SMALL_ROWS * LANES - flat.shape[0]))
    return flat.reshape(SMALL_ROWS, LANES)


def _unpack_small(a):
    flat = a.reshape(-1)
    out, off = {}, 0
    for n, size in SMALL:
        out[n] = flat[off:off + size]
        off += size
    return out


def kernel(x, positions, g_mix, w_in, g_cq, w_uq, g_ckv, w_ukv, g_sb_out, g_mla_out, w_out, g_ffn, w_up, conv_w, conv_b, w_down, g_final, loss_target, m_g_mix, m_w_in, m_g_cq, m_w_uq, m_g_ckv, m_w_ukv, m_g_sb_out, m_g_mla_out, m_w_out, m_g_ffn, m_w_up, m_conv_w, m_conv_b, m_w_down, m_g_final, v_g_mix, v_w_in, v_g_cq, v_w_uq, v_g_ckv, v_w_ukv, v_g_sb_out, v_g_mla_out, v_w_out, v_g_ffn, v_w_up, v_conv_w, v_conv_b, v_w_down, v_g_final):
    B, S, D = x.shape
    T = B * S
    xf = x.reshape(T, D)
    tgt = loss_target.reshape(T, D)
    pos = positions.reshape(T, 1)
    half = MLA_ROPE // 2
    inv_freq = 1.0 / (ROPE_BASE ** (jnp.arange(half, dtype=F32) * (2.0 / MLA_ROPE)))
    invf = jnp.tile(inv_freq, LANES // half).reshape(1, LANES)
    place_idx = jnp.stack([lax.axis_index("c"), 2 * lax.axis_index("x") + lax.axis_index("y")]).astype(jnp.int32)

    names = ("w_in", "w_uq", "w_ukv", "w_out", "w_up", "w_down", "conv_w")
    shard = {"w_in": w_in[0], "w_uq": w_uq[0], "w_ukv": w_ukv[0], "w_out": w_out[0], "w_up": w_up[0],
             "w_down": w_down[0], "conv_w": conv_w[0]}
    sent = {n: shard[n] if n == "conv_w" else shard[n].astype(BF16) for n in names}
    later = names[1:]
    w_in_all = _all_gather([sent["w_in"]], name="ag_w_in")[0]
    w_in_all, rest = lax.optimization_barrier((w_in_all, [sent[n] for n in later]))
    got = {"w_in": w_in_all}
    got.update(zip(later, _all_gather_async(rest, name="ag_weights_async", collective_id=0)))
    wi = _assemble(got["w_in"], SEG_W_IN, ZERO_W_IN, P_COLS, name="asm_w_in")
    wuq = _assemble(got["w_uq"], SEG_W_UQ, ZERO_W_UQ, 2 * MLA_W, name="asm_w_uq")
    wukv = _assemble(got["w_ukv"], SEG_W_UKV, (), 2 * MLA_W, name="asm_w_ukv")
    wup = _assemble(got["w_up"], SEG_W_UP, (), 2 * D_FF, name="asm_w_up")
    cwi = _assemble(got["conv_w"], SEG_W_UP, (), 2 * D_FF, name="asm_conv_w")
    wo = got["w_out"].reshape(D, D)
    wdn = got["w_down"].reshape(D_FF, D)
    cbi = _ff_interleave(conv_b)

    h, p = _rms_matmul_nn(xf, g_mix, wi, tm=512, name="proj_in")
    o_sb, ltot = _sb_fwd(p, seq=S, name="sb_fwd")
    cq, qfull = _rms_matmul_nn(p, g_cq, wuq, tm=512, name="proj_uq", col_block=P_CQ // Q_LORA)
    ckv, kvm = _rms_matmul_nn(p, g_ckv, wukv, tm=512, name="proj_ukv", col_block=P_CKV // KV_LORA, out_dtype=BF16)
    qm, krt = _rope_fwd(qfull, p, pos, invf, tm=512, name="rope_fwd")
    o_mla, lse = _mla_fwd(qm, kvm, krt, seq=S, name="mla_fwd")
    ocat = _rms2_fwd(o_sb, o_mla, g_sb_out, g_mla_out, tm=512, name="rms_heads")
    x1 = _matmul_nn(ocat, wo, tm=512, tn=1024, out_dtype=F32, name="proj_out", residual=xf)
    hf, u = _rms_matmul_nn(x1, g_ffn, wup, tm=256, name="ffn_up")
    a = _conv_fwd(u, cwi, cbi, seq=S, name="conv_fwd")
    dx2, dg_final, loss_row = _matmul_nn_loss(a, wdn, x1, g_final.reshape(1, D), tgt, tm=512, name="ffn_down_loss")

    da = _matmul_nt(dx2, wdn, tm=1024, tn=D_FF // 2, out_dtype=F32, name="d_ffn_down")
    dw_down = _matmul_tn(a, dx2, tm=D_FF // 2, tn=1024, tk=1024, name="dw_down")
    du, dcw, dcb = _conv_bwd(u, da, cwi, cbi, seq=S, name="conv_bwd")
    dw_up = _matmul_tn(hf, du, tm=1024, tn=D_FF, tk=1024, name="dw_up")
    dx1, dg_ffn = _matmul_nt_rms_bwd(du, wup, x1, g_ffn, tm=512, name="d_ffn_up", residual=dx2)
    docat = _matmul_nt(dx1, wo, tm=512, tn=1024, out_dtype=F32, name="d_proj_out")
    dw_out = _matmul_tn(ocat, dx1, tm=1024, tn=1024, tk=1024, name="dw_out")
    do_sb, do_mla, dg_sb, dg_mla = _rms2_bwd(docat, o_sb, o_mla, g_sb_out, g_mla_out, tm=512, name="rms_heads_bwd")

    early = ("w_down", "w_up", "conv_w", "w_out")
    g8 = {"w_up": _disassemble(dw_up, SEG_W_UP, shard["w_up"].shape[1], name="split_dw_up"),
          "conv_w": _disassemble(dcw, SEG_W_UP, shard["conv_w"].shape[1], name="split_dconv_w"),
          "w_out": dw_out.reshape((N_DEV,) + shard["w_out"].shape),
          "w_down": dw_down.reshape((N_DEV,) + shard["w_down"].shape)}
    sib_e = _rs_sibling_async([g8[n] for n in early], name="rs_sibling_async", collective_id=1)

    dq_sb, dk_sb, dv_sb = _sb_bwd(p, ltot, do_sb, seq=S, name="sb_bwd")
    sib_e, dq_sb = lax.optimization_barrier((sib_e, dq_sb))
    own, r3 = {}, {}
    sums_e = [_rs_chip_sum(g8[n], fs, place_idx, name="rs_chip_sum_" + n) for n, fs in zip(early, sib_e)]
    r3.update(zip(early, _rs_chips_async([h4 for _, h4 in sums_e], name="rs_chips_async", collective_id=2)))
    own.update({n: f for n, (f, _) in zip(early, sums_e)})

    dqn, dqr, dkn, dvm, dkr = _mla_bwd(qm, kvm, krt, o_mla, lse, do_mla, seq=S, name="mla_bwd")
    dqr_u, dkr_u = _rope_bwd(dqr, dkr, pos, invf, tm=512, name="rope_bwd")
    dqm = jnp.concatenate([dqn, dqr_u], axis=1)
    dkvm = jnp.concatenate([dkn, dvm], axis=1)
    dw_uq = _matmul_tn(cq, dqm, tm=Q_LORA, tn=1024, tk=1024, name="dw_uq")
    dw_ukv = _matmul_tn(ckv, dkvm, tm=KV_LORA, tn=1024, tk=1024, name="dw_ukv")
    dcq, dg_cq = _matmul_nt_rms_bwd(dqm, wuq, p, g_cq, tm=512, name="d_proj_uq", col_block=P_CQ // Q_LORA, out_dtype=BF16)
    dckv, dg_ckv = _matmul_nt_rms_bwd(dkvm, wukv, p, g_ckv, tm=512, name="d_proj_ukv", col_block=P_CKV // KV_LORA,
                                      out_dtype=BF16)
    dp = jnp.concatenate([dq_sb, dk_sb, dv_sb, dckv, dkr_u, dcq], axis=1)
    dw_in = _matmul_tn(h, dp, tm=1024, tn=P_COLS, tk=1024, name="dw_in")

    late = ("w_in", "w_uq", "w_ukv")
    g8.update({"w_in": _disassemble(dw_in, SEG_W_IN, shard["w_in"].shape[1], name="split_dw_in"),
               "w_uq": _disassemble(dw_uq, SEG_W_UQ, shard["w_uq"].shape[1], name="split_dw_uq"),
               "w_ukv": _disassemble(dw_ukv, SEG_W_UKV, shard["w_ukv"].shape[1], name="split_dw_ukv")})
    sib_l = _rs_sibling_async([g8[n] for n in late], name="rs_sibling_late", collective_id=3)
    dx, dg_mix = _matmul_nt_rms_bwd(dp, wi, xf, g_mix, tm=512, name="d_proj_in", residual=dx1)
    sib_l, dx = lax.optimization_barrier((sib_l, dx))
    sums_l = [_rs_chip_sum(g8[n], fs, place_idx, name="rs_chip_sum_" + n) for n, fs in zip(late, sib_l)]
    r3.update(zip(late, _rs_chips_async([h4 for _, h4 in sums_l], name="rs_chips_late", collective_id=4)))
    own.update({n: f for n, (f, _) in zip(late, sums_l)})

    small_part = {"g_mix": dg_mix, "g_cq": dg_cq, "g_ckv": dg_ckv, "g_sb_out": dg_sb, "g_mla_out": dg_mla,
                  "g_ffn": dg_ffn, "conv_b": _ff_deinterleave(dcb), "g_final": dg_final}
    small_all, = _all_gather([_pack_small(small_part, tail=loss_row[0, 0:1])], name="ag_small_grads")
    gsmall = _sum8(small_all, name="sum_small_grads")

    params = {"w_in": (w_in, m_w_in, v_w_in), "w_uq": (w_uq, m_w_uq, v_w_uq), "w_ukv": (w_ukv, m_w_ukv, v_w_ukv),
              "w_out": (w_out, m_w_out, v_w_out), "w_up": (w_up, m_w_up, v_w_up), "conv_w": (conv_w, m_conv_w, v_conv_w),
              "w_down": (w_down, m_w_down, v_w_down)}
    grad, delta, new_m, new_v = {}, {}, {}, {}
    for n, (w_, m_, v_) in params.items():
        g_, d_, mn_, vn_ = _adamw_rs(own[n], r3[n], w_[0], m_[0], v_[0], name="adamw_" + n)
        grad[n], delta[n], new_m[n], new_v[n] = g_[None], d_[None], mn_[None], vn_[None]
    small_w = {"g_mix": g_mix, "g_cq": g_cq, "g_ckv": g_ckv, "g_sb_out": g_sb_out, "g_mla_out": g_mla_out,
               "g_ffn": g_ffn, "conv_b": conv_b, "g_final": g_final}
    small_m = {"g_mix": m_g_mix, "g_cq": m_g_cq, "g_ckv": m_g_ckv, "g_sb_out": m_g_sb_out, "g_mla_out": m_g_mla_out,
               "g_ffn": m_g_ffn, "conv_b": m_conv_b, "g_final": m_g_final}
    small_v = {"g_mix": v_g_mix, "g_cq": v_g_cq, "g_ckv": v_g_ckv, "g_sb_out": v_g_sb_out, "g_mla_out": v_g_mla_out,
               "g_ffn": v_g_ffn, "conv_b": v_conv_b, "g_final": v_g_final}
    ds_, ms_, vs_ = _adamw(_pack_small(small_w), gsmall, _pack_small(small_m), _pack_small(small_v), name="adamw_small")
    for src, dst in ((_unpack_small(gsmall), grad), (_unpack_small(ds_), delta), (_unpack_small(ms_), new_m), (_unpack_small(vs_), new_v)):
        for n, _ in SMALL:
            dst[n] = src[n].reshape(small_w[n].shape)

    loss = gsmall.reshape(-1)[SMALL_USED]
    order = ("g_mix", "w_in", "g_cq", "w_uq", "g_ckv", "w_ukv", "g_sb_out", "g_mla_out", "w_out", "g_ffn", "w_up",
             "conv_w", "conv_b", "w_down", "g_final")
    return (loss, dx.reshape(B, S, D), *[grad[n] for n in order], *[delta[n] for n in order],
            *[new_m[n] for n in order], *[new_v[n] for n in order])
```

```python
import jax
import jax.numpy as jnp
from jax import lax
from jax.experimental import pallas as pl
from jax.experimental.pallas import tpu as pltpu
from jax.experimental.pallas import tpu_sc as plsc

F32 = jnp.float32
BF16 = jnp.bfloat16

D_MODEL = 1024
SB_HEADS = 8
SB_HEAD_DIM = 64
MLA_HEADS = 8
MLA_NOPE = 64
MLA_ROPE = 32
MLA_V = 64
Q_LORA = 384
KV_LORA = 256
D_FF = 2816
ROPE_BASE = 10000.0
EPS = 1e-6
SB_W = SB_HEADS * SB_HEAD_DIM
MLA_W = MLA_HEADS * MLA_V
MLA_QK = MLA_NOPE + MLA_ROPE
IN_COLS = 3 * SB_W + Q_LORA + KV_LORA + MLA_ROPE

ADAM_LR = 0.001
ADAM_B1 = 0.9
ADAM_B2 = 0.999
ADAM_EPS = 1e-08
ADAM_WD = 0.01
ADAM_STEP = 10

N_DEV = 8
MESH_AXES = ("x", "y", "c")
LANES = 128
V7X_VMEM_LIMIT = 56 * 1024 * 1024
FF_BLK = 256
N_FF_BLK = D_FF // FF_BLK

P_Q, P_K, P_V = 0, SB_W, 2 * SB_W
P_CKV = 3 * SB_W
P_KRT = P_CKV + KV_LORA
P_CQ = P_KRT + LANES
P_COLS = P_CQ + Q_LORA

MESH = pl.DeviceIdType.MESH
ANY = pl.BlockSpec(memory_space=pl.ANY)


def _cparams(sem=None, vmem=V7X_VMEM_LIMIT):
    return pltpu.CompilerParams(dimension_semantics=sem, vmem_limit_bytes=vmem)


def _matmul_nn(a, b, *, tm, tn, out_dtype, name, residual=None):
    M, K = a.shape
    N = b.shape[1]
    assert M % tm == 0 and N % tn == 0, (name, a.shape, b.shape)
    in_specs = [pl.BlockSpec((tm, K), lambda i, j: (i, 0)), pl.BlockSpec((K, tn), lambda i, j: (0, j))]
    args = [a, b]
    if residual is not None:
        in_specs.append(pl.BlockSpec((tm, tn), lambda i, j: (i, j)))
        args.append(residual)

    def body(*refs):
        a_ref, b_ref = refs[0], refs[1]
        o_ref = refs[-1]
        acc = jnp.dot(a_ref[...].astype(BF16), b_ref[...], preferred_element_type=F32)
        if residual is not None:
            acc = acc + refs[2][...]
        o_ref[...] = acc.astype(out_dtype)

    return pl.pallas_call(
        body, name=name, grid=(M // tm, N // tn), in_specs=in_specs,
        out_specs=pl.BlockSpec((tm, tn), lambda i, j: (i, j)),
        out_shape=jax.ShapeDtypeStruct((M, N), out_dtype),
        compiler_params=_cparams(("parallel", "parallel")),
    )(*args)


def _matmul_nt(a, b, *, tm, tn, out_dtype, name):
    M, K = a.shape
    N = b.shape[0]
    assert M % tm == 0 and N % tn == 0, (name, a.shape, b.shape)

    def body(a_ref, b_ref, o_ref):
        acc = lax.dot_general(a_ref[...].astype(BF16), b_ref[...], (((1,), (1,)), ((), ())),
                              preferred_element_type=F32)
        o_ref[...] = acc.astype(out_dtype)

    return pl.pallas_call(
        body, name=name, grid=(M // tm, N // tn),
        in_specs=[pl.BlockSpec((tm, K), lambda i, j: (i, 0)), pl.BlockSpec((tn, K), lambda i, j: (j, 0))],
        out_specs=pl.BlockSpec((tm, tn), lambda i, j: (i, j)),
        out_shape=jax.ShapeDtypeStruct((M, N), out_dtype),
        compiler_params=_cparams(("parallel", "parallel")),
    )(a, b)


def _matmul_tn(a, b, *, tm, tn, tk, name):
    K, M = a.shape
    N = b.shape[1]
    assert M % tm == 0 and N % tn == 0 and K % tk == 0, (name, a.shape, b.shape)

    def body(a_ref, b_ref, o_ref):
        k = pl.program_id(2)
        part = lax.dot_general(a_ref[...].astype(BF16), b_ref[...].astype(BF16), (((0,), (0,)), ((), ())),
                               preferred_element_type=F32)

        @pl.when(k == 0)
        def _():
            o_ref[...] = part

        @pl.when(k > 0)
        def _():
            o_ref[...] += part

    return pl.pallas_call(
        body, name=name, grid=(M // tm, N // tn, K // tk),
        in_specs=[pl.BlockSpec((tk, tm), lambda i, j, k: (k, i)), pl.BlockSpec((tk, tn), lambda i, j, k: (k, j))],
        out_specs=pl.BlockSpec((tm, tn), lambda i, j, k: (i, j)),
        out_shape=jax.ShapeDtypeStruct((M, N), F32),
        compiler_params=_cparams(("parallel", "parallel", "arbitrary")),
    )(a, b)


def _rms(xf, g):
    r = lax.rsqrt(jnp.mean(xf * xf, axis=1, keepdims=True) + EPS)
    return (xf * r) * g


def _rms_grad(dyf, xf, g):
    r = lax.rsqrt(jnp.mean(xf * xf, axis=1, keepdims=True) + EPS)
    xh = xf * r
    dyg = dyf * g
    dx = r * (dyg - xh * jnp.mean(dyg * xh, axis=1, keepdims=True))
    return dx, jnp.sum(dyf * xh, axis=0, keepdims=True)


def _accumulate(ref, part):
    @pl.when(pl.program_id(0) == 0)
    def _():
        ref[...] = part

    @pl.when(pl.program_id(0) > 0)
    def _():
        ref[...] += part


def _rms_fwd(x, g, *, tm, name, col_block=0):
    T = x.shape[0]
    C = g.shape[1]

    def body(x_ref, g_ref, o_ref):
        o_ref[...] = _rms(x_ref[...], g_ref[...]).astype(BF16)

    return pl.pallas_call(
        body, name=name, grid=(T // tm,),
        in_specs=[pl.BlockSpec((tm, C), lambda i: (i, col_block)), pl.BlockSpec((1, C), lambda i: (0, 0))],
        out_specs=pl.BlockSpec((tm, C), lambda i: (i, 0)),
        out_shape=jax.ShapeDtypeStruct((T, C), BF16),
        compiler_params=_cparams(("parallel",)),
    )(x, g)


def _rms_bwd(dy, x, g, *, tm, name, residual=None, col_block=0, out_dtype=F32):
    T = dy.shape[0]
    C = g.shape[1]
    in_specs = [pl.BlockSpec((tm, C), lambda i: (i, 0)), pl.BlockSpec((tm, C), lambda i: (i, col_block)),
                pl.BlockSpec((1, C), lambda i: (0, 0))]
    args = [dy, x, g]
    if residual is not None:
        in_specs.append(pl.BlockSpec((tm, C), lambda i: (i, 0)))
        args.append(residual)

    def body(*refs):
        dy_ref, x_ref, g_ref = refs[:3]
        dx_ref, dg_ref = refs[-2:]
        dx, part = _rms_grad(dy_ref[...].astype(F32), x_ref[...], g_ref[...])
        if residual is not None:
            dx = dx + refs[3][...]
        dx_ref[...] = dx.astype(out_dtype)
        _accumulate(dg_ref, part)

    return pl.pallas_call(
        body, name=name, grid=(T // tm,), in_specs=in_specs,
        out_specs=[pl.BlockSpec((tm, C), lambda i: (i, 0)), pl.BlockSpec((1, C), lambda i: (0, 0))],
        out_shape=[jax.ShapeDtypeStruct((T, C), out_dtype), jax.ShapeDtypeStruct((1, C), F32)],
        compiler_params=_cparams(("arbitrary",)),
    )(*args)


def _rms_matmul_nn(x, g, w, *, tm, name, col_block=0, out_dtype=F32):
    T = x.shape[0]
    C, N = w.shape
    assert T % tm == 0, (name, x.shape)

    def body(x_ref, g_ref, w_ref, h_ref, o_ref):
        hb = _rms(x_ref[...], g_ref[...]).astype(BF16)
        h_ref[...] = hb
        o_ref[...] = jnp.dot(hb, w_ref[...], preferred_element_type=F32).astype(out_dtype)

    return pl.pallas_call(
        body, name=name, grid=(T // tm,),
        in_specs=[pl.BlockSpec((tm, C), lambda i: (i, col_block)), pl.BlockSpec((1, C), lambda i: (0, 0)),
                  pl.BlockSpec((C, N), lambda i: (0, 0))],
        out_specs=[pl.BlockSpec((tm, C), lambda i: (i, 0)), pl.BlockSpec((tm, N), lambda i: (i, 0))],
        out_shape=[jax.ShapeDtypeStruct((T, C), BF16), jax.ShapeDtypeStruct((T, N), out_dtype)],
        compiler_params=_cparams(("parallel",)),
    )(x, g, w)


def _matmul_nt_rms_bwd(a, b, x, g, *, tm, name, residual=None, col_block=0, out_dtype=F32):
    M, K = a.shape
    C = b.shape[0]
    assert M % tm == 0, (name, a.shape)
    in_specs = [pl.BlockSpec((tm, K), lambda i: (i, 0)), pl.BlockSpec((C, K), lambda i: (0, 0)),
                pl.BlockSpec((tm, C), lambda i: (i, col_block)), pl.BlockSpec((1, C), lambda i: (0, 0))]
    args = [a, b, x, g]
    if residual is not None:
        in_specs.append(pl.BlockSpec((tm, C), lambda i: (i, 0)))
        args.append(residual)

    def body(*refs):
        a_ref, b_ref, x_ref, g_ref = refs[:4]
        dx_ref, dg_ref = refs[-2:]
        dy = lax.dot_general(a_ref[...].astype(BF16), b_ref[...], (((1,), (1,)), ((), ())), preferred_element_type=F32)
        dx, part = _rms_grad(dy, x_ref[...], g_ref[...])
        if residual is not None:
            dx = dx + refs[4][...]
        dx_ref[...] = dx.astype(out_dtype)
        _accumulate(dg_ref, part)

    return pl.pallas_call(
        body, name=name, grid=(M // tm,), in_specs=in_specs,
        out_specs=[pl.BlockSpec((tm, C), lambda i: (i, 0)), pl.BlockSpec((1, C), lambda i: (0, 0))],
        out_shape=[jax.ShapeDtypeStruct((M, C), out_dtype), jax.ShapeDtypeStruct((1, C), F32)],
        compiler_params=_cparams(("arbitrary",)),
    )(*args)


def _matmul_nn_loss(a, w, x1, g, tgt, *, tm, name):
    M, K = a.shape
    C = w.shape[1]
    assert M % tm == 0, (name, a.shape)

    def body(a_ref, w_ref, x_ref, g_ref, t_ref, dx_ref, dg_ref, loss_ref):
        xf = x_ref[...] + jnp.dot(a_ref[...], w_ref[...], preferred_element_type=F32)
        gf = g_ref[...]
        err = _rms(xf, gf) - t_ref[...]
        lpart = 0.5 * jnp.sum(jnp.mean(err * err, axis=1, keepdims=True), axis=0, keepdims=True)
        dx, gpart = _rms_grad(err * (1.0 / C), xf, gf)
        dx_ref[...] = dx
        _accumulate(dg_ref, gpart)
        _accumulate(loss_ref, jnp.broadcast_to(lpart, (1, LANES)))

    row = pl.BlockSpec((tm, C), lambda i: (i, 0))
    return pl.pallas_call(
        body, name=name, grid=(M // tm,),
        in_specs=[pl.BlockSpec((tm, K), lambda i: (i, 0)), pl.BlockSpec((K, C), lambda i: (0, 0)), row,
                  pl.BlockSpec((1, C), lambda i: (0, 0)), row],
        out_specs=[row, pl.BlockSpec((1, C), lambda i: (0, 0)), pl.BlockSpec((1, LANES), lambda i: (0, 0))],
        out_shape=[jax.ShapeDtypeStruct((M, C), F32), jax.ShapeDtypeStruct((1, C), F32),
                   jax.ShapeDtypeStruct((1, LANES), F32)],
        compiler_params=_cparams(("arbitrary",)),
    )(a, w, x1, g, tgt)


def _rms2_fwd(xa, xb, ga, gb, *, tm, name):
    T, C = xa.shape

    def body(xa_ref, xb_ref, ga_ref, gb_ref, o_ref):
        o_ref[:, :C] = _rms(xa_ref[...], ga_ref[...]).astype(BF16)
        o_ref[:, C:] = _rms(xb_ref[...], gb_ref[...]).astype(BF16)

    row = pl.BlockSpec((tm, C), lambda i: (i, 0))
    gsp = pl.BlockSpec((1, C), lambda i: (0, 0))
    return pl.pallas_call(
        body, name=name, grid=(T // tm,), in_specs=[row, row, gsp, gsp],
        out_specs=pl.BlockSpec((tm, 2 * C), lambda i: (i, 0)),
        out_shape=jax.ShapeDtypeStruct((T, 2 * C), BF16),
        compiler_params=_cparams(("parallel",)),
    )(xa, xb, ga, gb)


def _rms2_bwd(dy, xa, xb, ga, gb, *, tm, name):
    T, C = xa.shape

    def body(dy_ref, xa_ref, xb_ref, ga_ref, gb_ref, dxa_ref, dxb_ref, dga_ref, dgb_ref):
        dxa, pa = _rms_grad(dy_ref[:, :C], xa_ref[...], ga_ref[...])
        dxb, pb = _rms_grad(dy_ref[:, C:], xb_ref[...], gb_ref[...])
        dxa_ref[...] = dxa
        dxb_ref[...] = dxb
        _accumulate(dga_ref, pa)
        _accumulate(dgb_ref, pb)

    row = pl.BlockSpec((tm, C), lambda i: (i, 0))
    gsp = pl.BlockSpec((1, C), lambda i: (0, 0))
    return pl.pallas_call(
        body, name=name, grid=(T // tm,),
        in_specs=[pl.BlockSpec((tm, 2 * C), lambda i: (i, 0)), row, row, gsp, gsp],
        out_specs=[row, row, gsp, gsp],
        out_shape=[jax.ShapeDtypeStruct((T, C), F32), jax.ShapeDtypeStruct((T, C), F32),
                   jax.ShapeDtypeStruct((1, C), F32), jax.ShapeDtypeStruct((1, C), F32)],
        compiler_params=_cparams(("arbitrary",)),
    )(dy, xa, xb, ga, gb)


def _final_loss(x2, g, tgt, *, tm, name):
    T, C = x2.shape

    def body(x_ref, g_ref, t_ref, dx_ref, dg_ref, loss_ref):
        xf = x_ref[...]
        gf = g_ref[...]
        err = _rms(xf, gf) - t_ref[...]
        lpart = 0.5 * jnp.sum(jnp.mean(err * err, axis=1, keepdims=True), axis=0, keepdims=True)
        dx, gpart = _rms_grad(err * (1.0 / C), xf, gf)
        dx_ref[...] = dx
        _accumulate(dg_ref, gpart)
        _accumulate(loss_ref, jnp.broadcast_to(lpart, (1, LANES)))

    return pl.pallas_call(
        body, name=name, grid=(T // tm,),
        in_specs=[pl.BlockSpec((tm, C), lambda i: (i, 0)), pl.BlockSpec((1, C), lambda i: (0, 0)),
                  pl.BlockSpec((tm, C), lambda i: (i, 0))],
        out_specs=[pl.BlockSpec((tm, C), lambda i: (i, 0)), pl.BlockSpec((1, C), lambda i: (0, 0)),
                   pl.BlockSpec((1, LANES), lambda i: (0, 0))],
        out_shape=[jax.ShapeDtypeStruct((T, C), F32), jax.ShapeDtypeStruct((1, C), F32),
                   jax.ShapeDtypeStruct((1, LANES), F32)],
        compiler_params=_cparams(("arbitrary",)),
    )(x2, g, tgt)


ATT_T = 256
ATT_PAIRS = 2
NEG_BIG = -1e30


def _lane_iota():
    return lax.broadcasted_iota(jnp.int32, (1, LANES), 1)


def _head_masks():
    first = _lane_iota() < SB_HEAD_DIM
    return first, jnp.logical_not(first)


def _pick(mask, x):
    return jnp.where(mask, x, jnp.zeros_like(x))


def _lane_value(t, lane):
    return jnp.sum(jnp.where(_lane_iota() == lane, t, 0.0), axis=1, keepdims=True)


def _split_hi_lo(x):
    hi = x.astype(BF16)
    lo = (x - hi.astype(F32)).astype(BF16)
    return jnp.concatenate([hi, lo], axis=1)


def _tri(n, kind):
    r = lax.broadcasted_iota(jnp.int32, (n, n), 0)
    c = lax.broadcasted_iota(jnp.int32, (n, n), 1)
    u = {"suffix_excl": r > c, "prefix_incl": r <= c, "prefix_excl": r < c}[kind].astype(BF16)
    return jnp.concatenate([u, u], axis=0)


def _dot_nt(a, b):
    return lax.dot_general(a, b, (((1,), (1,)), ((), ())), preferred_element_type=F32)


def _dot_tn(a, b):
    return lax.dot_general(a, b, (((0,), (0,)), ((), ())), preferred_element_type=F32)


def _dot(a, b):
    return jnp.dot(a, b, preferred_element_type=F32)


def _causal_mask(n, strict):
    r = lax.broadcasted_iota(jnp.int32, (n, n), 0)
    c = lax.broadcasted_iota(jnp.int32, (n, n), 1)
    return (c < r) if strict else (c <= r)


LOG2E = 1.4426950408889634


def _sb_logs(qh, kj, vis):
    z2 = _dot_nt(qh, kj) * LOG2E
    nk = jnp.maximum(z2, 0.0) + jnp.log2(1.0 + jnp.exp2(-jnp.abs(z2)))
    lb = z2 - nk
    if vis is not None:
        nk = jnp.where(vis, nk, 0.0)
    return lb, nk


def _sb_fwd(p, *, seq, name):
    T = p.shape[0]
    B = T // seq
    TQ = ATT_T
    nq = seq // TQ
    PP = ATT_PAIRS
    W = PP * LANES
    nstep = SB_W // W
    NH = 2 * PP

    def body(q_ref, k_ref, v_ref, o_ref, lt_ref, q_s, k_s, v_s):
        masks = _head_masks()
        q = q_ref[...] * (SB_HEAD_DIM ** -0.5)
        v = v_ref[...]
        k_s[...] = k_ref[...].astype(BF16)
        for h in range(NH):
            ps = slice((h // 2) * LANES, (h // 2 + 1) * LANES)
            hs = slice(h * LANES, (h + 1) * LANES)
            q_s[:, hs] = _pick(masks[h % 2], q[:, ps]).astype(BF16)
            v_s[:, hs] = _pick(masks[h % 2], v[:, ps]).astype(BF16)
        u_suf = _tri(TQ, "suffix_excl")
        vis = _causal_mask(TQ, True)

        def q_block(i, carry):
            q0 = pl.multiple_of(i * TQ, TQ)
            qs = [q_s[pl.ds(q0, TQ), h * LANES:(h + 1) * LANES] for h in range(NH)]

            def tile(k0, c, mask):
                rs, accs = list(c[:NH]), list(c[NH:])
                logs = [_sb_logs(qs[h], k_s[pl.ds(k0, TQ), (h // 2) * LANES:(h // 2 + 1) * LANES], mask) for h in range(NH)]
                sums = [_dot(_split_hi_lo(nk), u_suf) for _, nk in logs]
                for h in range(NH):
                    a = jnp.exp2(logs[h][0] - sums[h] - rs[h])
                    if mask is not None:
                        a = jnp.where(mask, a, 0.0)
                    accs[h // 2] = accs[h // 2] + _dot(a.astype(BF16), v_s[pl.ds(k0, TQ), h * LANES:(h + 1) * LANES])
                    rs[h] = rs[h] + jnp.sum(logs[h][1], axis=1, keepdims=True)
                return tuple(rs) + tuple(accs)

            zero = jnp.zeros((TQ, 1), F32)
            c = tile(q0, (zero,) * NH + (jnp.zeros((TQ, LANES), F32),) * PP, vis)

            def k_block(jj, c):
                return tile(pl.multiple_of((i - 1 - jj) * TQ, TQ), c, None)

            c = lax.fori_loop(0, i, k_block, c)
            for pr in range(PP):
                ps = slice(pr * LANES, (pr + 1) * LANES)
                o_ref[pl.ds(q0, TQ), ps] = c[NH + pr]
                lt_ref[pl.ds(q0, TQ), ps] = jnp.where(masks[0], c[2 * pr], c[2 * pr + 1])
            return carry

        lax.fori_loop(0, nq, q_block, 0)

    blk = lambda off: pl.BlockSpec((seq, W), lambda b, g: (b, off + g))
    out_blk = pl.BlockSpec((seq, W), lambda b, g: (b, g))
    return pl.pallas_call(
        body, name=name, grid=(B, nstep),
        in_specs=[blk(P_Q // W), blk(P_K // W), blk(P_V // W)],
        out_specs=[out_blk, out_blk],
        out_shape=[jax.ShapeDtypeStruct((T, SB_W), F32), jax.ShapeDtypeStruct((T, SB_W), F32)],
        scratch_shapes=[pltpu.VMEM((seq, NH * LANES), BF16), pltpu.VMEM((seq, W), BF16), pltpu.VMEM((seq, NH * LANES), BF16)],
        compiler_params=_cparams(("parallel", "parallel")),
    )(p, p, p)


def _sb_bwd(p, ltot, do, *, seq, name):
    T = p.shape[0]
    B = T // seq
    TQ = ATT_T
    nq = seq // TQ
    PP = ATT_PAIRS
    W = PP * LANES
    nstep = SB_W // W
    NH = 2 * PP
    scale = SB_HEAD_DIM ** -0.5

    def body(q_ref, k_ref, v_ref, lt_ref, do_ref, dq_ref, dk_ref, dv_ref, q_s, k_s, v_s, do_s, dk_s, dv_s):
        masks = _head_masks()
        q = q_ref[...] * scale
        dof = do_ref[...]
        k_s[...] = k_ref[...].astype(BF16)
        v_s[...] = v_ref[...].astype(BF16)
        for h in range(NH):
            ps = slice((h // 2) * LANES, (h // 2 + 1) * LANES)
            hs = slice(h * LANES, (h + 1) * LANES)
            q_s[:, hs] = _pick(masks[h % 2], q[:, ps]).astype(BF16)
            do_s[:, hs] = _pick(masks[h % 2], dof[:, ps]).astype(BF16)
        dk_s[...] = jnp.zeros_like(dk_s)
        dv_s[...] = jnp.zeros_like(dv_s)
        u_pin = _tri(TQ, "prefix_incl")
        u_pex = _tri(TQ, "prefix_excl")[:TQ]
        vis = _causal_mask(TQ, True)

        def q_block(i, carry):
            q0 = pl.multiple_of(i * TQ, TQ)
            qs = [q_s[pl.ds(q0, TQ), h * LANES:(h + 1) * LANES] for h in range(NH)]
            dos = [do_s[pl.ds(q0, TQ), h * LANES:(h + 1) * LANES] for h in range(NH)]
            lt = lt_ref[pl.ds(q0, TQ), :]
            lts = [_lane_value(lt[:, (h // 2) * LANES:(h // 2 + 1) * LANES], (h % 2) * SB_HEAD_DIM) for h in range(NH)]

            def tile(k0, c, mask):
                cs, gs, accs = list(c[:NH]), list(c[NH:2 * NH]), list(c[2 * NH:])
                kjs = [k_s[pl.ds(k0, TQ), pr * LANES:(pr + 1) * LANES] for pr in range(PP)]
                vjs = [v_s[pl.ds(k0, TQ), pr * LANES:(pr + 1) * LANES] for pr in range(PP)]
                logs = [_sb_logs(qs[h], kjs[h // 2], mask) for h in range(NH)]
                pins = [_dot(_split_hi_lo(nk), u_pin) for _, nk in logs]
                das = [_dot_nt(dos[h], vjs[h // 2]) for h in range(NH)]
                a_l, g_l = [], []
                for h in range(NH):
                    a = jnp.exp2(logs[h][0] - ((lts[h] - cs[h]) - pins[h]))
                    if mask is not None:
                        a = jnp.where(mask, a, 0.0)
                    a_l.append(a)
                    g_l.append(das[h] * a)
                pres = [_dot(g.astype(BF16), u_pex) for g in g_l]
                dz_l = []
                for h in range(NH):
                    dz = g_l[h] - jnp.exp2(logs[h][0]) * (g_l[h] + (pres[h] + gs[h]))
                    if mask is not None:
                        dz = jnp.where(mask, dz, 0.0)
                    dz_l.append(dz.astype(BF16))
                for h in range(NH):
                    accs[h] = accs[h] + _dot(dz_l[h], kjs[h // 2])
                for pr in range(PP):
                    ps = slice(pr * LANES, (pr + 1) * LANES)
                    ha, hb = 2 * pr, 2 * pr + 1
                    dk_s[pl.ds(k0, TQ), ps] += _dot_tn(dz_l[ha], qs[ha]) + _dot_tn(dz_l[hb], qs[hb])
                    dv_s[pl.ds(k0, TQ), ps] += _dot_tn(a_l[ha].astype(BF16), dos[ha]) + _dot_tn(a_l[hb].astype(BF16), dos[hb])
                for h in range(NH):
                    cs[h] = cs[h] + jnp.sum(logs[h][1], axis=1, keepdims=True)
                    gs[h] = gs[h] + jnp.sum(g_l[h], axis=1, keepdims=True)
                return tuple(cs) + tuple(gs) + tuple(accs)

            z1 = jnp.zeros((TQ, 1), F32)
            zl = jnp.zeros((TQ, LANES), F32)

            def k_block(j, c):
                return tile(pl.multiple_of(j * TQ, TQ), c, None)

            c = lax.fori_loop(0, i, k_block, (z1,) * (2 * NH) + (zl,) * NH)
            c = tile(q0, c, vis)
            for pr in range(PP):
                dq = jnp.where(masks[0], c[2 * NH + 2 * pr], c[2 * NH + 2 * pr + 1]) * scale
                dq_ref[pl.ds(q0, TQ), pr * LANES:(pr + 1) * LANES] = dq.astype(BF16)
            return carry

        lax.fori_loop(0, nq, q_block, 0)
        dk_ref[...] = dk_s[...].astype(BF16)
        dv_ref[...] = dv_s[...].astype(BF16)

    blk = lambda off: pl.BlockSpec((seq, W), lambda b, g: (b, off + g))
    out_blk = pl.BlockSpec((seq, W), lambda b, g: (b, g))
    return pl.pallas_call(
        body, name=name, grid=(B, nstep),
        in_specs=[blk(P_Q // W), blk(P_K // W), blk(P_V // W), out_blk, out_blk],
        out_specs=[out_blk, out_blk, out_blk],
        out_shape=[jax.ShapeDtypeStruct((T, SB_W), BF16) for _ in range(3)],
        scratch_shapes=[pltpu.VMEM((seq, NH * LANES), BF16), pltpu.VMEM((seq, W), BF16), pltpu.VMEM((seq, W), BF16),
                        pltpu.VMEM((seq, NH * LANES), BF16), pltpu.VMEM((seq, W), F32), pltpu.VMEM((seq, W), F32)],
        compiler_params=_cparams(("parallel", "parallel")),
    )(p, p, p, ltot, do)


def _mla_masks():
    lane = lax.broadcasted_iota(jnp.int32, (1, 2 * LANES), 1)
    ma = (lane < MLA_NOPE) | ((lane >= LANES) & (lane < LANES + MLA_ROPE))
    mb = ((lane >= MLA_NOPE) & (lane < LANES)) | ((lane >= LANES + MLA_ROPE) & (lane < LANES + 2 * MLA_ROPE))
    return ma, mb


def _mla_fwd(qm, kvm, krt, *, seq, name):
    T = qm.shape[0]
    B = T // seq
    TQ = ATT_T
    nq = seq // TQ
    PP = ATT_PAIRS
    W = PP * LANES
    nstep = MLA_W // W
    NH = 2 * PP
    CW = 2 * LANES
    scale = MLA_QK ** -0.5

    def body(qn_ref, qr_ref, kn_ref, v_ref, kr_ref, o_ref, lse_ref, q_s, kc_s, v_s):
        hm = _head_masks()
        mm = _mla_masks()
        v = v_ref[...]
        for pr in range(PP):
            ps = slice(pr * LANES, (pr + 1) * LANES)
            qc = jnp.concatenate([qn_ref[:, ps], qr_ref[:, ps]], axis=1)
            kc_s[:, pr * CW:(pr + 1) * CW] = jnp.concatenate([kn_ref[:, ps], kr_ref[...]], axis=1)
            for e in range(2):
                h = 2 * pr + e
                q_s[:, h * CW:(h + 1) * CW] = _pick(mm[e], qc)
                v_s[:, h * LANES:(h + 1) * LANES] = _pick(hm[e], v[:, ps])
        vis = _causal_mask(TQ, False)

        def q_block(i, carry):
            q0 = pl.multiple_of(i * TQ, TQ)
            qs = [q_s[pl.ds(q0, TQ), h * CW:(h + 1) * CW] for h in range(NH)]

            def tile(k0, c, mask):
                ms, ls, accs = list(c[:NH]), list(c[NH:2 * NH]), list(c[2 * NH:])
                ss = [_dot_nt(qs[h], kc_s[pl.ds(k0, TQ), (h // 2) * CW:(h // 2 + 1) * CW]) * scale for h in range(NH)]
                if mask is not None:
                    ss = [jnp.where(mask, s, NEG_BIG) for s in ss]
                m_new = [jnp.maximum(ms[h], jnp.max(ss[h], axis=1, keepdims=True)) for h in range(NH)]
                alphas = [jnp.exp(ms[h] - m_new[h]) for h in range(NH)]
                prs = [jnp.exp(ss[h] - m_new[h]) for h in range(NH)]
                outs = [_dot(prs[h].astype(BF16), v_s[pl.ds(k0, TQ), h * LANES:(h + 1) * LANES]) for h in range(NH)]
                ls = [alphas[h] * ls[h] + jnp.sum(prs[h], axis=1, keepdims=True) for h in range(NH)]
                for pr in range(PP):
                    accs[pr] = accs[pr] * jnp.where(hm[0], alphas[2 * pr], alphas[2 * pr + 1]) + outs[2 * pr] + outs[2 * pr + 1]
                return tuple(m_new) + tuple(ls) + tuple(accs)

            neg = jnp.full((TQ, 1), NEG_BIG, F32)
            z1 = jnp.zeros((TQ, 1), F32)

            def k_block(j, c):
                return tile(pl.multiple_of(j * TQ, TQ), c, None)

            c = lax.fori_loop(0, i, k_block, (neg,) * NH + (z1,) * NH + (jnp.zeros((TQ, LANES), F32),) * PP)
            c = tile(q0, c, vis)
            for pr in range(PP):
                ps = slice(pr * LANES, (pr + 1) * LANES)
                m_a, m_b, l_a, l_b = c[2 * pr], c[2 * pr + 1], c[NH + 2 * pr], c[NH + 2 * pr + 1]
                o_ref[pl.ds(q0, TQ), ps] = c[2 * NH + pr] / jnp.where(hm[0], l_a, l_b)
                lse_ref[pl.ds(q0, TQ), ps] = jnp.where(hm[0], m_a + jnp.log(l_a), m_b + jnp.log(l_b))
            return carry

        lax.fori_loop(0, nq, q_block, 0)

    blk = lambda off: pl.BlockSpec((seq, W), lambda b, g: (b, off + g))
    out_blk = pl.BlockSpec((seq, W), lambda b, g: (b, g))
    return pl.pallas_call(
        body, name=name, grid=(B, nstep),
        in_specs=[blk(0), blk(nstep), blk(0), blk(nstep), pl.BlockSpec((seq, LANES), lambda b, g: (b, 0))],
        out_specs=[out_blk, out_blk],
        out_shape=[jax.ShapeDtypeStruct((T, MLA_W), F32), jax.ShapeDtypeStruct((T, MLA_W), F32)],
        scratch_shapes=[pltpu.VMEM((seq, NH * CW), BF16), pltpu.VMEM((seq, PP * CW), BF16), pltpu.VMEM((seq, NH * LANES), BF16)],
        compiler_params=_cparams(("parallel", "parallel")),
    )(qm, qm, kvm, kvm, krt)


def _mla_bwd(qm, kvm, krt, o, lse, do, *, seq, name):
    T = qm.shape[0]
    B = T // seq
    TQ = ATT_T
    nq = seq // TQ
    PP = ATT_PAIRS
    W = PP * LANES
    nstep = MLA_W // W
    NH = 2 * PP
    CW = 2 * LANES
    scale = MLA_QK ** -0.5

    def body(qn_ref, qr_ref, kn_ref, v_ref, kr_ref, o_ref, lse_ref, do_ref,
             dqn_ref, dqr_ref, dkn_ref, dv_ref, dkr_ref, q_s, kc_s, do_s, dkc_s, dv_s):
        hm = _head_masks()
        mm = _mla_masks()
        dof = do_ref[...]
        for pr in range(PP):
            ps = slice(pr * LANES, (pr + 1) * LANES)
            qc = jnp.concatenate([qn_ref[:, ps], qr_ref[:, ps]], axis=1)
            kc_s[:, pr * CW:(pr + 1) * CW] = jnp.concatenate([kn_ref[:, ps], kr_ref[...]], axis=1)
            for e in range(2):
                h = 2 * pr + e
                q_s[:, h * CW:(h + 1) * CW] = _pick(mm[e], qc)
                do_s[:, h * LANES:(h + 1) * LANES] = _pick(hm[e], dof[:, ps]).astype(BF16)
        dkc_s[...] = jnp.zeros_like(dkc_s)
        dv_s[...] = jnp.zeros_like(dv_s)
        vis = _causal_mask(TQ, False)

        def q_block(i, carry):
            q0 = pl.multiple_of(i * TQ, TQ)
            qs = [q_s[pl.ds(q0, TQ), h * CW:(h + 1) * CW] for h in range(NH)]
            dos = [do_s[pl.ds(q0, TQ), h * LANES:(h + 1) * LANES] for h in range(NH)]
            lse_t = lse_ref[pl.ds(q0, TQ), :]
            dd = do_ref[pl.ds(q0, TQ), :] * o_ref[pl.ds(q0, TQ), :]
            lses, ds_ = [], []
            for h in range(NH):
                ps = slice((h // 2) * LANES, (h // 2 + 1) * LANES)
                lses.append(_lane_value(lse_t[:, ps], (h % 2) * MLA_V))
                ds_.append(jnp.sum(_pick(hm[h % 2], dd[:, ps]), axis=1, keepdims=True))

            def tile(k0, c, mask):
                accs = list(c)
                kcs = [kc_s[pl.ds(k0, TQ), pr * CW:(pr + 1) * CW] for pr in range(PP)]
                vjs = [v_ref[pl.ds(k0, TQ), pr * LANES:(pr + 1) * LANES] for pr in range(PP)]
                ss = [_dot_nt(qs[h], kcs[h // 2]) * scale for h in range(NH)]
                dps = [_dot_nt(dos[h], vjs[h // 2]) for h in range(NH)]
                p_l, ds_l = [], []
                for h in range(NH):
                    pr_ = jnp.exp(ss[h] - lses[h])
                    if mask is not None:
                        pr_ = jnp.where(mask, pr_, 0.0)
                    p_l.append(pr_.astype(BF16))
                    ds_l.append((pr_ * (dps[h] - ds_[h]) * scale).astype(BF16))
                for h in range(NH):
                    accs[h] = accs[h] + _dot(ds_l[h], kcs[h // 2])
                for pr in range(PP):
                    ha, hb = 2 * pr, 2 * pr + 1
                    dkc_s[pl.ds(k0, TQ), pr * CW:(pr + 1) * CW] += _dot_tn(ds_l[ha], qs[ha]) + _dot_tn(ds_l[hb], qs[hb])
                    dv_s[pl.ds(k0, TQ), pr * LANES:(pr + 1) * LANES] += _dot_tn(p_l[ha], dos[ha]) + _dot_tn(p_l[hb], dos[hb])
                return tuple(accs)

            zc = jnp.zeros((TQ, CW), F32)

            def k_block(j, c):
                return tile(pl.multiple_of(j * TQ, TQ), c, None)

            c = lax.fori_loop(0, i, k_block, (zc,) * NH)
            c = tile(q0, c, vis)
            for pr in range(PP):
                ps = slice(pr * LANES, (pr + 1) * LANES)
                dq = _pick(mm[0], c[2 * pr]) + _pick(mm[1], c[2 * pr + 1])
                dqn_ref[pl.ds(q0, TQ), ps] = dq[:, :LANES].astype(BF16)
                dqr_ref[pl.ds(q0, TQ), ps] = dq[:, LANES:]
            return carry

        lax.fori_loop(0, nq, q_block, 0)
        dkr = dkc_s[:, LANES:CW]
        for pr in range(PP):
            dkn_ref[:, pr * LANES:(pr + 1) * LANES] = dkc_s[:, pr * CW:pr * CW + LANES].astype(BF16)
            if pr > 0:
                dkr = dkr + dkc_s[:, pr * CW + LANES:(pr + 1) * CW]
        dv_ref[...] = dv_s[...].astype(BF16)
        g = pl.program_id(1)

        @pl.when(g == 0)
        def _():
            dkr_ref[...] = dkr

        @pl.when(g > 0)
        def _():
            dkr_ref[...] += dkr

    blk = lambda off: pl.BlockSpec((seq, W), lambda b, g: (b, off + g))
    out_blk = pl.BlockSpec((seq, W), lambda b, g: (b, g))
    one_blk = pl.BlockSpec((seq, LANES), lambda b, g: (b, 0))
    return pl.pallas_call(
        body, name=name, grid=(B, nstep),
        in_specs=[blk(0), blk(nstep), blk(0), blk(nstep), one_blk, out_blk, out_blk, out_blk],
        out_specs=[out_blk, out_blk, out_blk, out_blk, one_blk],
        out_shape=[jax.ShapeDtypeStruct((T, MLA_W), BF16), jax.ShapeDtypeStruct((T, MLA_W), F32),
                   jax.ShapeDtypeStruct((T, MLA_W), BF16), jax.ShapeDtypeStruct((T, MLA_W), BF16),
                   jax.ShapeDtypeStruct((T, LANES), F32)],
        scratch_shapes=[pltpu.VMEM((seq, NH * CW), BF16), pltpu.VMEM((seq, PP * CW), BF16), pltpu.VMEM((seq, NH * LANES), BF16),
                        pltpu.VMEM((seq, PP * CW), F32), pltpu.VMEM((seq, W), F32)],
        compiler_params=_cparams(("parallel", "arbitrary")),
    )(qm, qm, kvm, kvm, krt, o, lse, do)


def _rope_tables(pos_ref, invf_ref):
    ang = pos_ref[...].astype(F32) * invf_ref[...]
    first = (_lane_iota() % MLA_ROPE) < (MLA_ROPE // 2)
    return jnp.cos(ang), jnp.sin(ang), first


def _rope_apply(x, cos, sin, first):
    rot = jnp.where(first, -pltpu.roll(x, LANES - MLA_ROPE // 2, 1), pltpu.roll(x, MLA_ROPE // 2, 1))
    return x * cos + rot * sin


def _rope_apply_t(dy, cos, sin, first):
    dys = dy * sin
    rot_t = jnp.where(first, pltpu.roll(dys, LANES - MLA_ROPE // 2, 1), -pltpu.roll(dys, MLA_ROPE // 2, 1))
    return dy * cos + rot_t


def _rope_fwd(qfull, p, pos, invf, *, tm, name):
    T = qfull.shape[0]
    ntile = MLA_W // LANES

    def body(q_ref, kr_ref, pos_ref, invf_ref, qm_ref, krt_ref):
        cos, sin, first = _rope_tables(pos_ref, invf_ref)
        qm_ref[:, :MLA_W] = q_ref[:, :MLA_W].astype(BF16)
        for t in range(ntile):
            sl = slice(MLA_W + t * LANES, MLA_W + (t + 1) * LANES)
            qm_ref[:, sl] = _rope_apply(q_ref[:, sl], cos, sin, first).astype(BF16)
        krt_ref[...] = _rope_apply(kr_ref[...], cos, sin, first).astype(BF16)

    return pl.pallas_call(
        body, name=name, grid=(T // tm,),
        in_specs=[pl.BlockSpec((tm, 2 * MLA_W), lambda i: (i, 0)), pl.BlockSpec((tm, LANES), lambda i: (i, P_KRT // LANES)),
                  pl.BlockSpec((tm, 1), lambda i: (i, 0)), pl.BlockSpec((1, LANES), lambda i: (0, 0))],
        out_specs=[pl.BlockSpec((tm, 2 * MLA_W), lambda i: (i, 0)), pl.BlockSpec((tm, LANES), lambda i: (i, 0))],
        out_shape=[jax.ShapeDtypeStruct((T, 2 * MLA_W), BF16), jax.ShapeDtypeStruct((T, LANES), BF16)],
        compiler_params=_cparams(("parallel",)),
    )(qfull, p, pos, invf)


def _proj_uq_rope(p, g, wuq, pos, invf, *, tm, name):
    T = p.shape[0]
    ntile = MLA_W // LANES

    def body(x_ref, kr_ref, g_ref, w_ref, pos_ref, invf_ref, cq_ref, qm_ref, krt_ref):
        cos, sin, first = _rope_tables(pos_ref, invf_ref)
        hb = _rms(x_ref[...], g_ref[...]).astype(BF16)
        cq_ref[...] = hb
        q = jnp.dot(hb, w_ref[...], preferred_element_type=F32)
        qm_ref[:, :MLA_W] = q[:, :MLA_W].astype(BF16)
        for t in range(ntile):
            sl = slice(MLA_W + t * LANES, MLA_W + (t + 1) * LANES)
            qm_ref[:, sl] = _rope_apply(q[:, sl], cos, sin, first).astype(BF16)
        krt_ref[...] = _rope_apply(kr_ref[...], cos, sin, first).astype(BF16)

    return pl.pallas_call(
        body, name=name, grid=(T // tm,),
        in_specs=[pl.BlockSpec((tm, Q_LORA), lambda i: (i, P_CQ // Q_LORA)), pl.BlockSpec((tm, LANES), lambda i: (i, P_KRT // LANES)),
                  pl.BlockSpec((1, Q_LORA), lambda i: (0, 0)), pl.BlockSpec((Q_LORA, 2 * MLA_W), lambda i: (0, 0)),
                  pl.BlockSpec((tm, 1), lambda i: (i, 0)), pl.BlockSpec((1, LANES), lambda i: (0, 0))],
        out_specs=[pl.BlockSpec((tm, Q_LORA), lambda i: (i, 0)), pl.BlockSpec((tm, 2 * MLA_W), lambda i: (i, 0)),
                   pl.BlockSpec((tm, LANES), lambda i: (i, 0))],
        out_shape=[jax.ShapeDtypeStruct((T, Q_LORA), BF16), jax.ShapeDtypeStruct((T, 2 * MLA_W), BF16),
                   jax.ShapeDtypeStruct((T, LANES), BF16)],
        compiler_params=_cparams(("parallel",)),
    )(p, p, g, wuq, pos, invf)


def _d_proj_uq_rope(dqn, dqr, dkr, wuq, p, g, pos, invf, *, tm, name):
    T = dqn.shape[0]
    ntile = MLA_W // LANES

    def body(dqn_ref, dqr_ref, dkr_ref, w_ref, x_ref, g_ref, pos_ref, invf_ref, dqm_ref, dx_ref, dg_ref, dkr_o_ref):
        cos, sin, first = _rope_tables(pos_ref, invf_ref)
        dqm_ref[:, :MLA_W] = dqn_ref[...]
        for t in range(ntile):
            sl = slice(t * LANES, (t + 1) * LANES)
            dqm_ref[:, MLA_W + t * LANES:MLA_W + (t + 1) * LANES] = _rope_apply_t(dqr_ref[:, sl], cos, sin, first).astype(BF16)
        dkr_o_ref[...] = _rope_apply_t(dkr_ref[...], cos, sin, first).astype(BF16)
        dy = lax.dot_general(dqm_ref[...], w_ref[...], (((1,), (1,)), ((), ())), preferred_element_type=F32)
        dx, part = _rms_grad(dy, x_ref[...], g_ref[...])
        dx_ref[...] = dx.astype(BF16)
        _accumulate(dg_ref, part)

    half = pl.BlockSpec((tm, MLA_W), lambda i: (i, 0))
    tile = pl.BlockSpec((tm, LANES), lambda i: (i, 0))
    return pl.pallas_call(
        body, name=name, grid=(T // tm,),
        in_specs=[half, half, tile, pl.BlockSpec((Q_LORA, 2 * MLA_W), lambda i: (0, 0)),
                  pl.BlockSpec((tm, Q_LORA), lambda i: (i, P_CQ // Q_LORA)), pl.BlockSpec((1, Q_LORA), lambda i: (0, 0)),
                  pl.BlockSpec((tm, 1), lambda i: (i, 0)), pl.BlockSpec((1, LANES), lambda i: (0, 0))],
        out_specs=[pl.BlockSpec((tm, 2 * MLA_W), lambda i: (i, 0)), pl.BlockSpec((tm, Q_LORA), lambda i: (i, 0)),
                   pl.BlockSpec((1, Q_LORA), lambda i: (0, 0)), tile],
        out_shape=[jax.ShapeDtypeStruct((T, 2 * MLA_W), BF16), jax.ShapeDtypeStruct((T, Q_LORA), BF16),
                   jax.ShapeDtypeStruct((1, Q_LORA), F32), jax.ShapeDtypeStruct((T, LANES), BF16)],
        compiler_params=_cparams(("arbitrary",)),
    )(dqn, dqr, dkr, wuq, p, g, pos, invf)


def _d_proj_cat(pieces, b, x, g, *, tm, name, residual=None, col_block=0, out_dtype=F32):
    M = pieces[0].shape[0]
    widths = [pc.shape[1] for pc in pieces]
    K = sum(widths)
    C = b.shape[0]
    n = len(pieces)
    in_specs = [pl.BlockSpec((tm, w), lambda i: (i, 0)) for w in widths]
    in_specs += [pl.BlockSpec((C, K), lambda i: (0, 0)), pl.BlockSpec((tm, C), lambda i: (i, col_block)),
                 pl.BlockSpec((1, C), lambda i: (0, 0))]
    args = list(pieces) + [b, x, g]
    if residual is not None:
        in_specs.append(pl.BlockSpec((tm, C), lambda i: (i, 0)))
        args.append(residual)

    def body(*refs):
        b_ref, x_ref, g_ref = refs[n:n + 3]
        cat_ref, dx_ref, dg_ref = refs[-3:]
        off = 0
        for r, w in zip(refs[:n], widths):
            cat_ref[:, off:off + w] = r[...]
            off += w
        dy = lax.dot_general(cat_ref[...], b_ref[...], (((1,), (1,)), ((), ())), preferred_element_type=F32)
        dx, part = _rms_grad(dy, x_ref[...], g_ref[...])
        if residual is not None:
            dx = dx + refs[n + 3][...]
        dx_ref[...] = dx.astype(out_dtype)
        _accumulate(dg_ref, part)

    return pl.pallas_call(
        body, name=name, grid=(M // tm,), in_specs=in_specs,
        out_specs=[pl.BlockSpec((tm, K), lambda i: (i, 0)), pl.BlockSpec((tm, C), lambda i: (i, 0)),
                   pl.BlockSpec((1, C), lambda i: (0, 0))],
        out_shape=[jax.ShapeDtypeStruct((M, K), BF16), jax.ShapeDtypeStruct((M, C), out_dtype),
                   jax.ShapeDtypeStruct((1, C), F32)],
        compiler_params=_cparams(("arbitrary",)),
    )(*args)


def _heads_out(xa, xb, ga, gb, w, resid, *, tm, name):
    T, C = xa.shape
    N = w.shape[1]

    def body(xa_ref, xb_ref, ga_ref, gb_ref, w_ref, r_ref, oc_ref, o_ref):
        oc_ref[:, :C] = _rms(xa_ref[...], ga_ref[...]).astype(BF16)
        oc_ref[:, C:] = _rms(xb_ref[...], gb_ref[...]).astype(BF16)
        o_ref[...] = r_ref[...] + jnp.dot(oc_ref[...], w_ref[...], preferred_element_type=F32)

    row = pl.BlockSpec((tm, C), lambda i: (i, 0))
    gsp = pl.BlockSpec((1, C), lambda i: (0, 0))
    full = pl.BlockSpec((tm, N), lambda i: (i, 0))
    return pl.pallas_call(
        body, name=name, grid=(T // tm,),
        in_specs=[row, row, gsp, gsp, pl.BlockSpec((2 * C, N), lambda i: (0, 0)), full],
        out_specs=[pl.BlockSpec((tm, 2 * C), lambda i: (i, 0)), full],
        out_shape=[jax.ShapeDtypeStruct((T, 2 * C), BF16), jax.ShapeDtypeStruct((T, N), F32)],
        compiler_params=_cparams(("parallel",)),
    )(xa, xb, ga, gb, w, resid)


def _heads_out_bwd(dout, w, xa, xb, ga, gb, *, tm, name):
    T, C = xa.shape
    N = w.shape[1]

    def body(d_ref, w_ref, xa_ref, xb_ref, ga_ref, gb_ref, dxa_ref, dxb_ref, dga_ref, dgb_ref):
        dy = lax.dot_general(d_ref[...].astype(BF16), w_ref[...], (((1,), (1,)), ((), ())), preferred_element_type=F32)
        dxa, pa = _rms_grad(dy[:, :C], xa_ref[...], ga_ref[...])
        dxb, pb = _rms_grad(dy[:, C:], xb_ref[...], gb_ref[...])
        dxa_ref[...] = dxa
        dxb_ref[...] = dxb
        _accumulate(dga_ref, pa)
        _accumulate(dgb_ref, pb)

    row = pl.BlockSpec((tm, C), lambda i: (i, 0))
    gsp = pl.BlockSpec((1, C), lambda i: (0, 0))
    return pl.pallas_call(
        body, name=name, grid=(T // tm,),
        in_specs=[pl.BlockSpec((tm, N), lambda i: (i, 0)), pl.BlockSpec((2 * C, N), lambda i: (0, 0)), row, row, gsp, gsp],
        out_specs=[row, row, gsp, gsp],
        out_shape=[jax.ShapeDtypeStruct((T, C), F32), jax.ShapeDtypeStruct((T, C), F32),
                   jax.ShapeDtypeStruct((1, C), F32), jax.ShapeDtypeStruct((1, C), F32)],
        compiler_params=_cparams(("arbitrary",)),
    )(dout, w, xa, xb, ga, gb)


def _rope_bwd(dqr, dkr, pos, invf, *, tm, name):
    T = dqr.shape[0]
    ntile = MLA_W // LANES

    def body(dq_ref, dk_ref, pos_ref, invf_ref, oq_ref, ok_ref):
        cos, sin, first = _rope_tables(pos_ref, invf_ref)
        for t in range(ntile):
            sl = slice(t * LANES, (t + 1) * LANES)
            oq_ref[:, sl] = _rope_apply_t(dq_ref[:, sl], cos, sin, first).astype(BF16)
        ok_ref[...] = _rope_apply_t(dk_ref[...], cos, sin, first).astype(BF16)

    return pl.pallas_call(
        body, name=name, grid=(T // tm,),
        in_specs=[pl.BlockSpec((tm, MLA_W), lambda i: (i, 0)), pl.BlockSpec((tm, LANES), lambda i: (i, 0)),
                  pl.BlockSpec((tm, 1), lambda i: (i, 0)), pl.BlockSpec((1, LANES), lambda i: (0, 0))],
        out_specs=[pl.BlockSpec((tm, MLA_W), lambda i: (i, 0)), pl.BlockSpec((tm, LANES), lambda i: (i, 0))],
        out_shape=[jax.ShapeDtypeStruct((T, MLA_W), BF16), jax.ShapeDtypeStruct((T, LANES), BF16)],
        compiler_params=_cparams(("parallel",)),
    )(dqr, dkr, pos, invf)


CONV_ROWS = 256
HALO = 8


def _conv_taps(w_ref):
    return w_ref[0:1, :], w_ref[1:2, :], w_ref[2:3, :]


def _conv_rows(cur, prev, w, bias):
    ext = jnp.concatenate([prev, cur], axis=0)
    u1 = pltpu.roll(ext, 1, 0)[HALO:]
    u2 = pltpu.roll(ext, 2, 0)[HALO:]
    return w[2] * cur + w[1] * u1 + w[0] * u2 + bias, u1, u2


def _conv_fwd(u, w, bias, *, seq, name):
    T = u.shape[0]
    B = T // seq
    W2 = 2 * FF_BLK

    def body(u_ref, w_ref, b_ref, a_ref):
        wv = _conv_taps(w_ref)
        bv = b_ref[...]
        for c in range(seq // CONV_ROWS):
            r0 = c * CONV_ROWS
            cur = u_ref[r0:r0 + CONV_ROWS, :]
            prev = u_ref[r0 - HALO:r0, :] if c > 0 else jnp.zeros((HALO, W2), F32)
            y, _, _ = _conv_rows(cur, prev, wv, bv)
            gc = y[:, :FF_BLK]
            a_ref[r0:r0 + CONV_ROWS, :] = (gc * (1.0 / (1.0 + jnp.exp(-gc))) * y[:, FF_BLK:]).astype(BF16)

    return pl.pallas_call(
        body, name=name, grid=(B, N_FF_BLK),
        in_specs=[pl.BlockSpec((seq, W2), lambda b, j: (b, j)), pl.BlockSpec((3, W2), lambda b, j: (0, j)),
                  pl.BlockSpec((1, W2), lambda b, j: (0, j))],
        out_specs=pl.BlockSpec((seq, FF_BLK), lambda b, j: (b, j)),
        out_shape=jax.ShapeDtypeStruct((T, D_FF), BF16),
        compiler_params=_cparams(("parallel", "parallel")),
    )(u, w, bias)


def _conv_bwd(u, da, w, bias, *, seq, name):
    T = u.shape[0]
    B = T // seq
    W2 = 2 * FF_BLK
    nchunk = seq // CONV_ROWS

    def body(u_ref, da_ref, w_ref, b_ref, du_ref, dw_ref, db_ref, duc_s):
        wv = _conv_taps(w_ref)
        bv = b_ref[...]
        zrow = jnp.zeros((1, W2), F32)
        dw0, dw1, dw2, dbs = zrow, zrow, zrow, zrow
        for c in range(nchunk):
            r0 = c * CONV_ROWS
            cur = u_ref[r0:r0 + CONV_ROWS, :]
            prev = u_ref[r0 - HALO:r0, :] if c > 0 else jnp.zeros((HALO, W2), F32)
            y, u1, u2 = _conv_rows(cur, prev, wv, bv)
            gc = y[:, :FF_BLK]
            vc = y[:, FF_BLK:]
            sg = 1.0 / (1.0 + jnp.exp(-gc))
            dav = da_ref[r0:r0 + CONV_ROWS, :]
            duc = jnp.concatenate([dav * vc * (sg * (1.0 + gc * (1.0 - sg))), dav * (gc * sg)], axis=1)
            duc_s[r0:r0 + CONV_ROWS, :] = duc
            dw0 = dw0 + jnp.sum(duc * u2, axis=0, keepdims=True)
            dw1 = dw1 + jnp.sum(duc * u1, axis=0, keepdims=True)
            dw2 = dw2 + jnp.sum(duc * cur, axis=0, keepdims=True)
            dbs = dbs + jnp.sum(duc, axis=0, keepdims=True)
        duc_s[seq:seq + HALO, :] = jnp.zeros((HALO, W2), F32)
        n_ext = CONV_ROWS + HALO
        for c in range(nchunk):
            r0 = c * CONV_ROWS
            ext = duc_s[r0:r0 + n_ext, :]
            s1 = pltpu.roll(ext, n_ext - 1, 0)[:CONV_ROWS]
            s2 = pltpu.roll(ext, n_ext - 2, 0)[:CONV_ROWS]
            du_ref[r0:r0 + CONV_ROWS, :] = (wv[2] * ext[:CONV_ROWS] + wv[1] * s1 + wv[0] * s2).astype(BF16)

        first = pl.program_id(1) == 0

        @pl.when(first)
        def _():
            dw_ref[0:1, :] = dw0
            dw_ref[1:2, :] = dw1
            dw_ref[2:3, :] = dw2
            db_ref[...] = dbs

        @pl.when(jnp.logical_not(first))
        def _():
            dw_ref[0:1, :] += dw0
            dw_ref[1:2, :] += dw1
            dw_ref[2:3, :] += dw2
            db_ref[...] += dbs

    return pl.pallas_call(
        body, name=name, grid=(N_FF_BLK, B),
        in_specs=[pl.BlockSpec((seq, W2), lambda j, b: (b, j)), pl.BlockSpec((seq, FF_BLK), lambda j, b: (b, j)),
                  pl.BlockSpec((3, W2), lambda j, b: (0, j)), pl.BlockSpec((1, W2), lambda j, b: (0, j))],
        out_specs=[pl.BlockSpec((seq, W2), lambda j, b: (b, j)), pl.BlockSpec((3, W2), lambda j, b: (0, j)),
                   pl.BlockSpec((1, W2), lambda j, b: (0, j))],
        out_shape=[jax.ShapeDtypeStruct((T, 2 * D_FF), BF16), jax.ShapeDtypeStruct((3, 2 * D_FF), F32),
                   jax.ShapeDtypeStruct((1, 2 * D_FF), F32)],
        scratch_shapes=[pltpu.VMEM((seq + HALO, W2), F32)],
        compiler_params=_cparams(("parallel", "arbitrary")),
    )(u, da, w, bias)


def _place():
    return lax.axis_index("x"), lax.axis_index("y"), lax.axis_index("c")


def _other_chips(x, y):
    return [(1 - x, y), (x, 1 - y), (1 - x, 1 - y)]


def _all_gather(vs, *, name):
    n = len(vs)

    def body(*refs):
        v_refs, out_refs = refs[:n], refs[n:2 * n]
        send_sems, recv_sems, local_sems = refs[2 * n:]
        x, y, c = _place()
        me, sibling = (x, y, c), (x, y, 1 - c)
        chips = _other_chips(x, y)

        def slab(a, px, py, pc):
            return out_refs[a].at[4 * px + 2 * py + pc]

        def copy(a, k, block, to, src=None):
            return pltpu.make_async_remote_copy(
                src_ref=slab(a, *block) if src is None else src, dst_ref=slab(a, *block),
                send_sem=send_sems.at[7 * a + k], recv_sem=recv_sems.at[7 * a + k], device_id=to, device_id_type=MESH)

        mine = [pltpu.make_async_copy(v_refs[a], slab(a, *me), local_sems.at[a]) for a in range(n)]
        for cp in mine:
            cp.start()
        first = []
        for a in range(n):
            first.append(copy(a, 0, me, sibling, src=v_refs[a]))
            first += [copy(a, 1 + j, me, (*chip, c), src=v_refs[a]) for j, chip in enumerate(chips)]
        for cp in first:
            cp.start()
        passed = []
        for j, chip in enumerate(chips):
            for a in range(n):
                copy(a, 1 + j, (*chip, c), me).wait_recv()
                cp = copy(a, 4 + j, (*chip, c), sibling)
                cp.start()
                passed.append(cp)
        for a in range(n):
            copy(a, 0, sibling, me).wait_recv()
            for j, chip in enumerate(chips):
                copy(a, 4 + j, (*chip, 1 - c), me).wait_recv()
        for cp in first + passed:
            cp.wait_send()
        for cp in mine:
            cp.wait()

    return pl.pallas_call(
        body, name=name, in_specs=[ANY] * n, out_specs=[ANY] * n,
        out_shape=[jax.ShapeDtypeStruct((N_DEV,) + v.shape, v.dtype) for v in vs],
        scratch_shapes=[pltpu.SemaphoreType.DMA((7 * n,)), pltpu.SemaphoreType.DMA((7 * n,)), pltpu.SemaphoreType.DMA((n,))],
    )(*vs)


def _all_gather_async(vs, *, name, collective_id):
    n = len(vs)
    v_refs = [jax.new_ref(v, memory_space=pltpu.MemorySpace.HBM) for v in vs]
    out_refs = [jax.empty_ref(jax.ShapeDtypeStruct((N_DEV,) + v.shape, v.dtype), memory_space=pltpu.MemorySpace.HBM)
                for v in vs]

    @pl.kernel(mesh=plsc.ScalarSubcoreMesh(axis_name="seq", num_cores=1), name=name,
               scratch_types=(pltpu.SemaphoreType.DMA((7 * n,)), pltpu.SemaphoreType.DMA((7 * n,)),
                              pltpu.SemaphoreType.DMA((n,))),
               compiler_params=pltpu.CompilerParams(collective_id=collective_id))
    def launch(send_sems, recv_sems, local_sems):
        x, y, c = _place()
        me, sibling = (x, y, c), (x, y, 1 - c)
        chips = _other_chips(x, y)
        peers = [sibling] + [(*chip, c) for chip in chips]
        barrier = pltpu.get_barrier_semaphore()
        for peer in peers:
            pl.semaphore_signal(barrier, inc=1, device_id=peer, device_id_type=MESH)
        pl.semaphore_wait(barrier, len(peers))

        def slab(a, px, py, pc):
            return out_refs[a].at[4 * px + 2 * py + pc]

        def copy(a, k, block, to, src=None):
            return pltpu.make_async_remote_copy(
                src_ref=slab(a, *block) if src is None else src, dst_ref=slab(a, *block),
                send_sem=send_sems.at[7 * a + k], recv_sem=recv_sems.at[7 * a + k], device_id=to, device_id_type=MESH)

        mine = [pltpu.make_async_copy(v_refs[a], slab(a, *me), local_sems.at[a]) for a in range(n)]
        for cp in mine:
            cp.start()
        first = []
        for a in range(n):
            first.append(copy(a, 0, me, sibling, src=v_refs[a]))
            first += [copy(a, 1 + j, me, (*chip, c), src=v_refs[a]) for j, chip in enumerate(chips)]
        for cp in first:
            cp.start()
        passed = []
        for j, chip in enumerate(chips):
            for a in range(n):
                copy(a, 1 + j, (*chip, c), me).wait_recv()
                cp = copy(a, 4 + j, (*chip, c), sibling)
                cp.start()
                passed.append(cp)
        for a in range(n):
            copy(a, 0, sibling, me).wait_recv()
            for j, chip in enumerate(chips):
                copy(a, 4 + j, (*chip, 1 - c), me).wait_recv()
        for cp in first + passed:
            cp.wait_send()
        for cp in mine:
            cp.wait()

    launch()
    return [r[...] for r in out_refs]


def _rs_sibling(g8s, *, name):
    n = len(g8s)

    def body(*refs):
        g_refs, out_refs = refs[:n], refs[n:2 * n]
        send_sems, recv_sems = refs[2 * n:]
        x, y, c = _place()
        copies = [
            pltpu.make_async_remote_copy(
                src_ref=g_refs[a].at[2 * k + 1 - c], dst_ref=out_refs[a].at[k],
                send_sem=send_sems.at[4 * a + k], recv_sem=recv_sems.at[4 * a + k],
                device_id=(x, y, 1 - c), device_id_type=MESH)
            for a in range(n) for k in range(4)]
        for cp in copies:
            cp.start()
        for cp in copies:
            cp.wait()

    return pl.pallas_call(
        body, name=name, in_specs=[ANY] * n, out_specs=[ANY] * n,
        out_shape=[jax.ShapeDtypeStruct((4,) + g.shape[1:], g.dtype) for g in g8s],
        scratch_shapes=[pltpu.SemaphoreType.DMA((4 * n,)), pltpu.SemaphoreType.DMA((4 * n,))],
    )(*g8s)


def _handshake(peers):
    barrier = pltpu.get_barrier_semaphore()
    for peer in peers:
        pl.semaphore_signal(barrier, inc=1, device_id=peer, device_id_type=MESH)
    pl.semaphore_wait(barrier, len(peers))


def _hbm_refs(arrays, lead):
    src = [jax.new_ref(a, memory_space=pltpu.MemorySpace.HBM) for a in arrays]
    dst = [jax.empty_ref(jax.ShapeDtypeStruct((lead,) + a.shape[1:], a.dtype), memory_space=pltpu.MemorySpace.HBM)
           for a in arrays]
    return src, dst


def _rs_sibling_async(g8s, *, name, collective_id):
    n = len(g8s)
    g_refs, out_refs = _hbm_refs(g8s, 4)

    @pl.kernel(mesh=plsc.ScalarSubcoreMesh(axis_name="seq", num_cores=1), name=name,
               scratch_types=(pltpu.SemaphoreType.DMA((4 * n,)), pltpu.SemaphoreType.DMA((4 * n,))),
               compiler_params=pltpu.CompilerParams(collective_id=collective_id))
    def launch(send_sems, recv_sems):
        x, y, c = _place()
        _handshake([(x, y, 1 - c)])
        copies = [
            pltpu.make_async_remote_copy(
                src_ref=g_refs[a].at[2 * k + 1 - c], dst_ref=out_refs[a].at[k],
                send_sem=send_sems.at[4 * a + k], recv_sem=recv_sems.at[4 * a + k],
                device_id=(x, y, 1 - c), device_id_type=MESH)
            for a in range(n) for k in range(4)]
        for cp in copies:
            cp.start()
        for cp in copies:
            cp.wait()

    launch()
    return [r[...] for r in out_refs]


def _rs_chips_async(h4s, *, name, collective_id):
    n = len(h4s)
    h_refs, out_refs = _hbm_refs(h4s, 3)

    @pl.kernel(mesh=plsc.ScalarSubcoreMesh(axis_name="seq", num_cores=1), name=name,
               scratch_types=(pltpu.SemaphoreType.DMA((3 * n,)), pltpu.SemaphoreType.DMA((3 * n,))),
               compiler_params=pltpu.CompilerParams(collective_id=collective_id))
    def launch(send_sems, recv_sems):
        x, y, c = _place()
        chips = _other_chips(x, y)
        _handshake([(cx, cy, c) for cx, cy in chips])
        copies = [
            pltpu.make_async_remote_copy(
                src_ref=h_refs[a].at[2 * cx + cy], dst_ref=out_refs[a].at[j],
                send_sem=send_sems.at[3 * a + j], recv_sem=recv_sems.at[3 * a + j],
                device_id=(cx, cy, c), device_id_type=MESH)
            for a in range(n) for j, (cx, cy) in enumerate(chips)]
        for cp in copies:
            cp.start()
        for cp in copies:
            cp.wait()

    launch()
    return [r[...] for r in out_refs]


def _row_tile(rows):
    return rows if rows <= 512 else 256


def _rs_chip_sum(g8, from_sibling, place_idx, *, name):
    _, R, C = g8.shape
    tr = _row_tile(R)

    def body(pi_ref, a_ref, b_ref, f_ref, h_ref):
        s = a_ref[...] + b_ref[...]
        h_ref[...] = s.astype(BF16)

        @pl.when(pl.program_id(1) == pi_ref[1])
        def _():
            f_ref[...] = s

    blk = pl.BlockSpec((None, tr, C), lambda r, k, pi_ref: (k, r, 0))
    return pl.pallas_call(
        body, name=name,
        grid_spec=pltpu.PrefetchScalarGridSpec(
            num_scalar_prefetch=1, grid=(R // tr, 4),
            in_specs=[pl.BlockSpec((None, tr, C), lambda r, k, pi_ref: (2 * k + pi_ref[0], r, 0)), blk],
            out_specs=[pl.BlockSpec((tr, C), lambda r, k, pi_ref: (r, 0)), blk]),
        out_shape=[jax.ShapeDtypeStruct((R, C), F32), jax.ShapeDtypeStruct((4, R, C), BF16)],
        compiler_params=_cparams(("parallel", "arbitrary")),
    )(place_idx, g8, from_sibling)


def _rs_chips(h4s, *, name):
    n = len(h4s)

    def body(*refs):
        h_refs, out_refs = refs[:n], refs[n:2 * n]
        send_sems, recv_sems = refs[2 * n:]
        x, y, c = _place()
        copies = [
            pltpu.make_async_remote_copy(
                src_ref=h_refs[a].at[2 * cx + cy], dst_ref=out_refs[a].at[j],
                send_sem=send_sems.at[3 * a + j], recv_sem=recv_sems.at[3 * a + j],
                device_id=(cx, cy, c), device_id_type=MESH)
            for a in range(n) for j, (cx, cy) in enumerate(_other_chips(x, y))]
        for cp in copies:
            cp.start()
        for cp in copies:
            cp.wait()

    return pl.pallas_call(
        body, name=name, in_specs=[ANY] * n, out_specs=[ANY] * n,
        out_shape=[jax.ShapeDtypeStruct((3,) + h.shape[1:], h.dtype) for h in h4s],
        scratch_shapes=[pltpu.SemaphoreType.DMA((3 * n,)), pltpu.SemaphoreType.DMA((3 * n,))],
    )(*h4s)


def _split_moves(segments, chunk):
    moves = []
    for dst, src, length in segments:
        while length > 0:
            dev, off = divmod(src, chunk)
            take = min(length, chunk - off)
            moves.append((dst, dev, off, take))
            dst, src, length = dst + take, src + take, length - take
    return moves


def _assemble(stacked, segments, zero_spans, out_cols, *, name):
    _, R, c = stacked.shape
    tr = _row_tile(R)
    moves = _split_moves(segments, c)

    def body(x_ref, o_ref):
        for dst, dev, off, take in moves:
            o_ref[:, dst:dst + take] = x_ref[dev, :, off:off + take]
        for a, b in zero_spans:
            o_ref[:, a:b] = jnp.zeros((tr, b - a), o_ref.dtype)

    return pl.pallas_call(
        body, name=name, grid=(R // tr,),
        in_specs=[pl.BlockSpec((N_DEV, tr, c), lambda i: (0, i, 0))],
        out_specs=pl.BlockSpec((tr, out_cols), lambda i: (i, 0)),
        out_shape=jax.ShapeDtypeStruct((R, out_cols), stacked.dtype),
        compiler_params=_cparams(("parallel",)),
    )(stacked)


def _disassemble(full, segments, chunk, *, name):
    R = full.shape[0]
    tr = _row_tile(R)
    moves = _split_moves(segments, chunk)

    def body(x_ref, o_ref):
        seen = set()
        for dst, dev, off, take in moves:
            piece = x_ref[:, dst:dst + take]
            if (dev, off) in seen:
                piece = piece + o_ref[dev, :, off:off + take]
            seen.add((dev, off))
            o_ref[dev, :, off:off + take] = piece

    return pl.pallas_call(
        body, name=name, grid=(R // tr,),
        in_specs=[pl.BlockSpec((tr, full.shape[1]), lambda i: (i, 0))],
        out_specs=pl.BlockSpec((N_DEV, tr, chunk), lambda i: (0, i, 0)),
        out_shape=jax.ShapeDtypeStruct((N_DEV, R, chunk), F32),
        compiler_params=_cparams(("parallel",)),
    )(full)


_O_CQ = 3 * SB_W
_O_CKV = _O_CQ + Q_LORA
_O_KR = _O_CKV + KV_LORA
SEG_W_IN = ((0, 0, 3 * SB_W), (P_CKV, _O_CKV, KV_LORA), (P_KRT, _O_KR, MLA_ROPE), (P_KRT + MLA_ROPE, _O_KR, MLA_ROPE),
            (P_CQ, _O_CQ, Q_LORA))
ZERO_W_IN = ((P_KRT + 2 * MLA_ROPE, P_CQ),)
SEG_W_UQ = tuple((MLA_NOPE * h, MLA_QK * h, MLA_NOPE) for h in range(MLA_HEADS)) + tuple(
    (MLA_W + LANES * (h // 2) + MLA_ROPE * (h % 2), MLA_QK * h + MLA_NOPE, MLA_ROPE) for h in range(MLA_HEADS))
ZERO_W_UQ = tuple((MLA_W + LANES * g + 2 * MLA_ROPE, MLA_W + LANES * (g + 1)) for g in range(MLA_HEADS // 2))
SEG_W_UKV = tuple((MLA_NOPE * h, (MLA_NOPE + MLA_V) * h, MLA_NOPE) for h in range(MLA_HEADS)) + tuple(
    (MLA_W + MLA_V * h, (MLA_NOPE + MLA_V) * h + MLA_NOPE, MLA_V) for h in range(MLA_HEADS))
SEG_W_UP = tuple((2 * FF_BLK * blk + FF_BLK * half, D_FF * half + FF_BLK * blk, FF_BLK)
                 for half in range(2) for blk in range(N_FF_BLK))


def _sum8(g, *, name):
    _, R, C = g.shape

    def body(g_ref, o_ref):
        acc = g_ref[0]
        for k in range(1, N_DEV):
            acc = acc + g_ref[k]
        o_ref[...] = acc

    return pl.pallas_call(
        body, name=name, out_shape=jax.ShapeDtypeStruct((R, C), F32),
    )(g)


def _adamw_math(w, gf, m, v):
    c1 = 1.0 / (1.0 - ADAM_B1 ** ADAM_STEP)
    c2 = 1.0 / (1.0 - ADAM_B2 ** ADAM_STEP)
    mn = ADAM_B1 * m + (1.0 - ADAM_B1) * gf
    vn = ADAM_B2 * v + (1.0 - ADAM_B2) * (gf * gf)
    return -ADAM_LR * ((mn * c1) / (jnp.sqrt(vn * c2) + ADAM_EPS) + ADAM_WD * w), mn, vn


def _adamw(w, g, m, v, *, name):
    R, C = w.shape
    tr = _row_tile(R)

    def body(w_ref, g_ref, m_ref, v_ref, d_ref, mo_ref, vo_ref):
        d_ref[...], mo_ref[...], vo_ref[...] = _adamw_math(w_ref[...], g_ref[...], m_ref[...], v_ref[...])

    blk = pl.BlockSpec((tr, C), lambda i: (i, 0))
    shp = jax.ShapeDtypeStruct((R, C), F32)
    return pl.pallas_call(
        body, name=name, grid=(R // tr,), in_specs=[blk] * 4, out_specs=[blk] * 3,
        out_shape=[shp, shp, shp], compiler_params=_cparams(("parallel",)),
    )(w, g, m, v)


def _adamw_rs(own, r3, w, m, v, *, name):
    R, C = w.shape
    tr = _row_tile(R)

    def body(f_ref, r_ref, w_ref, m_ref, v_ref, g_ref, d_ref, mo_ref, vo_ref):
        gf = ((f_ref[...] + r_ref[0].astype(F32)) + r_ref[1].astype(F32)) + r_ref[2].astype(F32)
        g_ref[...] = gf
        d_ref[...], mo_ref[...], vo_ref[...] = _adamw_math(w_ref[...], gf, m_ref[...], v_ref[...])

    blk = pl.BlockSpec((tr, C), lambda i: (i, 0))
    shp = jax.ShapeDtypeStruct((R, C), F32)
    return pl.pallas_call(
        body, name=name, grid=(R // tr,),
        in_specs=[blk, pl.BlockSpec((3, tr, C), lambda i: (0, i, 0)), blk, blk, blk], out_specs=[blk] * 4,
        out_shape=[shp] * 4, compiler_params=_cparams(("parallel",)),
    )(own, r3, w, m, v)


def _ff_interleave(a):
    lead = a.shape[:-1]
    return a.reshape(*lead, 2, N_FF_BLK, FF_BLK).swapaxes(-3, -2).reshape(*lead, 2 * D_FF)


def _ff_deinterleave(a):
    lead = a.shape[:-1]
    return a.reshape(*lead, N_FF_BLK, 2, FF_BLK).swapaxes(-3, -2).reshape(*lead, 2 * D_FF)


SMALL =(("g_mix", D_MODEL), ("g_cq", Q_LORA), ("g_ckv", KV_LORA), ("g_sb_out", SB_W), ("g_mla_out", MLA_W),
         ("g_ffn", D_MODEL), ("conv_b", 2 * D_FF), ("g_final", D_MODEL))
SMALL_ROWS = 88


SMALL_USED = sum(size for _, size in SMALL)


def _pack_small(d, tail=None):
    parts = [d[n].reshape(-1) for n, _ in SMALL] + ([] if tail is None else [tail])
    flat = jnp.concatenate(parts)
    flat = jnp.pad(flat, (0, SMALL_ROWS * LANES - flat.shape[0]))
    return flat.reshape(SMALL_ROWS, LANES)


def _unpack_small(a):
    flat = a.reshape(-1)
    out, off = {}, 0
    for n, size in SMALL:
        out[n] = flat[off:off + size]
        off += size
    return out


def kernel(x, positions, g_mix, w_in, g_cq, w_uq, g_ckv, w_ukv, g_sb_out, g_mla_out, w_out, g_ffn, w_up, conv_w, conv_b, w_down, g_final, loss_target, m_g_mix, m_w_in, m_g_cq, m_w_uq, m_g_ckv, m_w_ukv, m_g_sb_out, m_g_mla_out, m_w_out, m_g_ffn, m_w_up, m_conv_w, m_conv_b, m_w_down, m_g_final, v_g_mix, v_w_in, v_g_cq, v_w_uq, v_g_ckv, v_w_ukv, v_g_sb_out, v_g_mla_out, v_w_out, v_g_ffn, v_w_up, v_conv_w, v_conv_b, v_w_down, v_g_final):
    B, S, D = x.shape
    T = B * S
    xf = x.reshape(T, D)
    tgt = loss_target.reshape(T, D)
    pos = positions.reshape(T, 1)
    half = MLA_ROPE // 2
    inv_freq = 1.0 / (ROPE_BASE ** (jnp.arange(half, dtype=F32) * (2.0 / MLA_ROPE)))
    invf = jnp.tile(inv_freq, LANES // half).reshape(1, LANES)
    place_idx = jnp.stack([lax.axis_index("c"), 2 * lax.axis_index("x") + lax.axis_index("y")]).astype(jnp.int32)

    names = ("w_in", "w_uq", "w_ukv", "w_out", "w_up", "w_down", "conv_w")
    shard = {"w_in": w_in[0], "w_uq": w_uq[0], "w_ukv": w_ukv[0], "w_out": w_out[0], "w_up": w_up[0],
             "w_down": w_down[0], "conv_w": conv_w[0]}
    sent = {n: shard[n] if n == "conv_w" else shard[n].astype(BF16) for n in names}
    later = names[1:]
    w_in_all = _all_gather([sent["w_in"]], name="ag_w_in")[0]
    w_in_all, rest = lax.optimization_barrier((w_in_all, [sent[n] for n in later]))
    got = {"w_in": w_in_all}
    got.update(zip(later, _all_gather_async(rest, name="ag_weights_async", collective_id=0)))
    wi = _assemble(got["w_in"], SEG_W_IN, ZERO_W_IN, P_COLS, name="asm_w_in")
    wuq = _assemble(got["w_uq"], SEG_W_UQ, ZERO_W_UQ, 2 * MLA_W, name="asm_w_uq")
    wukv = _assemble(got["w_ukv"], SEG_W_UKV, (), 2 * MLA_W, name="asm_w_ukv")
    wup = _assemble(got["w_up"], SEG_W_UP, (), 2 * D_FF, name="asm_w_up")
    cwi = _assemble(got["conv_w"], SEG_W_UP, (), 2 * D_FF, name="asm_conv_w")
    wo = got["w_out"].reshape(D, D)
    wdn = got["w_down"].reshape(D_FF, D)
    cbi = _ff_interleave(conv_b)

    h, p = _rms_matmul_nn(xf, g_mix, wi, tm=512, name="proj_in")
    o_sb, ltot = _sb_fwd(p, seq=S, name="sb_fwd")
    cq, qm, krt = _proj_uq_rope(p, g_cq, wuq, pos, invf, tm=512, name="proj_uq")
    ckv, kvm = _rms_matmul_nn(p, g_ckv, wukv, tm=512, name="proj_ukv", col_block=P_CKV // KV_LORA, out_dtype=BF16)
    o_mla, lse = _mla_fwd(qm, kvm, krt, seq=S, name="mla_fwd")
    ocat, x1 = _heads_out(o_sb, o_mla, g_sb_out, g_mla_out, wo, xf, tm=512, name="proj_out")
    hf, u = _rms_matmul_nn(x1, g_ffn, wup, tm=256, name="ffn_up")
    a = _conv_fwd(u, cwi, cbi, seq=S, name="conv_fwd")
    dx2, dg_final, loss_row = _matmul_nn_loss(a, wdn, x1, g_final.reshape(1, D), tgt, tm=512, name="ffn_down_loss")

    da = _matmul_nt(dx2, wdn, tm=1024, tn=D_FF // 2, out_dtype=F32, name="d_ffn_down")
    dw_down = _matmul_tn(a, dx2, tm=D_FF // 2, tn=1024, tk=1024, name="dw_down")
    du, dcw, dcb = _conv_bwd(u, da, cwi, cbi, seq=S, name="conv_bwd")
    dw_up = _matmul_tn(hf, du, tm=1024, tn=D_FF, tk=1024, name="dw_up")
    dx1, dg_ffn = _matmul_nt_rms_bwd(du, wup, x1, g_ffn, tm=512, name="d_ffn_up", residual=dx2)
    dw_out = _matmul_tn(ocat, dx1, tm=1024, tn=1024, tk=1024, name="dw_out")
    do_sb, do_mla, dg_sb, dg_mla = _heads_out_bwd(dx1, wo, o_sb, o_mla, g_sb_out, g_mla_out, tm=512, name="d_proj_out")

    early = ("w_down", "w_up", "conv_w", "w_out")
    g8 = {"w_up": _disassemble(dw_up, SEG_W_UP, shard["w_up"].shape[1], name="split_dw_up"),
          "conv_w": _disassemble(dcw, SEG_W_UP, shard["conv_w"].shape[1], name="split_dconv_w"),
          "w_out": dw_out.reshape((N_DEV,) + shard["w_out"].shape),
          "w_down": dw_down.reshape((N_DEV,) + shard["w_down"].shape)}
    sib_e = _rs_sibling_async([g8[n] for n in early], name="rs_sibling_async", collective_id=1)

    dq_sb, dk_sb, dv_sb = _sb_bwd(p, ltot, do_sb, seq=S, name="sb_bwd")
    sib_e, dq_sb = lax.optimization_barrier((sib_e, dq_sb))
    own, r3 = {}, {}
    sums_e = [_rs_chip_sum(g8[n], fs, place_idx, name="rs_chip_sum_" + n) for n, fs in zip(early, sib_e)]
    r3.update(zip(early, _rs_chips_async([h4 for _, h4 in sums_e], name="rs_chips_async", collective_id=2)))
    own.update({n: f for n, (f, _) in zip(early, sums_e)})

    dqn, dqr, dkn, dvm, dkr = _mla_bwd(qm, kvm, krt, o_mla, lse, do_mla, seq=S, name="mla_bwd")
    dqm, dcq, dg_cq, dkr_u = _d_proj_uq_rope(dqn, dqr, dkr, wuq, p, g_cq, pos, invf, tm=512, name="d_proj_uq")
    dkvm, dckv, dg_ckv = _d_proj_cat([dkn, dvm], wukv, p, g_ckv, tm=512, name="d_proj_ukv",
                                     col_block=P_CKV // KV_LORA, out_dtype=BF16)
    dw_uq = _matmul_tn(cq, dqm, tm=Q_LORA, tn=1024, tk=1024, name="dw_uq")
    dw_ukv = _matmul_tn(ckv, dkvm, tm=KV_LORA, tn=1024, tk=1024, name="dw_ukv")
    dp, dx, dg_mix = _d_proj_cat([dq_sb, dk_sb, dv_sb, dckv, dkr_u, dcq], wi, xf, g_mix, tm=512, name="d_proj_in",
                                 residual=dx1)
    dw_in = _matmul_tn(h, dp, tm=1024, tn=P_COLS, tk=1024, name="dw_in")

    late = ("w_in", "w_uq", "w_ukv")
    g8.update({"w_in": _disassemble(dw_in, SEG_W_IN, shard["w_in"].shape[1], name="split_dw_in"),
               "w_uq": _disassemble(dw_uq, SEG_W_UQ, shard["w_uq"].shape[1], name="split_dw_uq"),
               "w_ukv": _disassemble(dw_ukv, SEG_W_UKV, shard["w_ukv"].shape[1], name="split_dw_ukv")})
    sib_l = _rs_sibling_async([g8[n] for n in late], name="rs_sibling_late", collective_id=3)

    params = {"w_in": (w_in, m_w_in, v_w_in), "w_uq": (w_uq, m_w_uq, v_w_uq), "w_ukv": (w_ukv, m_w_ukv, v_w_ukv),
              "w_out": (w_out, m_w_out, v_w_out), "w_up": (w_up, m_w_up, v_w_up), "conv_w": (conv_w, m_conv_w, v_conv_w),
              "w_down": (w_down, m_w_down, v_w_down)}
    grad, delta, new_m, new_v = {}, {}, {}, {}

    def adamw_group(group):
        for n in group:
            w_, m_, v_ = params[n]
            g_, d_, mn_, vn_ = _adamw_rs(own[n], r3[n], w_[0], m_[0], v_[0], name="adamw_" + n)
            grad[n], delta[n], new_m[n], new_v[n] = g_[None], d_[None], mn_[None], vn_[None]

    adamw_group(("w_up", "w_down"))
    sib_l, grad["w_up"] = lax.optimization_barrier((sib_l, grad["w_up"]))
    sums_l = [_rs_chip_sum(g8[n], fs, place_idx, name="rs_chip_sum_" + n) for n, fs in zip(late, sib_l)]
    r3.update(zip(late, _rs_chips_async([h4 for _, h4 in sums_l], name="rs_chips_late", collective_id=4)))
    own.update({n: f for n, (f, _) in zip(late, sums_l)})
    adamw_group(("conv_w", "w_out"))

    small_part = {"g_mix": dg_mix, "g_cq": dg_cq, "g_ckv": dg_ckv, "g_sb_out": dg_sb, "g_mla_out": dg_mla,
                  "g_ffn": dg_ffn, "conv_b": _ff_deinterleave(dcb), "g_final": dg_final}
    small_all, = _all_gather([_pack_small(small_part, tail=loss_row[0, 0:1])], name="ag_small_grads")
    gsmall = _sum8(small_all, name="sum_small_grads")
    adamw_group(late)
    small_w = {"g_mix": g_mix, "g_cq": g_cq, "g_ckv": g_ckv, "g_sb_out": g_sb_out, "g_mla_out": g_mla_out,
               "g_ffn": g_ffn, "conv_b": conv_b, "g_final": g_final}
    small_m = {"g_mix": m_g_mix, "g_cq": m_g_cq, "g_ckv": m_g_ckv, "g_sb_out": m_g_sb_out, "g_mla_out": m_g_mla_out,
               "g_ffn": m_g_ffn, "conv_b": m_conv_b, "g_final": m_g_final}
    small_v = {"g_mix": v_g_mix, "g_cq": v_g_cq, "g_ckv": v_g_ckv, "g_sb_out": v_g_sb_out, "g_mla_out": v_g_mla_out,
               "g_ffn": v_g_ffn, "conv_b": v_conv_b, "g_final": v_g_final}
    ds_, ms_, vs_ = _adamw(_pack_small(small_w), gsmall, _pack_small(small_m), _pack_small(small_v), name="adamw_small")
    for src, dst in ((_unpack_small(gsmall), grad), (_unpack_small(ds_), delta), (_unpack_small(ms_), new_m), (_unpack_small(vs_), new_v)):
        for n, _ in SMALL:
            dst[n] = src[n].reshape(small_w[n].shape)

    loss = gsmall.reshape(-1)[SMALL_USED]
    order = ("g_mix", "w_in", "g_cq", "w_uq", "g_ckv", "w_ukv", "g_sb_out", "g_mla_out", "w_out", "g_ffn", "w_up",
             "conv_w", "conv_b", "w_down", "g_final")
    return (loss, dx.reshape(B, S, D), *[grad[n] for n in order], *[delta[n] for n in order],
            *[new_m[n] for n in order], *[new_v[n] for n in order])
```

```python
import jax
import jax.numpy as jnp
from jax import lax
from jax.experimental import pallas as pl
from jax.experimental.pallas import tpu as pltpu
from jax.experimental.pallas import tpu_sc as plsc

F32 = jnp.float32
BF16 = jnp.bfloat16

D_MODEL = 1024
SB_HEADS = 8
SB_HEAD_DIM = 64
MLA_HEADS = 8
MLA_NOPE = 64
MLA_ROPE = 32
MLA_V = 64
Q_LORA = 384
KV_LORA = 256
D_FF = 2816
ROPE_BASE = 10000.0
EPS = 1e-6
SB_W = SB_HEADS * SB_HEAD_DIM
MLA_W = MLA_HEADS * MLA_V
MLA_QK = MLA_NOPE + MLA_ROPE

ADAM_LR = 0.001
ADAM_B1 = 0.9
ADAM_B2 = 0.999
ADAM_EPS = 1e-08
ADAM_WD = 0.01
ADAM_STEP = 10

N_DEV = 8
LANES = 128
V7X_VMEM_LIMIT = 56 * 1024 * 1024
FF_BLK = 256
N_FF_BLK = D_FF // FF_BLK

P_Q, P_K, P_V = 0, SB_W, 2 * SB_W
P_CKV = 3 * SB_W
P_KRT = P_CKV + KV_LORA
P_CQ = P_KRT + LANES
P_COLS = P_CQ + Q_LORA

MESH = pl.DeviceIdType.MESH
ANY = pl.BlockSpec(memory_space=pl.ANY)


def _cparams(sem=None, vmem=V7X_VMEM_LIMIT):
    return pltpu.CompilerParams(dimension_semantics=sem, vmem_limit_bytes=vmem)


def _matmul_nt(a, b, *, tm, tn, out_dtype, name):
    M, K = a.shape
    N = b.shape[0]
    assert M % tm == 0 and N % tn == 0, (name, a.shape, b.shape)

    def body(a_ref, b_ref, o_ref):
        acc = lax.dot_general(a_ref[...].astype(BF16), b_ref[...], (((1,), (1,)), ((), ())),
                              preferred_element_type=F32)
        o_ref[...] = acc.astype(out_dtype)

    return pl.pallas_call(
        body, name=name, grid=(M // tm, N // tn),
        in_specs=[pl.BlockSpec((tm, K), lambda i, j: (i, 0)), pl.BlockSpec((tn, K), lambda i, j: (j, 0))],
        out_specs=pl.BlockSpec((tm, tn), lambda i, j: (i, j)),
        out_shape=jax.ShapeDtypeStruct((M, N), out_dtype),
        compiler_params=_cparams(("parallel", "parallel")),
    )(a, b)


def _matmul_tn(a, b, *, tm, tn, tk, name, out_dtype=F32):
    K, M = a.shape
    N = b.shape[1]
    assert M % tm == 0 and N % tn == 0 and K % tk == 0, (name, a.shape, b.shape)
    n_k = K // tk
    narrow = out_dtype != F32

    def body(a_ref, b_ref, o_ref, *scratch):
        acc_ref = scratch[0] if narrow else o_ref
        k = pl.program_id(2)
        part = lax.dot_general(a_ref[...].astype(BF16), b_ref[...].astype(BF16), (((0,), (0,)), ((), ())),
                               preferred_element_type=F32)

        @pl.when(k == 0)
        def _():
            acc_ref[...] = part

        @pl.when(k > 0)
        def _():
            acc_ref[...] += part

        if narrow:
            @pl.when(k == n_k - 1)
            def _():
                o_ref[...] = acc_ref[...].astype(out_dtype)

    return pl.pallas_call(
        body, name=name, grid=(M // tm, N // tn, n_k),
        in_specs=[pl.BlockSpec((tk, tm), lambda i, j, k: (k, i)), pl.BlockSpec((tk, tn), lambda i, j, k: (k, j))],
        out_specs=pl.BlockSpec((tm, tn), lambda i, j, k: (i, j)),
        out_shape=jax.ShapeDtypeStruct((M, N), out_dtype),
        scratch_shapes=[pltpu.VMEM((tm, tn), F32)] if narrow else [],
        compiler_params=_cparams(("parallel", "parallel", "arbitrary")),
    )(a, b)


def _rms(xf, g):
    r = lax.rsqrt(jnp.mean(xf * xf, axis=1, keepdims=True) + EPS)
    return (xf * r) * g


def _rms_grad(dyf, xf, g):
    r = lax.rsqrt(jnp.mean(xf * xf, axis=1, keepdims=True) + EPS)
    xh = xf * r
    dyg = dyf * g
    dx = r * (dyg - xh * jnp.mean(dyg * xh, axis=1, keepdims=True))
    return dx, jnp.sum(dyf * xh, axis=0, keepdims=True)


def _accumulate(ref, part):
    @pl.when(pl.program_id(0) == 0)
    def _():
        ref[...] = part

    @pl.when(pl.program_id(0) > 0)
    def _():
        ref[...] += part


def _rms_matmul_nn(x, g, w, *, tm, name, col_block=0, out_dtype=F32):
    T = x.shape[0]
    C, N = w.shape
    assert T % tm == 0, (name, x.shape)

    def body(x_ref, g_ref, w_ref, h_ref, o_ref):
        hb = _rms(x_ref[...], g_ref[...]).astype(BF16)
        h_ref[...] = hb
        o_ref[...] = jnp.dot(hb, w_ref[...], preferred_element_type=F32).astype(out_dtype)

    return pl.pallas_call(
        body, name=name, grid=(T // tm,),
        in_specs=[pl.BlockSpec((tm, C), lambda i: (i, col_block)), pl.BlockSpec((1, C), lambda i: (0, 0)),
                  pl.BlockSpec((C, N), lambda i: (0, 0))],
        out_specs=[pl.BlockSpec((tm, C), lambda i: (i, 0)), pl.BlockSpec((tm, N), lambda i: (i, 0))],
        out_shape=[jax.ShapeDtypeStruct((T, C), BF16), jax.ShapeDtypeStruct((T, N), out_dtype)],
        compiler_params=_cparams(("parallel",)),
    )(x, g, w)


def _matmul_nt_rms_bwd(a, b, x, g, *, tm, name, residual=None, col_block=0, out_dtype=F32):
    M, K = a.shape
    C = b.shape[0]
    assert M % tm == 0, (name, a.shape)
    in_specs = [pl.BlockSpec((tm, K), lambda i: (i, 0)), pl.BlockSpec((C, K), lambda i: (0, 0)),
                pl.BlockSpec((tm, C), lambda i: (i, col_block)), pl.BlockSpec((1, C), lambda i: (0, 0))]
    args = [a, b, x, g]
    if residual is not None:
        in_specs.append(pl.BlockSpec((tm, C), lambda i: (i, 0)))
        args.append(residual)

    def body(*refs):
        a_ref, b_ref, x_ref, g_ref = refs[:4]
        dx_ref, dg_ref = refs[-2:]
        dy = lax.dot_general(a_ref[...].astype(BF16), b_ref[...], (((1,), (1,)), ((), ())), preferred_element_type=F32)
        dx, part = _rms_grad(dy, x_ref[...], g_ref[...])
        if residual is not None:
            dx = dx + refs[4][...]
        dx_ref[...] = dx.astype(out_dtype)
        _accumulate(dg_ref, part)

    return pl.pallas_call(
        body, name=name, grid=(M // tm,), in_specs=in_specs,
        out_specs=[pl.BlockSpec((tm, C), lambda i: (i, 0)), pl.BlockSpec((1, C), lambda i: (0, 0))],
        out_shape=[jax.ShapeDtypeStruct((M, C), out_dtype), jax.ShapeDtypeStruct((1, C), F32)],
        compiler_params=_cparams(("arbitrary",)),
    )(*args)


def _matmul_nn_loss(a, w, x1, g, tgt, *, tm, name):
    M, K = a.shape
    C = w.shape[1]
    assert M % tm == 0, (name, a.shape)

    def body(a_ref, w_ref, x_ref, g_ref, t_ref, dx_ref, dg_ref, loss_ref):
        xf = x_ref[...] + jnp.dot(a_ref[...], w_ref[...], preferred_element_type=F32)
        gf = g_ref[...]
        err = _rms(xf, gf) - t_ref[...]
        lpart = 0.5 * jnp.sum(jnp.mean(err * err, axis=1, keepdims=True), axis=0, keepdims=True)
        dx, gpart = _rms_grad(err * (1.0 / C), xf, gf)
        dx_ref[...] = dx
        _accumulate(dg_ref, gpart)
        _accumulate(loss_ref, jnp.broadcast_to(lpart, (1, LANES)))

    row = pl.BlockSpec((tm, C), lambda i: (i, 0))
    return pl.pallas_call(
        body, name=name, grid=(M // tm,),
        in_specs=[pl.BlockSpec((tm, K), lambda i: (i, 0)), pl.BlockSpec((K, C), lambda i: (0, 0)), row,
                  pl.BlockSpec((1, C), lambda i: (0, 0)), row],
        out_specs=[row, pl.BlockSpec((1, C), lambda i: (0, 0)), pl.BlockSpec((1, LANES), lambda i: (0, 0))],
        out_shape=[jax.ShapeDtypeStruct((M, C), F32), jax.ShapeDtypeStruct((1, C), F32),
                   jax.ShapeDtypeStruct((1, LANES), F32)],
        compiler_params=_cparams(("arbitrary",)),
    )(a, w, x1, g, tgt)


ATT_T = 256
ATT_PAIRS = 2
NEG_BIG = -1e30


def _lane_iota():
    return lax.broadcasted_iota(jnp.int32, (1, LANES), 1)


def _head_masks():
    first = _lane_iota() < SB_HEAD_DIM
    return first, jnp.logical_not(first)


def _pick(mask, x):
    return jnp.where(mask, x, jnp.zeros_like(x))


def _lane_value(t, lane):
    return jnp.sum(jnp.where(_lane_iota() == lane, t, 0.0), axis=1, keepdims=True)


def _split_hi_lo(x):
    hi = x.astype(BF16)
    lo = (x - hi.astype(F32)).astype(BF16)
    return jnp.concatenate([hi, lo], axis=1)


def _tri(n, kind):
    r = lax.broadcasted_iota(jnp.int32, (n, n), 0)
    c = lax.broadcasted_iota(jnp.int32, (n, n), 1)
    u = {"suffix_excl": r > c, "prefix_incl": r <= c, "prefix_excl": r < c}[kind].astype(BF16)
    return jnp.concatenate([u, u], axis=0)


def _dot_nt(a, b):
    return lax.dot_general(a, b, (((1,), (1,)), ((), ())), preferred_element_type=F32)


def _dot_tn(a, b):
    return lax.dot_general(a, b, (((0,), (0,)), ((), ())), preferred_element_type=F32)


def _dot(a, b):
    return jnp.dot(a, b, preferred_element_type=F32)


def _causal_mask(n, strict):
    r = lax.broadcasted_iota(jnp.int32, (n, n), 0)
    c = lax.broadcasted_iota(jnp.int32, (n, n), 1)
    return (c < r) if strict else (c <= r)


LOG2E = 1.4426950408889634


def _sb_logs(qh, kj, vis):
    z2 = _dot_nt(qh, kj) * LOG2E
    nk = jnp.maximum(z2, 0.0) + jnp.log2(1.0 + jnp.exp2(-jnp.abs(z2)))
    lb = z2 - nk
    if vis is not None:
        nk = jnp.where(vis, nk, 0.0)
    return lb, nk


def _sb_fwd(p, *, seq, name):
    T = p.shape[0]
    B = T // seq
    TQ = ATT_T
    nq = seq // TQ
    PP = ATT_PAIRS
    W = PP * LANES
    nstep = SB_W // W
    NH = 2 * PP

    def body(q_ref, k_ref, v_ref, o_ref, lt_ref, q_s, k_s, v_s):
        masks = _head_masks()
        q = q_ref[...] * (SB_HEAD_DIM ** -0.5)
        v = v_ref[...]
        k_s[...] = k_ref[...].astype(BF16)
        for h in range(NH):
            ps = slice((h // 2) * LANES, (h // 2 + 1) * LANES)
            hs = slice(h * LANES, (h + 1) * LANES)
            q_s[:, hs] = _pick(masks[h % 2], q[:, ps]).astype(BF16)
            v_s[:, hs] = _pick(masks[h % 2], v[:, ps]).astype(BF16)
        u_suf = _tri(TQ, "suffix_excl")
        vis = _causal_mask(TQ, True)

        def q_block(i, carry):
            q0 = pl.multiple_of(i * TQ, TQ)
            qs = [q_s[pl.ds(q0, TQ), h * LANES:(h + 1) * LANES] for h in range(NH)]

            def tile(k0, c, mask):
                rs, accs = list(c[:NH]), list(c[NH:])
                logs = [_sb_logs(qs[h], k_s[pl.ds(k0, TQ), (h // 2) * LANES:(h // 2 + 1) * LANES], mask) for h in range(NH)]
                sums = [_dot(_split_hi_lo(nk), u_suf) for _, nk in logs]
                for h in range(NH):
                    a = jnp.exp2(logs[h][0] - sums[h] - rs[h])
                    if mask is not None:
                        a = jnp.where(mask, a, 0.0)
                    accs[h // 2] = accs[h // 2] + _dot(a.astype(BF16), v_s[pl.ds(k0, TQ), h * LANES:(h + 1) * LANES])
                    rs[h] = rs[h] + jnp.sum(logs[h][1], axis=1, keepdims=True)
                return tuple(rs) + tuple(accs)

            zero = jnp.zeros((TQ, 1), F32)
            c = tile(q0, (zero,) * NH + (jnp.zeros((TQ, LANES), F32),) * PP, vis)

            def k_block(jj, c):
                return tile(pl.multiple_of((i - 1 - jj) * TQ, TQ), c, None)

            c = lax.fori_loop(0, i, k_block, c)
            for pr in range(PP):
                ps = slice(pr * LANES, (pr + 1) * LANES)
                o_ref[pl.ds(q0, TQ), ps] = c[NH + pr]
                lt_ref[pl.ds(q0, TQ), ps] = jnp.where(masks[0], c[2 * pr], c[2 * pr + 1])
            return carry

        lax.fori_loop(0, nq, q_block, 0)

    blk = lambda off: pl.BlockSpec((seq, W), lambda b, g: (b, off + g))
    out_blk = pl.BlockSpec((seq, W), lambda b, g: (b, g))
    return pl.pallas_call(
        body, name=name, grid=(B, nstep),
        in_specs=[blk(P_Q // W), blk(P_K // W), blk(P_V // W)],
        out_specs=[out_blk, out_blk],
        out_shape=[jax.ShapeDtypeStruct((T, SB_W), F32), jax.ShapeDtypeStruct((T, SB_W), F32)],
        scratch_shapes=[pltpu.VMEM((seq, NH * LANES), BF16), pltpu.VMEM((seq, W), BF16), pltpu.VMEM((seq, NH * LANES), BF16)],
        compiler_params=_cparams(("parallel", "parallel")),
    )(p, p, p)


def _sb_bwd(p, ltot, do, *, seq, name):
    T = p.shape[0]
    B = T // seq
    TQ = ATT_T
    nq = seq // TQ
    PP = ATT_PAIRS
    W = PP * LANES
    nstep = SB_W // W
    NH = 2 * PP
    scale = SB_HEAD_DIM ** -0.5

    def body(q_ref, k_ref, v_ref, lt_ref, do_ref, dq_ref, dk_ref, dv_ref, q_s, k_s, v_s, do_s, dk_s, dv_s):
        masks = _head_masks()
        q = q_ref[...] * scale
        dof = do_ref[...]
        k_s[...] = k_ref[...].astype(BF16)
        v_s[...] = v_ref[...].astype(BF16)
        for h in range(NH):
            ps = slice((h // 2) * LANES, (h // 2 + 1) * LANES)
            hs = slice(h * LANES, (h + 1) * LANES)
            q_s[:, hs] = _pick(masks[h % 2], q[:, ps]).astype(BF16)
            do_s[:, hs] = _pick(masks[h % 2], dof[:, ps]).astype(BF16)
        dk_s[...] = jnp.zeros_like(dk_s)
        dv_s[...] = jnp.zeros_like(dv_s)
        u_pin = _tri(TQ, "prefix_incl")
        u_pex = _tri(TQ, "prefix_excl")[:TQ]
        vis = _causal_mask(TQ, True)

        def q_block(i, carry):
            q0 = pl.multiple_of(i * TQ, TQ)
            qs = [q_s[pl.ds(q0, TQ), h * LANES:(h + 1) * LANES] for h in range(NH)]
            dos = [do_s[pl.ds(q0, TQ), h * LANES:(h + 1) * LANES] for h in range(NH)]
            lt = lt_ref[pl.ds(q0, TQ), :]
            lts = [_lane_value(lt[:, (h // 2) * LANES:(h // 2 + 1) * LANES], (h % 2) * SB_HEAD_DIM) for h in range(NH)]

            def tile(k0, c, mask):
                cs, gs, accs = list(c[:NH]), list(c[NH:2 * NH]), list(c[2 * NH:])
                kjs = [k_s[pl.ds(k0, TQ), pr * LANES:(pr + 1) * LANES] for pr in range(PP)]
                vjs = [v_s[pl.ds(k0, TQ), pr * LANES:(pr + 1) * LANES] for pr in range(PP)]
                logs = [_sb_logs(qs[h], kjs[h // 2], mask) for h in range(NH)]
                pins = [_dot(_split_hi_lo(nk), u_pin) for _, nk in logs]
                das = [_dot_nt(dos[h], vjs[h // 2]) for h in range(NH)]
                a_l, g_l = [], []
                for h in range(NH):
                    a = jnp.exp2(logs[h][0] - ((lts[h] - cs[h]) - pins[h]))
                    if mask is not None:
                        a = jnp.where(mask, a, 0.0)
                    a_l.append(a)
                    g_l.append(das[h] * a)
                pres = [_dot(g.astype(BF16), u_pex) for g in g_l]
                dz_l = []
                for h in range(NH):
                    dz = g_l[h] - jnp.exp2(logs[h][0]) * (g_l[h] + (pres[h] + gs[h]))
                    if mask is not None:
                        dz = jnp.where(mask, dz, 0.0)
                    dz_l.append(dz.astype(BF16))
                for h in range(NH):
                    accs[h] = accs[h] + _dot(dz_l[h], kjs[h // 2])
                for pr in range(PP):
                    ps = slice(pr * LANES, (pr + 1) * LANES)
                    ha, hb = 2 * pr, 2 * pr + 1
                    dk_s[pl.ds(k0, TQ), ps] += _dot_tn(dz_l[ha], qs[ha]) + _dot_tn(dz_l[hb], qs[hb])
                    dv_s[pl.ds(k0, TQ), ps] += _dot_tn(a_l[ha].astype(BF16), dos[ha]) + _dot_tn(a_l[hb].astype(BF16), dos[hb])
                for h in range(NH):
                    cs[h] = cs[h] + jnp.sum(logs[h][1], axis=1, keepdims=True)
                    gs[h] = gs[h] + jnp.sum(g_l[h], axis=1, keepdims=True)
                return tuple(cs) + tuple(gs) + tuple(accs)

            z1 = jnp.zeros((TQ, 1), F32)
            zl = jnp.zeros((TQ, LANES), F32)

            def k_block(j, c):
                return tile(pl.multiple_of(j * TQ, TQ), c, None)

            c = lax.fori_loop(0, i, k_block, (z1,) * (2 * NH) + (zl,) * NH)
            c = tile(q0, c, vis)
            for pr in range(PP):
                dq = jnp.where(masks[0], c[2 * NH + 2 * pr], c[2 * NH + 2 * pr + 1]) * scale
                dq_ref[pl.ds(q0, TQ), pr * LANES:(pr + 1) * LANES] = dq.astype(BF16)
            return carry

        lax.fori_loop(0, nq, q_block, 0)
        dk_ref[...] = dk_s[...].astype(BF16)
        dv_ref[...] = dv_s[...].astype(BF16)

    blk = lambda off: pl.BlockSpec((seq, W), lambda b, g: (b, off + g))
    out_blk = pl.BlockSpec((seq, W), lambda b, g: (b, g))
    return pl.pallas_call(
        body, name=name, grid=(B, nstep),
        in_specs=[blk(P_Q // W), blk(P_K // W), blk(P_V // W), out_blk, out_blk],
        out_specs=[out_blk, out_blk, out_blk],
        out_shape=[jax.ShapeDtypeStruct((T, SB_W), BF16) for _ in range(3)],
        scratch_shapes=[pltpu.VMEM((seq, NH * LANES), BF16), pltpu.VMEM((seq, W), BF16), pltpu.VMEM((seq, W), BF16),
                        pltpu.VMEM((seq, NH * LANES), BF16), pltpu.VMEM((seq, W), F32), pltpu.VMEM((seq, W), F32)],
        compiler_params=_cparams(("parallel", "parallel")),
    )(p, p, p, ltot, do)


def _mla_masks():
    lane = lax.broadcasted_iota(jnp.int32, (1, 2 * LANES), 1)
    ma = (lane < MLA_NOPE) | ((lane >= LANES) & (lane < LANES + MLA_ROPE))
    mb = ((lane >= MLA_NOPE) & (lane < LANES)) | ((lane >= LANES + MLA_ROPE) & (lane < LANES + 2 * MLA_ROPE))
    return ma, mb


def _mla_fwd(qm, kvm, krt, *, seq, name):
    T = qm.shape[0]
    B = T // seq
    TQ = ATT_T
    nq = seq // TQ
    PP = ATT_PAIRS
    W = PP * LANES
    nstep = MLA_W // W
    NH = 2 * PP
    CW = 2 * LANES
    scale = MLA_QK ** -0.5

    def body(qn_ref, qr_ref, kn_ref, v_ref, kr_ref, o_ref, lse_ref, q_s, kc_s, v_s):
        hm = _head_masks()
        mm = _mla_masks()
        v = v_ref[...]
        for pr in range(PP):
            ps = slice(pr * LANES, (pr + 1) * LANES)
            qc = jnp.concatenate([qn_ref[:, ps], qr_ref[:, ps]], axis=1)
            kc_s[:, pr * CW:(pr + 1) * CW] = jnp.concatenate([kn_ref[:, ps], kr_ref[...]], axis=1)
            for e in range(2):
                h = 2 * pr + e
                q_s[:, h * CW:(h + 1) * CW] = _pick(mm[e], qc)
                v_s[:, h * LANES:(h + 1) * LANES] = _pick(hm[e], v[:, ps])
        vis = _causal_mask(TQ, False)

        def q_block(i, carry):
            q0 = pl.multiple_of(i * TQ, TQ)
            qs = [q_s[pl.ds(q0, TQ), h * CW:(h + 1) * CW] for h in range(NH)]

            def tile(k0, c, mask):
                ms, ls, accs = list(c[:NH]), list(c[NH:2 * NH]), list(c[2 * NH:])
                ss = [_dot_nt(qs[h], kc_s[pl.ds(k0, TQ), (h // 2) * CW:(h // 2 + 1) * CW]) * scale for h in range(NH)]
                if mask is not None:
                    ss = [jnp.where(mask, s, NEG_BIG) for s in ss]
                m_new = [jnp.maximum(ms[h], jnp.max(ss[h], axis=1, keepdims=True)) for h in range(NH)]
                alphas = [jnp.exp(ms[h] - m_new[h]) for h in range(NH)]
                prs = [jnp.exp(ss[h] - m_new[h]) for h in range(NH)]
                outs = [_dot(prs[h].astype(BF16), v_s[pl.ds(k0, TQ), h * LANES:(h + 1) * LANES]) for h in range(NH)]
                ls = [alphas[h] * ls[h] + jnp.sum(prs[h], axis=1, keepdims=True) for h in range(NH)]
                for pr in range(PP):
                    accs[pr] = accs[pr] * jnp.where(hm[0], alphas[2 * pr], alphas[2 * pr + 1]) + outs[2 * pr] + outs[2 * pr + 1]
                return tuple(m_new) + tuple(ls) + tuple(accs)

            neg = jnp.full((TQ, 1), NEG_BIG, F32)
            z1 = jnp.zeros((TQ, 1), F32)

            def k_block(j, c):
                return tile(pl.multiple_of(j * TQ, TQ), c, None)

            c = lax.fori_loop(0, i, k_block, (neg,) * NH + (z1,) * NH + (jnp.zeros((TQ, LANES), F32),) * PP)
            c = tile(q0, c, vis)
            for pr in range(PP):
                ps = slice(pr * LANES, (pr + 1) * LANES)
                m_a, m_b, l_a, l_b = c[2 * pr], c[2 * pr + 1], c[NH + 2 * pr], c[NH + 2 * pr + 1]
                o_ref[pl.ds(q0, TQ), ps] = c[2 * NH + pr] / jnp.where(hm[0], l_a, l_b)
                lse_ref[pl.ds(q0, TQ), ps] = jnp.where(hm[0], m_a + jnp.log(l_a), m_b + jnp.log(l_b))
            return carry

        lax.fori_loop(0, nq, q_block, 0)

    blk = lambda off: pl.BlockSpec((seq, W), lambda b, g: (b, off + g))
    out_blk = pl.BlockSpec((seq, W), lambda b, g: (b, g))
    return pl.pallas_call(
        body, name=name, grid=(B, nstep),
        in_specs=[blk(0), blk(nstep), blk(0), blk(nstep), pl.BlockSpec((seq, LANES), lambda b, g: (b, 0))],
        out_specs=[out_blk, out_blk],
        out_shape=[jax.ShapeDtypeStruct((T, MLA_W), F32), jax.ShapeDtypeStruct((T, MLA_W), F32)],
        scratch_shapes=[pltpu.VMEM((seq, NH * CW), BF16), pltpu.VMEM((seq, PP * CW), BF16), pltpu.VMEM((seq, NH * LANES), BF16)],
        compiler_params=_cparams(("parallel", "parallel")),
    )(qm, qm, kvm, kvm, krt)


def _mla_bwd(qm, kvm, krt, o, lse, do, *, seq, name):
    T = qm.shape[0]
    B = T // seq
    TQ = ATT_T
    nq = seq // TQ
    PP = ATT_PAIRS
    W = PP * LANES
    nstep = MLA_W // W
    NH = 2 * PP
    CW = 2 * LANES
    scale = MLA_QK ** -0.5

    def body(qn_ref, qr_ref, kn_ref, v_ref, kr_ref, o_ref, lse_ref, do_ref,
             dqn_ref, dqr_ref, dkn_ref, dv_ref, dkr_ref, q_s, kc_s, do_s, dkc_s, dv_s):
        hm = _head_masks()
        mm = _mla_masks()
        dof = do_ref[...]
        for pr in range(PP):
            ps = slice(pr * LANES, (pr + 1) * LANES)
            qc = jnp.concatenate([qn_ref[:, ps], qr_ref[:, ps]], axis=1)
            kc_s[:, pr * CW:(pr + 1) * CW] = jnp.concatenate([kn_ref[:, ps], kr_ref[...]], axis=1)
            for e in range(2):
                h = 2 * pr + e
                q_s[:, h * CW:(h + 1) * CW] = _pick(mm[e], qc)
                do_s[:, h * LANES:(h + 1) * LANES] = _pick(hm[e], dof[:, ps]).astype(BF16)
        dkc_s[...] = jnp.zeros_like(dkc_s)
        dv_s[...] = jnp.zeros_like(dv_s)
        vis = _causal_mask(TQ, False)

        def q_block(i, carry):
            q0 = pl.multiple_of(i * TQ, TQ)
            qs = [q_s[pl.ds(q0, TQ), h * CW:(h + 1) * CW] for h in range(NH)]
            dos = [do_s[pl.ds(q0, TQ), h * LANES:(h + 1) * LANES] for h in range(NH)]
            lse_t = lse_ref[pl.ds(q0, TQ), :]
            dd = do_ref[pl.ds(q0, TQ), :] * o_ref[pl.ds(q0, TQ), :]
            lses, ds_ = [], []
            for h in range(NH):
                ps = slice((h // 2) * LANES, (h // 2 + 1) * LANES)
                lses.append(_lane_value(lse_t[:, ps], (h % 2) * MLA_V))
                ds_.append(jnp.sum(_pick(hm[h % 2], dd[:, ps]), axis=1, keepdims=True))

            def tile(k0, c, mask):
                accs = list(c)
                kcs = [kc_s[pl.ds(k0, TQ), pr * CW:(pr + 1) * CW] for pr in range(PP)]
                vjs = [v_ref[pl.ds(k0, TQ), pr * LANES:(pr + 1) * LANES] for pr in range(PP)]
                ss = [_dot_nt(qs[h], kcs[h // 2]) * scale for h in range(NH)]
                dps = [_dot_nt(dos[h], vjs[h // 2]) for h in range(NH)]
                p_l, ds_l = [], []
                for h in range(NH):
                    pr_ = jnp.exp(ss[h] - lses[h])
                    if mask is not None:
                        pr_ = jnp.where(mask, pr_, 0.0)
                    p_l.append(pr_.astype(BF16))
                    ds_l.append((pr_ * (dps[h] - ds_[h]) * scale).astype(BF16))
                for h in range(NH):
                    accs[h] = accs[h] + _dot(ds_l[h], kcs[h // 2])
                for pr in range(PP):
                    ha, hb = 2 * pr, 2 * pr + 1
                    dkc_s[pl.ds(k0, TQ), pr * CW:(pr + 1) * CW] += _dot_tn(ds_l[ha], qs[ha]) + _dot_tn(ds_l[hb], qs[hb])
                    dv_s[pl.ds(k0, TQ), pr * LANES:(pr + 1) * LANES] += _dot_tn(p_l[ha], dos[ha]) + _dot_tn(p_l[hb], dos[hb])
                return tuple(accs)

            zc = jnp.zeros((TQ, CW), F32)

            def k_block(j, c):
                return tile(pl.multiple_of(j * TQ, TQ), c, None)

            c = lax.fori_loop(0, i, k_block, (zc,) * NH)
            c = tile(q0, c, vis)
            for pr in range(PP):
                ps = slice(pr * LANES, (pr + 1) * LANES)
                dq = _pick(mm[0], c[2 * pr]) + _pick(mm[1], c[2 * pr + 1])
                dqn_ref[pl.ds(q0, TQ), ps] = dq[:, :LANES].astype(BF16)
                dqr_ref[pl.ds(q0, TQ), ps] = dq[:, LANES:]
            return carry

        lax.fori_loop(0, nq, q_block, 0)
        dkr = dkc_s[:, LANES:CW]
        for pr in range(PP):
            dkn_ref[:, pr * LANES:(pr + 1) * LANES] = dkc_s[:, pr * CW:pr * CW + LANES].astype(BF16)
            if pr > 0:
                dkr = dkr + dkc_s[:, pr * CW + LANES:(pr + 1) * CW]
        dv_ref[...] = dv_s[...].astype(BF16)
        g = pl.program_id(1)

        @pl.when(g == 0)
        def _():
            dkr_ref[...] = dkr

        @pl.when(g > 0)
        def _():
            dkr_ref[...] += dkr

    blk = lambda off: pl.BlockSpec((seq, W), lambda b, g: (b, off + g))
    out_blk = pl.BlockSpec((seq, W), lambda b, g: (b, g))
    one_blk = pl.BlockSpec((seq, LANES), lambda b, g: (b, 0))
    return pl.pallas_call(
        body, name=name, grid=(B, nstep),
        in_specs=[blk(0), blk(nstep), blk(0), blk(nstep), one_blk, out_blk, out_blk, out_blk],
        out_specs=[out_blk, out_blk, out_blk, out_blk, one_blk],
        out_shape=[jax.ShapeDtypeStruct((T, MLA_W), BF16), jax.ShapeDtypeStruct((T, MLA_W), F32),
                   jax.ShapeDtypeStruct((T, MLA_W), BF16), jax.ShapeDtypeStruct((T, MLA_W), BF16),
                   jax.ShapeDtypeStruct((T, LANES), F32)],
        scratch_shapes=[pltpu.VMEM((seq, NH * CW), BF16), pltpu.VMEM((seq, PP * CW), BF16), pltpu.VMEM((seq, NH * LANES), BF16),
                        pltpu.VMEM((seq, PP * CW), F32), pltpu.VMEM((seq, W), F32)],
        compiler_params=_cparams(("parallel", "arbitrary")),
    )(qm, qm, kvm, kvm, krt, o, lse, do)


def _rope_tables(pos_ref, invf_ref):
    ang = pos_ref[...].astype(F32) * invf_ref[...]
    first = (_lane_iota() % MLA_ROPE) < (MLA_ROPE // 2)
    return jnp.cos(ang), jnp.sin(ang), first


def _rope_apply(x, cos, sin, first):
    rot = jnp.where(first, -pltpu.roll(x, LANES - MLA_ROPE // 2, 1), pltpu.roll(x, MLA_ROPE // 2, 1))
    return x * cos + rot * sin


def _rope_apply_t(dy, cos, sin, first):
    dys = dy * sin
    rot_t = jnp.where(first, pltpu.roll(dys, LANES - MLA_ROPE // 2, 1), -pltpu.roll(dys, MLA_ROPE // 2, 1))
    return dy * cos + rot_t


def _proj_uq_rope(p, g, wuq, pos, invf, *, tm, name):
    T = p.shape[0]
    ntile = MLA_W // LANES

    def body(x_ref, kr_ref, g_ref, w_ref, pos_ref, invf_ref, cq_ref, qm_ref, krt_ref):
        cos, sin, first = _rope_tables(pos_ref, invf_ref)
        hb = _rms(x_ref[...], g_ref[...]).astype(BF16)
        cq_ref[...] = hb
        q = jnp.dot(hb, w_ref[...], preferred_element_type=F32)
        qm_ref[:, :MLA_W] = q[:, :MLA_W].astype(BF16)
        for t in range(ntile):
            sl = slice(MLA_W + t * LANES, MLA_W + (t + 1) * LANES)
            qm_ref[:, sl] = _rope_apply(q[:, sl], cos, sin, first).astype(BF16)
        krt_ref[...] = _rope_apply(kr_ref[...], cos, sin, first).astype(BF16)

    return pl.pallas_call(
        body, name=name, grid=(T // tm,),
        in_specs=[pl.BlockSpec((tm, Q_LORA), lambda i: (i, P_CQ // Q_LORA)), pl.BlockSpec((tm, LANES), lambda i: (i, P_KRT // LANES)),
                  pl.BlockSpec((1, Q_LORA), lambda i: (0, 0)), pl.BlockSpec((Q_LORA, 2 * MLA_W), lambda i: (0, 0)),
                  pl.BlockSpec((tm, 1), lambda i: (i, 0)), pl.BlockSpec((1, LANES), lambda i: (0, 0))],
        out_specs=[pl.BlockSpec((tm, Q_LORA), lambda i: (i, 0)), pl.BlockSpec((tm, 2 * MLA_W), lambda i: (i, 0)),
                   pl.BlockSpec((tm, LANES), lambda i: (i, 0))],
        out_shape=[jax.ShapeDtypeStruct((T, Q_LORA), BF16), jax.ShapeDtypeStruct((T, 2 * MLA_W), BF16),
                   jax.ShapeDtypeStruct((T, LANES), BF16)],
        compiler_params=_cparams(("parallel",)),
    )(p, p, g, wuq, pos, invf)


def _d_proj_uq_rope(dqn, dqr, dkr, wuq, p, g, pos, invf, *, tm, name):
    T = dqn.shape[0]
    ntile = MLA_W // LANES

    def body(dqn_ref, dqr_ref, dkr_ref, w_ref, x_ref, g_ref, pos_ref, invf_ref, dqm_ref, dx_ref, dg_ref, dkr_o_ref):
        cos, sin, first = _rope_tables(pos_ref, invf_ref)
        dqm_ref[:, :MLA_W] = dqn_ref[...]
        for t in range(ntile):
            sl = slice(t * LANES, (t + 1) * LANES)
            dqm_ref[:, MLA_W + t * LANES:MLA_W + (t + 1) * LANES] = _rope_apply_t(dqr_ref[:, sl], cos, sin, first).astype(BF16)
        dkr_o_ref[...] = _rope_apply_t(dkr_ref[...], cos, sin, first).astype(BF16)
        dy = lax.dot_general(dqm_ref[...], w_ref[...], (((1,), (1,)), ((), ())), preferred_element_type=F32)
        dx, part = _rms_grad(dy, x_ref[...], g_ref[...])
        dx_ref[...] = dx.astype(BF16)
        _accumulate(dg_ref, part)

    half = pl.BlockSpec((tm, MLA_W), lambda i: (i, 0))
    tile = pl.BlockSpec((tm, LANES), lambda i: (i, 0))
    return pl.pallas_call(
        body, name=name, grid=(T // tm,),
        in_specs=[half, half, tile, pl.BlockSpec((Q_LORA, 2 * MLA_W), lambda i: (0, 0)),
                  pl.BlockSpec((tm, Q_LORA), lambda i: (i, P_CQ // Q_LORA)), pl.BlockSpec((1, Q_LORA), lambda i: (0, 0)),
                  pl.BlockSpec((tm, 1), lambda i: (i, 0)), pl.BlockSpec((1, LANES), lambda i: (0, 0))],
        out_specs=[pl.BlockSpec((tm, 2 * MLA_W), lambda i: (i, 0)), pl.BlockSpec((tm, Q_LORA), lambda i: (i, 0)),
                   pl.BlockSpec((1, Q_LORA), lambda i: (0, 0)), tile],
        out_shape=[jax.ShapeDtypeStruct((T, 2 * MLA_W), BF16), jax.ShapeDtypeStruct((T, Q_LORA), BF16),
                   jax.ShapeDtypeStruct((1, Q_LORA), F32), jax.ShapeDtypeStruct((T, LANES), BF16)],
        compiler_params=_cparams(("arbitrary",)),
    )(dqn, dqr, dkr, wuq, p, g, pos, invf)


def _d_proj_cat(pieces, b, x, g, *, tm, name, residual=None, col_block=0, out_dtype=F32):
    M = pieces[0].shape[0]
    widths = [pc.shape[1] for pc in pieces]
    K = sum(widths)
    C = b.shape[0]
    n = len(pieces)
    in_specs = [pl.BlockSpec((tm, w), lambda i: (i, 0)) for w in widths]
    in_specs += [pl.BlockSpec((C, K), lambda i: (0, 0)), pl.BlockSpec((tm, C), lambda i: (i, col_block)),
                 pl.BlockSpec((1, C), lambda i: (0, 0))]
    args = list(pieces) + [b, x, g]
    if residual is not None:
        in_specs.append(pl.BlockSpec((tm, C), lambda i: (i, 0)))
        args.append(residual)

    def body(*refs):
        b_ref, x_ref, g_ref = refs[n:n + 3]
        cat_ref, dx_ref, dg_ref = refs[-3:]
        off = 0
        for r, w in zip(refs[:n], widths):
            cat_ref[:, off:off + w] = r[...]
            off += w
        dy = lax.dot_general(cat_ref[...], b_ref[...], (((1,), (1,)), ((), ())), preferred_element_type=F32)
        dx, part = _rms_grad(dy, x_ref[...], g_ref[...])
        if residual is not None:
            dx = dx + refs[n + 3][...]
        dx_ref[...] = dx.astype(out_dtype)
        _accumulate(dg_ref, part)

    return pl.pallas_call(
        body, name=name, grid=(M // tm,), in_specs=in_specs,
        out_specs=[pl.BlockSpec((tm, K), lambda i: (i, 0)), pl.BlockSpec((tm, C), lambda i: (i, 0)),
                   pl.BlockSpec((1, C), lambda i: (0, 0))],
        out_shape=[jax.ShapeDtypeStruct((M, K), BF16), jax.ShapeDtypeStruct((M, C), out_dtype),
                   jax.ShapeDtypeStruct((1, C), F32)],
        compiler_params=_cparams(("arbitrary",)),
    )(*args)


def _heads_out(xa, xb, ga, gb, w, resid, *, tm, name):
    T, C = xa.shape
    N = w.shape[1]

    def body(xa_ref, xb_ref, ga_ref, gb_ref, w_ref, r_ref, oc_ref, o_ref):
        oc_ref[:, :C] = _rms(xa_ref[...], ga_ref[...]).astype(BF16)
        oc_ref[:, C:] = _rms(xb_ref[...], gb_ref[...]).astype(BF16)
        o_ref[...] = r_ref[...] + jnp.dot(oc_ref[...], w_ref[...], preferred_element_type=F32)

    row = pl.BlockSpec((tm, C), lambda i: (i, 0))
    gsp = pl.BlockSpec((1, C), lambda i: (0, 0))
    full = pl.BlockSpec((tm, N), lambda i: (i, 0))
    return pl.pallas_call(
        body, name=name, grid=(T // tm,),
        in_specs=[row, row, gsp, gsp, pl.BlockSpec((2 * C, N), lambda i: (0, 0)), full],
        out_specs=[pl.BlockSpec((tm, 2 * C), lambda i: (i, 0)), full],
        out_shape=[jax.ShapeDtypeStruct((T, 2 * C), BF16), jax.ShapeDtypeStruct((T, N), F32)],
        compiler_params=_cparams(("parallel",)),
    )(xa, xb, ga, gb, w, resid)


def _heads_out_bwd(dout, w, xa, xb, ga, gb, *, tm, name):
    T, C = xa.shape
    N = w.shape[1]

    def body(d_ref, w_ref, xa_ref, xb_ref, ga_ref, gb_ref, dxa_ref, dxb_ref, dga_ref, dgb_ref):
        dy = lax.dot_general(d_ref[...].astype(BF16), w_ref[...], (((1,), (1,)), ((), ())), preferred_element_type=F32)
        dxa, pa = _rms_grad(dy[:, :C], xa_ref[...], ga_ref[...])
        dxb, pb = _rms_grad(dy[:, C:], xb_ref[...], gb_ref[...])
        dxa_ref[...] = dxa
        dxb_ref[...] = dxb
        _accumulate(dga_ref, pa)
        _accumulate(dgb_ref, pb)

    row = pl.BlockSpec((tm, C), lambda i: (i, 0))
    gsp = pl.BlockSpec((1, C), lambda i: (0, 0))
    return pl.pallas_call(
        body, name=name, grid=(T // tm,),
        in_specs=[pl.BlockSpec((tm, N), lambda i: (i, 0)), pl.BlockSpec((2 * C, N), lambda i: (0, 0)), row, row, gsp, gsp],
        out_specs=[row, row, gsp, gsp],
        out_shape=[jax.ShapeDtypeStruct((T, C), F32), jax.ShapeDtypeStruct((T, C), F32),
                   jax.ShapeDtypeStruct((1, C), F32), jax.ShapeDtypeStruct((1, C), F32)],
        compiler_params=_cparams(("arbitrary",)),
    )(dout, w, xa, xb, ga, gb)


CONV_ROWS = 256
HALO = 8


def _conv_taps(w_ref):
    return w_ref[0:1, :], w_ref[1:2, :], w_ref[2:3, :]


def _conv_rows(cur, prev, w, bias):
    ext = jnp.concatenate([prev, cur], axis=0)
    u1 = pltpu.roll(ext, 1, 0)[HALO:]
    u2 = pltpu.roll(ext, 2, 0)[HALO:]
    return w[2] * cur + w[1] * u1 + w[0] * u2 + bias, u1, u2


def _conv_fwd(u, w, bias, *, seq, name):
    T = u.shape[0]
    B = T // seq
    W2 = 2 * FF_BLK

    def body(u_ref, w_ref, b_ref, a_ref):
        wv = _conv_taps(w_ref)
        bv = b_ref[...]
        for c in range(seq // CONV_ROWS):
            r0 = c * CONV_ROWS
            cur = u_ref[r0:r0 + CONV_ROWS, :]
            prev = u_ref[r0 - HALO:r0, :] if c > 0 else jnp.zeros((HALO, W2), F32)
            y, _, _ = _conv_rows(cur, prev, wv, bv)
            gc = y[:, :FF_BLK]
            a_ref[r0:r0 + CONV_ROWS, :] = (gc * (1.0 / (1.0 + jnp.exp(-gc))) * y[:, FF_BLK:]).astype(BF16)

    return pl.pallas_call(
        body, name=name, grid=(B, N_FF_BLK),
        in_specs=[pl.BlockSpec((seq, W2), lambda b, j: (b, j)), pl.BlockSpec((3, W2), lambda b, j: (0, j)),
                  pl.BlockSpec((1, W2), lambda b, j: (0, j))],
        out_specs=pl.BlockSpec((seq, FF_BLK), lambda b, j: (b, j)),
        out_shape=jax.ShapeDtypeStruct((T, D_FF), BF16),
        compiler_params=_cparams(("parallel", "parallel")),
    )(u, w, bias)


def _conv_bwd(u, da, w, bias, *, seq, name):
    T = u.shape[0]
    B = T // seq
    W2 = 2 * FF_BLK
    nchunk = seq // CONV_ROWS

    def body(u_ref, da_ref, w_ref, b_ref, du_ref, dw_ref, db_ref, duc_s):
        wv = _conv_taps(w_ref)
        bv = b_ref[...]
        zrow = jnp.zeros((1, W2), F32)
        dw0, dw1, dw2, dbs = zrow, zrow, zrow, zrow
        for c in range(nchunk):
            r0 = c * CONV_ROWS
            cur = u_ref[r0:r0 + CONV_ROWS, :]
            prev = u_ref[r0 - HALO:r0, :] if c > 0 else jnp.zeros((HALO, W2), F32)
            y, u1, u2 = _conv_rows(cur, prev, wv, bv)
            gc = y[:, :FF_BLK]
            vc = y[:, FF_BLK:]
            sg = 1.0 / (1.0 + jnp.exp(-gc))
            dav = da_ref[r0:r0 + CONV_ROWS, :]
            duc = jnp.concatenate([dav * vc * (sg * (1.0 + gc * (1.0 - sg))), dav * (gc * sg)], axis=1)
            duc_s[r0:r0 + CONV_ROWS, :] = duc
            dw0 = dw0 + jnp.sum(duc * u2, axis=0, keepdims=True)
            dw1 = dw1 + jnp.sum(duc * u1, axis=0, keepdims=True)
            dw2 = dw2 + jnp.sum(duc * cur, axis=0, keepdims=True)
            dbs = dbs + jnp.sum(duc, axis=0, keepdims=True)
        duc_s[seq:seq + HALO, :] = jnp.zeros((HALO, W2), F32)
        n_ext = CONV_ROWS + HALO
        for c in range(nchunk):
            r0 = c * CONV_ROWS
            ext = duc_s[r0:r0 + n_ext, :]
            s1 = pltpu.roll(ext, n_ext - 1, 0)[:CONV_ROWS]
            s2 = pltpu.roll(ext, n_ext - 2, 0)[:CONV_ROWS]
            du_ref[r0:r0 + CONV_ROWS, :] = (wv[2] * ext[:CONV_ROWS] + wv[1] * s1 + wv[0] * s2).astype(BF16)

        first = pl.program_id(1) == 0

        @pl.when(first)
        def _():
            dw_ref[0:1, :] = dw0
            dw_ref[1:2, :] = dw1
            dw_ref[2:3, :] = dw2
            db_ref[...] = dbs

        @pl.when(jnp.logical_not(first))
        def _():
            dw_ref[0:1, :] += dw0
            dw_ref[1:2, :] += dw1
            dw_ref[2:3, :] += dw2
            db_ref[...] += dbs

    return pl.pallas_call(
        body, name=name, grid=(N_FF_BLK, B),
        in_specs=[pl.BlockSpec((seq, W2), lambda j, b: (b, j)), pl.BlockSpec((seq, FF_BLK), lambda j, b: (b, j)),
                  pl.BlockSpec((3, W2), lambda j, b: (0, j)), pl.BlockSpec((1, W2), lambda j, b: (0, j))],
        out_specs=[pl.BlockSpec((seq, W2), lambda j, b: (b, j)), pl.BlockSpec((3, W2), lambda j, b: (0, j)),
                   pl.BlockSpec((1, W2), lambda j, b: (0, j))],
        out_shape=[jax.ShapeDtypeStruct((T, 2 * D_FF), BF16), jax.ShapeDtypeStruct((3, 2 * D_FF), F32),
                   jax.ShapeDtypeStruct((1, 2 * D_FF), F32)],
        scratch_shapes=[pltpu.VMEM((seq + HALO, W2), F32)],
        compiler_params=_cparams(("parallel", "arbitrary")),
    )(u, da, w, bias)


def _place():
    return lax.axis_index("x"), lax.axis_index("y"), lax.axis_index("c")


def _other_chips(x, y):
    return [(1 - x, y), (x, 1 - y), (1 - x, 1 - y)]


def _all_gather(vs, *, name):
    n = len(vs)

    def body(*refs):
        v_refs, out_refs = refs[:n], refs[n:2 * n]
        send_sems, recv_sems, local_sems = refs[2 * n:]
        x, y, c = _place()
        me, sibling = (x, y, c), (x, y, 1 - c)
        chips = _other_chips(x, y)

        def slab(a, px, py, pc):
            return out_refs[a].at[4 * px + 2 * py + pc]

        def copy(a, k, block, to, src=None):
            return pltpu.make_async_remote_copy(
                src_ref=slab(a, *block) if src is None else src, dst_ref=slab(a, *block),
                send_sem=send_sems.at[7 * a + k], recv_sem=recv_sems.at[7 * a + k], device_id=to, device_id_type=MESH)

        mine = [pltpu.make_async_copy(v_refs[a], slab(a, *me), local_sems.at[a]) for a in range(n)]
        for cp in mine:
            cp.start()
        first = []
        for a in range(n):
            first.append(copy(a, 0, me, sibling, src=v_refs[a]))
            first += [copy(a, 1 + j, me, (*chip, c), src=v_refs[a]) for j, chip in enumerate(chips)]
        for cp in first:
            cp.start()
        passed = []
        for j, chip in enumerate(chips):
            for a in range(n):
                copy(a, 1 + j, (*chip, c), me).wait_recv()
                cp = copy(a, 4 + j, (*chip, c), sibling)
                cp.start()
                passed.append(cp)
        for a in range(n):
            copy(a, 0, sibling, me).wait_recv()
            for j, chip in enumerate(chips):
                copy(a, 4 + j, (*chip, 1 - c), me).wait_recv()
        for cp in first + passed:
            cp.wait_send()
        for cp in mine:
            cp.wait()

    return pl.pallas_call(
        body, name=name, in_specs=[ANY] * n, out_specs=[ANY] * n,
        out_shape=[jax.ShapeDtypeStruct((N_DEV,) + v.shape, v.dtype) for v in vs],
        scratch_shapes=[pltpu.SemaphoreType.DMA((7 * n,)), pltpu.SemaphoreType.DMA((7 * n,)), pltpu.SemaphoreType.DMA((n,))],
    )(*vs)


def _all_gather_async(vs, *, name, collective_id):
    n = len(vs)
    v_refs = [jax.new_ref(v, memory_space=pltpu.MemorySpace.HBM) for v in vs]
    out_refs = [jax.empty_ref(jax.ShapeDtypeStruct((N_DEV,) + v.shape, v.dtype), memory_space=pltpu.MemorySpace.HBM)
                for v in vs]

    @pl.kernel(mesh=plsc.ScalarSubcoreMesh(axis_name="seq", num_cores=1), name=name,
               scratch_types=(pltpu.SemaphoreType.DMA((7 * n,)), pltpu.SemaphoreType.DMA((7 * n,)),
                              pltpu.SemaphoreType.DMA((n,))),
               compiler_params=pltpu.CompilerParams(collective_id=collective_id))
    def launch(send_sems, recv_sems, local_sems):
        x, y, c = _place()
        me, sibling = (x, y, c), (x, y, 1 - c)
        chips = _other_chips(x, y)
        peers = [sibling] + [(*chip, c) for chip in chips]
        barrier = pltpu.get_barrier_semaphore()
        for peer in peers:
            pl.semaphore_signal(barrier, inc=1, device_id=peer, device_id_type=MESH)
        pl.semaphore_wait(barrier, len(peers))

        def slab(a, px, py, pc):
            return out_refs[a].at[4 * px + 2 * py + pc]

        def copy(a, k, block, to, src=None):
            return pltpu.make_async_remote_copy(
                src_ref=slab(a, *block) if src is None else src, dst_ref=slab(a, *block),
                send_sem=send_sems.at[7 * a + k], recv_sem=recv_sems.at[7 * a + k], device_id=to, device_id_type=MESH)

        mine = [pltpu.make_async_copy(v_refs[a], slab(a, *me), local_sems.at[a]) for a in range(n)]
        for cp in mine:
            cp.start()
        first = []
        for a in range(n):
            first.append(copy(a, 0, me, sibling, src=v_refs[a]))
            first += [copy(a, 1 + j, me, (*chip, c), src=v_refs[a]) for j, chip in enumerate(chips)]
        for cp in first:
            cp.start()
        passed = []
        for j, chip in enumerate(chips):
            for a in range(n):
                copy(a, 1 + j, (*chip, c), me).wait_recv()
                cp = copy(a, 4 + j, (*chip, c), sibling)
                cp.start()
                passed.append(cp)
        for a in range(n):
            copy(a, 0, sibling, me).wait_recv()
            for j, chip in enumerate(chips):
                copy(a, 4 + j, (*chip, 1 - c), me).wait_recv()
        for cp in first + passed:
            cp.wait_send()
        for cp in mine:
            cp.wait()

    launch()
    return [r[...] for r in out_refs]


def _handshake(peers):
    barrier = pltpu.get_barrier_semaphore()
    for peer in peers:
        pl.semaphore_signal(barrier, inc=1, device_id=peer, device_id_type=MESH)
    pl.semaphore_wait(barrier, len(peers))


def _hbm_refs(arrays, lead):
    src = [jax.new_ref(a, memory_space=pltpu.MemorySpace.HBM) for a in arrays]
    dst = [jax.empty_ref(jax.ShapeDtypeStruct((lead,) + a.shape[1:], a.dtype), memory_space=pltpu.MemorySpace.HBM)
           for a in arrays]
    return src, dst


def _rs_sibling_async(g8s, *, name, collective_id):
    n = len(g8s)
    g_refs, out_refs = _hbm_refs(g8s, 4)

    @pl.kernel(mesh=plsc.ScalarSubcoreMesh(axis_name="seq", num_cores=1), name=name,
               scratch_types=(pltpu.SemaphoreType.DMA((4 * n,)), pltpu.SemaphoreType.DMA((4 * n,))),
               compiler_params=pltpu.CompilerParams(collective_id=collective_id))
    def launch(send_sems, recv_sems):
        x, y, c = _place()
        _handshake([(x, y, 1 - c)])
        copies = [
            pltpu.make_async_remote_copy(
                src_ref=g_refs[a].at[2 * k + 1 - c], dst_ref=out_refs[a].at[k],
                send_sem=send_sems.at[4 * a + k], recv_sem=recv_sems.at[4 * a + k],
                device_id=(x, y, 1 - c), device_id_type=MESH)
            for a in range(n) for k in range(4)]
        for cp in copies:
            cp.start()
        for cp in copies:
            cp.wait()

    launch()
    return [r[...] for r in out_refs]


def _rs_chips_async(h4s, *, name, collective_id):
    n = len(h4s)
    h_refs, out_refs = _hbm_refs(h4s, 3)

    @pl.kernel(mesh=plsc.ScalarSubcoreMesh(axis_name="seq", num_cores=1), name=name,
               scratch_types=(pltpu.SemaphoreType.DMA((3 * n,)), pltpu.SemaphoreType.DMA((3 * n,))),
               compiler_params=pltpu.CompilerParams(collective_id=collective_id))
    def launch(send_sems, recv_sems):
        x, y, c = _place()
        chips = _other_chips(x, y)
        _handshake([(cx, cy, c) for cx, cy in chips])
        copies = [
            pltpu.make_async_remote_copy(
                src_ref=h_refs[a].at[2 * cx + cy], dst_ref=out_refs[a].at[j],
                send_sem=send_sems.at[3 * a + j], recv_sem=recv_sems.at[3 * a + j],
                device_id=(cx, cy, c), device_id_type=MESH)
            for a in range(n) for j, (cx, cy) in enumerate(chips)]
        for cp in copies:
            cp.start()
        for cp in copies:
            cp.wait()

    launch()
    return [r[...] for r in out_refs]


def _peer(x, y, c, k):
    return ((1 - x) if k & 4 else x, (1 - y) if k & 2 else y, (1 - c) if k & 1 else c)


def _rs_direct_async(g8s, *, name, collective_id):
    n = len(g8s)
    g_refs, out_refs = _hbm_refs(g8s, N_DEV - 1)

    @pl.kernel(mesh=plsc.ScalarSubcoreMesh(axis_name="seq", num_cores=1), name=name,
               scratch_types=(pltpu.SemaphoreType.DMA((7 * n,)), pltpu.SemaphoreType.DMA((7 * n,))),
               compiler_params=pltpu.CompilerParams(collective_id=collective_id))
    def launch(send_sems, recv_sems):
        x, y, c = _place()
        peers = [_peer(x, y, c, k) for k in range(1, N_DEV)]
        _handshake(peers)
        copies = [
            pltpu.make_async_remote_copy(
                src_ref=g_refs[a].at[4 * px + 2 * py + pc], dst_ref=out_refs[a].at[k],
                send_sem=send_sems.at[7 * a + k], recv_sem=recv_sems.at[7 * a + k],
                device_id=(px, py, pc), device_id_type=MESH)
            for a in range(n) for k, (px, py, pc) in enumerate(peers)]
        for cp in copies:
            cp.start()
        for cp in copies:
            cp.wait()

    launch()
    return [r[...] for r in out_refs]


def _row_tile(rows):
    return rows if rows <= 512 else 256


def _rs_chip_sum(g8, from_sibling, place_idx, *, name):
    _, R, C = g8.shape
    tr = _row_tile(R)

    def body(pi_ref, a_ref, b_ref, f_ref, h_ref):
        s = a_ref[...] + b_ref[...]
        h_ref[...] = s.astype(BF16)

        @pl.when(pl.program_id(1) == pi_ref[1])
        def _():
            f_ref[...] = s

    blk = pl.BlockSpec((None, tr, C), lambda r, k, pi_ref: (k, r, 0))
    return pl.pallas_call(
        body, name=name,
        grid_spec=pltpu.PrefetchScalarGridSpec(
            num_scalar_prefetch=1, grid=(R // tr, 4),
            in_specs=[pl.BlockSpec((None, tr, C), lambda r, k, pi_ref: (2 * k + pi_ref[0], r, 0)), blk],
            out_specs=[pl.BlockSpec((tr, C), lambda r, k, pi_ref: (r, 0)), blk]),
        out_shape=[jax.ShapeDtypeStruct((R, C), F32), jax.ShapeDtypeStruct((4, R, C), BF16)],
        compiler_params=_cparams(("parallel", "arbitrary")),
    )(place_idx, g8, from_sibling)


def _split_moves(segments, chunk):
    moves = []
    for dst, src, length in segments:
        while length > 0:
            dev, off = divmod(src, chunk)
            take = min(length, chunk - off)
            moves.append((dst, dev, off, take))
            dst, src, length = dst + take, src + take, length - take
    return moves


def _assemble(stacked, segments, zero_spans, out_cols, *, name):
    _, R, c = stacked.shape
    tr = _row_tile(R)
    moves = _split_moves(segments, c)

    def body(x_ref, o_ref):
        for dst, dev, off, take in moves:
            o_ref[:, dst:dst + take] = x_ref[dev, :, off:off + take]
        for a, b in zero_spans:
            o_ref[:, a:b] = jnp.zeros((tr, b - a), o_ref.dtype)

    return pl.pallas_call(
        body, name=name, grid=(R // tr,),
        in_specs=[pl.BlockSpec((N_DEV, tr, c), lambda i: (0, i, 0))],
        out_specs=pl.BlockSpec((tr, out_cols), lambda i: (i, 0)),
        out_shape=jax.ShapeDtypeStruct((R, out_cols), stacked.dtype),
        compiler_params=_cparams(("parallel",)),
    )(stacked)


def _disassemble(full, segments, chunk, *, name, out_dtype=F32):
    R = full.shape[0]
    tr = _row_tile(R)
    moves = _split_moves(segments, chunk)

    def body(x_ref, o_ref):
        seen = set()
        for dst, dev, off, take in moves:
            piece = x_ref[:, dst:dst + take]
            if (dev, off) in seen:
                piece = piece + o_ref[dev, :, off:off + take]
            seen.add((dev, off))
            o_ref[dev, :, off:off + take] = piece.astype(out_dtype)

    return pl.pallas_call(
        body, name=name, grid=(R // tr,),
        in_specs=[pl.BlockSpec((tr, full.shape[1]), lambda i: (i, 0))],
        out_specs=pl.BlockSpec((N_DEV, tr, chunk), lambda i: (0, i, 0)),
        out_shape=jax.ShapeDtypeStruct((N_DEV, R, chunk), out_dtype),
        compiler_params=_cparams(("parallel",)),
    )(full)


_O_CQ = 3 * SB_W
_O_CKV = _O_CQ + Q_LORA
_O_KR = _O_CKV + KV_LORA
SEG_W_IN = ((0, 0, 3 * SB_W), (P_CKV, _O_CKV, KV_LORA), (P_KRT, _O_KR, MLA_ROPE), (P_KRT + MLA_ROPE, _O_KR, MLA_ROPE),
            (P_CQ, _O_CQ, Q_LORA))
ZERO_W_IN = ((P_KRT + 2 * MLA_ROPE, P_CQ),)
SEG_W_UQ = tuple((MLA_NOPE * h, MLA_QK * h, MLA_NOPE) for h in range(MLA_HEADS)) + tuple(
    (MLA_W + LANES * (h // 2) + MLA_ROPE * (h % 2), MLA_QK * h + MLA_NOPE, MLA_ROPE) for h in range(MLA_HEADS))
ZERO_W_UQ = tuple((MLA_W + LANES * g + 2 * MLA_ROPE, MLA_W + LANES * (g + 1)) for g in range(MLA_HEADS // 2))
SEG_W_UKV = tuple((MLA_NOPE * h, (MLA_NOPE + MLA_V) * h, MLA_NOPE) for h in range(MLA_HEADS)) + tuple(
    (MLA_W + MLA_V * h, (MLA_NOPE + MLA_V) * h + MLA_NOPE, MLA_V) for h in range(MLA_HEADS))
SEG_W_UP = tuple((2 * FF_BLK * blk + FF_BLK * half, D_FF * half + FF_BLK * blk, FF_BLK)
                 for half in range(2) for blk in range(N_FF_BLK))


def _sum8(g, *, name):
    _, R, C = g.shape

    def body(g_ref, o_ref):
        acc = g_ref[0]
        for k in range(1, N_DEV):
            acc = acc + g_ref[k]
        o_ref[...] = acc

    return pl.pallas_call(
        body, name=name, out_shape=jax.ShapeDtypeStruct((R, C), F32),
    )(g)


def _adamw_math(w, gf, m, v):
    c1 = 1.0 / (1.0 - ADAM_B1 ** ADAM_STEP)
    c2 = 1.0 / (1.0 - ADAM_B2 ** ADAM_STEP)
    mn = ADAM_B1 * m + (1.0 - ADAM_B1) * gf
    vn = ADAM_B2 * v + (1.0 - ADAM_B2) * (gf * gf)
    return -ADAM_LR * ((mn * c1) / (jnp.sqrt(vn * c2) + ADAM_EPS) + ADAM_WD * w), mn, vn


def _adamw(w, g, m, v, *, name):
    R, C = w.shape
    tr = _row_tile(R)

    def body(w_ref, g_ref, m_ref, v_ref, d_ref, mo_ref, vo_ref):
        d_ref[...], mo_ref[...], vo_ref[...] = _adamw_math(w_ref[...], g_ref[...], m_ref[...], v_ref[...])

    blk = pl.BlockSpec((tr, C), lambda i: (i, 0))
    shp = jax.ShapeDtypeStruct((R, C), F32)
    return pl.pallas_call(
        body, name=name, grid=(R // tr,), in_specs=[blk] * 4, out_specs=[blk] * 3,
        out_shape=[shp, shp, shp], compiler_params=_cparams(("parallel",)),
    )(w, g, m, v)


def _adamw_rs8(g8, r7, me_idx, w, m, v, *, name):
    R, C = w.shape
    tr = _row_tile(R)

    def body(i_ref, f_ref, r_ref, w_ref, m_ref, v_ref, g_ref, d_ref, mo_ref, vo_ref):
        gf = f_ref[...].astype(F32)
        for k in range(N_DEV - 1):
            gf = gf + r_ref[k].astype(F32)
        g_ref[...] = gf
        d_ref[...], mo_ref[...], vo_ref[...] = _adamw_math(w_ref[...], gf, m_ref[...], v_ref[...])

    blk = pl.BlockSpec((tr, C), lambda i, i_ref: (i, 0))
    shp = jax.ShapeDtypeStruct((R, C), F32)
    return pl.pallas_call(
        body, name=name,
        grid_spec=pltpu.PrefetchScalarGridSpec(
            num_scalar_prefetch=1, grid=(R // tr,),
            in_specs=[pl.BlockSpec((None, tr, C), lambda i, i_ref: (i_ref[0], i, 0)),
                      pl.BlockSpec((N_DEV - 1, tr, C), lambda i, i_ref: (0, i, 0)), blk, blk, blk],
            out_specs=[blk] * 4),
        out_shape=[shp] * 4, compiler_params=_cparams(("parallel",)),
    )(me_idx, g8, r7, w, m, v)


def _adamw_rs(own, r3, w, m, v, *, name):
    R, C = w.shape
    tr = _row_tile(R)

    def body(f_ref, r_ref, w_ref, m_ref, v_ref, g_ref, d_ref, mo_ref, vo_ref):
        gf = ((f_ref[...] + r_ref[0].astype(F32)) + r_ref[1].astype(F32)) + r_ref[2].astype(F32)
        g_ref[...] = gf
        d_ref[...], mo_ref[...], vo_ref[...] = _adamw_math(w_ref[...], gf, m_ref[...], v_ref[...])

    blk = pl.BlockSpec((tr, C), lambda i: (i, 0))
    shp = jax.ShapeDtypeStruct((R, C), F32)
    return pl.pallas_call(
        body, name=name, grid=(R // tr,),
        in_specs=[blk, pl.BlockSpec((3, tr, C), lambda i: (0, i, 0)), blk, blk, blk], out_specs=[blk] * 4,
        out_shape=[shp] * 4, compiler_params=_cparams(("parallel",)),
    )(own, r3, w, m, v)


def _ff_interleave(a):
    lead = a.shape[:-1]
    return a.reshape(*lead, 2, N_FF_BLK, FF_BLK).swapaxes(-3, -2).reshape(*lead, 2 * D_FF)


def _ff_deinterleave(a):
    lead = a.shape[:-1]
    return a.reshape(*lead, N_FF_BLK, 2, FF_BLK).swapaxes(-3, -2).reshape(*lead, 2 * D_FF)


SMALL =(("g_mix", D_MODEL), ("g_cq", Q_LORA), ("g_ckv", KV_LORA), ("g_sb_out", SB_W), ("g_mla_out", MLA_W),
         ("g_ffn", D_MODEL), ("conv_b", 2 * D_FF), ("g_final", D_MODEL))
SMALL_ROWS = 88


SMALL_USED = sum(size for _, size in SMALL)


def _pack_small(d, tail=None):
    parts = [d[n].reshape(-1) for n, _ in SMALL] + ([] if tail is None else [tail])
    flat = jnp.concatenate(parts)
    flat = jnp.pad(flat, (0, SMALL_ROWS * LANES - flat.shape[0]))
    return flat.reshape(SMALL_ROWS, LANES)


def _unpack_small(a):
    flat = a.reshape(-1)
    out, off = {}, 0
    for n, size in SMALL:
        out[n] = flat[off:off + size]
        off += size
    return out


def kernel(x, positions, g_mix, w_in, g_cq, w_uq, g_ckv, w_ukv, g_sb_out, g_mla_out, w_out, g_ffn, w_up, conv_w, conv_b, w_down, g_final, loss_target, m_g_mix, m_w_in, m_g_cq, m_w_uq, m_g_ckv, m_w_ukv, m_g_sb_out, m_g_mla_out, m_w_out, m_g_ffn, m_w_up, m_conv_w, m_conv_b, m_w_down, m_g_final, v_g_mix, v_w_in, v_g_cq, v_w_uq, v_g_ckv, v_w_ukv, v_g_sb_out, v_g_mla_out, v_w_out, v_g_ffn, v_w_up, v_conv_w, v_conv_b, v_w_down, v_g_final):
    B, S, D = x.shape
    T = B * S
    xf = x.reshape(T, D)
    tgt = loss_target.reshape(T, D)
    pos = positions.reshape(T, 1)
    half = MLA_ROPE // 2
    inv_freq = 1.0 / (ROPE_BASE ** (jnp.arange(half, dtype=F32) * (2.0 / MLA_ROPE)))
    invf = jnp.tile(inv_freq, LANES // half).reshape(1, LANES)
    place_idx = jnp.stack([lax.axis_index("c"), 2 * lax.axis_index("x") + lax.axis_index("y")]).astype(jnp.int32)
    me_idx = (4 * lax.axis_index("x") + 2 * lax.axis_index("y") + lax.axis_index("c")).astype(jnp.int32).reshape(1)

    names = ("w_in", "w_uq", "w_ukv", "w_out", "w_up", "w_down", "conv_w")
    shard = {"w_in": w_in[0], "w_uq": w_uq[0], "w_ukv": w_ukv[0], "w_out": w_out[0], "w_up": w_up[0],
             "w_down": w_down[0], "conv_w": conv_w[0]}
    sent = {n: shard[n] if n == "conv_w" else shard[n].astype(BF16) for n in names}
    later = names[1:]
    w_in_all = _all_gather([sent["w_in"]], name="ag_w_in")[0]
    w_in_all, rest = lax.optimization_barrier((w_in_all, [sent[n] for n in later]))
    got = {"w_in": w_in_all}
    got.update(zip(later, _all_gather_async(rest, name="ag_weights_async", collective_id=0)))
    wi = _assemble(got["w_in"], SEG_W_IN, ZERO_W_IN, P_COLS, name="asm_w_in")
    wuq = _assemble(got["w_uq"], SEG_W_UQ, ZERO_W_UQ, 2 * MLA_W, name="asm_w_uq")
    wukv = _assemble(got["w_ukv"], SEG_W_UKV, (), 2 * MLA_W, name="asm_w_ukv")
    wup = _assemble(got["w_up"], SEG_W_UP, (), 2 * D_FF, name="asm_w_up")
    cwi = _assemble(got["conv_w"], SEG_W_UP, (), 2 * D_FF, name="asm_conv_w")
    wo = got["w_out"].reshape(D, D)
    wdn = got["w_down"].reshape(D_FF, D)
    cbi = _ff_interleave(conv_b)

    h, p = _rms_matmul_nn(xf, g_mix, wi, tm=512, name="proj_in")
    o_sb, ltot = _sb_fwd(p, seq=S, name="sb_fwd")
    cq, qm, krt = _proj_uq_rope(p, g_cq, wuq, pos, invf, tm=512, name="proj_uq")
    ckv, kvm = _rms_matmul_nn(p, g_ckv, wukv, tm=512, name="proj_ukv", col_block=P_CKV // KV_LORA, out_dtype=BF16)
    o_mla, lse = _mla_fwd(qm, kvm, krt, seq=S, name="mla_fwd")
    ocat, x1 = _heads_out(o_sb, o_mla, g_sb_out, g_mla_out, wo, xf, tm=512, name="proj_out")
    hf, u = _rms_matmul_nn(x1, g_ffn, wup, tm=256, name="ffn_up")
    a = _conv_fwd(u, cwi, cbi, seq=S, name="conv_fwd")
    dx2, dg_final, loss_row = _matmul_nn_loss(a, wdn, x1, g_final.reshape(1, D), tgt, tm=512, name="ffn_down_loss")

    da = _matmul_nt(dx2, wdn, tm=1024, tn=D_FF // 2, out_dtype=F32, name="d_ffn_down")
    dw_down = _matmul_tn(a, dx2, tm=D_FF // 2, tn=1024, tk=1024, name="dw_down", out_dtype=BF16)
    du, dcw, dcb = _conv_bwd(u, da, cwi, cbi, seq=S, name="conv_bwd")
    dw_up = _matmul_tn(hf, du, tm=1024, tn=D_FF, tk=1024, name="dw_up")
    dx1, dg_ffn = _matmul_nt_rms_bwd(du, wup, x1, g_ffn, tm=512, name="d_ffn_up", residual=dx2)
    dw_out = _matmul_tn(ocat, dx1, tm=1024, tn=1024, tk=1024, name="dw_out", out_dtype=BF16)
    do_sb, do_mla, dg_sb, dg_mla = _heads_out_bwd(dx1, wo, o_sb, o_mla, g_sb_out, g_mla_out, tm=512, name="d_proj_out")

    early = ("w_down", "w_up", "conv_w", "w_out")
    g8 = {"w_up": _disassemble(dw_up, SEG_W_UP, shard["w_up"].shape[1], name="split_dw_up", out_dtype=BF16),
          "conv_w": _disassemble(dcw, SEG_W_UP, shard["conv_w"].shape[1], name="split_dconv_w", out_dtype=BF16),
          "w_out": dw_out.reshape((N_DEV,) + shard["w_out"].shape),
          "w_down": dw_down.reshape((N_DEV,) + shard["w_down"].shape)}
    r7 = dict(zip(early, _rs_direct_async([g8[n] for n in early], name="rs_direct_async", collective_id=1)))
    own, r3 = {}, {}

    dq_sb, dk_sb, dv_sb = _sb_bwd(p, ltot, do_sb, seq=S, name="sb_bwd")
    dqn, dqr, dkn, dvm, dkr = _mla_bwd(qm, kvm, krt, o_mla, lse, do_mla, seq=S, name="mla_bwd")
    dqm, dcq, dg_cq, dkr_u = _d_proj_uq_rope(dqn, dqr, dkr, wuq, p, g_cq, pos, invf, tm=512, name="d_proj_uq")
    dkvm, dckv, dg_ckv = _d_proj_cat([dkn, dvm], wukv, p, g_ckv, tm=512, name="d_proj_ukv",
                                     col_block=P_CKV // KV_LORA, out_dtype=BF16)
    dw_uq = _matmul_tn(cq, dqm, tm=Q_LORA, tn=1024, tk=1024, name="dw_uq")
    dw_ukv = _matmul_tn(ckv, dkvm, tm=KV_LORA, tn=1024, tk=1024, name="dw_ukv")
    dp, dx, dg_mix = _d_proj_cat([dq_sb, dk_sb, dv_sb, dckv, dkr_u, dcq], wi, xf, g_mix, tm=512, name="d_proj_in",
                                 residual=dx1)
    dw_in = _matmul_tn(h, dp, tm=1024, tn=P_COLS, tk=1024, name="dw_in")

    late = ("w_in", "w_uq", "w_ukv")
    g8.update({"w_in": _disassemble(dw_in, SEG_W_IN, shard["w_in"].shape[1], name="split_dw_in"),
               "w_uq": _disassemble(dw_uq, SEG_W_UQ, shard["w_uq"].shape[1], name="split_dw_uq"),
               "w_ukv": _disassemble(dw_ukv, SEG_W_UKV, shard["w_ukv"].shape[1], name="split_dw_ukv")})
    sib_l = _rs_sibling_async([g8[n] for n in late], name="rs_sibling_late", collective_id=3)

    params = {"w_in": (w_in, m_w_in, v_w_in), "w_uq": (w_uq, m_w_uq, v_w_uq), "w_ukv": (w_ukv, m_w_ukv, v_w_ukv),
              "w_out": (w_out, m_w_out, v_w_out), "w_up": (w_up, m_w_up, v_w_up), "conv_w": (conv_w, m_conv_w, v_conv_w),
              "w_down": (w_down, m_w_down, v_w_down)}
    grad, delta, new_m, new_v = {}, {}, {}, {}

    def adamw_group(group):
        for n in group:
            w_, m_, v_ = params[n]
            if n in r7:
                res = _adamw_rs8(g8[n], r7[n], me_idx, w_[0], m_[0], v_[0], name="adamw_" + n)
            else:
                res = _adamw_rs(own[n], r3[n], w_[0], m_[0], v_[0], name="adamw_" + n)
            grad[n], delta[n], new_m[n], new_v[n] = [r[None] for r in res]

    adamw_group(("w_up", "w_down"))
    sib_l, grad["w_up"] = lax.optimization_barrier((sib_l, grad["w_up"]))
    sums_l = [_rs_chip_sum(g8[n], fs, place_idx, name="rs_chip_sum_" + n) for n, fs in zip(late, sib_l)]
    r3.update(zip(late, _rs_chips_async([h4 for _, h4 in sums_l], name="rs_chips_late", collective_id=4)))
    own.update({n: f for n, (f, _) in zip(late, sums_l)})
    adamw_group(("conv_w", "w_out"))

    small_part = {"g_mix": dg_mix, "g_cq": dg_cq, "g_ckv": dg_ckv, "g_sb_out": dg_sb, "g_mla_out": dg_mla,
                  "g_ffn": dg_ffn, "conv_b": _ff_deinterleave(dcb), "g_final": dg_final}
    small_all, = _all_gather([_pack_small(small_part, tail=loss_row[0, 0:1])], name="ag_small_grads")
    gsmall = _sum8(small_all, name="sum_small_grads")
    adamw_group(late)
    small_w = {"g_mix": g_mix, "g_cq": g_cq, "g_ckv": g_ckv, "g_sb_out": g_sb_out, "g_mla_out": g_mla_out,
               "g_ffn": g_ffn, "conv_b": conv_b, "g_final": g_final}
    small_m = {"g_mix": m_g_mix, "g_cq": m_g_cq, "g_ckv": m_g_ckv, "g_sb_out": m_g_sb_out, "g_mla_out": m_g_mla_out,
               "g_ffn": m_g_ffn, "conv_b": m_conv_b, "g_final": m_g_final}
    small_v = {"g_mix": v_g_mix, "g_cq": v_g_cq, "g_ckv": v_g_ckv, "g_sb_out": v_g_sb_out, "g_mla_out": v_g_mla_out,
               "g_ffn": v_g_ffn, "conv_b": v_conv_b, "g_final": v_g_final}
    ds_, ms_, vs_ = _adamw(_pack_small(small_w), gsmall, _pack_small(small_m), _pack_small(small_v), name="adamw_small")
    for src, dst in ((_unpack_small(gsmall), grad), (_unpack_small(ds_), delta), (_unpack_small(ms_), new_m), (_unpack_small(vs_), new_v)):
        for n, _ in SMALL:
            dst[n] = src[n].reshape(small_w[n].shape)

    loss = gsmall.reshape(-1)[SMALL_USED]
    order = ("g_mix", "w_in", "g_cq", "w_uq", "g_ckv", "w_ukv", "g_sb_out", "g_mla_out", "w_out", "g_ffn", "w_up",
             "conv_w", "conv_b", "w_down", "g_final")
    return (loss, dx.reshape(B, S, D), *[grad[n] for n in order], *[delta[n] for n in order],
            *[new_m[n] for n in order], *[new_v[n] for n in order])
```

```python
import jax
import jax.numpy as jnp
from jax import lax
from jax.experimental import pallas as pl
from jax.experimental.pallas import tpu as pltpu
from jax.experimental.pallas import tpu_sc as plsc

F32 = jnp.float32
BF16 = jnp.bfloat16

D_MODEL = 1024
SB_HEADS = 8
SB_HEAD_DIM = 64
MLA_HEADS = 8
MLA_NOPE = 64
MLA_ROPE = 32
MLA_V = 64
Q_LORA = 384
KV_LORA = 256
D_FF = 2816
ROPE_BASE = 10000.0
EPS = 1e-6
SB_W = SB_HEADS * SB_HEAD_DIM
MLA_W = MLA_HEADS * MLA_V
MLA_QK = MLA_NOPE + MLA_ROPE

ADAM_LR = 0.001
ADAM_B1 = 0.9
ADAM_B2 = 0.999
ADAM_EPS = 1e-08
ADAM_WD = 0.01
ADAM_STEP = 10

N_DEV = 8
LANES = 128
V7X_VMEM_LIMIT = 56 * 1024 * 1024
FF_BLK = 256
N_FF_BLK = D_FF // FF_BLK

P_Q, P_K, P_V = 0, SB_W, 2 * SB_W
P_CKV = 3 * SB_W
P_KRT = P_CKV + KV_LORA
P_CQ = P_KRT + LANES
P_COLS = P_CQ + Q_LORA

MESH = pl.DeviceIdType.MESH
ANY = pl.BlockSpec(memory_space=pl.ANY)


def _cparams(sem=None, vmem=V7X_VMEM_LIMIT):
    return pltpu.CompilerParams(dimension_semantics=sem, vmem_limit_bytes=vmem)


def _matmul_nt(a, b, *, tm, tn, out_dtype, name):
    M, K = a.shape
    N = b.shape[0]
    assert M % tm == 0 and N % tn == 0, (name, a.shape, b.shape)

    def body(a_ref, b_ref, o_ref):
        acc = lax.dot_general(a_ref[...].astype(BF16), b_ref[...], (((1,), (1,)), ((), ())),
                              preferred_element_type=F32)
        o_ref[...] = acc.astype(out_dtype)

    return pl.pallas_call(
        body, name=name, grid=(M // tm, N // tn),
        in_specs=[pl.BlockSpec((tm, K), lambda i, j: (i, 0)), pl.BlockSpec((tn, K), lambda i, j: (j, 0))],
        out_specs=pl.BlockSpec((tm, tn), lambda i, j: (i, j)),
        out_shape=jax.ShapeDtypeStruct((M, N), out_dtype),
        compiler_params=_cparams(("parallel", "parallel")),
    )(a, b)


def _matmul_tn(a, b, *, tm, tn, tk, name, out_dtype=F32):
    K, M = a.shape
    N = b.shape[1]
    assert M % tm == 0 and N % tn == 0 and K % tk == 0, (name, a.shape, b.shape)
    n_k = K // tk
    narrow = out_dtype != F32

    def body(a_ref, b_ref, o_ref, *scratch):
        acc_ref = scratch[0] if narrow else o_ref
        k = pl.program_id(2)
        part = lax.dot_general(a_ref[...].astype(BF16), b_ref[...].astype(BF16), (((0,), (0,)), ((), ())),
                               preferred_element_type=F32)

        @pl.when(k == 0)
        def _():
            acc_ref[...] = part

        @pl.when(k > 0)
        def _():
            acc_ref[...] += part

        if narrow:
            @pl.when(k == n_k - 1)
            def _():
                o_ref[...] = acc_ref[...].astype(out_dtype)

    return pl.pallas_call(
        body, name=name, grid=(M // tm, N // tn, n_k),
        in_specs=[pl.BlockSpec((tk, tm), lambda i, j, k: (k, i)), pl.BlockSpec((tk, tn), lambda i, j, k: (k, j))],
        out_specs=pl.BlockSpec((tm, tn), lambda i, j, k: (i, j)),
        out_shape=jax.ShapeDtypeStruct((M, N), out_dtype),
        scratch_shapes=[pltpu.VMEM((tm, tn), F32)] if narrow else [],
        compiler_params=_cparams(("parallel", "parallel", "arbitrary")),
    )(a, b)


def _rms(xf, g):
    r = lax.rsqrt(jnp.mean(xf * xf, axis=1, keepdims=True) + EPS)
    return (xf * r) * g


def _rms_grad(dyf, xf, g):
    r = lax.rsqrt(jnp.mean(xf * xf, axis=1, keepdims=True) + EPS)
    xh = xf * r
    dyg = dyf * g
    dx = r * (dyg - xh * jnp.mean(dyg * xh, axis=1, keepdims=True))
    return dx, jnp.sum(dyf * xh, axis=0, keepdims=True)


def _accumulate(ref, part):
    @pl.when(pl.program_id(0) == 0)
    def _():
        ref[...] = part

    @pl.when(pl.program_id(0) > 0)
    def _():
        ref[...] += part


def _rms_matmul_nn(x, g, w, *, tm, name, col_block=0, out_dtype=F32):
    T = x.shape[0]
    C, N = w.shape
    assert T % tm == 0, (name, x.shape)

    def body(x_ref, g_ref, w_ref, h_ref, o_ref):
        hb = _rms(x_ref[...], g_ref[...]).astype(BF16)
        h_ref[...] = hb
        o_ref[...] = jnp.dot(hb, w_ref[...], preferred_element_type=F32).astype(out_dtype)

    return pl.pallas_call(
        body, name=name, grid=(T // tm,),
        in_specs=[pl.BlockSpec((tm, C), lambda i: (i, col_block)), pl.BlockSpec((1, C), lambda i: (0, 0)),
                  pl.BlockSpec((C, N), lambda i: (0, 0))],
        out_specs=[pl.BlockSpec((tm, C), lambda i: (i, 0)), pl.BlockSpec((tm, N), lambda i: (i, 0))],
        out_shape=[jax.ShapeDtypeStruct((T, C), BF16), jax.ShapeDtypeStruct((T, N), out_dtype)],
        compiler_params=_cparams(("parallel",)),
    )(x, g, w)


def _matmul_nt_rms_bwd(a, b, x, g, *, tm, name, residual=None, col_block=0, out_dtype=F32):
    M, K = a.shape
    C = b.shape[0]
    assert M % tm == 0, (name, a.shape)
    in_specs = [pl.BlockSpec((tm, K), lambda i: (i, 0)), pl.BlockSpec((C, K), lambda i: (0, 0)),
                pl.BlockSpec((tm, C), lambda i: (i, col_block)), pl.BlockSpec((1, C), lambda i: (0, 0))]
    args = [a, b, x, g]
    if residual is not None:
        in_specs.append(pl.BlockSpec((tm, C), lambda i: (i, 0)))
        args.append(residual)

    def body(*refs):
        a_ref, b_ref, x_ref, g_ref = refs[:4]
        dx_ref, dg_ref = refs[-2:]
        dy = lax.dot_general(a_ref[...].astype(BF16), b_ref[...], (((1,), (1,)), ((), ())), preferred_element_type=F32)
        dx, part = _rms_grad(dy, x_ref[...], g_ref[...])
        if residual is not None:
            dx = dx + refs[4][...]
        dx_ref[...] = dx.astype(out_dtype)
        _accumulate(dg_ref, part)

    return pl.pallas_call(
        body, name=name, grid=(M // tm,), in_specs=in_specs,
        out_specs=[pl.BlockSpec((tm, C), lambda i: (i, 0)), pl.BlockSpec((1, C), lambda i: (0, 0))],
        out_shape=[jax.ShapeDtypeStruct((M, C), out_dtype), jax.ShapeDtypeStruct((1, C), F32)],
        compiler_params=_cparams(("arbitrary",)),
    )(*args)


def _matmul_nn_loss(a, w, x1, g, tgt, *, tm, name):
    M, K = a.shape
    C = w.shape[1]
    assert M % tm == 0, (name, a.shape)

    def body(a_ref, w_ref, x_ref, g_ref, t_ref, dx_ref, dg_ref, loss_ref):
        xf = x_ref[...] + jnp.dot(a_ref[...], w_ref[...], preferred_element_type=F32)
        gf = g_ref[...]
        err = _rms(xf, gf) - t_ref[...]
        lpart = 0.5 * jnp.sum(jnp.mean(err * err, axis=1, keepdims=True), axis=0, keepdims=True)
        dx, gpart = _rms_grad(err * (1.0 / C), xf, gf)
        dx_ref[...] = dx
        _accumulate(dg_ref, gpart)
        _accumulate(loss_ref, jnp.broadcast_to(lpart, (1, LANES)))

    row = pl.BlockSpec((tm, C), lambda i: (i, 0))
    return pl.pallas_call(
        body, name=name, grid=(M // tm,),
        in_specs=[pl.BlockSpec((tm, K), lambda i: (i, 0)), pl.BlockSpec((K, C), lambda i: (0, 0)), row,
                  pl.BlockSpec((1, C), lambda i: (0, 0)), row],
        out_specs=[row, pl.BlockSpec((1, C), lambda i: (0, 0)), pl.BlockSpec((1, LANES), lambda i: (0, 0))],
        out_shape=[jax.ShapeDtypeStruct((M, C), F32), jax.ShapeDtypeStruct((1, C), F32),
                   jax.ShapeDtypeStruct((1, LANES), F32)],
        compiler_params=_cparams(("arbitrary",)),
    )(a, w, x1, g, tgt)


ATT_T = 256
ATT_PAIRS = 2
NEG_BIG = -1e30


def _lane_iota():
    return lax.broadcasted_iota(jnp.int32, (1, LANES), 1)


def _head_masks():
    first = _lane_iota() < SB_HEAD_DIM
    return first, jnp.logical_not(first)


def _pick(mask, x):
    return jnp.where(mask, x, jnp.zeros_like(x))


def _lane_value(t, lane):
    return jnp.sum(jnp.where(_lane_iota() == lane, t, 0.0), axis=1, keepdims=True)


def _split_hi_lo(x):
    hi = x.astype(BF16)
    lo = (x - hi.astype(F32)).astype(BF16)
    return jnp.concatenate([hi, lo], axis=1)


def _tri(n, kind):
    r = lax.broadcasted_iota(jnp.int32, (n, n), 0)
    c = lax.broadcasted_iota(jnp.int32, (n, n), 1)
    u = {"suffix_excl": r > c, "prefix_incl": r <= c, "prefix_excl": r < c}[kind].astype(BF16)
    return jnp.concatenate([u, u], axis=0)


def _dot_nt(a, b):
    return lax.dot_general(a, b, (((1,), (1,)), ((), ())), preferred_element_type=F32)


def _dot_tn(a, b):
    return lax.dot_general(a, b, (((0,), (0,)), ((), ())), preferred_element_type=F32)


def _dot(a, b):
    return jnp.dot(a, b, preferred_element_type=F32)


def _causal_mask(n, strict):
    r = lax.broadcasted_iota(jnp.int32, (n, n), 0)
    c = lax.broadcasted_iota(jnp.int32, (n, n), 1)
    return (c < r) if strict else (c <= r)


LOG2E = 1.4426950408889634


def _sb_logs(qh, kj, vis):
    z2 = _dot_nt(qh, kj) * LOG2E
    nk = jnp.maximum(z2, 0.0) + jnp.log2(1.0 + jnp.exp2(-jnp.abs(z2)))
    lb = z2 - nk
    if vis is not None:
        nk = jnp.where(vis, nk, 0.0)
    return lb, nk


def _sb_fwd(p, *, seq, name):
    T = p.shape[0]
    B = T // seq
    TQ = ATT_T
    nq = seq // TQ
    PP = ATT_PAIRS
    W = PP * LANES
    nstep = SB_W // W
    NH = 2 * PP

    def body(q_ref, k_ref, v_ref, o_ref, lt_ref, q_s, k_s, v_s):
        masks = _head_masks()
        q = q_ref[...] * (SB_HEAD_DIM ** -0.5)
        v = v_ref[...]
        k_s[...] = k_ref[...].astype(BF16)
        for h in range(NH):
            ps = slice((h // 2) * LANES, (h // 2 + 1) * LANES)
            hs = slice(h * LANES, (h + 1) * LANES)
            q_s[:, hs] = _pick(masks[h % 2], q[:, ps]).astype(BF16)
            v_s[:, hs] = _pick(masks[h % 2], v[:, ps]).astype(BF16)
        u_suf = _tri(TQ, "suffix_excl")
        vis = _causal_mask(TQ, True)

        def q_block(i, carry):
            q0 = pl.multiple_of(i * TQ, TQ)
            qs = [q_s[pl.ds(q0, TQ), h * LANES:(h + 1) * LANES] for h in range(NH)]

            def tile(k0, c, mask):
                rs, accs = list(c[:NH]), list(c[NH:])
                logs = [_sb_logs(qs[h], k_s[pl.ds(k0, TQ), (h // 2) * LANES:(h // 2 + 1) * LANES], mask) for h in range(NH)]
                sums = [_dot(_split_hi_lo(nk), u_suf) for _, nk in logs]
                for h in range(NH):
                    a = jnp.exp2(logs[h][0] - sums[h] - rs[h])
                    if mask is not None:
                        a = jnp.where(mask, a, 0.0)
                    accs[h // 2] = accs[h // 2] + _dot(a.astype(BF16), v_s[pl.ds(k0, TQ), h * LANES:(h + 1) * LANES])
                    rs[h] = rs[h] + jnp.sum(logs[h][1], axis=1, keepdims=True)
                return tuple(rs) + tuple(accs)

            zero = jnp.zeros((TQ, 1), F32)
            c = tile(q0, (zero,) * NH + (jnp.zeros((TQ, LANES), F32),) * PP, vis)

            def k_block(jj, c):
                return tile(pl.multiple_of((i - 1 - jj) * TQ, TQ), c, None)

            c = lax.fori_loop(0, i, k_block, c)
            for pr in range(PP):
                ps = slice(pr * LANES, (pr + 1) * LANES)
                o_ref[pl.ds(q0, TQ), ps] = c[NH + pr]
                lt_ref[pl.ds(q0, TQ), ps] = jnp.where(masks[0], c[2 * pr], c[2 * pr + 1])
            return carry

        lax.fori_loop(0, nq, q_block, 0)

    blk = lambda off: pl.BlockSpec((seq, W), lambda b, g: (b, off + g))
    out_blk = pl.BlockSpec((seq, W), lambda b, g: (b, g))
    return pl.pallas_call(
        body, name=name, grid=(B, nstep),
        in_specs=[blk(P_Q // W), blk(P_K // W), blk(P_V // W)],
        out_specs=[out_blk, out_blk],
        out_shape=[jax.ShapeDtypeStruct((T, SB_W), F32), jax.ShapeDtypeStruct((T, SB_W), F32)],
        scratch_shapes=[pltpu.VMEM((seq, NH * LANES), BF16), pltpu.VMEM((seq, W), BF16), pltpu.VMEM((seq, NH * LANES), BF16)],
        compiler_params=_cparams(("parallel", "parallel")),
    )(p, p, p)


def _sb_bwd(p, ltot, do, *, seq, name):
    T = p.shape[0]
    B = T // seq
    TQ = ATT_T
    nq = seq // TQ
    PP = ATT_PAIRS
    W = PP * LANES
    nstep = SB_W // W
    NH = 2 * PP
    scale = SB_HEAD_DIM ** -0.5

    def body(q_ref, k_ref, v_ref, lt_ref, do_ref, dq_ref, dk_ref, dv_ref, q_s, k_s, v_s, do_s, dk_s, dv_s):
        masks = _head_masks()
        q = q_ref[...] * scale
        dof = do_ref[...]
        k_s[...] = k_ref[...].astype(BF16)
        v_s[...] = v_ref[...].astype(BF16)
        for h in range(NH):
            ps = slice((h // 2) * LANES, (h // 2 + 1) * LANES)
            hs = slice(h * LANES, (h + 1) * LANES)
            q_s[:, hs] = _pick(masks[h % 2], q[:, ps]).astype(BF16)
            do_s[:, hs] = _pick(masks[h % 2], dof[:, ps]).astype(BF16)
        dk_s[...] = jnp.zeros_like(dk_s)
        dv_s[...] = jnp.zeros_like(dv_s)
        u_pin = _tri(TQ, "prefix_incl")
        u_pex = _tri(TQ, "prefix_excl")[:TQ]
        vis = _causal_mask(TQ, True)

        def q_block(i, carry):
            q0 = pl.multiple_of(i * TQ, TQ)
            qs = [q_s[pl.ds(q0, TQ), h * LANES:(h + 1) * LANES] for h in range(NH)]
            dos = [do_s[pl.ds(q0, TQ), h * LANES:(h + 1) * LANES] for h in range(NH)]
            lt = lt_ref[pl.ds(q0, TQ), :]
            lts = [_lane_value(lt[:, (h // 2) * LANES:(h // 2 + 1) * LANES], (h % 2) * SB_HEAD_DIM) for h in range(NH)]

            def tile(k0, c, mask):
                cs, gs, accs = list(c[:NH]), list(c[NH:2 * NH]), list(c[2 * NH:])
                kjs = [k_s[pl.ds(k0, TQ), pr * LANES:(pr + 1) * LANES] for pr in range(PP)]
                vjs = [v_s[pl.ds(k0, TQ), pr * LANES:(pr + 1) * LANES] for pr in range(PP)]
                logs = [_sb_logs(qs[h], kjs[h // 2], mask) for h in range(NH)]
                pins = [_dot(_split_hi_lo(nk), u_pin) for _, nk in logs]
                das = [_dot_nt(dos[h], vjs[h // 2]) for h in range(NH)]
                a_l, g_l = [], []
                for h in range(NH):
                    a = jnp.exp2(logs[h][0] - ((lts[h] - cs[h]) - pins[h]))
                    if mask is not None:
                        a = jnp.where(mask, a, 0.0)
                    a_l.append(a)
                    g_l.append(das[h] * a)
                pres = [_dot(g.astype(BF16), u_pex) for g in g_l]
                dz_l = []
                for h in range(NH):
                    dz = g_l[h] - jnp.exp2(logs[h][0]) * (g_l[h] + (pres[h] + gs[h]))
                    if mask is not None:
                        dz = jnp.where(mask, dz, 0.0)
                    dz_l.append(dz.astype(BF16))
                for h in range(NH):
                    accs[h] = accs[h] + _dot(dz_l[h], kjs[h // 2])
                for pr in range(PP):
                    ps = slice(pr * LANES, (pr + 1) * LANES)
                    ha, hb = 2 * pr, 2 * pr + 1
                    dk_s[pl.ds(k0, TQ), ps] += _dot_tn(dz_l[ha], qs[ha]) + _dot_tn(dz_l[hb], qs[hb])
                    dv_s[pl.ds(k0, TQ), ps] += _dot_tn(a_l[ha].astype(BF16), dos[ha]) + _dot_tn(a_l[hb].astype(BF16), dos[hb])
                for h in range(NH):
                    cs[h] = cs[h] + jnp.sum(logs[h][1], axis=1, keepdims=True)
                    gs[h] = gs[h] + jnp.sum(g_l[h], axis=1, keepdims=True)
                return tuple(cs) + tuple(gs) + tuple(accs)

            z1 = jnp.zeros((TQ, 1), F32)
            zl = jnp.zeros((TQ, LANES), F32)

            def k_block(j, c):
                return tile(pl.multiple_of(j * TQ, TQ), c, None)

            c = lax.fori_loop(0, i, k_block, (z1,) * (2 * NH) + (zl,) * NH)
            c = tile(q0, c, vis)
            for pr in range(PP):
                dq = jnp.where(masks[0], c[2 * NH + 2 * pr], c[2 * NH + 2 * pr + 1]) * scale
                dq_ref[pl.ds(q0, TQ), pr * LANES:(pr + 1) * LANES] = dq.astype(BF16)
            return carry

        lax.fori_loop(0, nq, q_block, 0)
        dk_ref[...] = dk_s[...].astype(BF16)
        dv_ref[...] = dv_s[...].astype(BF16)

    blk = lambda off: pl.BlockSpec((seq, W), lambda b, g: (b, off + g))
    out_blk = pl.BlockSpec((seq, W), lambda b, g: (b, g))
    return pl.pallas_call(
        body, name=name, grid=(B, nstep),
        in_specs=[blk(P_Q // W), blk(P_K // W), blk(P_V // W), out_blk, out_blk],
        out_specs=[out_blk, out_blk, out_blk],
        out_shape=[jax.ShapeDtypeStruct((T, SB_W), BF16) for _ in range(3)],
        scratch_shapes=[pltpu.VMEM((seq, NH * LANES), BF16), pltpu.VMEM((seq, W), BF16), pltpu.VMEM((seq, W), BF16),
                        pltpu.VMEM((seq, NH * LANES), BF16), pltpu.VMEM((seq, W), F32), pltpu.VMEM((seq, W), F32)],
        compiler_params=_cparams(("parallel", "parallel")),
    )(p, p, p, ltot, do)


def _mla_masks():
    lane = lax.broadcasted_iota(jnp.int32, (1, 2 * LANES), 1)
    ma = (lane < MLA_NOPE) | ((lane >= LANES) & (lane < LANES + MLA_ROPE))
    mb = ((lane >= MLA_NOPE) & (lane < LANES)) | ((lane >= LANES + MLA_ROPE) & (lane < LANES + 2 * MLA_ROPE))
    return ma, mb


def _mla_fwd(qm, kvm, krt, *, seq, name):
    T = qm.shape[0]
    B = T // seq
    TQ = ATT_T
    nq = seq // TQ
    PP = ATT_PAIRS
    W = PP * LANES
    nstep = MLA_W // W
    NH = 2 * PP
    CW = 2 * LANES
    scale = MLA_QK ** -0.5

    def body(qn_ref, qr_ref, kn_ref, v_ref, kr_ref, o_ref, lse_ref, q_s, kc_s, v_s):
        hm = _head_masks()
        mm = _mla_masks()
        v = v_ref[...]
        for pr in range(PP):
            ps = slice(pr * LANES, (pr + 1) * LANES)
            qc = jnp.concatenate([qn_ref[:, ps], qr_ref[:, ps]], axis=1)
            kc_s[:, pr * CW:(pr + 1) * CW] = jnp.concatenate([kn_ref[:, ps], kr_ref[...]], axis=1)
            for e in range(2):
                h = 2 * pr + e
                q_s[:, h * CW:(h + 1) * CW] = _pick(mm[e], qc)
                v_s[:, h * LANES:(h + 1) * LANES] = _pick(hm[e], v[:, ps])
        vis = _causal_mask(TQ, False)

        def q_block(i, carry):
            q0 = pl.multiple_of(i * TQ, TQ)
            qs = [q_s[pl.ds(q0, TQ), h * CW:(h + 1) * CW] for h in range(NH)]

            def tile(k0, c, mask):
                ms, ls, accs = list(c[:NH]), list(c[NH:2 * NH]), list(c[2 * NH:])
                ss = [_dot_nt(qs[h], kc_s[pl.ds(k0, TQ), (h // 2) * CW:(h // 2 + 1) * CW]) * scale for h in range(NH)]
                if mask is not None:
                    ss = [jnp.where(mask, s, NEG_BIG) for s in ss]
                m_new = [jnp.maximum(ms[h], jnp.max(ss[h], axis=1, keepdims=True)) for h in range(NH)]
                alphas = [jnp.exp(ms[h] - m_new[h]) for h in range(NH)]
                prs = [jnp.exp(ss[h] - m_new[h]) for h in range(NH)]
                outs = [_dot(prs[h].astype(BF16), v_s[pl.ds(k0, TQ), h * LANES:(h + 1) * LANES]) for h in range(NH)]
                ls = [alphas[h] * ls[h] + jnp.sum(prs[h], axis=1, keepdims=True) for h in range(NH)]
                for pr in range(PP):
                    accs[pr] = accs[pr] * jnp.where(hm[0], alphas[2 * pr], alphas[2 * pr + 1]) + outs[2 * pr] + outs[2 * pr + 1]
                return tuple(m_new) + tuple(ls) + tuple(accs)

            neg = jnp.full((TQ, 1), NEG_BIG, F32)
            z1 = jnp.zeros((TQ, 1), F32)

            def k_block(j, c):
                return tile(pl.multiple_of(j * TQ, TQ), c, None)

            c = lax.fori_loop(0, i, k_block, (neg,) * NH + (z1,) * NH + (jnp.zeros((TQ, LANES), F32),) * PP)
            c = tile(q0, c, vis)
            for pr in range(PP):
                ps = slice(pr * LANES, (pr + 1) * LANES)
                m_a, m_b, l_a, l_b = c[2 * pr], c[2 * pr + 1], c[NH + 2 * pr], c[NH + 2 * pr + 1]
                o_ref[pl.ds(q0, TQ), ps] = c[2 * NH + pr] / jnp.where(hm[0], l_a, l_b)
                lse_ref[pl.ds(q0, TQ), ps] = jnp.where(hm[0], m_a + jnp.log(l_a), m_b + jnp.log(l_b))
            return carry

        lax.fori_loop(0, nq, q_block, 0)

    blk = lambda off: pl.BlockSpec((seq, W), lambda b, g: (b, off + g))
    out_blk = pl.BlockSpec((seq, W), lambda b, g: (b, g))
    return pl.pallas_call(
        body, name=name, grid=(B, nstep),
        in_specs=[blk(0), blk(nstep), blk(0), blk(nstep), pl.BlockSpec((seq, LANES), lambda b, g: (b, 0))],
        out_specs=[out_blk, out_blk],
        out_shape=[jax.ShapeDtypeStruct((T, MLA_W), F32), jax.ShapeDtypeStruct((T, MLA_W), F32)],
        scratch_shapes=[pltpu.VMEM((seq, NH * CW), BF16), pltpu.VMEM((seq, PP * CW), BF16), pltpu.VMEM((seq, NH * LANES), BF16)],
        compiler_params=_cparams(("parallel", "parallel")),
    )(qm, qm, kvm, kvm, krt)


def _mla_bwd(qm, kvm, krt, o, lse, do, *, seq, name):
    T = qm.shape[0]
    B = T // seq
    TQ = ATT_T
    nq = seq // TQ
    PP = ATT_PAIRS
    W = PP * LANES
    nstep = MLA_W // W
    NH = 2 * PP
    CW = 2 * LANES
    scale = MLA_QK ** -0.5

    def body(qn_ref, qr_ref, kn_ref, v_ref, kr_ref, o_ref, lse_ref, do_ref,
             dqn_ref, dqr_ref, dkn_ref, dv_ref, dkr_ref, q_s, kc_s, do_s, dkc_s, dv_s):
        hm = _head_masks()
        mm = _mla_masks()
        dof = do_ref[...]
        for pr in range(PP):
            ps = slice(pr * LANES, (pr + 1) * LANES)
            qc = jnp.concatenate([qn_ref[:, ps], qr_ref[:, ps]], axis=1)
            kc_s[:, pr * CW:(pr + 1) * CW] = jnp.concatenate([kn_ref[:, ps], kr_ref[...]], axis=1)
            for e in range(2):
                h = 2 * pr + e
                q_s[:, h * CW:(h + 1) * CW] = _pick(mm[e], qc)
                do_s[:, h * LANES:(h + 1) * LANES] = _pick(hm[e], dof[:, ps]).astype(BF16)
        dkc_s[...] = jnp.zeros_like(dkc_s)
        dv_s[...] = jnp.zeros_like(dv_s)
        vis = _causal_mask(TQ, False)

        def q_block(i, carry):
            q0 = pl.multiple_of(i * TQ, TQ)
            qs = [q_s[pl.ds(q0, TQ), h * CW:(h + 1) * CW] for h in range(NH)]
            dos = [do_s[pl.ds(q0, TQ), h * LANES:(h + 1) * LANES] for h in range(NH)]
            lse_t = lse_ref[pl.ds(q0, TQ), :]
            dd = do_ref[pl.ds(q0, TQ), :] * o_ref[pl.ds(q0, TQ), :]
            lses, ds_ = [], []
            for h in range(NH):
                ps = slice((h // 2) * LANES, (h // 2 + 1) * LANES)
                lses.append(_lane_value(lse_t[:, ps], (h % 2) * MLA_V))
                ds_.append(jnp.sum(_pick(hm[h % 2], dd[:, ps]), axis=1, keepdims=True))

            def tile(k0, c, mask):
                accs = list(c)
                kcs = [kc_s[pl.ds(k0, TQ), pr * CW:(pr + 1) * CW] for pr in range(PP)]
                vjs = [v_ref[pl.ds(k0, TQ), pr * LANES:(pr + 1) * LANES] for pr in range(PP)]
                ss = [_dot_nt(qs[h], kcs[h // 2]) * scale for h in range(NH)]
                dps = [_dot_nt(dos[h], vjs[h // 2]) for h in range(NH)]
                p_l, ds_l = [], []
                for h in range(NH):
                    pr_ = jnp.exp(ss[h] - lses[h])
                    if mask is not None:
                        pr_ = jnp.where(mask, pr_, 0.0)
                    p_l.append(pr_.astype(BF16))
                    ds_l.append((pr_ * (dps[h] - ds_[h]) * scale).astype(BF16))
                for h in range(NH):
                    accs[h] = accs[h] + _dot(ds_l[h], kcs[h // 2])
                for pr in range(PP):
                    ha, hb = 2 * pr, 2 * pr + 1
                    dkc_s[pl.ds(k0, TQ), pr * CW:(pr + 1) * CW] += _dot_tn(ds_l[ha], qs[ha]) + _dot_tn(ds_l[hb], qs[hb])
                    dv_s[pl.ds(k0, TQ), pr * LANES:(pr + 1) * LANES] += _dot_tn(p_l[ha], dos[ha]) + _dot_tn(p_l[hb], dos[hb])
                return tuple(accs)

            zc = jnp.zeros((TQ, CW), F32)

            def k_block(j, c):
                return tile(pl.multiple_of(j * TQ, TQ), c, None)

            c = lax.fori_loop(0, i, k_block, (zc,) * NH)
            c = tile(q0, c, vis)
            for pr in range(PP):
                ps = slice(pr * LANES, (pr + 1) * LANES)
                dq = _pick(mm[0], c[2 * pr]) + _pick(mm[1], c[2 * pr + 1])
                dqn_ref[pl.ds(q0, TQ), ps] = dq[:, :LANES].astype(BF16)
                dqr_ref[pl.ds(q0, TQ), ps] = dq[:, LANES:]
            return carry

        lax.fori_loop(0, nq, q_block, 0)
        dkr = dkc_s[:, LANES:CW]
        for pr in range(PP):
            dkn_ref[:, pr * LANES:(pr + 1) * LANES] = dkc_s[:, pr * CW:pr * CW + LANES].astype(BF16)
            if pr > 0:
                dkr = dkr + dkc_s[:, pr * CW + LANES:(pr + 1) * CW]
        dv_ref[...] = dv_s[...].astype(BF16)
        g = pl.program_id(1)

        @pl.when(g == 0)
        def _():
            dkr_ref[...] = dkr

        @pl.when(g > 0)
        def _():
            dkr_ref[...] += dkr

    blk = lambda off: pl.BlockSpec((seq, W), lambda b, g: (b, off + g))
    out_blk = pl.BlockSpec((seq, W), lambda b, g: (b, g))
    one_blk = pl.BlockSpec((seq, LANES), lambda b, g: (b, 0))
    return pl.pallas_call(
        body, name=name, grid=(B, nstep),
        in_specs=[blk(0), blk(nstep), blk(0), blk(nstep), one_blk, out_blk, out_blk, out_blk],
        out_specs=[out_blk, out_blk, out_blk, out_blk, one_blk],
        out_shape=[jax.ShapeDtypeStruct((T, MLA_W), BF16), jax.ShapeDtypeStruct((T, MLA_W), F32),
                   jax.ShapeDtypeStruct((T, MLA_W), BF16), jax.ShapeDtypeStruct((T, MLA_W), BF16),
                   jax.ShapeDtypeStruct((T, LANES), F32)],
        scratch_shapes=[pltpu.VMEM((seq, NH * CW), BF16), pltpu.VMEM((seq, PP * CW), BF16), pltpu.VMEM((seq, NH * LANES), BF16),
                        pltpu.VMEM((seq, PP * CW), F32), pltpu.VMEM((seq, W), F32)],
        compiler_params=_cparams(("parallel", "arbitrary")),
    )(qm, qm, kvm, kvm, krt, o, lse, do)


def _rope_tables(pos_ref, invf_ref):
    ang = pos_ref[...].astype(F32) * invf_ref[...]
    first = (_lane_iota() % MLA_ROPE) < (MLA_ROPE // 2)
    return jnp.cos(ang), jnp.sin(ang), first


def _rope_apply(x, cos, sin, first):
    rot = jnp.where(first, -pltpu.roll(x, LANES - MLA_ROPE // 2, 1), pltpu.roll(x, MLA_ROPE // 2, 1))
    return x * cos + rot * sin


def _rope_apply_t(dy, cos, sin, first):
    dys = dy * sin
    rot_t = jnp.where(first, pltpu.roll(dys, LANES - MLA_ROPE // 2, 1), -pltpu.roll(dys, MLA_ROPE // 2, 1))
    return dy * cos + rot_t


def _proj_uq_rope(p, g, wuq, pos, invf, *, tm, name):
    T = p.shape[0]
    ntile = MLA_W // LANES

    def body(x_ref, kr_ref, g_ref, w_ref, pos_ref, invf_ref, cq_ref, qm_ref, krt_ref):
        cos, sin, first = _rope_tables(pos_ref, invf_ref)
        hb = _rms(x_ref[...], g_ref[...]).astype(BF16)
        cq_ref[...] = hb
        q = jnp.dot(hb, w_ref[...], preferred_element_type=F32)
        qm_ref[:, :MLA_W] = q[:, :MLA_W].astype(BF16)
        for t in range(ntile):
            sl = slice(MLA_W + t * LANES, MLA_W + (t + 1) * LANES)
            qm_ref[:, sl] = _rope_apply(q[:, sl], cos, sin, first).astype(BF16)
        krt_ref[...] = _rope_apply(kr_ref[...], cos, sin, first).astype(BF16)

    return pl.pallas_call(
        body, name=name, grid=(T // tm,),
        in_specs=[pl.BlockSpec((tm, Q_LORA), lambda i: (i, P_CQ // Q_LORA)), pl.BlockSpec((tm, LANES), lambda i: (i, P_KRT // LANES)),
                  pl.BlockSpec((1, Q_LORA), lambda i: (0, 0)), pl.BlockSpec((Q_LORA, 2 * MLA_W), lambda i: (0, 0)),
                  pl.BlockSpec((tm, 1), lambda i: (i, 0)), pl.BlockSpec((1, LANES), lambda i: (0, 0))],
        out_specs=[pl.BlockSpec((tm, Q_LORA), lambda i: (i, 0)), pl.BlockSpec((tm, 2 * MLA_W), lambda i: (i, 0)),
                   pl.BlockSpec((tm, LANES), lambda i: (i, 0))],
        out_shape=[jax.ShapeDtypeStruct((T, Q_LORA), BF16), jax.ShapeDtypeStruct((T, 2 * MLA_W), BF16),
                   jax.ShapeDtypeStruct((T, LANES), BF16)],
        compiler_params=_cparams(("parallel",)),
    )(p, p, g, wuq, pos, invf)


def _d_proj_uq_rope(dqn, dqr, dkr, wuq, p, g, pos, invf, *, tm, name):
    T = dqn.shape[0]
    ntile = MLA_W // LANES

    def body(dqn_ref, dqr_ref, dkr_ref, w_ref, x_ref, g_ref, pos_ref, invf_ref, dqm_ref, dx_ref, dg_ref, dkr_o_ref):
        cos, sin, first = _rope_tables(pos_ref, invf_ref)
        dqm_ref[:, :MLA_W] = dqn_ref[...]
        for t in range(ntile):
            sl = slice(t * LANES, (t + 1) * LANES)
            dqm_ref[:, MLA_W + t * LANES:MLA_W + (t + 1) * LANES] = _rope_apply_t(dqr_ref[:, sl], cos, sin, first).astype(BF16)
        dkr_o_ref[...] = _rope_apply_t(dkr_ref[...], cos, sin, first).astype(BF16)
        dy = lax.dot_general(dqm_ref[...], w_ref[...], (((1,), (1,)), ((), ())), preferred_element_type=F32)
        dx, part = _rms_grad(dy, x_ref[...], g_ref[...])
        dx_ref[...] = dx.astype(BF16)
        _accumulate(dg_ref, part)

    half = pl.BlockSpec((tm, MLA_W), lambda i: (i, 0))
    tile = pl.BlockSpec((tm, LANES), lambda i: (i, 0))
    return pl.pallas_call(
        body, name=name, grid=(T // tm,),
        in_specs=[half, half, tile, pl.BlockSpec((Q_LORA, 2 * MLA_W), lambda i: (0, 0)),
                  pl.BlockSpec((tm, Q_LORA), lambda i: (i, P_CQ // Q_LORA)), pl.BlockSpec((1, Q_LORA), lambda i: (0, 0)),
                  pl.BlockSpec((tm, 1), lambda i: (i, 0)), pl.BlockSpec((1, LANES), lambda i: (0, 0))],
        out_specs=[pl.BlockSpec((tm, 2 * MLA_W), lambda i: (i, 0)), pl.BlockSpec((tm, Q_LORA), lambda i: (i, 0)),
                   pl.BlockSpec((1, Q_LORA), lambda i: (0, 0)), tile],
        out_shape=[jax.ShapeDtypeStruct((T, 2 * MLA_W), BF16), jax.ShapeDtypeStruct((T, Q_LORA), BF16),
                   jax.ShapeDtypeStruct((1, Q_LORA), F32), jax.ShapeDtypeStruct((T, LANES), BF16)],
        compiler_params=_cparams(("arbitrary",)),
    )(dqn, dqr, dkr, wuq, p, g, pos, invf)


def _d_proj_cat(pieces, b, x, g, *, tm, name, residual=None, col_block=0, out_dtype=F32):
    M = pieces[0].shape[0]
    widths = [pc.shape[1] for pc in pieces]
    K = sum(widths)
    C = b.shape[0]
    n = len(pieces)
    in_specs = [pl.BlockSpec((tm, w), lambda i: (i, 0)) for w in widths]
    in_specs += [pl.BlockSpec((C, K), lambda i: (0, 0)), pl.BlockSpec((tm, C), lambda i: (i, col_block)),
                 pl.BlockSpec((1, C), lambda i: (0, 0))]
    args = list(pieces) + [b, x, g]
    if residual is not None:
        in_specs.append(pl.BlockSpec((tm, C), lambda i: (i, 0)))
        args.append(residual)

    def body(*refs):
        b_ref, x_ref, g_ref = refs[n:n + 3]
        cat_ref, dx_ref, dg_ref = refs[-3:]
        off = 0
        for r, w in zip(refs[:n], widths):
            cat_ref[:, off:off + w] = r[...]
            off += w
        dy = lax.dot_general(cat_ref[...], b_ref[...], (((1,), (1,)), ((), ())), preferred_element_type=F32)
        dx, part = _rms_grad(dy, x_ref[...], g_ref[...])
        if residual is not None:
            dx = dx + refs[n + 3][...]
        dx_ref[...] = dx.astype(out_dtype)
        _accumulate(dg_ref, part)

    return pl.pallas_call(
        body, name=name, grid=(M // tm,), in_specs=in_specs,
        out_specs=[pl.BlockSpec((tm, K), lambda i: (i, 0)), pl.BlockSpec((tm, C), lambda i: (i, 0)),
                   pl.BlockSpec((1, C), lambda i: (0, 0))],
        out_shape=[jax.ShapeDtypeStruct((M, K), BF16), jax.ShapeDtypeStruct((M, C), out_dtype),
                   jax.ShapeDtypeStruct((1, C), F32)],
        compiler_params=_cparams(("arbitrary",)),
    )(*args)


def _heads_out(xa, xb, ga, gb, w, resid, *, tm, name):
    T, C = xa.shape
    N = w.shape[1]

    def body(xa_ref, xb_ref, ga_ref, gb_ref, w_ref, r_ref, oc_ref, o_ref):
        oc_ref[:, :C] = _rms(xa_ref[...], ga_ref[...]).astype(BF16)
        oc_ref[:, C:] = _rms(xb_ref[...], gb_ref[...]).astype(BF16)
        o_ref[...] = r_ref[...] + jnp.dot(oc_ref[...], w_ref[...], preferred_element_type=F32)

    row = pl.BlockSpec((tm, C), lambda i: (i, 0))
    gsp = pl.BlockSpec((1, C), lambda i: (0, 0))
    full = pl.BlockSpec((tm, N), lambda i: (i, 0))
    return pl.pallas_call(
        body, name=name, grid=(T // tm,),
        in_specs=[row, row, gsp, gsp, pl.BlockSpec((2 * C, N), lambda i: (0, 0)), full],
        out_specs=[pl.BlockSpec((tm, 2 * C), lambda i: (i, 0)), full],
        out_shape=[jax.ShapeDtypeStruct((T, 2 * C), BF16), jax.ShapeDtypeStruct((T, N), F32)],
        compiler_params=_cparams(("parallel",)),
    )(xa, xb, ga, gb, w, resid)


def _heads_out_bwd(dout, w, xa, xb, ga, gb, *, tm, name):
    T, C = xa.shape
    N = w.shape[1]

    def body(d_ref, w_ref, xa_ref, xb_ref, ga_ref, gb_ref, dxa_ref, dxb_ref, dga_ref, dgb_ref):
        dy = lax.dot_general(d_ref[...].astype(BF16), w_ref[...], (((1,), (1,)), ((), ())), preferred_element_type=F32)
        dxa, pa = _rms_grad(dy[:, :C], xa_ref[...], ga_ref[...])
        dxb, pb = _rms_grad(dy[:, C:], xb_ref[...], gb_ref[...])
        dxa_ref[...] = dxa
        dxb_ref[...] = dxb
        _accumulate(dga_ref, pa)
        _accumulate(dgb_ref, pb)

    row = pl.BlockSpec((tm, C), lambda i: (i, 0))
    gsp = pl.BlockSpec((1, C), lambda i: (0, 0))
    return pl.pallas_call(
        body, name=name, grid=(T // tm,),
        in_specs=[pl.BlockSpec((tm, N), lambda i: (i, 0)), pl.BlockSpec((2 * C, N), lambda i: (0, 0)), row, row, gsp, gsp],
        out_specs=[row, row, gsp, gsp],
        out_shape=[jax.ShapeDtypeStruct((T, C), F32), jax.ShapeDtypeStruct((T, C), F32),
                   jax.ShapeDtypeStruct((1, C), F32), jax.ShapeDtypeStruct((1, C), F32)],
        compiler_params=_cparams(("arbitrary",)),
    )(dout, w, xa, xb, ga, gb)


CONV_ROWS = 256
HALO = 8


def _conv_taps(w_ref):
    return w_ref[0:1, :], w_ref[1:2, :], w_ref[2:3, :]


def _conv_rows(cur, prev, w, bias):
    ext = jnp.concatenate([prev, cur], axis=0)
    u1 = pltpu.roll(ext, 1, 0)[HALO:]
    u2 = pltpu.roll(ext, 2, 0)[HALO:]
    return w[2] * cur + w[1] * u1 + w[0] * u2 + bias, u1, u2


def _conv_fwd(u, w, bias, *, seq, name):
    T = u.shape[0]
    B = T // seq
    W2 = 2 * FF_BLK

    def body(u_ref, w_ref, b_ref, a_ref):
        wv = _conv_taps(w_ref)
        bv = b_ref[...]
        for c in range(seq // CONV_ROWS):
            r0 = c * CONV_ROWS
            cur = u_ref[r0:r0 + CONV_ROWS, :]
            prev = u_ref[r0 - HALO:r0, :] if c > 0 else jnp.zeros((HALO, W2), F32)
            y, _, _ = _conv_rows(cur, prev, wv, bv)
            gc = y[:, :FF_BLK]
            a_ref[r0:r0 + CONV_ROWS, :] = (gc * (1.0 / (1.0 + jnp.exp(-gc))) * y[:, FF_BLK:]).astype(BF16)

    return pl.pallas_call(
        body, name=name, grid=(B, N_FF_BLK),
        in_specs=[pl.BlockSpec((seq, W2), lambda b, j: (b, j)), pl.BlockSpec((3, W2), lambda b, j: (0, j)),
                  pl.BlockSpec((1, W2), lambda b, j: (0, j))],
        out_specs=pl.BlockSpec((seq, FF_BLK), lambda b, j: (b, j)),
        out_shape=jax.ShapeDtypeStruct((T, D_FF), BF16),
        compiler_params=_cparams(("parallel", "parallel")),
    )(u, w, bias)


def _conv_bwd(u, da, w, bias, *, seq, name):
    T = u.shape[0]
    B = T // seq
    W2 = 2 * FF_BLK
    nchunk = seq // CONV_ROWS

    def body(u_ref, da_ref, w_ref, b_ref, du_ref, dw_ref, db_ref, duc_s):
        wv = _conv_taps(w_ref)
        bv = b_ref[...]
        zrow = jnp.zeros((1, W2), F32)
        dw0, dw1, dw2, dbs = zrow, zrow, zrow, zrow
        for c in range(nchunk):
            r0 = c * CONV_ROWS
            cur = u_ref[r0:r0 + CONV_ROWS, :]
            prev = u_ref[r0 - HALO:r0, :] if c > 0 else jnp.zeros((HALO, W2), F32)
            y, u1, u2 = _conv_rows(cur, prev, wv, bv)
            gc = y[:, :FF_BLK]
            vc = y[:, FF_BLK:]
            sg = 1.0 / (1.0 + jnp.exp(-gc))
            dav = da_ref[r0:r0 + CONV_ROWS, :]
            duc = jnp.concatenate([dav * vc * (sg * (1.0 + gc * (1.0 - sg))), dav * (gc * sg)], axis=1)
            duc_s[r0:r0 + CONV_ROWS, :] = duc
            dw0 = dw0 + jnp.sum(duc * u2, axis=0, keepdims=True)
            dw1 = dw1 + jnp.sum(duc * u1, axis=0, keepdims=True)
            dw2 = dw2 + jnp.sum(duc * cur, axis=0, keepdims=True)
            dbs = dbs + jnp.sum(duc, axis=0, keepdims=True)
        duc_s[seq:seq + HALO, :] = jnp.zeros((HALO, W2), F32)
        n_ext = CONV_ROWS + HALO
        for c in range(nchunk):
            r0 = c * CONV_ROWS
            ext = duc_s[r0:r0 + n_ext, :]
            s1 = pltpu.roll(ext, n_ext - 1, 0)[:CONV_ROWS]
            s2 = pltpu.roll(ext, n_ext - 2, 0)[:CONV_ROWS]
            du_ref[r0:r0 + CONV_ROWS, :] = (wv[2] * ext[:CONV_ROWS] + wv[1] * s1 + wv[0] * s2).astype(BF16)

        first = pl.program_id(1) == 0

        @pl.when(first)
        def _():
            dw_ref[0:1, :] = dw0
            dw_ref[1:2, :] = dw1
            dw_ref[2:3, :] = dw2
            db_ref[...] = dbs

        @pl.when(jnp.logical_not(first))
        def _():
            dw_ref[0:1, :] += dw0
            dw_ref[1:2, :] += dw1
            dw_ref[2:3, :] += dw2
            db_ref[...] += dbs

    return pl.pallas_call(
        body, name=name, grid=(N_FF_BLK, B),
        in_specs=[pl.BlockSpec((seq, W2), lambda j, b: (b, j)), pl.BlockSpec((seq, FF_BLK), lambda j, b: (b, j)),
                  pl.BlockSpec((3, W2), lambda j, b: (0, j)), pl.BlockSpec((1, W2), lambda j, b: (0, j))],
        out_specs=[pl.BlockSpec((seq, W2), lambda j, b: (b, j)), pl.BlockSpec((3, W2), lambda j, b: (0, j)),
                   pl.BlockSpec((1, W2), lambda j, b: (0, j))],
        out_shape=[jax.ShapeDtypeStruct((T, 2 * D_FF), BF16), jax.ShapeDtypeStruct((3, 2 * D_FF), F32),
                   jax.ShapeDtypeStruct((1, 2 * D_FF), F32)],
        scratch_shapes=[pltpu.VMEM((seq + HALO, W2), F32)],
        compiler_params=_cparams(("parallel", "arbitrary")),
    )(u, da, w, bias)


def _place():
    return lax.axis_index("x"), lax.axis_index("y"), lax.axis_index("c")


def _other_chips(x, y):
    return [(1 - x, y), (x, 1 - y), (1 - x, 1 - y)]


def _all_gather(vs, *, name):
    n = len(vs)

    def body(*refs):
        v_refs, out_refs = refs[:n], refs[n:2 * n]
        send_sems, recv_sems, local_sems = refs[2 * n:]
        x, y, c = _place()
        me, sibling = (x, y, c), (x, y, 1 - c)
        chips = _other_chips(x, y)

        def slab(a, px, py, pc):
            return out_refs[a].at[4 * px + 2 * py + pc]

        def copy(a, k, block, to, src=None):
            return pltpu.make_async_remote_copy(
                src_ref=slab(a, *block) if src is None else src, dst_ref=slab(a, *block),
                send_sem=send_sems.at[7 * a + k], recv_sem=recv_sems.at[7 * a + k], device_id=to, device_id_type=MESH)

        mine = [pltpu.make_async_copy(v_refs[a], slab(a, *me), local_sems.at[a]) for a in range(n)]
        for cp in mine:
            cp.start()
        first = []
        for a in range(n):
            first.append(copy(a, 0, me, sibling, src=v_refs[a]))
            first += [copy(a, 1 + j, me, (*chip, c), src=v_refs[a]) for j, chip in enumerate(chips)]
        for cp in first:
            cp.start()
        passed = []
        for j, chip in enumerate(chips):
            for a in range(n):
                copy(a, 1 + j, (*chip, c), me).wait_recv()
                cp = copy(a, 4 + j, (*chip, c), sibling)
                cp.start()
                passed.append(cp)
        for a in range(n):
            copy(a, 0, sibling, me).wait_recv()
            for j, chip in enumerate(chips):
                copy(a, 4 + j, (*chip, 1 - c), me).wait_recv()
        for cp in first + passed:
            cp.wait_send()
        for cp in mine:
            cp.wait()

    return pl.pallas_call(
        body, name=name, in_specs=[ANY] * n, out_specs=[ANY] * n,
        out_shape=[jax.ShapeDtypeStruct((N_DEV,) + v.shape, v.dtype) for v in vs],
        scratch_shapes=[pltpu.SemaphoreType.DMA((7 * n,)), pltpu.SemaphoreType.DMA((7 * n,)), pltpu.SemaphoreType.DMA((n,))],
    )(*vs)


def _all_gather_async(vs, *, name, collective_id):
    n = len(vs)
    v_refs = [jax.new_ref(v, memory_space=pltpu.MemorySpace.HBM) for v in vs]
    out_refs = [jax.empty_ref(jax.ShapeDtypeStruct((N_DEV,) + v.shape, v.dtype), memory_space=pltpu.MemorySpace.HBM)
                for v in vs]

    @pl.kernel(mesh=plsc.ScalarSubcoreMesh(axis_name="seq", num_cores=1), name=name,
               scratch_types=(pltpu.SemaphoreType.DMA((7 * n,)), pltpu.SemaphoreType.DMA((7 * n,)),
                              pltpu.SemaphoreType.DMA((n,))),
               compiler_params=pltpu.CompilerParams(collective_id=collective_id))
    def launch(send_sems, recv_sems, local_sems):
        x, y, c = _place()
        me, sibling = (x, y, c), (x, y, 1 - c)
        chips = _other_chips(x, y)
        peers = [sibling] + [(*chip, c) for chip in chips]
        barrier = pltpu.get_barrier_semaphore()
        for peer in peers:
            pl.semaphore_signal(barrier, inc=1, device_id=peer, device_id_type=MESH)
        pl.semaphore_wait(barrier, len(peers))

        def slab(a, px, py, pc):
            return out_refs[a].at[4 * px + 2 * py + pc]

        def copy(a, k, block, to, src=None):
            return pltpu.make_async_remote_copy(
                src_ref=slab(a, *block) if src is None else src, dst_ref=slab(a, *block),
                send_sem=send_sems.at[7 * a + k], recv_sem=recv_sems.at[7 * a + k], device_id=to, device_id_type=MESH)

        mine = [pltpu.make_async_copy(v_refs[a], slab(a, *me), local_sems.at[a]) for a in range(n)]
        for cp in mine:
            cp.start()
        first = []
        for a in range(n):
            first.append(copy(a, 0, me, sibling, src=v_refs[a]))
            first += [copy(a, 1 + j, me, (*chip, c), src=v_refs[a]) for j, chip in enumerate(chips)]
        for cp in first:
            cp.start()
        passed = []
        for j, chip in enumerate(chips):
            for a in range(n):
                copy(a, 1 + j, (*chip, c), me).wait_recv()
                cp = copy(a, 4 + j, (*chip, c), sibling)
                cp.start()
                passed.append(cp)
        for a in range(n):
            copy(a, 0, sibling, me).wait_recv()
            for j, chip in enumerate(chips):
                copy(a, 4 + j, (*chip, 1 - c), me).wait_recv()
        for cp in first + passed:
            cp.wait_send()
        for cp in mine:
            cp.wait()

    launch()
    return [r[...] for r in out_refs]


def _handshake(peers):
    barrier = pltpu.get_barrier_semaphore()
    for peer in peers:
        pl.semaphore_signal(barrier, inc=1, device_id=peer, device_id_type=MESH)
    pl.semaphore_wait(barrier, len(peers))


def _hbm_refs(arrays, lead):
    src = [jax.new_ref(a, memory_space=pltpu.MemorySpace.HBM) for a in arrays]
    dst = [jax.empty_ref(jax.ShapeDtypeStruct((lead,) + a.shape[1:], a.dtype), memory_space=pltpu.MemorySpace.HBM)
           for a in arrays]
    return src, dst


def _rs_sibling_async(g8s, *, name, collective_id):
    n = len(g8s)
    g_refs, out_refs = _hbm_refs(g8s, 4)

    @pl.kernel(mesh=plsc.ScalarSubcoreMesh(axis_name="seq", num_cores=1), name=name,
               scratch_types=(pltpu.SemaphoreType.DMA((4 * n,)), pltpu.SemaphoreType.DMA((4 * n,))),
               compiler_params=pltpu.CompilerParams(collective_id=collective_id))
    def launch(send_sems, recv_sems):
        x, y, c = _place()
        _handshake([(x, y, 1 - c)])
        copies = [
            pltpu.make_async_remote_copy(
                src_ref=g_refs[a].at[2 * k + 1 - c], dst_ref=out_refs[a].at[k],
                send_sem=send_sems.at[4 * a + k], recv_sem=recv_sems.at[4 * a + k],
                device_id=(x, y, 1 - c), device_id_type=MESH)
            for a in range(n) for k in range(4)]
        for cp in copies:
            cp.start()
        for cp in copies:
            cp.wait()

    launch()
    return [r[...] for r in out_refs]


def _rs_chips_async(h4s, *, name, collective_id):
    n = len(h4s)
    h_refs, out_refs = _hbm_refs(h4s, 3)

    @pl.kernel(mesh=plsc.ScalarSubcoreMesh(axis_name="seq", num_cores=1), name=name,
               scratch_types=(pltpu.SemaphoreType.DMA((3 * n,)), pltpu.SemaphoreType.DMA((3 * n,))),
               compiler_params=pltpu.CompilerParams(collective_id=collective_id))
    def launch(send_sems, recv_sems):
        x, y, c = _place()
        chips = _other_chips(x, y)
        _handshake([(cx, cy, c) for cx, cy in chips])
        copies = [
            pltpu.make_async_remote_copy(
                src_ref=h_refs[a].at[2 * cx + cy], dst_ref=out_refs[a].at[j],
                send_sem=send_sems.at[3 * a + j], recv_sem=recv_sems.at[3 * a + j],
                device_id=(cx, cy, c), device_id_type=MESH)
            for a in range(n) for j, (cx, cy) in enumerate(chips)]
        for cp in copies:
            cp.start()
        for cp in copies:
            cp.wait()

    launch()
    return [r[...] for r in out_refs]


def _peer(x, y, c, k):
    return ((1 - x) if k & 4 else x, (1 - y) if k & 2 else y, (1 - c) if k & 1 else c)


def _rs_direct_async(g8s, *, name, collective_id):
    n = len(g8s)
    g_refs, out_refs = _hbm_refs(g8s, N_DEV - 1)

    @pl.kernel(mesh=plsc.ScalarSubcoreMesh(axis_name="seq", num_cores=1), name=name,
               scratch_types=(pltpu.SemaphoreType.DMA((7 * n,)), pltpu.SemaphoreType.DMA((7 * n,))),
               compiler_params=pltpu.CompilerParams(collective_id=collective_id))
    def launch(send_sems, recv_sems):
        x, y, c = _place()
        peers = [_peer(x, y, c, k) for k in range(1, N_DEV)]
        _handshake(peers)
        copies = [
            pltpu.make_async_remote_copy(
                src_ref=g_refs[a].at[4 * px + 2 * py + pc], dst_ref=out_refs[a].at[k],
                send_sem=send_sems.at[7 * a + k], recv_sem=recv_sems.at[7 * a + k],
                device_id=(px, py, pc), device_id_type=MESH)
            for a in range(n) for k, (px, py, pc) in enumerate(peers)]
        for cp in copies:
            cp.start()
        for cp in copies:
            cp.wait()

    launch()
    return [r[...] for r in out_refs]


def _row_tile(rows):
    if rows <= 512:
        return rows
    return next(t for t in (512, 384, 352, 256, 128) if rows % t == 0)


def _rs_chip_sum(g8, from_sibling, place_idx, *, name):
    _, R, C = g8.shape
    tr = _row_tile(R)

    def body(pi_ref, a_ref, b_ref, f_ref, h_ref):
        s = a_ref[...] + b_ref[...]
        h_ref[...] = s.astype(BF16)

        @pl.when(pl.program_id(1) == pi_ref[1])
        def _():
            f_ref[...] = s

    blk = pl.BlockSpec((None, tr, C), lambda r, k, pi_ref: (k, r, 0))
    return pl.pallas_call(
        body, name=name,
        grid_spec=pltpu.PrefetchScalarGridSpec(
            num_scalar_prefetch=1, grid=(R // tr, 4),
            in_specs=[pl.BlockSpec((None, tr, C), lambda r, k, pi_ref: (2 * k + pi_ref[0], r, 0)), blk],
            out_specs=[pl.BlockSpec((tr, C), lambda r, k, pi_ref: (r, 0)), blk]),
        out_shape=[jax.ShapeDtypeStruct((R, C), F32), jax.ShapeDtypeStruct((4, R, C), BF16)],
        compiler_params=_cparams(("parallel", "arbitrary")),
    )(place_idx, g8, from_sibling)


def _split_moves(segments, chunk):
    moves = []
    for dst, src, length in segments:
        while length > 0:
            dev, off = divmod(src, chunk)
            take = min(length, chunk - off)
            moves.append((dst, dev, off, take))
            dst, src, length = dst + take, src + take, length - take
    return moves


def _assemble(stacked, segments, zero_spans, out_cols, *, name):
    _, R, c = stacked.shape
    tr = _row_tile(R)
    moves = _split_moves(segments, c)

    def body(x_ref, o_ref):
        for dst, dev, off, take in moves:
            o_ref[:, dst:dst + take] = x_ref[dev, :, off:off + take]
        for a, b in zero_spans:
            o_ref[:, a:b] = jnp.zeros((tr, b - a), o_ref.dtype)

    return pl.pallas_call(
        body, name=name, grid=(R // tr,),
        in_specs=[pl.BlockSpec((N_DEV, tr, c), lambda i: (0, i, 0))],
        out_specs=pl.BlockSpec((tr, out_cols), lambda i: (i, 0)),
        out_shape=jax.ShapeDtypeStruct((R, out_cols), stacked.dtype),
        compiler_params=_cparams(("parallel",)),
    )(stacked)


def _disassemble(full, segments, chunk, *, name, out_dtype=F32):
    R = full.shape[0]
    tr = _row_tile(R)
    moves = _split_moves(segments, chunk)

    def body(x_ref, o_ref):
        seen = set()
        for dst, dev, off, take in moves:
            piece = x_ref[:, dst:dst + take]
            if (dev, off) in seen:
                piece = piece + o_ref[dev, :, off:off + take]
            seen.add((dev, off))
            o_ref[dev, :, off:off + take] = piece.astype(out_dtype)

    return pl.pallas_call(
        body, name=name, grid=(R // tr,),
        in_specs=[pl.BlockSpec((tr, full.shape[1]), lambda i: (i, 0))],
        out_specs=pl.BlockSpec((N_DEV, tr, chunk), lambda i: (0, i, 0)),
        out_shape=jax.ShapeDtypeStruct((N_DEV, R, chunk), out_dtype),
        compiler_params=_cparams(("parallel",)),
    )(full)


def _disassemble_rows(full_t, segments, chunk, *, name, out_dtype=F32):
    R = full_t.shape[1]
    tc = next(t for t in (2 * LANES, LANES) if R % t == 0)
    moves = _split_moves(segments, chunk)

    def body(x_ref, o_ref):
        seen = set()
        for dst, dev, off, take in moves:
            piece = x_ref[dst:dst + take, :]
            if (dev, off) in seen:
                piece = piece + o_ref[dev, off:off + take, :]
            seen.add((dev, off))
            o_ref[dev, off:off + take, :] = piece.astype(out_dtype)

    return pl.pallas_call(
        body, name=name, grid=(R // tc,),
        in_specs=[pl.BlockSpec((full_t.shape[0], tc), lambda i: (0, i))],
        out_specs=pl.BlockSpec((N_DEV, chunk, tc), lambda i: (0, 0, i)),
        out_shape=jax.ShapeDtypeStruct((N_DEV, chunk, R), out_dtype),
        compiler_params=_cparams(("parallel",)),
    )(full_t)


_O_CQ = 3 * SB_W
_O_CKV = _O_CQ + Q_LORA
_O_KR = _O_CKV + KV_LORA
SEG_W_IN = ((0, 0, 3 * SB_W), (P_CKV, _O_CKV, KV_LORA), (P_KRT, _O_KR, MLA_ROPE), (P_KRT + MLA_ROPE, _O_KR, MLA_ROPE),
            (P_CQ, _O_CQ, Q_LORA))
ZERO_W_IN = ((P_KRT + 2 * MLA_ROPE, P_CQ),)
SEG_W_UQ = tuple((MLA_NOPE * h, MLA_QK * h, MLA_NOPE) for h in range(MLA_HEADS)) + tuple(
    (MLA_W + LANES * (h // 2) + MLA_ROPE * (h % 2), MLA_QK * h + MLA_NOPE, MLA_ROPE) for h in range(MLA_HEADS))
ZERO_W_UQ = tuple((MLA_W + LANES * g + 2 * MLA_ROPE, MLA_W + LANES * (g + 1)) for g in range(MLA_HEADS // 2))
SEG_W_UKV = tuple((MLA_NOPE * h, (MLA_NOPE + MLA_V) * h, MLA_NOPE) for h in range(MLA_HEADS)) + tuple(
    (MLA_W + MLA_V * h, (MLA_NOPE + MLA_V) * h + MLA_NOPE, MLA_V) for h in range(MLA_HEADS))
SEG_W_UP = tuple((2 * FF_BLK * blk + FF_BLK * half, D_FF * half + FF_BLK * blk, FF_BLK)
                 for half in range(2) for blk in range(N_FF_BLK))


def _sum8(g, *, name):
    _, R, C = g.shape

    def body(g_ref, o_ref):
        acc = g_ref[0]
        for k in range(1, N_DEV):
            acc = acc + g_ref[k]
        o_ref[...] = acc

    return pl.pallas_call(
        body, name=name, out_shape=jax.ShapeDtypeStruct((R, C), F32),
    )(g)


def _adamw_math(w, gf, m, v):
    c1 = 1.0 / (1.0 - ADAM_B1 ** ADAM_STEP)
    c2 = 1.0 / (1.0 - ADAM_B2 ** ADAM_STEP)
    mn = ADAM_B1 * m + (1.0 - ADAM_B1) * gf
    vn = ADAM_B2 * v + (1.0 - ADAM_B2) * (gf * gf)
    return -ADAM_LR * ((mn * c1) / (jnp.sqrt(vn * c2) + ADAM_EPS) + ADAM_WD * w), mn, vn


def _adamw(w, g, m, v, *, name):
    R, C = w.shape
    tr = _row_tile(R)

    def body(w_ref, g_ref, m_ref, v_ref, d_ref, mo_ref, vo_ref):
        d_ref[...], mo_ref[...], vo_ref[...] = _adamw_math(w_ref[...], g_ref[...], m_ref[...], v_ref[...])

    blk = pl.BlockSpec((tr, C), lambda i: (i, 0))
    shp = jax.ShapeDtypeStruct((R, C), F32)
    return pl.pallas_call(
        body, name=name, grid=(R // tr,), in_specs=[blk] * 4, out_specs=[blk] * 3,
        out_shape=[shp, shp, shp], compiler_params=_cparams(("parallel",)),
    )(w, g, m, v)


def _adamw_rs8(g8, r7, me_idx, w, m, v, *, name):
    R, C = w.shape
    tr = _row_tile(R)

    def body(i_ref, f_ref, r_ref, w_ref, m_ref, v_ref, g_ref, d_ref, mo_ref, vo_ref):
        gf = f_ref[...].astype(F32)
        for k in range(N_DEV - 1):
            gf = gf + r_ref[k].astype(F32)
        g_ref[...] = gf
        d_ref[...], mo_ref[...], vo_ref[...] = _adamw_math(w_ref[...], gf, m_ref[...], v_ref[...])

    blk = pl.BlockSpec((tr, C), lambda i, i_ref: (i, 0))
    shp = jax.ShapeDtypeStruct((R, C), F32)
    return pl.pallas_call(
        body, name=name,
        grid_spec=pltpu.PrefetchScalarGridSpec(
            num_scalar_prefetch=1, grid=(R // tr,),
            in_specs=[pl.BlockSpec((None, tr, C), lambda i, i_ref: (i_ref[0], i, 0)),
                      pl.BlockSpec((N_DEV - 1, tr, C), lambda i, i_ref: (0, i, 0)), blk, blk, blk],
            out_specs=[blk] * 4),
        out_shape=[shp] * 4, compiler_params=_cparams(("parallel",)),
    )(me_idx, g8, r7, w, m, v)


def _adamw_rs(own, r3, w, m, v, *, name):
    R, C = w.shape
    tr = _row_tile(R)

    def body(f_ref, r_ref, w_ref, m_ref, v_ref, g_ref, d_ref, mo_ref, vo_ref):
        gf = ((f_ref[...] + r_ref[0].astype(F32)) + r_ref[1].astype(F32)) + r_ref[2].astype(F32)
        g_ref[...] = gf
        d_ref[...], mo_ref[...], vo_ref[...] = _adamw_math(w_ref[...], gf, m_ref[...], v_ref[...])

    blk = pl.BlockSpec((tr, C), lambda i: (i, 0))
    shp = jax.ShapeDtypeStruct((R, C), F32)
    return pl.pallas_call(
        body, name=name, grid=(R // tr,),
        in_specs=[blk, pl.BlockSpec((3, tr, C), lambda i: (0, i, 0)), blk, blk, blk], out_specs=[blk] * 4,
        out_shape=[shp] * 4, compiler_params=_cparams(("parallel",)),
    )(own, r3, w, m, v)


def _ff_interleave(a):
    lead = a.shape[:-1]
    return a.reshape(*lead, 2, N_FF_BLK, FF_BLK).swapaxes(-3, -2).reshape(*lead, 2 * D_FF)


def _ff_deinterleave(a):
    lead = a.shape[:-1]
    return a.reshape(*lead, N_FF_BLK, 2, FF_BLK).swapaxes(-3, -2).reshape(*lead, 2 * D_FF)


SMALL =(("g_mix", D_MODEL), ("g_cq", Q_LORA), ("g_ckv", KV_LORA), ("g_sb_out", SB_W), ("g_mla_out", MLA_W),
         ("g_ffn", D_MODEL), ("conv_b", 2 * D_FF), ("g_final", D_MODEL))
SMALL_ROWS = 88


SMALL_USED = sum(size for _, size in SMALL)


def _pack_small(d, tail=None):
    parts = [d[n].reshape(-1) for n, _ in SMALL] + ([] if tail is None else [tail])
    flat = jnp.concatenate(parts)
    flat = jnp.pad(flat, (0, SMALL_ROWS * LANES - flat.shape[0]))
    return flat.reshape(SMALL_ROWS, LANES)


def _unpack_small(a):
    flat = a.reshape(-1)
    out, off = {}, 0
    for n, size in SMALL:
        out[n] = flat[off:off + size]
        off += size
    return out


def kernel(x, positions, g_mix, w_in, g_cq, w_uq, g_ckv, w_ukv, g_sb_out, g_mla_out, w_out, g_ffn, w_up, conv_w, conv_b, w_down, g_final, loss_target, m_g_mix, m_w_in, m_g_cq, m_w_uq, m_g_ckv, m_w_ukv, m_g_sb_out, m_g_mla_out, m_w_out, m_g_ffn, m_w_up, m_conv_w, m_conv_b, m_w_down, m_g_final, v_g_mix, v_w_in, v_g_cq, v_w_uq, v_g_ckv, v_w_ukv, v_g_sb_out, v_g_mla_out, v_w_out, v_g_ffn, v_w_up, v_conv_w, v_conv_b, v_w_down, v_g_final):
    B, S, D = x.shape
    T = B * S
    xf = x.reshape(T, D)
    tgt = loss_target.reshape(T, D)
    pos = positions.reshape(T, 1)
    half = MLA_ROPE // 2
    inv_freq = 1.0 / (ROPE_BASE ** (jnp.arange(half, dtype=F32) * (2.0 / MLA_ROPE)))
    invf = jnp.tile(inv_freq, LANES // half).reshape(1, LANES)
    place_idx = jnp.stack([lax.axis_index("c"), 2 * lax.axis_index("x") + lax.axis_index("y")]).astype(jnp.int32)
    me_idx = (4 * lax.axis_index("x") + 2 * lax.axis_index("y") + lax.axis_index("c")).astype(jnp.int32).reshape(1)

    names = ("w_in", "w_uq", "w_ukv", "w_out", "w_up", "w_down", "conv_w")
    shard = {"w_in": w_in[0], "w_uq": w_uq[0], "w_ukv": w_ukv[0], "w_out": w_out[0], "w_up": w_up[0],
             "w_down": w_down[0], "conv_w": conv_w[0]}
    sent = {n: shard[n] if n == "conv_w" else shard[n].astype(BF16) for n in names}
    later = names[1:]
    w_in_all = _all_gather([sent["w_in"]], name="ag_w_in")[0]
    w_in_all, rest = lax.optimization_barrier((w_in_all, [sent[n] for n in later]))
    got = {"w_in": w_in_all}
    got.update(zip(later, _all_gather_async(rest, name="ag_weights_async", collective_id=0)))
    wi = _assemble(got["w_in"], SEG_W_IN, ZERO_W_IN, P_COLS, name="asm_w_in")
    wuq = _assemble(got["w_uq"], SEG_W_UQ, ZERO_W_UQ, 2 * MLA_W, name="asm_w_uq")
    wukv = _assemble(got["w_ukv"], SEG_W_UKV, (), 2 * MLA_W, name="asm_w_ukv")
    wup = _assemble(got["w_up"], SEG_W_UP, (), 2 * D_FF, name="asm_w_up")
    cwi = _assemble(got["conv_w"], SEG_W_UP, (), 2 * D_FF, name="asm_conv_w")
    wo = got["w_out"].reshape(D, D)
    wdn = got["w_down"].reshape(D_FF, D)
    cbi = _ff_interleave(conv_b)

    h, p = _rms_matmul_nn(xf, g_mix, wi, tm=512, name="proj_in")
    o_sb, ltot = _sb_fwd(p, seq=S, name="sb_fwd")
    cq, qm, krt = _proj_uq_rope(p, g_cq, wuq, pos, invf, tm=512, name="proj_uq")
    ckv, kvm = _rms_matmul_nn(p, g_ckv, wukv, tm=512, name="proj_ukv", col_block=P_CKV // KV_LORA, out_dtype=BF16)
    o_mla, lse = _mla_fwd(qm, kvm, krt, seq=S, name="mla_fwd")
    ocat, x1 = _heads_out(o_sb, o_mla, g_sb_out, g_mla_out, wo, xf, tm=512, name="proj_out")
    hf, u = _rms_matmul_nn(x1, g_ffn, wup, tm=256, name="ffn_up")
    a = _conv_fwd(u, cwi, cbi, seq=S, name="conv_fwd")
    dx2, dg_final, loss_row = _matmul_nn_loss(a, wdn, x1, g_final.reshape(1, D), tgt, tm=512, name="ffn_down_loss")

    da = _matmul_nt(dx2, wdn, tm=1024, tn=D_FF // 2, out_dtype=F32, name="d_ffn_down")
    dw_down = _matmul_tn(a, dx2, tm=D_FF // 2, tn=1024, tk=1024, name="dw_down", out_dtype=BF16)
    du, dcw, dcb = _conv_bwd(u, da, cwi, cbi, seq=S, name="conv_bwd")
    dw_up_t = _matmul_tn(du, hf, tm=D_FF, tn=1024, tk=1024, name="dw_up")
    dx1, dg_ffn = _matmul_nt_rms_bwd(du, wup, x1, g_ffn, tm=512, name="d_ffn_up", residual=dx2)
    dw_out = _matmul_tn(ocat, dx1, tm=1024, tn=1024, tk=1024, name="dw_out", out_dtype=BF16)
    do_sb, do_mla, dg_sb, dg_mla = _heads_out_bwd(dx1, wo, o_sb, o_mla, g_sb_out, g_mla_out, tm=512, name="d_proj_out")

    early = ("w_down", "w_up", "conv_w", "w_out")
    g8 = {"w_up": _disassemble_rows(dw_up_t, SEG_W_UP, shard["w_up"].shape[1], name="split_dw_up", out_dtype=BF16),
          "conv_w": _disassemble(dcw, SEG_W_UP, shard["conv_w"].shape[1], name="split_dconv_w", out_dtype=BF16),
          "w_out": dw_out.reshape((N_DEV,) + shard["w_out"].shape),
          "w_down": dw_down.reshape((N_DEV,) + shard["w_down"].shape)}
    r7 = dict(zip(early, _rs_direct_async([g8[n] for n in early], name="rs_direct_async", collective_id=1)))
    own, r3 = {}, {}

    dq_sb, dk_sb, dv_sb = _sb_bwd(p, ltot, do_sb, seq=S, name="sb_bwd")
    dqn, dqr, dkn, dvm, dkr = _mla_bwd(qm, kvm, krt, o_mla, lse, do_mla, seq=S, name="mla_bwd")
    dqm, dcq, dg_cq, dkr_u = _d_proj_uq_rope(dqn, dqr, dkr, wuq, p, g_cq, pos, invf, tm=512, name="d_proj_uq")
    dkvm, dckv, dg_ckv = _d_proj_cat([dkn, dvm], wukv, p, g_ckv, tm=512, name="d_proj_ukv",
                                     col_block=P_CKV // KV_LORA, out_dtype=BF16)
    dw_uq_t = _matmul_tn(dqm, cq, tm=2 * MLA_W, tn=Q_LORA, tk=1024, name="dw_uq")
    dw_ukv = _matmul_tn(ckv, dkvm, tm=KV_LORA, tn=1024, tk=1024, name="dw_ukv")
    dp, dx, dg_mix = _d_proj_cat([dq_sb, dk_sb, dv_sb, dckv, dkr_u, dcq], wi, xf, g_mix, tm=512, name="d_proj_in",
                                 residual=dx1)
    dw_in_t = _matmul_tn(dp, h, tm=P_COLS, tn=1024, tk=1024, name="dw_in")

    late = ("w_in", "w_uq", "w_ukv")
    g8.update({"w_in": _disassemble_rows(dw_in_t, SEG_W_IN, shard["w_in"].shape[1], name="split_dw_in"),
               "w_uq": _disassemble_rows(dw_uq_t, SEG_W_UQ, shard["w_uq"].shape[1], name="split_dw_uq"),
               "w_ukv": _disassemble(dw_ukv, SEG_W_UKV, shard["w_ukv"].shape[1], name="split_dw_ukv")})
    sib_l = _rs_sibling_async([g8[n] for n in late], name="rs_sibling_late", collective_id=3)

    params = {"w_in": (w_in, m_w_in, v_w_in), "w_uq": (w_uq, m_w_uq, v_w_uq), "w_ukv": (w_ukv, m_w_ukv, v_w_ukv),
              "w_out": (w_out, m_w_out, v_w_out), "w_up": (w_up, m_w_up, v_w_up), "conv_w": (conv_w, m_conv_w, v_conv_w),
              "w_down": (w_down, m_w_down, v_w_down)}
    grad, delta, new_m, new_v = {}, {}, {}, {}

    transposed = ("w_in", "w_uq", "w_up")

    def adamw_group(group):
        for n in group:
            flip = jnp.transpose if n in transposed else (lambda t: t)
            w_, m_, v_ = [flip(t[0]) for t in params[n]]
            if n in r7:
                res = _adamw_rs8(g8[n], r7[n], me_idx, w_, m_, v_, name="adamw_" + n)
            else:
                res = _adamw_rs(own[n], r3[n], w_, m_, v_, name="adamw_" + n)
            grad[n], delta[n], new_m[n], new_v[n] = [flip(r)[None] for r in res]

    adamw_group(("w_up", "w_down"))
    sib_l, grad["w_up"] = lax.optimization_barrier((sib_l, grad["w_up"]))
    sums_l = [_rs_chip_sum(g8[n], fs, place_idx, name="rs_chip_sum_" + n) for n, fs in zip(late, sib_l)]
    r3.update(zip(late, _rs_chips_async([h4 for _, h4 in sums_l], name="rs_chips_late", collective_id=4)))
    own.update({n: f for n, (f, _) in zip(late, sums_l)})
    adamw_group(("conv_w", "w_out"))

    small_part = {"g_mix": dg_mix, "g_cq": dg_cq, "g_ckv": dg_ckv, "g_sb_out": dg_sb, "g_mla_out": dg_mla,
                  "g_ffn": dg_ffn, "conv_b": _ff_deinterleave(dcb), "g_final": dg_final}
    small_all, = _all_gather([_pack_small(small_part, tail=loss_row[0, 0:1])], name="ag_small_grads")
    gsmall = _sum8(small_all, name="sum_small_grads")
    adamw_group(late)
    small_w = {"g_mix": g_mix, "g_cq": g_cq, "g_ckv": g_ckv, "g_sb_out": g_sb_out, "g_mla_out": g_mla_out,
               "g_ffn": g_ffn, "conv_b": conv_b, "g_final": g_final}
    small_m = {"g_mix": m_g_mix, "g_cq": m_g_cq, "g_ckv": m_g_ckv, "g_sb_out": m_g_sb_out, "g_mla_out": m_g_mla_out,
               "g_ffn": m_g_ffn, "conv_b": m_conv_b, "g_final": m_g_final}
    small_v = {"g_mix": v_g_mix, "g_cq": v_g_cq, "g_ckv": v_g_ckv, "g_sb_out": v_g_sb_out, "g_mla_out": v_g_mla_out,
               "g_ffn": v_g_ffn, "conv_b": v_conv_b, "g_final": v_g_final}
    ds_, ms_, vs_ = _adamw(_pack_small(small_w), gsmall, _pack_small(small_m), _pack_small(small_v), name="adamw_small")
    for src, dst in ((_unpack_small(gsmall), grad), (_unpack_small(ds_), delta), (_unpack_small(ms_), new_m), (_unpack_small(vs_), new_v)):
        for n, _ in SMALL:
            dst[n] = src[n].reshape(small_w[n].shape)

    loss = gsmall.reshape(-1)[SMALL_USED]
    order = ("g_mix", "w_in", "g_cq", "w_uq", "g_ckv", "w_ukv", "g_sb_out", "g_mla_out", "w_out", "g_ffn", "w_up",
             "conv_w", "conv_b", "w_down", "g_final")
    return (loss, dx.reshape(B, S, D), *[grad[n] for n in order], *[delta[n] for n in order],
            *[new_m[n] for n in order], *[new_v[n] for n in order])
```

```python
import jax
import jax.numpy as jnp
from jax import lax
from jax.experimental import pallas as pl
from jax.experimental.pallas import tpu as pltpu
from jax.experimental.pallas import tpu_sc as plsc

F32 = jnp.float32
BF16 = jnp.bfloat16

D_MODEL = 1024
SB_HEADS = 8
SB_HEAD_DIM = 64
MLA_HEADS = 8
MLA_NOPE = 64
MLA_ROPE = 32
MLA_V = 64
Q_LORA = 384
KV_LORA = 256
D_FF = 2816
ROPE_BASE = 10000.0
EPS = 1e-6
SB_W = SB_HEADS * SB_HEAD_DIM
MLA_W = MLA_HEADS * MLA_V
MLA_QK = MLA_NOPE + MLA_ROPE

ADAM_LR = 0.001
ADAM_B1 = 0.9
ADAM_B2 = 0.999
ADAM_EPS = 1e-08
ADAM_WD = 0.01
ADAM_STEP = 10

N_DEV = 8
LANES = 128
V7X_VMEM_LIMIT = 56 * 1024 * 1024
FF_BLK = 256
N_FF_BLK = D_FF // FF_BLK

P_Q, P_K, P_V = 0, SB_W, 2 * SB_W
P_CKV = 3 * SB_W
P_KRT = P_CKV + KV_LORA
P_CQ = P_KRT + LANES
P_COLS = P_CQ + Q_LORA

MESH = pl.DeviceIdType.MESH
ANY = pl.BlockSpec(memory_space=pl.ANY)


def _cparams(sem=None, vmem=V7X_VMEM_LIMIT):
    return pltpu.CompilerParams(dimension_semantics=sem, vmem_limit_bytes=vmem)


def _matmul_tn(a, b, *, tm, tn, tk, name, out_dtype=F32):
    K, M = a.shape
    N = b.shape[1]
    assert M % tm == 0 and N % tn == 0 and K % tk == 0, (name, a.shape, b.shape)
    n_k = K // tk
    narrow = out_dtype != F32

    def body(a_ref, b_ref, o_ref, *scratch):
        acc_ref = scratch[0] if narrow else o_ref
        k = pl.program_id(2)
        part = lax.dot_general(a_ref[...].astype(BF16), b_ref[...].astype(BF16), (((0,), (0,)), ((), ())),
                               preferred_element_type=F32)

        @pl.when(k == 0)
        def _():
            acc_ref[...] = part

        @pl.when(k > 0)
        def _():
            acc_ref[...] += part

        if narrow:
            @pl.when(k == n_k - 1)
            def _():
                o_ref[...] = acc_ref[...].astype(out_dtype)

    return pl.pallas_call(
        body, name=name, grid=(M // tm, N // tn, n_k),
        in_specs=[pl.BlockSpec((tk, tm), lambda i, j, k: (k, i)), pl.BlockSpec((tk, tn), lambda i, j, k: (k, j))],
        out_specs=pl.BlockSpec((tm, tn), lambda i, j, k: (i, j)),
        out_shape=jax.ShapeDtypeStruct((M, N), out_dtype),
        scratch_shapes=[pltpu.VMEM((tm, tn), F32)] if narrow else [],
        compiler_params=_cparams(("parallel", "parallel", "arbitrary")),
    )(a, b)


def _rms(xf, g):
    r = lax.rsqrt(jnp.mean(xf * xf, axis=1, keepdims=True) + EPS)
    return (xf * r) * g


def _rms_grad(dyf, xf, g):
    r = lax.rsqrt(jnp.mean(xf * xf, axis=1, keepdims=True) + EPS)
    xh = xf * r
    dyg = dyf * g
    dx = r * (dyg - xh * jnp.mean(dyg * xh, axis=1, keepdims=True))
    return dx, jnp.sum(dyf * xh, axis=0, keepdims=True)


def _accumulate(ref, part):
    @pl.when(pl.program_id(0) == 0)
    def _():
        ref[...] = part

    @pl.when(pl.program_id(0) > 0)
    def _():
        ref[...] += part


def _rms_matmul_nn(x, g, w, *, tm, name, col_block=0, out_dtype=F32):
    T = x.shape[0]
    C, N = w.shape
    assert T % tm == 0, (name, x.shape)

    def body(x_ref, g_ref, w_ref, h_ref, o_ref):
        hb = _rms(x_ref[...], g_ref[...]).astype(BF16)
        h_ref[...] = hb
        o_ref[...] = jnp.dot(hb, w_ref[...], preferred_element_type=F32).astype(out_dtype)

    return pl.pallas_call(
        body, name=name, grid=(T // tm,),
        in_specs=[pl.BlockSpec((tm, C), lambda i: (i, col_block)), pl.BlockSpec((1, C), lambda i: (0, 0)),
                  pl.BlockSpec((C, N), lambda i: (0, 0))],
        out_specs=[pl.BlockSpec((tm, C), lambda i: (i, 0)), pl.BlockSpec((tm, N), lambda i: (i, 0))],
        out_shape=[jax.ShapeDtypeStruct((T, C), BF16), jax.ShapeDtypeStruct((T, N), out_dtype)],
        compiler_params=_cparams(("parallel",)),
    )(x, g, w)


def _matmul_nt_rms_bwd(a, b, x, g, *, tm, name, residual=None, col_block=0, out_dtype=F32):
    M, K = a.shape
    C = b.shape[0]
    assert M % tm == 0, (name, a.shape)
    in_specs = [pl.BlockSpec((tm, K), lambda i: (i, 0)), pl.BlockSpec((C, K), lambda i: (0, 0)),
                pl.BlockSpec((tm, C), lambda i: (i, col_block)), pl.BlockSpec((1, C), lambda i: (0, 0))]
    args = [a, b, x, g]
    if residual is not None:
        in_specs.append(pl.BlockSpec((tm, C), lambda i: (i, 0)))
        args.append(residual)

    def body(*refs):
        a_ref, b_ref, x_ref, g_ref = refs[:4]
        dx_ref, dg_ref = refs[-2:]
        dy = lax.dot_general(a_ref[...].astype(BF16), b_ref[...], (((1,), (1,)), ((), ())), preferred_element_type=F32)
        dx, part = _rms_grad(dy, x_ref[...], g_ref[...])
        if residual is not None:
            dx = dx + refs[4][...]
        dx_ref[...] = dx.astype(out_dtype)
        _accumulate(dg_ref, part)

    return pl.pallas_call(
        body, name=name, grid=(M // tm,), in_specs=in_specs,
        out_specs=[pl.BlockSpec((tm, C), lambda i: (i, 0)), pl.BlockSpec((1, C), lambda i: (0, 0))],
        out_shape=[jax.ShapeDtypeStruct((M, C), out_dtype), jax.ShapeDtypeStruct((1, C), F32)],
        compiler_params=_cparams(("arbitrary",)),
    )(*args)


def _matmul_nn_loss(a, w, x1, g, tgt, *, tm, name):
    M, K = a.shape
    C = w.shape[1]
    assert M % tm == 0, (name, a.shape)

    def body(a_ref, w_ref, x_ref, g_ref, t_ref, dx_ref, dg_ref, loss_ref):
        xf = x_ref[...] + jnp.dot(a_ref[...], w_ref[...], preferred_element_type=F32)
        gf = g_ref[...]
        err = _rms(xf, gf) - t_ref[...]
        lpart = 0.5 * jnp.sum(jnp.mean(err * err, axis=1, keepdims=True), axis=0, keepdims=True)
        dx, gpart = _rms_grad(err * (1.0 / C), xf, gf)
        dx_ref[...] = dx
        _accumulate(dg_ref, gpart)
        _accumulate(loss_ref, jnp.broadcast_to(lpart, (1, LANES)))

    row = pl.BlockSpec((tm, C), lambda i: (i, 0))
    return pl.pallas_call(
        body, name=name, grid=(M // tm,),
        in_specs=[pl.BlockSpec((tm, K), lambda i: (i, 0)), pl.BlockSpec((K, C), lambda i: (0, 0)), row,
                  pl.BlockSpec((1, C), lambda i: (0, 0)), row],
        out_specs=[row, pl.BlockSpec((1, C), lambda i: (0, 0)), pl.BlockSpec((1, LANES), lambda i: (0, 0))],
        out_shape=[jax.ShapeDtypeStruct((M, C), F32), jax.ShapeDtypeStruct((1, C), F32),
                   jax.ShapeDtypeStruct((1, LANES), F32)],
        compiler_params=_cparams(("arbitrary",)),
    )(a, w, x1, g, tgt)


ATT_T = 256
ATT_PAIRS = 2
NEG_BIG = -1e30


def _lane_iota():
    return lax.broadcasted_iota(jnp.int32, (1, LANES), 1)


def _head_masks():
    first = _lane_iota() < SB_HEAD_DIM
    return first, jnp.logical_not(first)


def _pick(mask, x):
    return jnp.where(mask, x, jnp.zeros_like(x))


def _lane_value(t, lane):
    return jnp.sum(jnp.where(_lane_iota() == lane, t, 0.0), axis=1, keepdims=True)


def _split_hi_lo(x):
    hi = x.astype(BF16)
    lo = (x - hi.astype(F32)).astype(BF16)
    return jnp.concatenate([hi, lo], axis=1)


def _tri(n, kind):
    r = lax.broadcasted_iota(jnp.int32, (n, n), 0)
    c = lax.broadcasted_iota(jnp.int32, (n, n), 1)
    u = {"suffix_excl": r > c, "prefix_incl": r <= c, "prefix_excl": r < c}[kind].astype(BF16)
    return jnp.concatenate([u, u], axis=0)


def _dot_nt(a, b):
    return lax.dot_general(a, b, (((1,), (1,)), ((), ())), preferred_element_type=F32)


def _dot_tn(a, b):
    return lax.dot_general(a, b, (((0,), (0,)), ((), ())), preferred_element_type=F32)


def _dot(a, b):
    return jnp.dot(a, b, preferred_element_type=F32)


def _causal_mask(n, strict):
    r = lax.broadcasted_iota(jnp.int32, (n, n), 0)
    c = lax.broadcasted_iota(jnp.int32, (n, n), 1)
    return (c < r) if strict else (c <= r)


LOG2E = 1.4426950408889634


def _sb_logs(qh, kj, vis):
    z2 = _dot_nt(qh, kj) * LOG2E
    nk = jnp.maximum(z2, 0.0) + jnp.log2(1.0 + jnp.exp2(-jnp.abs(z2)))
    lb = z2 - nk
    if vis is not None:
        nk = jnp.where(vis, nk, 0.0)
    return lb, nk


def _sb_fwd(p, *, seq, name):
    T = p.shape[0]
    B = T // seq
    TQ = ATT_T
    nq = seq // TQ
    PP = ATT_PAIRS
    W = PP * LANES
    nstep = SB_W // W
    NH = 2 * PP

    def body(q_ref, k_ref, v_ref, o_ref, lt_ref, q_s, k_s, v_s):
        masks = _head_masks()
        q = q_ref[...] * (SB_HEAD_DIM ** -0.5)
        v = v_ref[...]
        k_s[...] = k_ref[...].astype(BF16)
        for h in range(NH):
            ps = slice((h // 2) * LANES, (h // 2 + 1) * LANES)
            hs = slice(h * LANES, (h + 1) * LANES)
            q_s[:, hs] = _pick(masks[h % 2], q[:, ps]).astype(BF16)
            v_s[:, hs] = _pick(masks[h % 2], v[:, ps]).astype(BF16)
        u_suf = _tri(TQ, "suffix_excl")
        vis = _causal_mask(TQ, True)

        def q_block(i, carry):
            q0 = pl.multiple_of(i * TQ, TQ)
            qs = [q_s[pl.ds(q0, TQ), h * LANES:(h + 1) * LANES] for h in range(NH)]

            def tile(k0, c, mask):
                rs, accs = list(c[:NH]), list(c[NH:])
                logs = [_sb_logs(qs[h], k_s[pl.ds(k0, TQ), (h // 2) * LANES:(h // 2 + 1) * LANES], mask) for h in range(NH)]
                sums = [_dot(_split_hi_lo(nk), u_suf) for _, nk in logs]
                for h in range(NH):
                    a = jnp.exp2(logs[h][0] - sums[h] - rs[h])
                    if mask is not None:
                        a = jnp.where(mask, a, 0.0)
                    accs[h // 2] = accs[h // 2] + _dot(a.astype(BF16), v_s[pl.ds(k0, TQ), h * LANES:(h + 1) * LANES])
                    rs[h] = rs[h] + jnp.sum(logs[h][1], axis=1, keepdims=True)
                return tuple(rs) + tuple(accs)

            zero = jnp.zeros((TQ, 1), F32)
            c = tile(q0, (zero,) * NH + (jnp.zeros((TQ, LANES), F32),) * PP, vis)

            def k_block(jj, c):
                return tile(pl.multiple_of((i - 1 - jj) * TQ, TQ), c, None)

            c = lax.fori_loop(0, i, k_block, c)
            for pr in range(PP):
                ps = slice(pr * LANES, (pr + 1) * LANES)
                o_ref[pl.ds(q0, TQ), ps] = c[NH + pr]
                lt_ref[pl.ds(q0, TQ), ps] = jnp.where(masks[0], c[2 * pr], c[2 * pr + 1])
            return carry

        lax.fori_loop(0, nq, q_block, 0)

    blk = lambda off: pl.BlockSpec((seq, W), lambda b, g: (b, off + g))
    out_blk = pl.BlockSpec((seq, W), lambda b, g: (b, g))
    return pl.pallas_call(
        body, name=name, grid=(B, nstep),
        in_specs=[blk(P_Q // W), blk(P_K // W), blk(P_V // W)],
        out_specs=[out_blk, out_blk],
        out_shape=[jax.ShapeDtypeStruct((T, SB_W), F32), jax.ShapeDtypeStruct((T, SB_W), F32)],
        scratch_shapes=[pltpu.VMEM((seq, NH * LANES), BF16), pltpu.VMEM((seq, W), BF16), pltpu.VMEM((seq, NH * LANES), BF16)],
        compiler_params=_cparams(("parallel", "parallel")),
    )(p, p, p)


def _sb_bwd(p, ltot, do, *, seq, name):
    T = p.shape[0]
    B = T // seq
    TQ = ATT_T
    nq = seq // TQ
    PP = ATT_PAIRS
    W = PP * LANES
    nstep = SB_W // W
    NH = 2 * PP
    scale = SB_HEAD_DIM ** -0.5

    def body(q_ref, k_ref, v_ref, lt_ref, do_ref, dq_ref, dk_ref, dv_ref, q_s, k_s, v_s, do_s, dk_s, dv_s):
        masks = _head_masks()
        q = q_ref[...] * scale
        dof = do_ref[...]
        k_s[...] = k_ref[...].astype(BF16)
        v_s[...] = v_ref[...].astype(BF16)
        for h in range(NH):
            ps = slice((h // 2) * LANES, (h // 2 + 1) * LANES)
            hs = slice(h * LANES, (h + 1) * LANES)
            q_s[:, hs] = _pick(masks[h % 2], q[:, ps]).astype(BF16)
            do_s[:, hs] = _pick(masks[h % 2], dof[:, ps]).astype(BF16)
        dk_s[...] = jnp.zeros_like(dk_s)
        dv_s[...] = jnp.zeros_like(dv_s)
        u_pin = _tri(TQ, "prefix_incl")
        u_pex = _tri(TQ, "prefix_excl")[:TQ]
        vis = _causal_mask(TQ, True)

        def q_block(i, carry):
            q0 = pl.multiple_of(i * TQ, TQ)
            qs = [q_s[pl.ds(q0, TQ), h * LANES:(h + 1) * LANES] for h in range(NH)]
            dos = [do_s[pl.ds(q0, TQ), h * LANES:(h + 1) * LANES] for h in range(NH)]
            lt = lt_ref[pl.ds(q0, TQ), :]
            lts = [_lane_value(lt[:, (h // 2) * LANES:(h // 2 + 1) * LANES], (h % 2) * SB_HEAD_DIM) for h in range(NH)]

            def tile(k0, c, mask):
                cs, gs, accs = list(c[:NH]), list(c[NH:2 * NH]), list(c[2 * NH:])
                kjs = [k_s[pl.ds(k0, TQ), pr * LANES:(pr + 1) * LANES] for pr in range(PP)]
                vjs = [v_s[pl.ds(k0, TQ), pr * LANES:(pr + 1) * LANES] for pr in range(PP)]
                logs = [_sb_logs(qs[h], kjs[h // 2], mask) for h in range(NH)]
                pins = [_dot(_split_hi_lo(nk), u_pin) for _, nk in logs]
                das = [_dot_nt(dos[h], vjs[h // 2]) for h in range(NH)]
                a_l, g_l = [], []
                for h in range(NH):
                    a = jnp.exp2(logs[h][0] - ((lts[h] - cs[h]) - pins[h]))
                    if mask is not None:
                        a = jnp.where(mask, a, 0.0)
                    a_l.append(a)
                    g_l.append(das[h] * a)
                pres = [_dot(g.astype(BF16), u_pex) for g in g_l]
                dz_l = []
                for h in range(NH):
                    dz = g_l[h] - jnp.exp2(logs[h][0]) * (g_l[h] + (pres[h] + gs[h]))
                    if mask is not None:
                        dz = jnp.where(mask, dz, 0.0)
                    dz_l.append(dz.astype(BF16))
                for h in range(NH):
                    accs[h] = accs[h] + _dot(dz_l[h], kjs[h // 2])
                for pr in range(PP):
                    ps = slice(pr * LANES, (pr + 1) * LANES)
                    ha, hb = 2 * pr, 2 * pr + 1
                    dk_s[pl.ds(k0, TQ), ps] += _dot_tn(dz_l[ha], qs[ha]) + _dot_tn(dz_l[hb], qs[hb])
                    dv_s[pl.ds(k0, TQ), ps] += _dot_tn(a_l[ha].astype(BF16), dos[ha]) + _dot_tn(a_l[hb].astype(BF16), dos[hb])
                for h in range(NH):
                    cs[h] = cs[h] + jnp.sum(logs[h][1], axis=1, keepdims=True)
                    gs[h] = gs[h] + jnp.sum(g_l[h], axis=1, keepdims=True)
                return tuple(cs) + tuple(gs) + tuple(accs)

            z1 = jnp.zeros((TQ, 1), F32)
            zl = jnp.zeros((TQ, LANES), F32)

            def k_block(j, c):
                return tile(pl.multiple_of(j * TQ, TQ), c, None)

            c = lax.fori_loop(0, i, k_block, (z1,) * (2 * NH) + (zl,) * NH)
            c = tile(q0, c, vis)
            for pr in range(PP):
                dq = jnp.where(masks[0], c[2 * NH + 2 * pr], c[2 * NH + 2 * pr + 1]) * scale
                dq_ref[pl.ds(q0, TQ), pr * LANES:(pr + 1) * LANES] = dq.astype(BF16)
            return carry

        lax.fori_loop(0, nq, q_block, 0)
        dk_ref[...] = dk_s[...].astype(BF16)
        dv_ref[...] = dv_s[...].astype(BF16)

    blk = lambda off: pl.BlockSpec((seq, W), lambda b, g: (b, off + g))
    out_blk = pl.BlockSpec((seq, W), lambda b, g: (b, g))
    return pl.pallas_call(
        body, name=name, grid=(B, nstep),
        in_specs=[blk(P_Q // W), blk(P_K // W), blk(P_V // W), out_blk, out_blk],
        out_specs=[out_blk, out_blk, out_blk],
        out_shape=[jax.ShapeDtypeStruct((T, SB_W), BF16) for _ in range(3)],
        scratch_shapes=[pltpu.VMEM((seq, NH * LANES), BF16), pltpu.VMEM((seq, W), BF16), pltpu.VMEM((seq, W), BF16),
                        pltpu.VMEM((seq, NH * LANES), BF16), pltpu.VMEM((seq, W), F32), pltpu.VMEM((seq, W), F32)],
        compiler_params=_cparams(("parallel", "parallel")),
    )(p, p, p, ltot, do)


def _mla_masks():
    lane = lax.broadcasted_iota(jnp.int32, (1, 2 * LANES), 1)
    ma = (lane < MLA_NOPE) | ((lane >= LANES) & (lane < LANES + MLA_ROPE))
    mb = ((lane >= MLA_NOPE) & (lane < LANES)) | ((lane >= LANES + MLA_ROPE) & (lane < LANES + 2 * MLA_ROPE))
    return ma, mb


def _mla_fwd(qm, kvm, krt, *, seq, name):
    T = qm.shape[0]
    B = T // seq
    TQ = ATT_T
    nq = seq // TQ
    PP = ATT_PAIRS
    W = PP * LANES
    nstep = MLA_W // W
    NH = 2 * PP
    CW = 2 * LANES
    scale = MLA_QK ** -0.5

    def body(qn_ref, qr_ref, kn_ref, v_ref, kr_ref, o_ref, lse_ref, q_s, kc_s, v_s):
        hm = _head_masks()
        mm = _mla_masks()
        v = v_ref[...]
        for pr in range(PP):
            ps = slice(pr * LANES, (pr + 1) * LANES)
            qc = jnp.concatenate([qn_ref[:, ps], qr_ref[:, ps]], axis=1)
            kc_s[:, pr * CW:(pr + 1) * CW] = jnp.concatenate([kn_ref[:, ps], kr_ref[...]], axis=1)
            for e in range(2):
                h = 2 * pr + e
                q_s[:, h * CW:(h + 1) * CW] = _pick(mm[e], qc)
                v_s[:, h * LANES:(h + 1) * LANES] = _pick(hm[e], v[:, ps])
        vis = _causal_mask(TQ, False)

        def q_block(i, carry):
            q0 = pl.multiple_of(i * TQ, TQ)
            qs = [q_s[pl.ds(q0, TQ), h * CW:(h + 1) * CW] for h in range(NH)]

            def tile(k0, c, mask):
                ms, ls, accs = list(c[:NH]), list(c[NH:2 * NH]), list(c[2 * NH:])
                ss = [_dot_nt(qs[h], kc_s[pl.ds(k0, TQ), (h // 2) * CW:(h // 2 + 1) * CW]) * scale for h in range(NH)]
                if mask is not None:
                    ss = [jnp.where(mask, s, NEG_BIG) for s in ss]
                m_new = [jnp.maximum(ms[h], jnp.max(ss[h], axis=1, keepdims=True)) for h in range(NH)]
                alphas = [jnp.exp(ms[h] - m_new[h]) for h in range(NH)]
                prs = [jnp.exp(ss[h] - m_new[h]) for h in range(NH)]
                outs = [_dot(prs[h].astype(BF16), v_s[pl.ds(k0, TQ), h * LANES:(h + 1) * LANES]) for h in range(NH)]
                ls = [alphas[h] * ls[h] + jnp.sum(prs[h], axis=1, keepdims=True) for h in range(NH)]
                for pr in range(PP):
                    accs[pr] = accs[pr] * jnp.where(hm[0], alphas[2 * pr], alphas[2 * pr + 1]) + outs[2 * pr] + outs[2 * pr + 1]
                return tuple(m_new) + tuple(ls) + tuple(accs)

            neg = jnp.full((TQ, 1), NEG_BIG, F32)
            z1 = jnp.zeros((TQ, 1), F32)

            def k_block(j, c):
                return tile(pl.multiple_of(j * TQ, TQ), c, None)

            c = lax.fori_loop(0, i, k_block, (neg,) * NH + (z1,) * NH + (jnp.zeros((TQ, LANES), F32),) * PP)
            c = tile(q0, c, vis)
            for pr in range(PP):
                ps = slice(pr * LANES, (pr + 1) * LANES)
                m_a, m_b, l_a, l_b = c[2 * pr], c[2 * pr + 1], c[NH + 2 * pr], c[NH + 2 * pr + 1]
                o_ref[pl.ds(q0, TQ), ps] = c[2 * NH + pr] / jnp.where(hm[0], l_a, l_b)
                lse_ref[pl.ds(q0, TQ), ps] = jnp.where(hm[0], m_a + jnp.log(l_a), m_b + jnp.log(l_b))
            return carry

        lax.fori_loop(0, nq, q_block, 0)

    blk = lambda off: pl.BlockSpec((seq, W), lambda b, g: (b, off + g))
    out_blk = pl.BlockSpec((seq, W), lambda b, g: (b, g))
    return pl.pallas_call(
        body, name=name, grid=(B, nstep),
        in_specs=[blk(0), blk(nstep), blk(0), blk(nstep), pl.BlockSpec((seq, LANES), lambda b, g: (b, 0))],
        out_specs=[out_blk, out_blk],
        out_shape=[jax.ShapeDtypeStruct((T, MLA_W), F32), jax.ShapeDtypeStruct((T, MLA_W), F32)],
        scratch_shapes=[pltpu.VMEM((seq, NH * CW), BF16), pltpu.VMEM((seq, PP * CW), BF16), pltpu.VMEM((seq, NH * LANES), BF16)],
        compiler_params=_cparams(("parallel", "parallel")),
    )(qm, qm, kvm, kvm, krt)


def _mla_bwd(qm, kvm, krt, o, lse, do, *, seq, name):
    T = qm.shape[0]
    B = T // seq
    TQ = ATT_T
    nq = seq // TQ
    PP = ATT_PAIRS
    W = PP * LANES
    nstep = MLA_W // W
    NH = 2 * PP
    CW = 2 * LANES
    scale = MLA_QK ** -0.5

    def body(qn_ref, qr_ref, kn_ref, v_ref, kr_ref, o_ref, lse_ref, do_ref,
             dqn_ref, dqr_ref, dkn_ref, dv_ref, dkr_ref, q_s, kc_s, do_s, dkc_s, dv_s):
        hm = _head_masks()
        mm = _mla_masks()
        dof = do_ref[...]
        for pr in range(PP):
            ps = slice(pr * LANES, (pr + 1) * LANES)
            qc = jnp.concatenate([qn_ref[:, ps], qr_ref[:, ps]], axis=1)
            kc_s[:, pr * CW:(pr + 1) * CW] = jnp.concatenate([kn_ref[:, ps], kr_ref[...]], axis=1)
            for e in range(2):
                h = 2 * pr + e
                q_s[:, h * CW:(h + 1) * CW] = _pick(mm[e], qc)
                do_s[:, h * LANES:(h + 1) * LANES] = _pick(hm[e], dof[:, ps]).astype(BF16)
        dkc_s[...] = jnp.zeros_like(dkc_s)
        dv_s[...] = jnp.zeros_like(dv_s)
        vis = _causal_mask(TQ, False)

        def q_block(i, carry):
            q0 = pl.multiple_of(i * TQ, TQ)
            qs = [q_s[pl.ds(q0, TQ), h * CW:(h + 1) * CW] for h in range(NH)]
            dos = [do_s[pl.ds(q0, TQ), h * LANES:(h + 1) * LANES] for h in range(NH)]
            lse_t = lse_ref[pl.ds(q0, TQ), :]
            dd = do_ref[pl.ds(q0, TQ), :] * o_ref[pl.ds(q0, TQ), :]
            lses, ds_ = [], []
            for h in range(NH):
                ps = slice((h // 2) * LANES, (h // 2 + 1) * LANES)
                lses.append(_lane_value(lse_t[:, ps], (h % 2) * MLA_V))
                ds_.append(jnp.sum(_pick(hm[h % 2], dd[:, ps]), axis=1, keepdims=True))

            def tile(k0, c, mask):
                accs = list(c)
                kcs = [kc_s[pl.ds(k0, TQ), pr * CW:(pr + 1) * CW] for pr in range(PP)]
                vjs = [v_ref[pl.ds(k0, TQ), pr * LANES:(pr + 1) * LANES] for pr in range(PP)]
                ss = [_dot_nt(qs[h], kcs[h // 2]) * scale for h in range(NH)]
                dps = [_dot_nt(dos[h], vjs[h // 2]) for h in range(NH)]
                p_l, ds_l = [], []
                for h in range(NH):
                    pr_ = jnp.exp(ss[h] - lses[h])
                    if mask is not None:
                        pr_ = jnp.where(mask, pr_, 0.0)
                    p_l.append(pr_.astype(BF16))
                    ds_l.append((pr_ * (dps[h] - ds_[h]) * scale).astype(BF16))
                for h in range(NH):
                    accs[h] = accs[h] + _dot(ds_l[h], kcs[h // 2])
                for pr in range(PP):
                    ha, hb = 2 * pr, 2 * pr + 1
                    dkc_s[pl.ds(k0, TQ), pr * CW:(pr + 1) * CW] += _dot_tn(ds_l[ha], qs[ha]) + _dot_tn(ds_l[hb], qs[hb])
                    dv_s[pl.ds(k0, TQ), pr * LANES:(pr + 1) * LANES] += _dot_tn(p_l[ha], dos[ha]) + _dot_tn(p_l[hb], dos[hb])
                return tuple(accs)

            zc = jnp.zeros((TQ, CW), F32)

            def k_block(j, c):
                return tile(pl.multiple_of(j * TQ, TQ), c, None)

            c = lax.fori_loop(0, i, k_block, (zc,) * NH)
            c = tile(q0, c, vis)
            for pr in range(PP):
                ps = slice(pr * LANES, (pr + 1) * LANES)
                dq = _pick(mm[0], c[2 * pr]) + _pick(mm[1], c[2 * pr + 1])
                dqn_ref[pl.ds(q0, TQ), ps] = dq[:, :LANES].astype(BF16)
                dqr_ref[pl.ds(q0, TQ), ps] = dq[:, LANES:]
            return carry

        lax.fori_loop(0, nq, q_block, 0)
        dkr = dkc_s[:, LANES:CW]
        for pr in range(PP):
            dkn_ref[:, pr * LANES:(pr + 1) * LANES] = dkc_s[:, pr * CW:pr * CW + LANES].astype(BF16)
            if pr > 0:
                dkr = dkr + dkc_s[:, pr * CW + LANES:(pr + 1) * CW]
        dv_ref[...] = dv_s[...].astype(BF16)
        g = pl.program_id(1)

        @pl.when(g == 0)
        def _():
            dkr_ref[...] = dkr

        @pl.when(g > 0)
        def _():
            dkr_ref[...] += dkr

    blk = lambda off: pl.BlockSpec((seq, W), lambda b, g: (b, off + g))
    out_blk = pl.BlockSpec((seq, W), lambda b, g: (b, g))
    one_blk = pl.BlockSpec((seq, LANES), lambda b, g: (b, 0))
    return pl.pallas_call(
        body, name=name, grid=(B, nstep),
        in_specs=[blk(0), blk(nstep), blk(0), blk(nstep), one_blk, out_blk, out_blk, out_blk],
        out_specs=[out_blk, out_blk, out_blk, out_blk, one_blk],
        out_shape=[jax.ShapeDtypeStruct((T, MLA_W), BF16), jax.ShapeDtypeStruct((T, MLA_W), F32),
                   jax.ShapeDtypeStruct((T, MLA_W), BF16), jax.ShapeDtypeStruct((T, MLA_W), BF16),
                   jax.ShapeDtypeStruct((T, LANES), F32)],
        scratch_shapes=[pltpu.VMEM((seq, NH * CW), BF16), pltpu.VMEM((seq, PP * CW), BF16), pltpu.VMEM((seq, NH * LANES), BF16),
                        pltpu.VMEM((seq, PP * CW), F32), pltpu.VMEM((seq, W), F32)],
        compiler_params=_cparams(("parallel", "arbitrary")),
    )(qm, qm, kvm, kvm, krt, o, lse, do)


def _rope_tables(pos_ref, invf_ref):
    ang = pos_ref[...].astype(F32) * invf_ref[...]
    first = (_lane_iota() % MLA_ROPE) < (MLA_ROPE // 2)
    return jnp.cos(ang), jnp.sin(ang), first


def _rope_apply(x, cos, sin, first):
    rot = jnp.where(first, -pltpu.roll(x, LANES - MLA_ROPE // 2, 1), pltpu.roll(x, MLA_ROPE // 2, 1))
    return x * cos + rot * sin


def _rope_apply_t(dy, cos, sin, first):
    dys = dy * sin
    rot_t = jnp.where(first, pltpu.roll(dys, LANES - MLA_ROPE // 2, 1), -pltpu.roll(dys, MLA_ROPE // 2, 1))
    return dy * cos + rot_t


def _proj_uq_rope(p, g, wuq, pos, invf, *, tm, name):
    T = p.shape[0]
    ntile = MLA_W // LANES

    def body(x_ref, kr_ref, g_ref, w_ref, pos_ref, invf_ref, cq_ref, qm_ref, krt_ref):
        cos, sin, first = _rope_tables(pos_ref, invf_ref)
        hb = _rms(x_ref[...], g_ref[...]).astype(BF16)
        cq_ref[...] = hb
        q = jnp.dot(hb, w_ref[...], preferred_element_type=F32)
        qm_ref[:, :MLA_W] = q[:, :MLA_W].astype(BF16)
        for t in range(ntile):
            sl = slice(MLA_W + t * LANES, MLA_W + (t + 1) * LANES)
            qm_ref[:, sl] = _rope_apply(q[:, sl], cos, sin, first).astype(BF16)
        krt_ref[...] = _rope_apply(kr_ref[...], cos, sin, first).astype(BF16)

    return pl.pallas_call(
        body, name=name, grid=(T // tm,),
        in_specs=[pl.BlockSpec((tm, Q_LORA), lambda i: (i, P_CQ // Q_LORA)), pl.BlockSpec((tm, LANES), lambda i: (i, P_KRT // LANES)),
                  pl.BlockSpec((1, Q_LORA), lambda i: (0, 0)), pl.BlockSpec((Q_LORA, 2 * MLA_W), lambda i: (0, 0)),
                  pl.BlockSpec((tm, 1), lambda i: (i, 0)), pl.BlockSpec((1, LANES), lambda i: (0, 0))],
        out_specs=[pl.BlockSpec((tm, Q_LORA), lambda i: (i, 0)), pl.BlockSpec((tm, 2 * MLA_W), lambda i: (i, 0)),
                   pl.BlockSpec((tm, LANES), lambda i: (i, 0))],
        out_shape=[jax.ShapeDtypeStruct((T, Q_LORA), BF16), jax.ShapeDtypeStruct((T, 2 * MLA_W), BF16),
                   jax.ShapeDtypeStruct((T, LANES), BF16)],
        compiler_params=_cparams(("parallel",)),
    )(p, p, g, wuq, pos, invf)


def _d_proj_uq_rope(dqn, dqr, dkr, wuq, p, g, pos, invf, *, tm, name):
    T = dqn.shape[0]
    ntile = MLA_W // LANES

    def body(dqn_ref, dqr_ref, dkr_ref, w_ref, x_ref, g_ref, pos_ref, invf_ref, dqm_ref, dx_ref, dg_ref, dkr_o_ref):
        cos, sin, first = _rope_tables(pos_ref, invf_ref)
        dqm_ref[:, :MLA_W] = dqn_ref[...]
        for t in range(ntile):
            sl = slice(t * LANES, (t + 1) * LANES)
            dqm_ref[:, MLA_W + t * LANES:MLA_W + (t + 1) * LANES] = _rope_apply_t(dqr_ref[:, sl], cos, sin, first).astype(BF16)
        dkr_o_ref[...] = _rope_apply_t(dkr_ref[...], cos, sin, first).astype(BF16)
        dy = lax.dot_general(dqm_ref[...], w_ref[...], (((1,), (1,)), ((), ())), preferred_element_type=F32)
        dx, part = _rms_grad(dy, x_ref[...], g_ref[...])
        dx_ref[...] = dx.astype(BF16)
        _accumulate(dg_ref, part)

    half = pl.BlockSpec((tm, MLA_W), lambda i: (i, 0))
    tile = pl.BlockSpec((tm, LANES), lambda i: (i, 0))
    return pl.pallas_call(
        body, name=name, grid=(T // tm,),
        in_specs=[half, half, tile, pl.BlockSpec((Q_LORA, 2 * MLA_W), lambda i: (0, 0)),
                  pl.BlockSpec((tm, Q_LORA), lambda i: (i, P_CQ // Q_LORA)), pl.BlockSpec((1, Q_LORA), lambda i: (0, 0)),
                  pl.BlockSpec((tm, 1), lambda i: (i, 0)), pl.BlockSpec((1, LANES), lambda i: (0, 0))],
        out_specs=[pl.BlockSpec((tm, 2 * MLA_W), lambda i: (i, 0)), pl.BlockSpec((tm, Q_LORA), lambda i: (i, 0)),
                   pl.BlockSpec((1, Q_LORA), lambda i: (0, 0)), tile],
        out_shape=[jax.ShapeDtypeStruct((T, 2 * MLA_W), BF16), jax.ShapeDtypeStruct((T, Q_LORA), BF16),
                   jax.ShapeDtypeStruct((1, Q_LORA), F32), jax.ShapeDtypeStruct((T, LANES), BF16)],
        compiler_params=_cparams(("arbitrary",)),
    )(dqn, dqr, dkr, wuq, p, g, pos, invf)


def _d_proj_cat(pieces, b, x, g, *, tm, name, residual=None, col_block=0, out_dtype=F32):
    M = pieces[0].shape[0]
    widths = [pc.shape[1] for pc in pieces]
    K = sum(widths)
    C = b.shape[0]
    n = len(pieces)
    in_specs = [pl.BlockSpec((tm, w), lambda i: (i, 0)) for w in widths]
    in_specs += [pl.BlockSpec((C, K), lambda i: (0, 0)), pl.BlockSpec((tm, C), lambda i: (i, col_block)),
                 pl.BlockSpec((1, C), lambda i: (0, 0))]
    args = list(pieces) + [b, x, g]
    if residual is not None:
        in_specs.append(pl.BlockSpec((tm, C), lambda i: (i, 0)))
        args.append(residual)

    def body(*refs):
        b_ref, x_ref, g_ref = refs[n:n + 3]
        cat_ref, dx_ref, dg_ref = refs[-3:]
        off = 0
        for r, w in zip(refs[:n], widths):
            cat_ref[:, off:off + w] = r[...]
            off += w
        dy = lax.dot_general(cat_ref[...], b_ref[...], (((1,), (1,)), ((), ())), preferred_element_type=F32)
        dx, part = _rms_grad(dy, x_ref[...], g_ref[...])
        if residual is not None:
            dx = dx + refs[n + 3][...]
        dx_ref[...] = dx.astype(out_dtype)
        _accumulate(dg_ref, part)

    return pl.pallas_call(
        body, name=name, grid=(M // tm,), in_specs=in_specs,
        out_specs=[pl.BlockSpec((tm, K), lambda i: (i, 0)), pl.BlockSpec((tm, C), lambda i: (i, 0)),
                   pl.BlockSpec((1, C), lambda i: (0, 0))],
        out_shape=[jax.ShapeDtypeStruct((M, K), BF16), jax.ShapeDtypeStruct((M, C), out_dtype),
                   jax.ShapeDtypeStruct((1, C), F32)],
        compiler_params=_cparams(("arbitrary",)),
    )(*args)


def _heads_out(xa, xb, ga, gb, w, resid, *, tm, name):
    T, C = xa.shape
    N = w.shape[1]

    def body(xa_ref, xb_ref, ga_ref, gb_ref, w_ref, r_ref, oc_ref, o_ref):
        oc_ref[:, :C] = _rms(xa_ref[...], ga_ref[...]).astype(BF16)
        oc_ref[:, C:] = _rms(xb_ref[...], gb_ref[...]).astype(BF16)
        o_ref[...] = r_ref[...] + jnp.dot(oc_ref[...], w_ref[...], preferred_element_type=F32)

    row = pl.BlockSpec((tm, C), lambda i: (i, 0))
    gsp = pl.BlockSpec((1, C), lambda i: (0, 0))
    full = pl.BlockSpec((tm, N), lambda i: (i, 0))
    return pl.pallas_call(
        body, name=name, grid=(T // tm,),
        in_specs=[row, row, gsp, gsp, pl.BlockSpec((2 * C, N), lambda i: (0, 0)), full],
        out_specs=[pl.BlockSpec((tm, 2 * C), lambda i: (i, 0)), full],
        out_shape=[jax.ShapeDtypeStruct((T, 2 * C), BF16), jax.ShapeDtypeStruct((T, N), F32)],
        compiler_params=_cparams(("parallel",)),
    )(xa, xb, ga, gb, w, resid)


def _heads_out_bwd(dout, w, xa, xb, ga, gb, *, tm, name):
    T, C = xa.shape
    N = w.shape[1]

    def body(d_ref, w_ref, xa_ref, xb_ref, ga_ref, gb_ref, dxa_ref, dxb_ref, dga_ref, dgb_ref):
        dy = lax.dot_general(d_ref[...].astype(BF16), w_ref[...], (((1,), (1,)), ((), ())), preferred_element_type=F32)
        dxa, pa = _rms_grad(dy[:, :C], xa_ref[...], ga_ref[...])
        dxb, pb = _rms_grad(dy[:, C:], xb_ref[...], gb_ref[...])
        dxa_ref[...] = dxa
        dxb_ref[...] = dxb
        _accumulate(dga_ref, pa)
        _accumulate(dgb_ref, pb)

    row = pl.BlockSpec((tm, C), lambda i: (i, 0))
    gsp = pl.BlockSpec((1, C), lambda i: (0, 0))
    return pl.pallas_call(
        body, name=name, grid=(T // tm,),
        in_specs=[pl.BlockSpec((tm, N), lambda i: (i, 0)), pl.BlockSpec((2 * C, N), lambda i: (0, 0)), row, row, gsp, gsp],
        out_specs=[row, row, gsp, gsp],
        out_shape=[jax.ShapeDtypeStruct((T, C), F32), jax.ShapeDtypeStruct((T, C), F32),
                   jax.ShapeDtypeStruct((1, C), F32), jax.ShapeDtypeStruct((1, C), F32)],
        compiler_params=_cparams(("arbitrary",)),
    )(dout, w, xa, xb, ga, gb)


CONV_ROWS = 256
HALO = 8


def _conv_taps(w_ref):
    return w_ref[0:1, :], w_ref[1:2, :], w_ref[2:3, :]


def _conv_rows(cur, prev, w, bias):
    ext = jnp.concatenate([prev, cur], axis=0)
    u1 = pltpu.roll(ext, 1, 0)[HALO:]
    u2 = pltpu.roll(ext, 2, 0)[HALO:]
    return w[2] * cur + w[1] * u1 + w[0] * u2 + bias, u1, u2


def _conv_fwd(u, w, bias, *, seq, name):
    T = u.shape[0]
    B = T // seq
    W2 = 2 * FF_BLK

    def body(u_ref, w_ref, b_ref, a_ref):
        wv = _conv_taps(w_ref)
        bv = b_ref[...]
        for c in range(seq // CONV_ROWS):
            r0 = c * CONV_ROWS
            cur = u_ref[r0:r0 + CONV_ROWS, :]
            prev = u_ref[r0 - HALO:r0, :] if c > 0 else jnp.zeros((HALO, W2), F32)
            y, _, _ = _conv_rows(cur, prev, wv, bv)
            gc = y[:, :FF_BLK]
            a_ref[r0:r0 + CONV_ROWS, :] = (gc * (1.0 / (1.0 + jnp.exp(-gc))) * y[:, FF_BLK:]).astype(BF16)

    return pl.pallas_call(
        body, name=name, grid=(B, N_FF_BLK),
        in_specs=[pl.BlockSpec((seq, W2), lambda b, j: (b, j)), pl.BlockSpec((3, W2), lambda b, j: (0, j)),
                  pl.BlockSpec((1, W2), lambda b, j: (0, j))],
        out_specs=pl.BlockSpec((seq, FF_BLK), lambda b, j: (b, j)),
        out_shape=jax.ShapeDtypeStruct((T, D_FF), BF16),
        compiler_params=_cparams(("parallel", "parallel")),
    )(u, w, bias)


def _conv_bwd(u, dx2, wdn, w, bias, *, seq, name):
    T = u.shape[0]
    B = T // seq
    D = dx2.shape[1]
    W2 = 2 * FF_BLK
    nchunk = seq // CONV_ROWS

    def body(u_ref, dx_ref, wd_ref, w_ref, b_ref, du_ref, dw_ref, db_ref, dwd_ref, duc_s, dwd_s):
        wv = _conv_taps(w_ref)
        bv = b_ref[...]
        wd = wd_ref[...]
        zrow = jnp.zeros((1, W2), F32)
        dw0, dw1, dw2, dbs = zrow, zrow, zrow, zrow
        dwd = jnp.zeros((FF_BLK, D), F32)
        for c in range(nchunk):
            r0 = c * CONV_ROWS
            cur = u_ref[r0:r0 + CONV_ROWS, :]
            prev = u_ref[r0 - HALO:r0, :] if c > 0 else jnp.zeros((HALO, W2), F32)
            y, u1, u2 = _conv_rows(cur, prev, wv, bv)
            gc = y[:, :FF_BLK]
            vc = y[:, FF_BLK:]
            sg = 1.0 / (1.0 + jnp.exp(-gc))
            dxc = dx_ref[r0:r0 + CONV_ROWS, :].astype(BF16)
            dav = _dot_nt(dxc, wd)
            silu = gc * sg
            dwd = dwd + _dot_tn((silu * vc).astype(BF16), dxc)
            duc = jnp.concatenate([dav * vc * (sg * (1.0 + gc * (1.0 - sg))), dav * silu], axis=1)
            duc_s[r0:r0 + CONV_ROWS, :] = duc
            dw0 = dw0 + jnp.sum(duc * u2, axis=0, keepdims=True)
            dw1 = dw1 + jnp.sum(duc * u1, axis=0, keepdims=True)
            dw2 = dw2 + jnp.sum(duc * cur, axis=0, keepdims=True)
            dbs = dbs + jnp.sum(duc, axis=0, keepdims=True)
        duc_s[seq:seq + HALO, :] = jnp.zeros((HALO, W2), F32)
        n_ext = CONV_ROWS + HALO
        for c in range(nchunk):
            r0 = c * CONV_ROWS
            ext = duc_s[r0:r0 + n_ext, :]
            s1 = pltpu.roll(ext, n_ext - 1, 0)[:CONV_ROWS]
            s2 = pltpu.roll(ext, n_ext - 2, 0)[:CONV_ROWS]
            du_ref[r0:r0 + CONV_ROWS, :] = (wv[2] * ext[:CONV_ROWS] + wv[1] * s1 + wv[0] * s2).astype(BF16)

        b = pl.program_id(1)

        @pl.when(b == 0)
        def _():
            dw_ref[0:1, :] = dw0
            dw_ref[1:2, :] = dw1
            dw_ref[2:3, :] = dw2
            db_ref[...] = dbs
            dwd_s[...] = dwd

        @pl.when(b > 0)
        def _():
            dw_ref[0:1, :] += dw0
            dw_ref[1:2, :] += dw1
            dw_ref[2:3, :] += dw2
            db_ref[...] += dbs
            dwd_s[...] += dwd

        @pl.when(b == B - 1)
        def _():
            dwd_ref[...] = dwd_s[...].astype(BF16)

    return pl.pallas_call(
        body, name=name, grid=(N_FF_BLK, B),
        in_specs=[pl.BlockSpec((seq, W2), lambda j, b: (b, j)), pl.BlockSpec((seq, D), lambda j, b: (b, 0)),
                  pl.BlockSpec((FF_BLK, D), lambda j, b: (j, 0)),
                  pl.BlockSpec((3, W2), lambda j, b: (0, j)), pl.BlockSpec((1, W2), lambda j, b: (0, j))],
        out_specs=[pl.BlockSpec((seq, W2), lambda j, b: (b, j)), pl.BlockSpec((3, W2), lambda j, b: (0, j)),
                   pl.BlockSpec((1, W2), lambda j, b: (0, j)), pl.BlockSpec((FF_BLK, D), lambda j, b: (j, 0))],
        out_shape=[jax.ShapeDtypeStruct((T, 2 * D_FF), BF16), jax.ShapeDtypeStruct((3, 2 * D_FF), F32),
                   jax.ShapeDtypeStruct((1, 2 * D_FF), F32), jax.ShapeDtypeStruct((D_FF, D), BF16)],
        scratch_shapes=[pltpu.VMEM((seq + HALO, W2), F32), pltpu.VMEM((FF_BLK, D), F32)],
        compiler_params=_cparams(("parallel", "arbitrary")),
    )(u, dx2, wdn, w, bias)


def _place():
    return lax.axis_index("x"), lax.axis_index("y"), lax.axis_index("c")


def _other_chips(x, y):
    return [(1 - x, y), (x, 1 - y), (1 - x, 1 - y)]


def _all_gather(vs, *, name):
    n = len(vs)

    def body(*refs):
        v_refs, out_refs = refs[:n], refs[n:2 * n]
        send_sems, recv_sems, local_sems = refs[2 * n:]
        x, y, c = _place()
        me, sibling = (x, y, c), (x, y, 1 - c)
        chips = _other_chips(x, y)

        def slab(a, px, py, pc):
            return out_refs[a].at[4 * px + 2 * py + pc]

        def copy(a, k, block, to, src=None):
            return pltpu.make_async_remote_copy(
                src_ref=slab(a, *block) if src is None else src, dst_ref=slab(a, *block),
                send_sem=send_sems.at[7 * a + k], recv_sem=recv_sems.at[7 * a + k], device_id=to, device_id_type=MESH)

        mine = [pltpu.make_async_copy(v_refs[a], slab(a, *me), local_sems.at[a]) for a in range(n)]
        for cp in mine:
            cp.start()
        first = []
        for a in range(n):
            first.append(copy(a, 0, me, sibling, src=v_refs[a]))
            first += [copy(a, 1 + j, me, (*chip, c), src=v_refs[a]) for j, chip in enumerate(chips)]
        for cp in first:
            cp.start()
        passed = []
        for j, chip in enumerate(chips):
            for a in range(n):
                copy(a, 1 + j, (*chip, c), me).wait_recv()
                cp = copy(a, 4 + j, (*chip, c), sibling)
                cp.start()
                passed.append(cp)
        for a in range(n):
            copy(a, 0, sibling, me).wait_recv()
            for j, chip in enumerate(chips):
                copy(a, 4 + j, (*chip, 1 - c), me).wait_recv()
        for cp in first + passed:
            cp.wait_send()
        for cp in mine:
            cp.wait()

    return pl.pallas_call(
        body, name=name, in_specs=[ANY] * n, out_specs=[ANY] * n,
        out_shape=[jax.ShapeDtypeStruct((N_DEV,) + v.shape, v.dtype) for v in vs],
        scratch_shapes=[pltpu.SemaphoreType.DMA((7 * n,)), pltpu.SemaphoreType.DMA((7 * n,)), pltpu.SemaphoreType.DMA((n,))],
    )(*vs)


def _all_gather_async(vs, *, name, collective_id):
    n = len(vs)
    v_refs = [jax.new_ref(v, memory_space=pltpu.MemorySpace.HBM) for v in vs]
    out_refs = [jax.empty_ref(jax.ShapeDtypeStruct((N_DEV,) + v.shape, v.dtype), memory_space=pltpu.MemorySpace.HBM)
                for v in vs]

    @pl.kernel(mesh=plsc.ScalarSubcoreMesh(axis_name="seq", num_cores=1), name=name,
               scratch_types=(pltpu.SemaphoreType.DMA((7 * n,)), pltpu.SemaphoreType.DMA((7 * n,)),
                              pltpu.SemaphoreType.DMA((n,))),
               compiler_params=pltpu.CompilerParams(collective_id=collective_id))
    def launch(send_sems, recv_sems, local_sems):
        x, y, c = _place()
        me, sibling = (x, y, c), (x, y, 1 - c)
        chips = _other_chips(x, y)
        peers = [sibling] + [(*chip, c) for chip in chips]
        barrier = pltpu.get_barrier_semaphore()
        for peer in peers:
            pl.semaphore_signal(barrier, inc=1, device_id=peer, device_id_type=MESH)
        pl.semaphore_wait(barrier, len(peers))

        def slab(a, px, py, pc):
            return out_refs[a].at[4 * px + 2 * py + pc]

        def copy(a, k, block, to, src=None):
            return pltpu.make_async_remote_copy(
                src_ref=slab(a, *block) if src is None else src, dst_ref=slab(a, *block),
                send_sem=send_sems.at[7 * a + k], recv_sem=recv_sems.at[7 * a + k], device_id=to, device_id_type=MESH)

        mine = [pltpu.make_async_copy(v_refs[a], slab(a, *me), local_sems.at[a]) for a in range(n)]
        for cp in mine:
            cp.start()
        first = []
        for a in range(n):
            first.append(copy(a, 0, me, sibling, src=v_refs[a]))
            first += [copy(a, 1 + j, me, (*chip, c), src=v_refs[a]) for j, chip in enumerate(chips)]
        for cp in first:
            cp.start()
        passed = []
        for j, chip in enumerate(chips):
            for a in range(n):
                copy(a, 1 + j, (*chip, c), me).wait_recv()
                cp = copy(a, 4 + j, (*chip, c), sibling)
                cp.start()
                passed.append(cp)
        for a in range(n):
            copy(a, 0, sibling, me).wait_recv()
            for j, chip in enumerate(chips):
                copy(a, 4 + j, (*chip, 1 - c), me).wait_recv()
        for cp in first + passed:
            cp.wait_send()
        for cp in mine:
            cp.wait()

    launch()
    return [r[...] for r in out_refs]


def _handshake(peers):
    barrier = pltpu.get_barrier_semaphore()
    for peer in peers:
        pl.semaphore_signal(barrier, inc=1, device_id=peer, device_id_type=MESH)
    pl.semaphore_wait(barrier, len(peers))


def _hbm_refs(arrays, lead):
    src = [jax.new_ref(a, memory_space=pltpu.MemorySpace.HBM) for a in arrays]
    dst = [jax.empty_ref(jax.ShapeDtypeStruct((lead,) + a.shape[1:], a.dtype), memory_space=pltpu.MemorySpace.HBM)
           for a in arrays]
    return src, dst


def _rs_sibling_async(g8s, *, name, collective_id):
    n = len(g8s)
    g_refs, out_refs = _hbm_refs(g8s, 4)

    @pl.kernel(mesh=plsc.ScalarSubcoreMesh(axis_name="seq", num_cores=1), name=name,
               scratch_types=(pltpu.SemaphoreType.DMA((4 * n,)), pltpu.SemaphoreType.DMA((4 * n,))),
               compiler_params=pltpu.CompilerParams(collective_id=collective_id))
    def launch(send_sems, recv_sems):
        x, y, c = _place()
        _handshake([(x, y, 1 - c)])
        copies = [
            pltpu.make_async_remote_copy(
                src_ref=g_refs[a].at[2 * k + 1 - c], dst_ref=out_refs[a].at[k],
                send_sem=send_sems.at[4 * a + k], recv_sem=recv_sems.at[4 * a + k],
                device_id=(x, y, 1 - c), device_id_type=MESH)
            for a in range(n) for k in range(4)]
        for cp in copies:
            cp.start()
        for cp in copies:
            cp.wait()

    launch()
    return [r[...] for r in out_refs]


def _rs_chips_async(h4s, *, name, collective_id):
    n = len(h4s)
    h_refs, out_refs = _hbm_refs(h4s, 3)

    @pl.kernel(mesh=plsc.ScalarSubcoreMesh(axis_name="seq", num_cores=1), name=name,
               scratch_types=(pltpu.SemaphoreType.DMA((3 * n,)), pltpu.SemaphoreType.DMA((3 * n,))),
               compiler_params=pltpu.CompilerParams(collective_id=collective_id))
    def launch(send_sems, recv_sems):
        x, y, c = _place()
        chips = _other_chips(x, y)
        _handshake([(cx, cy, c) for cx, cy in chips])
        copies = [
            pltpu.make_async_remote_copy(
                src_ref=h_refs[a].at[2 * cx + cy], dst_ref=out_refs[a].at[j],
                send_sem=send_sems.at[3 * a + j], recv_sem=recv_sems.at[3 * a + j],
                device_id=(cx, cy, c), device_id_type=MESH)
            for a in range(n) for j, (cx, cy) in enumerate(chips)]
        for cp in copies:
            cp.start()
        for cp in copies:
            cp.wait()

    launch()
    return [r[...] for r in out_refs]


def _peer(x, y, c, k):
    return ((1 - x) if k & 4 else x, (1 - y) if k & 2 else y, (1 - c) if k & 1 else c)


def _rs_direct_async(g8s, *, name, collective_id):
    n = len(g8s)
    g_refs, out_refs = _hbm_refs(g8s, N_DEV - 1)

    @pl.kernel(mesh=plsc.ScalarSubcoreMesh(axis_name="seq", num_cores=1), name=name,
               scratch_types=(pltpu.SemaphoreType.DMA((7 * n,)), pltpu.SemaphoreType.DMA((7 * n,))),
               compiler_params=pltpu.CompilerParams(collective_id=collective_id))
    def launch(send_sems, recv_sems):
        x, y, c = _place()
        peers = [_peer(x, y, c, k) for k in range(1, N_DEV)]
        _handshake(peers)
        copies = [
            pltpu.make_async_remote_copy(
                src_ref=g_refs[a].at[4 * px + 2 * py + pc], dst_ref=out_refs[a].at[k],
                send_sem=send_sems.at[7 * a + k], recv_sem=recv_sems.at[7 * a + k],
                device_id=(px, py, pc), device_id_type=MESH)
            for a in range(n) for k, (px, py, pc) in enumerate(peers)]
        for cp in copies:
            cp.start()
        for cp in copies:
            cp.wait()

    launch()
    return [r[...] for r in out_refs]


def _row_tile(rows):
    if rows <= 512:
        return rows
    return next(t for t in (512, 384, 352, 256, 128) if rows % t == 0)


def _rs_chip_sum(g8, from_sibling, place_idx, *, name):
    _, R, C = g8.shape
    tr = _row_tile(R)

    def body(pi_ref, a_ref, b_ref, f_ref, h_ref):
        s = a_ref[...] + b_ref[...]
        h_ref[...] = s.astype(BF16)

        @pl.when(pl.program_id(1) == pi_ref[1])
        def _():
            f_ref[...] = s

    blk = pl.BlockSpec((None, tr, C), lambda r, k, pi_ref: (k, r, 0))
    return pl.pallas_call(
        body, name=name,
        grid_spec=pltpu.PrefetchScalarGridSpec(
            num_scalar_prefetch=1, grid=(R // tr, 4),
            in_specs=[pl.BlockSpec((None, tr, C), lambda r, k, pi_ref: (2 * k + pi_ref[0], r, 0)), blk],
            out_specs=[pl.BlockSpec((tr, C), lambda r, k, pi_ref: (r, 0)), blk]),
        out_shape=[jax.ShapeDtypeStruct((R, C), F32), jax.ShapeDtypeStruct((4, R, C), BF16)],
        compiler_params=_cparams(("parallel", "arbitrary")),
    )(place_idx, g8, from_sibling)


def _split_moves(segments, chunk):
    moves = []
    for dst, src, length in segments:
        while length > 0:
            dev, off = divmod(src, chunk)
            take = min(length, chunk - off)
            moves.append((dst, dev, off, take))
            dst, src, length = dst + take, src + take, length - take
    return moves


def _assemble(stacked, segments, zero_spans, out_cols, *, name):
    _, R, c = stacked.shape
    tr = _row_tile(R)
    moves = _split_moves(segments, c)

    def body(x_ref, o_ref):
        for dst, dev, off, take in moves:
            o_ref[:, dst:dst + take] = x_ref[dev, :, off:off + take]
        for a, b in zero_spans:
            o_ref[:, a:b] = jnp.zeros((tr, b - a), o_ref.dtype)

    return pl.pallas_call(
        body, name=name, grid=(R // tr,),
        in_specs=[pl.BlockSpec((N_DEV, tr, c), lambda i: (0, i, 0))],
        out_specs=pl.BlockSpec((tr, out_cols), lambda i: (i, 0)),
        out_shape=jax.ShapeDtypeStruct((R, out_cols), stacked.dtype),
        compiler_params=_cparams(("parallel",)),
    )(stacked)


def _disassemble(full, segments, chunk, *, name, out_dtype=F32):
    R = full.shape[0]
    tr = _row_tile(R)
    moves = _split_moves(segments, chunk)

    def body(x_ref, o_ref):
        seen = set()
        for dst, dev, off, take in moves:
            piece = x_ref[:, dst:dst + take]
            if (dev, off) in seen:
                piece = piece + o_ref[dev, :, off:off + take]
            seen.add((dev, off))
            o_ref[dev, :, off:off + take] = piece.astype(out_dtype)

    return pl.pallas_call(
        body, name=name, grid=(R // tr,),
        in_specs=[pl.BlockSpec((tr, full.shape[1]), lambda i: (i, 0))],
        out_specs=pl.BlockSpec((N_DEV, tr, chunk), lambda i: (0, i, 0)),
        out_shape=jax.ShapeDtypeStruct((N_DEV, R, chunk), out_dtype),
        compiler_params=_cparams(("parallel",)),
    )(full)


def _disassemble_rows(full_t, segments, chunk, *, name, out_dtype=F32):
    R = full_t.shape[1]
    tc = next(t for t in (2 * LANES, LANES) if R % t == 0)
    moves = _split_moves(segments, chunk)

    def body(x_ref, o_ref):
        seen = set()
        for dst, dev, off, take in moves:
            piece = x_ref[dst:dst + take, :]
            if (dev, off) in seen:
                piece = piece + o_ref[dev, off:off + take, :]
            seen.add((dev, off))
            o_ref[dev, off:off + take, :] = piece.astype(out_dtype)

    return pl.pallas_call(
        body, name=name, grid=(R // tc,),
        in_specs=[pl.BlockSpec((full_t.shape[0], tc), lambda i: (0, i))],
        out_specs=pl.BlockSpec((N_DEV, chunk, tc), lambda i: (0, 0, i)),
        out_shape=jax.ShapeDtypeStruct((N_DEV, chunk, R), out_dtype),
        compiler_params=_cparams(("parallel",)),
    )(full_t)


_O_CQ = 3 * SB_W
_O_CKV = _O_CQ + Q_LORA
_O_KR = _O_CKV + KV_LORA
SEG_W_IN = ((0, 0, 3 * SB_W), (P_CKV, _O_CKV, KV_LORA), (P_KRT, _O_KR, MLA_ROPE), (P_KRT + MLA_ROPE, _O_KR, MLA_ROPE),
            (P_CQ, _O_CQ, Q_LORA))
ZERO_W_IN = ((P_KRT + 2 * MLA_ROPE, P_CQ),)
SEG_W_UQ = tuple((MLA_NOPE * h, MLA_QK * h, MLA_NOPE) for h in range(MLA_HEADS)) + tuple(
    (MLA_W + LANES * (h // 2) + MLA_ROPE * (h % 2), MLA_QK * h + MLA_NOPE, MLA_ROPE) for h in range(MLA_HEADS))
ZERO_W_UQ = tuple((MLA_W + LANES * g + 2 * MLA_ROPE, MLA_W + LANES * (g + 1)) for g in range(MLA_HEADS // 2))
SEG_W_UKV = tuple((MLA_NOPE * h, (MLA_NOPE + MLA_V) * h, MLA_NOPE) for h in range(MLA_HEADS)) + tuple(
    (MLA_W + MLA_V * h, (MLA_NOPE + MLA_V) * h + MLA_NOPE, MLA_V) for h in range(MLA_HEADS))
SEG_W_UP = tuple((2 * FF_BLK * blk + FF_BLK * half, D_FF * half + FF_BLK * blk, FF_BLK)
                 for half in range(2) for blk in range(N_FF_BLK))


def _sum8(g, *, name):
    _, R, C = g.shape

    def body(g_ref, o_ref):
        acc = g_ref[0]
        for k in range(1, N_DEV):
            acc = acc + g_ref[k]
        o_ref[...] = acc

    return pl.pallas_call(
        body, name=name, out_shape=jax.ShapeDtypeStruct((R, C), F32),
    )(g)


def _adamw_math(w, gf, m, v):
    c1 = 1.0 / (1.0 - ADAM_B1 ** ADAM_STEP)
    c2 = 1.0 / (1.0 - ADAM_B2 ** ADAM_STEP)
    mn = ADAM_B1 * m + (1.0 - ADAM_B1) * gf
    vn = ADAM_B2 * v + (1.0 - ADAM_B2) * (gf * gf)
    return -ADAM_LR * ((mn * c1) / (jnp.sqrt(vn * c2) + ADAM_EPS) + ADAM_WD * w), mn, vn


def _adamw(w, g, m, v, *, name):
    R, C = w.shape
    tr = _row_tile(R)

    def body(w_ref, g_ref, m_ref, v_ref, d_ref, mo_ref, vo_ref):
        d_ref[...], mo_ref[...], vo_ref[...] = _adamw_math(w_ref[...], g_ref[...], m_ref[...], v_ref[...])

    blk = pl.BlockSpec((tr, C), lambda i: (i, 0))
    shp = jax.ShapeDtypeStruct((R, C), F32)
    return pl.pallas_call(
        body, name=name, grid=(R // tr,), in_specs=[blk] * 4, out_specs=[blk] * 3,
        out_shape=[shp, shp, shp], compiler_params=_cparams(("parallel",)),
    )(w, g, m, v)


def _adamw_rs8(g8, r7, me_idx, w, m, v, *, name):
    R, C = w.shape
    tr = _row_tile(R)

    def body(i_ref, f_ref, r_ref, w_ref, m_ref, v_ref, g_ref, d_ref, mo_ref, vo_ref):
        gf = f_ref[...].astype(F32)
        for k in range(N_DEV - 1):
            gf = gf + r_ref[k].astype(F32)
        g_ref[...] = gf
        d_ref[...], mo_ref[...], vo_ref[...] = _adamw_math(w_ref[...], gf, m_ref[...], v_ref[...])

    blk = pl.BlockSpec((tr, C), lambda i, i_ref: (i, 0))
    shp = jax.ShapeDtypeStruct((R, C), F32)
    return pl.pallas_call(
        body, name=name,
        grid_spec=pltpu.PrefetchScalarGridSpec(
            num_scalar_prefetch=1, grid=(R // tr,),
            in_specs=[pl.BlockSpec((None, tr, C), lambda i, i_ref: (i_ref[0], i, 0)),
                      pl.BlockSpec((N_DEV - 1, tr, C), lambda i, i_ref: (0, i, 0)), blk, blk, blk],
            out_specs=[blk] * 4),
        out_shape=[shp] * 4, compiler_params=_cparams(("parallel",)),
    )(me_idx, g8, r7, w, m, v)


def _adamw_rs(own, r3, w, m, v, *, name):
    R, C = w.shape
    tr = _row_tile(R)

    def body(f_ref, r_ref, w_ref, m_ref, v_ref, g_ref, d_ref, mo_ref, vo_ref):
        gf = ((f_ref[...] + r_ref[0].astype(F32)) + r_ref[1].astype(F32)) + r_ref[2].astype(F32)
        g_ref[...] = gf
        d_ref[...], mo_ref[...], vo_ref[...] = _adamw_math(w_ref[...], gf, m_ref[...], v_ref[...])

    blk = pl.BlockSpec((tr, C), lambda i: (i, 0))
    shp = jax.ShapeDtypeStruct((R, C), F32)
    return pl.pallas_call(
        body, name=name, grid=(R // tr,),
        in_specs=[blk, pl.BlockSpec((3, tr, C), lambda i: (0, i, 0)), blk, blk, blk], out_specs=[blk] * 4,
        out_shape=[shp] * 4, compiler_params=_cparams(("parallel",)),
    )(own, r3, w, m, v)


def _ff_interleave(a):
    lead = a.shape[:-1]
    return a.reshape(*lead, 2, N_FF_BLK, FF_BLK).swapaxes(-3, -2).reshape(*lead, 2 * D_FF)


def _ff_deinterleave(a):
    lead = a.shape[:-1]
    return a.reshape(*lead, N_FF_BLK, 2, FF_BLK).swapaxes(-3, -2).reshape(*lead, 2 * D_FF)


SMALL =(("g_mix", D_MODEL), ("g_cq", Q_LORA), ("g_ckv", KV_LORA), ("g_sb_out", SB_W), ("g_mla_out", MLA_W),
         ("g_ffn", D_MODEL), ("conv_b", 2 * D_FF), ("g_final", D_MODEL))
SMALL_ROWS = 88


SMALL_USED = sum(size for _, size in SMALL)


def _pack_small(d, tail=None):
    parts = [d[n].reshape(-1) for n, _ in SMALL] + ([] if tail is None else [tail])
    flat = jnp.concatenate(parts)
    flat = jnp.pad(flat, (0, SMALL_ROWS * LANES - flat.shape[0]))
    return flat.reshape(SMALL_ROWS, LANES)


def _unpack_small(a):
    flat = a.reshape(-1)
    out, off = {}, 0
    for n, size in SMALL:
        out[n] = flat[off:off + size]
        off += size
    return out


def kernel(x, positions, g_mix, w_in, g_cq, w_uq, g_ckv, w_ukv, g_sb_out, g_mla_out, w_out, g_ffn, w_up, conv_w, conv_b, w_down, g_final, loss_target, m_g_mix, m_w_in, m_g_cq, m_w_uq, m_g_ckv, m_w_ukv, m_g_sb_out, m_g_mla_out, m_w_out, m_g_ffn, m_w_up, m_conv_w, m_conv_b, m_w_down, m_g_final, v_g_mix, v_w_in, v_g_cq, v_w_uq, v_g_ckv, v_w_ukv, v_g_sb_out, v_g_mla_out, v_w_out, v_g_ffn, v_w_up, v_conv_w, v_conv_b, v_w_down, v_g_final):
    B, S, D = x.shape
    T = B * S
    xf = x.reshape(T, D)
    tgt = loss_target.reshape(T, D)
    pos = positions.reshape(T, 1)
    half = MLA_ROPE // 2
    inv_freq = 1.0 / (ROPE_BASE ** (jnp.arange(half, dtype=F32) * (2.0 / MLA_ROPE)))
    invf = jnp.tile(inv_freq, LANES // half).reshape(1, LANES)
    place_idx = jnp.stack([lax.axis_index("c"), 2 * lax.axis_index("x") + lax.axis_index("y")]).astype(jnp.int32)
    me_idx = (4 * lax.axis_index("x") + 2 * lax.axis_index("y") + lax.axis_index("c")).astype(jnp.int32).reshape(1)

    names = ("w_in", "w_uq", "w_ukv", "w_out", "w_up", "w_down", "conv_w")
    shard = {"w_in": w_in[0], "w_uq": w_uq[0], "w_ukv": w_ukv[0], "w_out": w_out[0], "w_up": w_up[0],
             "w_down": w_down[0], "conv_w": conv_w[0]}
    sent = {n: shard[n] if n == "conv_w" else shard[n].astype(BF16) for n in names}
    later = names[1:]
    w_in_all = _all_gather([sent["w_in"]], name="ag_w_in")[0]
    w_in_all, rest = lax.optimization_barrier((w_in_all, [sent[n] for n in later]))
    got = {"w_in": w_in_all}
    got.update(zip(later, _all_gather_async(rest, name="ag_weights_async", collective_id=0)))
    wi = _assemble(got["w_in"], SEG_W_IN, ZERO_W_IN, P_COLS, name="asm_w_in")
    wuq = _assemble(got["w_uq"], SEG_W_UQ, ZERO_W_UQ, 2 * MLA_W, name="asm_w_uq")
    wukv = _assemble(got["w_ukv"], SEG_W_UKV, (), 2 * MLA_W, name="asm_w_ukv")
    wup = _assemble(got["w_up"], SEG_W_UP, (), 2 * D_FF, name="asm_w_up")
    cwi = _assemble(got["conv_w"], SEG_W_UP, (), 2 * D_FF, name="asm_conv_w")
    wo = got["w_out"].reshape(D, D)
    wdn = got["w_down"].reshape(D_FF, D)
    cbi = _ff_interleave(conv_b)

    h, p = _rms_matmul_nn(xf, g_mix, wi, tm=512, name="proj_in")
    o_sb, ltot = _sb_fwd(p, seq=S, name="sb_fwd")
    cq, qm, krt = _proj_uq_rope(p, g_cq, wuq, pos, invf, tm=512, name="proj_uq")
    ckv, kvm = _rms_matmul_nn(p, g_ckv, wukv, tm=512, name="proj_ukv", col_block=P_CKV // KV_LORA, out_dtype=BF16)
    o_mla, lse = _mla_fwd(qm, kvm, krt, seq=S, name="mla_fwd")
    ocat, x1 = _heads_out(o_sb, o_mla, g_sb_out, g_mla_out, wo, xf, tm=512, name="proj_out")
    hf, u = _rms_matmul_nn(x1, g_ffn, wup, tm=256, name="ffn_up")
    a = _conv_fwd(u, cwi, cbi, seq=S, name="conv_fwd")
    dx2, dg_final, loss_row = _matmul_nn_loss(a, wdn, x1, g_final.reshape(1, D), tgt, tm=512, name="ffn_down_loss")

    du, dcw, dcb, dw_down = _conv_bwd(u, dx2, wdn, cwi, cbi, seq=S, name="conv_bwd")
    dw_up_t = _matmul_tn(du, hf, tm=D_FF, tn=1024, tk=1024, name="dw_up")
    dx1, dg_ffn = _matmul_nt_rms_bwd(du, wup, x1, g_ffn, tm=512, name="d_ffn_up", residual=dx2)
    dw_out = _matmul_tn(ocat, dx1, tm=1024, tn=1024, tk=1024, name="dw_out", out_dtype=BF16)
    do_sb, do_mla, dg_sb, dg_mla = _heads_out_bwd(dx1, wo, o_sb, o_mla, g_sb_out, g_mla_out, tm=512, name="d_proj_out")

    early = ("w_down", "w_up", "conv_w", "w_out")
    g8 = {"w_up": _disassemble_rows(dw_up_t, SEG_W_UP, shard["w_up"].shape[1], name="split_dw_up", out_dtype=BF16),
          "conv_w": _disassemble(dcw, SEG_W_UP, shard["conv_w"].shape[1], name="split_dconv_w", out_dtype=BF16),
          "w_out": dw_out.reshape((N_DEV,) + shard["w_out"].shape),
          "w_down": dw_down.reshape((N_DEV,) + shard["w_down"].shape)}
    r7 = dict(zip(early, _rs_direct_async([g8[n] for n in early], name="rs_direct_async", collective_id=1)))
    own, r3 = {}, {}

    dq_sb, dk_sb, dv_sb = _sb_bwd(p, ltot, do_sb, seq=S, name="sb_bwd")
    dqn, dqr, dkn, dvm, dkr = _mla_bwd(qm, kvm, krt, o_mla, lse, do_mla, seq=S, name="mla_bwd")
    dqm, dcq, dg_cq, dkr_u = _d_proj_uq_rope(dqn, dqr, dkr, wuq, p, g_cq, pos, invf, tm=512, name="d_proj_uq")
    dkvm, dckv, dg_ckv = _d_proj_cat([dkn, dvm], wukv, p, g_ckv, tm=512, name="d_proj_ukv",
                                     col_block=P_CKV // KV_LORA, out_dtype=BF16)
    dw_uq_t = _matmul_tn(dqm, cq, tm=2 * MLA_W, tn=Q_LORA, tk=1024, name="dw_uq")
    dw_ukv = _matmul_tn(ckv, dkvm, tm=KV_LORA, tn=1024, tk=1024, name="dw_ukv")
    dp, dx, dg_mix = _d_proj_cat([dq_sb, dk_sb, dv_sb, dckv, dkr_u, dcq], wi, xf, g_mix, tm=512, name="d_proj_in",
                                 residual=dx1)
    dw_in_t = _matmul_tn(dp, h, tm=P_COLS, tn=1024, tk=1024, name="dw_in")

    late = ("w_in", "w_uq", "w_ukv")
    g8.update({"w_in": _disassemble_rows(dw_in_t, SEG_W_IN, shard["w_in"].shape[1], name="split_dw_in"),
               "w_uq": _disassemble_rows(dw_uq_t, SEG_W_UQ, shard["w_uq"].shape[1], name="split_dw_uq"),
               "w_ukv": _disassemble(dw_ukv, SEG_W_UKV, shard["w_ukv"].shape[1], name="split_dw_ukv")})
    sib_l = _rs_sibling_async([g8[n] for n in late], name="rs_sibling_late", collective_id=3)

    params = {"w_in": (w_in, m_w_in, v_w_in), "w_uq": (w_uq, m_w_uq, v_w_uq), "w_ukv": (w_ukv, m_w_ukv, v_w_ukv),
              "w_out": (w_out, m_w_out, v_w_out), "w_up": (w_up, m_w_up, v_w_up), "conv_w": (conv_w, m_conv_w, v_conv_w),
              "w_down": (w_down, m_w_down, v_w_down)}
    grad, delta, new_m, new_v = {}, {}, {}, {}

    transposed = ("w_in", "w_uq", "w_up")

    def adamw_group(group):
        for n in group:
            flip = jnp.transpose if n in transposed else (lambda t: t)
            w_, m_, v_ = [flip(t[0]) for t in params[n]]
            if n in r7:
                res = _adamw_rs8(g8[n], r7[n], me_idx, w_, m_, v_, name="adamw_" + n)
            else:
                res = _adamw_rs(own[n], r3[n], w_, m_, v_, name="adamw_" + n)
            grad[n], delta[n], new_m[n], new_v[n] = [flip(r)[None] for r in res]

    adamw_group(("w_up", "w_down"))
    sib_l, grad["w_up"] = lax.optimization_barrier((sib_l, grad["w_up"]))
    sums_l = [_rs_chip_sum(g8[n], fs, place_idx, name="rs_chip_sum_" + n) for n, fs in zip(late, sib_l)]
    r3.update(zip(late, _rs_chips_async([h4 for _, h4 in sums_l], name="rs_chips_late", collective_id=4)))
    own.update({n: f for n, (f, _) in zip(late, sums_l)})
    small_part = {"g_mix": dg_mix, "g_cq": dg_cq, "g_ckv": dg_ckv, "g_sb_out": dg_sb, "g_mla_out": dg_mla,
                  "g_ffn": dg_ffn, "conv_b": _ff_deinterleave(dcb), "g_final": dg_final}
    small_all, = _all_gather_async([_pack_small(small_part, tail=loss_row[0, 0:1])], name="ag_small_async",
                                   collective_id=5)
    adamw_group(("conv_w", "w_out"))
    adamw_group(late)
    gsmall = _sum8(small_all, name="sum_small_grads")
    small_w = {"g_mix": g_mix, "g_cq": g_cq, "g_ckv": g_ckv, "g_sb_out": g_sb_out, "g_mla_out": g_mla_out,
               "g_ffn": g_ffn, "conv_b": conv_b, "g_final": g_final}
    small_m = {"g_mix": m_g_mix, "g_cq": m_g_cq, "g_ckv": m_g_ckv, "g_sb_out": m_g_sb_out, "g_mla_out": m_g_mla_out,
               "g_ffn": m_g_ffn, "conv_b": m_conv_b, "g_final": m_g_final}
    small_v = {"g_mix": v_g_mix, "g_cq": v_g_cq, "g_ckv": v_g_ckv, "g_sb_out": v_g_sb_out, "g_mla_out": v_g_mla_out,
               "g_ffn": v_g_ffn, "conv_b": v_conv_b, "g_final": v_g_final}
    ds_, ms_, vs_ = _adamw(_pack_small(small_w), gsmall, _pack_small(small_m), _pack_small(small_v), name="adamw_small")
    for src, dst in ((_unpack_small(gsmall), grad), (_unpack_small(ds_), delta), (_unpack_small(ms_), new_m), (_unpack_small(vs_), new_v)):
        for n, _ in SMALL:
            dst[n] = src[n].reshape(small_w[n].shape)

    loss = gsmall.reshape(-1)[SMALL_USED]
    order = ("g_mix", "w_in", "g_cq", "w_uq", "g_ckv", "w_ukv", "g_sb_out", "g_mla_out", "w_out", "g_ffn", "w_up",
             "conv_w", "conv_b", "w_down", "g_final")
    return (loss, dx.reshape(B, S, D), *[grad[n] for n in order], *[delta[n] for n in order],
            *[new_m[n] for n in order], *[new_v[n] for n in order])
```

```python
import jax
import jax.numpy as jnp
from jax import lax
from jax.experimental import pallas as pl
from jax.experimental.pallas import tpu as pltpu
from jax.experimental.pallas import tpu_sc as plsc

F32 = jnp.float32
BF16 = jnp.bfloat16

D_MODEL = 1024
SB_HEADS = 8
SB_HEAD_DIM = 64
MLA_HEADS = 8
MLA_NOPE = 64
MLA_ROPE = 32
MLA_V = 64
Q_LORA = 384
KV_LORA = 256
D_FF = 2816
ROPE_BASE = 10000.0
EPS = 1e-6
SB_W = SB_HEADS * SB_HEAD_DIM
MLA_W = MLA_HEADS * MLA_V
MLA_QK = MLA_NOPE + MLA_ROPE

ADAM_LR = 0.001
ADAM_B1 = 0.9
ADAM_B2 = 0.999
ADAM_EPS = 1e-08
ADAM_WD = 0.01
ADAM_STEP = 10

N_DEV = 8
LANES = 128
V7X_VMEM_LIMIT = 56 * 1024 * 1024
FF_BLK = 256
N_FF_BLK = D_FF // FF_BLK

P_Q, P_K, P_V = 0, SB_W, 2 * SB_W
P_CKV = 3 * SB_W
P_KRT = P_CKV + KV_LORA
P_CQ = P_KRT + LANES
P_COLS = P_CQ + Q_LORA

MESH = pl.DeviceIdType.MESH
ANY = pl.BlockSpec(memory_space=pl.ANY)


def _cparams(sem=None, vmem=V7X_VMEM_LIMIT):
    return pltpu.CompilerParams(dimension_semantics=sem, vmem_limit_bytes=vmem)


def _matmul_tn(a, b, *, tm, tn, tk, name, out_dtype=F32):
    K, M = a.shape
    N = b.shape[1]
    assert M % tm == 0 and N % tn == 0 and K % tk == 0, (name, a.shape, b.shape)
    n_k = K // tk
    narrow = out_dtype != F32

    def body(a_ref, b_ref, o_ref, *scratch):
        acc_ref = scratch[0] if narrow else o_ref
        k = pl.program_id(2)
        part = lax.dot_general(a_ref[...].astype(BF16), b_ref[...].astype(BF16), (((0,), (0,)), ((), ())),
                               preferred_element_type=F32)

        @pl.when(k == 0)
        def _():
            acc_ref[...] = part

        @pl.when(k > 0)
        def _():
            acc_ref[...] += part

        if narrow:
            @pl.when(k == n_k - 1)
            def _():
                o_ref[...] = acc_ref[...].astype(out_dtype)

    return pl.pallas_call(
        body, name=name, grid=(M // tm, N // tn, n_k),
        in_specs=[pl.BlockSpec((tk, tm), lambda i, j, k: (k, i)), pl.BlockSpec((tk, tn), lambda i, j, k: (k, j))],
        out_specs=pl.BlockSpec((tm, tn), lambda i, j, k: (i, j)),
        out_shape=jax.ShapeDtypeStruct((M, N), out_dtype),
        scratch_shapes=[pltpu.VMEM((tm, tn), F32)] if narrow else [],
        compiler_params=_cparams(("parallel", "parallel", "arbitrary")),
    )(a, b)


def _rms(xf, g):
    r = lax.rsqrt(jnp.mean(xf * xf, axis=1, keepdims=True) + EPS)
    return (xf * r) * g


def _rms_grad(dyf, xf, g):
    r = lax.rsqrt(jnp.mean(xf * xf, axis=1, keepdims=True) + EPS)
    xh = xf * r
    dyg = dyf * g
    dx = r * (dyg - xh * jnp.mean(dyg * xh, axis=1, keepdims=True))
    return dx, jnp.sum(dyf * xh, axis=0, keepdims=True)


def _accumulate(ref, part):
    @pl.when(pl.program_id(0) == 0)
    def _():
        ref[...] = part

    @pl.when(pl.program_id(0) > 0)
    def _():
        ref[...] += part


def _rms_matmul_nn(x, g, w, *, tm, name, col_block=0, out_dtype=F32):
    T = x.shape[0]
    C, N = w.shape
    assert T % tm == 0, (name, x.shape)

    def body(x_ref, g_ref, w_ref, h_ref, o_ref):
        hb = _rms(x_ref[...], g_ref[...]).astype(BF16)
        h_ref[...] = hb
        o_ref[...] = jnp.dot(hb, w_ref[...], preferred_element_type=F32).astype(out_dtype)

    return pl.pallas_call(
        body, name=name, grid=(T // tm,),
        in_specs=[pl.BlockSpec((tm, C), lambda i: (i, col_block)), pl.BlockSpec((1, C), lambda i: (0, 0)),
                  pl.BlockSpec((C, N), lambda i: (0, 0))],
        out_specs=[pl.BlockSpec((tm, C), lambda i: (i, 0)), pl.BlockSpec((tm, N), lambda i: (i, 0))],
        out_shape=[jax.ShapeDtypeStruct((T, C), BF16), jax.ShapeDtypeStruct((T, N), out_dtype)],
        compiler_params=_cparams(("parallel",)),
    )(x, g, w)


def _matmul_nt_rms_bwd(a, b, x, g, *, tm, name, residual=None, col_block=0, out_dtype=F32):
    M, K = a.shape
    C = b.shape[0]
    assert M % tm == 0, (name, a.shape)
    in_specs = [pl.BlockSpec((tm, K), lambda i: (i, 0)), pl.BlockSpec((C, K), lambda i: (0, 0)),
                pl.BlockSpec((tm, C), lambda i: (i, col_block)), pl.BlockSpec((1, C), lambda i: (0, 0))]
    args = [a, b, x, g]
    if residual is not None:
        in_specs.append(pl.BlockSpec((tm, C), lambda i: (i, 0)))
        args.append(residual)

    def body(*refs):
        a_ref, b_ref, x_ref, g_ref = refs[:4]
        dx_ref, dg_ref = refs[-2:]
        dy = lax.dot_general(a_ref[...].astype(BF16), b_ref[...], (((1,), (1,)), ((), ())), preferred_element_type=F32)
        dx, part = _rms_grad(dy, x_ref[...], g_ref[...])
        if residual is not None:
            dx = dx + refs[4][...]
        dx_ref[...] = dx.astype(out_dtype)
        _accumulate(dg_ref, part)

    return pl.pallas_call(
        body, name=name, grid=(M // tm,), in_specs=in_specs,
        out_specs=[pl.BlockSpec((tm, C), lambda i: (i, 0)), pl.BlockSpec((1, C), lambda i: (0, 0))],
        out_shape=[jax.ShapeDtypeStruct((M, C), out_dtype), jax.ShapeDtypeStruct((1, C), F32)],
        compiler_params=_cparams(("arbitrary",)),
    )(*args)


def _matmul_nn_loss(a, w, x1, g, tgt, *, tm, name):
    M, K = a.shape
    C = w.shape[1]
    assert M % tm == 0, (name, a.shape)

    nsub = 4
    ts = tm // nsub

    def body(a_ref, w_ref, x_ref, g_ref, t_ref, dx_ref, dg_ref, loss_ref):
        gf = g_ref[...]
        wv = w_ref[...]
        rows = [slice(r * ts, (r + 1) * ts) for r in range(nsub)]
        xs = [x_ref[rw, :] + jnp.dot(a_ref[rw, :], wv, preferred_element_type=F32) for rw in rows]
        lpart, gpart = 0.0, 0.0
        for rw, xf in zip(rows, xs):
            err = _rms(xf, gf) - t_ref[rw, :]
            lpart = lpart + 0.5 * jnp.sum(jnp.mean(err * err, axis=1, keepdims=True), axis=0, keepdims=True)
            dx, gp = _rms_grad(err * (1.0 / C), xf, gf)
            dx_ref[rw, :] = dx
            gpart = gpart + gp
        _accumulate(dg_ref, gpart)
        _accumulate(loss_ref, jnp.broadcast_to(lpart, (1, LANES)))

    row = pl.BlockSpec((tm, C), lambda i: (i, 0))
    return pl.pallas_call(
        body, name=name, grid=(M // tm,),
        in_specs=[pl.BlockSpec((tm, K), lambda i: (i, 0)), pl.BlockSpec((K, C), lambda i: (0, 0)), row,
                  pl.BlockSpec((1, C), lambda i: (0, 0)), row],
        out_specs=[row, pl.BlockSpec((1, C), lambda i: (0, 0)), pl.BlockSpec((1, LANES), lambda i: (0, 0))],
        out_shape=[jax.ShapeDtypeStruct((M, C), F32), jax.ShapeDtypeStruct((1, C), F32),
                   jax.ShapeDtypeStruct((1, LANES), F32)],
        compiler_params=_cparams(("arbitrary",)),
    )(a, w, x1, g, tgt)


ATT_T = 256
ATT_PAIRS = 2
NEG_BIG = -1e30


def _lane_iota():
    return lax.broadcasted_iota(jnp.int32, (1, LANES), 1)


def _head_masks():
    first = _lane_iota() < SB_HEAD_DIM
    return first, jnp.logical_not(first)


def _pick(mask, x):
    return jnp.where(mask, x, jnp.zeros_like(x))


def _lane_value(t, lane):
    return jnp.sum(jnp.where(_lane_iota() == lane, t, 0.0), axis=1, keepdims=True)


def _split_hi_lo(x):
    hi = x.astype(BF16)
    lo = (x - hi.astype(F32)).astype(BF16)
    return jnp.concatenate([hi, lo], axis=1)


def _tri(n, kind):
    r = lax.broadcasted_iota(jnp.int32, (n, n), 0)
    c = lax.broadcasted_iota(jnp.int32, (n, n), 1)
    u = {"suffix_excl": r > c, "prefix_incl": r <= c, "prefix_excl": r < c}[kind].astype(BF16)
    return jnp.concatenate([u, u], axis=0)


def _dot_nt(a, b):
    return lax.dot_general(a, b, (((1,), (1,)), ((), ())), preferred_element_type=F32)


def _dot_tn(a, b):
    return lax.dot_general(a, b, (((0,), (0,)), ((), ())), preferred_element_type=F32)


def _dot(a, b):
    return jnp.dot(a, b, preferred_element_type=F32)


def _causal_mask(n, strict):
    r = lax.broadcasted_iota(jnp.int32, (n, n), 0)
    c = lax.broadcasted_iota(jnp.int32, (n, n), 1)
    return (c < r) if strict else (c <= r)


LOG2E = 1.4426950408889634


def _sb_logs(qh, kj, vis):
    z2 = _dot_nt(qh, kj) * LOG2E
    nk = jnp.maximum(z2, 0.0) + jnp.log2(1.0 + jnp.exp2(-jnp.abs(z2)))
    lb = z2 - nk
    if vis is not None:
        nk = jnp.where(vis, nk, 0.0)
    return lb, nk


def _sb_fwd(p, *, seq, name):
    T = p.shape[0]
    B = T // seq
    TQ = ATT_T
    nq = seq // TQ
    PP = ATT_PAIRS
    W = PP * LANES
    nstep = SB_W // W
    NH = 2 * PP

    def body(q_ref, k_ref, v_ref, o_ref, lt_ref, q_s, k_s, v_s):
        masks = _head_masks()
        q = q_ref[...] * (SB_HEAD_DIM ** -0.5)
        v = v_ref[...]
        k_s[...] = k_ref[...].astype(BF16)
        for h in range(NH):
            ps = slice((h // 2) * LANES, (h // 2 + 1) * LANES)
            hs = slice(h * LANES, (h + 1) * LANES)
            q_s[:, hs] = _pick(masks[h % 2], q[:, ps]).astype(BF16)
            v_s[:, hs] = _pick(masks[h % 2], v[:, ps]).astype(BF16)
        u_suf = _tri(TQ, "suffix_excl")
        vis = _causal_mask(TQ, True)

        def q_block(i, carry):
            q0 = pl.multiple_of(i * TQ, TQ)
            qs = [q_s[pl.ds(q0, TQ), h * LANES:(h + 1) * LANES] for h in range(NH)]

            def tile(k0, c, mask):
                rs, accs = list(c[:NH]), list(c[NH:])
                logs = [_sb_logs(qs[h], k_s[pl.ds(k0, TQ), (h // 2) * LANES:(h // 2 + 1) * LANES], mask) for h in range(NH)]
                sums = [_dot(_split_hi_lo(nk), u_suf) for _, nk in logs]
                for h in range(NH):
                    a = jnp.exp2(logs[h][0] - sums[h] - rs[h])
                    if mask is not None:
                        a = jnp.where(mask, a, 0.0)
                    accs[h // 2] = accs[h // 2] + _dot(a.astype(BF16), v_s[pl.ds(k0, TQ), h * LANES:(h + 1) * LANES])
                    rs[h] = rs[h] + jnp.sum(logs[h][1], axis=1, keepdims=True)
                return tuple(rs) + tuple(accs)

            zero = jnp.zeros((TQ, 1), F32)
            c = tile(q0, (zero,) * NH + (jnp.zeros((TQ, LANES), F32),) * PP, vis)

            def k_block(jj, c):
                return tile(pl.multiple_of((i - 1 - jj) * TQ, TQ), c, None)

            c = lax.fori_loop(0, i, k_block, c)
            for pr in range(PP):
                ps = slice(pr * LANES, (pr + 1) * LANES)
                o_ref[pl.ds(q0, TQ), ps] = c[NH + pr]
                lt_ref[pl.ds(q0, TQ), ps] = jnp.where(masks[0], c[2 * pr], c[2 * pr + 1])
            return carry

        lax.fori_loop(0, nq, q_block, 0)

    blk = lambda off: pl.BlockSpec((seq, W), lambda b, g: (b, off + g))
    out_blk = pl.BlockSpec((seq, W), lambda b, g: (b, g))
    return pl.pallas_call(
        body, name=name, grid=(B, nstep),
        in_specs=[blk(P_Q // W), blk(P_K // W), blk(P_V // W)],
        out_specs=[out_blk, out_blk],
        out_shape=[jax.ShapeDtypeStruct((T, SB_W), F32), jax.ShapeDtypeStruct((T, SB_W), F32)],
        scratch_shapes=[pltpu.VMEM((seq, NH * LANES), BF16), pltpu.VMEM((seq, W), BF16), pltpu.VMEM((seq, NH * LANES), BF16)],
        compiler_params=_cparams(("parallel", "parallel")),
    )(p, p, p)


def _sb_bwd(p, ltot, do, *, seq, name):
    T = p.shape[0]
    B = T // seq
    TQ = ATT_T
    nq = seq // TQ
    PP = ATT_PAIRS
    W = PP * LANES
    nstep = SB_W // W
    NH = 2 * PP
    scale = SB_HEAD_DIM ** -0.5

    def body(q_ref, k_ref, v_ref, lt_ref, do_ref, dq_ref, dk_ref, dv_ref, q_s, k_s, v_s, do_s, dk_s, dv_s):
        masks = _head_masks()
        q = q_ref[...] * scale
        dof = do_ref[...]
        k_s[...] = k_ref[...].astype(BF16)
        v_s[...] = v_ref[...].astype(BF16)
        for h in range(NH):
            ps = slice((h // 2) * LANES, (h // 2 + 1) * LANES)
            hs = slice(h * LANES, (h + 1) * LANES)
            q_s[:, hs] = _pick(masks[h % 2], q[:, ps]).astype(BF16)
            do_s[:, hs] = _pick(masks[h % 2], dof[:, ps]).astype(BF16)
        dk_s[...] = jnp.zeros_like(dk_s)
        dv_s[...] = jnp.zeros_like(dv_s)
        u_pin = _tri(TQ, "prefix_incl")
        u_pex = _tri(TQ, "prefix_excl")[:TQ]
        vis = _causal_mask(TQ, True)

        def q_block(i, carry):
            q0 = pl.multiple_of(i * TQ, TQ)
            qs = [q_s[pl.ds(q0, TQ), h * LANES:(h + 1) * LANES] for h in range(NH)]
            dos = [do_s[pl.ds(q0, TQ), h * LANES:(h + 1) * LANES] for h in range(NH)]
            lt = lt_ref[pl.ds(q0, TQ), :]
            lts = [_lane_value(lt[:, (h // 2) * LANES:(h // 2 + 1) * LANES], (h % 2) * SB_HEAD_DIM) for h in range(NH)]

            def tile(k0, c, mask):
                cs, gs, accs = list(c[:NH]), list(c[NH:2 * NH]), list(c[2 * NH:])
                kjs = [k_s[pl.ds(k0, TQ), pr * LANES:(pr + 1) * LANES] for pr in range(PP)]
                vjs = [v_s[pl.ds(k0, TQ), pr * LANES:(pr + 1) * LANES] for pr in range(PP)]
                logs = [_sb_logs(qs[h], kjs[h // 2], mask) for h in range(NH)]
                pins = [_dot(_split_hi_lo(nk), u_pin) for _, nk in logs]
                das = [_dot_nt(dos[h], vjs[h // 2]) for h in range(NH)]
                a_l, g_l = [], []
                for h in range(NH):
                    a = jnp.exp2(logs[h][0] - ((lts[h] - cs[h]) - pins[h]))
                    if mask is not None:
                        a = jnp.where(mask, a, 0.0)
                    a_l.append(a)
                    g_l.append(das[h] * a)
                pres = [_dot(g.astype(BF16), u_pex) for g in g_l]
                dz_l = []
                for h in range(NH):
                    dz = g_l[h] - jnp.exp2(logs[h][0]) * (g_l[h] + (pres[h] + gs[h]))
                    if mask is not None:
                        dz = jnp.where(mask, dz, 0.0)
                    dz_l.append(dz.astype(BF16))
                for h in range(NH):
                    accs[h] = accs[h] + _dot(dz_l[h], kjs[h // 2])
                for pr in range(PP):
                    ps = slice(pr * LANES, (pr + 1) * LANES)
                    ha, hb = 2 * pr, 2 * pr + 1
                    dk_s[pl.ds(k0, TQ), ps] += _dot_tn(dz_l[ha], qs[ha]) + _dot_tn(dz_l[hb], qs[hb])
                    dv_s[pl.ds(k0, TQ), ps] += _dot_tn(a_l[ha].astype(BF16), dos[ha]) + _dot_tn(a_l[hb].astype(BF16), dos[hb])
                for h in range(NH):
                    cs[h] = cs[h] + jnp.sum(logs[h][1], axis=1, keepdims=True)
                    gs[h] = gs[h] + jnp.sum(g_l[h], axis=1, keepdims=True)
                return tuple(cs) + tuple(gs) + tuple(accs)

            z1 = jnp.zeros((TQ, 1), F32)
            zl = jnp.zeros((TQ, LANES), F32)

            def k_block(j, c):
                return tile(pl.multiple_of(j * TQ, TQ), c, None)

            c = lax.fori_loop(0, i, k_block, (z1,) * (2 * NH) + (zl,) * NH)
            c = tile(q0, c, vis)
            for pr in range(PP):
                dq = jnp.where(masks[0], c[2 * NH + 2 * pr], c[2 * NH + 2 * pr + 1]) * scale
                dq_ref[pl.ds(q0, TQ), pr * LANES:(pr + 1) * LANES] = dq.astype(BF16)
            return carry

        lax.fori_loop(0, nq, q_block, 0)
        dk_ref[...] = dk_s[...].astype(BF16)
        dv_ref[...] = dv_s[...].astype(BF16)

    blk = lambda off: pl.BlockSpec((seq, W), lambda b, g: (b, off + g))
    out_blk = pl.BlockSpec((seq, W), lambda b, g: (b, g))
    return pl.pallas_call(
        body, name=name, grid=(B, nstep),
        in_specs=[blk(P_Q // W), blk(P_K // W), blk(P_V // W), out_blk, out_blk],
        out_specs=[out_blk, out_blk, out_blk],
        out_shape=[jax.ShapeDtypeStruct((T, SB_W), BF16) for _ in range(3)],
        scratch_shapes=[pltpu.VMEM((seq, NH * LANES), BF16), pltpu.VMEM((seq, W), BF16), pltpu.VMEM((seq, W), BF16),
                        pltpu.VMEM((seq, NH * LANES), BF16), pltpu.VMEM((seq, W), F32), pltpu.VMEM((seq, W), F32)],
        compiler_params=_cparams(("parallel", "parallel")),
    )(p, p, p, ltot, do)


def _mla_masks():
    lane = lax.broadcasted_iota(jnp.int32, (1, 2 * LANES), 1)
    ma = (lane < MLA_NOPE) | ((lane >= LANES) & (lane < LANES + MLA_ROPE))
    mb = ((lane >= MLA_NOPE) & (lane < LANES)) | ((lane >= LANES + MLA_ROPE) & (lane < LANES + 2 * MLA_ROPE))
    return ma, mb


def _mla_fwd(qm, kvm, krt, *, seq, name):
    T = qm.shape[0]
    B = T // seq
    TQ = ATT_T
    nq = seq // TQ
    PP = ATT_PAIRS
    W = PP * LANES
    nstep = MLA_W // W
    NH = 2 * PP
    CW = 2 * LANES
    scale = MLA_QK ** -0.5

    def body(qn_ref, qr_ref, kn_ref, v_ref, kr_ref, o_ref, lse_ref, q_s, kc_s, v_s):
        hm = _head_masks()
        mm = _mla_masks()
        v = v_ref[...]
        for pr in range(PP):
            ps = slice(pr * LANES, (pr + 1) * LANES)
            qc = jnp.concatenate([qn_ref[:, ps], qr_ref[:, ps]], axis=1)
            kc_s[:, pr * CW:(pr + 1) * CW] = jnp.concatenate([kn_ref[:, ps], kr_ref[...]], axis=1)
            for e in range(2):
                h = 2 * pr + e
                q_s[:, h * CW:(h + 1) * CW] = _pick(mm[e], qc)
                v_s[:, h * LANES:(h + 1) * LANES] = _pick(hm[e], v[:, ps])
        vis = _causal_mask(TQ, False)

        def q_block(i, carry):
            q0 = pl.multiple_of(i * TQ, TQ)
            qs = [q_s[pl.ds(q0, TQ), h * CW:(h + 1) * CW] for h in range(NH)]

            def tile(k0, c, mask):
                ms, ls, accs = list(c[:NH]), list(c[NH:2 * NH]), list(c[2 * NH:])
                ss = [_dot_nt(qs[h], kc_s[pl.ds(k0, TQ), (h // 2) * CW:(h // 2 + 1) * CW]) * scale for h in range(NH)]
                if mask is not None:
                    ss = [jnp.where(mask, s, NEG_BIG) for s in ss]
                m_new = [jnp.maximum(ms[h], jnp.max(ss[h], axis=1, keepdims=True)) for h in range(NH)]
                alphas = [jnp.exp(ms[h] - m_new[h]) for h in range(NH)]
                prs = [jnp.exp(ss[h] - m_new[h]) for h in range(NH)]
                outs = [_dot(prs[h].astype(BF16), v_s[pl.ds(k0, TQ), h * LANES:(h + 1) * LANES]) for h in range(NH)]
                ls = [alphas[h] * ls[h] + jnp.sum(prs[h], axis=1, keepdims=True) for h in range(NH)]
                for pr in range(PP):
                    accs[pr] = accs[pr] * jnp.where(hm[0], alphas[2 * pr], alphas[2 * pr + 1]) + outs[2 * pr] + outs[2 * pr + 1]
                return tuple(m_new) + tuple(ls) + tuple(accs)

            neg = jnp.full((TQ, 1), NEG_BIG, F32)
            z1 = jnp.zeros((TQ, 1), F32)

            def k_block(j, c):
                return tile(pl.multiple_of(j * TQ, TQ), c, None)

            c = lax.fori_loop(0, i, k_block, (neg,) * NH + (z1,) * NH + (jnp.zeros((TQ, LANES), F32),) * PP)
            c = tile(q0, c, vis)
            for pr in range(PP):
                ps = slice(pr * LANES, (pr + 1) * LANES)
                m_a, m_b, l_a, l_b = c[2 * pr], c[2 * pr + 1], c[NH + 2 * pr], c[NH + 2 * pr + 1]
                o_ref[pl.ds(q0, TQ), ps] = c[2 * NH + pr] / jnp.where(hm[0], l_a, l_b)
                lse_ref[pl.ds(q0, TQ), ps] = jnp.where(hm[0], m_a + jnp.log(l_a), m_b + jnp.log(l_b))
            return carry

        lax.fori_loop(0, nq, q_block, 0)

    blk = lambda off: pl.BlockSpec((seq, W), lambda b, g: (b, off + g))
    out_blk = pl.BlockSpec((seq, W), lambda b, g: (b, g))
    return pl.pallas_call(
        body, name=name, grid=(B, nstep),
        in_specs=[blk(0), blk(nstep), blk(0), blk(nstep), pl.BlockSpec((seq, LANES), lambda b, g: (b, 0))],
        out_specs=[out_blk, out_blk],
        out_shape=[jax.ShapeDtypeStruct((T, MLA_W), F32), jax.ShapeDtypeStruct((T, MLA_W), F32)],
        scratch_shapes=[pltpu.VMEM((seq, NH * CW), BF16), pltpu.VMEM((seq, PP * CW), BF16), pltpu.VMEM((seq, NH * LANES), BF16)],
        compiler_params=_cparams(("parallel", "parallel")),
    )(qm, qm, kvm, kvm, krt)


def _mla_bwd(qm, kvm, krt, o, lse, do, *, seq, name):
    T = qm.shape[0]
    B = T // seq
    TQ = ATT_T
    nq = seq // TQ
    PP = ATT_PAIRS
    W = PP * LANES
    nstep = MLA_W // W
    NH = 2 * PP
    CW = 2 * LANES
    scale = MLA_QK ** -0.5

    def body(qn_ref, qr_ref, kn_ref, v_ref, kr_ref, o_ref, lse_ref, do_ref,
             dqn_ref, dqr_ref, dkn_ref, dv_ref, dkr_ref, q_s, kc_s, do_s, dkc_s, dv_s):
        hm = _head_masks()
        mm = _mla_masks()
        dof = do_ref[...]
        for pr in range(PP):
            ps = slice(pr * LANES, (pr + 1) * LANES)
            qc = jnp.concatenate([qn_ref[:, ps], qr_ref[:, ps]], axis=1)
            kc_s[:, pr * CW:(pr + 1) * CW] = jnp.concatenate([kn_ref[:, ps], kr_ref[...]], axis=1)
            for e in range(2):
                h = 2 * pr + e
                q_s[:, h * CW:(h + 1) * CW] = _pick(mm[e], qc)
                do_s[:, h * LANES:(h + 1) * LANES] = _pick(hm[e], dof[:, ps]).astype(BF16)
        dkc_s[...] = jnp.zeros_like(dkc_s)
        dv_s[...] = jnp.zeros_like(dv_s)
        vis = _causal_mask(TQ, False)

        def q_block(i, carry):
            q0 = pl.multiple_of(i * TQ, TQ)
            qs = [q_s[pl.ds(q0, TQ), h * CW:(h + 1) * CW] for h in range(NH)]
            dos = [do_s[pl.ds(q0, TQ), h * LANES:(h + 1) * LANES] for h in range(NH)]
            lse_t = lse_ref[pl.ds(q0, TQ), :]
            dd = do_ref[pl.ds(q0, TQ), :] * o_ref[pl.ds(q0, TQ), :]
            lses, ds_ = [], []
            for h in range(NH):
                ps = slice((h // 2) * LANES, (h // 2 + 1) * LANES)
                lses.append(_lane_value(lse_t[:, ps], (h % 2) * MLA_V))
                ds_.append(jnp.sum(_pick(hm[h % 2], dd[:, ps]), axis=1, keepdims=True))

            def tile(k0, c, mask):
                accs = list(c)
                kcs = [kc_s[pl.ds(k0, TQ), pr * CW:(pr + 1) * CW] for pr in range(PP)]
                vjs = [v_ref[pl.ds(k0, TQ), pr * LANES:(pr + 1) * LANES] for pr in range(PP)]
                ss = [_dot_nt(qs[h], kcs[h // 2]) * scale for h in range(NH)]
                dps = [_dot_nt(dos[h], vjs[h // 2]) for h in range(NH)]
                p_l, ds_l = [], []
                for h in range(NH):
                    pr_ = jnp.exp(ss[h] - lses[h])
                    if mask is not None:
                        pr_ = jnp.where(mask, pr_, 0.0)
                    p_l.append(pr_.astype(BF16))
                    ds_l.append((pr_ * (dps[h] - ds_[h]) * scale).astype(BF16))
                for h in range(NH):
                    accs[h] = accs[h] + _dot(ds_l[h], kcs[h // 2])
                for pr in range(PP):
                    ha, hb = 2 * pr, 2 * pr + 1
                    dkc_s[pl.ds(k0, TQ), pr * CW:(pr + 1) * CW] += _dot_tn(ds_l[ha], qs[ha]) + _dot_tn(ds_l[hb], qs[hb])
                    dv_s[pl.ds(k0, TQ), pr * LANES:(pr + 1) * LANES] += _dot_tn(p_l[ha], dos[ha]) + _dot_tn(p_l[hb], dos[hb])
                return tuple(accs)

            zc = jnp.zeros((TQ, CW), F32)

            def k_block(j, c):
                return tile(pl.multiple_of(j * TQ, TQ), c, None)

            c = lax.fori_loop(0, i, k_block, (zc,) * NH)
            c = tile(q0, c, vis)
            for pr in range(PP):
                ps = slice(pr * LANES, (pr + 1) * LANES)
                dq = _pick(mm[0], c[2 * pr]) + _pick(mm[1], c[2 * pr + 1])
                dqn_ref[pl.ds(q0, TQ), ps] = dq[:, :LANES].astype(BF16)
                dqr_ref[pl.ds(q0, TQ), ps] = dq[:, LANES:]
            return carry

        lax.fori_loop(0, nq, q_block, 0)
        dkr = dkc_s[:, LANES:CW]
        for pr in range(PP):
            dkn_ref[:, pr * LANES:(pr + 1) * LANES] = dkc_s[:, pr * CW:pr * CW + LANES].astype(BF16)
            if pr > 0:
                dkr = dkr + dkc_s[:, pr * CW + LANES:(pr + 1) * CW]
        dv_ref[...] = dv_s[...].astype(BF16)
        g = pl.program_id(1)

        @pl.when(g == 0)
        def _():
            dkr_ref[...] = dkr

        @pl.when(g > 0)
        def _():
            dkr_ref[...] += dkr

    blk = lambda off: pl.BlockSpec((seq, W), lambda b, g: (b, off + g))
    out_blk = pl.BlockSpec((seq, W), lambda b, g: (b, g))
    one_blk = pl.BlockSpec((seq, LANES), lambda b, g: (b, 0))
    return pl.pallas_call(
        body, name=name, grid=(B, nstep),
        in_specs=[blk(0), blk(nstep), blk(0), blk(nstep), one_blk, out_blk, out_blk, out_blk],
        out_specs=[out_blk, out_blk, out_blk, out_blk, one_blk],
        out_shape=[jax.ShapeDtypeStruct((T, MLA_W), BF16), jax.ShapeDtypeStruct((T, MLA_W), F32),
                   jax.ShapeDtypeStruct((T, MLA_W), BF16), jax.ShapeDtypeStruct((T, MLA_W), BF16),
                   jax.ShapeDtypeStruct((T, LANES), F32)],
        scratch_shapes=[pltpu.VMEM((seq, NH * CW), BF16), pltpu.VMEM((seq, PP * CW), BF16), pltpu.VMEM((seq, NH * LANES), BF16),
                        pltpu.VMEM((seq, PP * CW), F32), pltpu.VMEM((seq, W), F32)],
        compiler_params=_cparams(("parallel", "arbitrary")),
    )(qm, qm, kvm, kvm, krt, o, lse, do)


def _rope_tables(pos_ref, invf_ref):
    ang = pos_ref[...].astype(F32) * invf_ref[...]
    first = (_lane_iota() % MLA_ROPE) < (MLA_ROPE // 2)
    return jnp.cos(ang), jnp.sin(ang), first


def _rope_apply(x, cos, sin, first):
    rot = jnp.where(first, -pltpu.roll(x, LANES - MLA_ROPE // 2, 1), pltpu.roll(x, MLA_ROPE // 2, 1))
    return x * cos + rot * sin


def _rope_apply_t(dy, cos, sin, first):
    dys = dy * sin
    rot_t = jnp.where(first, pltpu.roll(dys, LANES - MLA_ROPE // 2, 1), -pltpu.roll(dys, MLA_ROPE // 2, 1))
    return dy * cos + rot_t


def _proj_uq_rope(p, g, wuq, pos, invf, *, tm, name):
    T = p.shape[0]
    ntile = MLA_W // LANES

    def body(x_ref, kr_ref, g_ref, w_ref, pos_ref, invf_ref, cq_ref, qm_ref, krt_ref):
        cos, sin, first = _rope_tables(pos_ref, invf_ref)
        hb = _rms(x_ref[...], g_ref[...]).astype(BF16)
        cq_ref[...] = hb
        q = jnp.dot(hb, w_ref[...], preferred_element_type=F32)
        qm_ref[:, :MLA_W] = q[:, :MLA_W].astype(BF16)
        for t in range(ntile):
            sl = slice(MLA_W + t * LANES, MLA_W + (t + 1) * LANES)
            qm_ref[:, sl] = _rope_apply(q[:, sl], cos, sin, first).astype(BF16)
        krt_ref[...] = _rope_apply(kr_ref[...], cos, sin, first).astype(BF16)

    return pl.pallas_call(
        body, name=name, grid=(T // tm,),
        in_specs=[pl.BlockSpec((tm, Q_LORA), lambda i: (i, P_CQ // Q_LORA)), pl.BlockSpec((tm, LANES), lambda i: (i, P_KRT // LANES)),
                  pl.BlockSpec((1, Q_LORA), lambda i: (0, 0)), pl.BlockSpec((Q_LORA, 2 * MLA_W), lambda i: (0, 0)),
                  pl.BlockSpec((tm, 1), lambda i: (i, 0)), pl.BlockSpec((1, LANES), lambda i: (0, 0))],
        out_specs=[pl.BlockSpec((tm, Q_LORA), lambda i: (i, 0)), pl.BlockSpec((tm, 2 * MLA_W), lambda i: (i, 0)),
                   pl.BlockSpec((tm, LANES), lambda i: (i, 0))],
        out_shape=[jax.ShapeDtypeStruct((T, Q_LORA), BF16), jax.ShapeDtypeStruct((T, 2 * MLA_W), BF16),
                   jax.ShapeDtypeStruct((T, LANES), BF16)],
        compiler_params=_cparams(("parallel",)),
    )(p, p, g, wuq, pos, invf)


def _d_proj_uq_rope(dqn, dqr, dkr, wuq, p, g, pos, invf, *, tm, name):
    T = dqn.shape[0]
    ntile = MLA_W // LANES

    def body(dqn_ref, dqr_ref, dkr_ref, w_ref, x_ref, g_ref, pos_ref, invf_ref, dqm_ref, dx_ref, dg_ref, dkr_o_ref):
        cos, sin, first = _rope_tables(pos_ref, invf_ref)
        dqm_ref[:, :MLA_W] = dqn_ref[...]
        for t in range(ntile):
            sl = slice(t * LANES, (t + 1) * LANES)
            dqm_ref[:, MLA_W + t * LANES:MLA_W + (t + 1) * LANES] = _rope_apply_t(dqr_ref[:, sl], cos, sin, first).astype(BF16)
        dkr_o_ref[...] = _rope_apply_t(dkr_ref[...], cos, sin, first).astype(BF16)
        dy = lax.dot_general(dqm_ref[...], w_ref[...], (((1,), (1,)), ((), ())), preferred_element_type=F32)
        dx, part = _rms_grad(dy, x_ref[...], g_ref[...])
        dx_ref[...] = dx.astype(BF16)
        _accumulate(dg_ref, part)

    half = pl.BlockSpec((tm, MLA_W), lambda i: (i, 0))
    tile = pl.BlockSpec((tm, LANES), lambda i: (i, 0))
    return pl.pallas_call(
        body, name=name, grid=(T // tm,),
        in_specs=[half, half, tile, pl.BlockSpec((Q_LORA, 2 * MLA_W), lambda i: (0, 0)),
                  pl.BlockSpec((tm, Q_LORA), lambda i: (i, P_CQ // Q_LORA)), pl.BlockSpec((1, Q_LORA), lambda i: (0, 0)),
                  pl.BlockSpec((tm, 1), lambda i: (i, 0)), pl.BlockSpec((1, LANES), lambda i: (0, 0))],
        out_specs=[pl.BlockSpec((tm, 2 * MLA_W), lambda i: (i, 0)), pl.BlockSpec((tm, Q_LORA), lambda i: (i, 0)),
                   pl.BlockSpec((1, Q_LORA), lambda i: (0, 0)), tile],
        out_shape=[jax.ShapeDtypeStruct((T, 2 * MLA_W), BF16), jax.ShapeDtypeStruct((T, Q_LORA), BF16),
                   jax.ShapeDtypeStruct((1, Q_LORA), F32), jax.ShapeDtypeStruct((T, LANES), BF16)],
        compiler_params=_cparams(("arbitrary",)),
    )(dqn, dqr, dkr, wuq, p, g, pos, invf)


def _d_proj_cat(pieces, b, x, g, *, tm, name, residual=None, col_block=0, out_dtype=F32):
    M = pieces[0].shape[0]
    widths = [pc.shape[1] for pc in pieces]
    K = sum(widths)
    C = b.shape[0]
    n = len(pieces)
    in_specs = [pl.BlockSpec((tm, w), lambda i: (i, 0)) for w in widths]
    in_specs += [pl.BlockSpec((C, K), lambda i: (0, 0)), pl.BlockSpec((tm, C), lambda i: (i, col_block)),
                 pl.BlockSpec((1, C), lambda i: (0, 0))]
    args = list(pieces) + [b, x, g]
    if residual is not None:
        in_specs.append(pl.BlockSpec((tm, C), lambda i: (i, 0)))
        args.append(residual)

    def body(*refs):
        b_ref, x_ref, g_ref = refs[n:n + 3]
        cat_ref, dx_ref, dg_ref = refs[-3:]
        off = 0
        for r, w in zip(refs[:n], widths):
            cat_ref[:, off:off + w] = r[...]
            off += w
        dy = lax.dot_general(cat_ref[...], b_ref[...], (((1,), (1,)), ((), ())), preferred_element_type=F32)
        dx, part = _rms_grad(dy, x_ref[...], g_ref[...])
        if residual is not None:
            dx = dx + refs[n + 3][...]
        dx_ref[...] = dx.astype(out_dtype)
        _accumulate(dg_ref, part)

    return pl.pallas_call(
        body, name=name, grid=(M // tm,), in_specs=in_specs,
        out_specs=[pl.BlockSpec((tm, K), lambda i: (i, 0)), pl.BlockSpec((tm, C), lambda i: (i, 0)),
                   pl.BlockSpec((1, C), lambda i: (0, 0))],
        out_shape=[jax.ShapeDtypeStruct((M, K), BF16), jax.ShapeDtypeStruct((M, C), out_dtype),
                   jax.ShapeDtypeStruct((1, C), F32)],
        compiler_params=_cparams(("arbitrary",)),
    )(*args)


def _heads_out(xa, xb, ga, gb, w, resid, *, tm, name):
    T, C = xa.shape
    N = w.shape[1]

    def body(xa_ref, xb_ref, ga_ref, gb_ref, w_ref, r_ref, oc_ref, o_ref):
        oc_ref[:, :C] = _rms(xa_ref[...], ga_ref[...]).astype(BF16)
        oc_ref[:, C:] = _rms(xb_ref[...], gb_ref[...]).astype(BF16)
        o_ref[...] = r_ref[...] + jnp.dot(oc_ref[...], w_ref[...], preferred_element_type=F32)

    row = pl.BlockSpec((tm, C), lambda i: (i, 0))
    gsp = pl.BlockSpec((1, C), lambda i: (0, 0))
    full = pl.BlockSpec((tm, N), lambda i: (i, 0))
    return pl.pallas_call(
        body, name=name, grid=(T // tm,),
        in_specs=[row, row, gsp, gsp, pl.BlockSpec((2 * C, N), lambda i: (0, 0)), full],
        out_specs=[pl.BlockSpec((tm, 2 * C), lambda i: (i, 0)), full],
        out_shape=[jax.ShapeDtypeStruct((T, 2 * C), BF16), jax.ShapeDtypeStruct((T, N), F32)],
        compiler_params=_cparams(("parallel",)),
    )(xa, xb, ga, gb, w, resid)


def _heads_out_bwd(dout, w, xa, xb, ga, gb, *, tm, name):
    T, C = xa.shape
    N = w.shape[1]

    def body(d_ref, w_ref, xa_ref, xb_ref, ga_ref, gb_ref, dxa_ref, dxb_ref, dga_ref, dgb_ref):
        dy = lax.dot_general(d_ref[...].astype(BF16), w_ref[...], (((1,), (1,)), ((), ())), preferred_element_type=F32)
        dxa, pa = _rms_grad(dy[:, :C], xa_ref[...], ga_ref[...])
        dxb, pb = _rms_grad(dy[:, C:], xb_ref[...], gb_ref[...])
        dxa_ref[...] = dxa
        dxb_ref[...] = dxb
        _accumulate(dga_ref, pa)
        _accumulate(dgb_ref, pb)

    row = pl.BlockSpec((tm, C), lambda i: (i, 0))
    gsp = pl.BlockSpec((1, C), lambda i: (0, 0))
    return pl.pallas_call(
        body, name=name, grid=(T // tm,),
        in_specs=[pl.BlockSpec((tm, N), lambda i: (i, 0)), pl.BlockSpec((2 * C, N), lambda i: (0, 0)), row, row, gsp, gsp],
        out_specs=[row, row, gsp, gsp],
        out_shape=[jax.ShapeDtypeStruct((T, C), F32), jax.ShapeDtypeStruct((T, C), F32),
                   jax.ShapeDtypeStruct((1, C), F32), jax.ShapeDtypeStruct((1, C), F32)],
        compiler_params=_cparams(("arbitrary",)),
    )(dout, w, xa, xb, ga, gb)


CONV_ROWS = 256
HALO = 8


def _conv_taps(w_ref):
    return w_ref[0:1, :], w_ref[1:2, :], w_ref[2:3, :]


def _conv_rows(cur, prev, w, bias):
    ext = jnp.concatenate([prev, cur], axis=0)
    u1 = pltpu.roll(ext, 1, 0)[HALO:]
    u2 = pltpu.roll(ext, 2, 0)[HALO:]
    return w[2] * cur + w[1] * u1 + w[0] * u2 + bias, u1, u2


def _conv_fwd(u, w, bias, *, seq, name):
    T = u.shape[0]
    B = T // seq
    W2 = 2 * FF_BLK

    def body(u_ref, w_ref, b_ref, a_ref):
        wv = _conv_taps(w_ref)
        bv = b_ref[...]
        for c in range(seq // CONV_ROWS):
            r0 = c * CONV_ROWS
            cur = u_ref[r0:r0 + CONV_ROWS, :]
            prev = u_ref[r0 - HALO:r0, :] if c > 0 else jnp.zeros((HALO, W2), F32)
            y, _, _ = _conv_rows(cur, prev, wv, bv)
            gc = y[:, :FF_BLK]
            a_ref[r0:r0 + CONV_ROWS, :] = (gc * (1.0 / (1.0 + jnp.exp(-gc))) * y[:, FF_BLK:]).astype(BF16)

    return pl.pallas_call(
        body, name=name, grid=(B, N_FF_BLK),
        in_specs=[pl.BlockSpec((seq, W2), lambda b, j: (b, j)), pl.BlockSpec((3, W2), lambda b, j: (0, j)),
                  pl.BlockSpec((1, W2), lambda b, j: (0, j))],
        out_specs=pl.BlockSpec((seq, FF_BLK), lambda b, j: (b, j)),
        out_shape=jax.ShapeDtypeStruct((T, D_FF), BF16),
        compiler_params=_cparams(("parallel", "parallel")),
    )(u, w, bias)


def _conv_bwd(u, dx2, wdn, w, bias, *, seq, name):
    T = u.shape[0]
    B = T // seq
    D = dx2.shape[1]
    W2 = 2 * FF_BLK
    nchunk = seq // CONV_ROWS

    def body(u_ref, dx_ref, wd_ref, w_ref, b_ref, du_ref, dw_ref, db_ref, dwd_ref, duc_s, dwd_s):
        wv = _conv_taps(w_ref)
        bv = b_ref[...]
        wd = wd_ref[...]
        zrow = jnp.zeros((1, W2), F32)
        dw0, dw1, dw2, dbs = zrow, zrow, zrow, zrow
        dwd = jnp.zeros((FF_BLK, D), F32)
        for c in range(nchunk):
            r0 = c * CONV_ROWS
            cur = u_ref[r0:r0 + CONV_ROWS, :]
            prev = u_ref[r0 - HALO:r0, :] if c > 0 else jnp.zeros((HALO, W2), F32)
            y, u1, u2 = _conv_rows(cur, prev, wv, bv)
            gc = y[:, :FF_BLK]
            vc = y[:, FF_BLK:]
            sg = 1.0 / (1.0 + jnp.exp(-gc))
            dxc = dx_ref[r0:r0 + CONV_ROWS, :].astype(BF16)
            dav = _dot_nt(dxc, wd)
            silu = gc * sg
            dwd = dwd + _dot_tn((silu * vc).astype(BF16), dxc)
            duc = jnp.concatenate([dav * vc * (sg * (1.0 + gc * (1.0 - sg))), dav * silu], axis=1)
            duc_s[r0:r0 + CONV_ROWS, :] = duc
            dw0 = dw0 + jnp.sum(duc * u2, axis=0, keepdims=True)
            dw1 = dw1 + jnp.sum(duc * u1, axis=0, keepdims=True)
            dw2 = dw2 + jnp.sum(duc * cur, axis=0, keepdims=True)
            dbs = dbs + jnp.sum(duc, axis=0, keepdims=True)
        duc_s[seq:seq + HALO, :] = jnp.zeros((HALO, W2), F32)
        n_ext = CONV_ROWS + HALO
        for c in range(nchunk):
            r0 = c * CONV_ROWS
            ext = duc_s[r0:r0 + n_ext, :]
            s1 = pltpu.roll(ext, n_ext - 1, 0)[:CONV_ROWS]
            s2 = pltpu.roll(ext, n_ext - 2, 0)[:CONV_ROWS]
            du_ref[r0:r0 + CONV_ROWS, :] = (wv[2] * ext[:CONV_ROWS] + wv[1] * s1 + wv[0] * s2).astype(BF16)

        b = pl.program_id(1)

        @pl.when(b == 0)
        def _():
            dw_ref[0:1, :] = dw0
            dw_ref[1:2, :] = dw1
            dw_ref[2:3, :] = dw2
            db_ref[...] = dbs
            dwd_s[...] = dwd

        @pl.when(b > 0)
        def _():
            dw_ref[0:1, :] += dw0
            dw_ref[1:2, :] += dw1
            dw_ref[2:3, :] += dw2
            db_ref[...] += dbs
            dwd_s[...] += dwd

        @pl.when(b == B - 1)
        def _():
            dwd_ref[...] = dwd_s[...].astype(BF16)

    return pl.pallas_call(
        body, name=name, grid=(N_FF_BLK, B),
        in_specs=[pl.BlockSpec((seq, W2), lambda j, b: (b, j)), pl.BlockSpec((seq, D), lambda j, b: (b, 0)),
                  pl.BlockSpec((FF_BLK, D), lambda j, b: (j, 0)),
                  pl.BlockSpec((3, W2), lambda j, b: (0, j)), pl.BlockSpec((1, W2), lambda j, b: (0, j))],
        out_specs=[pl.BlockSpec((seq, W2), lambda j, b: (b, j)), pl.BlockSpec((3, W2), lambda j, b: (0, j)),
                   pl.BlockSpec((1, W2), lambda j, b: (0, j)), pl.BlockSpec((FF_BLK, D), lambda j, b: (j, 0))],
        out_shape=[jax.ShapeDtypeStruct((T, 2 * D_FF), BF16), jax.ShapeDtypeStruct((3, 2 * D_FF), F32),
                   jax.ShapeDtypeStruct((1, 2 * D_FF), F32), jax.ShapeDtypeStruct((D_FF, D), BF16)],
        scratch_shapes=[pltpu.VMEM((seq + HALO, W2), F32), pltpu.VMEM((FF_BLK, D), F32)],
        compiler_params=_cparams(("parallel", "arbitrary")),
    )(u, dx2, wdn, w, bias)


def _place():
    return lax.axis_index("x"), lax.axis_index("y"), lax.axis_index("c")


def _other_chips(x, y):
    return [(1 - x, y), (x, 1 - y), (1 - x, 1 - y)]


def _all_gather(vs, *, name):
    n = len(vs)

    def body(*refs):
        v_refs, out_refs = refs[:n], refs[n:2 * n]
        send_sems, recv_sems, local_sems = refs[2 * n:]
        x, y, c = _place()
        me, sibling = (x, y, c), (x, y, 1 - c)
        chips = _other_chips(x, y)

        def slab(a, px, py, pc):
            return out_refs[a].at[4 * px + 2 * py + pc]

        def copy(a, k, block, to, src=None):
            return pltpu.make_async_remote_copy(
                src_ref=slab(a, *block) if src is None else src, dst_ref=slab(a, *block),
                send_sem=send_sems.at[7 * a + k], recv_sem=recv_sems.at[7 * a + k], device_id=to, device_id_type=MESH)

        mine = [pltpu.make_async_copy(v_refs[a], slab(a, *me), local_sems.at[a]) for a in range(n)]
        for cp in mine:
            cp.start()
        first = []
        for a in range(n):
            first.append(copy(a, 0, me, sibling, src=v_refs[a]))
            first += [copy(a, 1 + j, me, (*chip, c), src=v_refs[a]) for j, chip in enumerate(chips)]
        for cp in first:
            cp.start()
        passed = []
        for j, chip in enumerate(chips):
            for a in range(n):
                copy(a, 1 + j, (*chip, c), me).wait_recv()
                cp = copy(a, 4 + j, (*chip, c), sibling)
                cp.start()
                passed.append(cp)
        for a in range(n):
            copy(a, 0, sibling, me).wait_recv()
            for j, chip in enumerate(chips):
                copy(a, 4 + j, (*chip, 1 - c), me).wait_recv()
        for cp in first + passed:
            cp.wait_send()
        for cp in mine:
            cp.wait()

    return pl.pallas_call(
        body, name=name, in_specs=[ANY] * n, out_specs=[ANY] * n,
        out_shape=[jax.ShapeDtypeStruct((N_DEV,) + v.shape, v.dtype) for v in vs],
        scratch_shapes=[pltpu.SemaphoreType.DMA((7 * n,)), pltpu.SemaphoreType.DMA((7 * n,)), pltpu.SemaphoreType.DMA((n,))],
    )(*vs)


def _all_gather_async(vs, *, name, collective_id):
    n = len(vs)
    v_refs = [jax.new_ref(v, memory_space=pltpu.MemorySpace.HBM) for v in vs]
    out_refs = [jax.empty_ref(jax.ShapeDtypeStruct((N_DEV,) + v.shape, v.dtype), memory_space=pltpu.MemorySpace.HBM)
                for v in vs]

    @pl.kernel(mesh=plsc.ScalarSubcoreMesh(axis_name="seq", num_cores=1), name=name,
               scratch_types=(pltpu.SemaphoreType.DMA((7 * n,)), pltpu.SemaphoreType.DMA((7 * n,)),
                              pltpu.SemaphoreType.DMA((n,))),
               compiler_params=pltpu.CompilerParams(collective_id=collective_id))
    def launch(send_sems, recv_sems, local_sems):
        x, y, c = _place()
        me, sibling = (x, y, c), (x, y, 1 - c)
        chips = _other_chips(x, y)
        peers = [sibling] + [(*chip, c) for chip in chips]
        barrier = pltpu.get_barrier_semaphore()
        for peer in peers:
            pl.semaphore_signal(barrier, inc=1, device_id=peer, device_id_type=MESH)
        pl.semaphore_wait(barrier, len(peers))

        def slab(a, px, py, pc):
            return out_refs[a].at[4 * px + 2 * py + pc]

        def copy(a, k, block, to, src=None):
            return pltpu.make_async_remote_copy(
                src_ref=slab(a, *block) if src is None else src, dst_ref=slab(a, *block),
                send_sem=send_sems.at[7 * a + k], recv_sem=recv_sems.at[7 * a + k], device_id=to, device_id_type=MESH)

        mine = [pltpu.make_async_copy(v_refs[a], slab(a, *me), local_sems.at[a]) for a in range(n)]
        for cp in mine:
            cp.start()
        first = []
        for a in range(n):
            first.append(copy(a, 0, me, sibling, src=v_refs[a]))
            first += [copy(a, 1 + j, me, (*chip, c), src=v_refs[a]) for j, chip in enumerate(chips)]
        for cp in first:
            cp.start()
        passed = []
        for j, chip in enumerate(chips):
            for a in range(n):
                copy(a, 1 + j, (*chip, c), me).wait_recv()
                cp = copy(a, 4 + j, (*chip, c), sibling)
                cp.start()
                passed.append(cp)
        for a in range(n):
            copy(a, 0, sibling, me).wait_recv()
            for j, chip in enumerate(chips):
                copy(a, 4 + j, (*chip, 1 - c), me).wait_recv()
        for cp in first + passed:
            cp.wait_send()
        for cp in mine:
            cp.wait()

    launch()
    return [r[...] for r in out_refs]


def _handshake(peers):
    barrier = pltpu.get_barrier_semaphore()
    for peer in peers:
        pl.semaphore_signal(barrier, inc=1, device_id=peer, device_id_type=MESH)
    pl.semaphore_wait(barrier, len(peers))


def _hbm_refs(arrays, lead):
    src = [jax.new_ref(a, memory_space=pltpu.MemorySpace.HBM) for a in arrays]
    dst = [jax.empty_ref(jax.ShapeDtypeStruct((lead,) + a.shape[1:], a.dtype), memory_space=pltpu.MemorySpace.HBM)
           for a in arrays]
    return src, dst


def _rs_sibling_async(g8s, *, name, collective_id):
    n = len(g8s)
    g_refs, out_refs = _hbm_refs(g8s, 4)

    @pl.kernel(mesh=plsc.ScalarSubcoreMesh(axis_name="seq", num_cores=1), name=name,
               scratch_types=(pltpu.SemaphoreType.DMA((4 * n,)), pltpu.SemaphoreType.DMA((4 * n,))),
               compiler_params=pltpu.CompilerParams(collective_id=collective_id))
    def launch(send_sems, recv_sems):
        x, y, c = _place()
        _handshake([(x, y, 1 - c)])
        copies = [
            pltpu.make_async_remote_copy(
                src_ref=g_refs[a].at[2 * k + 1 - c], dst_ref=out_refs[a].at[k],
                send_sem=send_sems.at[4 * a + k], recv_sem=recv_sems.at[4 * a + k],
                device_id=(x, y, 1 - c), device_id_type=MESH)
            for a in range(n) for k in range(4)]
        for cp in copies:
            cp.start()
        for cp in copies:
            cp.wait()

    launch()
    return [r[...] for r in out_refs]


def _rs_chips_async(h4s, *, name, collective_id):
    n = len(h4s)
    h_refs, out_refs = _hbm_refs(h4s, 3)

    @pl.kernel(mesh=plsc.ScalarSubcoreMesh(axis_name="seq", num_cores=1), name=name,
               scratch_types=(pltpu.SemaphoreType.DMA((3 * n,)), pltpu.SemaphoreType.DMA((3 * n,))),
               compiler_params=pltpu.CompilerParams(collective_id=collective_id))
    def launch(send_sems, recv_sems):
        x, y, c = _place()
        chips = _other_chips(x, y)
        _handshake([(cx, cy, c) for cx, cy in chips])
        copies = [
            pltpu.make_async_remote_copy(
                src_ref=h_refs[a].at[2 * cx + cy], dst_ref=out_refs[a].at[j],
                send_sem=send_sems.at[3 * a + j], recv_sem=recv_sems.at[3 * a + j],
                device_id=(cx, cy, c), device_id_type=MESH)
            for a in range(n) for j, (cx, cy) in enumerate(chips)]
        for cp in copies:
            cp.start()
        for cp in copies:
            cp.wait()

    launch()
    return [r[...] for r in out_refs]


def _peer(x, y, c, k):
    return ((1 - x) if k & 4 else x, (1 - y) if k & 2 else y, (1 - c) if k & 1 else c)


def _rs_direct_async(g8s, *, name, collective_id):
    n = len(g8s)
    g_refs, out_refs = _hbm_refs(g8s, N_DEV - 1)

    @pl.kernel(mesh=plsc.ScalarSubcoreMesh(axis_name="seq", num_cores=1), name=name,
               scratch_types=(pltpu.SemaphoreType.DMA((7 * n,)), pltpu.SemaphoreType.DMA((7 * n,))),
               compiler_params=pltpu.CompilerParams(collective_id=collective_id))
    def launch(send_sems, recv_sems):
        x, y, c = _place()
        peers = [_peer(x, y, c, k) for k in range(1, N_DEV)]
        _handshake(peers)
        copies = [
            pltpu.make_async_remote_copy(
                src_ref=g_refs[a].at[4 * px + 2 * py + pc], dst_ref=out_refs[a].at[k],
                send_sem=send_sems.at[7 * a + k], recv_sem=recv_sems.at[7 * a + k],
                device_id=(px, py, pc), device_id_type=MESH)
            for a in range(n) for k, (px, py, pc) in enumerate(peers)]
        for cp in copies:
            cp.start()
        for cp in copies:
            cp.wait()

    launch()
    return [r[...] for r in out_refs]


def _row_tile(rows):
    if rows <= 512:
        return rows
    return next(t for t in (512, 384, 352, 256, 128) if rows % t == 0)


def _rs_chip_sum(g8, from_sibling, place_idx, *, name):
    _, R, C = g8.shape
    tr = _row_tile(R)

    def body(pi_ref, a_ref, b_ref, f_ref, h_ref):
        s = a_ref[...] + b_ref[...]
        h_ref[...] = s.astype(BF16)

        @pl.when(pl.program_id(1) == pi_ref[1])
        def _():
            f_ref[...] = s

    blk = pl.BlockSpec((None, tr, C), lambda r, k, pi_ref: (k, r, 0))
    return pl.pallas_call(
        body, name=name,
        grid_spec=pltpu.PrefetchScalarGridSpec(
            num_scalar_prefetch=1, grid=(R // tr, 4),
            in_specs=[pl.BlockSpec((None, tr, C), lambda r, k, pi_ref: (2 * k + pi_ref[0], r, 0)), blk],
            out_specs=[pl.BlockSpec((tr, C), lambda r, k, pi_ref: (r, 0)), blk]),
        out_shape=[jax.ShapeDtypeStruct((R, C), F32), jax.ShapeDtypeStruct((4, R, C), BF16)],
        compiler_params=_cparams(("parallel", "arbitrary")),
    )(place_idx, g8, from_sibling)


def _split_moves(segments, chunk):
    moves = []
    for dst, src, length in segments:
        while length > 0:
            dev, off = divmod(src, chunk)
            take = min(length, chunk - off)
            moves.append((dst, dev, off, take))
            dst, src, length = dst + take, src + take, length - take
    return moves


def _assemble(stacked, segments, zero_spans, out_cols, *, name):
    _, R, c = stacked.shape
    tr = _row_tile(R)
    moves = _split_moves(segments, c)

    def body(x_ref, o_ref):
        for dst, dev, off, take in moves:
            o_ref[:, dst:dst + take] = x_ref[dev, :, off:off + take]
        for a, b in zero_spans:
            o_ref[:, a:b] = jnp.zeros((tr, b - a), o_ref.dtype)

    return pl.pallas_call(
        body, name=name, grid=(R // tr,),
        in_specs=[pl.BlockSpec((N_DEV, tr, c), lambda i: (0, i, 0))],
        out_specs=pl.BlockSpec((tr, out_cols), lambda i: (i, 0)),
        out_shape=jax.ShapeDtypeStruct((R, out_cols), stacked.dtype),
        compiler_params=_cparams(("parallel",)),
    )(stacked)


def _disassemble(full, segments, chunk, *, name, out_dtype=F32):
    R = full.shape[0]
    tr = _row_tile(R)
    moves = _split_moves(segments, chunk)

    def body(x_ref, o_ref):
        seen = set()
        for dst, dev, off, take in moves:
            piece = x_ref[:, dst:dst + take]
            if (dev, off) in seen:
                piece = piece + o_ref[dev, :, off:off + take]
            seen.add((dev, off))
            o_ref[dev, :, off:off + take] = piece.astype(out_dtype)

    return pl.pallas_call(
        body, name=name, grid=(R // tr,),
        in_specs=[pl.BlockSpec((tr, full.shape[1]), lambda i: (i, 0))],
        out_specs=pl.BlockSpec((N_DEV, tr, chunk), lambda i: (0, i, 0)),
        out_shape=jax.ShapeDtypeStruct((N_DEV, R, chunk), out_dtype),
        compiler_params=_cparams(("parallel",)),
    )(full)


def _disassemble_rows(full_t, segments, chunk, *, name, out_dtype=F32):
    R = full_t.shape[1]
    tc = next(t for t in (2 * LANES, LANES) if R % t == 0)
    moves = _split_moves(segments, chunk)

    def body(x_ref, o_ref):
        seen = set()
        for dst, dev, off, take in moves:
            piece = x_ref[dst:dst + take, :]
            if (dev, off) in seen:
                piece = piece + o_ref[dev, off:off + take, :]
            seen.add((dev, off))
            o_ref[dev, off:off + take, :] = piece.astype(out_dtype)

    return pl.pallas_call(
        body, name=name, grid=(R // tc,),
        in_specs=[pl.BlockSpec((full_t.shape[0], tc), lambda i: (0, i))],
        out_specs=pl.BlockSpec((N_DEV, chunk, tc), lambda i: (0, 0, i)),
        out_shape=jax.ShapeDtypeStruct((N_DEV, chunk, R), out_dtype),
        compiler_params=_cparams(("parallel",)),
    )(full_t)


_O_CQ = 3 * SB_W
_O_CKV = _O_CQ + Q_LORA
_O_KR = _O_CKV + KV_LORA
SEG_W_IN = ((0, 0, 3 * SB_W), (P_CKV, _O_CKV, KV_LORA), (P_KRT, _O_KR, MLA_ROPE), (P_KRT + MLA_ROPE, _O_KR, MLA_ROPE),
            (P_CQ, _O_CQ, Q_LORA))
ZERO_W_IN = ((P_KRT + 2 * MLA_ROPE, P_CQ),)
SEG_W_UQ = tuple((MLA_NOPE * h, MLA_QK * h, MLA_NOPE) for h in range(MLA_HEADS)) + tuple(
    (MLA_W + LANES * (h // 2) + MLA_ROPE * (h % 2), MLA_QK * h + MLA_NOPE, MLA_ROPE) for h in range(MLA_HEADS))
ZERO_W_UQ = tuple((MLA_W + LANES * g + 2 * MLA_ROPE, MLA_W + LANES * (g + 1)) for g in range(MLA_HEADS // 2))
SEG_W_UKV = tuple((MLA_NOPE * h, (MLA_NOPE + MLA_V) * h, MLA_NOPE) for h in range(MLA_HEADS)) + tuple(
    (MLA_W + MLA_V * h, (MLA_NOPE + MLA_V) * h + MLA_NOPE, MLA_V) for h in range(MLA_HEADS))
SEG_W_UP = tuple((2 * FF_BLK * blk + FF_BLK * half, D_FF * half + FF_BLK * blk, FF_BLK)
                 for half in range(2) for blk in range(N_FF_BLK))


def _sum8(g, *, name):
    _, R, C = g.shape

    def body(g_ref, o_ref):
        acc = g_ref[0]
        for k in range(1, N_DEV):
            acc = acc + g_ref[k]
        o_ref[...] = acc

    return pl.pallas_call(
        body, name=name, out_shape=jax.ShapeDtypeStruct((R, C), F32),
    )(g)


def _adamw_math(w, gf, m, v):
    c1 = 1.0 / (1.0 - ADAM_B1 ** ADAM_STEP)
    c2 = 1.0 / (1.0 - ADAM_B2 ** ADAM_STEP)
    mn = ADAM_B1 * m + (1.0 - ADAM_B1) * gf
    vn = ADAM_B2 * v + (1.0 - ADAM_B2) * (gf * gf)
    return -ADAM_LR * ((mn * c1) / (jnp.sqrt(vn * c2) + ADAM_EPS) + ADAM_WD * w), mn, vn


def _adamw(w, g, m, v, *, name):
    R, C = w.shape
    tr = _row_tile(R)

    def body(w_ref, g_ref, m_ref, v_ref, d_ref, mo_ref, vo_ref):
        d_ref[...], mo_ref[...], vo_ref[...] = _adamw_math(w_ref[...], g_ref[...], m_ref[...], v_ref[...])

    blk = pl.BlockSpec((tr, C), lambda i: (i, 0))
    shp = jax.ShapeDtypeStruct((R, C), F32)
    return pl.pallas_call(
        body, name=name, grid=(R // tr,), in_specs=[blk] * 4, out_specs=[blk] * 3,
        out_shape=[shp, shp, shp], compiler_params=_cparams(("parallel",)),
    )(w, g, m, v)


def _adamw_rs8(g8, r7, me_idx, w, m, v, *, name):
    R, C = w.shape
    tr = _row_tile(R)

    def body(i_ref, f_ref, r_ref, w_ref, m_ref, v_ref, g_ref, d_ref, mo_ref, vo_ref):
        gf = f_ref[...].astype(F32)
        for k in range(N_DEV - 1):
            gf = gf + r_ref[k].astype(F32)
        g_ref[...] = gf
        d_ref[...], mo_ref[...], vo_ref[...] = _adamw_math(w_ref[...], gf, m_ref[...], v_ref[...])

    blk = pl.BlockSpec((tr, C), lambda i, i_ref: (i, 0))
    shp = jax.ShapeDtypeStruct((R, C), F32)
    return pl.pallas_call(
        body, name=name,
        grid_spec=pltpu.PrefetchScalarGridSpec(
            num_scalar_prefetch=1, grid=(R // tr,),
            in_specs=[pl.BlockSpec((None, tr, C), lambda i, i_ref: (i_ref[0], i, 0)),
                      pl.BlockSpec((N_DEV - 1, tr, C), lambda i, i_ref: (0, i, 0)), blk, blk, blk],
            out_specs=[blk] * 4),
        out_shape=[shp] * 4, compiler_params=_cparams(("parallel",)),
    )(me_idx, g8, r7, w, m, v)


def _adamw_rs(own, r3, w, m, v, *, name):
    R, C = w.shape
    tr = _row_tile(R)

    def body(f_ref, r_ref, w_ref, m_ref, v_ref, g_ref, d_ref, mo_ref, vo_ref):
        gf = ((f_ref[...] + r_ref[0].astype(F32)) + r_ref[1].astype(F32)) + r_ref[2].astype(F32)
        g_ref[...] = gf
        d_ref[...], mo_ref[...], vo_ref[...] = _adamw_math(w_ref[...], gf, m_ref[...], v_ref[...])

    blk = pl.BlockSpec((tr, C), lambda i: (i, 0))
    shp = jax.ShapeDtypeStruct((R, C), F32)
    return pl.pallas_call(
        body, name=name, grid=(R // tr,),
        in_specs=[blk, pl.BlockSpec((3, tr, C), lambda i: (0, i, 0)), blk, blk, blk], out_specs=[blk] * 4,
        out_shape=[shp] * 4, compiler_params=_cparams(("parallel",)),
    )(own, r3, w, m, v)


def _ff_interleave(a):
    lead = a.shape[:-1]
    return a.reshape(*lead, 2, N_FF_BLK, FF_BLK).swapaxes(-3, -2).reshape(*lead, 2 * D_FF)


def _ff_deinterleave(a):
    lead = a.shape[:-1]
    return a.reshape(*lead, N_FF_BLK, 2, FF_BLK).swapaxes(-3, -2).reshape(*lead, 2 * D_FF)


SMALL =(("g_mix", D_MODEL), ("g_cq", Q_LORA), ("g_ckv", KV_LORA), ("g_sb_out", SB_W), ("g_mla_out", MLA_W),
         ("g_ffn", D_MODEL), ("conv_b", 2 * D_FF), ("g_final", D_MODEL))
SMALL_ROWS = 88


SMALL_USED = sum(size for _, size in SMALL)


def _pack_small(d, tail=None):
    parts = [d[n].reshape(-1) for n, _ in SMALL] + ([] if tail is None else [tail])
    flat = jnp.concatenate(parts)
    flat = jnp.pad(flat, (0, SMALL_ROWS * LANES - flat.shape[0]))
    return flat.reshape(SMALL_ROWS, LANES)


def _unpack_small(a):
    flat = a.reshape(-1)
    out, off = {}, 0
    for n, size in SMALL:
        out[n] = flat[off:off + size]
        off += size
    return out


def kernel(x, positions, g_mix, w_in, g_cq, w_uq, g_ckv, w_ukv, g_sb_out, g_mla_out, w_out, g_ffn, w_up, conv_w, conv_b, w_down, g_final, loss_target, m_g_mix, m_w_in, m_g_cq, m_w_uq, m_g_ckv, m_w_ukv, m_g_sb_out, m_g_mla_out, m_w_out, m_g_ffn, m_w_up, m_conv_w, m_conv_b, m_w_down, m_g_final, v_g_mix, v_w_in, v_g_cq, v_w_uq, v_g_ckv, v_w_ukv, v_g_sb_out, v_g_mla_out, v_w_out, v_g_ffn, v_w_up, v_conv_w, v_conv_b, v_w_down, v_g_final):
    B, S, D = x.shape
    T = B * S
    xf = x.reshape(T, D)
    tgt = loss_target.reshape(T, D)
    pos = positions.reshape(T, 1)
    half = MLA_ROPE // 2
    inv_freq = 1.0 / (ROPE_BASE ** (jnp.arange(half, dtype=F32) * (2.0 / MLA_ROPE)))
    invf = jnp.tile(inv_freq, LANES // half).reshape(1, LANES)
    place_idx = jnp.stack([lax.axis_index("c"), 2 * lax.axis_index("x") + lax.axis_index("y")]).astype(jnp.int32)
    me_idx = (4 * lax.axis_index("x") + 2 * lax.axis_index("y") + lax.axis_index("c")).astype(jnp.int32).reshape(1)

    names = ("w_in", "w_uq", "w_ukv", "w_out", "w_up", "w_down", "conv_w")
    shard = {"w_in": w_in[0], "w_uq": w_uq[0], "w_ukv": w_ukv[0], "w_out": w_out[0], "w_up": w_up[0],
             "w_down": w_down[0], "conv_w": conv_w[0]}
    sent = {n: shard[n] if n == "conv_w" else shard[n].astype(BF16) for n in names}
    later = names[1:]
    w_in_all = _all_gather([sent["w_in"]], name="ag_w_in")[0]
    w_in_all, rest = lax.optimization_barrier((w_in_all, [sent[n] for n in later]))
    got = {"w_in": w_in_all}
    got.update(zip(later, _all_gather_async(rest, name="ag_weights_async", collective_id=0)))
    wi = _assemble(got["w_in"], SEG_W_IN, ZERO_W_IN, P_COLS, name="asm_w_in")
    wuq = _assemble(got["w_uq"], SEG_W_UQ, ZERO_W_UQ, 2 * MLA_W, name="asm_w_uq")
    wukv = _assemble(got["w_ukv"], SEG_W_UKV, (), 2 * MLA_W, name="asm_w_ukv")
    wup = _assemble(got["w_up"], SEG_W_UP, (), 2 * D_FF, name="asm_w_up")
    cwi = _assemble(got["conv_w"], SEG_W_UP, (), 2 * D_FF, name="asm_conv_w")
    wo = got["w_out"].reshape(D, D)
    wdn = got["w_down"].reshape(D_FF, D)
    cbi = _ff_interleave(conv_b)

    h, p = _rms_matmul_nn(xf, g_mix, wi, tm=512, name="proj_in")
    o_sb, ltot = _sb_fwd(p, seq=S, name="sb_fwd")
    cq, qm, krt = _proj_uq_rope(p, g_cq, wuq, pos, invf, tm=512, name="proj_uq")
    ckv, kvm = _rms_matmul_nn(p, g_ckv, wukv, tm=512, name="proj_ukv", col_block=P_CKV // KV_LORA, out_dtype=BF16)
    o_mla, lse = _mla_fwd(qm, kvm, krt, seq=S, name="mla_fwd")
    ocat, x1 = _heads_out(o_sb, o_mla, g_sb_out, g_mla_out, wo, xf, tm=512, name="proj_out")
    hf, u = _rms_matmul_nn(x1, g_ffn, wup, tm=256, name="ffn_up")
    a = _conv_fwd(u, cwi, cbi, seq=S, name="conv_fwd")
    dx2, dg_final, loss_row = _matmul_nn_loss(a, wdn, x1, g_final.reshape(1, D), tgt, tm=512, name="ffn_down_loss")

    du, dcw, dcb, dw_down = _conv_bwd(u, dx2, wdn, cwi, cbi, seq=S, name="conv_bwd")
    dw_up_t = _matmul_tn(du, hf, tm=D_FF, tn=1024, tk=1024, name="dw_up")
    dx1, dg_ffn = _matmul_nt_rms_bwd(du, wup, x1, g_ffn, tm=512, name="d_ffn_up", residual=dx2)
    dw_out = _matmul_tn(ocat, dx1, tm=1024, tn=1024, tk=1024, name="dw_out", out_dtype=BF16)
    do_sb, do_mla, dg_sb, dg_mla = _heads_out_bwd(dx1, wo, o_sb, o_mla, g_sb_out, g_mla_out, tm=512, name="d_proj_out")

    early = ("w_down", "w_up", "conv_w", "w_out")
    g8 = {"w_up": _disassemble_rows(dw_up_t, SEG_W_UP, shard["w_up"].shape[1], name="split_dw_up", out_dtype=BF16),
          "conv_w": _disassemble(dcw, SEG_W_UP, shard["conv_w"].shape[1], name="split_dconv_w", out_dtype=BF16),
          "w_out": dw_out.reshape((N_DEV,) + shard["w_out"].shape),
          "w_down": dw_down.reshape((N_DEV,) + shard["w_down"].shape)}
    r7 = dict(zip(early, _rs_direct_async([g8[n] for n in early], name="rs_direct_async", collective_id=1)))
    own, r3 = {}, {}

    dq_sb, dk_sb, dv_sb = _sb_bwd(p, ltot, do_sb, seq=S, name="sb_bwd")
    dqn, dqr, dkn, dvm, dkr = _mla_bwd(qm, kvm, krt, o_mla, lse, do_mla, seq=S, name="mla_bwd")
    dqm, dcq, dg_cq, dkr_u = _d_proj_uq_rope(dqn, dqr, dkr, wuq, p, g_cq, pos, invf, tm=512, name="d_proj_uq")
    dkvm, dckv, dg_ckv = _d_proj_cat([dkn, dvm], wukv, p, g_ckv, tm=512, name="d_proj_ukv",
                                     col_block=P_CKV // KV_LORA, out_dtype=BF16)
    dw_uq_t = _matmul_tn(dqm, cq, tm=2 * MLA_W, tn=Q_LORA, tk=1024, name="dw_uq")
    dw_ukv = _matmul_tn(ckv, dkvm, tm=KV_LORA, tn=1024, tk=1024, name="dw_ukv")
    dp, dx, dg_mix = _d_proj_cat([dq_sb, dk_sb, dv_sb, dckv, dkr_u, dcq], wi, xf, g_mix, tm=512, name="d_proj_in",
                                 residual=dx1)
    dw_in_t = _matmul_tn(dp, h, tm=P_COLS, tn=1024, tk=1024, name="dw_in")

    late = ("w_in", "w_uq", "w_ukv")
    g8.update({"w_in": _disassemble_rows(dw_in_t, SEG_W_IN, shard["w_in"].shape[1], name="split_dw_in"),
               "w_uq": _disassemble_rows(dw_uq_t, SEG_W_UQ, shard["w_uq"].shape[1], name="split_dw_uq"),
               "w_ukv": _disassemble(dw_ukv, SEG_W_UKV, shard["w_ukv"].shape[1], name="split_dw_ukv")})
    sib_l = _rs_sibling_async([g8[n] for n in late], name="rs_sibling_late", collective_id=3)

    params = {"w_in": (w_in, m_w_in, v_w_in), "w_uq": (w_uq, m_w_uq, v_w_uq), "w_ukv": (w_ukv, m_w_ukv, v_w_ukv),
              "w_out": (w_out, m_w_out, v_w_out), "w_up": (w_up, m_w_up, v_w_up), "conv_w": (conv_w, m_conv_w, v_conv_w),
              "w_down": (w_down, m_w_down, v_w_down)}
    grad, delta, new_m, new_v = {}, {}, {}, {}

    transposed = ("w_in", "w_uq", "w_up")

    def adamw_group(group):
        for n in group:
            flip = jnp.transpose if n in transposed else (lambda t: t)
            w_, m_, v_ = [flip(t[0]) for t in params[n]]
            if n in r7:
                res = _adamw_rs8(g8[n], r7[n], me_idx, w_, m_, v_, name="adamw_" + n)
            else:
                res = _adamw_rs(own[n], r3[n], w_, m_, v_, name="adamw_" + n)
            grad[n], delta[n], new_m[n], new_v[n] = [flip(r)[None] for r in res]

    adamw_group(("w_down", "w_out", "conv_w"))
    sib_l, grad["w_down"] = lax.optimization_barrier((sib_l, grad["w_down"]))
    sums_l = [_rs_chip_sum(g8[n], fs, place_idx, name="rs_chip_sum_" + n) for n, fs in zip(late, sib_l)]
    r3.update(zip(late, _rs_chips_async([h4 for _, h4 in sums_l], name="rs_chips_late", collective_id=4)))
    own.update({n: f for n, (f, _) in zip(late, sums_l)})
    small_part = {"g_mix": dg_mix, "g_cq": dg_cq, "g_ckv": dg_ckv, "g_sb_out": dg_sb, "g_mla_out": dg_mla,
                  "g_ffn": dg_ffn, "conv_b": _ff_deinterleave(dcb), "g_final": dg_final}
    small_all, = _all_gather_async([_pack_small(small_part, tail=loss_row[0, 0:1])], name="ag_small_async",
                                   collective_id=5)
    adamw_group(("w_up",))
    adamw_group(late)
    gsmall = _sum8(small_all, name="sum_small_grads")
    small_w = {"g_mix": g_mix, "g_cq": g_cq, "g_ckv": g_ckv, "g_sb_out": g_sb_out, "g_mla_out": g_mla_out,
               "g_ffn": g_ffn, "conv_b": conv_b, "g_final": g_final}
    small_m = {"g_mix": m_g_mix, "g_cq": m_g_cq, "g_ckv": m_g_ckv, "g_sb_out": m_g_sb_out, "g_mla_out": m_g_mla_out,
               "g_ffn": m_g_ffn, "conv_b": m_conv_b, "g_final": m_g_final}
    small_v = {"g_mix": v_g_mix, "g_cq": v_g_cq, "g_ckv": v_g_ckv, "g_sb_out": v_g_sb_out, "g_mla_out": v_g_mla_out,
               "g_ffn": v_g_ffn, "conv_b": v_conv_b, "g_final": v_g_final}
    ds_, ms_, vs_ = _adamw(_pack_small(small_w), gsmall, _pack_small(small_m), _pack_small(small_v), name="adamw_small")
    for src, dst in ((_unpack_small(gsmall), grad), (_unpack_small(ds_), delta), (_unpack_small(ms_), new_m), (_unpack_small(vs_), new_v)):
        for n, _ in SMALL:
            dst[n] = src[n].reshape(small_w[n].shape)

    loss = gsmall.reshape(-1)[SMALL_USED]
    order = ("g_mix", "w_in", "g_cq", "w_uq", "g_ckv", "w_ukv", "g_sb_out", "g_mla_out", "w_out", "g_ffn", "w_up",
             "conv_w", "conv_b", "w_down", "g_final")
    return (loss, dx.reshape(B, S, D), *[grad[n] for n in order], *[delta[n] for n in order],
            *[new_m[n] for n in order], *[new_v[n] for n in order])
```

```python
import jax
import jax.numpy as jnp
from jax import lax
from jax.experimental import pallas as pl
from jax.experimental.pallas import tpu as pltpu
from jax.experimental.pallas import tpu_sc as plsc

F32 = jnp.float32
BF16 = jnp.bfloat16

D_MODEL = 1024
SB_HEADS = 8
SB_HEAD_DIM = 64
MLA_HEADS = 8
MLA_NOPE = 64
MLA_ROPE = 32
MLA_V = 64
Q_LORA = 384
KV_LORA = 256
D_FF = 2816
ROPE_BASE = 10000.0
EPS = 1e-6
SB_W = SB_HEADS * SB_HEAD_DIM
MLA_W = MLA_HEADS * MLA_V
MLA_QK = MLA_NOPE + MLA_ROPE

ADAM_LR = 0.001
ADAM_B1 = 0.9
ADAM_B2 = 0.999
ADAM_EPS = 1e-08
ADAM_WD = 0.01
ADAM_STEP = 10

N_DEV = 8
LANES = 128
V7X_VMEM_LIMIT = 56 * 1024 * 1024
FF_BLK = 256
N_FF_BLK = D_FF // FF_BLK

P_Q, P_K, P_V = 0, SB_W, 2 * SB_W
P_CKV = 3 * SB_W
P_KRT = P_CKV + KV_LORA
P_CQ = P_KRT + LANES
P_COLS = P_CQ + Q_LORA

MESH = pl.DeviceIdType.MESH
ANY = pl.BlockSpec(memory_space=pl.ANY)


def _cparams(sem=None, vmem=V7X_VMEM_LIMIT):
    return pltpu.CompilerParams(dimension_semantics=sem, vmem_limit_bytes=vmem)


def _matmul_tn(a, b, *, tm, tn, tk, name, out_dtype=F32):
    K, M = a.shape
    N = b.shape[1]
    assert M % tm == 0 and N % tn == 0 and K % tk == 0, (name, a.shape, b.shape)
    n_k = K // tk
    narrow = out_dtype != F32

    def body(a_ref, b_ref, o_ref, *scratch):
        acc_ref = scratch[0] if narrow else o_ref
        k = pl.program_id(2)
        part = lax.dot_general(a_ref[...].astype(BF16), b_ref[...].astype(BF16), (((0,), (0,)), ((), ())),
                               preferred_element_type=F32)

        @pl.when(k == 0)
        def _():
            acc_ref[...] = part

        @pl.when(k > 0)
        def _():
            acc_ref[...] += part

        if narrow:
            @pl.when(k == n_k - 1)
            def _():
                o_ref[...] = acc_ref[...].astype(out_dtype)

    return pl.pallas_call(
        body, name=name, grid=(M // tm, N // tn, n_k),
        in_specs=[pl.BlockSpec((tk, tm), lambda i, j, k: (k, i)), pl.BlockSpec((tk, tn), lambda i, j, k: (k, j))],
        out_specs=pl.BlockSpec((tm, tn), lambda i, j, k: (i, j)),
        out_shape=jax.ShapeDtypeStruct((M, N), out_dtype),
        scratch_shapes=[pltpu.VMEM((tm, tn), F32)] if narrow else [],
        compiler_params=_cparams(("parallel", "parallel", "arbitrary")),
    )(a, b)


def _rms(xf, g):
    r = lax.rsqrt(jnp.mean(xf * xf, axis=1, keepdims=True) + EPS)
    return (xf * r) * g


def _rms_grad(dyf, xf, g):
    r = lax.rsqrt(jnp.mean(xf * xf, axis=1, keepdims=True) + EPS)
    xh = xf * r
    dyg = dyf * g
    dx = r * (dyg - xh * jnp.mean(dyg * xh, axis=1, keepdims=True))
    return dx, jnp.sum(dyf * xh, axis=0, keepdims=True)


def _accumulate(ref, part):
    @pl.when(pl.program_id(0) == 0)
    def _():
        ref[...] = part

    @pl.when(pl.program_id(0) > 0)
    def _():
        ref[...] += part


def _rms_matmul_nn(x, g, w, *, tm, name, col_block=0, out_dtype=F32, w_transposed=False):
    T = x.shape[0]
    C, N = w.shape[::-1] if w_transposed else w.shape
    assert T % tm == 0, (name, x.shape)
    contract = (((1,), (1,)), ((), ())) if w_transposed else (((1,), (0,)), ((), ()))

    def body(x_ref, g_ref, w_ref, h_ref, o_ref):
        hb = _rms(x_ref[...], g_ref[...]).astype(BF16)
        h_ref[...] = hb
        o_ref[...] = lax.dot_general(hb, w_ref[...], contract, preferred_element_type=F32).astype(out_dtype)

    return pl.pallas_call(
        body, name=name, grid=(T // tm,),
        in_specs=[pl.BlockSpec((tm, C), lambda i: (i, col_block)), pl.BlockSpec((1, C), lambda i: (0, 0)),
                  pl.BlockSpec(w.shape, lambda i: (0, 0))],
        out_specs=[pl.BlockSpec((tm, C), lambda i: (i, 0)), pl.BlockSpec((tm, N), lambda i: (i, 0))],
        out_shape=[jax.ShapeDtypeStruct((T, C), BF16), jax.ShapeDtypeStruct((T, N), out_dtype)],
        compiler_params=_cparams(("parallel",)),
    )(x, g, w)


def _matmul_nt_rms_bwd(a, b, x, g, *, tm, name, residual=None, col_block=0, out_dtype=F32):
    M, K = a.shape
    C = b.shape[0]
    assert M % tm == 0, (name, a.shape)
    in_specs = [pl.BlockSpec((tm, K), lambda i: (i, 0)), pl.BlockSpec((C, K), lambda i: (0, 0)),
                pl.BlockSpec((tm, C), lambda i: (i, col_block)), pl.BlockSpec((1, C), lambda i: (0, 0))]
    args = [a, b, x, g]
    if residual is not None:
        in_specs.append(pl.BlockSpec((tm, C), lambda i: (i, 0)))
        args.append(residual)

    def body(*refs):
        a_ref, b_ref, x_ref, g_ref = refs[:4]
        dx_ref, dg_ref = refs[-2:]
        dy = lax.dot_general(a_ref[...].astype(BF16), b_ref[...], (((1,), (1,)), ((), ())), preferred_element_type=F32)
        dx, part = _rms_grad(dy, x_ref[...], g_ref[...])
        if residual is not None:
            dx = dx + refs[4][...]
        dx_ref[...] = dx.astype(out_dtype)
        _accumulate(dg_ref, part)

    return pl.pallas_call(
        body, name=name, grid=(M // tm,), in_specs=in_specs,
        out_specs=[pl.BlockSpec((tm, C), lambda i: (i, 0)), pl.BlockSpec((1, C), lambda i: (0, 0))],
        out_shape=[jax.ShapeDtypeStruct((M, C), out_dtype), jax.ShapeDtypeStruct((1, C), F32)],
        compiler_params=_cparams(("arbitrary",)),
    )(*args)


def _matmul_nn_loss(a, w, x1, g, tgt, *, tm, name):
    M, K = a.shape
    C = w.shape[1]
    assert M % tm == 0, (name, a.shape)

    nsub = 4
    ts = tm // nsub

    def body(a_ref, w_ref, x_ref, g_ref, t_ref, dx_ref, dg_ref, loss_ref):
        gf = g_ref[...]
        wv = w_ref[...]
        rows = [slice(r * ts, (r + 1) * ts) for r in range(nsub)]
        xs = [x_ref[rw, :] + jnp.dot(a_ref[rw, :], wv, preferred_element_type=F32) for rw in rows]
        lpart, gpart = 0.0, 0.0
        for rw, xf in zip(rows, xs):
            err = _rms(xf, gf) - t_ref[rw, :]
            lpart = lpart + 0.5 * jnp.sum(jnp.mean(err * err, axis=1, keepdims=True), axis=0, keepdims=True)
            dx, gp = _rms_grad(err * (1.0 / C), xf, gf)
            dx_ref[rw, :] = dx
            gpart = gpart + gp
        _accumulate(dg_ref, gpart)
        _accumulate(loss_ref, jnp.broadcast_to(lpart, (1, LANES)))

    row = pl.BlockSpec((tm, C), lambda i: (i, 0))
    return pl.pallas_call(
        body, name=name, grid=(M // tm,),
        in_specs=[pl.BlockSpec((tm, K), lambda i: (i, 0)), pl.BlockSpec((K, C), lambda i: (0, 0)), row,
                  pl.BlockSpec((1, C), lambda i: (0, 0)), row],
        out_specs=[row, pl.BlockSpec((1, C), lambda i: (0, 0)), pl.BlockSpec((1, LANES), lambda i: (0, 0))],
        out_shape=[jax.ShapeDtypeStruct((M, C), F32), jax.ShapeDtypeStruct((1, C), F32),
                   jax.ShapeDtypeStruct((1, LANES), F32)],
        compiler_params=_cparams(("arbitrary",)),
    )(a, w, x1, g, tgt)


ATT_T = 256
ATT_PAIRS = 2
NEG_BIG = -1e30


def _lane_iota():
    return lax.broadcasted_iota(jnp.int32, (1, LANES), 1)


def _head_masks():
    first = _lane_iota() < SB_HEAD_DIM
    return first, jnp.logical_not(first)


def _pick(mask, x):
    return jnp.where(mask, x, jnp.zeros_like(x))


def _lane_value(t, lane):
    return jnp.sum(jnp.where(_lane_iota() == lane, t, 0.0), axis=1, keepdims=True)


def _split_hi_lo(x):
    hi = x.astype(BF16)
    lo = (x - hi.astype(F32)).astype(BF16)
    return jnp.concatenate([hi, lo], axis=1)


def _tri(n, kind):
    r = lax.broadcasted_iota(jnp.int32, (n, n), 0)
    c = lax.broadcasted_iota(jnp.int32, (n, n), 1)
    u = {"suffix_excl": r > c, "prefix_incl": r <= c, "prefix_excl": r < c}[kind].astype(BF16)
    return jnp.concatenate([u, u], axis=0)


def _dot_nt(a, b):
    return lax.dot_general(a, b, (((1,), (1,)), ((), ())), preferred_element_type=F32)


def _dot_tn(a, b):
    return lax.dot_general(a, b, (((0,), (0,)), ((), ())), preferred_element_type=F32)


def _dot(a, b):
    return jnp.dot(a, b, preferred_element_type=F32)


def _causal_mask(n, strict):
    r = lax.broadcasted_iota(jnp.int32, (n, n), 0)
    c = lax.broadcasted_iota(jnp.int32, (n, n), 1)
    return (c < r) if strict else (c <= r)


LOG2E = 1.4426950408889634


def _sb_logs(qh, kj, vis):
    z2 = _dot_nt(qh, kj) * LOG2E
    nk = jnp.maximum(z2, 0.0) + jnp.log2(1.0 + jnp.exp2(-jnp.abs(z2)))
    lb = z2 - nk
    if vis is not None:
        nk = jnp.where(vis, nk, 0.0)
    return lb, nk


def _sb_fwd(p, *, seq, name):
    T = p.shape[0]
    B = T // seq
    TQ = ATT_T
    nq = seq // TQ
    PP = ATT_PAIRS
    W = PP * LANES
    nstep = SB_W // W
    NH = 2 * PP

    def body(q_ref, k_ref, v_ref, o_ref, lt_ref, q_s, k_s, v_s):
        masks = _head_masks()
        q = q_ref[...] * (SB_HEAD_DIM ** -0.5)
        v = v_ref[...]
        k_s[...] = k_ref[...].astype(BF16)
        for h in range(NH):
            ps = slice((h // 2) * LANES, (h // 2 + 1) * LANES)
            hs = slice(h * LANES, (h + 1) * LANES)
            q_s[:, hs] = _pick(masks[h % 2], q[:, ps]).astype(BF16)
            v_s[:, hs] = _pick(masks[h % 2], v[:, ps]).astype(BF16)
        u_suf = _tri(TQ, "suffix_excl")
        vis = _causal_mask(TQ, True)

        def q_block(i, carry):
            q0 = pl.multiple_of(i * TQ, TQ)
            qs = [q_s[pl.ds(q0, TQ), h * LANES:(h + 1) * LANES] for h in range(NH)]

            def tile(k0, c, mask):
                rs, accs = list(c[:NH]), list(c[NH:])
                logs = [_sb_logs(qs[h], k_s[pl.ds(k0, TQ), (h // 2) * LANES:(h // 2 + 1) * LANES], mask) for h in range(NH)]
                sums = [_dot(_split_hi_lo(nk), u_suf) for _, nk in logs]
                for h in range(NH):
                    a = jnp.exp2(logs[h][0] - sums[h] - rs[h])
                    if mask is not None:
                        a = jnp.where(mask, a, 0.0)
                    accs[h // 2] = accs[h // 2] + _dot(a.astype(BF16), v_s[pl.ds(k0, TQ), h * LANES:(h + 1) * LANES])
                    rs[h] = rs[h] + jnp.sum(logs[h][1], axis=1, keepdims=True)
                return tuple(rs) + tuple(accs)

            zero = jnp.zeros((TQ, 1), F32)
            c = tile(q0, (zero,) * NH + (jnp.zeros((TQ, LANES), F32),) * PP, vis)

            def k_block(jj, c):
                return tile(pl.multiple_of((i - 1 - jj) * TQ, TQ), c, None)

            c = lax.fori_loop(0, i, k_block, c)
            for pr in range(PP):
                ps = slice(pr * LANES, (pr + 1) * LANES)
                o_ref[pl.ds(q0, TQ), ps] = c[NH + pr]
                lt_ref[pl.ds(q0, TQ), ps] = jnp.where(masks[0], c[2 * pr], c[2 * pr + 1])
            return carry

        lax.fori_loop(0, nq, q_block, 0)

    blk = lambda off: pl.BlockSpec((seq, W), lambda b, g: (b, off + g))
    out_blk = pl.BlockSpec((seq, W), lambda b, g: (b, g))
    return pl.pallas_call(
        body, name=name, grid=(B, nstep),
        in_specs=[blk(P_Q // W), blk(P_K // W), blk(P_V // W)],
        out_specs=[out_blk, out_blk],
        out_shape=[jax.ShapeDtypeStruct((T, SB_W), F32), jax.ShapeDtypeStruct((T, SB_W), F32)],
        scratch_shapes=[pltpu.VMEM((seq, NH * LANES), BF16), pltpu.VMEM((seq, W), BF16), pltpu.VMEM((seq, NH * LANES), BF16)],
        compiler_params=_cparams(("parallel", "parallel")),
    )(p, p, p)


def _sb_bwd(p, ltot, do, *, seq, name):
    T = p.shape[0]
    B = T // seq
    TQ = ATT_T
    nq = seq // TQ
    PP = ATT_PAIRS
    W = PP * LANES
    nstep = SB_W // W
    NH = 2 * PP
    scale = SB_HEAD_DIM ** -0.5

    def body(q_ref, k_ref, v_ref, lt_ref, do_ref, dq_ref, dk_ref, dv_ref, q_s, k_s, v_s, do_s, dk_s, dv_s):
        masks = _head_masks()
        q = q_ref[...] * scale
        dof = do_ref[...]
        k_s[...] = k_ref[...].astype(BF16)
        v_s[...] = v_ref[...].astype(BF16)
        for h in range(NH):
            ps = slice((h // 2) * LANES, (h // 2 + 1) * LANES)
            hs = slice(h * LANES, (h + 1) * LANES)
            q_s[:, hs] = _pick(masks[h % 2], q[:, ps]).astype(BF16)
            do_s[:, hs] = _pick(masks[h % 2], dof[:, ps]).astype(BF16)
        dk_s[...] = jnp.zeros_like(dk_s)
        dv_s[...] = jnp.zeros_like(dv_s)
        u_pin = _tri(TQ, "prefix_incl")
        u_pex = _tri(TQ, "prefix_excl")[:TQ]
        vis = _causal_mask(TQ, True)

        def q_block(i, carry):
            q0 = pl.multiple_of(i * TQ, TQ)
            qs = [q_s[pl.ds(q0, TQ), h * LANES:(h + 1) * LANES] for h in range(NH)]
            dos = [do_s[pl.ds(q0, TQ), h * LANES:(h + 1) * LANES] for h in range(NH)]
            lt = lt_ref[pl.ds(q0, TQ), :]
            lts = [_lane_value(lt[:, (h // 2) * LANES:(h // 2 + 1) * LANES], (h % 2) * SB_HEAD_DIM) for h in range(NH)]

            def tile(k0, c, mask):
                cs, gs, accs = list(c[:NH]), list(c[NH:2 * NH]), list(c[2 * NH:])
                kjs = [k_s[pl.ds(k0, TQ), pr * LANES:(pr + 1) * LANES] for pr in range(PP)]
                vjs = [v_s[pl.ds(k0, TQ), pr * LANES:(pr + 1) * LANES] for pr in range(PP)]
                logs = [_sb_logs(qs[h], kjs[h // 2], mask) for h in range(NH)]
                pins = [_dot(_split_hi_lo(nk), u_pin) for _, nk in logs]
                das = [_dot_nt(dos[h], vjs[h // 2]) for h in range(NH)]
                a_l, g_l = [], []
                for h in range(NH):
                    a = jnp.exp2(logs[h][0] - ((lts[h] - cs[h]) - pins[h]))
                    if mask is not None:
                        a = jnp.where(mask, a, 0.0)
                    a_l.append(a)
                    g_l.append(das[h] * a)
                pres = [_dot(g.astype(BF16), u_pex) for g in g_l]
                dz_l = []
                for h in range(NH):
                    dz = g_l[h] - jnp.exp2(logs[h][0]) * (g_l[h] + (pres[h] + gs[h]))
                    if mask is not None:
                        dz = jnp.where(mask, dz, 0.0)
                    dz_l.append(dz.astype(BF16))
                for h in range(NH):
                    accs[h] = accs[h] + _dot(dz_l[h], kjs[h // 2])
                for pr in range(PP):
                    ps = slice(pr * LANES, (pr + 1) * LANES)
                    ha, hb = 2 * pr, 2 * pr + 1
                    dk_s[pl.ds(k0, TQ), ps] += _dot_tn(dz_l[ha], qs[ha]) + _dot_tn(dz_l[hb], qs[hb])
                    dv_s[pl.ds(k0, TQ), ps] += _dot_tn(a_l[ha].astype(BF16), dos[ha]) + _dot_tn(a_l[hb].astype(BF16), dos[hb])
                for h in range(NH):
                    cs[h] = cs[h] + jnp.sum(logs[h][1], axis=1, keepdims=True)
                    gs[h] = gs[h] + jnp.sum(g_l[h], axis=1, keepdims=True)
                return tuple(cs) + tuple(gs) + tuple(accs)

            z1 = jnp.zeros((TQ, 1), F32)
            zl = jnp.zeros((TQ, LANES), F32)

            def k_block(j, c):
                return tile(pl.multiple_of(j * TQ, TQ), c, None)

            c = lax.fori_loop(0, i, k_block, (z1,) * (2 * NH) + (zl,) * NH)
            c = tile(q0, c, vis)
            for pr in range(PP):
                dq = jnp.where(masks[0], c[2 * NH + 2 * pr], c[2 * NH + 2 * pr + 1]) * scale
                dq_ref[pl.ds(q0, TQ), pr * LANES:(pr + 1) * LANES] = dq.astype(BF16)
            return carry

        lax.fori_loop(0, nq, q_block, 0)
        dk_ref[...] = dk_s[...].astype(BF16)
        dv_ref[...] = dv_s[...].astype(BF16)

    blk = lambda off: pl.BlockSpec((seq, W), lambda b, g: (b, off + g))
    out_blk = pl.BlockSpec((seq, W), lambda b, g: (b, g))
    return pl.pallas_call(
        body, name=name, grid=(B, nstep),
        in_specs=[blk(P_Q // W), blk(P_K // W), blk(P_V // W), out_blk, out_blk],
        out_specs=[out_blk, out_blk, out_blk],
        out_shape=[jax.ShapeDtypeStruct((T, SB_W), BF16) for _ in range(3)],
        scratch_shapes=[pltpu.VMEM((seq, NH * LANES), BF16), pltpu.VMEM((seq, W), BF16), pltpu.VMEM((seq, W), BF16),
                        pltpu.VMEM((seq, NH * LANES), BF16), pltpu.VMEM((seq, W), F32), pltpu.VMEM((seq, W), F32)],
        compiler_params=_cparams(("parallel", "parallel")),
    )(p, p, p, ltot, do)


def _mla_masks():
    lane = lax.broadcasted_iota(jnp.int32, (1, 2 * LANES), 1)
    ma = (lane < MLA_NOPE) | ((lane >= LANES) & (lane < LANES + MLA_ROPE))
    mb = ((lane >= MLA_NOPE) & (lane < LANES)) | ((lane >= LANES + MLA_ROPE) & (lane < LANES + 2 * MLA_ROPE))
    return ma, mb


def _mla_fwd(qm, kvm, krt, *, seq, name):
    T = qm.shape[0]
    B = T // seq
    TQ = ATT_T
    nq = seq // TQ
    PP = ATT_PAIRS
    W = PP * LANES
    nstep = MLA_W // W
    NH = 2 * PP
    CW = 2 * LANES
    scale = MLA_QK ** -0.5

    def body(qn_ref, qr_ref, kn_ref, v_ref, kr_ref, o_ref, lse_ref, q_s, kc_s, v_s):
        hm = _head_masks()
        mm = _mla_masks()
        v = v_ref[...]
        for pr in range(PP):
            ps = slice(pr * LANES, (pr + 1) * LANES)
            qc = jnp.concatenate([qn_ref[:, ps], qr_ref[:, ps]], axis=1)
            kc_s[:, pr * CW:(pr + 1) * CW] = jnp.concatenate([kn_ref[:, ps], kr_ref[...]], axis=1)
            for e in range(2):
                h = 2 * pr + e
                q_s[:, h * CW:(h + 1) * CW] = _pick(mm[e], qc)
                v_s[:, h * LANES:(h + 1) * LANES] = _pick(hm[e], v[:, ps])
        vis = _causal_mask(TQ, False)

        def q_block(i, carry):
            q0 = pl.multiple_of(i * TQ, TQ)
            qs = [q_s[pl.ds(q0, TQ), h * CW:(h + 1) * CW] for h in range(NH)]

            def tile(k0, c, mask):
                ms, ls, accs = list(c[:NH]), list(c[NH:2 * NH]), list(c[2 * NH:])
                ss = [_dot_nt(qs[h], kc_s[pl.ds(k0, TQ), (h // 2) * CW:(h // 2 + 1) * CW]) * scale for h in range(NH)]
                if mask is not None:
                    ss = [jnp.where(mask, s, NEG_BIG) for s in ss]
                m_new = [jnp.maximum(ms[h], jnp.max(ss[h], axis=1, keepdims=True)) for h in range(NH)]
                alphas = [jnp.exp(ms[h] - m_new[h]) for h in range(NH)]
                prs = [jnp.exp(ss[h] - m_new[h]) for h in range(NH)]
                outs = [_dot(prs[h].astype(BF16), v_s[pl.ds(k0, TQ), h * LANES:(h + 1) * LANES]) for h in range(NH)]
                ls = [alphas[h] * ls[h] + jnp.sum(prs[h], axis=1, keepdims=True) for h in range(NH)]
                for pr in range(PP):
                    accs[pr] = accs[pr] * jnp.where(hm[0], alphas[2 * pr], alphas[2 * pr + 1]) + outs[2 * pr] + outs[2 * pr + 1]
                return tuple(m_new) + tuple(ls) + tuple(accs)

            neg = jnp.full((TQ, 1), NEG_BIG, F32)
            z1 = jnp.zeros((TQ, 1), F32)

            def k_block(j, c):
                return tile(pl.multiple_of(j * TQ, TQ), c, None)

            c = lax.fori_loop(0, i, k_block, (neg,) * NH + (z1,) * NH + (jnp.zeros((TQ, LANES), F32),) * PP)
            c = tile(q0, c, vis)
            for pr in range(PP):
                ps = slice(pr * LANES, (pr + 1) * LANES)
                m_a, m_b, l_a, l_b = c[2 * pr], c[2 * pr + 1], c[NH + 2 * pr], c[NH + 2 * pr + 1]
                o_ref[pl.ds(q0, TQ), ps] = c[2 * NH + pr] / jnp.where(hm[0], l_a, l_b)
                lse_ref[pl.ds(q0, TQ), ps] = jnp.where(hm[0], m_a + jnp.log(l_a), m_b + jnp.log(l_b))
            return carry

        lax.fori_loop(0, nq, q_block, 0)

    blk = lambda off: pl.BlockSpec((seq, W), lambda b, g: (b, off + g))
    out_blk = pl.BlockSpec((seq, W), lambda b, g: (b, g))
    return pl.pallas_call(
        body, name=name, grid=(B, nstep),
        in_specs=[blk(0), blk(nstep), blk(0), blk(nstep), pl.BlockSpec((seq, LANES), lambda b, g: (b, 0))],
        out_specs=[out_blk, out_blk],
        out_shape=[jax.ShapeDtypeStruct((T, MLA_W), F32), jax.ShapeDtypeStruct((T, MLA_W), F32)],
        scratch_shapes=[pltpu.VMEM((seq, NH * CW), BF16), pltpu.VMEM((seq, PP * CW), BF16), pltpu.VMEM((seq, NH * LANES), BF16)],
        compiler_params=_cparams(("parallel", "parallel")),
    )(qm, qm, kvm, kvm, krt)


def _mla_bwd(qm, kvm, krt, o, lse, do, *, seq, name):
    T = qm.shape[0]
    B = T // seq
    TQ = ATT_T
    nq = seq // TQ
    PP = ATT_PAIRS
    W = PP * LANES
    nstep = MLA_W // W
    NH = 2 * PP
    CW = 2 * LANES
    scale = MLA_QK ** -0.5

    def body(qn_ref, qr_ref, kn_ref, v_ref, kr_ref, o_ref, lse_ref, do_ref,
             dqn_ref, dqr_ref, dkn_ref, dv_ref, dkr_ref, q_s, kc_s, do_s, dkc_s, dv_s):
        hm = _head_masks()
        mm = _mla_masks()
        dof = do_ref[...]
        for pr in range(PP):
            ps = slice(pr * LANES, (pr + 1) * LANES)
            qc = jnp.concatenate([qn_ref[:, ps], qr_ref[:, ps]], axis=1)
            kc_s[:, pr * CW:(pr + 1) * CW] = jnp.concatenate([kn_ref[:, ps], kr_ref[...]], axis=1)
            for e in range(2):
                h = 2 * pr + e
                q_s[:, h * CW:(h + 1) * CW] = _pick(mm[e], qc)
                do_s[:, h * LANES:(h + 1) * LANES] = _pick(hm[e], dof[:, ps]).astype(BF16)
        dkc_s[...] = jnp.zeros_like(dkc_s)
        dv_s[...] = jnp.zeros_like(dv_s)
        vis = _causal_mask(TQ, False)

        def q_block(i, carry):
            q0 = pl.multiple_of(i * TQ, TQ)
            qs = [q_s[pl.ds(q0, TQ), h * CW:(h + 1) * CW] for h in range(NH)]
            dos = [do_s[pl.ds(q0, TQ), h * LANES:(h + 1) * LANES] for h in range(NH)]
            lse_t = lse_ref[pl.ds(q0, TQ), :]
            dd = do_ref[pl.ds(q0, TQ), :] * o_ref[pl.ds(q0, TQ), :]
            lses, ds_ = [], []
            for h in range(NH):
                ps = slice((h // 2) * LANES, (h // 2 + 1) * LANES)
                lses.append(_lane_value(lse_t[:, ps], (h % 2) * MLA_V))
                ds_.append(jnp.sum(_pick(hm[h % 2], dd[:, ps]), axis=1, keepdims=True))

            def tile(k0, c, mask):
                accs = list(c)
                kcs = [kc_s[pl.ds(k0, TQ), pr * CW:(pr + 1) * CW] for pr in range(PP)]
                vjs = [v_ref[pl.ds(k0, TQ), pr * LANES:(pr + 1) * LANES] for pr in range(PP)]
                ss = [_dot_nt(qs[h], kcs[h // 2]) * scale for h in range(NH)]
                dps = [_dot_nt(dos[h], vjs[h // 2]) for h in range(NH)]
                p_l, ds_l = [], []
                for h in range(NH):
                    pr_ = jnp.exp(ss[h] - lses[h])
                    if mask is not None:
                        pr_ = jnp.where(mask, pr_, 0.0)
                    p_l.append(pr_.astype(BF16))
                    ds_l.append((pr_ * (dps[h] - ds_[h]) * scale).astype(BF16))
                for h in range(NH):
                    accs[h] = accs[h] + _dot(ds_l[h], kcs[h // 2])
                for pr in range(PP):
                    ha, hb = 2 * pr, 2 * pr + 1
                    dkc_s[pl.ds(k0, TQ), pr * CW:(pr + 1) * CW] += _dot_tn(ds_l[ha], qs[ha]) + _dot_tn(ds_l[hb], qs[hb])
                    dv_s[pl.ds(k0, TQ), pr * LANES:(pr + 1) * LANES] += _dot_tn(p_l[ha], dos[ha]) + _dot_tn(p_l[hb], dos[hb])
                return tuple(accs)

            zc = jnp.zeros((TQ, CW), F32)

            def k_block(j, c):
                return tile(pl.multiple_of(j * TQ, TQ), c, None)

            c = lax.fori_loop(0, i, k_block, (zc,) * NH)
            c = tile(q0, c, vis)
            for pr in range(PP):
                ps = slice(pr * LANES, (pr + 1) * LANES)
                dq = _pick(mm[0], c[2 * pr]) + _pick(mm[1], c[2 * pr + 1])
                dqn_ref[pl.ds(q0, TQ), ps] = dq[:, :LANES].astype(BF16)
                dqr_ref[pl.ds(q0, TQ), ps] = dq[:, LANES:]
            return carry

        lax.fori_loop(0, nq, q_block, 0)
        dkr = dkc_s[:, LANES:CW]
        for pr in range(PP):
            dkn_ref[:, pr * LANES:(pr + 1) * LANES] = dkc_s[:, pr * CW:pr * CW + LANES].astype(BF16)
            if pr > 0:
                dkr = dkr + dkc_s[:, pr * CW + LANES:(pr + 1) * CW]
        dv_ref[...] = dv_s[...].astype(BF16)
        g = pl.program_id(1)

        @pl.when(g == 0)
        def _():
            dkr_ref[...] = dkr

        @pl.when(g > 0)
        def _():
            dkr_ref[...] += dkr

    blk = lambda off: pl.BlockSpec((seq, W), lambda b, g: (b, off + g))
    out_blk = pl.BlockSpec((seq, W), lambda b, g: (b, g))
    one_blk = pl.BlockSpec((seq, LANES), lambda b, g: (b, 0))
    return pl.pallas_call(
        body, name=name, grid=(B, nstep),
        in_specs=[blk(0), blk(nstep), blk(0), blk(nstep), one_blk, out_blk, out_blk, out_blk],
        out_specs=[out_blk, out_blk, out_blk, out_blk, one_blk],
        out_shape=[jax.ShapeDtypeStruct((T, MLA_W), BF16), jax.ShapeDtypeStruct((T, MLA_W), F32),
                   jax.ShapeDtypeStruct((T, MLA_W), BF16), jax.ShapeDtypeStruct((T, MLA_W), BF16),
                   jax.ShapeDtypeStruct((T, LANES), F32)],
        scratch_shapes=[pltpu.VMEM((seq, NH * CW), BF16), pltpu.VMEM((seq, PP * CW), BF16), pltpu.VMEM((seq, NH * LANES), BF16),
                        pltpu.VMEM((seq, PP * CW), F32), pltpu.VMEM((seq, W), F32)],
        compiler_params=_cparams(("parallel", "arbitrary")),
    )(qm, qm, kvm, kvm, krt, o, lse, do)


def _rope_tables(pos_ref, invf_ref):
    ang = pos_ref[...].astype(F32) * invf_ref[...]
    first = (_lane_iota() % MLA_ROPE) < (MLA_ROPE // 2)
    return jnp.cos(ang), jnp.sin(ang), first


def _rope_apply(x, cos, sin, first):
    rot = jnp.where(first, -pltpu.roll(x, LANES - MLA_ROPE // 2, 1), pltpu.roll(x, MLA_ROPE // 2, 1))
    return x * cos + rot * sin


def _rope_apply_t(dy, cos, sin, first):
    dys = dy * sin
    rot_t = jnp.where(first, pltpu.roll(dys, LANES - MLA_ROPE // 2, 1), -pltpu.roll(dys, MLA_ROPE // 2, 1))
    return dy * cos + rot_t


def _proj_uq_rope(p, g, wuq, pos, invf, *, tm, name):
    T = p.shape[0]
    ntile = MLA_W // LANES

    def body(x_ref, kr_ref, g_ref, w_ref, pos_ref, invf_ref, cq_ref, qm_ref, krt_ref):
        cos, sin, first = _rope_tables(pos_ref, invf_ref)
        hb = _rms(x_ref[...], g_ref[...]).astype(BF16)
        cq_ref[...] = hb
        q = jnp.dot(hb, w_ref[...], preferred_element_type=F32)
        qm_ref[:, :MLA_W] = q[:, :MLA_W].astype(BF16)
        for t in range(ntile):
            sl = slice(MLA_W + t * LANES, MLA_W + (t + 1) * LANES)
            qm_ref[:, sl] = _rope_apply(q[:, sl], cos, sin, first).astype(BF16)
        krt_ref[...] = _rope_apply(kr_ref[...], cos, sin, first).astype(BF16)

    return pl.pallas_call(
        body, name=name, grid=(T // tm,),
        in_specs=[pl.BlockSpec((tm, Q_LORA), lambda i: (i, P_CQ // Q_LORA)), pl.BlockSpec((tm, LANES), lambda i: (i, P_KRT // LANES)),
                  pl.BlockSpec((1, Q_LORA), lambda i: (0, 0)), pl.BlockSpec((Q_LORA, 2 * MLA_W), lambda i: (0, 0)),
                  pl.BlockSpec((tm, 1), lambda i: (i, 0)), pl.BlockSpec((1, LANES), lambda i: (0, 0))],
        out_specs=[pl.BlockSpec((tm, Q_LORA), lambda i: (i, 0)), pl.BlockSpec((tm, 2 * MLA_W), lambda i: (i, 0)),
                   pl.BlockSpec((tm, LANES), lambda i: (i, 0))],
        out_shape=[jax.ShapeDtypeStruct((T, Q_LORA), BF16), jax.ShapeDtypeStruct((T, 2 * MLA_W), BF16),
                   jax.ShapeDtypeStruct((T, LANES), BF16)],
        compiler_params=_cparams(("parallel",)),
    )(p, p, g, wuq, pos, invf)


def _d_proj_uq_rope(dqn, dqr, dkr, wuq, p, g, pos, invf, *, tm, name):
    T = dqn.shape[0]
    ntile = MLA_W // LANES

    def body(dqn_ref, dqr_ref, dkr_ref, w_ref, x_ref, g_ref, pos_ref, invf_ref, dqm_ref, dx_ref, dg_ref, dkr_o_ref):
        cos, sin, first = _rope_tables(pos_ref, invf_ref)
        dqm_ref[:, :MLA_W] = dqn_ref[...]
        for t in range(ntile):
            sl = slice(t * LANES, (t + 1) * LANES)
            dqm_ref[:, MLA_W + t * LANES:MLA_W + (t + 1) * LANES] = _rope_apply_t(dqr_ref[:, sl], cos, sin, first).astype(BF16)
        dkr_o_ref[...] = _rope_apply_t(dkr_ref[...], cos, sin, first).astype(BF16)
        dy = lax.dot_general(dqm_ref[...], w_ref[...], (((1,), (1,)), ((), ())), preferred_element_type=F32)
        dx, part = _rms_grad(dy, x_ref[...], g_ref[...])
        dx_ref[...] = dx.astype(BF16)
        _accumulate(dg_ref, part)

    half = pl.BlockSpec((tm, MLA_W), lambda i: (i, 0))
    tile = pl.BlockSpec((tm, LANES), lambda i: (i, 0))
    return pl.pallas_call(
        body, name=name, grid=(T // tm,),
        in_specs=[half, half, tile, pl.BlockSpec((Q_LORA, 2 * MLA_W), lambda i: (0, 0)),
                  pl.BlockSpec((tm, Q_LORA), lambda i: (i, P_CQ // Q_LORA)), pl.BlockSpec((1, Q_LORA), lambda i: (0, 0)),
                  pl.BlockSpec((tm, 1), lambda i: (i, 0)), pl.BlockSpec((1, LANES), lambda i: (0, 0))],
        out_specs=[pl.BlockSpec((tm, 2 * MLA_W), lambda i: (i, 0)), pl.BlockSpec((tm, Q_LORA), lambda i: (i, 0)),
                   pl.BlockSpec((1, Q_LORA), lambda i: (0, 0)), tile],
        out_shape=[jax.ShapeDtypeStruct((T, 2 * MLA_W), BF16), jax.ShapeDtypeStruct((T, Q_LORA), BF16),
                   jax.ShapeDtypeStruct((1, Q_LORA), F32), jax.ShapeDtypeStruct((T, LANES), BF16)],
        compiler_params=_cparams(("arbitrary",)),
    )(dqn, dqr, dkr, wuq, p, g, pos, invf)


def _d_proj_cat(pieces, b, x, g, *, tm, name, residual=None, col_block=0, out_dtype=F32, b_transposed=False):
    M = pieces[0].shape[0]
    widths = [pc.shape[1] for pc in pieces]
    K = sum(widths)
    C = b.shape[1] if b_transposed else b.shape[0]
    n = len(pieces)
    contract = (((1,), (0,)), ((), ())) if b_transposed else (((1,), (1,)), ((), ()))
    in_specs = [pl.BlockSpec((tm, w), lambda i: (i, 0)) for w in widths]
    in_specs += [pl.BlockSpec(b.shape, lambda i: (0, 0)), pl.BlockSpec((tm, C), lambda i: (i, col_block)),
                 pl.BlockSpec((1, C), lambda i: (0, 0))]
    args = list(pieces) + [b, x, g]
    if residual is not None:
        in_specs.append(pl.BlockSpec((tm, C), lambda i: (i, 0)))
        args.append(residual)

    def body(*refs):
        b_ref, x_ref, g_ref = refs[n:n + 3]
        cat_ref, dx_ref, dg_ref = refs[-3:]
        off = 0
        for r, w in zip(refs[:n], widths):
            cat_ref[:, off:off + w] = r[...]
            off += w
        dy = lax.dot_general(cat_ref[...], b_ref[...], contract, preferred_element_type=F32)
        dx, part = _rms_grad(dy, x_ref[...], g_ref[...])
        if residual is not None:
            dx = dx + refs[n + 3][...]
        dx_ref[...] = dx.astype(out_dtype)
        _accumulate(dg_ref, part)

    return pl.pallas_call(
        body, name=name, grid=(M // tm,), in_specs=in_specs,
        out_specs=[pl.BlockSpec((tm, K), lambda i: (i, 0)), pl.BlockSpec((tm, C), lambda i: (i, 0)),
                   pl.BlockSpec((1, C), lambda i: (0, 0))],
        out_shape=[jax.ShapeDtypeStruct((M, K), BF16), jax.ShapeDtypeStruct((M, C), out_dtype),
                   jax.ShapeDtypeStruct((1, C), F32)],
        compiler_params=_cparams(("arbitrary",)),
    )(*args)


def _heads_out(xa, xb, ga, gb, w, resid, *, tm, name):
    T, C = xa.shape
    N = w.shape[1]

    def body(xa_ref, xb_ref, ga_ref, gb_ref, w_ref, r_ref, oc_ref, o_ref):
        oc_ref[:, :C] = _rms(xa_ref[...], ga_ref[...]).astype(BF16)
        oc_ref[:, C:] = _rms(xb_ref[...], gb_ref[...]).astype(BF16)
        o_ref[...] = r_ref[...] + jnp.dot(oc_ref[...], w_ref[...], preferred_element_type=F32)

    row = pl.BlockSpec((tm, C), lambda i: (i, 0))
    gsp = pl.BlockSpec((1, C), lambda i: (0, 0))
    full = pl.BlockSpec((tm, N), lambda i: (i, 0))
    return pl.pallas_call(
        body, name=name, grid=(T // tm,),
        in_specs=[row, row, gsp, gsp, pl.BlockSpec((2 * C, N), lambda i: (0, 0)), full],
        out_specs=[pl.BlockSpec((tm, 2 * C), lambda i: (i, 0)), full],
        out_shape=[jax.ShapeDtypeStruct((T, 2 * C), BF16), jax.ShapeDtypeStruct((T, N), F32)],
        compiler_params=_cparams(("parallel",)),
    )(xa, xb, ga, gb, w, resid)


def _heads_out_bwd(dout, w, xa, xb, ga, gb, *, tm, name):
    T, C = xa.shape
    N = w.shape[1]

    def body(d_ref, w_ref, xa_ref, xb_ref, ga_ref, gb_ref, dxa_ref, dxb_ref, dga_ref, dgb_ref):
        dy = lax.dot_general(d_ref[...].astype(BF16), w_ref[...], (((1,), (1,)), ((), ())), preferred_element_type=F32)
        dxa, pa = _rms_grad(dy[:, :C], xa_ref[...], ga_ref[...])
        dxb, pb = _rms_grad(dy[:, C:], xb_ref[...], gb_ref[...])
        dxa_ref[...] = dxa
        dxb_ref[...] = dxb
        _accumulate(dga_ref, pa)
        _accumulate(dgb_ref, pb)

    row = pl.BlockSpec((tm, C), lambda i: (i, 0))
    gsp = pl.BlockSpec((1, C), lambda i: (0, 0))
    return pl.pallas_call(
        body, name=name, grid=(T // tm,),
        in_specs=[pl.BlockSpec((tm, N), lambda i: (i, 0)), pl.BlockSpec((2 * C, N), lambda i: (0, 0)), row, row, gsp, gsp],
        out_specs=[row, row, gsp, gsp],
        out_shape=[jax.ShapeDtypeStruct((T, C), F32), jax.ShapeDtypeStruct((T, C), F32),
                   jax.ShapeDtypeStruct((1, C), F32), jax.ShapeDtypeStruct((1, C), F32)],
        compiler_params=_cparams(("arbitrary",)),
    )(dout, w, xa, xb, ga, gb)


CONV_ROWS = 256
HALO = 8


def _conv_taps(w_ref):
    return w_ref[0:1, :], w_ref[1:2, :], w_ref[2:3, :]


def _conv_rows(cur, prev, w, bias):
    ext = jnp.concatenate([prev, cur], axis=0)
    u1 = pltpu.roll(ext, 1, 0)[HALO:]
    u2 = pltpu.roll(ext, 2, 0)[HALO:]
    return w[2] * cur + w[1] * u1 + w[0] * u2 + bias, u1, u2


def _conv_fwd(u, w, bias, *, seq, name):
    T = u.shape[0]
    B = T // seq
    W2 = 2 * FF_BLK

    def body(u_ref, w_ref, b_ref, a_ref):
        wv = _conv_taps(w_ref)
        bv = b_ref[...]
        for c in range(seq // CONV_ROWS):
            r0 = c * CONV_ROWS
            cur = u_ref[r0:r0 + CONV_ROWS, :]
            prev = u_ref[r0 - HALO:r0, :] if c > 0 else jnp.zeros((HALO, W2), F32)
            y, _, _ = _conv_rows(cur, prev, wv, bv)
            gc = y[:, :FF_BLK]
            a_ref[r0:r0 + CONV_ROWS, :] = (gc * (1.0 / (1.0 + jnp.exp(-gc))) * y[:, FF_BLK:]).astype(BF16)

    return pl.pallas_call(
        body, name=name, grid=(B, N_FF_BLK),
        in_specs=[pl.BlockSpec((seq, W2), lambda b, j: (b, j)), pl.BlockSpec((3, W2), lambda b, j: (0, j)),
                  pl.BlockSpec((1, W2), lambda b, j: (0, j))],
        out_specs=pl.BlockSpec((seq, FF_BLK), lambda b, j: (b, j)),
        out_shape=jax.ShapeDtypeStruct((T, D_FF), BF16),
        compiler_params=_cparams(("parallel", "parallel")),
    )(u, w, bias)


def _conv_bwd(u, dx2, wdn, w, bias, *, seq, name):
    T = u.shape[0]
    B = T // seq
    D = dx2.shape[1]
    W2 = 2 * FF_BLK
    nchunk = seq // CONV_ROWS

    def body(u_ref, dx_ref, wd_ref, w_ref, b_ref, du_ref, dw_ref, db_ref, dwd_ref, duc_s, dwd_s):
        wv = _conv_taps(w_ref)
        bv = b_ref[...]
        wd = wd_ref[...]
        zrow = jnp.zeros((1, W2), F32)
        dw0, dw1, dw2, dbs = zrow, zrow, zrow, zrow
        dwd = jnp.zeros((FF_BLK, D), F32)
        for c in range(nchunk):
            r0 = c * CONV_ROWS
            cur = u_ref[r0:r0 + CONV_ROWS, :]
            prev = u_ref[r0 - HALO:r0, :] if c > 0 else jnp.zeros((HALO, W2), F32)
            y, u1, u2 = _conv_rows(cur, prev, wv, bv)
            gc = y[:, :FF_BLK]
            vc = y[:, FF_BLK:]
            sg = 1.0 / (1.0 + jnp.exp(-gc))
            dxc = dx_ref[r0:r0 + CONV_ROWS, :].astype(BF16)
            dav = _dot_nt(dxc, wd)
            silu = gc * sg
            dwd = dwd + _dot_tn((silu * vc).astype(BF16), dxc)
            duc = jnp.concatenate([dav * vc * (sg * (1.0 + gc * (1.0 - sg))), dav * silu], axis=1)
            duc_s[r0:r0 + CONV_ROWS, :] = duc
            dw0 = dw0 + jnp.sum(duc * u2, axis=0, keepdims=True)
            dw1 = dw1 + jnp.sum(duc * u1, axis=0, keepdims=True)
            dw2 = dw2 + jnp.sum(duc * cur, axis=0, keepdims=True)
            dbs = dbs + jnp.sum(duc, axis=0, keepdims=True)
        duc_s[seq:seq + HALO, :] = jnp.zeros((HALO, W2), F32)
        n_ext = CONV_ROWS + HALO
        for c in range(nchunk):
            r0 = c * CONV_ROWS
            ext = duc_s[r0:r0 + n_ext, :]
            s1 = pltpu.roll(ext, n_ext - 1, 0)[:CONV_ROWS]
            s2 = pltpu.roll(ext, n_ext - 2, 0)[:CONV_ROWS]
            du_ref[r0:r0 + CONV_ROWS, :] = (wv[2] * ext[:CONV_ROWS] + wv[1] * s1 + wv[0] * s2).astype(BF16)

        b = pl.program_id(1)

        @pl.when(b == 0)
        def _():
            dw_ref[0:1, :] = dw0
            dw_ref[1:2, :] = dw1
            dw_ref[2:3, :] = dw2
            db_ref[...] = dbs
            dwd_s[...] = dwd

        @pl.when(b > 0)
        def _():
            dw_ref[0:1, :] += dw0
            dw_ref[1:2, :] += dw1
            dw_ref[2:3, :] += dw2
            db_ref[...] += dbs
            dwd_s[...] += dwd

        @pl.when(b == B - 1)
        def _():
            dwd_ref[...] = dwd_s[...].astype(BF16)

    return pl.pallas_call(
        body, name=name, grid=(N_FF_BLK, B),
        in_specs=[pl.BlockSpec((seq, W2), lambda j, b: (b, j)), pl.BlockSpec((seq, D), lambda j, b: (b, 0)),
                  pl.BlockSpec((FF_BLK, D), lambda j, b: (j, 0)),
                  pl.BlockSpec((3, W2), lambda j, b: (0, j)), pl.BlockSpec((1, W2), lambda j, b: (0, j))],
        out_specs=[pl.BlockSpec((seq, W2), lambda j, b: (b, j)), pl.BlockSpec((3, W2), lambda j, b: (0, j)),
                   pl.BlockSpec((1, W2), lambda j, b: (0, j)), pl.BlockSpec((FF_BLK, D), lambda j, b: (j, 0))],
        out_shape=[jax.ShapeDtypeStruct((T, 2 * D_FF), BF16), jax.ShapeDtypeStruct((3, 2 * D_FF), F32),
                   jax.ShapeDtypeStruct((1, 2 * D_FF), F32), jax.ShapeDtypeStruct((D_FF, D), BF16)],
        scratch_shapes=[pltpu.VMEM((seq + HALO, W2), F32), pltpu.VMEM((FF_BLK, D), F32)],
        compiler_params=_cparams(("parallel", "arbitrary")),
    )(u, dx2, wdn, w, bias)


def _place():
    return lax.axis_index("x"), lax.axis_index("y"), lax.axis_index("c")


def _other_chips(x, y):
    return [(1 - x, y), (x, 1 - y), (1 - x, 1 - y)]


def _all_gather(vs, *, name):
    n = len(vs)

    def body(*refs):
        v_refs, out_refs = refs[:n], refs[n:2 * n]
        send_sems, recv_sems, local_sems = refs[2 * n:]
        x, y, c = _place()
        me, sibling = (x, y, c), (x, y, 1 - c)
        chips = _other_chips(x, y)

        def slab(a, px, py, pc):
            return out_refs[a].at[4 * px + 2 * py + pc]

        def copy(a, k, block, to, src=None):
            return pltpu.make_async_remote_copy(
                src_ref=slab(a, *block) if src is None else src, dst_ref=slab(a, *block),
                send_sem=send_sems.at[7 * a + k], recv_sem=recv_sems.at[7 * a + k], device_id=to, device_id_type=MESH)

        mine = [pltpu.make_async_copy(v_refs[a], slab(a, *me), local_sems.at[a]) for a in range(n)]
        for cp in mine:
            cp.start()
        first = []
        for a in range(n):
            first.append(copy(a, 0, me, sibling, src=v_refs[a]))
            first += [copy(a, 1 + j, me, (*chip, c), src=v_refs[a]) for j, chip in enumerate(chips)]
        for cp in first:
            cp.start()
        passed = []
        for j, chip in enumerate(chips):
            for a in range(n):
                copy(a, 1 + j, (*chip, c), me).wait_recv()
                cp = copy(a, 4 + j, (*chip, c), sibling)
                cp.start()
                passed.append(cp)
        for a in range(n):
            copy(a, 0, sibling, me).wait_recv()
            for j, chip in enumerate(chips):
                copy(a, 4 + j, (*chip, 1 - c), me).wait_recv()
        for cp in first + passed:
            cp.wait_send()
        for cp in mine:
            cp.wait()

    return pl.pallas_call(
        body, name=name, in_specs=[ANY] * n, out_specs=[ANY] * n,
        out_shape=[jax.ShapeDtypeStruct((N_DEV,) + v.shape, v.dtype) for v in vs],
        scratch_shapes=[pltpu.SemaphoreType.DMA((7 * n,)), pltpu.SemaphoreType.DMA((7 * n,)), pltpu.SemaphoreType.DMA((n,))],
    )(*vs)


def _all_gather_async(vs, *, name, collective_id):
    n = len(vs)
    v_refs = [jax.new_ref(v, memory_space=pltpu.MemorySpace.HBM) for v in vs]
    out_refs = [jax.empty_ref(jax.ShapeDtypeStruct((N_DEV,) + v.shape, v.dtype), memory_space=pltpu.MemorySpace.HBM)
                for v in vs]

    @pl.kernel(mesh=plsc.ScalarSubcoreMesh(axis_name="seq", num_cores=1), name=name,
               scratch_types=(pltpu.SemaphoreType.DMA((7 * n,)), pltpu.SemaphoreType.DMA((7 * n,)),
                              pltpu.SemaphoreType.DMA((n,))),
               compiler_params=pltpu.CompilerParams(collective_id=collective_id))
    def launch(send_sems, recv_sems, local_sems):
        x, y, c = _place()
        me, sibling = (x, y, c), (x, y, 1 - c)
        chips = _other_chips(x, y)
        peers = [sibling] + [(*chip, c) for chip in chips]
        barrier = pltpu.get_barrier_semaphore()
        for peer in peers:
            pl.semaphore_signal(barrier, inc=1, device_id=peer, device_id_type=MESH)
        pl.semaphore_wait(barrier, len(peers))

        def slab(a, px, py, pc):
            return out_refs[a].at[4 * px + 2 * py + pc]

        def copy(a, k, block, to, src=None):
            return pltpu.make_async_remote_copy(
                src_ref=slab(a, *block) if src is None else src, dst_ref=slab(a, *block),
                send_sem=send_sems.at[7 * a + k], recv_sem=recv_sems.at[7 * a + k], device_id=to, device_id_type=MESH)

        mine = [pltpu.make_async_copy(v_refs[a], slab(a, *me), local_sems.at[a]) for a in range(n)]
        for cp in mine:
            cp.start()
        first = []
        for a in range(n):
            first.append(copy(a, 0, me, sibling, src=v_refs[a]))
            first += [copy(a, 1 + j, me, (*chip, c), src=v_refs[a]) for j, chip in enumerate(chips)]
        for cp in first:
            cp.start()
        passed = []
        for j, chip in enumerate(chips):
            for a in range(n):
                copy(a, 1 + j, (*chip, c), me).wait_recv()
                cp = copy(a, 4 + j, (*chip, c), sibling)
                cp.start()
                passed.append(cp)
        for a in range(n):
            copy(a, 0, sibling, me).wait_recv()
            for j, chip in enumerate(chips):
                copy(a, 4 + j, (*chip, 1 - c), me).wait_recv()
        for cp in first + passed:
            cp.wait_send()
        for cp in mine:
            cp.wait()

    launch()
    return [r[...] for r in out_refs]


def _handshake(peers):
    barrier = pltpu.get_barrier_semaphore()
    for peer in peers:
        pl.semaphore_signal(barrier, inc=1, device_id=peer, device_id_type=MESH)
    pl.semaphore_wait(barrier, len(peers))


def _hbm_refs(arrays, lead):
    src = [jax.new_ref(a, memory_space=pltpu.MemorySpace.HBM) for a in arrays]
    dst = [jax.empty_ref(jax.ShapeDtypeStruct((lead,) + a.shape[1:], a.dtype), memory_space=pltpu.MemorySpace.HBM)
           for a in arrays]
    return src, dst


def _rs_sibling_async(g8s, *, name, collective_id):
    n = len(g8s)
    g_refs, out_refs = _hbm_refs(g8s, 4)

    @pl.kernel(mesh=plsc.ScalarSubcoreMesh(axis_name="seq", num_cores=1), name=name,
               scratch_types=(pltpu.SemaphoreType.DMA((4 * n,)), pltpu.SemaphoreType.DMA((4 * n,))),
               compiler_params=pltpu.CompilerParams(collective_id=collective_id))
    def launch(send_sems, recv_sems):
        x, y, c = _place()
        _handshake([(x, y, 1 - c)])
        copies = [
            pltpu.make_async_remote_copy(
                src_ref=g_refs[a].at[2 * k + 1 - c], dst_ref=out_refs[a].at[k],
                send_sem=send_sems.at[4 * a + k], recv_sem=recv_sems.at[4 * a + k],
                device_id=(x, y, 1 - c), device_id_type=MESH)
            for a in range(n) for k in range(4)]
        for cp in copies:
            cp.start()
        for cp in copies:
            cp.wait()

    launch()
    return [r[...] for r in out_refs]


def _rs_chips_async(h4s, *, name, collective_id):
    n = len(h4s)
    h_refs, out_refs = _hbm_refs(h4s, 3)

    @pl.kernel(mesh=plsc.ScalarSubcoreMesh(axis_name="seq", num_cores=1), name=name,
               scratch_types=(pltpu.SemaphoreType.DMA((3 * n,)), pltpu.SemaphoreType.DMA((3 * n,))),
               compiler_params=pltpu.CompilerParams(collective_id=collective_id))
    def launch(send_sems, recv_sems):
        x, y, c = _place()
        chips = _other_chips(x, y)
        _handshake([(cx, cy, c) for cx, cy in chips])
        copies = [
            pltpu.make_async_remote_copy(
                src_ref=h_refs[a].at[2 * cx + cy], dst_ref=out_refs[a].at[j],
                send_sem=send_sems.at[3 * a + j], recv_sem=recv_sems.at[3 * a + j],
                device_id=(cx, cy, c), device_id_type=MESH)
            for a in range(n) for j, (cx, cy) in enumerate(chips)]
        for cp in copies:
            cp.start()
        for cp in copies:
            cp.wait()

    launch()
    return [r[...] for r in out_refs]


def _peer(x, y, c, k):
    return ((1 - x) if k & 4 else x, (1 - y) if k & 2 else y, (1 - c) if k & 1 else c)


def _rs_direct_async(g8s, *, name, collective_id):
    n = len(g8s)
    g_refs, out_refs = _hbm_refs(g8s, N_DEV - 1)

    @pl.kernel(mesh=plsc.ScalarSubcoreMesh(axis_name="seq", num_cores=1), name=name,
               scratch_types=(pltpu.SemaphoreType.DMA((7 * n,)), pltpu.SemaphoreType.DMA((7 * n,))),
               compiler_params=pltpu.CompilerParams(collective_id=collective_id))
    def launch(send_sems, recv_sems):
        x, y, c = _place()
        peers = [_peer(x, y, c, k) for k in range(1, N_DEV)]
        _handshake(peers)
        copies = [
            pltpu.make_async_remote_copy(
                src_ref=g_refs[a].at[4 * px + 2 * py + pc], dst_ref=out_refs[a].at[k],
                send_sem=send_sems.at[7 * a + k], recv_sem=recv_sems.at[7 * a + k],
                device_id=(px, py, pc), device_id_type=MESH)
            for a in range(n) for k, (px, py, pc) in enumerate(peers)]
        for cp in copies:
            cp.start()
        for cp in copies:
            cp.wait()

    launch()
    return [r[...] for r in out_refs]


def _row_tile(rows):
    if rows <= 512:
        return rows
    return next(t for t in (512, 384, 352, 256, 128) if rows % t == 0)


def _rs_chip_sum(g8, from_sibling, place_idx, *, name):
    _, R, C = g8.shape
    tr = _row_tile(R)

    def body(pi_ref, a_ref, b_ref, f_ref, h_ref):
        s = a_ref[...] + b_ref[...]
        h_ref[...] = s.astype(BF16)

        @pl.when(pl.program_id(1) == pi_ref[1])
        def _():
            f_ref[...] = s

    blk = pl.BlockSpec((None, tr, C), lambda r, k, pi_ref: (k, r, 0))
    return pl.pallas_call(
        body, name=name,
        grid_spec=pltpu.PrefetchScalarGridSpec(
            num_scalar_prefetch=1, grid=(R // tr, 4),
            in_specs=[pl.BlockSpec((None, tr, C), lambda r, k, pi_ref: (2 * k + pi_ref[0], r, 0)), blk],
            out_specs=[pl.BlockSpec((tr, C), lambda r, k, pi_ref: (r, 0)), blk]),
        out_shape=[jax.ShapeDtypeStruct((R, C), F32), jax.ShapeDtypeStruct((4, R, C), BF16)],
        compiler_params=_cparams(("parallel", "arbitrary")),
    )(place_idx, g8, from_sibling)


def _split_moves(segments, chunk):
    moves = []
    for dst, src, length in segments:
        while length > 0:
            dev, off = divmod(src, chunk)
            take = min(length, chunk - off)
            moves.append((dst, dev, off, take))
            dst, src, length = dst + take, src + take, length - take
    return moves


def _assemble(stacked, segments, zero_spans, out_cols, *, name):
    _, R, c = stacked.shape
    tr = _row_tile(R)
    moves = _split_moves(segments, c)

    def body(x_ref, o_ref):
        for dst, dev, off, take in moves:
            o_ref[:, dst:dst + take] = x_ref[dev, :, off:off + take]
        for a, b in zero_spans:
            o_ref[:, a:b] = jnp.zeros((tr, b - a), o_ref.dtype)

    return pl.pallas_call(
        body, name=name, grid=(R // tr,),
        in_specs=[pl.BlockSpec((N_DEV, tr, c), lambda i: (0, i, 0))],
        out_specs=pl.BlockSpec((tr, out_cols), lambda i: (i, 0)),
        out_shape=jax.ShapeDtypeStruct((R, out_cols), stacked.dtype),
        compiler_params=_cparams(("parallel",)),
    )(stacked)


def _disassemble(full, segments, chunk, *, name, out_dtype=F32):
    R = full.shape[0]
    tr = _row_tile(R)
    moves = _split_moves(segments, chunk)

    def body(x_ref, o_ref):
        seen = set()
        for dst, dev, off, take in moves:
            piece = x_ref[:, dst:dst + take]
            if (dev, off) in seen:
                piece = piece + o_ref[dev, :, off:off + take]
            seen.add((dev, off))
            o_ref[dev, :, off:off + take] = piece.astype(out_dtype)

    return pl.pallas_call(
        body, name=name, grid=(R // tr,),
        in_specs=[pl.BlockSpec((tr, full.shape[1]), lambda i: (i, 0))],
        out_specs=pl.BlockSpec((N_DEV, tr, chunk), lambda i: (0, i, 0)),
        out_shape=jax.ShapeDtypeStruct((N_DEV, R, chunk), out_dtype),
        compiler_params=_cparams(("parallel",)),
    )(full)


def _assemble_rows(stacked, segments, zero_spans, out_rows, *, name):
    _, c, R = stacked.shape
    tc = next(t for t in (2 * LANES, LANES) if R % t == 0)
    moves = _split_moves(segments, c)

    def body(x_ref, o_ref):
        for dst, dev, off, take in moves:
            o_ref[dst:dst + take, :] = x_ref[dev, off:off + take, :]
        for a, b in zero_spans:
            o_ref[a:b, :] = jnp.zeros((b - a, tc), o_ref.dtype)

    return pl.pallas_call(
        body, name=name, grid=(R // tc,),
        in_specs=[pl.BlockSpec((N_DEV, c, tc), lambda i: (0, 0, i))],
        out_specs=pl.BlockSpec((out_rows, tc), lambda i: (0, i)),
        out_shape=jax.ShapeDtypeStruct((out_rows, R), stacked.dtype),
        compiler_params=_cparams(("parallel",)),
    )(stacked)


def _disassemble_rows(full_t, segments, chunk, *, name, out_dtype=F32):
    R = full_t.shape[1]
    tc = next(t for t in (2 * LANES, LANES) if R % t == 0)
    moves = _split_moves(segments, chunk)

    def body(x_ref, o_ref):
        seen = set()
        for dst, dev, off, take in moves:
            piece = x_ref[dst:dst + take, :]
            if (dev, off) in seen:
                piece = piece + o_ref[dev, off:off + take, :]
            seen.add((dev, off))
            o_ref[dev, off:off + take, :] = piece.astype(out_dtype)

    return pl.pallas_call(
        body, name=name, grid=(R // tc,),
        in_specs=[pl.BlockSpec((full_t.shape[0], tc), lambda i: (0, i))],
        out_specs=pl.BlockSpec((N_DEV, chunk, tc), lambda i: (0, 0, i)),
        out_shape=jax.ShapeDtypeStruct((N_DEV, chunk, R), out_dtype),
        compiler_params=_cparams(("parallel",)),
    )(full_t)


_O_CQ = 3 * SB_W
_O_CKV = _O_CQ + Q_LORA
_O_KR = _O_CKV + KV_LORA
SEG_W_IN = ((0, 0, 3 * SB_W), (P_CKV, _O_CKV, KV_LORA), (P_KRT, _O_KR, MLA_ROPE), (P_KRT + MLA_ROPE, _O_KR, MLA_ROPE),
            (P_CQ, _O_CQ, Q_LORA))
ZERO_W_IN = ((P_KRT + 2 * MLA_ROPE, P_CQ),)
SEG_W_UQ = tuple((MLA_NOPE * h, MLA_QK * h, MLA_NOPE) for h in range(MLA_HEADS)) + tuple(
    (MLA_W + LANES * (h // 2) + MLA_ROPE * (h % 2), MLA_QK * h + MLA_NOPE, MLA_ROPE) for h in range(MLA_HEADS))
ZERO_W_UQ = tuple((MLA_W + LANES * g + 2 * MLA_ROPE, MLA_W + LANES * (g + 1)) for g in range(MLA_HEADS // 2))
SEG_W_UKV = tuple((MLA_NOPE * h, (MLA_NOPE + MLA_V) * h, MLA_NOPE) for h in range(MLA_HEADS)) + tuple(
    (MLA_W + MLA_V * h, (MLA_NOPE + MLA_V) * h + MLA_NOPE, MLA_V) for h in range(MLA_HEADS))
SEG_W_UP = tuple((2 * FF_BLK * blk + FF_BLK * half, D_FF * half + FF_BLK * blk, FF_BLK)
                 for half in range(2) for blk in range(N_FF_BLK))


def _sum8(g, *, name):
    _, R, C = g.shape

    def body(g_ref, o_ref):
        acc = g_ref[0]
        for k in range(1, N_DEV):
            acc = acc + g_ref[k]
        o_ref[...] = acc

    return pl.pallas_call(
        body, name=name, out_shape=jax.ShapeDtypeStruct((R, C), F32),
    )(g)


def _adamw_math(w, gf, m, v):
    c1 = 1.0 / (1.0 - ADAM_B1 ** ADAM_STEP)
    c2 = 1.0 / (1.0 - ADAM_B2 ** ADAM_STEP)
    mn = ADAM_B1 * m + (1.0 - ADAM_B1) * gf
    vn = ADAM_B2 * v + (1.0 - ADAM_B2) * (gf * gf)
    return -ADAM_LR * ((mn * c1) / (jnp.sqrt(vn * c2) + ADAM_EPS) + ADAM_WD * w), mn, vn


def _adamw(w, g, m, v, *, name):
    R, C = w.shape
    tr = _row_tile(R)

    def body(w_ref, g_ref, m_ref, v_ref, d_ref, mo_ref, vo_ref):
        d_ref[...], mo_ref[...], vo_ref[...] = _adamw_math(w_ref[...], g_ref[...], m_ref[...], v_ref[...])

    blk = pl.BlockSpec((tr, C), lambda i: (i, 0))
    shp = jax.ShapeDtypeStruct((R, C), F32)
    return pl.pallas_call(
        body, name=name, grid=(R // tr,), in_specs=[blk] * 4, out_specs=[blk] * 3,
        out_shape=[shp, shp, shp], compiler_params=_cparams(("parallel",)),
    )(w, g, m, v)


def _adamw_rs8(g8, r7, me_idx, w, m, v, *, name):
    R, C = w.shape
    tr = _row_tile(R)

    def body(i_ref, f_ref, r_ref, w_ref, m_ref, v_ref, g_ref, d_ref, mo_ref, vo_ref):
        gf = f_ref[...].astype(F32)
        for k in range(N_DEV - 1):
            gf = gf + r_ref[k].astype(F32)
        g_ref[...] = gf
        d_ref[...], mo_ref[...], vo_ref[...] = _adamw_math(w_ref[...], gf, m_ref[...], v_ref[...])

    blk = pl.BlockSpec((tr, C), lambda i, i_ref: (i, 0))
    shp = jax.ShapeDtypeStruct((R, C), F32)
    return pl.pallas_call(
        body, name=name,
        grid_spec=pltpu.PrefetchScalarGridSpec(
            num_scalar_prefetch=1, grid=(R // tr,),
            in_specs=[pl.BlockSpec((None, tr, C), lambda i, i_ref: (i_ref[0], i, 0)),
                      pl.BlockSpec((N_DEV - 1, tr, C), lambda i, i_ref: (0, i, 0)), blk, blk, blk],
            out_specs=[blk] * 4),
        out_shape=[shp] * 4, compiler_params=_cparams(("parallel",)),
    )(me_idx, g8, r7, w, m, v)


def _adamw_rs(own, r3, w, m, v, *, name):
    R, C = w.shape
    tr = _row_tile(R)

    def body(f_ref, r_ref, w_ref, m_ref, v_ref, g_ref, d_ref, mo_ref, vo_ref):
        gf = ((f_ref[...] + r_ref[0].astype(F32)) + r_ref[1].astype(F32)) + r_ref[2].astype(F32)
        g_ref[...] = gf
        d_ref[...], mo_ref[...], vo_ref[...] = _adamw_math(w_ref[...], gf, m_ref[...], v_ref[...])

    blk = pl.BlockSpec((tr, C), lambda i: (i, 0))
    shp = jax.ShapeDtypeStruct((R, C), F32)
    return pl.pallas_call(
        body, name=name, grid=(R // tr,),
        in_specs=[blk, pl.BlockSpec((3, tr, C), lambda i: (0, i, 0)), blk, blk, blk], out_specs=[blk] * 4,
        out_shape=[shp] * 4, compiler_params=_cparams(("parallel",)),
    )(own, r3, w, m, v)


def _ff_interleave(a):
    lead = a.shape[:-1]
    return a.reshape(*lead, 2, N_FF_BLK, FF_BLK).swapaxes(-3, -2).reshape(*lead, 2 * D_FF)


def _ff_deinterleave(a):
    lead = a.shape[:-1]
    return a.reshape(*lead, N_FF_BLK, 2, FF_BLK).swapaxes(-3, -2).reshape(*lead, 2 * D_FF)


SMALL =(("g_mix", D_MODEL), ("g_cq", Q_LORA), ("g_ckv", KV_LORA), ("g_sb_out", SB_W), ("g_mla_out", MLA_W),
         ("g_ffn", D_MODEL), ("conv_b", 2 * D_FF), ("g_final", D_MODEL))
SMALL_ROWS = 88


SMALL_USED = sum(size for _, size in SMALL)


def _pack_small(d, tail=None):
    parts = [d[n].reshape(-1) for n, _ in SMALL] + ([] if tail is None else [tail])
    flat = jnp.concatenate(parts)
    flat = jnp.pad(flat, (0, SMALL_ROWS * LANES - flat.shape[0]))
    return flat.reshape(SMALL_ROWS, LANES)


def _unpack_small(a):
    flat = a.reshape(-1)
    out, off = {}, 0
    for n, size in SMALL:
        out[n] = flat[off:off + size]
        off += size
    return out


def kernel(x, positions, g_mix, w_in, g_cq, w_uq, g_ckv, w_ukv, g_sb_out, g_mla_out, w_out, g_ffn, w_up, conv_w, conv_b, w_down, g_final, loss_target, m_g_mix, m_w_in, m_g_cq, m_w_uq, m_g_ckv, m_w_ukv, m_g_sb_out, m_g_mla_out, m_w_out, m_g_ffn, m_w_up, m_conv_w, m_conv_b, m_w_down, m_g_final, v_g_mix, v_w_in, v_g_cq, v_w_uq, v_g_ckv, v_w_ukv, v_g_sb_out, v_g_mla_out, v_w_out, v_g_ffn, v_w_up, v_conv_w, v_conv_b, v_w_down, v_g_final):
    B, S, D = x.shape
    T = B * S
    xf = x.reshape(T, D)
    tgt = loss_target.reshape(T, D)
    pos = positions.reshape(T, 1)
    half = MLA_ROPE // 2
    inv_freq = 1.0 / (ROPE_BASE ** (jnp.arange(half, dtype=F32) * (2.0 / MLA_ROPE)))
    invf = jnp.tile(inv_freq, LANES // half).reshape(1, LANES)
    place_idx = jnp.stack([lax.axis_index("c"), 2 * lax.axis_index("x") + lax.axis_index("y")]).astype(jnp.int32)
    me_idx = (4 * lax.axis_index("x") + 2 * lax.axis_index("y") + lax.axis_index("c")).astype(jnp.int32).reshape(1)

    names = ("w_in", "w_uq", "w_ukv", "w_out", "w_up", "w_down", "conv_w")
    shard = {"w_in": w_in[0], "w_uq": w_uq[0], "w_ukv": w_ukv[0], "w_out": w_out[0], "w_up": w_up[0],
             "w_down": w_down[0], "conv_w": conv_w[0]}
    sent = {n: shard[n] if n == "conv_w" else shard[n].astype(BF16) for n in names}
    later = names[1:]
    w_in_all = _all_gather([jnp.transpose(shard["w_in"]).astype(BF16)], name="ag_w_in")[0]
    w_in_all, rest = lax.optimization_barrier((w_in_all, [sent[n] for n in later]))
    got = {"w_in": w_in_all}
    got.update(zip(later, _all_gather_async(rest, name="ag_weights_async", collective_id=0)))
    wi_t = _assemble_rows(got["w_in"], SEG_W_IN, ZERO_W_IN, P_COLS, name="asm_w_in")
    wuq = _assemble(got["w_uq"], SEG_W_UQ, ZERO_W_UQ, 2 * MLA_W, name="asm_w_uq")
    wukv = _assemble(got["w_ukv"], SEG_W_UKV, (), 2 * MLA_W, name="asm_w_ukv")
    wup = _assemble(got["w_up"], SEG_W_UP, (), 2 * D_FF, name="asm_w_up")
    cwi = _assemble(got["conv_w"], SEG_W_UP, (), 2 * D_FF, name="asm_conv_w")
    wo = got["w_out"].reshape(D, D)
    wdn = got["w_down"].reshape(D_FF, D)
    cbi = _ff_interleave(conv_b)

    h, p = _rms_matmul_nn(xf, g_mix, wi_t, tm=512, name="proj_in", w_transposed=True)
    o_sb, ltot = _sb_fwd(p, seq=S, name="sb_fwd")
    cq, qm, krt = _proj_uq_rope(p, g_cq, wuq, pos, invf, tm=512, name="proj_uq")
    ckv, kvm = _rms_matmul_nn(p, g_ckv, wukv, tm=512, name="proj_ukv", col_block=P_CKV // KV_LORA, out_dtype=BF16)
    o_mla, lse = _mla_fwd(qm, kvm, krt, seq=S, name="mla_fwd")
    ocat, x1 = _heads_out(o_sb, o_mla, g_sb_out, g_mla_out, wo, xf, tm=512, name="proj_out")
    hf, u = _rms_matmul_nn(x1, g_ffn, wup, tm=256, name="ffn_up")
    a = _conv_fwd(u, cwi, cbi, seq=S, name="conv_fwd")
    dx2, dg_final, loss_row = _matmul_nn_loss(a, wdn, x1, g_final.reshape(1, D), tgt, tm=512, name="ffn_down_loss")

    du, dcw, dcb, dw_down = _conv_bwd(u, dx2, wdn, cwi, cbi, seq=S, name="conv_bwd")
    dw_up_t = _matmul_tn(du, hf, tm=D_FF, tn=1024, tk=1024, name="dw_up")
    dx1, dg_ffn = _matmul_nt_rms_bwd(du, wup, x1, g_ffn, tm=512, name="d_ffn_up", residual=dx2)
    dw_out = _matmul_tn(ocat, dx1, tm=1024, tn=1024, tk=1024, name="dw_out", out_dtype=BF16)
    do_sb, do_mla, dg_sb, dg_mla = _heads_out_bwd(dx1, wo, o_sb, o_mla, g_sb_out, g_mla_out, tm=512, name="d_proj_out")

    early = ("w_down", "w_up", "conv_w", "w_out")
    g8 = {"w_up": _disassemble_rows(dw_up_t, SEG_W_UP, shard["w_up"].shape[1], name="split_dw_up", out_dtype=BF16),
          "conv_w": _disassemble(dcw, SEG_W_UP, shard["conv_w"].shape[1], name="split_dconv_w", out_dtype=BF16),
          "w_out": dw_out.reshape((N_DEV,) + shard["w_out"].shape),
          "w_down": dw_down.reshape((N_DEV,) + shard["w_down"].shape)}
    r7 = dict(zip(early, _rs_direct_async([g8[n] for n in early], name="rs_direct_async", collective_id=1)))
    own, r3 = {}, {}

    dq_sb, dk_sb, dv_sb = _sb_bwd(p, ltot, do_sb, seq=S, name="sb_bwd")
    dqn, dqr, dkn, dvm, dkr = _mla_bwd(qm, kvm, krt, o_mla, lse, do_mla, seq=S, name="mla_bwd")
    dqm, dcq, dg_cq, dkr_u = _d_proj_uq_rope(dqn, dqr, dkr, wuq, p, g_cq, pos, invf, tm=512, name="d_proj_uq")
    dkvm, dckv, dg_ckv = _d_proj_cat([dkn, dvm], wukv, p, g_ckv, tm=512, name="d_proj_ukv",
                                     col_block=P_CKV // KV_LORA, out_dtype=BF16)
    dw_uq_t = _matmul_tn(dqm, cq, tm=2 * MLA_W, tn=Q_LORA, tk=1024, name="dw_uq")
    dw_ukv = _matmul_tn(ckv, dkvm, tm=KV_LORA, tn=1024, tk=1024, name="dw_ukv")
    dp, dx, dg_mix = _d_proj_cat([dq_sb, dk_sb, dv_sb, dckv, dkr_u, dcq], wi_t, xf, g_mix, tm=512, name="d_proj_in",
                                 residual=dx1, b_transposed=True)
    dw_in_t = _matmul_tn(dp, h, tm=P_COLS, tn=1024, tk=1024, name="dw_in")

    late = ("w_in", "w_uq", "w_ukv")
    g8.update({"w_in": _disassemble_rows(dw_in_t, SEG_W_IN, shard["w_in"].shape[1], name="split_dw_in"),
               "w_uq": _disassemble_rows(dw_uq_t, SEG_W_UQ, shard["w_uq"].shape[1], name="split_dw_uq"),
               "w_ukv": _disassemble(dw_ukv, SEG_W_UKV, shard["w_ukv"].shape[1], name="split_dw_ukv")})
    sib_l = _rs_sibling_async([g8[n] for n in late], name="rs_sibling_late", collective_id=3)

    params = {"w_in": (w_in, m_w_in, v_w_in), "w_uq": (w_uq, m_w_uq, v_w_uq), "w_ukv": (w_ukv, m_w_ukv, v_w_ukv),
              "w_out": (w_out, m_w_out, v_w_out), "w_up": (w_up, m_w_up, v_w_up), "conv_w": (conv_w, m_conv_w, v_conv_w),
              "w_down": (w_down, m_w_down, v_w_down)}
    grad, delta, new_m, new_v = {}, {}, {}, {}

    transposed = ("w_in", "w_uq", "w_up")

    def adamw_group(group):
        for n in group:
            flip = jnp.transpose if n in transposed else (lambda t: t)
            w_, m_, v_ = [flip(t[0]) for t in params[n]]
            if n in r7:
                res = _adamw_rs8(g8[n], r7[n], me_idx, w_, m_, v_, name="adamw_" + n)
            else:
                res = _adamw_rs(own[n], r3[n], w_, m_, v_, name="adamw_" + n)
            grad[n], delta[n], new_m[n], new_v[n] = [flip(r)[None] for r in res]

    adamw_group(("w_down", "w_out", "conv_w"))
    sib_l, grad["w_down"] = lax.optimization_barrier((sib_l, grad["w_down"]))
    sums_l = [_rs_chip_sum(g8[n], fs, place_idx, name="rs_chip_sum_" + n) for n, fs in zip(late, sib_l)]
    r3.update(zip(late, _rs_chips_async([h4 for _, h4 in sums_l], name="rs_chips_late", collective_id=4)))
    own.update({n: f for n, (f, _) in zip(late, sums_l)})
    small_part = {"g_mix": dg_mix, "g_cq": dg_cq, "g_ckv": dg_ckv, "g_sb_out": dg_sb, "g_mla_out": dg_mla,
                  "g_ffn": dg_ffn, "conv_b": _ff_deinterleave(dcb), "g_final": dg_final}
    small_all, = _all_gather_async([_pack_small(small_part, tail=loss_row[0, 0:1])], name="ag_small_async",
                                   collective_id=5)
    adamw_group(("w_up",))
    adamw_group(late)
    gsmall = _sum8(small_all, name="sum_small_grads")
    small_w = {"g_mix": g_mix, "g_cq": g_cq, "g_ckv": g_ckv, "g_sb_out": g_sb_out, "g_mla_out": g_mla_out,
               "g_ffn": g_ffn, "conv_b": conv_b, "g_final": g_final}
    small_m = {"g_mix": m_g_mix, "g_cq": m_g_cq, "g_ckv": m_g_ckv, "g_sb_out": m_g_sb_out, "g_mla_out": m_g_mla_out,
               "g_ffn": m_g_ffn, "conv_b": m_conv_b, "g_final": m_g_final}
    small_v = {"g_mix": v_g_mix, "g_cq": v_g_cq, "g_ckv": v_g_ckv, "g_sb_out": v_g_sb_out, "g_mla_out": v_g_mla_out,
               "g_ffn": v_g_ffn, "conv_b": v_conv_b, "g_final": v_g_final}
    ds_, ms_, vs_ = _adamw(_pack_small(small_w), gsmall, _pack_small(small_m), _pack_small(small_v), name="adamw_small")
    for src, dst in ((_unpack_small(gsmall), grad), (_unpack_small(ds_), delta), (_unpack_small(ms_), new_m), (_unpack_small(vs_), new_v)):
        for n, _ in SMALL:
            dst[n] = src[n].reshape(small_w[n].shape)

    loss = gsmall.reshape(-1)[SMALL_USED]
    order = ("g_mix", "w_in", "g_cq", "w_uq", "g_ckv", "w_ukv", "g_sb_out", "g_mla_out", "w_out", "g_ffn", "w_up",
             "conv_w", "conv_b", "w_down", "g_final")
    return (loss, dx.reshape(B, S, D), *[grad[n] for n in order], *[delta[n] for n in order],
            *[new_m[n] for n in order], *[new_v[n] for n in order])
```

```python
import jax
import jax.numpy as jnp
from jax import lax
from jax.experimental import pallas as pl
from jax.experimental.pallas import tpu as pltpu
from jax.experimental.pallas import tpu_sc as plsc

F32 = jnp.float32
BF16 = jnp.bfloat16

D_MODEL = 1024
SB_HEADS = 8
SB_HEAD_DIM = 64
MLA_HEADS = 8
MLA_NOPE = 64
MLA_ROPE = 32
MLA_V = 64
Q_LORA = 384
KV_LORA = 256
D_FF = 2816
ROPE_BASE = 10000.0
EPS = 1e-6
SB_W = SB_HEADS * SB_HEAD_DIM
MLA_W = MLA_HEADS * MLA_V
MLA_QK = MLA_NOPE + MLA_ROPE

ADAM_LR = 0.001
ADAM_B1 = 0.9
ADAM_B2 = 0.999
ADAM_EPS = 1e-08
ADAM_WD = 0.01
ADAM_STEP = 10

N_DEV = 8
LANES = 128
V7X_VMEM_LIMIT = 56 * 1024 * 1024
FF_BLK = 256
N_FF_BLK = D_FF // FF_BLK

P_Q, P_K, P_V = 0, SB_W, 2 * SB_W
P_CKV = 3 * SB_W
P_KRT = P_CKV + KV_LORA
P_CQ = P_KRT + LANES
P_COLS = P_CQ + Q_LORA

MESH = pl.DeviceIdType.MESH
ANY = pl.BlockSpec(memory_space=pl.ANY)


def _cparams(sem=None, vmem=V7X_VMEM_LIMIT):
    return pltpu.CompilerParams(dimension_semantics=sem, vmem_limit_bytes=vmem)


def _matmul_tn(a, b, *, tm, tn, tk, name, out_dtype=F32):
    K, M = a.shape
    N = b.shape[1]
    assert M % tm == 0 and N % tn == 0 and K % tk == 0, (name, a.shape, b.shape)
    n_k = K // tk
    narrow = out_dtype != F32

    def body(a_ref, b_ref, o_ref, *scratch):
        acc_ref = scratch[0] if narrow else o_ref
        k = pl.program_id(2)
        part = lax.dot_general(a_ref[...].astype(BF16), b_ref[...].astype(BF16), (((0,), (0,)), ((), ())),
                               preferred_element_type=F32)

        @pl.when(k == 0)
        def _():
            acc_ref[...] = part

        @pl.when(k > 0)
        def _():
            acc_ref[...] += part

        if narrow:
            @pl.when(k == n_k - 1)
            def _():
                o_ref[...] = acc_ref[...].astype(out_dtype)

    return pl.pallas_call(
        body, name=name, grid=(M // tm, N // tn, n_k),
        in_specs=[pl.BlockSpec((tk, tm), lambda i, j, k: (k, i)), pl.BlockSpec((tk, tn), lambda i, j, k: (k, j))],
        out_specs=pl.BlockSpec((tm, tn), lambda i, j, k: (i, j)),
        out_shape=jax.ShapeDtypeStruct((M, N), out_dtype),
        scratch_shapes=[pltpu.VMEM((tm, tn), F32)] if narrow else [],
        compiler_params=_cparams(("parallel", "parallel", "arbitrary")),
    )(a, b)


def _rms(xf, g):
    r = lax.rsqrt(jnp.mean(xf * xf, axis=1, keepdims=True) + EPS)
    return (xf * r) * g


def _rms_grad(dyf, xf, g):
    r = lax.rsqrt(jnp.mean(xf * xf, axis=1, keepdims=True) + EPS)
    xh = xf * r
    dyg = dyf * g
    dx = r * (dyg - xh * jnp.mean(dyg * xh, axis=1, keepdims=True))
    return dx, jnp.sum(dyf * xh, axis=0, keepdims=True)


def _accumulate(ref, part):
    @pl.when(pl.program_id(0) == 0)
    def _():
        ref[...] = part

    @pl.when(pl.program_id(0) > 0)
    def _():
        ref[...] += part


def _rms_matmul_nn(x, g, w, *, tm, name, col_block=0, out_dtype=F32, w_transposed=False):
    T = x.shape[0]
    C, N = w.shape[::-1] if w_transposed else w.shape
    assert T % tm == 0, (name, x.shape)
    contract = (((1,), (1,)), ((), ())) if w_transposed else (((1,), (0,)), ((), ()))

    def body(x_ref, g_ref, w_ref, h_ref, o_ref):
        hb = _rms(x_ref[...], g_ref[...]).astype(BF16)
        h_ref[...] = hb
        o_ref[...] = lax.dot_general(hb, w_ref[...], contract, preferred_element_type=F32).astype(out_dtype)

    return pl.pallas_call(
        body, name=name, grid=(T // tm,),
        in_specs=[pl.BlockSpec((tm, C), lambda i: (i, col_block)), pl.BlockSpec((1, C), lambda i: (0, 0)),
                  pl.BlockSpec(w.shape, lambda i: (0, 0))],
        out_specs=[pl.BlockSpec((tm, C), lambda i: (i, 0)), pl.BlockSpec((tm, N), lambda i: (i, 0))],
        out_shape=[jax.ShapeDtypeStruct((T, C), BF16), jax.ShapeDtypeStruct((T, N), out_dtype)],
        compiler_params=_cparams(("parallel",)),
    )(x, g, w)


def _matmul_nt_rms_bwd(a, b, x, g, *, tm, name, residual=None, col_block=0, out_dtype=F32):
    M, K = a.shape
    C = b.shape[0]
    assert M % tm == 0, (name, a.shape)
    in_specs = [pl.BlockSpec((tm, K), lambda i: (i, 0)), pl.BlockSpec((C, K), lambda i: (0, 0)),
                pl.BlockSpec((tm, C), lambda i: (i, col_block)), pl.BlockSpec((1, C), lambda i: (0, 0))]
    args = [a, b, x, g]
    if residual is not None:
        in_specs.append(pl.BlockSpec((tm, C), lambda i: (i, 0)))
        args.append(residual)

    def body(*refs):
        a_ref, b_ref, x_ref, g_ref = refs[:4]
        dx_ref, dg_ref = refs[-2:]
        dy = lax.dot_general(a_ref[...].astype(BF16), b_ref[...], (((1,), (1,)), ((), ())), preferred_element_type=F32)
        dx, part = _rms_grad(dy, x_ref[...], g_ref[...])
        if residual is not None:
            dx = dx + refs[4][...]
        dx_ref[...] = dx.astype(out_dtype)
        _accumulate(dg_ref, part)

    return pl.pallas_call(
        body, name=name, grid=(M // tm,), in_specs=in_specs,
        out_specs=[pl.BlockSpec((tm, C), lambda i: (i, 0)), pl.BlockSpec((1, C), lambda i: (0, 0))],
        out_shape=[jax.ShapeDtypeStruct((M, C), out_dtype), jax.ShapeDtypeStruct((1, C), F32)],
        compiler_params=_cparams(("arbitrary",)),
    )(*args)


def _matmul_nn_loss(a, w, x1, g, tgt, *, tm, name):
    M, K = a.shape
    C = w.shape[1]
    assert M % tm == 0, (name, a.shape)

    nsub = 4
    ts = tm // nsub

    def body(a_ref, w_ref, x_ref, g_ref, t_ref, dx_ref, dg_ref, loss_ref):
        gf = g_ref[...]
        wv = w_ref[...]
        rows = [slice(r * ts, (r + 1) * ts) for r in range(nsub)]
        xs = [x_ref[rw, :] + jnp.dot(a_ref[rw, :], wv, preferred_element_type=F32) for rw in rows]
        lpart, gpart = 0.0, 0.0
        for rw, xf in zip(rows, xs):
            err = _rms(xf, gf) - t_ref[rw, :]
            lpart = lpart + 0.5 * jnp.sum(jnp.mean(err * err, axis=1, keepdims=True), axis=0, keepdims=True)
            dx, gp = _rms_grad(err * (1.0 / C), xf, gf)
            dx_ref[rw, :] = dx
            gpart = gpart + gp
        _accumulate(dg_ref, gpart)
        _accumulate(loss_ref, jnp.broadcast_to(lpart, (1, LANES)))

    row = pl.BlockSpec((tm, C), lambda i: (i, 0))
    return pl.pallas_call(
        body, name=name, grid=(M // tm,),
        in_specs=[pl.BlockSpec((tm, K), lambda i: (i, 0)), pl.BlockSpec((K, C), lambda i: (0, 0)), row,
                  pl.BlockSpec((1, C), lambda i: (0, 0)), row],
        out_specs=[row, pl.BlockSpec((1, C), lambda i: (0, 0)), pl.BlockSpec((1, LANES), lambda i: (0, 0))],
        out_shape=[jax.ShapeDtypeStruct((M, C), F32), jax.ShapeDtypeStruct((1, C), F32),
                   jax.ShapeDtypeStruct((1, LANES), F32)],
        compiler_params=_cparams(("arbitrary",)),
    )(a, w, x1, g, tgt)


ATT_T = 256
ATT_PAIRS = 2
NEG_BIG = -1e30


def _lane_iota():
    return lax.broadcasted_iota(jnp.int32, (1, LANES), 1)


def _head_masks():
    first = _lane_iota() < SB_HEAD_DIM
    return first, jnp.logical_not(first)


def _pick(mask, x):
    return jnp.where(mask, x, jnp.zeros_like(x))


def _lane_value(t, lane):
    return jnp.sum(jnp.where(_lane_iota() == lane, t, 0.0), axis=1, keepdims=True)


def _split_hi_lo(x):
    hi = x.astype(BF16)
    lo = (x - hi.astype(F32)).astype(BF16)
    return jnp.concatenate([hi, lo], axis=1)


def _tri(n, kind):
    r = lax.broadcasted_iota(jnp.int32, (n, n), 0)
    c = lax.broadcasted_iota(jnp.int32, (n, n), 1)
    u = {"suffix_excl": r > c, "prefix_incl": r <= c, "prefix_excl": r < c}[kind].astype(BF16)
    return jnp.concatenate([u, u], axis=0)


def _dot_nt(a, b):
    return lax.dot_general(a, b, (((1,), (1,)), ((), ())), preferred_element_type=F32)


def _dot_tn(a, b):
    return lax.dot_general(a, b, (((0,), (0,)), ((), ())), preferred_element_type=F32)


def _dot(a, b):
    return jnp.dot(a, b, preferred_element_type=F32)


def _causal_mask(n, strict):
    r = lax.broadcasted_iota(jnp.int32, (n, n), 0)
    c = lax.broadcasted_iota(jnp.int32, (n, n), 1)
    return (c < r) if strict else (c <= r)


LOG2E = 1.4426950408889634


def _sb_logs(qh, kj, vis):
    z2 = _dot_nt(qh, kj) * LOG2E
    nk = jnp.maximum(z2, 0.0) + jnp.log2(1.0 + jnp.exp2(-jnp.abs(z2)))
    lb = z2 - nk
    if vis is not None:
        nk = jnp.where(vis, nk, 0.0)
    return lb, nk


def _sb_fwd(p, *, seq, name):
    T = p.shape[0]
    B = T // seq
    TQ = ATT_T
    nq = seq // TQ
    PP = ATT_PAIRS
    W = PP * LANES
    nstep = SB_W // W
    NH = 2 * PP

    def body(q_ref, k_ref, v_ref, o_ref, lt_ref, q_s, k_s, v_s):
        masks = _head_masks()
        q = q_ref[...] * (SB_HEAD_DIM ** -0.5)
        v = v_ref[...]
        k_s[...] = k_ref[...].astype(BF16)
        for h in range(NH):
            ps = slice((h // 2) * LANES, (h // 2 + 1) * LANES)
            hs = slice(h * LANES, (h + 1) * LANES)
            q_s[:, hs] = _pick(masks[h % 2], q[:, ps]).astype(BF16)
            v_s[:, hs] = _pick(masks[h % 2], v[:, ps]).astype(BF16)
        u_suf = _tri(TQ, "suffix_excl")
        vis = _causal_mask(TQ, True)

        def q_block(i, carry):
            q0 = pl.multiple_of(i * TQ, TQ)
            qs = [q_s[pl.ds(q0, TQ), h * LANES:(h + 1) * LANES] for h in range(NH)]

            def tile(k0, c, mask):
                rs, accs = list(c[:NH]), list(c[NH:])
                logs = [_sb_logs(qs[h], k_s[pl.ds(k0, TQ), (h // 2) * LANES:(h // 2 + 1) * LANES], mask) for h in range(NH)]
                sums = [_dot(_split_hi_lo(nk), u_suf) for _, nk in logs]
                for h in range(NH):
                    a = jnp.exp2(logs[h][0] - sums[h] - rs[h])
                    if mask is not None:
                        a = jnp.where(mask, a, 0.0)
                    accs[h // 2] = accs[h // 2] + _dot(a.astype(BF16), v_s[pl.ds(k0, TQ), h * LANES:(h + 1) * LANES])
                    rs[h] = rs[h] + jnp.sum(logs[h][1], axis=1, keepdims=True)
                return tuple(rs) + tuple(accs)

            zero = jnp.zeros((TQ, 1), F32)
            c = tile(q0, (zero,) * NH + (jnp.zeros((TQ, LANES), F32),) * PP, vis)

            def k_block(jj, c):
                return tile(pl.multiple_of((i - 1 - jj) * TQ, TQ), c, None)

            c = lax.fori_loop(0, i, k_block, c)
            for pr in range(PP):
                ps = slice(pr * LANES, (pr + 1) * LANES)
                o_ref[pl.ds(q0, TQ), ps] = c[NH + pr]
                lt_ref[pl.ds(q0, TQ), ps] = jnp.where(masks[0], c[2 * pr], c[2 * pr + 1])
            return carry

        lax.fori_loop(0, nq, q_block, 0)

    blk = lambda off: pl.BlockSpec((seq, W), lambda b, g: (b, off + g))
    out_blk = pl.BlockSpec((seq, W), lambda b, g: (b, g))
    return pl.pallas_call(
        body, name=name, grid=(B, nstep),
        in_specs=[blk(P_Q // W), blk(P_K // W), blk(P_V // W)],
        out_specs=[out_blk, out_blk],
        out_shape=[jax.ShapeDtypeStruct((T, SB_W), F32), jax.ShapeDtypeStruct((T, SB_W), F32)],
        scratch_shapes=[pltpu.VMEM((seq, NH * LANES), BF16), pltpu.VMEM((seq, W), BF16), pltpu.VMEM((seq, NH * LANES), BF16)],
        compiler_params=_cparams(("parallel", "parallel")),
    )(p, p, p)


def _sb_bwd(p, ltot, do, *, seq, name):
    T = p.shape[0]
    B = T // seq
    TQ = ATT_T
    nq = seq // TQ
    PP = ATT_PAIRS
    W = PP * LANES
    nstep = SB_W // W
    NH = 2 * PP
    scale = SB_HEAD_DIM ** -0.5

    def body(q_ref, k_ref, v_ref, lt_ref, do_ref, dq_ref, dk_ref, dv_ref, q_s, k_s, v_s, do_s, dk_s, dv_s):
        masks = _head_masks()
        q = q_ref[...] * scale
        dof = do_ref[...]
        k_s[...] = k_ref[...].astype(BF16)
        v_s[...] = v_ref[...].astype(BF16)
        for h in range(NH):
            ps = slice((h // 2) * LANES, (h // 2 + 1) * LANES)
            hs = slice(h * LANES, (h + 1) * LANES)
            q_s[:, hs] = _pick(masks[h % 2], q[:, ps]).astype(BF16)
            do_s[:, hs] = _pick(masks[h % 2], dof[:, ps]).astype(BF16)
        dk_s[...] = jnp.zeros_like(dk_s)
        dv_s[...] = jnp.zeros_like(dv_s)
        u_pin = _tri(TQ, "prefix_incl")
        u_pex = _tri(TQ, "prefix_excl")[:TQ]
        vis = _causal_mask(TQ, True)

        def q_block(i, carry):
            q0 = pl.multiple_of(i * TQ, TQ)
            qs = [q_s[pl.ds(q0, TQ), h * LANES:(h + 1) * LANES] for h in range(NH)]
            dos = [do_s[pl.ds(q0, TQ), h * LANES:(h + 1) * LANES] for h in range(NH)]
            lt = lt_ref[pl.ds(q0, TQ), :]
            lts = [_lane_value(lt[:, (h // 2) * LANES:(h // 2 + 1) * LANES], (h % 2) * SB_HEAD_DIM) for h in range(NH)]

            def tile(k0, c, mask):
                cs, gs, accs = list(c[:NH]), list(c[NH:2 * NH]), list(c[2 * NH:])
                kjs = [k_s[pl.ds(k0, TQ), pr * LANES:(pr + 1) * LANES] for pr in range(PP)]
                vjs = [v_s[pl.ds(k0, TQ), pr * LANES:(pr + 1) * LANES] for pr in range(PP)]
                logs = [_sb_logs(qs[h], kjs[h // 2], mask) for h in range(NH)]
                pins = [_dot(_split_hi_lo(nk), u_pin) for _, nk in logs]
                das = [_dot_nt(dos[h], vjs[h // 2]) for h in range(NH)]
                a_l, g_l = [], []
                for h in range(NH):
                    a = jnp.exp2(logs[h][0] - ((lts[h] - cs[h]) - pins[h]))
                    if mask is not None:
                        a = jnp.where(mask, a, 0.0)
                    a_l.append(a)
                    g_l.append(das[h] * a)
                pres = [_dot(g.astype(BF16), u_pex) for g in g_l]
                dz_l = []
                for h in range(NH):
                    dz = g_l[h] - jnp.exp2(logs[h][0]) * (g_l[h] + (pres[h] + gs[h]))
                    if mask is not None:
                        dz = jnp.where(mask, dz, 0.0)
                    dz_l.append(dz.astype(BF16))
                for h in range(NH):
                    accs[h] = accs[h] + _dot(dz_l[h], kjs[h // 2])
                for pr in range(PP):
                    ps = slice(pr * LANES, (pr + 1) * LANES)
                    ha, hb = 2 * pr, 2 * pr + 1
                    dk_s[pl.ds(k0, TQ), ps] += _dot_tn(dz_l[ha], qs[ha]) + _dot_tn(dz_l[hb], qs[hb])
                    dv_s[pl.ds(k0, TQ), ps] += _dot_tn(a_l[ha].astype(BF16), dos[ha]) + _dot_tn(a_l[hb].astype(BF16), dos[hb])
                for h in range(NH):
                    cs[h] = cs[h] + jnp.sum(logs[h][1], axis=1, keepdims=True)
                    gs[h] = gs[h] + jnp.sum(g_l[h], axis=1, keepdims=True)
                return tuple(cs) + tuple(gs) + tuple(accs)

            z1 = jnp.zeros((TQ, 1), F32)
            zl = jnp.zeros((TQ, LANES), F32)

            def k_block(j, c):
                return tile(pl.multiple_of(j * TQ, TQ), c, None)

            c = lax.fori_loop(0, i, k_block, (z1,) * (2 * NH) + (zl,) * NH)
            c = tile(q0, c, vis)
            for pr in range(PP):
                dq = jnp.where(masks[0], c[2 * NH + 2 * pr], c[2 * NH + 2 * pr + 1]) * scale
                dq_ref[pl.ds(q0, TQ), pr * LANES:(pr + 1) * LANES] = dq.astype(BF16)
            return carry

        lax.fori_loop(0, nq, q_block, 0)
        dk_ref[...] = dk_s[...].astype(BF16)
        dv_ref[...] = dv_s[...].astype(BF16)

    blk = lambda off: pl.BlockSpec((seq, W), lambda b, g: (b, off + g))
    out_blk = pl.BlockSpec((seq, W), lambda b, g: (b, g))
    return pl.pallas_call(
        body, name=name, grid=(B, nstep),
        in_specs=[blk(P_Q // W), blk(P_K // W), blk(P_V // W), out_blk, out_blk],
        out_specs=[out_blk, out_blk, out_blk],
        out_shape=[jax.ShapeDtypeStruct((T, SB_W), BF16) for _ in range(3)],
        scratch_shapes=[pltpu.VMEM((seq, NH * LANES), BF16), pltpu.VMEM((seq, W), BF16), pltpu.VMEM((seq, W), BF16),
                        pltpu.VMEM((seq, NH * LANES), BF16), pltpu.VMEM((seq, W), F32), pltpu.VMEM((seq, W), F32)],
        compiler_params=_cparams(("parallel", "parallel")),
    )(p, p, p, ltot, do)


def _mla_masks():
    lane = lax.broadcasted_iota(jnp.int32, (1, 2 * LANES), 1)
    ma = (lane < MLA_NOPE) | ((lane >= LANES) & (lane < LANES + MLA_ROPE))
    mb = ((lane >= MLA_NOPE) & (lane < LANES)) | ((lane >= LANES + MLA_ROPE) & (lane < LANES + 2 * MLA_ROPE))
    return ma, mb


def _mla_fwd(qm, kvm, krt, *, seq, name):
    T = qm.shape[0]
    B = T // seq
    TQ = ATT_T
    nq = seq // TQ
    PP = ATT_PAIRS
    W = PP * LANES
    nstep = MLA_W // W
    NH = 2 * PP
    CW = 2 * LANES
    scale = MLA_QK ** -0.5

    def body(qn_ref, qr_ref, kn_ref, v_ref, kr_ref, o_ref, lse_ref, q_s, kc_s, v_s):
        hm = _head_masks()
        mm = _mla_masks()
        v = v_ref[...]
        for pr in range(PP):
            ps = slice(pr * LANES, (pr + 1) * LANES)
            qc = jnp.concatenate([qn_ref[:, ps], qr_ref[:, ps]], axis=1)
            kc_s[:, pr * CW:(pr + 1) * CW] = jnp.concatenate([kn_ref[:, ps], kr_ref[...]], axis=1)
            for e in range(2):
                h = 2 * pr + e
                q_s[:, h * CW:(h + 1) * CW] = _pick(mm[e], qc)
                v_s[:, h * LANES:(h + 1) * LANES] = _pick(hm[e], v[:, ps])
        vis = _causal_mask(TQ, False)

        def q_block(i, carry):
            q0 = pl.multiple_of(i * TQ, TQ)
            qs = [q_s[pl.ds(q0, TQ), h * CW:(h + 1) * CW] for h in range(NH)]

            def tile(k0, c, mask):
                ms, ls, accs = list(c[:NH]), list(c[NH:2 * NH]), list(c[2 * NH:])
                ss = [_dot_nt(qs[h], kc_s[pl.ds(k0, TQ), (h // 2) * CW:(h // 2 + 1) * CW]) * scale for h in range(NH)]
                if mask is not None:
                    ss = [jnp.where(mask, s, NEG_BIG) for s in ss]
                m_new = [jnp.maximum(ms[h], jnp.max(ss[h], axis=1, keepdims=True)) for h in range(NH)]
                alphas = [jnp.exp(ms[h] - m_new[h]) for h in range(NH)]
                prs = [jnp.exp(ss[h] - m_new[h]) for h in range(NH)]
                outs = [_dot(prs[h].astype(BF16), v_s[pl.ds(k0, TQ), h * LANES:(h + 1) * LANES]) for h in range(NH)]
                ls = [alphas[h] * ls[h] + jnp.sum(prs[h], axis=1, keepdims=True) for h in range(NH)]
                for pr in range(PP):
                    accs[pr] = accs[pr] * jnp.where(hm[0], alphas[2 * pr], alphas[2 * pr + 1]) + outs[2 * pr] + outs[2 * pr + 1]
                return tuple(m_new) + tuple(ls) + tuple(accs)

            neg = jnp.full((TQ, 1), NEG_BIG, F32)
            z1 = jnp.zeros((TQ, 1), F32)

            def k_block(j, c):
                return tile(pl.multiple_of(j * TQ, TQ), c, None)

            c = lax.fori_loop(0, i, k_block, (neg,) * NH + (z1,) * NH + (jnp.zeros((TQ, LANES), F32),) * PP)
            c = tile(q0, c, vis)
            for pr in range(PP):
                ps = slice(pr * LANES, (pr + 1) * LANES)
                m_a, m_b, l_a, l_b = c[2 * pr], c[2 * pr + 1], c[NH + 2 * pr], c[NH + 2 * pr + 1]
                o_ref[pl.ds(q0, TQ), ps] = c[2 * NH + pr] / jnp.where(hm[0], l_a, l_b)
                lse_ref[pl.ds(q0, TQ), ps] = jnp.where(hm[0], m_a + jnp.log(l_a), m_b + jnp.log(l_b))
            return carry

        lax.fori_loop(0, nq, q_block, 0)

    blk = lambda off: pl.BlockSpec((seq, W), lambda b, g: (b, off + g))
    out_blk = pl.BlockSpec((seq, W), lambda b, g: (b, g))
    return pl.pallas_call(
        body, name=name, grid=(B, nstep),
        in_specs=[blk(0), blk(nstep), blk(0), blk(nstep), pl.BlockSpec((seq, LANES), lambda b, g: (b, 0))],
        out_specs=[out_blk, out_blk],
        out_shape=[jax.ShapeDtypeStruct((T, MLA_W), F32), jax.ShapeDtypeStruct((T, MLA_W), F32)],
        scratch_shapes=[pltpu.VMEM((seq, NH * CW), BF16), pltpu.VMEM((seq, PP * CW), BF16), pltpu.VMEM((seq, NH * LANES), BF16)],
        compiler_params=_cparams(("parallel", "parallel")),
    )(qm, qm, kvm, kvm, krt)


def _mla_bwd(qm, kvm, krt, o, lse, do, *, seq, name):
    T = qm.shape[0]
    B = T // seq
    TQ = ATT_T
    nq = seq // TQ
    PP = ATT_PAIRS
    W = PP * LANES
    nstep = MLA_W // W
    NH = 2 * PP
    CW = 2 * LANES
    scale = MLA_QK ** -0.5

    def body(qn_ref, qr_ref, kn_ref, v_ref, kr_ref, o_ref, lse_ref, do_ref,
             dqn_ref, dqr_ref, dkn_ref, dv_ref, dkr_ref, q_s, kc_s, do_s, dkc_s, dv_s):
        hm = _head_masks()
        mm = _mla_masks()
        dof = do_ref[...]
        for pr in range(PP):
            ps = slice(pr * LANES, (pr + 1) * LANES)
            qc = jnp.concatenate([qn_ref[:, ps], qr_ref[:, ps]], axis=1)
            kc_s[:, pr * CW:(pr + 1) * CW] = jnp.concatenate([kn_ref[:, ps], kr_ref[...]], axis=1)
            for e in range(2):
                h = 2 * pr + e
                q_s[:, h * CW:(h + 1) * CW] = _pick(mm[e], qc)
                do_s[:, h * LANES:(h + 1) * LANES] = _pick(hm[e], dof[:, ps]).astype(BF16)
        dkc_s[...] = jnp.zeros_like(dkc_s)
        dv_s[...] = jnp.zeros_like(dv_s)
        vis = _causal_mask(TQ, False)

        def q_block(i, carry):
            q0 = pl.multiple_of(i * TQ, TQ)
            qs = [q_s[pl.ds(q0, TQ), h * CW:(h + 1) * CW] for h in range(NH)]
            dos = [do_s[pl.ds(q0, TQ), h * LANES:(h + 1) * LANES] for h in range(NH)]
            lse_t = lse_ref[pl.ds(q0, TQ), :]
            dd = do_ref[pl.ds(q0, TQ), :] * o_ref[pl.ds(q0, TQ), :]
            lses, ds_ = [], []
            for h in range(NH):
                ps = slice((h // 2) * LANES, (h // 2 + 1) * LANES)
                lses.append(_lane_value(lse_t[:, ps], (h % 2) * MLA_V))
                ds_.append(jnp.sum(_pick(hm[h % 2], dd[:, ps]), axis=1, keepdims=True))

            def tile(k0, c, mask):
                accs = list(c)
                kcs = [kc_s[pl.ds(k0, TQ), pr * CW:(pr + 1) * CW] for pr in range(PP)]
                vjs = [v_ref[pl.ds(k0, TQ), pr * LANES:(pr + 1) * LANES] for pr in range(PP)]
                ss = [_dot_nt(qs[h], kcs[h // 2]) * scale for h in range(NH)]
                dps = [_dot_nt(dos[h], vjs[h // 2]) for h in range(NH)]
                p_l, ds_l = [], []
                for h in range(NH):
                    pr_ = jnp.exp(ss[h] - lses[h])
                    if mask is not None:
                        pr_ = jnp.where(mask, pr_, 0.0)
                    p_l.append(pr_.astype(BF16))
                    ds_l.append((pr_ * (dps[h] - ds_[h]) * scale).astype(BF16))
                for h in range(NH):
                    accs[h] = accs[h] + _dot(ds_l[h], kcs[h // 2])
                for pr in range(PP):
                    ha, hb = 2 * pr, 2 * pr + 1
                    dkc_s[pl.ds(k0, TQ), pr * CW:(pr + 1) * CW] += _dot_tn(ds_l[ha], qs[ha]) + _dot_tn(ds_l[hb], qs[hb])
                    dv_s[pl.ds(k0, TQ), pr * LANES:(pr + 1) * LANES] += _dot_tn(p_l[ha], dos[ha]) + _dot_tn(p_l[hb], dos[hb])
                return tuple(accs)

            zc = jnp.zeros((TQ, CW), F32)

            def k_block(j, c):
                return tile(pl.multiple_of(j * TQ, TQ), c, None)

            c = lax.fori_loop(0, i, k_block, (zc,) * NH)
            c = tile(q0, c, vis)
            for pr in range(PP):
                ps = slice(pr * LANES, (pr + 1) * LANES)
                dq = _pick(mm[0], c[2 * pr]) + _pick(mm[1], c[2 * pr + 1])
                dqn_ref[pl.ds(q0, TQ), ps] = dq[:, :LANES].astype(BF16)
                dqr_ref[pl.ds(q0, TQ), ps] = dq[:, LANES:]
            return carry

        lax.fori_loop(0, nq, q_block, 0)
        dkr = dkc_s[:, LANES:CW]
        for pr in range(PP):
            dkn_ref[:, pr * LANES:(pr + 1) * LANES] = dkc_s[:, pr * CW:pr * CW + LANES].astype(BF16)
            if pr > 0:
                dkr = dkr + dkc_s[:, pr * CW + LANES:(pr + 1) * CW]
        dv_ref[...] = dv_s[...].astype(BF16)
        g = pl.program_id(1)

        @pl.when(g == 0)
        def _():
            dkr_ref[...] = dkr

        @pl.when(g > 0)
        def _():
            dkr_ref[...] += dkr

    blk = lambda off: pl.BlockSpec((seq, W), lambda b, g: (b, off + g))
    out_blk = pl.BlockSpec((seq, W), lambda b, g: (b, g))
    one_blk = pl.BlockSpec((seq, LANES), lambda b, g: (b, 0))
    return pl.pallas_call(
        body, name=name, grid=(B, nstep),
        in_specs=[blk(0), blk(nstep), blk(0), blk(nstep), one_blk, out_blk, out_blk, out_blk],
        out_specs=[out_blk, out_blk, out_blk, out_blk, one_blk],
        out_shape=[jax.ShapeDtypeStruct((T, MLA_W), BF16), jax.ShapeDtypeStruct((T, MLA_W), F32),
                   jax.ShapeDtypeStruct((T, MLA_W), BF16), jax.ShapeDtypeStruct((T, MLA_W), BF16),
                   jax.ShapeDtypeStruct((T, LANES), F32)],
        scratch_shapes=[pltpu.VMEM((seq, NH * CW), BF16), pltpu.VMEM((seq, PP * CW), BF16), pltpu.VMEM((seq, NH * LANES), BF16),
                        pltpu.VMEM((seq, PP * CW), F32), pltpu.VMEM((seq, W), F32)],
        compiler_params=_cparams(("parallel", "arbitrary")),
    )(qm, qm, kvm, kvm, krt, o, lse, do)


def _rope_tables(pos_ref, invf_ref):
    ang = pos_ref[...].astype(F32) * invf_ref[...]
    first = (_lane_iota() % MLA_ROPE) < (MLA_ROPE // 2)
    return jnp.cos(ang), jnp.sin(ang), first


def _rope_apply(x, cos, sin, first):
    rot = jnp.where(first, -pltpu.roll(x, LANES - MLA_ROPE // 2, 1), pltpu.roll(x, MLA_ROPE // 2, 1))
    return x * cos + rot * sin


def _rope_apply_t(dy, cos, sin, first):
    dys = dy * sin
    rot_t = jnp.where(first, pltpu.roll(dys, LANES - MLA_ROPE // 2, 1), -pltpu.roll(dys, MLA_ROPE // 2, 1))
    return dy * cos + rot_t


def _proj_uq_rope(p, g, wuq, pos, invf, *, tm, name):
    T = p.shape[0]
    ntile = MLA_W // LANES

    def body(x_ref, kr_ref, g_ref, w_ref, pos_ref, invf_ref, cq_ref, qm_ref, krt_ref):
        cos, sin, first = _rope_tables(pos_ref, invf_ref)
        hb = _rms(x_ref[...], g_ref[...]).astype(BF16)
        cq_ref[...] = hb
        q = jnp.dot(hb, w_ref[...], preferred_element_type=F32)
        qm_ref[:, :MLA_W] = q[:, :MLA_W].astype(BF16)
        for t in range(ntile):
            sl = slice(MLA_W + t * LANES, MLA_W + (t + 1) * LANES)
            qm_ref[:, sl] = _rope_apply(q[:, sl], cos, sin, first).astype(BF16)
        krt_ref[...] = _rope_apply(kr_ref[...], cos, sin, first).astype(BF16)

    return pl.pallas_call(
        body, name=name, grid=(T // tm,),
        in_specs=[pl.BlockSpec((tm, Q_LORA), lambda i: (i, P_CQ // Q_LORA)), pl.BlockSpec((tm, LANES), lambda i: (i, P_KRT // LANES)),
                  pl.BlockSpec((1, Q_LORA), lambda i: (0, 0)), pl.BlockSpec((Q_LORA, 2 * MLA_W), lambda i: (0, 0)),
                  pl.BlockSpec((tm, 1), lambda i: (i, 0)), pl.BlockSpec((1, LANES), lambda i: (0, 0))],
        out_specs=[pl.BlockSpec((tm, Q_LORA), lambda i: (i, 0)), pl.BlockSpec((tm, 2 * MLA_W), lambda i: (i, 0)),
                   pl.BlockSpec((tm, LANES), lambda i: (i, 0))],
        out_shape=[jax.ShapeDtypeStruct((T, Q_LORA), BF16), jax.ShapeDtypeStruct((T, 2 * MLA_W), BF16),
                   jax.ShapeDtypeStruct((T, LANES), BF16)],
        compiler_params=_cparams(("parallel",)),
    )(p, p, g, wuq, pos, invf)


def _d_proj_uq_rope(dqn, dqr, dkr, wuq, cq, p, g, pos, invf, *, tm, name):
    T = dqn.shape[0]
    ntile = MLA_W // LANES

    def body(dqn_ref, dqr_ref, dkr_ref, w_ref, cq_ref, x_ref, g_ref, pos_ref, invf_ref,
             dw_ref, dx_ref, dg_ref, dkr_o_ref, dqm_s):
        cos, sin, first = _rope_tables(pos_ref, invf_ref)
        dqm_s[:, :MLA_W] = dqn_ref[...]
        for t in range(ntile):
            sl = slice(t * LANES, (t + 1) * LANES)
            dqm_s[:, MLA_W + t * LANES:MLA_W + (t + 1) * LANES] = _rope_apply_t(dqr_ref[:, sl], cos, sin, first).astype(BF16)
        dkr_o_ref[...] = _rope_apply_t(dkr_ref[...], cos, sin, first).astype(BF16)
        dqm = dqm_s[...]
        dy = lax.dot_general(dqm, w_ref[...], (((1,), (1,)), ((), ())), preferred_element_type=F32)
        dx, part = _rms_grad(dy, x_ref[...], g_ref[...])
        dx_ref[...] = dx.astype(BF16)
        _accumulate(dg_ref, part)
        _accumulate(dw_ref, _dot_tn(dqm, cq_ref[...]))

    half = pl.BlockSpec((tm, MLA_W), lambda i: (i, 0))
    tile = pl.BlockSpec((tm, LANES), lambda i: (i, 0))
    lat = pl.BlockSpec((tm, Q_LORA), lambda i: (i, 0))
    return pl.pallas_call(
        body, name=name, grid=(T // tm,),
        in_specs=[half, half, tile, pl.BlockSpec((Q_LORA, 2 * MLA_W), lambda i: (0, 0)), lat,
                  pl.BlockSpec((tm, Q_LORA), lambda i: (i, P_CQ // Q_LORA)), pl.BlockSpec((1, Q_LORA), lambda i: (0, 0)),
                  pl.BlockSpec((tm, 1), lambda i: (i, 0)), pl.BlockSpec((1, LANES), lambda i: (0, 0))],
        out_specs=[pl.BlockSpec((2 * MLA_W, Q_LORA), lambda i: (0, 0)), lat,
                   pl.BlockSpec((1, Q_LORA), lambda i: (0, 0)), tile],
        out_shape=[jax.ShapeDtypeStruct((2 * MLA_W, Q_LORA), F32), jax.ShapeDtypeStruct((T, Q_LORA), BF16),
                   jax.ShapeDtypeStruct((1, Q_LORA), F32), jax.ShapeDtypeStruct((T, LANES), BF16)],
        scratch_shapes=[pltpu.VMEM((tm, 2 * MLA_W), BF16)],
        compiler_params=_cparams(("arbitrary",)),
    )(dqn, dqr, dkr, wuq, cq, p, g, pos, invf)


def _d_proj_cat(pieces, b, x, g, *, tm, name, residual=None, col_block=0, out_dtype=F32, b_transposed=False,
                wgrad_act=None):
    M = pieces[0].shape[0]
    widths = [pc.shape[1] for pc in pieces]
    K = sum(widths)
    C = b.shape[1] if b_transposed else b.shape[0]
    n = len(pieces)
    fuse_w = wgrad_act is not None
    contract = (((1,), (0,)), ((), ())) if b_transposed else (((1,), (1,)), ((), ()))
    in_specs = [pl.BlockSpec((tm, w), lambda i: (i, 0)) for w in widths]
    in_specs += [pl.BlockSpec(b.shape, lambda i: (0, 0)), pl.BlockSpec((tm, C), lambda i: (i, col_block)),
                 pl.BlockSpec((1, C), lambda i: (0, 0))]
    args = list(pieces) + [b, x, g]
    if residual is not None:
        in_specs.append(pl.BlockSpec((tm, C), lambda i: (i, 0)))
        args.append(residual)
    if fuse_w:
        in_specs.append(pl.BlockSpec((tm, C), lambda i: (i, 0)))
        args.append(wgrad_act)

    def body(*refs):
        b_ref, x_ref, g_ref = refs[n:n + 3]
        first_ref, dx_ref, dg_ref = refs[-4:-1] if fuse_w else refs[-3:]
        cat_ref = refs[-1] if fuse_w else first_ref
        off = 0
        for r, w in zip(refs[:n], widths):
            cat_ref[:, off:off + w] = r[...]
            off += w
        cat = cat_ref[...]
        dy = lax.dot_general(cat, b_ref[...], contract, preferred_element_type=F32)
        dx, part = _rms_grad(dy, x_ref[...], g_ref[...])
        if residual is not None:
            dx = dx + refs[n + 3][...]
        dx_ref[...] = dx.astype(out_dtype)
        _accumulate(dg_ref, part)
        if fuse_w:
            act_ref = refs[n + 3 + (residual is not None)]
            _accumulate(first_ref, _dot_tn(act_ref[...], cat))

    first_spec = pl.BlockSpec((C, K), lambda i: (0, 0)) if fuse_w else pl.BlockSpec((tm, K), lambda i: (i, 0))
    first_shape = jax.ShapeDtypeStruct((C, K), F32) if fuse_w else jax.ShapeDtypeStruct((M, K), BF16)
    return pl.pallas_call(
        body, name=name, grid=(M // tm,), in_specs=in_specs,
        out_specs=[first_spec, pl.BlockSpec((tm, C), lambda i: (i, 0)), pl.BlockSpec((1, C), lambda i: (0, 0))],
        out_shape=[first_shape, jax.ShapeDtypeStruct((M, C), out_dtype), jax.ShapeDtypeStruct((1, C), F32)],
        scratch_shapes=[pltpu.VMEM((tm, K), BF16)] if fuse_w else [],
        compiler_params=_cparams(("arbitrary",)),
    )(*args)


def _heads_out(xa, xb, ga, gb, w, resid, *, tm, name):
    T, C = xa.shape
    N = w.shape[1]

    def body(xa_ref, xb_ref, ga_ref, gb_ref, w_ref, r_ref, oc_ref, o_ref):
        oc_ref[:, :C] = _rms(xa_ref[...], ga_ref[...]).astype(BF16)
        oc_ref[:, C:] = _rms(xb_ref[...], gb_ref[...]).astype(BF16)
        o_ref[...] = r_ref[...] + jnp.dot(oc_ref[...], w_ref[...], preferred_element_type=F32)

    row = pl.BlockSpec((tm, C), lambda i: (i, 0))
    gsp = pl.BlockSpec((1, C), lambda i: (0, 0))
    full = pl.BlockSpec((tm, N), lambda i: (i, 0))
    return pl.pallas_call(
        body, name=name, grid=(T // tm,),
        in_specs=[row, row, gsp, gsp, pl.BlockSpec((2 * C, N), lambda i: (0, 0)), full],
        out_specs=[pl.BlockSpec((tm, 2 * C), lambda i: (i, 0)), full],
        out_shape=[jax.ShapeDtypeStruct((T, 2 * C), BF16), jax.ShapeDtypeStruct((T, N), F32)],
        compiler_params=_cparams(("parallel",)),
    )(xa, xb, ga, gb, w, resid)


def _heads_out_bwd(dout, w, ocat, xa, xb, ga, gb, *, tm, name):
    T, C = xa.shape
    N = w.shape[1]
    n_steps = T // tm

    def body(d_ref, w_ref, oc_ref, xa_ref, xb_ref, ga_ref, gb_ref, dxa_ref, dxb_ref, dga_ref, dgb_ref, dw_ref, dw_s):
        db = d_ref[...].astype(BF16)
        dy = lax.dot_general(db, w_ref[...], (((1,), (1,)), ((), ())), preferred_element_type=F32)
        dxa, pa = _rms_grad(dy[:, :C], xa_ref[...], ga_ref[...])
        dxb, pb = _rms_grad(dy[:, C:], xb_ref[...], gb_ref[...])
        dxa_ref[...] = dxa
        dxb_ref[...] = dxb
        _accumulate(dga_ref, pa)
        _accumulate(dgb_ref, pb)
        _accumulate(dw_s, _dot_tn(oc_ref[...], db))

        @pl.when(pl.program_id(0) == n_steps - 1)
        def _():
            dw_ref[...] = dw_s[...].astype(BF16)

    row = pl.BlockSpec((tm, C), lambda i: (i, 0))
    gsp = pl.BlockSpec((1, C), lambda i: (0, 0))
    wsp = pl.BlockSpec((2 * C, N), lambda i: (0, 0))
    return pl.pallas_call(
        body, name=name, grid=(n_steps,),
        in_specs=[pl.BlockSpec((tm, N), lambda i: (i, 0)), wsp, pl.BlockSpec((tm, 2 * C), lambda i: (i, 0)), row, row, gsp, gsp],
        out_specs=[row, row, gsp, gsp, wsp],
        out_shape=[jax.ShapeDtypeStruct((T, C), F32), jax.ShapeDtypeStruct((T, C), F32),
                   jax.ShapeDtypeStruct((1, C), F32), jax.ShapeDtypeStruct((1, C), F32),
                   jax.ShapeDtypeStruct((2 * C, N), BF16)],
        scratch_shapes=[pltpu.VMEM((2 * C, N), F32)],
        compiler_params=_cparams(("arbitrary",)),
    )(dout, w, ocat, xa, xb, ga, gb)


CONV_ROWS = 256
HALO = 8


def _conv_taps(w_ref):
    return w_ref[0:1, :], w_ref[1:2, :], w_ref[2:3, :]


def _conv_rows(cur, prev, w, bias):
    ext = jnp.concatenate([prev, cur], axis=0)
    u1 = pltpu.roll(ext, 1, 0)[HALO:]
    u2 = pltpu.roll(ext, 2, 0)[HALO:]
    return w[2] * cur + w[1] * u1 + w[0] * u2 + bias, u1, u2


def _conv_fwd(u, w, bias, *, seq, name):
    T = u.shape[0]
    B = T // seq
    W2 = 2 * FF_BLK

    def body(u_ref, w_ref, b_ref, a_ref):
        wv = _conv_taps(w_ref)
        bv = b_ref[...]
        for c in range(seq // CONV_ROWS):
            r0 = c * CONV_ROWS
            cur = u_ref[r0:r0 + CONV_ROWS, :]
            prev = u_ref[r0 - HALO:r0, :] if c > 0 else jnp.zeros((HALO, W2), F32)
            y, _, _ = _conv_rows(cur, prev, wv, bv)
            gc = y[:, :FF_BLK]
            a_ref[r0:r0 + CONV_ROWS, :] = (gc * (1.0 / (1.0 + jnp.exp(-gc))) * y[:, FF_BLK:]).astype(BF16)

    return pl.pallas_call(
        body, name=name, grid=(B, N_FF_BLK),
        in_specs=[pl.BlockSpec((seq, W2), lambda b, j: (b, j)), pl.BlockSpec((3, W2), lambda b, j: (0, j)),
                  pl.BlockSpec((1, W2), lambda b, j: (0, j))],
        out_specs=pl.BlockSpec((seq, FF_BLK), lambda b, j: (b, j)),
        out_shape=jax.ShapeDtypeStruct((T, D_FF), BF16),
        compiler_params=_cparams(("parallel", "parallel")),
    )(u, w, bias)


def _conv_bwd(u, dx2, wdn, w, bias, *, seq, name):
    T = u.shape[0]
    B = T // seq
    D = dx2.shape[1]
    W2 = 2 * FF_BLK
    nchunk = seq // CONV_ROWS

    def body(u_ref, dx_ref, wd_ref, w_ref, b_ref, du_ref, dw_ref, db_ref, dwd_ref, duc_s, dwd_s):
        wv = _conv_taps(w_ref)
        bv = b_ref[...]
        wd = wd_ref[...]
        zrow = jnp.zeros((1, W2), F32)
        dw0, dw1, dw2, dbs = zrow, zrow, zrow, zrow
        dwd = jnp.zeros((FF_BLK, D), F32)
        for c in range(nchunk):
            r0 = c * CONV_ROWS
            cur = u_ref[r0:r0 + CONV_ROWS, :]
            prev = u_ref[r0 - HALO:r0, :] if c > 0 else jnp.zeros((HALO, W2), F32)
            y, u1, u2 = _conv_rows(cur, prev, wv, bv)
            gc = y[:, :FF_BLK]
            vc = y[:, FF_BLK:]
            sg = 1.0 / (1.0 + jnp.exp(-gc))
            dxc = dx_ref[r0:r0 + CONV_ROWS, :].astype(BF16)
            dav = _dot_nt(dxc, wd)
            silu = gc * sg
            dwd = dwd + _dot_tn((silu * vc).astype(BF16), dxc)
            duc = jnp.concatenate([dav * vc * (sg * (1.0 + gc * (1.0 - sg))), dav * silu], axis=1)
            duc_s[r0:r0 + CONV_ROWS, :] = duc
            dw0 = dw0 + jnp.sum(duc * u2, axis=0, keepdims=True)
            dw1 = dw1 + jnp.sum(duc * u1, axis=0, keepdims=True)
            dw2 = dw2 + jnp.sum(duc * cur, axis=0, keepdims=True)
            dbs = dbs + jnp.sum(duc, axis=0, keepdims=True)
        duc_s[seq:seq + HALO, :] = jnp.zeros((HALO, W2), F32)
        n_ext = CONV_ROWS + HALO
        for c in range(nchunk):
            r0 = c * CONV_ROWS
            ext = duc_s[r0:r0 + n_ext, :]
            s1 = pltpu.roll(ext, n_ext - 1, 0)[:CONV_ROWS]
            s2 = pltpu.roll(ext, n_ext - 2, 0)[:CONV_ROWS]
            du_ref[r0:r0 + CONV_ROWS, :] = (wv[2] * ext[:CONV_ROWS] + wv[1] * s1 + wv[0] * s2).astype(BF16)

        b = pl.program_id(1)

        @pl.when(b == 0)
        def _():
            dw_ref[0:1, :] = dw0
            dw_ref[1:2, :] = dw1
            dw_ref[2:3, :] = dw2
            db_ref[...] = dbs
            dwd_s[...] = dwd

        @pl.when(b > 0)
        def _():
            dw_ref[0:1, :] += dw0
            dw_ref[1:2, :] += dw1
            dw_ref[2:3, :] += dw2
            db_ref[...] += dbs
            dwd_s[...] += dwd

        @pl.when(b == B - 1)
        def _():
            dwd_ref[...] = dwd_s[...].astype(BF16)

    return pl.pallas_call(
        body, name=name, grid=(N_FF_BLK, B),
        in_specs=[pl.BlockSpec((seq, W2), lambda j, b: (b, j)), pl.BlockSpec((seq, D), lambda j, b: (b, 0)),
                  pl.BlockSpec((FF_BLK, D), lambda j, b: (j, 0)),
                  pl.BlockSpec((3, W2), lambda j, b: (0, j)), pl.BlockSpec((1, W2), lambda j, b: (0, j))],
        out_specs=[pl.BlockSpec((seq, W2), lambda j, b: (b, j)), pl.BlockSpec((3, W2), lambda j, b: (0, j)),
                   pl.BlockSpec((1, W2), lambda j, b: (0, j)), pl.BlockSpec((FF_BLK, D), lambda j, b: (j, 0))],
        out_shape=[jax.ShapeDtypeStruct((T, 2 * D_FF), BF16), jax.ShapeDtypeStruct((3, 2 * D_FF), F32),
                   jax.ShapeDtypeStruct((1, 2 * D_FF), F32), jax.ShapeDtypeStruct((D_FF, D), BF16)],
        scratch_shapes=[pltpu.VMEM((seq + HALO, W2), F32), pltpu.VMEM((FF_BLK, D), F32)],
        compiler_params=_cparams(("parallel", "arbitrary")),
    )(u, dx2, wdn, w, bias)


def _place():
    return lax.axis_index("x"), lax.axis_index("y"), lax.axis_index("c")


def _other_chips(x, y):
    return [(1 - x, y), (x, 1 - y), (1 - x, 1 - y)]


def _all_gather(vs, *, name):
    n = len(vs)

    def body(*refs):
        v_refs, out_refs = refs[:n], refs[n:2 * n]
        send_sems, recv_sems, local_sems = refs[2 * n:]
        x, y, c = _place()
        me, sibling = (x, y, c), (x, y, 1 - c)
        chips = _other_chips(x, y)

        def slab(a, px, py, pc):
            return out_refs[a].at[4 * px + 2 * py + pc]

        def copy(a, k, block, to, src=None):
            return pltpu.make_async_remote_copy(
                src_ref=slab(a, *block) if src is None else src, dst_ref=slab(a, *block),
                send_sem=send_sems.at[7 * a + k], recv_sem=recv_sems.at[7 * a + k], device_id=to, device_id_type=MESH)

        mine = [pltpu.make_async_copy(v_refs[a], slab(a, *me), local_sems.at[a]) for a in range(n)]
        for cp in mine:
            cp.start()
        first = []
        for a in range(n):
            first.append(copy(a, 0, me, sibling, src=v_refs[a]))
            first += [copy(a, 1 + j, me, (*chip, c), src=v_refs[a]) for j, chip in enumerate(chips)]
        for cp in first:
            cp.start()
        passed = []
        for j, chip in enumerate(chips):
            for a in range(n):
                copy(a, 1 + j, (*chip, c), me).wait_recv()
                cp = copy(a, 4 + j, (*chip, c), sibling)
                cp.start()
                passed.append(cp)
        for a in range(n):
            copy(a, 0, sibling, me).wait_recv()
            for j, chip in enumerate(chips):
                copy(a, 4 + j, (*chip, 1 - c), me).wait_recv()
        for cp in first + passed:
            cp.wait_send()
        for cp in mine:
            cp.wait()

    return pl.pallas_call(
        body, name=name, in_specs=[ANY] * n, out_specs=[ANY] * n,
        out_shape=[jax.ShapeDtypeStruct((N_DEV,) + v.shape, v.dtype) for v in vs],
        scratch_shapes=[pltpu.SemaphoreType.DMA((7 * n,)), pltpu.SemaphoreType.DMA((7 * n,)), pltpu.SemaphoreType.DMA((n,))],
    )(*vs)


def _all_gather_async(vs, *, name, collective_id):
    n = len(vs)
    v_refs = [jax.new_ref(v, memory_space=pltpu.MemorySpace.HBM) for v in vs]
    out_refs = [jax.empty_ref(jax.ShapeDtypeStruct((N_DEV,) + v.shape, v.dtype), memory_space=pltpu.MemorySpace.HBM)
                for v in vs]

    @pl.kernel(mesh=plsc.ScalarSubcoreMesh(axis_name="seq", num_cores=1), name=name,
               scratch_types=(pltpu.SemaphoreType.DMA((7 * n,)), pltpu.SemaphoreType.DMA((7 * n,)),
                              pltpu.SemaphoreType.DMA((n,))),
               compiler_params=pltpu.CompilerParams(collective_id=collective_id))
    def launch(send_sems, recv_sems, local_sems):
        x, y, c = _place()
        me, sibling = (x, y, c), (x, y, 1 - c)
        chips = _other_chips(x, y)
        peers = [sibling] + [(*chip, c) for chip in chips]
        barrier = pltpu.get_barrier_semaphore()
        for peer in peers:
            pl.semaphore_signal(barrier, inc=1, device_id=peer, device_id_type=MESH)
        pl.semaphore_wait(barrier, len(peers))

        def slab(a, px, py, pc):
            return out_refs[a].at[4 * px + 2 * py + pc]

        def copy(a, k, block, to, src=None):
            return pltpu.make_async_remote_copy(
                src_ref=slab(a, *block) if src is None else src, dst_ref=slab(a, *block),
                send_sem=send_sems.at[7 * a + k], recv_sem=recv_sems.at[7 * a + k], device_id=to, device_id_type=MESH)

        mine = [pltpu.make_async_copy(v_refs[a], slab(a, *me), local_sems.at[a]) for a in range(n)]
        for cp in mine:
            cp.start()
        first = []
        for a in range(n):
            first.append(copy(a, 0, me, sibling, src=v_refs[a]))
            first += [copy(a, 1 + j, me, (*chip, c), src=v_refs[a]) for j, chip in enumerate(chips)]
        for cp in first:
            cp.start()
        passed = []
        for j, chip in enumerate(chips):
            for a in range(n):
                copy(a, 1 + j, (*chip, c), me).wait_recv()
                cp = copy(a, 4 + j, (*chip, c), sibling)
                cp.start()
                passed.append(cp)
        for a in range(n):
            copy(a, 0, sibling, me).wait_recv()
            for j, chip in enumerate(chips):
                copy(a, 4 + j, (*chip, 1 - c), me).wait_recv()
        for cp in first + passed:
            cp.wait_send()
        for cp in mine:
            cp.wait()

    launch()
    return [r[...] for r in out_refs]


def _handshake(peers):
    barrier = pltpu.get_barrier_semaphore()
    for peer in peers:
        pl.semaphore_signal(barrier, inc=1, device_id=peer, device_id_type=MESH)
    pl.semaphore_wait(barrier, len(peers))


def _hbm_refs(arrays, lead):
    src = [jax.new_ref(a, memory_space=pltpu.MemorySpace.HBM) for a in arrays]
    dst = [jax.empty_ref(jax.ShapeDtypeStruct((lead,) + a.shape[1:], a.dtype), memory_space=pltpu.MemorySpace.HBM)
           for a in arrays]
    return src, dst


def _rs_sibling_async(g8s, *, name, collective_id):
    n = len(g8s)
    g_refs, out_refs = _hbm_refs(g8s, 4)

    @pl.kernel(mesh=plsc.ScalarSubcoreMesh(axis_name="seq", num_cores=1), name=name,
               scratch_types=(pltpu.SemaphoreType.DMA((4 * n,)), pltpu.SemaphoreType.DMA((4 * n,))),
               compiler_params=pltpu.CompilerParams(collective_id=collective_id))
    def launch(send_sems, recv_sems):
        x, y, c = _place()
        _handshake([(x, y, 1 - c)])
        copies = [
            pltpu.make_async_remote_copy(
                src_ref=g_refs[a].at[2 * k + 1 - c], dst_ref=out_refs[a].at[k],
                send_sem=send_sems.at[4 * a + k], recv_sem=recv_sems.at[4 * a + k],
                device_id=(x, y, 1 - c), device_id_type=MESH)
            for a in range(n) for k in range(4)]
        for cp in copies:
            cp.start()
        for cp in copies:
            cp.wait()

    launch()
    return [r[...] for r in out_refs]


def _rs_chips_async(h4s, *, name, collective_id):
    n = len(h4s)
    h_refs, out_refs = _hbm_refs(h4s, 3)

    @pl.kernel(mesh=plsc.ScalarSubcoreMesh(axis_name="seq", num_cores=1), name=name,
               scratch_types=(pltpu.SemaphoreType.DMA((3 * n,)), pltpu.SemaphoreType.DMA((3 * n,))),
               compiler_params=pltpu.CompilerParams(collective_id=collective_id))
    def launch(send_sems, recv_sems):
        x, y, c = _place()
        chips = _other_chips(x, y)
        _handshake([(cx, cy, c) for cx, cy in chips])
        copies = [
            pltpu.make_async_remote_copy(
                src_ref=h_refs[a].at[2 * cx + cy], dst_ref=out_refs[a].at[j],
                send_sem=send_sems.at[3 * a + j], recv_sem=recv_sems.at[3 * a + j],
                device_id=(cx, cy, c), device_id_type=MESH)
            for a in range(n) for j, (cx, cy) in enumerate(chips)]
        for cp in copies:
            cp.start()
        for cp in copies:
            cp.wait()

    launch()
    return [r[...] for r in out_refs]


def _peer(x, y, c, k):
    return ((1 - x) if k & 4 else x, (1 - y) if k & 2 else y, (1 - c) if k & 1 else c)


def _rs_direct_async(g8s, *, name, collective_id):
    n = len(g8s)
    g_refs, out_refs = _hbm_refs(g8s, N_DEV - 1)

    @pl.kernel(mesh=plsc.ScalarSubcoreMesh(axis_name="seq", num_cores=1), name=name,
               scratch_types=(pltpu.SemaphoreType.DMA((7 * n,)), pltpu.SemaphoreType.DMA((7 * n,))),
               compiler_params=pltpu.CompilerParams(collective_id=collective_id))
    def launch(send_sems, recv_sems):
        x, y, c = _place()
        peers = [_peer(x, y, c, k) for k in range(1, N_DEV)]
        _handshake(peers)
        copies = [
            pltpu.make_async_remote_copy(
                src_ref=g_refs[a].at[4 * px + 2 * py + pc], dst_ref=out_refs[a].at[k],
                send_sem=send_sems.at[7 * a + k], recv_sem=recv_sems.at[7 * a + k],
                device_id=(px, py, pc), device_id_type=MESH)
            for a in range(n) for k, (px, py, pc) in enumerate(peers)]
        for cp in copies:
            cp.start()
        for cp in copies:
            cp.wait()

    launch()
    return [r[...] for r in out_refs]


def _row_tile(rows):
    if rows <= 512:
        return rows
    return next(t for t in (512, 384, 352, 256, 128) if rows % t == 0)


def _rs_chip_sum(g8, from_sibling, place_idx, *, name):
    _, R, C = g8.shape
    tr = _row_tile(R)

    def body(pi_ref, a_ref, b_ref, f_ref, h_ref):
        s = a_ref[...] + b_ref[...]
        h_ref[...] = s.astype(BF16)

        @pl.when(pl.program_id(1) == pi_ref[1])
        def _():
            f_ref[...] = s

    blk = pl.BlockSpec((None, tr, C), lambda r, k, pi_ref: (k, r, 0))
    return pl.pallas_call(
        body, name=name,
        grid_spec=pltpu.PrefetchScalarGridSpec(
            num_scalar_prefetch=1, grid=(R // tr, 4),
            in_specs=[pl.BlockSpec((None, tr, C), lambda r, k, pi_ref: (2 * k + pi_ref[0], r, 0)), blk],
            out_specs=[pl.BlockSpec((tr, C), lambda r, k, pi_ref: (r, 0)), blk]),
        out_shape=[jax.ShapeDtypeStruct((R, C), F32), jax.ShapeDtypeStruct((4, R, C), BF16)],
        compiler_params=_cparams(("parallel", "arbitrary")),
    )(place_idx, g8, from_sibling)


def _split_moves(segments, chunk):
    moves = []
    for dst, src, length in segments:
        while length > 0:
            dev, off = divmod(src, chunk)
            take = min(length, chunk - off)
            moves.append((dst, dev, off, take))
            dst, src, length = dst + take, src + take, length - take
    return moves


def _assemble(stacked, segments, zero_spans, out_cols, *, name):
    _, R, c = stacked.shape
    tr = _row_tile(R)
    moves = _split_moves(segments, c)

    def body(x_ref, o_ref):
        for dst, dev, off, take in moves:
            o_ref[:, dst:dst + take] = x_ref[dev, :, off:off + take]
        for a, b in zero_spans:
            o_ref[:, a:b] = jnp.zeros((tr, b - a), o_ref.dtype)

    return pl.pallas_call(
        body, name=name, grid=(R // tr,),
        in_specs=[pl.BlockSpec((N_DEV, tr, c), lambda i: (0, i, 0))],
        out_specs=pl.BlockSpec((tr, out_cols), lambda i: (i, 0)),
        out_shape=jax.ShapeDtypeStruct((R, out_cols), stacked.dtype),
        compiler_params=_cparams(("parallel",)),
    )(stacked)


def _disassemble(full, segments, chunk, *, name, out_dtype=F32):
    R = full.shape[0]
    tr = _row_tile(R)
    moves = _split_moves(segments, chunk)

    def body(x_ref, o_ref):
        seen = set()
        for dst, dev, off, take in moves:
            piece = x_ref[:, dst:dst + take]
            if (dev, off) in seen:
                piece = piece + o_ref[dev, :, off:off + take]
            seen.add((dev, off))
            o_ref[dev, :, off:off + take] = piece.astype(out_dtype)

    return pl.pallas_call(
        body, name=name, grid=(R // tr,),
        in_specs=[pl.BlockSpec((tr, full.shape[1]), lambda i: (i, 0))],
        out_specs=pl.BlockSpec((N_DEV, tr, chunk), lambda i: (0, i, 0)),
        out_shape=jax.ShapeDtypeStruct((N_DEV, R, chunk), out_dtype),
        compiler_params=_cparams(("parallel",)),
    )(full)


def _assemble_rows(stacked, segments, zero_spans, out_rows, *, name):
    _, c, R = stacked.shape
    tc = next(t for t in (2 * LANES, LANES) if R % t == 0)
    moves = _split_moves(segments, c)

    def body(x_ref, o_ref):
        for dst, dev, off, take in moves:
            o_ref[dst:dst + take, :] = x_ref[dev, off:off + take, :]
        for a, b in zero_spans:
            o_ref[a:b, :] = jnp.zeros((b - a, tc), o_ref.dtype)

    return pl.pallas_call(
        body, name=name, grid=(R // tc,),
        in_specs=[pl.BlockSpec((N_DEV, c, tc), lambda i: (0, 0, i))],
        out_specs=pl.BlockSpec((out_rows, tc), lambda i: (0, i)),
        out_shape=jax.ShapeDtypeStruct((out_rows, R), stacked.dtype),
        compiler_params=_cparams(("parallel",)),
    )(stacked)


def _disassemble_rows(full_t, segments, chunk, *, name, out_dtype=F32):
    R = full_t.shape[1]
    tc = next(t for t in (2 * LANES, LANES) if R % t == 0)
    moves = _split_moves(segments, chunk)

    def body(x_ref, o_ref):
        seen = set()
        for dst, dev, off, take in moves:
            piece = x_ref[dst:dst + take, :]
            if (dev, off) in seen:
                piece = piece + o_ref[dev, off:off + take, :]
            seen.add((dev, off))
            o_ref[dev, off:off + take, :] = piece.astype(out_dtype)

    return pl.pallas_call(
        body, name=name, grid=(R // tc,),
        in_specs=[pl.BlockSpec((full_t.shape[0], tc), lambda i: (0, i))],
        out_specs=pl.BlockSpec((N_DEV, chunk, tc), lambda i: (0, 0, i)),
        out_shape=jax.ShapeDtypeStruct((N_DEV, chunk, R), out_dtype),
        compiler_params=_cparams(("parallel",)),
    )(full_t)


_O_CQ = 3 * SB_W
_O_CKV = _O_CQ + Q_LORA
_O_KR = _O_CKV + KV_LORA
SEG_W_IN = ((0, 0, 3 * SB_W), (P_CKV, _O_CKV, KV_LORA), (P_KRT, _O_KR, MLA_ROPE), (P_KRT + MLA_ROPE, _O_KR, MLA_ROPE),
            (P_CQ, _O_CQ, Q_LORA))
ZERO_W_IN = ((P_KRT + 2 * MLA_ROPE, P_CQ),)
SEG_W_UQ = tuple((MLA_NOPE * h, MLA_QK * h, MLA_NOPE) for h in range(MLA_HEADS)) + tuple(
    (MLA_W + LANES * (h // 2) + MLA_ROPE * (h % 2), MLA_QK * h + MLA_NOPE, MLA_ROPE) for h in range(MLA_HEADS))
ZERO_W_UQ = tuple((MLA_W + LANES * g + 2 * MLA_ROPE, MLA_W + LANES * (g + 1)) for g in range(MLA_HEADS // 2))
SEG_W_UKV = tuple((MLA_NOPE * h, (MLA_NOPE + MLA_V) * h, MLA_NOPE) for h in range(MLA_HEADS)) + tuple(
    (MLA_W + MLA_V * h, (MLA_NOPE + MLA_V) * h + MLA_NOPE, MLA_V) for h in range(MLA_HEADS))
SEG_W_UP = tuple((2 * FF_BLK * blk + FF_BLK * half, D_FF * half + FF_BLK * blk, FF_BLK)
                 for half in range(2) for blk in range(N_FF_BLK))


def _sum8(g, *, name):
    _, R, C = g.shape

    def body(g_ref, o_ref):
        acc = g_ref[0]
        for k in range(1, N_DEV):
            acc = acc + g_ref[k]
        o_ref[...] = acc

    return pl.pallas_call(
        body, name=name, out_shape=jax.ShapeDtypeStruct((R, C), F32),
    )(g)


def _adamw_math(w, gf, m, v):
    c1 = 1.0 / (1.0 - ADAM_B1 ** ADAM_STEP)
    c2 = 1.0 / (1.0 - ADAM_B2 ** ADAM_STEP)
    mn = ADAM_B1 * m + (1.0 - ADAM_B1) * gf
    vn = ADAM_B2 * v + (1.0 - ADAM_B2) * (gf * gf)
    return -ADAM_LR * ((mn * c1) / (jnp.sqrt(vn * c2) + ADAM_EPS) + ADAM_WD * w), mn, vn


def _adamw(w, g, m, v, *, name):
    R, C = w.shape
    tr = _row_tile(R)

    def body(w_ref, g_ref, m_ref, v_ref, d_ref, mo_ref, vo_ref):
        d_ref[...], mo_ref[...], vo_ref[...] = _adamw_math(w_ref[...], g_ref[...], m_ref[...], v_ref[...])

    blk = pl.BlockSpec((tr, C), lambda i: (i, 0))
    shp = jax.ShapeDtypeStruct((R, C), F32)
    return pl.pallas_call(
        body, name=name, grid=(R // tr,), in_specs=[blk] * 4, out_specs=[blk] * 3,
        out_shape=[shp, shp, shp], compiler_params=_cparams(("parallel",)),
    )(w, g, m, v)


def _adamw_rs8(g8, r7, me_idx, w, m, v, *, name):
    R, C = w.shape
    tr = _row_tile(R)

    def body(i_ref, f_ref, r_ref, w_ref, m_ref, v_ref, g_ref, d_ref, mo_ref, vo_ref):
        gf = f_ref[...].astype(F32)
        for k in range(N_DEV - 1):
            gf = gf + r_ref[k].astype(F32)
        g_ref[...] = gf
        d_ref[...], mo_ref[...], vo_ref[...] = _adamw_math(w_ref[...], gf, m_ref[...], v_ref[...])

    blk = pl.BlockSpec((tr, C), lambda i, i_ref: (i, 0))
    shp = jax.ShapeDtypeStruct((R, C), F32)
    return pl.pallas_call(
        body, name=name,
        grid_spec=pltpu.PrefetchScalarGridSpec(
            num_scalar_prefetch=1, grid=(R // tr,),
            in_specs=[pl.BlockSpec((None, tr, C), lambda i, i_ref: (i_ref[0], i, 0)),
                      pl.BlockSpec((N_DEV - 1, tr, C), lambda i, i_ref: (0, i, 0)), blk, blk, blk],
            out_specs=[blk] * 4),
        out_shape=[shp] * 4, compiler_params=_cparams(("parallel",)),
    )(me_idx, g8, r7, w, m, v)


def _adamw_rs(own, r3, w, m, v, *, name):
    R, C = w.shape
    tr = _row_tile(R)

    def body(f_ref, r_ref, w_ref, m_ref, v_ref, g_ref, d_ref, mo_ref, vo_ref):
        gf = ((f_ref[...] + r_ref[0].astype(F32)) + r_ref[1].astype(F32)) + r_ref[2].astype(F32)
        g_ref[...] = gf
        d_ref[...], mo_ref[...], vo_ref[...] = _adamw_math(w_ref[...], gf, m_ref[...], v_ref[...])

    blk = pl.BlockSpec((tr, C), lambda i: (i, 0))
    shp = jax.ShapeDtypeStruct((R, C), F32)
    return pl.pallas_call(
        body, name=name, grid=(R // tr,),
        in_specs=[blk, pl.BlockSpec((3, tr, C), lambda i: (0, i, 0)), blk, blk, blk], out_specs=[blk] * 4,
        out_shape=[shp] * 4, compiler_params=_cparams(("parallel",)),
    )(own, r3, w, m, v)


def _ff_interleave(a):
    lead = a.shape[:-1]
    return a.reshape(*lead, 2, N_FF_BLK, FF_BLK).swapaxes(-3, -2).reshape(*lead, 2 * D_FF)


def _ff_deinterleave(a):
    lead = a.shape[:-1]
    return a.reshape(*lead, N_FF_BLK, 2, FF_BLK).swapaxes(-3, -2).reshape(*lead, 2 * D_FF)


SMALL =(("g_mix", D_MODEL), ("g_cq", Q_LORA), ("g_ckv", KV_LORA), ("g_sb_out", SB_W), ("g_mla_out", MLA_W),
         ("g_ffn", D_MODEL), ("conv_b", 2 * D_FF), ("g_final", D_MODEL))
SMALL_ROWS = 88


SMALL_USED = sum(size for _, size in SMALL)


def _pack_small(d, tail=None):
    parts = [d[n].reshape(-1) for n, _ in SMALL] + ([] if tail is None else [tail])
    flat = jnp.concatenate(parts)
    flat = jnp.pad(flat, (0, SMALL_ROWS * LANES - flat.shape[0]))
    return flat.reshape(SMALL_ROWS, LANES)


def _unpack_small(a):
    flat = a.reshape(-1)
    out, off = {}, 0
    for n, size in SMALL:
        out[n] = flat[off:off + size]
        off += size
    return out


def kernel(x, positions, g_mix, w_in, g_cq, w_uq, g_ckv, w_ukv, g_sb_out, g_mla_out, w_out, g_ffn, w_up, conv_w, conv_b, w_down, g_final, loss_target, m_g_mix, m_w_in, m_g_cq, m_w_uq, m_g_ckv, m_w_ukv, m_g_sb_out, m_g_mla_out, m_w_out, m_g_ffn, m_w_up, m_conv_w, m_conv_b, m_w_down, m_g_final, v_g_mix, v_w_in, v_g_cq, v_w_uq, v_g_ckv, v_w_ukv, v_g_sb_out, v_g_mla_out, v_w_out, v_g_ffn, v_w_up, v_conv_w, v_conv_b, v_w_down, v_g_final):
    B, S, D = x.shape
    T = B * S
    xf = x.reshape(T, D)
    tgt = loss_target.reshape(T, D)
    pos = positions.reshape(T, 1)
    half = MLA_ROPE // 2
    inv_freq = 1.0 / (ROPE_BASE ** (jnp.arange(half, dtype=F32) * (2.0 / MLA_ROPE)))
    invf = jnp.tile(inv_freq, LANES // half).reshape(1, LANES)
    place_idx = jnp.stack([lax.axis_index("c"), 2 * lax.axis_index("x") + lax.axis_index("y")]).astype(jnp.int32)
    me_idx = (4 * lax.axis_index("x") + 2 * lax.axis_index("y") + lax.axis_index("c")).astype(jnp.int32).reshape(1)

    names = ("w_in", "w_uq", "w_ukv", "w_out", "w_up", "w_down", "conv_w")
    shard = {"w_in": w_in[0], "w_uq": w_uq[0], "w_ukv": w_ukv[0], "w_out": w_out[0], "w_up": w_up[0],
             "w_down": w_down[0], "conv_w": conv_w[0]}
    sent = {n: shard[n] if n == "conv_w" else shard[n].astype(BF16) for n in names}
    later = names[1:]
    w_in_all = _all_gather([jnp.transpose(shard["w_in"]).astype(BF16)], name="ag_w_in")[0]
    w_in_all, rest = lax.optimization_barrier((w_in_all, [sent[n] for n in later]))
    got = {"w_in": w_in_all}
    got.update(zip(later, _all_gather_async(rest, name="ag_weights_async", collective_id=0)))
    wi_t = _assemble_rows(got["w_in"], SEG_W_IN, ZERO_W_IN, P_COLS, name="asm_w_in")
    wuq = _assemble(got["w_uq"], SEG_W_UQ, ZERO_W_UQ, 2 * MLA_W, name="asm_w_uq")
    wukv = _assemble(got["w_ukv"], SEG_W_UKV, (), 2 * MLA_W, name="asm_w_ukv")
    wup = _assemble(got["w_up"], SEG_W_UP, (), 2 * D_FF, name="asm_w_up")
    cwi = _assemble(got["conv_w"], SEG_W_UP, (), 2 * D_FF, name="asm_conv_w")
    wo = got["w_out"].reshape(D, D)
    wdn = got["w_down"].reshape(D_FF, D)
    cbi = _ff_interleave(conv_b)

    h, p = _rms_matmul_nn(xf, g_mix, wi_t, tm=512, name="proj_in", w_transposed=True)
    o_sb, ltot = _sb_fwd(p, seq=S, name="sb_fwd")
    cq, qm, krt = _proj_uq_rope(p, g_cq, wuq, pos, invf, tm=512, name="proj_uq")
    ckv, kvm = _rms_matmul_nn(p, g_ckv, wukv, tm=512, name="proj_ukv", col_block=P_CKV // KV_LORA, out_dtype=BF16)
    o_mla, lse = _mla_fwd(qm, kvm, krt, seq=S, name="mla_fwd")
    ocat, x1 = _heads_out(o_sb, o_mla, g_sb_out, g_mla_out, wo, xf, tm=512, name="proj_out")
    hf, u = _rms_matmul_nn(x1, g_ffn, wup, tm=256, name="ffn_up")
    a = _conv_fwd(u, cwi, cbi, seq=S, name="conv_fwd")
    dx2, dg_final, loss_row = _matmul_nn_loss(a, wdn, x1, g_final.reshape(1, D), tgt, tm=512, name="ffn_down_loss")

    du, dcw, dcb, dw_down = _conv_bwd(u, dx2, wdn, cwi, cbi, seq=S, name="conv_bwd")
    dw_up_t = _matmul_tn(du, hf, tm=D_FF, tn=1024, tk=1024, name="dw_up")
    dx1, dg_ffn = _matmul_nt_rms_bwd(du, wup, x1, g_ffn, tm=512, name="d_ffn_up", residual=dx2)
    do_sb, do_mla, dg_sb, dg_mla, dw_out = _heads_out_bwd(dx1, wo, ocat, o_sb, o_mla, g_sb_out, g_mla_out, tm=512,
                                                          name="d_proj_out")

    early = ("w_down", "w_up", "conv_w", "w_out")
    g8 = {"w_up": _disassemble_rows(dw_up_t, SEG_W_UP, shard["w_up"].shape[1], name="split_dw_up", out_dtype=BF16),
          "conv_w": _disassemble(dcw, SEG_W_UP, shard["conv_w"].shape[1], name="split_dconv_w", out_dtype=BF16),
          "w_out": dw_out.reshape((N_DEV,) + shard["w_out"].shape),
          "w_down": dw_down.reshape((N_DEV,) + shard["w_down"].shape)}
    r7 = dict(zip(early, _rs_direct_async([g8[n] for n in early], name="rs_direct_async", collective_id=1)))
    own, r3 = {}, {}

    dq_sb, dk_sb, dv_sb = _sb_bwd(p, ltot, do_sb, seq=S, name="sb_bwd")
    dqn, dqr, dkn, dvm, dkr = _mla_bwd(qm, kvm, krt, o_mla, lse, do_mla, seq=S, name="mla_bwd")
    dw_uq_t, dcq, dg_cq, dkr_u = _d_proj_uq_rope(dqn, dqr, dkr, wuq, cq, p, g_cq, pos, invf, tm=512, name="d_proj_uq")
    dw_ukv, dckv, dg_ckv = _d_proj_cat([dkn, dvm], wukv, p, g_ckv, tm=512, name="d_proj_ukv",
                                       col_block=P_CKV // KV_LORA, out_dtype=BF16, wgrad_act=ckv)
    dp, dx, dg_mix = _d_proj_cat([dq_sb, dk_sb, dv_sb, dckv, dkr_u, dcq], wi_t, xf, g_mix, tm=512, name="d_proj_in",
                                 residual=dx1, b_transposed=True)
    dw_in_t = _matmul_tn(dp, h, tm=P_COLS, tn=1024, tk=1024, name="dw_in")

    late = ("w_in", "w_uq", "w_ukv")
    g8.update({"w_in": _disassemble_rows(dw_in_t, SEG_W_IN, shard["w_in"].shape[1], name="split_dw_in"),
               "w_uq": _disassemble_rows(dw_uq_t, SEG_W_UQ, shard["w_uq"].shape[1], name="split_dw_uq"),
               "w_ukv": _disassemble(dw_ukv, SEG_W_UKV, shard["w_ukv"].shape[1], name="split_dw_ukv")})
    sib_l = _rs_sibling_async([g8[n] for n in late], name="rs_sibling_late", collective_id=3)

    params = {"w_in": (w_in, m_w_in, v_w_in), "w_uq": (w_uq, m_w_uq, v_w_uq), "w_ukv": (w_ukv, m_w_ukv, v_w_ukv),
              "w_out": (w_out, m_w_out, v_w_out), "w_up": (w_up, m_w_up, v_w_up), "conv_w": (conv_w, m_conv_w, v_conv_w),
              "w_down": (w_down, m_w_down, v_w_down)}
    grad, delta, new_m, new_v = {}, {}, {}, {}

    transposed = ("w_in", "w_uq", "w_up")

    def adamw_group(group):
        for n in group:
            flip = jnp.transpose if n in transposed else (lambda t: t)
            w_, m_, v_ = [flip(t[0]) for t in params[n]]
            if n in r7:
                res = _adamw_rs8(g8[n], r7[n], me_idx, w_, m_, v_, name="adamw_" + n)
            else:
                res = _adamw_rs(own[n], r3[n], w_, m_, v_, name="adamw_" + n)
            grad[n], delta[n], new_m[n], new_v[n] = [flip(r)[None] for r in res]

    adamw_group(("w_down", "w_out", "conv_w"))
    sib_l, grad["w_down"] = lax.optimization_barrier((sib_l, grad["w_down"]))
    sums_l = [_rs_chip_sum(g8[n], fs, place_idx, name="rs_chip_sum_" + n) for n, fs in zip(late, sib_l)]
    r3.update(zip(late, _rs_chips_async([h4 for _, h4 in sums_l], name="rs_chips_late", collective_id=4)))
    own.update({n: f for n, (f, _) in zip(late, sums_l)})
    small_part = {"g_mix": dg_mix, "g_cq": dg_cq, "g_ckv": dg_ckv, "g_sb_out": dg_sb, "g_mla_out": dg_mla,
                  "g_ffn": dg_ffn, "conv_b": _ff_deinterleave(dcb), "g_final": dg_final}
    small_all, = _all_gather_async([_pack_small(small_part, tail=loss_row[0, 0:1])], name="ag_small_async",
                                   collective_id=5)
    adamw_group(("w_up",))
    adamw_group(late)
    gsmall = _sum8(small_all, name="sum_small_grads")
    small_w = {"g_mix": g_mix, "g_cq": g_cq, "g_ckv": g_ckv, "g_sb_out": g_sb_out, "g_mla_out": g_mla_out,
               "g_ffn": g_ffn, "conv_b": conv_b, "g_final": g_final}
    small_m = {"g_mix": m_g_mix, "g_cq": m_g_cq, "g_ckv": m_g_ckv, "g_sb_out": m_g_sb_out, "g_mla_out": m_g_mla_out,
               "g_ffn": m_g_ffn, "conv_b": m_conv_b, "g_final": m_g_final}
    small_v = {"g_mix": v_g_mix, "g_cq": v_g_cq, "g_ckv": v_g_ckv, "g_sb_out": v_g_sb_out, "g_mla_out": v_g_mla_out,
               "g_ffn": v_g_ffn, "conv_b": v_conv_b, "g_final": v_g_final}
    ds_, ms_, vs_ = _adamw(_pack_small(small_w), gsmall, _pack_small(small_m), _pack_small(small_v), name="adamw_small")
    for src, dst in ((_unpack_small(gsmall), grad), (_unpack_small(ds_), delta), (_unpack_small(ms_), new_m), (_unpack_small(vs_), new_v)):
        for n, _ in SMALL:
            dst[n] = src[n].reshape(small_w[n].shape)

    loss = gsmall.reshape(-1)[SMALL_USED]
    order = ("g_mix", "w_in", "g_cq", "w_uq", "g_ckv", "w_ukv", "g_sb_out", "g_mla_out", "w_out", "g_ffn", "w_up",
             "conv_w", "conv_b", "w_down", "g_final")
    return (loss, dx.reshape(B, S, D), *[grad[n] for n in order], *[delta[n] for n in order],
            *[new_m[n] for n in order], *[new_v[n] for n in order])
```

```python
import jax
import jax.numpy as jnp
from jax import lax
from jax.experimental import pallas as pl
from jax.experimental.pallas import tpu as pltpu
from jax.experimental.pallas import tpu_sc as plsc

F32 = jnp.float32
BF16 = jnp.bfloat16

D_MODEL = 1024
SB_HEADS = 8
SB_HEAD_DIM = 64
MLA_HEADS = 8
MLA_NOPE = 64
MLA_ROPE = 32
MLA_V = 64
Q_LORA = 384
KV_LORA = 256
D_FF = 2816
ROPE_BASE = 10000.0
EPS = 1e-6
SB_W = SB_HEADS * SB_HEAD_DIM
MLA_W = MLA_HEADS * MLA_V
MLA_QK = MLA_NOPE + MLA_ROPE

ADAM_LR = 0.001
ADAM_B1 = 0.9
ADAM_B2 = 0.999
ADAM_EPS = 1e-08
ADAM_WD = 0.01
ADAM_STEP = 10

N_DEV = 8
LANES = 128
V7X_VMEM_LIMIT = 56 * 1024 * 1024
FF_BLK = 256
N_FF_BLK = D_FF // FF_BLK

P_Q, P_K, P_V = 0, SB_W, 2 * SB_W
P_CKV = 3 * SB_W
P_KRT = P_CKV + KV_LORA
P_CQ = P_KRT + LANES
P_COLS = P_CQ + Q_LORA

MESH = pl.DeviceIdType.MESH
ANY = pl.BlockSpec(memory_space=pl.ANY)


def _cparams(sem=None, vmem=V7X_VMEM_LIMIT):
    return pltpu.CompilerParams(dimension_semantics=sem, vmem_limit_bytes=vmem)


def _matmul_tn(a, b, *, tm, tn, tk, name, out_dtype=F32):
    K, M = a.shape
    N = b.shape[1]
    assert M % tm == 0 and N % tn == 0 and K % tk == 0, (name, a.shape, b.shape)
    n_k = K // tk
    narrow = out_dtype != F32

    def body(a_ref, b_ref, o_ref, *scratch):
        acc_ref = scratch[0] if narrow else o_ref
        k = pl.program_id(2)
        part = lax.dot_general(a_ref[...].astype(BF16), b_ref[...].astype(BF16), (((0,), (0,)), ((), ())),
                               preferred_element_type=F32)

        @pl.when(k == 0)
        def _():
            acc_ref[...] = part

        @pl.when(k > 0)
        def _():
            acc_ref[...] += part

        if narrow:
            @pl.when(k == n_k - 1)
            def _():
                o_ref[...] = acc_ref[...].astype(out_dtype)

    return pl.pallas_call(
        body, name=name, grid=(M // tm, N // tn, n_k),
        in_specs=[pl.BlockSpec((tk, tm), lambda i, j, k: (k, i)), pl.BlockSpec((tk, tn), lambda i, j, k: (k, j))],
        out_specs=pl.BlockSpec((tm, tn), lambda i, j, k: (i, j)),
        out_shape=jax.ShapeDtypeStruct((M, N), out_dtype),
        scratch_shapes=[pltpu.VMEM((tm, tn), F32)] if narrow else [],
        compiler_params=_cparams(("parallel", "parallel", "arbitrary")),
    )(a, b)


def _rms(xf, g):
    r = lax.rsqrt(jnp.mean(xf * xf, axis=1, keepdims=True) + EPS)
    return (xf * r) * g


def _rms_grad(dyf, xf, g):
    r = lax.rsqrt(jnp.mean(xf * xf, axis=1, keepdims=True) + EPS)
    xh = xf * r
    dyg = dyf * g
    dx = r * (dyg - xh * jnp.mean(dyg * xh, axis=1, keepdims=True))
    return dx, jnp.sum(dyf * xh, axis=0, keepdims=True)


def _accumulate(ref, part):
    @pl.when(pl.program_id(0) == 0)
    def _():
        ref[...] = part

    @pl.when(pl.program_id(0) > 0)
    def _():
        ref[...] += part


def _rms_matmul_nn(x, g, w, *, tm, name, col_block=0, out_dtype=F32, w_transposed=False):
    T = x.shape[0]
    C, N = w.shape[::-1] if w_transposed else w.shape
    assert T % tm == 0, (name, x.shape)
    contract = (((1,), (1,)), ((), ())) if w_transposed else (((1,), (0,)), ((), ()))

    def body(x_ref, g_ref, w_ref, h_ref, o_ref):
        hb = _rms(x_ref[...], g_ref[...]).astype(BF16)
        h_ref[...] = hb
        o_ref[...] = lax.dot_general(hb, w_ref[...], contract, preferred_element_type=F32).astype(out_dtype)

    return pl.pallas_call(
        body, name=name, grid=(T // tm,),
        in_specs=[pl.BlockSpec((tm, C), lambda i: (i, col_block)), pl.BlockSpec((1, C), lambda i: (0, 0)),
                  pl.BlockSpec(w.shape, lambda i: (0, 0))],
        out_specs=[pl.BlockSpec((tm, C), lambda i: (i, 0)), pl.BlockSpec((tm, N), lambda i: (i, 0))],
        out_shape=[jax.ShapeDtypeStruct((T, C), BF16), jax.ShapeDtypeStruct((T, N), out_dtype)],
        compiler_params=_cparams(("parallel",)),
    )(x, g, w)


def _matmul_nt_rms_bwd(a, b, x, g, *, tm, name, residual=None, col_block=0, out_dtype=F32):
    M, K = a.shape
    C = b.shape[0]
    assert M % tm == 0, (name, a.shape)
    in_specs = [pl.BlockSpec((tm, K), lambda i: (i, 0)), pl.BlockSpec((C, K), lambda i: (0, 0)),
                pl.BlockSpec((tm, C), lambda i: (i, col_block)), pl.BlockSpec((1, C), lambda i: (0, 0))]
    args = [a, b, x, g]
    if residual is not None:
        in_specs.append(pl.BlockSpec((tm, C), lambda i: (i, 0)))
        args.append(residual)

    def body(*refs):
        a_ref, b_ref, x_ref, g_ref = refs[:4]
        dx_ref, dg_ref = refs[-2:]
        dy = lax.dot_general(a_ref[...].astype(BF16), b_ref[...], (((1,), (1,)), ((), ())), preferred_element_type=F32)
        dx, part = _rms_grad(dy, x_ref[...], g_ref[...])
        if residual is not None:
            dx = dx + refs[4][...]
        dx_ref[...] = dx.astype(out_dtype)
        _accumulate(dg_ref, part)

    return pl.pallas_call(
        body, name=name, grid=(M // tm,), in_specs=in_specs,
        out_specs=[pl.BlockSpec((tm, C), lambda i: (i, 0)), pl.BlockSpec((1, C), lambda i: (0, 0))],
        out_shape=[jax.ShapeDtypeStruct((M, C), out_dtype), jax.ShapeDtypeStruct((1, C), F32)],
        compiler_params=_cparams(("arbitrary",)),
    )(*args)


def _matmul_nn_loss(a, w, x1, g, tgt, *, tm, name):
    M, K = a.shape
    C = w.shape[1]
    assert M % tm == 0, (name, a.shape)

    nsub = 4
    ts = tm // nsub

    def body(a_ref, w_ref, x_ref, g_ref, t_ref, dx_ref, dg_ref, loss_ref):
        gf = g_ref[...]
        wv = w_ref[...]
        rows = [slice(r * ts, (r + 1) * ts) for r in range(nsub)]
        xs = [x_ref[rw, :] + jnp.dot(a_ref[rw, :], wv, preferred_element_type=F32) for rw in rows]
        lpart, gpart = 0.0, 0.0
        for rw, xf in zip(rows, xs):
            err = _rms(xf, gf) - t_ref[rw, :]
            lpart = lpart + 0.5 * jnp.sum(jnp.mean(err * err, axis=1, keepdims=True), axis=0, keepdims=True)
            dx, gp = _rms_grad(err * (1.0 / C), xf, gf)
            dx_ref[rw, :] = dx
            gpart = gpart + gp
        _accumulate(dg_ref, gpart)
        _accumulate(loss_ref, jnp.broadcast_to(lpart, (1, LANES)))

    row = pl.BlockSpec((tm, C), lambda i: (i, 0))
    return pl.pallas_call(
        body, name=name, grid=(M // tm,),
        in_specs=[pl.BlockSpec((tm, K), lambda i: (i, 0)), pl.BlockSpec((K, C), lambda i: (0, 0)), row,
                  pl.BlockSpec((1, C), lambda i: (0, 0)), row],
        out_specs=[row, pl.BlockSpec((1, C), lambda i: (0, 0)), pl.BlockSpec((1, LANES), lambda i: (0, 0))],
        out_shape=[jax.ShapeDtypeStruct((M, C), F32), jax.ShapeDtypeStruct((1, C), F32),
                   jax.ShapeDtypeStruct((1, LANES), F32)],
        compiler_params=_cparams(("arbitrary",)),
    )(a, w, x1, g, tgt)


ATT_T = 256
ATT_PAIRS = 2
NEG_BIG = -1e30


def _lane_iota():
    return lax.broadcasted_iota(jnp.int32, (1, LANES), 1)


def _head_masks():
    first = _lane_iota() < SB_HEAD_DIM
    return first, jnp.logical_not(first)


def _pick(mask, x):
    return jnp.where(mask, x, jnp.zeros_like(x))


def _lane_value(t, lane):
    return jnp.sum(jnp.where(_lane_iota() == lane, t, 0.0), axis=1, keepdims=True)


def _split_hi_lo(x):
    hi = x.astype(BF16)
    lo = (x - hi.astype(F32)).astype(BF16)
    return jnp.concatenate([hi, lo], axis=1)


def _tri(n, kind):
    r = lax.broadcasted_iota(jnp.int32, (n, n), 0)
    c = lax.broadcasted_iota(jnp.int32, (n, n), 1)
    u = {"suffix_excl": r > c, "prefix_incl": r <= c, "prefix_excl": r < c}[kind].astype(BF16)
    return jnp.concatenate([u, u], axis=0)


def _dot_nt(a, b):
    return lax.dot_general(a, b, (((1,), (1,)), ((), ())), preferred_element_type=F32)


def _dot_tn(a, b):
    return lax.dot_general(a, b, (((0,), (0,)), ((), ())), preferred_element_type=F32)


def _dot(a, b):
    return jnp.dot(a, b, preferred_element_type=F32)


def _causal_mask(n, strict):
    r = lax.broadcasted_iota(jnp.int32, (n, n), 0)
    c = lax.broadcasted_iota(jnp.int32, (n, n), 1)
    return (c < r) if strict else (c <= r)


LOG2E = 1.4426950408889634


def _sb_logs(qh, kj, vis):
    z2 = _dot_nt(qh, kj) * LOG2E
    nk = jnp.maximum(z2, 0.0) + jnp.log2(1.0 + jnp.exp2(-jnp.abs(z2)))
    lb = z2 - nk
    if vis is not None:
        nk = jnp.where(vis, nk, 0.0)
    return lb, nk


def _sb_fwd(p, *, seq, name):
    T = p.shape[0]
    B = T // seq
    TQ = ATT_T
    nq = seq // TQ
    PP = ATT_PAIRS
    W = PP * LANES
    nstep = SB_W // W
    NH = 2 * PP

    def body(q_ref, k_ref, v_ref, o_ref, lt_ref, q_s, k_s, v_s):
        masks = _head_masks()
        q = q_ref[...] * (SB_HEAD_DIM ** -0.5)
        v = v_ref[...]
        k_s[...] = k_ref[...].astype(BF16)
        for h in range(NH):
            ps = slice((h // 2) * LANES, (h // 2 + 1) * LANES)
            hs = slice(h * LANES, (h + 1) * LANES)
            q_s[:, hs] = _pick(masks[h % 2], q[:, ps]).astype(BF16)
            v_s[:, hs] = _pick(masks[h % 2], v[:, ps]).astype(BF16)
        u_suf = _tri(TQ, "suffix_excl")
        vis = _causal_mask(TQ, True)

        def q_block(i, carry):
            q0 = pl.multiple_of(i * TQ, TQ)
            qs = [q_s[pl.ds(q0, TQ), h * LANES:(h + 1) * LANES] for h in range(NH)]

            def tile(k0, c, mask):
                rs, accs = list(c[:NH]), list(c[NH:])
                logs = [_sb_logs(qs[h], k_s[pl.ds(k0, TQ), (h // 2) * LANES:(h // 2 + 1) * LANES], mask) for h in range(NH)]
                sums = [_dot(_split_hi_lo(nk), u_suf) for _, nk in logs]
                for h in range(NH):
                    a = jnp.exp2(logs[h][0] - sums[h] - rs[h])
                    if mask is not None:
                        a = jnp.where(mask, a, 0.0)
                    accs[h // 2] = accs[h // 2] + _dot(a.astype(BF16), v_s[pl.ds(k0, TQ), h * LANES:(h + 1) * LANES])
                    rs[h] = rs[h] + jnp.sum(logs[h][1], axis=1, keepdims=True)
                return tuple(rs) + tuple(accs)

            zero = jnp.zeros((TQ, 1), F32)
            c = tile(q0, (zero,) * NH + (jnp.zeros((TQ, LANES), F32),) * PP, vis)

            def k_block(jj, c):
                return tile(pl.multiple_of((i - 1 - jj) * TQ, TQ), c, None)

            c = lax.fori_loop(0, i, k_block, c)
            for pr in range(PP):
                ps = slice(pr * LANES, (pr + 1) * LANES)
                o_ref[pl.ds(q0, TQ), ps] = c[NH + pr]
                lt_ref[pl.ds(q0, TQ), ps] = jnp.where(masks[0], c[2 * pr], c[2 * pr + 1])
            return carry

        lax.fori_loop(0, nq, q_block, 0)

    blk = lambda off: pl.BlockSpec((seq, W), lambda b, g: (b, off + g))
    out_blk = pl.BlockSpec((seq, W), lambda b, g: (b, g))
    return pl.pallas_call(
        body, name=name, grid=(B, nstep),
        in_specs=[blk(P_Q // W), blk(P_K // W), blk(P_V // W)],
        out_specs=[out_blk, out_blk],
        out_shape=[jax.ShapeDtypeStruct((T, SB_W), F32), jax.ShapeDtypeStruct((T, SB_W), F32)],
        scratch_shapes=[pltpu.VMEM((seq, NH * LANES), BF16), pltpu.VMEM((seq, W), BF16), pltpu.VMEM((seq, NH * LANES), BF16)],
        compiler_params=_cparams(("parallel", "parallel")),
    )(p, p, p)


def _sb_bwd(p, ltot, do, *, seq, name):
    T = p.shape[0]
    B = T // seq
    TQ = ATT_T
    nq = seq // TQ
    PP = ATT_PAIRS
    W = PP * LANES
    nstep = SB_W // W
    NH = 2 * PP
    scale = SB_HEAD_DIM ** -0.5

    def body(q_ref, k_ref, v_ref, lt_ref, do_ref, dq_ref, dk_ref, dv_ref, q_s, k_s, v_s, do_s, dk_s, dv_s):
        masks = _head_masks()
        q = q_ref[...] * scale
        dof = do_ref[...]
        k_s[...] = k_ref[...].astype(BF16)
        v_s[...] = v_ref[...].astype(BF16)
        for h in range(NH):
            ps = slice((h // 2) * LANES, (h // 2 + 1) * LANES)
            hs = slice(h * LANES, (h + 1) * LANES)
            q_s[:, hs] = _pick(masks[h % 2], q[:, ps]).astype(BF16)
            do_s[:, hs] = _pick(masks[h % 2], dof[:, ps]).astype(BF16)
        dk_s[...] = jnp.zeros_like(dk_s)
        dv_s[...] = jnp.zeros_like(dv_s)
        u_pin = _tri(TQ, "prefix_incl")
        u_pex = _tri(TQ, "prefix_excl")[:TQ]
        vis = _causal_mask(TQ, True)

        def q_block(i, carry):
            q0 = pl.multiple_of(i * TQ, TQ)
            qs = [q_s[pl.ds(q0, TQ), h * LANES:(h + 1) * LANES] for h in range(NH)]
            dos = [do_s[pl.ds(q0, TQ), h * LANES:(h + 1) * LANES] for h in range(NH)]
            lt = lt_ref[pl.ds(q0, TQ), :]
            lts = [_lane_value(lt[:, (h // 2) * LANES:(h // 2 + 1) * LANES], (h % 2) * SB_HEAD_DIM) for h in range(NH)]

            def tile(k0, c, mask):
                cs, gs, accs = list(c[:NH]), list(c[NH:2 * NH]), list(c[2 * NH:])
                kjs = [k_s[pl.ds(k0, TQ), pr * LANES:(pr + 1) * LANES] for pr in range(PP)]
                vjs = [v_s[pl.ds(k0, TQ), pr * LANES:(pr + 1) * LANES] for pr in range(PP)]
                logs = [_sb_logs(qs[h], kjs[h // 2], mask) for h in range(NH)]
                pins = [_dot(_split_hi_lo(nk), u_pin) for _, nk in logs]
                das = [_dot_nt(dos[h], vjs[h // 2]) for h in range(NH)]
                a_l, g_l = [], []
                for h in range(NH):
                    a = jnp.exp2(logs[h][0] - ((lts[h] - cs[h]) - pins[h]))
                    if mask is not None:
                        a = jnp.where(mask, a, 0.0)
                    a_l.append(a)
                    g_l.append(das[h] * a)
                pres = [_dot(g.astype(BF16), u_pex) for g in g_l]
                dz_l = []
                for h in range(NH):
                    dz = g_l[h] - jnp.exp2(logs[h][0]) * (g_l[h] + (pres[h] + gs[h]))
                    if mask is not None:
                        dz = jnp.where(mask, dz, 0.0)
                    dz_l.append(dz.astype(BF16))
                for h in range(NH):
                    accs[h] = accs[h] + _dot(dz_l[h], kjs[h // 2])
                for pr in range(PP):
                    ps = slice(pr * LANES, (pr + 1) * LANES)
                    ha, hb = 2 * pr, 2 * pr + 1
                    dk_s[ps, pl.ds(k0, TQ)] += _dot_tn(qs[ha], dz_l[ha]) + _dot_tn(qs[hb], dz_l[hb])
                    dv_s[ps, pl.ds(k0, TQ)] += _dot_tn(dos[ha], a_l[ha].astype(BF16)) + _dot_tn(dos[hb], a_l[hb].astype(BF16))
                for h in range(NH):
                    cs[h] = cs[h] + jnp.sum(logs[h][1], axis=1, keepdims=True)
                    gs[h] = gs[h] + jnp.sum(g_l[h], axis=1, keepdims=True)
                return tuple(cs) + tuple(gs) + tuple(accs)

            z1 = jnp.zeros((TQ, 1), F32)
            zl = jnp.zeros((TQ, LANES), F32)

            def k_block(j, c):
                return tile(pl.multiple_of(j * TQ, TQ), c, None)

            c = lax.fori_loop(0, i, k_block, (z1,) * (2 * NH) + (zl,) * NH)
            c = tile(q0, c, vis)
            for pr in range(PP):
                dq = jnp.where(masks[0], c[2 * NH + 2 * pr], c[2 * NH + 2 * pr + 1]) * scale
                dq_ref[pl.ds(q0, TQ), pr * LANES:(pr + 1) * LANES] = dq.astype(BF16)
            return carry

        lax.fori_loop(0, nq, q_block, 0)
        dk_ref[...] = dk_s[...].T.astype(BF16)
        dv_ref[...] = dv_s[...].T.astype(BF16)

    blk = lambda off: pl.BlockSpec((seq, W), lambda b, g: (b, off + g))
    out_blk = pl.BlockSpec((seq, W), lambda b, g: (b, g))
    return pl.pallas_call(
        body, name=name, grid=(B, nstep),
        in_specs=[blk(P_Q // W), blk(P_K // W), blk(P_V // W), out_blk, out_blk],
        out_specs=[out_blk, out_blk, out_blk],
        out_shape=[jax.ShapeDtypeStruct((T, SB_W), BF16) for _ in range(3)],
        scratch_shapes=[pltpu.VMEM((seq, NH * LANES), BF16), pltpu.VMEM((seq, W), BF16), pltpu.VMEM((seq, W), BF16),
                        pltpu.VMEM((seq, NH * LANES), BF16), pltpu.VMEM((W, seq), F32), pltpu.VMEM((W, seq), F32)],
        compiler_params=_cparams(("parallel", "parallel")),
    )(p, p, p, ltot, do)


def _mla_masks():
    lane = lax.broadcasted_iota(jnp.int32, (1, 2 * LANES), 1)
    ma = (lane < MLA_NOPE) | ((lane >= LANES) & (lane < LANES + MLA_ROPE))
    mb = ((lane >= MLA_NOPE) & (lane < LANES)) | ((lane >= LANES + MLA_ROPE) & (lane < LANES + 2 * MLA_ROPE))
    return ma, mb


def _mla_fwd(qm, kvm, krt, *, seq, name):
    T = qm.shape[0]
    B = T // seq
    TQ = ATT_T
    nq = seq // TQ
    PP = ATT_PAIRS
    W = PP * LANES
    nstep = MLA_W // W
    NH = 2 * PP
    CW = 2 * LANES
    scale = MLA_QK ** -0.5

    def body(qn_ref, qr_ref, kn_ref, v_ref, kr_ref, o_ref, lse_ref, q_s, kc_s, v_s):
        hm = _head_masks()
        mm = _mla_masks()
        v = v_ref[...]
        for pr in range(PP):
            ps = slice(pr * LANES, (pr + 1) * LANES)
            qc = jnp.concatenate([qn_ref[:, ps], qr_ref[:, ps]], axis=1)
            kc_s[:, pr * CW:(pr + 1) * CW] = jnp.concatenate([kn_ref[:, ps], kr_ref[...]], axis=1)
            for e in range(2):
                h = 2 * pr + e
                q_s[:, h * CW:(h + 1) * CW] = _pick(mm[e], qc)
                v_s[:, h * LANES:(h + 1) * LANES] = _pick(hm[e], v[:, ps])
        vis = _causal_mask(TQ, False)

        def q_block(i, carry):
            q0 = pl.multiple_of(i * TQ, TQ)
            qs = [q_s[pl.ds(q0, TQ), h * CW:(h + 1) * CW] for h in range(NH)]

            def tile(k0, c, mask):
                ms, ls, accs = list(c[:NH]), list(c[NH:2 * NH]), list(c[2 * NH:])
                ss = [_dot_nt(qs[h], kc_s[pl.ds(k0, TQ), (h // 2) * CW:(h // 2 + 1) * CW]) * scale for h in range(NH)]
                if mask is not None:
                    ss = [jnp.where(mask, s, NEG_BIG) for s in ss]
                m_new = [jnp.maximum(ms[h], jnp.max(ss[h], axis=1, keepdims=True)) for h in range(NH)]
                alphas = [jnp.exp(ms[h] - m_new[h]) for h in range(NH)]
                prs = [jnp.exp(ss[h] - m_new[h]) for h in range(NH)]
                outs = [_dot(prs[h].astype(BF16), v_s[pl.ds(k0, TQ), h * LANES:(h + 1) * LANES]) for h in range(NH)]
                ls = [alphas[h] * ls[h] + jnp.sum(prs[h], axis=1, keepdims=True) for h in range(NH)]
                for pr in range(PP):
                    accs[pr] = accs[pr] * jnp.where(hm[0], alphas[2 * pr], alphas[2 * pr + 1]) + outs[2 * pr] + outs[2 * pr + 1]
                return tuple(m_new) + tuple(ls) + tuple(accs)

            neg = jnp.full((TQ, 1), NEG_BIG, F32)
            z1 = jnp.zeros((TQ, 1), F32)

            def k_block(j, c):
                return tile(pl.multiple_of(j * TQ, TQ), c, None)

            c = lax.fori_loop(0, i, k_block, (neg,) * NH + (z1,) * NH + (jnp.zeros((TQ, LANES), F32),) * PP)
            c = tile(q0, c, vis)
            for pr in range(PP):
                ps = slice(pr * LANES, (pr + 1) * LANES)
                m_a, m_b, l_a, l_b = c[2 * pr], c[2 * pr + 1], c[NH + 2 * pr], c[NH + 2 * pr + 1]
                o_ref[pl.ds(q0, TQ), ps] = c[2 * NH + pr] / jnp.where(hm[0], l_a, l_b)
                lse_ref[pl.ds(q0, TQ), ps] = jnp.where(hm[0], m_a + jnp.log(l_a), m_b + jnp.log(l_b))
            return carry

        lax.fori_loop(0, nq, q_block, 0)

    blk = lambda off: pl.BlockSpec((seq, W), lambda b, g: (b, off + g))
    out_blk = pl.BlockSpec((seq, W), lambda b, g: (b, g))
    return pl.pallas_call(
        body, name=name, grid=(B, nstep),
        in_specs=[blk(0), blk(nstep), blk(0), blk(nstep), pl.BlockSpec((seq, LANES), lambda b, g: (b, 0))],
        out_specs=[out_blk, out_blk],
        out_shape=[jax.ShapeDtypeStruct((T, MLA_W), F32), jax.ShapeDtypeStruct((T, MLA_W), F32)],
        scratch_shapes=[pltpu.VMEM((seq, NH * CW), BF16), pltpu.VMEM((seq, PP * CW), BF16), pltpu.VMEM((seq, NH * LANES), BF16)],
        compiler_params=_cparams(("parallel", "parallel")),
    )(qm, qm, kvm, kvm, krt)


def _mla_bwd(qm, kvm, krt, o, lse, do, *, seq, name):
    T = qm.shape[0]
    B = T // seq
    TQ = ATT_T
    nq = seq // TQ
    PP = ATT_PAIRS
    W = PP * LANES
    nstep = MLA_W // W
    NH = 2 * PP
    CW = 2 * LANES
    scale = MLA_QK ** -0.5

    def body(qn_ref, qr_ref, kn_ref, v_ref, kr_ref, o_ref, lse_ref, do_ref,
             dqn_ref, dqr_ref, dkn_ref, dv_ref, dkr_ref, q_s, kc_s, do_s, dkc_s, dv_s):
        hm = _head_masks()
        mm = _mla_masks()
        dof = do_ref[...]
        for pr in range(PP):
            ps = slice(pr * LANES, (pr + 1) * LANES)
            qc = jnp.concatenate([qn_ref[:, ps], qr_ref[:, ps]], axis=1)
            kc_s[:, pr * CW:(pr + 1) * CW] = jnp.concatenate([kn_ref[:, ps], kr_ref[...]], axis=1)
            for e in range(2):
                h = 2 * pr + e
                q_s[:, h * CW:(h + 1) * CW] = _pick(mm[e], qc)
                do_s[:, h * LANES:(h + 1) * LANES] = _pick(hm[e], dof[:, ps]).astype(BF16)
        dkc_s[...] = jnp.zeros_like(dkc_s)
        dv_s[...] = jnp.zeros_like(dv_s)
        vis = _causal_mask(TQ, False)

        def q_block(i, carry):
            q0 = pl.multiple_of(i * TQ, TQ)
            qs = [q_s[pl.ds(q0, TQ), h * CW:(h + 1) * CW] for h in range(NH)]
            dos = [do_s[pl.ds(q0, TQ), h * LANES:(h + 1) * LANES] for h in range(NH)]
            lse_t = lse_ref[pl.ds(q0, TQ), :]
            dd = do_ref[pl.ds(q0, TQ), :] * o_ref[pl.ds(q0, TQ), :]
            lses, ds_ = [], []
            for h in range(NH):
                ps = slice((h // 2) * LANES, (h // 2 + 1) * LANES)
                lses.append(_lane_value(lse_t[:, ps], (h % 2) * MLA_V))
                ds_.append(jnp.sum(_pick(hm[h % 2], dd[:, ps]), axis=1, keepdims=True))

            def tile(k0, c, mask):
                accs = list(c)
                kcs = [kc_s[pl.ds(k0, TQ), pr * CW:(pr + 1) * CW] for pr in range(PP)]
                vjs = [v_ref[pl.ds(k0, TQ), pr * LANES:(pr + 1) * LANES] for pr in range(PP)]
                ss = [_dot_nt(qs[h], kcs[h // 2]) * scale for h in range(NH)]
                dps = [_dot_nt(dos[h], vjs[h // 2]) for h in range(NH)]
                p_l, ds_l = [], []
                for h in range(NH):
                    pr_ = jnp.exp(ss[h] - lses[h])
                    if mask is not None:
                        pr_ = jnp.where(mask, pr_, 0.0)
                    p_l.append(pr_.astype(BF16))
                    ds_l.append((pr_ * (dps[h] - ds_[h]) * scale).astype(BF16))
                for h in range(NH):
                    accs[h] = accs[h] + _dot(ds_l[h], kcs[h // 2])
                for pr in range(PP):
                    ha, hb = 2 * pr, 2 * pr + 1
                    dkc_s[pl.ds(k0, TQ), pr * CW:(pr + 1) * CW] += _dot_tn(ds_l[ha], qs[ha]) + _dot_tn(ds_l[hb], qs[hb])
                    dv_s[pr * LANES:(pr + 1) * LANES, pl.ds(k0, TQ)] += _dot_tn(dos[ha], p_l[ha]) + _dot_tn(dos[hb], p_l[hb])
                return tuple(accs)

            zc = jnp.zeros((TQ, CW), F32)

            def k_block(j, c):
                return tile(pl.multiple_of(j * TQ, TQ), c, None)

            c = lax.fori_loop(0, i, k_block, (zc,) * NH)
            c = tile(q0, c, vis)
            for pr in range(PP):
                ps = slice(pr * LANES, (pr + 1) * LANES)
                dq = _pick(mm[0], c[2 * pr]) + _pick(mm[1], c[2 * pr + 1])
                dqn_ref[pl.ds(q0, TQ), ps] = dq[:, :LANES].astype(BF16)
                dqr_ref[pl.ds(q0, TQ), ps] = dq[:, LANES:]
            return carry

        lax.fori_loop(0, nq, q_block, 0)
        dkr = dkc_s[:, LANES:CW]
        for pr in range(PP):
            dkn_ref[:, pr * LANES:(pr + 1) * LANES] = dkc_s[:, pr * CW:pr * CW + LANES].astype(BF16)
            if pr > 0:
                dkr = dkr + dkc_s[:, pr * CW + LANES:(pr + 1) * CW]
        dv_ref[...] = dv_s[...].T.astype(BF16)
        g = pl.program_id(1)

        @pl.when(g == 0)
        def _():
            dkr_ref[...] = dkr

        @pl.when(g > 0)
        def _():
            dkr_ref[...] += dkr

    blk = lambda off: pl.BlockSpec((seq, W), lambda b, g: (b, off + g))
    out_blk = pl.BlockSpec((seq, W), lambda b, g: (b, g))
    one_blk = pl.BlockSpec((seq, LANES), lambda b, g: (b, 0))
    return pl.pallas_call(
        body, name=name, grid=(B, nstep),
        in_specs=[blk(0), blk(nstep), blk(0), blk(nstep), one_blk, out_blk, out_blk, out_blk],
        out_specs=[out_blk, out_blk, out_blk, out_blk, one_blk],
        out_shape=[jax.ShapeDtypeStruct((T, MLA_W), BF16), jax.ShapeDtypeStruct((T, MLA_W), F32),
                   jax.ShapeDtypeStruct((T, MLA_W), BF16), jax.ShapeDtypeStruct((T, MLA_W), BF16),
                   jax.ShapeDtypeStruct((T, LANES), F32)],
        scratch_shapes=[pltpu.VMEM((seq, NH * CW), BF16), pltpu.VMEM((seq, PP * CW), BF16), pltpu.VMEM((seq, NH * LANES), BF16),
                        pltpu.VMEM((seq, PP * CW), F32), pltpu.VMEM((W, seq), F32)],
        compiler_params=_cparams(("parallel", "arbitrary")),
    )(qm, qm, kvm, kvm, krt, o, lse, do)


def _rope_tables(pos_ref, invf_ref):
    ang = pos_ref[...].astype(F32) * invf_ref[...]
    first = (_lane_iota() % MLA_ROPE) < (MLA_ROPE // 2)
    return jnp.cos(ang), jnp.sin(ang), first


def _rope_apply(x, cos, sin, first):
    rot = jnp.where(first, -pltpu.roll(x, LANES - MLA_ROPE // 2, 1), pltpu.roll(x, MLA_ROPE // 2, 1))
    return x * cos + rot * sin


def _rope_apply_t(dy, cos, sin, first):
    dys = dy * sin
    rot_t = jnp.where(first, pltpu.roll(dys, LANES - MLA_ROPE // 2, 1), -pltpu.roll(dys, MLA_ROPE // 2, 1))
    return dy * cos + rot_t


def _proj_uq_rope(p, g, wuq, pos, invf, *, tm, name):
    T = p.shape[0]
    ntile = MLA_W // LANES

    def body(x_ref, kr_ref, g_ref, w_ref, pos_ref, invf_ref, cq_ref, qm_ref, krt_ref):
        cos, sin, first = _rope_tables(pos_ref, invf_ref)
        hb = _rms(x_ref[...], g_ref[...]).astype(BF16)
        cq_ref[...] = hb
        q = jnp.dot(hb, w_ref[...], preferred_element_type=F32)
        qm_ref[:, :MLA_W] = q[:, :MLA_W].astype(BF16)
        for t in range(ntile):
            sl = slice(MLA_W + t * LANES, MLA_W + (t + 1) * LANES)
            qm_ref[:, sl] = _rope_apply(q[:, sl], cos, sin, first).astype(BF16)
        krt_ref[...] = _rope_apply(kr_ref[...], cos, sin, first).astype(BF16)

    return pl.pallas_call(
        body, name=name, grid=(T // tm,),
        in_specs=[pl.BlockSpec((tm, Q_LORA), lambda i: (i, P_CQ // Q_LORA)), pl.BlockSpec((tm, LANES), lambda i: (i, P_KRT // LANES)),
                  pl.BlockSpec((1, Q_LORA), lambda i: (0, 0)), pl.BlockSpec((Q_LORA, 2 * MLA_W), lambda i: (0, 0)),
                  pl.BlockSpec((tm, 1), lambda i: (i, 0)), pl.BlockSpec((1, LANES), lambda i: (0, 0))],
        out_specs=[pl.BlockSpec((tm, Q_LORA), lambda i: (i, 0)), pl.BlockSpec((tm, 2 * MLA_W), lambda i: (i, 0)),
                   pl.BlockSpec((tm, LANES), lambda i: (i, 0))],
        out_shape=[jax.ShapeDtypeStruct((T, Q_LORA), BF16), jax.ShapeDtypeStruct((T, 2 * MLA_W), BF16),
                   jax.ShapeDtypeStruct((T, LANES), BF16)],
        compiler_params=_cparams(("parallel",)),
    )(p, p, g, wuq, pos, invf)


def _d_proj_uq_rope(dqn, dqr, dkr, wuq, cq, p, g, pos, invf, *, tm, name):
    T = dqn.shape[0]
    ntile = MLA_W // LANES

    def body(dqn_ref, dqr_ref, dkr_ref, w_ref, cq_ref, x_ref, g_ref, pos_ref, invf_ref,
             dw_ref, dx_ref, dg_ref, dkr_o_ref, dqm_s):
        cos, sin, first = _rope_tables(pos_ref, invf_ref)
        dqm_s[:, :MLA_W] = dqn_ref[...]
        for t in range(ntile):
            sl = slice(t * LANES, (t + 1) * LANES)
            dqm_s[:, MLA_W + t * LANES:MLA_W + (t + 1) * LANES] = _rope_apply_t(dqr_ref[:, sl], cos, sin, first).astype(BF16)
        dkr_o_ref[...] = _rope_apply_t(dkr_ref[...], cos, sin, first).astype(BF16)
        dqm = dqm_s[...]
        dy = lax.dot_general(dqm, w_ref[...], (((1,), (1,)), ((), ())), preferred_element_type=F32)
        dx, part = _rms_grad(dy, x_ref[...], g_ref[...])
        dx_ref[...] = dx.astype(BF16)
        _accumulate(dg_ref, part)
        _accumulate(dw_ref, _dot_tn(dqm, cq_ref[...]))

    half = pl.BlockSpec((tm, MLA_W), lambda i: (i, 0))
    tile = pl.BlockSpec((tm, LANES), lambda i: (i, 0))
    lat = pl.BlockSpec((tm, Q_LORA), lambda i: (i, 0))
    return pl.pallas_call(
        body, name=name, grid=(T // tm,),
        in_specs=[half, half, tile, pl.BlockSpec((Q_LORA, 2 * MLA_W), lambda i: (0, 0)), lat,
                  pl.BlockSpec((tm, Q_LORA), lambda i: (i, P_CQ // Q_LORA)), pl.BlockSpec((1, Q_LORA), lambda i: (0, 0)),
                  pl.BlockSpec((tm, 1), lambda i: (i, 0)), pl.BlockSpec((1, LANES), lambda i: (0, 0))],
        out_specs=[pl.BlockSpec((2 * MLA_W, Q_LORA), lambda i: (0, 0)), lat,
                   pl.BlockSpec((1, Q_LORA), lambda i: (0, 0)), tile],
        out_shape=[jax.ShapeDtypeStruct((2 * MLA_W, Q_LORA), F32), jax.ShapeDtypeStruct((T, Q_LORA), BF16),
                   jax.ShapeDtypeStruct((1, Q_LORA), F32), jax.ShapeDtypeStruct((T, LANES), BF16)],
        scratch_shapes=[pltpu.VMEM((tm, 2 * MLA_W), BF16)],
        compiler_params=_cparams(("arbitrary",)),
    )(dqn, dqr, dkr, wuq, cq, p, g, pos, invf)


def _d_proj_cat(pieces, b, x, g, *, tm, name, residual=None, col_block=0, out_dtype=F32, b_transposed=False,
                wgrad_act=None):
    M = pieces[0].shape[0]
    widths = [pc.shape[1] for pc in pieces]
    K = sum(widths)
    C = b.shape[1] if b_transposed else b.shape[0]
    n = len(pieces)
    fuse_w = wgrad_act is not None
    contract = (((1,), (0,)), ((), ())) if b_transposed else (((1,), (1,)), ((), ()))
    in_specs = [pl.BlockSpec((tm, w), lambda i: (i, 0)) for w in widths]
    in_specs += [pl.BlockSpec(b.shape, lambda i: (0, 0)), pl.BlockSpec((tm, C), lambda i: (i, col_block)),
                 pl.BlockSpec((1, C), lambda i: (0, 0))]
    args = list(pieces) + [b, x, g]
    if residual is not None:
        in_specs.append(pl.BlockSpec((tm, C), lambda i: (i, 0)))
        args.append(residual)
    if fuse_w:
        in_specs.append(pl.BlockSpec((tm, C), lambda i: (i, 0)))
        args.append(wgrad_act)

    def body(*refs):
        b_ref, x_ref, g_ref = refs[n:n + 3]
        first_ref, dx_ref, dg_ref = refs[-4:-1] if fuse_w else refs[-3:]
        cat_ref = refs[-1] if fuse_w else first_ref
        off = 0
        for r, w in zip(refs[:n], widths):
            cat_ref[:, off:off + w] = r[...]
            off += w
        cat = cat_ref[...]
        dy = lax.dot_general(cat, b_ref[...], contract, preferred_element_type=F32)
        dx, part = _rms_grad(dy, x_ref[...], g_ref[...])
        if residual is not None:
            dx = dx + refs[n + 3][...]
        dx_ref[...] = dx.astype(out_dtype)
        _accumulate(dg_ref, part)
        if fuse_w:
            act_ref = refs[n + 3 + (residual is not None)]
            _accumulate(first_ref, _dot_tn(act_ref[...], cat))

    first_spec = pl.BlockSpec((C, K), lambda i: (0, 0)) if fuse_w else pl.BlockSpec((tm, K), lambda i: (i, 0))
    first_shape = jax.ShapeDtypeStruct((C, K), F32) if fuse_w else jax.ShapeDtypeStruct((M, K), BF16)
    return pl.pallas_call(
        body, name=name, grid=(M // tm,), in_specs=in_specs,
        out_specs=[first_spec, pl.BlockSpec((tm, C), lambda i: (i, 0)), pl.BlockSpec((1, C), lambda i: (0, 0))],
        out_shape=[first_shape, jax.ShapeDtypeStruct((M, C), out_dtype), jax.ShapeDtypeStruct((1, C), F32)],
        scratch_shapes=[pltpu.VMEM((tm, K), BF16)] if fuse_w else [],
        compiler_params=_cparams(("arbitrary",)),
    )(*args)


def _heads_out(xa, xb, ga, gb, w, resid, *, tm, name):
    T, C = xa.shape
    N = w.shape[1]

    def body(xa_ref, xb_ref, ga_ref, gb_ref, w_ref, r_ref, oc_ref, o_ref):
        oc_ref[:, :C] = _rms(xa_ref[...], ga_ref[...]).astype(BF16)
        oc_ref[:, C:] = _rms(xb_ref[...], gb_ref[...]).astype(BF16)
        o_ref[...] = r_ref[...] + jnp.dot(oc_ref[...], w_ref[...], preferred_element_type=F32)

    row = pl.BlockSpec((tm, C), lambda i: (i, 0))
    gsp = pl.BlockSpec((1, C), lambda i: (0, 0))
    full = pl.BlockSpec((tm, N), lambda i: (i, 0))
    return pl.pallas_call(
        body, name=name, grid=(T // tm,),
        in_specs=[row, row, gsp, gsp, pl.BlockSpec((2 * C, N), lambda i: (0, 0)), full],
        out_specs=[pl.BlockSpec((tm, 2 * C), lambda i: (i, 0)), full],
        out_shape=[jax.ShapeDtypeStruct((T, 2 * C), BF16), jax.ShapeDtypeStruct((T, N), F32)],
        compiler_params=_cparams(("parallel",)),
    )(xa, xb, ga, gb, w, resid)


def _heads_out_bwd(dout, w, ocat, xa, xb, ga, gb, *, tm, name):
    T, C = xa.shape
    N = w.shape[1]
    n_steps = T // tm

    def body(d_ref, w_ref, oc_ref, xa_ref, xb_ref, ga_ref, gb_ref, dxa_ref, dxb_ref, dga_ref, dgb_ref, dw_ref, dw_s):
        db = d_ref[...].astype(BF16)
        dy = lax.dot_general(db, w_ref[...], (((1,), (1,)), ((), ())), preferred_element_type=F32)
        dxa, pa = _rms_grad(dy[:, :C], xa_ref[...], ga_ref[...])
        dxb, pb = _rms_grad(dy[:, C:], xb_ref[...], gb_ref[...])
        dxa_ref[...] = dxa
        dxb_ref[...] = dxb
        _accumulate(dga_ref, pa)
        _accumulate(dgb_ref, pb)
        _accumulate(dw_s, _dot_tn(oc_ref[...], db))

        @pl.when(pl.program_id(0) == n_steps - 1)
        def _():
            dw_ref[...] = dw_s[...].astype(BF16)

    row = pl.BlockSpec((tm, C), lambda i: (i, 0))
    gsp = pl.BlockSpec((1, C), lambda i: (0, 0))
    wsp = pl.BlockSpec((2 * C, N), lambda i: (0, 0))
    return pl.pallas_call(
        body, name=name, grid=(n_steps,),
        in_specs=[pl.BlockSpec((tm, N), lambda i: (i, 0)), wsp, pl.BlockSpec((tm, 2 * C), lambda i: (i, 0)), row, row, gsp, gsp],
        out_specs=[row, row, gsp, gsp, wsp],
        out_shape=[jax.ShapeDtypeStruct((T, C), F32), jax.ShapeDtypeStruct((T, C), F32),
                   jax.ShapeDtypeStruct((1, C), F32), jax.ShapeDtypeStruct((1, C), F32),
                   jax.ShapeDtypeStruct((2 * C, N), BF16)],
        scratch_shapes=[pltpu.VMEM((2 * C, N), F32)],
        compiler_params=_cparams(("arbitrary",)),
    )(dout, w, ocat, xa, xb, ga, gb)


CONV_ROWS = 256
HALO = 8


def _conv_taps(w_ref):
    return w_ref[0:1, :], w_ref[1:2, :], w_ref[2:3, :]


def _conv_rows(cur, prev, w, bias):
    ext = jnp.concatenate([prev, cur], axis=0)
    u1 = pltpu.roll(ext, 1, 0)[HALO:]
    u2 = pltpu.roll(ext, 2, 0)[HALO:]
    return w[2] * cur + w[1] * u1 + w[0] * u2 + bias, u1, u2


def _conv_fwd(u, w, bias, *, seq, name):
    T = u.shape[0]
    B = T // seq
    W2 = 2 * FF_BLK

    def body(u_ref, w_ref, b_ref, a_ref):
        wv = _conv_taps(w_ref)
        bv = b_ref[...]
        for c in range(seq // CONV_ROWS):
            r0 = c * CONV_ROWS
            cur = u_ref[r0:r0 + CONV_ROWS, :]
            prev = u_ref[r0 - HALO:r0, :] if c > 0 else jnp.zeros((HALO, W2), F32)
            y, _, _ = _conv_rows(cur, prev, wv, bv)
            gc = y[:, :FF_BLK]
            a_ref[r0:r0 + CONV_ROWS, :] = (gc * (1.0 / (1.0 + jnp.exp(-gc))) * y[:, FF_BLK:]).astype(BF16)

    return pl.pallas_call(
        body, name=name, grid=(B, N_FF_BLK),
        in_specs=[pl.BlockSpec((seq, W2), lambda b, j: (b, j)), pl.BlockSpec((3, W2), lambda b, j: (0, j)),
                  pl.BlockSpec((1, W2), lambda b, j: (0, j))],
        out_specs=pl.BlockSpec((seq, FF_BLK), lambda b, j: (b, j)),
        out_shape=jax.ShapeDtypeStruct((T, D_FF), BF16),
        compiler_params=_cparams(("parallel", "parallel")),
    )(u, w, bias)


def _conv_bwd(u, dx2, wdn, w, bias, *, seq, name):
    T = u.shape[0]
    B = T // seq
    D = dx2.shape[1]
    W2 = 2 * FF_BLK
    nchunk = seq // CONV_ROWS

    def body(u_ref, dx_ref, wd_ref, w_ref, b_ref, du_ref, dw_ref, db_ref, dwd_ref, duc_s, dwd_s):
        wv = _conv_taps(w_ref)
        bv = b_ref[...]
        wd = wd_ref[...]
        zrow = jnp.zeros((1, W2), F32)
        dw0, dw1, dw2, dbs = zrow, zrow, zrow, zrow
        dwd = jnp.zeros((FF_BLK, D), F32)
        for c in range(nchunk):
            r0 = c * CONV_ROWS
            cur = u_ref[r0:r0 + CONV_ROWS, :]
            prev = u_ref[r0 - HALO:r0, :] if c > 0 else jnp.zeros((HALO, W2), F32)
            y, u1, u2 = _conv_rows(cur, prev, wv, bv)
            gc = y[:, :FF_BLK]
            vc = y[:, FF_BLK:]
            sg = 1.0 / (1.0 + jnp.exp(-gc))
            dxc = dx_ref[r0:r0 + CONV_ROWS, :].astype(BF16)
            dav = _dot_nt(dxc, wd)
            silu = gc * sg
            dwd = dwd + _dot_tn((silu * vc).astype(BF16), dxc)
            duc = jnp.concatenate([dav * vc * (sg * (1.0 + gc * (1.0 - sg))), dav * silu], axis=1)
            duc_s[r0:r0 + CONV_ROWS, :] = duc
            dw0 = dw0 + jnp.sum(duc * u2, axis=0, keepdims=True)
            dw1 = dw1 + jnp.sum(duc * u1, axis=0, keepdims=True)
            dw2 = dw2 + jnp.sum(duc * cur, axis=0, keepdims=True)
            dbs = dbs + jnp.sum(duc, axis=0, keepdims=True)
        duc_s[seq:seq + HALO, :] = jnp.zeros((HALO, W2), F32)
        n_ext = CONV_ROWS + HALO
        for c in range(nchunk):
            r0 = c * CONV_ROWS
            ext = duc_s[r0:r0 + n_ext, :]
            s1 = pltpu.roll(ext, n_ext - 1, 0)[:CONV_ROWS]
            s2 = pltpu.roll(ext, n_ext - 2, 0)[:CONV_ROWS]
            du_ref[r0:r0 + CONV_ROWS, :] = (wv[2] * ext[:CONV_ROWS] + wv[1] * s1 + wv[0] * s2).astype(BF16)

        b = pl.program_id(1)

        @pl.when(b == 0)
        def _():
            dw_ref[0:1, :] = dw0
            dw_ref[1:2, :] = dw1
            dw_ref[2:3, :] = dw2
            db_ref[...] = dbs
            dwd_s[...] = dwd

        @pl.when(b > 0)
        def _():
            dw_ref[0:1, :] += dw0
            dw_ref[1:2, :] += dw1
            dw_ref[2:3, :] += dw2
            db_ref[...] += dbs
            dwd_s[...] += dwd

        @pl.when(b == B - 1)
        def _():
            dwd_ref[...] = dwd_s[...].astype(BF16)

    return pl.pallas_call(
        body, name=name, grid=(N_FF_BLK, B),
        in_specs=[pl.BlockSpec((seq, W2), lambda j, b: (b, j)), pl.BlockSpec((seq, D), lambda j, b: (b, 0)),
                  pl.BlockSpec((FF_BLK, D), lambda j, b: (j, 0)),
                  pl.BlockSpec((3, W2), lambda j, b: (0, j)), pl.BlockSpec((1, W2), lambda j, b: (0, j))],
        out_specs=[pl.BlockSpec((seq, W2), lambda j, b: (b, j)), pl.BlockSpec((3, W2), lambda j, b: (0, j)),
                   pl.BlockSpec((1, W2), lambda j, b: (0, j)), pl.BlockSpec((FF_BLK, D), lambda j, b: (j, 0))],
        out_shape=[jax.ShapeDtypeStruct((T, 2 * D_FF), BF16), jax.ShapeDtypeStruct((3, 2 * D_FF), F32),
                   jax.ShapeDtypeStruct((1, 2 * D_FF), F32), jax.ShapeDtypeStruct((D_FF, D), BF16)],
        scratch_shapes=[pltpu.VMEM((seq + HALO, W2), F32), pltpu.VMEM((FF_BLK, D), F32)],
        compiler_params=_cparams(("parallel", "arbitrary")),
    )(u, dx2, wdn, w, bias)


def _place():
    return lax.axis_index("x"), lax.axis_index("y"), lax.axis_index("c")


def _other_chips(x, y):
    return [(1 - x, y), (x, 1 - y), (1 - x, 1 - y)]


def _all_gather(vs, *, name):
    n = len(vs)

    def body(*refs):
        v_refs, out_refs = refs[:n], refs[n:2 * n]
        send_sems, recv_sems, local_sems = refs[2 * n:]
        x, y, c = _place()
        me, sibling = (x, y, c), (x, y, 1 - c)
        chips = _other_chips(x, y)

        def slab(a, px, py, pc):
            return out_refs[a].at[4 * px + 2 * py + pc]

        def copy(a, k, block, to, src=None):
            return pltpu.make_async_remote_copy(
                src_ref=slab(a, *block) if src is None else src, dst_ref=slab(a, *block),
                send_sem=send_sems.at[7 * a + k], recv_sem=recv_sems.at[7 * a + k], device_id=to, device_id_type=MESH)

        mine = [pltpu.make_async_copy(v_refs[a], slab(a, *me), local_sems.at[a]) for a in range(n)]
        for cp in mine:
            cp.start()
        first = []
        for a in range(n):
            first.append(copy(a, 0, me, sibling, src=v_refs[a]))
            first += [copy(a, 1 + j, me, (*chip, c), src=v_refs[a]) for j, chip in enumerate(chips)]
        for cp in first:
            cp.start()
        passed = []
        for j, chip in enumerate(chips):
            for a in range(n):
                copy(a, 1 + j, (*chip, c), me).wait_recv()
                cp = copy(a, 4 + j, (*chip, c), sibling)
                cp.start()
                passed.append(cp)
        for a in range(n):
            copy(a, 0, sibling, me).wait_recv()
            for j, chip in enumerate(chips):
                copy(a, 4 + j, (*chip, 1 - c), me).wait_recv()
        for cp in first + passed:
            cp.wait_send()
        for cp in mine:
            cp.wait()

    return pl.pallas_call(
        body, name=name, in_specs=[ANY] * n, out_specs=[ANY] * n,
        out_shape=[jax.ShapeDtypeStruct((N_DEV,) + v.shape, v.dtype) for v in vs],
        scratch_shapes=[pltpu.SemaphoreType.DMA((7 * n,)), pltpu.SemaphoreType.DMA((7 * n,)), pltpu.SemaphoreType.DMA((n,))],
    )(*vs)


def _all_gather_async(vs, *, name, collective_id):
    n = len(vs)
    v_refs = [jax.new_ref(v, memory_space=pltpu.MemorySpace.HBM) for v in vs]
    out_refs = [jax.empty_ref(jax.ShapeDtypeStruct((N_DEV,) + v.shape, v.dtype), memory_space=pltpu.MemorySpace.HBM)
                for v in vs]

    @pl.kernel(mesh=plsc.ScalarSubcoreMesh(axis_name="seq", num_cores=1), name=name,
               scratch_types=(pltpu.SemaphoreType.DMA((7 * n,)), pltpu.SemaphoreType.DMA((7 * n,)),
                              pltpu.SemaphoreType.DMA((n,))),
               compiler_params=pltpu.CompilerParams(collective_id=collective_id))
    def launch(send_sems, recv_sems, local_sems):
        x, y, c = _place()
        me, sibling = (x, y, c), (x, y, 1 - c)
        chips = _other_chips(x, y)
        peers = [sibling] + [(*chip, c) for chip in chips]
        barrier = pltpu.get_barrier_semaphore()
        for peer in peers:
            pl.semaphore_signal(barrier, inc=1, device_id=peer, device_id_type=MESH)
        pl.semaphore_wait(barrier, len(peers))

        def slab(a, px, py, pc):
            return out_refs[a].at[4 * px + 2 * py + pc]

        def copy(a, k, block, to, src=None):
            return pltpu.make_async_remote_copy(
                src_ref=slab(a, *block) if src is None else src, dst_ref=slab(a, *block),
                send_sem=send_sems.at[7 * a + k], recv_sem=recv_sems.at[7 * a + k], device_id=to, device_id_type=MESH)

        mine = [pltpu.make_async_copy(v_refs[a], slab(a, *me), local_sems.at[a]) for a in range(n)]
        for cp in mine:
            cp.start()
        first = []
        for a in range(n):
            first.append(copy(a, 0, me, sibling, src=v_refs[a]))
            first += [copy(a, 1 + j, me, (*chip, c), src=v_refs[a]) for j, chip in enumerate(chips)]
        for cp in first:
            cp.start()
        passed = []
        for j, chip in enumerate(chips):
            for a in range(n):
                copy(a, 1 + j, (*chip, c), me).wait_recv()
                cp = copy(a, 4 + j, (*chip, c), sibling)
                cp.start()
                passed.append(cp)
        for a in range(n):
            copy(a, 0, sibling, me).wait_recv()
            for j, chip in enumerate(chips):
                copy(a, 4 + j, (*chip, 1 - c), me).wait_recv()
        for cp in first + passed:
            cp.wait_send()
        for cp in mine:
            cp.wait()

    launch()
    return [r[...] for r in out_refs]


def _handshake(peers):
    barrier = pltpu.get_barrier_semaphore()
    for peer in peers:
        pl.semaphore_signal(barrier, inc=1, device_id=peer, device_id_type=MESH)
    pl.semaphore_wait(barrier, len(peers))


def _hbm_refs(arrays, lead):
    src = [jax.new_ref(a, memory_space=pltpu.MemorySpace.HBM) for a in arrays]
    dst = [jax.empty_ref(jax.ShapeDtypeStruct((lead,) + a.shape[1:], a.dtype), memory_space=pltpu.MemorySpace.HBM)
           for a in arrays]
    return src, dst


def _rs_sibling_async(g8s, *, name, collective_id):
    n = len(g8s)
    g_refs, out_refs = _hbm_refs(g8s, 4)

    @pl.kernel(mesh=plsc.ScalarSubcoreMesh(axis_name="seq", num_cores=1), name=name,
               scratch_types=(pltpu.SemaphoreType.DMA((4 * n,)), pltpu.SemaphoreType.DMA((4 * n,))),
               compiler_params=pltpu.CompilerParams(collective_id=collective_id))
    def launch(send_sems, recv_sems):
        x, y, c = _place()
        _handshake([(x, y, 1 - c)])
        copies = [
            pltpu.make_async_remote_copy(
                src_ref=g_refs[a].at[2 * k + 1 - c], dst_ref=out_refs[a].at[k],
                send_sem=send_sems.at[4 * a + k], recv_sem=recv_sems.at[4 * a + k],
                device_id=(x, y, 1 - c), device_id_type=MESH)
            for a in range(n) for k in range(4)]
        for cp in copies:
            cp.start()
        for cp in copies:
            cp.wait()

    launch()
    return [r[...] for r in out_refs]


def _rs_chips_async(h4s, *, name, collective_id):
    n = len(h4s)
    h_refs, out_refs = _hbm_refs(h4s, 3)

    @pl.kernel(mesh=plsc.ScalarSubcoreMesh(axis_name="seq", num_cores=1), name=name,
               scratch_types=(pltpu.SemaphoreType.DMA((3 * n,)), pltpu.SemaphoreType.DMA((3 * n,))),
               compiler_params=pltpu.CompilerParams(collective_id=collective_id))
    def launch(send_sems, recv_sems):
        x, y, c = _place()
        chips = _other_chips(x, y)
        _handshake([(cx, cy, c) for cx, cy in chips])
        copies = [
            pltpu.make_async_remote_copy(
                src_ref=h_refs[a].at[2 * cx + cy], dst_ref=out_refs[a].at[j],
                send_sem=send_sems.at[3 * a + j], recv_sem=recv_sems.at[3 * a + j],
                device_id=(cx, cy, c), device_id_type=MESH)
            for a in range(n) for j, (cx, cy) in enumerate(chips)]
        for cp in copies:
            cp.start()
        for cp in copies:
            cp.wait()

    launch()
    return [r[...] for r in out_refs]


def _peer(x, y, c, k):
    return ((1 - x) if k & 4 else x, (1 - y) if k & 2 else y, (1 - c) if k & 1 else c)


def _rs_direct_async(g8s, *, name, collective_id):
    n = len(g8s)
    g_refs, out_refs = _hbm_refs(g8s, N_DEV - 1)

    @pl.kernel(mesh=plsc.ScalarSubcoreMesh(axis_name="seq", num_cores=1), name=name,
               scratch_types=(pltpu.SemaphoreType.DMA((7 * n,)), pltpu.SemaphoreType.DMA((7 * n,))),
               compiler_params=pltpu.CompilerParams(collective_id=collective_id))
    def launch(send_sems, recv_sems):
        x, y, c = _place()
        peers = [_peer(x, y, c, k) for k in range(1, N_DEV)]
        _handshake(peers)
        copies = [
            pltpu.make_async_remote_copy(
                src_ref=g_refs[a].at[4 * px + 2 * py + pc], dst_ref=out_refs[a].at[k],
                send_sem=send_sems.at[7 * a + k], recv_sem=recv_sems.at[7 * a + k],
                device_id=(px, py, pc), device_id_type=MESH)
            for a in range(n) for k, (px, py, pc) in enumerate(peers)]
        for cp in copies:
            cp.start()
        for cp in copies:
            cp.wait()

    launch()
    return [r[...] for r in out_refs]


def _row_tile(rows):
    if rows <= 512:
        return rows
    return next(t for t in (512, 384, 352, 256, 128) if rows % t == 0)


def _rs_chip_sum(g8, from_sibling, place_idx, *, name):
    _, R, C = g8.shape
    tr = _row_tile(R)

    def body(pi_ref, a_ref, b_ref, f_ref, h_ref):
        s = a_ref[...] + b_ref[...]
        h_ref[...] = s.astype(BF16)

        @pl.when(pl.program_id(1) == pi_ref[1])
        def _():
            f_ref[...] = s

    blk = pl.BlockSpec((None, tr, C), lambda r, k, pi_ref: (k, r, 0))
    return pl.pallas_call(
        body, name=name,
        grid_spec=pltpu.PrefetchScalarGridSpec(
            num_scalar_prefetch=1, grid=(R // tr, 4),
            in_specs=[pl.BlockSpec((None, tr, C), lambda r, k, pi_ref: (2 * k + pi_ref[0], r, 0)), blk],
            out_specs=[pl.BlockSpec((tr, C), lambda r, k, pi_ref: (r, 0)), blk]),
        out_shape=[jax.ShapeDtypeStruct((R, C), F32), jax.ShapeDtypeStruct((4, R, C), BF16)],
        compiler_params=_cparams(("parallel", "arbitrary")),
    )(place_idx, g8, from_sibling)


def _split_moves(segments, chunk):
    moves = []
    for dst, src, length in segments:
        while length > 0:
            dev, off = divmod(src, chunk)
            take = min(length, chunk - off)
            moves.append((dst, dev, off, take))
            dst, src, length = dst + take, src + take, length - take
    return moves


def _assemble(stacked, segments, zero_spans, out_cols, *, name):
    _, R, c = stacked.shape
    tr = _row_tile(R)
    moves = _split_moves(segments, c)

    def body(x_ref, o_ref):
        for dst, dev, off, take in moves:
            o_ref[:, dst:dst + take] = x_ref[dev, :, off:off + take]
        for a, b in zero_spans:
            o_ref[:, a:b] = jnp.zeros((tr, b - a), o_ref.dtype)

    return pl.pallas_call(
        body, name=name, grid=(R // tr,),
        in_specs=[pl.BlockSpec((N_DEV, tr, c), lambda i: (0, i, 0))],
        out_specs=pl.BlockSpec((tr, out_cols), lambda i: (i, 0)),
        out_shape=jax.ShapeDtypeStruct((R, out_cols), stacked.dtype),
        compiler_params=_cparams(("parallel",)),
    )(stacked)


def _disassemble(full, segments, chunk, *, name, out_dtype=F32):
    R = full.shape[0]
    tr = _row_tile(R)
    moves = _split_moves(segments, chunk)

    def body(x_ref, o_ref):
        seen = set()
        for dst, dev, off, take in moves:
            piece = x_ref[:, dst:dst + take]
            if (dev, off) in seen:
                piece = piece + o_ref[dev, :, off:off + take]
            seen.add((dev, off))
            o_ref[dev, :, off:off + take] = piece.astype(out_dtype)

    return pl.pallas_call(
        body, name=name, grid=(R // tr,),
        in_specs=[pl.BlockSpec((tr, full.shape[1]), lambda i: (i, 0))],
        out_specs=pl.BlockSpec((N_DEV, tr, chunk), lambda i: (0, i, 0)),
        out_shape=jax.ShapeDtypeStruct((N_DEV, R, chunk), out_dtype),
        compiler_params=_cparams(("parallel",)),
    )(full)


def _assemble_rows(stacked, segments, zero_spans, out_rows, *, name):
    _, c, R = stacked.shape
    tc = next(t for t in (2 * LANES, LANES) if R % t == 0)
    moves = _split_moves(segments, c)

    def body(x_ref, o_ref):
        for dst, dev, off, take in moves:
            o_ref[dst:dst + take, :] = x_ref[dev, off:off + take, :]
        for a, b in zero_spans:
            o_ref[a:b, :] = jnp.zeros((b - a, tc), o_ref.dtype)

    return pl.pallas_call(
        body, name=name, grid=(R // tc,),
        in_specs=[pl.BlockSpec((N_DEV, c, tc), lambda i: (0, 0, i))],
        out_specs=pl.BlockSpec((out_rows, tc), lambda i: (0, i)),
        out_shape=jax.ShapeDtypeStruct((out_rows, R), stacked.dtype),
        compiler_params=_cparams(("parallel",)),
    )(stacked)


def _disassemble_rows(full_t, segments, chunk, *, name, out_dtype=F32):
    R = full_t.shape[1]
    tc = next(t for t in (2 * LANES, LANES) if R % t == 0)
    moves = _split_moves(segments, chunk)

    def body(x_ref, o_ref):
        seen = set()
        for dst, dev, off, take in moves:
            piece = x_ref[dst:dst + take, :]
            if (dev, off) in seen:
                piece = piece + o_ref[dev, off:off + take, :]
            seen.add((dev, off))
            o_ref[dev, off:off + take, :] = piece.astype(out_dtype)

    return pl.pallas_call(
        body, name=name, grid=(R // tc,),
        in_specs=[pl.BlockSpec((full_t.shape[0], tc), lambda i: (0, i))],
        out_specs=pl.BlockSpec((N_DEV, chunk, tc), lambda i: (0, 0, i)),
        out_shape=jax.ShapeDtypeStruct((N_DEV, chunk, R), out_dtype),
        compiler_params=_cparams(("parallel",)),
    )(full_t)


_O_CQ = 3 * SB_W
_O_CKV = _O_CQ + Q_LORA
_O_KR = _O_CKV + KV_LORA
SEG_W_IN = ((0, 0, 3 * SB_W), (P_CKV, _O_CKV, KV_LORA), (P_KRT, _O_KR, MLA_ROPE), (P_KRT + MLA_ROPE, _O_KR, MLA_ROPE),
            (P_CQ, _O_CQ, Q_LORA))
ZERO_W_IN = ((P_KRT + 2 * MLA_ROPE, P_CQ),)
SEG_W_UQ = tuple((MLA_NOPE * h, MLA_QK * h, MLA_NOPE) for h in range(MLA_HEADS)) + tuple(
    (MLA_W + LANES * (h // 2) + MLA_ROPE * (h % 2), MLA_QK * h + MLA_NOPE, MLA_ROPE) for h in range(MLA_HEADS))
ZERO_W_UQ = tuple((MLA_W + LANES * g + 2 * MLA_ROPE, MLA_W + LANES * (g + 1)) for g in range(MLA_HEADS // 2))
SEG_W_UKV = tuple((MLA_NOPE * h, (MLA_NOPE + MLA_V) * h, MLA_NOPE) for h in range(MLA_HEADS)) + tuple(
    (MLA_W + MLA_V * h, (MLA_NOPE + MLA_V) * h + MLA_NOPE, MLA_V) for h in range(MLA_HEADS))
SEG_W_UP = tuple((2 * FF_BLK * blk + FF_BLK * half, D_FF * half + FF_BLK * blk, FF_BLK)
                 for half in range(2) for blk in range(N_FF_BLK))


def _sum8(g, *, name):
    _, R, C = g.shape

    def body(g_ref, o_ref):
        acc = g_ref[0]
        for k in range(1, N_DEV):
            acc = acc + g_ref[k]
        o_ref[...] = acc

    return pl.pallas_call(
        body, name=name, out_shape=jax.ShapeDtypeStruct((R, C), F32),
    )(g)


def _adamw_math(w, gf, m, v):
    c1 = 1.0 / (1.0 - ADAM_B1 ** ADAM_STEP)
    c2 = 1.0 / (1.0 - ADAM_B2 ** ADAM_STEP)
    mn = ADAM_B1 * m + (1.0 - ADAM_B1) * gf
    vn = ADAM_B2 * v + (1.0 - ADAM_B2) * (gf * gf)
    return -ADAM_LR * ((mn * c1) / (jnp.sqrt(vn * c2) + ADAM_EPS) + ADAM_WD * w), mn, vn


def _adamw(w, g, m, v, *, name):
    R, C = w.shape
    tr = _row_tile(R)

    def body(w_ref, g_ref, m_ref, v_ref, d_ref, mo_ref, vo_ref):
        d_ref[...], mo_ref[...], vo_ref[...] = _adamw_math(w_ref[...], g_ref[...], m_ref[...], v_ref[...])

    blk = pl.BlockSpec((tr, C), lambda i: (i, 0))
    shp = jax.ShapeDtypeStruct((R, C), F32)
    return pl.pallas_call(
        body, name=name, grid=(R // tr,), in_specs=[blk] * 4, out_specs=[blk] * 3,
        out_shape=[shp, shp, shp], compiler_params=_cparams(("parallel",)),
    )(w, g, m, v)


def _adamw_rs8(g8, r7, me_idx, w, m, v, *, name):
    R, C = w.shape
    tr = _row_tile(R)

    def body(i_ref, f_ref, r_ref, w_ref, m_ref, v_ref, g_ref, d_ref, mo_ref, vo_ref):
        gf = f_ref[...].astype(F32)
        for k in range(N_DEV - 1):
            gf = gf + r_ref[k].astype(F32)
        g_ref[...] = gf
        d_ref[...], mo_ref[...], vo_ref[...] = _adamw_math(w_ref[...], gf, m_ref[...], v_ref[...])

    blk = pl.BlockSpec((tr, C), lambda i, i_ref: (i, 0))
    shp = jax.ShapeDtypeStruct((R, C), F32)
    return pl.pallas_call(
        body, name=name,
        grid_spec=pltpu.PrefetchScalarGridSpec(
            num_scalar_prefetch=1, grid=(R // tr,),
            in_specs=[pl.BlockSpec((None, tr, C), lambda i, i_ref: (i_ref[0], i, 0)),
                      pl.BlockSpec((N_DEV - 1, tr, C), lambda i, i_ref: (0, i, 0)), blk, blk, blk],
            out_specs=[blk] * 4),
        out_shape=[shp] * 4, compiler_params=_cparams(("parallel",)),
    )(me_idx, g8, r7, w, m, v)


def _adamw_rs(own, r3, w, m, v, *, name):
    R, C = w.shape
    tr = _row_tile(R)

    def body(f_ref, r_ref, w_ref, m_ref, v_ref, g_ref, d_ref, mo_ref, vo_ref):
        gf = ((f_ref[...] + r_ref[0].astype(F32)) + r_ref[1].astype(F32)) + r_ref[2].astype(F32)
        g_ref[...] = gf
        d_ref[...], mo_ref[...], vo_ref[...] = _adamw_math(w_ref[...], gf, m_ref[...], v_ref[...])

    blk = pl.BlockSpec((tr, C), lambda i: (i, 0))
    shp = jax.ShapeDtypeStruct((R, C), F32)
    return pl.pallas_call(
        body, name=name, grid=(R // tr,),
        in_specs=[blk, pl.BlockSpec((3, tr, C), lambda i: (0, i, 0)), blk, blk, blk], out_specs=[blk] * 4,
        out_shape=[shp] * 4, compiler_params=_cparams(("parallel",)),
    )(own, r3, w, m, v)


def _ff_interleave(a):
    lead = a.shape[:-1]
    return a.reshape(*lead, 2, N_FF_BLK, FF_BLK).swapaxes(-3, -2).reshape(*lead, 2 * D_FF)


def _ff_deinterleave(a):
    lead = a.shape[:-1]
    return a.reshape(*lead, N_FF_BLK, 2, FF_BLK).swapaxes(-3, -2).reshape(*lead, 2 * D_FF)


SMALL =(("g_mix", D_MODEL), ("g_cq", Q_LORA), ("g_ckv", KV_LORA), ("g_sb_out", SB_W), ("g_mla_out", MLA_W),
         ("g_ffn", D_MODEL), ("conv_b", 2 * D_FF), ("g_final", D_MODEL))
SMALL_ROWS = 88


SMALL_USED = sum(size for _, size in SMALL)


def _pack_small(d, tail=None):
    parts = [d[n].reshape(-1) for n, _ in SMALL] + ([] if tail is None else [tail])
    flat = jnp.concatenate(parts)
    flat = jnp.pad(flat, (0, SMALL_ROWS * LANES - flat.shape[0]))
    return flat.reshape(SMALL_ROWS, LANES)


def _unpack_small(a):
    flat = a.reshape(-1)
    out, off = {}, 0
    for n, size in SMALL:
        out[n] = flat[off:off + size]
        off += size
    return out


def kernel(x, positions, g_mix, w_in, g_cq, w_uq, g_ckv, w_ukv, g_sb_out, g_mla_out, w_out, g_ffn, w_up, conv_w, conv_b, w_down, g_final, loss_target, m_g_mix, m_w_in, m_g_cq, m_w_uq, m_g_ckv, m_w_ukv, m_g_sb_out, m_g_mla_out, m_w_out, m_g_ffn, m_w_up, m_conv_w, m_conv_b, m_w_down, m_g_final, v_g_mix, v_w_in, v_g_cq, v_w_uq, v_g_ckv, v_w_ukv, v_g_sb_out, v_g_mla_out, v_w_out, v_g_ffn, v_w_up, v_conv_w, v_conv_b, v_w_down, v_g_final):
    B, S, D = x.shape
    T = B * S
    xf = x.reshape(T, D)
    tgt = loss_target.reshape(T, D)
    pos = positions.reshape(T, 1)
    half = MLA_ROPE // 2
    inv_freq = 1.0 / (ROPE_BASE ** (jnp.arange(half, dtype=F32) * (2.0 / MLA_ROPE)))
    invf = jnp.tile(inv_freq, LANES // half).reshape(1, LANES)
    place_idx = jnp.stack([lax.axis_index("c"), 2 * lax.axis_index("x") + lax.axis_index("y")]).astype(jnp.int32)
    me_idx = (4 * lax.axis_index("x") + 2 * lax.axis_index("y") + lax.axis_index("c")).astype(jnp.int32).reshape(1)

    names = ("w_in", "w_uq", "w_ukv", "w_out", "w_up", "w_down", "conv_w")
    shard = {"w_in": w_in[0], "w_uq": w_uq[0], "w_ukv": w_ukv[0], "w_out": w_out[0], "w_up": w_up[0],
             "w_down": w_down[0], "conv_w": conv_w[0]}
    sent = {n: shard[n] if n == "conv_w" else shard[n].astype(BF16) for n in names}
    later = names[1:]
    w_in_all = _all_gather([jnp.transpose(shard["w_in"]).astype(BF16)], name="ag_w_in")[0]
    w_in_all, rest = lax.optimization_barrier((w_in_all, [sent[n] for n in later]))
    got = {"w_in": w_in_all}
    got.update(zip(later, _all_gather_async(rest, name="ag_weights_async", collective_id=0)))
    wi_t = _assemble_rows(got["w_in"], SEG_W_IN, ZERO_W_IN, P_COLS, name="asm_w_in")
    wuq = _assemble(got["w_uq"], SEG_W_UQ, ZERO_W_UQ, 2 * MLA_W, name="asm_w_uq")
    wukv = _assemble(got["w_ukv"], SEG_W_UKV, (), 2 * MLA_W, name="asm_w_ukv")
    wup = _assemble(got["w_up"], SEG_W_UP, (), 2 * D_FF, name="asm_w_up")
    cwi = _assemble(got["conv_w"], SEG_W_UP, (), 2 * D_FF, name="asm_conv_w")
    wo = got["w_out"].reshape(D, D)
    wdn = got["w_down"].reshape(D_FF, D)
    cbi = _ff_interleave(conv_b)

    h, p = _rms_matmul_nn(xf, g_mix, wi_t, tm=512, name="proj_in", w_transposed=True)
    o_sb, ltot = _sb_fwd(p, seq=S, name="sb_fwd")
    cq, qm, krt = _proj_uq_rope(p, g_cq, wuq, pos, invf, tm=512, name="proj_uq")
    ckv, kvm = _rms_matmul_nn(p, g_ckv, wukv, tm=512, name="proj_ukv", col_block=P_CKV // KV_LORA, out_dtype=BF16)
    o_mla, lse = _mla_fwd(qm, kvm, krt, seq=S, name="mla_fwd")
    ocat, x1 = _heads_out(o_sb, o_mla, g_sb_out, g_mla_out, wo, xf, tm=512, name="proj_out")
    hf, u = _rms_matmul_nn(x1, g_ffn, wup, tm=256, name="ffn_up")
    a = _conv_fwd(u, cwi, cbi, seq=S, name="conv_fwd")
    dx2, dg_final, loss_row = _matmul_nn_loss(a, wdn, x1, g_final.reshape(1, D), tgt, tm=512, name="ffn_down_loss")

    du, dcw, dcb, dw_down = _conv_bwd(u, dx2, wdn, cwi, cbi, seq=S, name="conv_bwd")
    dw_up_t = _matmul_tn(du, hf, tm=D_FF, tn=1024, tk=1024, name="dw_up")
    dx1, dg_ffn = _matmul_nt_rms_bwd(du, wup, x1, g_ffn, tm=512, name="d_ffn_up", residual=dx2)
    do_sb, do_mla, dg_sb, dg_mla, dw_out = _heads_out_bwd(dx1, wo, ocat, o_sb, o_mla, g_sb_out, g_mla_out, tm=512,
                                                          name="d_proj_out")

    early = ("w_down", "w_up", "conv_w", "w_out")
    g8 = {"w_up": _disassemble_rows(dw_up_t, SEG_W_UP, shard["w_up"].shape[1], name="split_dw_up", out_dtype=BF16),
          "conv_w": _disassemble(dcw, SEG_W_UP, shard["conv_w"].shape[1], name="split_dconv_w", out_dtype=BF16),
          "w_out": dw_out.reshape((N_DEV,) + shard["w_out"].shape),
          "w_down": dw_down.reshape((N_DEV,) + shard["w_down"].shape)}
    r7 = dict(zip(early, _rs_direct_async([g8[n] for n in early], name="rs_direct_async", collective_id=1)))
    own, r3 = {}, {}

    dq_sb, dk_sb, dv_sb = _sb_bwd(p, ltot, do_sb, seq=S, name="sb_bwd")
    dqn, dqr, dkn, dvm, dkr = _mla_bwd(qm, kvm, krt, o_mla, lse, do_mla, seq=S, name="mla_bwd")
    dw_uq_t, dcq, dg_cq, dkr_u = _d_proj_uq_rope(dqn, dqr, dkr, wuq, cq, p, g_cq, pos, invf, tm=512, name="d_proj_uq")
    dw_ukv, dckv, dg_ckv = _d_proj_cat([dkn, dvm], wukv, p, g_ckv, tm=512, name="d_proj_ukv",
                                       col_block=P_CKV // KV_LORA, out_dtype=BF16, wgrad_act=ckv)
    dp, dx, dg_mix = _d_proj_cat([dq_sb, dk_sb, dv_sb, dckv, dkr_u, dcq], wi_t, xf, g_mix, tm=512, name="d_proj_in",
                                 residual=dx1, b_transposed=True)
    dw_in_t = _matmul_tn(dp, h, tm=P_COLS, tn=1024, tk=1024, name="dw_in")

    late = ("w_in", "w_uq", "w_ukv")
    g8.update({"w_in": _disassemble_rows(dw_in_t, SEG_W_IN, shard["w_in"].shape[1], name="split_dw_in"),
               "w_uq": _disassemble_rows(dw_uq_t, SEG_W_UQ, shard["w_uq"].shape[1], name="split_dw_uq"),
               "w_ukv": _disassemble(dw_ukv, SEG_W_UKV, shard["w_ukv"].shape[1], name="split_dw_ukv")})
    sib_l = _rs_sibling_async([g8[n] for n in late], name="rs_sibling_late", collective_id=3)

    params = {"w_in": (w_in, m_w_in, v_w_in), "w_uq": (w_uq, m_w_uq, v_w_uq), "w_ukv": (w_ukv, m_w_ukv, v_w_ukv),
              "w_out": (w_out, m_w_out, v_w_out), "w_up": (w_up, m_w_up, v_w_up), "conv_w": (conv_w, m_conv_w, v_conv_w),
              "w_down": (w_down, m_w_down, v_w_down)}
    grad, delta, new_m, new_v = {}, {}, {}, {}

    transposed = ("w_in", "w_uq", "w_up")

    def adamw_group(group):
        for n in group:
            flip = jnp.transpose if n in transposed else (lambda t: t)
            w_, m_, v_ = [flip(t[0]) for t in params[n]]
            if n in r7:
                res = _adamw_rs8(g8[n], r7[n], me_idx, w_, m_, v_, name="adamw_" + n)
            else:
                res = _adamw_rs(own[n], r3[n], w_, m_, v_, name="adamw_" + n)
            grad[n], delta[n], new_m[n], new_v[n] = [flip(r)[None] for r in res]

    adamw_group(("w_down", "w_out", "conv_w"))
    sib_l, grad["w_down"] = lax.optimization_barrier((sib_l, grad["w_down"]))
    sums_l = [_rs_chip_sum(g8[n], fs, place_idx, name="rs_chip_sum_" + n) for n, fs in zip(late, sib_l)]
    r3.update(zip(late, _rs_chips_async([h4 for _, h4 in sums_l], name="rs_chips_late", collective_id=4)))
    own.update({n: f for n, (f, _) in zip(late, sums_l)})
    small_part = {"g_mix": dg_mix, "g_cq": dg_cq, "g_ckv": dg_ckv, "g_sb_out": dg_sb, "g_mla_out": dg_mla,
                  "g_ffn": dg_ffn, "conv_b": _ff_deinterleave(dcb), "g_final": dg_final}
    small_all, = _all_gather_async([_pack_small(small_part, tail=loss_row[0, 0:1])], name="ag_small_async",
                                   collective_id=5)
    adamw_group(("w_up",))
    adamw_group(late)
    gsmall = _sum8(small_all, name="sum_small_grads")
    small_w = {"g_mix": g_mix, "g_cq": g_cq, "g_ckv": g_ckv, "g_sb_out": g_sb_out, "g_mla_out": g_mla_out,
               "g_ffn": g_ffn, "conv_b": conv_b, "g_final": g_final}
    small_m = {"g_mix": m_g_mix, "g_cq": m_g_cq, "g_ckv": m_g_ckv, "g_sb_out": m_g_sb_out, "g_mla_out": m_g_mla_out,
               "g_ffn": m_g_ffn, "conv_b": m_conv_b, "g_final": m_g_final}
    small_v = {"g_mix": v_g_mix, "g_cq": v_g_cq, "g_ckv": v_g_ckv, "g_sb_out": v_g_sb_out, "g_mla_out": v_g_mla_out,
               "g_ffn": v_g_ffn, "conv_b": v_conv_b, "g_final": v_g_final}
    ds_, ms_, vs_ = _adamw(_pack_small(small_w), gsmall, _pack_small(small_m), _pack_small(small_v), name="adamw_small")
    for src, dst in ((_unpack_small(gsmall), grad), (_unpack_small(ds_), delta), (_unpack_small(ms_), new_m), (_unpack_small(vs_), new_v)):
        for n, _ in SMALL:
            dst[n] = src[n].reshape(small_w[n].shape)

    loss = gsmall.reshape(-1)[SMALL_USED]
    order = ("g_mix", "w_in", "g_cq", "w_uq", "g_ckv", "w_ukv", "g_sb_out", "g_mla_out", "w_out", "g_ffn", "w_up",
             "conv_w", "conv_b", "w_down", "g_final")
    return (loss, dx.reshape(B, S, D), *[grad[n] for n in order], *[delta[n] for n in order],
            *[new_m[n] for n in order], *[new_v[n] for n in order])
```

```python
import jax
import jax.numpy as jnp
from jax import lax
from jax.experimental import pallas as pl
from jax.experimental.pallas import tpu as pltpu
from jax.experimental.pallas import tpu_sc as plsc

F32 = jnp.float32
BF16 = jnp.bfloat16

D_MODEL = 1024
SB_HEADS = 8
SB_HEAD_DIM = 64
MLA_HEADS = 8
MLA_NOPE = 64
MLA_ROPE = 32
MLA_V = 64
Q_LORA = 384
KV_LORA = 256
D_FF = 2816
ROPE_BASE = 10000.0
EPS = 1e-6
SB_W = SB_HEADS * SB_HEAD_DIM
MLA_W = MLA_HEADS * MLA_V
MLA_QK = MLA_NOPE + MLA_ROPE

ADAM_LR = 0.001
ADAM_B1 = 0.9
ADAM_B2 = 0.999
ADAM_EPS = 1e-08
ADAM_WD = 0.01
ADAM_STEP = 10

N_DEV = 8
LANES = 128
V7X_VMEM_LIMIT = 56 * 1024 * 1024
FF_BLK = 256
N_FF_BLK = D_FF // FF_BLK

P_Q, P_K, P_V = 0, SB_W, 2 * SB_W
P_CKV = 3 * SB_W
P_KRT = P_CKV + KV_LORA
P_CQ = P_KRT + LANES
P_COLS = P_CQ + Q_LORA

MESH = pl.DeviceIdType.MESH
ANY = pl.BlockSpec(memory_space=pl.ANY)


def _cparams(sem=None, vmem=V7X_VMEM_LIMIT):
    return pltpu.CompilerParams(dimension_semantics=sem, vmem_limit_bytes=vmem)


def _matmul_tn(a, b, *, tm, tn, tk, name, out_dtype=F32):
    K, M = a.shape
    N = b.shape[1]
    assert M % tm == 0 and N % tn == 0 and K % tk == 0, (name, a.shape, b.shape)
    n_k = K // tk
    narrow = out_dtype != F32

    def body(a_ref, b_ref, o_ref, *scratch):
        acc_ref = scratch[0] if narrow else o_ref
        k = pl.program_id(2)
        part = lax.dot_general(a_ref[...].astype(BF16), b_ref[...].astype(BF16), (((0,), (0,)), ((), ())),
                               preferred_element_type=F32)

        @pl.when(k == 0)
        def _():
            acc_ref[...] = part

        @pl.when(k > 0)
        def _():
            acc_ref[...] += part

        if narrow:
            @pl.when(k == n_k - 1)
            def _():
                o_ref[...] = acc_ref[...].astype(out_dtype)

    return pl.pallas_call(
        body, name=name, grid=(M // tm, N // tn, n_k),
        in_specs=[pl.BlockSpec((tk, tm), lambda i, j, k: (k, i)), pl.BlockSpec((tk, tn), lambda i, j, k: (k, j))],
        out_specs=pl.BlockSpec((tm, tn), lambda i, j, k: (i, j)),
        out_shape=jax.ShapeDtypeStruct((M, N), out_dtype),
        scratch_shapes=[pltpu.VMEM((tm, tn), F32)] if narrow else [],
        compiler_params=_cparams(("parallel", "parallel", "arbitrary")),
    )(a, b)


def _rms(xf, g):
    r = lax.rsqrt(jnp.mean(xf * xf, axis=1, keepdims=True) + EPS)
    return (xf * r) * g


def _rms_grad(dyf, xf, g):
    r = lax.rsqrt(jnp.mean(xf * xf, axis=1, keepdims=True) + EPS)
    xh = xf * r
    dyg = dyf * g
    dx = r * (dyg - xh * jnp.mean(dyg * xh, axis=1, keepdims=True))
    return dx, jnp.sum(dyf * xh, axis=0, keepdims=True)


def _accumulate(ref, part):
    @pl.when(pl.program_id(0) == 0)
    def _():
        ref[...] = part

    @pl.when(pl.program_id(0) > 0)
    def _():
        ref[...] += part


def _rms_matmul_nn(x, g, w, *, tm, name, col_block=0, out_dtype=F32, w_transposed=False):
    T = x.shape[0]
    C, N = w.shape[::-1] if w_transposed else w.shape
    assert T % tm == 0, (name, x.shape)
    contract = (((1,), (1,)), ((), ())) if w_transposed else (((1,), (0,)), ((), ()))

    def body(x_ref, g_ref, w_ref, h_ref, o_ref):
        hb = _rms(x_ref[...], g_ref[...]).astype(BF16)
        h_ref[...] = hb
        o_ref[...] = lax.dot_general(hb, w_ref[...], contract, preferred_element_type=F32).astype(out_dtype)

    return pl.pallas_call(
        body, name=name, grid=(T // tm,),
        in_specs=[pl.BlockSpec((tm, C), lambda i: (i, col_block)), pl.BlockSpec((1, C), lambda i: (0, 0)),
                  pl.BlockSpec(w.shape, lambda i: (0, 0))],
        out_specs=[pl.BlockSpec((tm, C), lambda i: (i, 0)), pl.BlockSpec((tm, N), lambda i: (i, 0))],
        out_shape=[jax.ShapeDtypeStruct((T, C), BF16), jax.ShapeDtypeStruct((T, N), out_dtype)],
        compiler_params=_cparams(("parallel",)),
    )(x, g, w)


def _matmul_nt_rms_bwd(a, b, x, g, *, tm, name, residual=None, col_block=0, out_dtype=F32):
    M, K = a.shape
    C = b.shape[0]
    assert M % tm == 0, (name, a.shape)
    in_specs = [pl.BlockSpec((tm, K), lambda i: (i, 0)), pl.BlockSpec((C, K), lambda i: (0, 0)),
                pl.BlockSpec((tm, C), lambda i: (i, col_block)), pl.BlockSpec((1, C), lambda i: (0, 0))]
    args = [a, b, x, g]
    if residual is not None:
        in_specs.append(pl.BlockSpec((tm, C), lambda i: (i, 0)))
        args.append(residual)

    def body(*refs):
        a_ref, b_ref, x_ref, g_ref = refs[:4]
        dx_ref, dg_ref = refs[-2:]
        dy = lax.dot_general(a_ref[...].astype(BF16), b_ref[...], (((1,), (1,)), ((), ())), preferred_element_type=F32)
        dx, part = _rms_grad(dy, x_ref[...], g_ref[...])
        if residual is not None:
            dx = dx + refs[4][...]
        dx_ref[...] = dx.astype(out_dtype)
        _accumulate(dg_ref, part)

    return pl.pallas_call(
        body, name=name, grid=(M // tm,), in_specs=in_specs,
        out_specs=[pl.BlockSpec((tm, C), lambda i: (i, 0)), pl.BlockSpec((1, C), lambda i: (0, 0))],
        out_shape=[jax.ShapeDtypeStruct((M, C), out_dtype), jax.ShapeDtypeStruct((1, C), F32)],
        compiler_params=_cparams(("arbitrary",)),
    )(*args)


def _matmul_nn_loss(a, w, x1, g, tgt, *, tm, name):
    M, K = a.shape
    C = w.shape[1]
    assert M % tm == 0, (name, a.shape)

    nsub = 4
    ts = tm // nsub

    def body(a_ref, w_ref, x_ref, g_ref, t_ref, dx_ref, dg_ref, loss_ref):
        gf = g_ref[...]
        wv = w_ref[...]
        rows = [slice(r * ts, (r + 1) * ts) for r in range(nsub)]
        xs = [x_ref[rw, :] + jnp.dot(a_ref[rw, :], wv, preferred_element_type=F32) for rw in rows]
        lpart, gpart = 0.0, 0.0
        for rw, xf in zip(rows, xs):
            err = _rms(xf, gf) - t_ref[rw, :]
            lpart = lpart + 0.5 * jnp.sum(jnp.mean(err * err, axis=1, keepdims=True), axis=0, keepdims=True)
            dx, gp = _rms_grad(err * (1.0 / C), xf, gf)
            dx_ref[rw, :] = dx
            gpart = gpart + gp
        _accumulate(dg_ref, gpart)
        _accumulate(loss_ref, jnp.broadcast_to(lpart, (1, LANES)))

    row = pl.BlockSpec((tm, C), lambda i: (i, 0))
    return pl.pallas_call(
        body, name=name, grid=(M // tm,),
        in_specs=[pl.BlockSpec((tm, K), lambda i: (i, 0)), pl.BlockSpec((K, C), lambda i: (0, 0)), row,
                  pl.BlockSpec((1, C), lambda i: (0, 0)), row],
        out_specs=[row, pl.BlockSpec((1, C), lambda i: (0, 0)), pl.BlockSpec((1, LANES), lambda i: (0, 0))],
        out_shape=[jax.ShapeDtypeStruct((M, C), F32), jax.ShapeDtypeStruct((1, C), F32),
                   jax.ShapeDtypeStruct((1, LANES), F32)],
        compiler_params=_cparams(("arbitrary",)),
    )(a, w, x1, g, tgt)


ATT_T = 256
ATT_PAIRS = 2
NEG_BIG = -1e30


def _lane_iota():
    return lax.broadcasted_iota(jnp.int32, (1, LANES), 1)


def _head_masks():
    first = _lane_iota() < SB_HEAD_DIM
    return first, jnp.logical_not(first)


def _pick(mask, x):
    return jnp.where(mask, x, jnp.zeros_like(x))


def _lane_value(t, lane):
    return jnp.sum(jnp.where(_lane_iota() == lane, t, 0.0), axis=1, keepdims=True)


def _split_hi_lo(x):
    hi = x.astype(BF16)
    lo = (x - hi.astype(F32)).astype(BF16)
    return jnp.concatenate([hi, lo], axis=1)


def _tri(n, kind):
    r = lax.broadcasted_iota(jnp.int32, (n, n), 0)
    c = lax.broadcasted_iota(jnp.int32, (n, n), 1)
    u = {"suffix_excl": r > c, "prefix_incl": r <= c, "prefix_excl": r < c}[kind].astype(BF16)
    return jnp.concatenate([u, u], axis=0)


def _dot_nt(a, b):
    return lax.dot_general(a, b, (((1,), (1,)), ((), ())), preferred_element_type=F32)


def _dot_tn(a, b):
    return lax.dot_general(a, b, (((0,), (0,)), ((), ())), preferred_element_type=F32)


def _dot(a, b):
    return jnp.dot(a, b, preferred_element_type=F32)


def _causal_mask(n, strict):
    r = lax.broadcasted_iota(jnp.int32, (n, n), 0)
    c = lax.broadcasted_iota(jnp.int32, (n, n), 1)
    return (c < r) if strict else (c <= r)


LOG2E = 1.4426950408889634


def _sb_logs(qh, kj, vis):
    z2 = _dot_nt(qh, kj) * LOG2E
    nk = jnp.maximum(z2, 0.0) + jnp.log2(1.0 + jnp.exp2(-jnp.abs(z2)))
    lb = z2 - nk
    if vis is not None:
        nk = jnp.where(vis, nk, 0.0)
    return lb, nk


def _sb_fwd(p, *, seq, name):
    T = p.shape[0]
    B = T // seq
    TQ = ATT_T
    nq = seq // TQ
    PP = ATT_PAIRS
    W = PP * LANES
    nstep = SB_W // W
    NH = 2 * PP

    def body(q_ref, k_ref, v_ref, o_ref, lt_ref, q_s, k_s, v_s):
        masks = _head_masks()
        q = q_ref[...] * (SB_HEAD_DIM ** -0.5)
        v = v_ref[...]
        k_s[...] = k_ref[...].astype(BF16)
        for h in range(NH):
            ps = slice((h // 2) * LANES, (h // 2 + 1) * LANES)
            hs = slice(h * LANES, (h + 1) * LANES)
            q_s[:, hs] = _pick(masks[h % 2], q[:, ps]).astype(BF16)
            v_s[:, hs] = _pick(masks[h % 2], v[:, ps]).astype(BF16)
        u_suf = _tri(TQ, "suffix_excl")
        vis = _causal_mask(TQ, True)

        def q_block(i, carry):
            q0 = pl.multiple_of(i * TQ, TQ)
            qs = [q_s[pl.ds(q0, TQ), h * LANES:(h + 1) * LANES] for h in range(NH)]

            def tile(k0, c, mask):
                rs, accs = list(c[:NH]), list(c[NH:])
                logs = [_sb_logs(qs[h], k_s[pl.ds(k0, TQ), (h // 2) * LANES:(h // 2 + 1) * LANES], mask) for h in range(NH)]
                sums = [_dot(_split_hi_lo(nk), u_suf) for _, nk in logs]
                for h in range(NH):
                    a = jnp.exp2(logs[h][0] - sums[h] - rs[h])
                    if mask is not None:
                        a = jnp.where(mask, a, 0.0)
                    accs[h // 2] = accs[h // 2] + _dot(a.astype(BF16), v_s[pl.ds(k0, TQ), h * LANES:(h + 1) * LANES])
                    rs[h] = rs[h] + jnp.sum(logs[h][1], axis=1, keepdims=True)
                return tuple(rs) + tuple(accs)

            zero = jnp.zeros((TQ, 1), F32)
            c = tile(q0, (zero,) * NH + (jnp.zeros((TQ, LANES), F32),) * PP, vis)

            def k_block(jj, c):
                return tile(pl.multiple_of((i - 1 - jj) * TQ, TQ), c, None)

            c = lax.fori_loop(0, i, k_block, c)
            for pr in range(PP):
                ps = slice(pr * LANES, (pr + 1) * LANES)
                o_ref[pl.ds(q0, TQ), ps] = c[NH + pr]
                lt_ref[pl.ds(q0, TQ), ps] = jnp.where(masks[0], c[2 * pr], c[2 * pr + 1])
            return carry

        lax.fori_loop(0, nq, q_block, 0)

    blk = lambda off: pl.BlockSpec((seq, W), lambda b, g: (b, off + g))
    out_blk = pl.BlockSpec((seq, W), lambda b, g: (b, g))
    return pl.pallas_call(
        body, name=name, grid=(B, nstep),
        in_specs=[blk(P_Q // W), blk(P_K // W), blk(P_V // W)],
        out_specs=[out_blk, out_blk],
        out_shape=[jax.ShapeDtypeStruct((T, SB_W), F32), jax.ShapeDtypeStruct((T, SB_W), F32)],
        scratch_shapes=[pltpu.VMEM((seq, NH * LANES), BF16), pltpu.VMEM((seq, W), BF16), pltpu.VMEM((seq, NH * LANES), BF16)],
        compiler_params=_cparams(("parallel", "parallel")),
    )(p, p, p)


def _sb_bwd(p, ltot, do, *, seq, name):
    T = p.shape[0]
    B = T // seq
    TQ = ATT_T
    nq = seq // TQ
    PP = ATT_PAIRS
    W = PP * LANES
    nstep = SB_W // W
    NH = 2 * PP
    scale = SB_HEAD_DIM ** -0.5

    def body(q_ref, k_ref, v_ref, lt_ref, do_ref, dq_ref, dk_ref, dv_ref, q_s, k_s, v_s, do_s, dk_s, dv_s):
        masks = _head_masks()
        q = q_ref[...] * scale
        dof = do_ref[...]
        k_s[...] = k_ref[...].astype(BF16)
        v_s[...] = v_ref[...].astype(BF16)
        for h in range(NH):
            ps = slice((h // 2) * LANES, (h // 2 + 1) * LANES)
            hs = slice(h * LANES, (h + 1) * LANES)
            q_s[:, hs] = _pick(masks[h % 2], q[:, ps]).astype(BF16)
            do_s[:, hs] = _pick(masks[h % 2], dof[:, ps]).astype(BF16)
        dk_s[...] = jnp.zeros_like(dk_s)
        dv_s[...] = jnp.zeros_like(dv_s)
        u_pin = _tri(TQ, "prefix_incl")
        u_pex = _tri(TQ, "prefix_excl")[:TQ]
        vis = _causal_mask(TQ, True)

        def q_block(i, carry):
            q0 = pl.multiple_of(i * TQ, TQ)
            qs = [q_s[pl.ds(q0, TQ), h * LANES:(h + 1) * LANES] for h in range(NH)]
            dos = [do_s[pl.ds(q0, TQ), h * LANES:(h + 1) * LANES] for h in range(NH)]
            lt = lt_ref[pl.ds(q0, TQ), :]
            lts = [_lane_value(lt[:, (h // 2) * LANES:(h // 2 + 1) * LANES], (h % 2) * SB_HEAD_DIM) for h in range(NH)]

            def tile(k0, c, mask):
                cs, gs, accs = list(c[:NH]), list(c[NH:2 * NH]), list(c[2 * NH:])
                kjs = [k_s[pl.ds(k0, TQ), pr * LANES:(pr + 1) * LANES] for pr in range(PP)]
                vjs = [v_s[pl.ds(k0, TQ), pr * LANES:(pr + 1) * LANES] for pr in range(PP)]
                logs = [_sb_logs(qs[h], kjs[h // 2], mask) for h in range(NH)]
                pins = [_dot(_split_hi_lo(nk), u_pin) for _, nk in logs]
                das = [_dot_nt(dos[h], vjs[h // 2]) for h in range(NH)]
                a_l, g_l = [], []
                for h in range(NH):
                    a = jnp.exp2(logs[h][0] - ((lts[h] - cs[h]) - pins[h]))
                    if mask is not None:
                        a = jnp.where(mask, a, 0.0)
                    a_l.append(a)
                    g_l.append(das[h] * a)
                pres = [_dot(g.astype(BF16), u_pex) for g in g_l]
                dz_l = []
                for h in range(NH):
                    dz = g_l[h] - jnp.exp2(logs[h][0]) * (g_l[h] + (pres[h] + gs[h]))
                    if mask is not None:
                        dz = jnp.where(mask, dz, 0.0)
                    dz_l.append(dz.astype(BF16))
                for h in range(NH):
                    accs[h] = accs[h] + _dot(dz_l[h], kjs[h // 2])
                for pr in range(PP):
                    ps = slice(pr * LANES, (pr + 1) * LANES)
                    ha, hb = 2 * pr, 2 * pr + 1
                    dk_s[ps, pl.ds(k0, TQ)] += _dot_tn(qs[ha], dz_l[ha]) + _dot_tn(qs[hb], dz_l[hb])
                    dv_s[ps, pl.ds(k0, TQ)] += _dot_tn(dos[ha], a_l[ha].astype(BF16)) + _dot_tn(dos[hb], a_l[hb].astype(BF16))
                for h in range(NH):
                    cs[h] = cs[h] + jnp.sum(logs[h][1], axis=1, keepdims=True)
                    gs[h] = gs[h] + jnp.sum(g_l[h], axis=1, keepdims=True)
                return tuple(cs) + tuple(gs) + tuple(accs)

            z1 = jnp.zeros((TQ, 1), F32)
            zl = jnp.zeros((TQ, LANES), F32)

            def k_block(j, c):
                return tile(pl.multiple_of(j * TQ, TQ), c, None)

            c = lax.fori_loop(0, i, k_block, (z1,) * (2 * NH) + (zl,) * NH)
            c = tile(q0, c, vis)
            for pr in range(PP):
                dq = jnp.where(masks[0], c[2 * NH + 2 * pr], c[2 * NH + 2 * pr + 1]) * scale
                dq_ref[pl.ds(q0, TQ), pr * LANES:(pr + 1) * LANES] = dq.astype(BF16)
            return carry

        lax.fori_loop(0, nq, q_block, 0)
        dk_ref[...] = dk_s[...].T.astype(BF16)
        dv_ref[...] = dv_s[...].T.astype(BF16)

    blk = lambda off: pl.BlockSpec((seq, W), lambda b, g: (b, off + g))
    out_blk = pl.BlockSpec((seq, W), lambda b, g: (b, g))
    return pl.pallas_call(
        body, name=name, grid=(B, nstep),
        in_specs=[blk(P_Q // W), blk(P_K // W), blk(P_V // W), out_blk, out_blk],
        out_specs=[out_blk, out_blk, out_blk],
        out_shape=[jax.ShapeDtypeStruct((T, SB_W), BF16) for _ in range(3)],
        scratch_shapes=[pltpu.VMEM((seq, NH * LANES), BF16), pltpu.VMEM((seq, W), BF16), pltpu.VMEM((seq, W), BF16),
                        pltpu.VMEM((seq, NH * LANES), BF16), pltpu.VMEM((W, seq), F32), pltpu.VMEM((W, seq), F32)],
        compiler_params=_cparams(("parallel", "parallel")),
    )(p, p, p, ltot, do)


def _mla_masks():
    lane = lax.broadcasted_iota(jnp.int32, (1, 2 * LANES), 1)
    ma = (lane < MLA_NOPE) | ((lane >= LANES) & (lane < LANES + MLA_ROPE))
    mb = ((lane >= MLA_NOPE) & (lane < LANES)) | ((lane >= LANES + MLA_ROPE) & (lane < LANES + 2 * MLA_ROPE))
    return ma, mb


def _mla_fwd(qm, kvm, krt, *, seq, name):
    T = qm.shape[0]
    B = T // seq
    TQ = ATT_T
    nq = seq // TQ
    PP = ATT_PAIRS
    W = PP * LANES
    nstep = MLA_W // W
    NH = 2 * PP
    CW = 2 * LANES
    scale = MLA_QK ** -0.5

    def body(qn_ref, qr_ref, kn_ref, v_ref, kr_ref, o_ref, lse_ref, q_s, kc_s, v_s):
        hm = _head_masks()
        mm = _mla_masks()
        v = v_ref[...]
        for pr in range(PP):
            ps = slice(pr * LANES, (pr + 1) * LANES)
            qc = jnp.concatenate([qn_ref[:, ps], qr_ref[:, ps]], axis=1)
            kc_s[:, pr * CW:(pr + 1) * CW] = jnp.concatenate([kn_ref[:, ps], kr_ref[...]], axis=1)
            for e in range(2):
                h = 2 * pr + e
                q_s[:, h * CW:(h + 1) * CW] = _pick(mm[e], qc)
                v_s[:, h * LANES:(h + 1) * LANES] = _pick(hm[e], v[:, ps])
        vis = _causal_mask(TQ, False)

        def q_block(i, carry):
            q0 = pl.multiple_of(i * TQ, TQ)
            qs = [q_s[pl.ds(q0, TQ), h * CW:(h + 1) * CW] for h in range(NH)]

            def tile(k0, c, mask):
                ms, ls, accs = list(c[:NH]), list(c[NH:2 * NH]), list(c[2 * NH:])
                ss = [_dot_nt(qs[h], kc_s[pl.ds(k0, TQ), (h // 2) * CW:(h // 2 + 1) * CW]) * scale for h in range(NH)]
                if mask is not None:
                    ss = [jnp.where(mask, s, NEG_BIG) for s in ss]
                m_new = [jnp.maximum(ms[h], jnp.max(ss[h], axis=1, keepdims=True)) for h in range(NH)]
                alphas = [jnp.exp(ms[h] - m_new[h]) for h in range(NH)]
                prs = [jnp.exp(ss[h] - m_new[h]) for h in range(NH)]
                outs = [_dot(prs[h].astype(BF16), v_s[pl.ds(k0, TQ), h * LANES:(h + 1) * LANES]) for h in range(NH)]
                ls = [alphas[h] * ls[h] + jnp.sum(prs[h], axis=1, keepdims=True) for h in range(NH)]
                for pr in range(PP):
                    accs[pr] = accs[pr] * jnp.where(hm[0], alphas[2 * pr], alphas[2 * pr + 1]) + outs[2 * pr] + outs[2 * pr + 1]
                return tuple(m_new) + tuple(ls) + tuple(accs)

            neg = jnp.full((TQ, 1), NEG_BIG, F32)
            z1 = jnp.zeros((TQ, 1), F32)

            def k_block(j, c):
                return tile(pl.multiple_of(j * TQ, TQ), c, None)

            c = lax.fori_loop(0, i, k_block, (neg,) * NH + (z1,) * NH + (jnp.zeros((TQ, LANES), F32),) * PP)
            c = tile(q0, c, vis)
            for pr in range(PP):
                ps = slice(pr * LANES, (pr + 1) * LANES)
                m_a, m_b, l_a, l_b = c[2 * pr], c[2 * pr + 1], c[NH + 2 * pr], c[NH + 2 * pr + 1]
                o_ref[pl.ds(q0, TQ), ps] = c[2 * NH + pr] / jnp.where(hm[0], l_a, l_b)
                lse_ref[pl.ds(q0, TQ), ps] = jnp.where(hm[0], m_a + jnp.log(l_a), m_b + jnp.log(l_b))
            return carry

        lax.fori_loop(0, nq, q_block, 0)

    blk = lambda off: pl.BlockSpec((seq, W), lambda b, g: (b, off + g))
    out_blk = pl.BlockSpec((seq, W), lambda b, g: (b, g))
    return pl.pallas_call(
        body, name=name, grid=(B, nstep),
        in_specs=[blk(0), blk(nstep), blk(0), blk(nstep), pl.BlockSpec((seq, LANES), lambda b, g: (b, 0))],
        out_specs=[out_blk, out_blk],
        out_shape=[jax.ShapeDtypeStruct((T, MLA_W), F32), jax.ShapeDtypeStruct((T, MLA_W), F32)],
        scratch_shapes=[pltpu.VMEM((seq, NH * CW), BF16), pltpu.VMEM((seq, PP * CW), BF16), pltpu.VMEM((seq, NH * LANES), BF16)],
        compiler_params=_cparams(("parallel", "parallel")),
    )(qm, qm, kvm, kvm, krt)


def _mla_bwd(qm, kvm, krt, o, lse, do, *, seq, name):
    T = qm.shape[0]
    B = T // seq
    TQ = ATT_T
    nq = seq // TQ
    PP = ATT_PAIRS
    W = PP * LANES
    nstep = MLA_W // W
    NH = 2 * PP
    CW = 2 * LANES
    scale = MLA_QK ** -0.5

    def body(qn_ref, qr_ref, kn_ref, v_ref, kr_ref, o_ref, lse_ref, do_ref,
             dqn_ref, dqr_ref, dkn_ref, dv_ref, dkr_ref, q_s, kc_s, do_s, dkc_s, dv_s):
        hm = _head_masks()
        mm = _mla_masks()
        dof = do_ref[...]
        for pr in range(PP):
            ps = slice(pr * LANES, (pr + 1) * LANES)
            qc = jnp.concatenate([qn_ref[:, ps], qr_ref[:, ps]], axis=1)
            kc_s[:, pr * CW:(pr + 1) * CW] = jnp.concatenate([kn_ref[:, ps], kr_ref[...]], axis=1)
            for e in range(2):
                h = 2 * pr + e
                q_s[:, h * CW:(h + 1) * CW] = _pick(mm[e], qc)
                do_s[:, h * LANES:(h + 1) * LANES] = _pick(hm[e], dof[:, ps]).astype(BF16)
        dkc_s[...] = jnp.zeros_like(dkc_s)
        dv_s[...] = jnp.zeros_like(dv_s)
        vis = _causal_mask(TQ, False)

        def q_block(i, carry):
            q0 = pl.multiple_of(i * TQ, TQ)
            qs = [q_s[pl.ds(q0, TQ), h * CW:(h + 1) * CW] for h in range(NH)]
            dos = [do_s[pl.ds(q0, TQ), h * LANES:(h + 1) * LANES] for h in range(NH)]
            lse_t = lse_ref[pl.ds(q0, TQ), :]
            dd = do_ref[pl.ds(q0, TQ), :] * o_ref[pl.ds(q0, TQ), :]
            lses, ds_ = [], []
            for h in range(NH):
                ps = slice((h // 2) * LANES, (h // 2 + 1) * LANES)
                lses.append(_lane_value(lse_t[:, ps], (h % 2) * MLA_V))
                ds_.append(jnp.sum(_pick(hm[h % 2], dd[:, ps]), axis=1, keepdims=True))

            def tile(k0, c, mask):
                accs = list(c)
                kcs = [kc_s[pl.ds(k0, TQ), pr * CW:(pr + 1) * CW] for pr in range(PP)]
                vjs = [v_ref[pl.ds(k0, TQ), pr * LANES:(pr + 1) * LANES] for pr in range(PP)]
                ss = [_dot_nt(qs[h], kcs[h // 2]) * scale for h in range(NH)]
                dps = [_dot_nt(dos[h], vjs[h // 2]) for h in range(NH)]
                p_l, ds_l = [], []
                for h in range(NH):
                    pr_ = jnp.exp(ss[h] - lses[h])
                    if mask is not None:
                        pr_ = jnp.where(mask, pr_, 0.0)
                    p_l.append(pr_.astype(BF16))
                    ds_l.append((pr_ * (dps[h] - ds_[h]) * scale).astype(BF16))
                for h in range(NH):
                    accs[h] = accs[h] + _dot(ds_l[h], kcs[h // 2])
                for pr in range(PP):
                    ha, hb = 2 * pr, 2 * pr + 1
                    dkc_s[pl.ds(k0, TQ), pr * CW:(pr + 1) * CW] += _dot_tn(ds_l[ha], qs[ha]) + _dot_tn(ds_l[hb], qs[hb])
                    dv_s[pr * LANES:(pr + 1) * LANES, pl.ds(k0, TQ)] += _dot_tn(dos[ha], p_l[ha]) + _dot_tn(dos[hb], p_l[hb])
                return tuple(accs)

            zc = jnp.zeros((TQ, CW), F32)

            def k_block(j, c):
                return tile(pl.multiple_of(j * TQ, TQ), c, None)

            c = lax.fori_loop(0, i, k_block, (zc,) * NH)
            c = tile(q0, c, vis)
            for pr in range(PP):
                ps = slice(pr * LANES, (pr + 1) * LANES)
                dq = _pick(mm[0], c[2 * pr]) + _pick(mm[1], c[2 * pr + 1])
                dqn_ref[pl.ds(q0, TQ), ps] = dq[:, :LANES].astype(BF16)
                dqr_ref[pl.ds(q0, TQ), ps] = dq[:, LANES:]
            return carry

        lax.fori_loop(0, nq, q_block, 0)
        dkr = dkc_s[:, LANES:CW]
        for pr in range(PP):
            dkn_ref[:, pr * LANES:(pr + 1) * LANES] = dkc_s[:, pr * CW:pr * CW + LANES].astype(BF16)
            if pr > 0:
                dkr = dkr + dkc_s[:, pr * CW + LANES:(pr + 1) * CW]
        dv_ref[...] = dv_s[...].T.astype(BF16)
        g = pl.program_id(1)

        @pl.when(g == 0)
        def _():
            dkr_ref[...] = dkr

        @pl.when(g > 0)
        def _():
            dkr_ref[...] += dkr

    blk = lambda off: pl.BlockSpec((seq, W), lambda b, g: (b, off + g))
    out_blk = pl.BlockSpec((seq, W), lambda b, g: (b, g))
    one_blk = pl.BlockSpec((seq, LANES), lambda b, g: (b, 0))
    return pl.pallas_call(
        body, name=name, grid=(B, nstep),
        in_specs=[blk(0), blk(nstep), blk(0), blk(nstep), one_blk, out_blk, out_blk, out_blk],
        out_specs=[out_blk, out_blk, out_blk, out_blk, one_blk],
        out_shape=[jax.ShapeDtypeStruct((T, MLA_W), BF16), jax.ShapeDtypeStruct((T, MLA_W), F32),
                   jax.ShapeDtypeStruct((T, MLA_W), BF16), jax.ShapeDtypeStruct((T, MLA_W), BF16),
                   jax.ShapeDtypeStruct((T, LANES), F32)],
        scratch_shapes=[pltpu.VMEM((seq, NH * CW), BF16), pltpu.VMEM((seq, PP * CW), BF16), pltpu.VMEM((seq, NH * LANES), BF16),
                        pltpu.VMEM((seq, PP * CW), F32), pltpu.VMEM((W, seq), F32)],
        compiler_params=_cparams(("parallel", "arbitrary")),
    )(qm, qm, kvm, kvm, krt, o, lse, do)


def _rope_tables(pos_ref, invf_ref):
    ang = pos_ref[...].astype(F32) * invf_ref[...]
    first = (_lane_iota() % MLA_ROPE) < (MLA_ROPE // 2)
    return jnp.cos(ang), jnp.sin(ang), first


def _rope_apply(x, cos, sin, first):
    rot = jnp.where(first, -pltpu.roll(x, LANES - MLA_ROPE // 2, 1), pltpu.roll(x, MLA_ROPE // 2, 1))
    return x * cos + rot * sin


def _rope_apply_t(dy, cos, sin, first):
    dys = dy * sin
    rot_t = jnp.where(first, pltpu.roll(dys, LANES - MLA_ROPE // 2, 1), -pltpu.roll(dys, MLA_ROPE // 2, 1))
    return dy * cos + rot_t


def _proj_uq_rope(p, g, wuq, pos, invf, *, tm, name):
    T = p.shape[0]
    ntile = MLA_W // LANES

    def body(x_ref, kr_ref, g_ref, w_ref, pos_ref, invf_ref, cq_ref, qm_ref, krt_ref):
        cos, sin, first = _rope_tables(pos_ref, invf_ref)
        hb = _rms(x_ref[...], g_ref[...]).astype(BF16)
        cq_ref[...] = hb
        q = jnp.dot(hb, w_ref[...], preferred_element_type=F32)
        qm_ref[:, :MLA_W] = q[:, :MLA_W].astype(BF16)
        for t in range(ntile):
            sl = slice(MLA_W + t * LANES, MLA_W + (t + 1) * LANES)
            qm_ref[:, sl] = _rope_apply(q[:, sl], cos, sin, first).astype(BF16)
        krt_ref[...] = _rope_apply(kr_ref[...], cos, sin, first).astype(BF16)

    return pl.pallas_call(
        body, name=name, grid=(T // tm,),
        in_specs=[pl.BlockSpec((tm, Q_LORA), lambda i: (i, P_CQ // Q_LORA)), pl.BlockSpec((tm, LANES), lambda i: (i, P_KRT // LANES)),
                  pl.BlockSpec((1, Q_LORA), lambda i: (0, 0)), pl.BlockSpec((Q_LORA, 2 * MLA_W), lambda i: (0, 0)),
                  pl.BlockSpec((tm, 1), lambda i: (i, 0)), pl.BlockSpec((1, LANES), lambda i: (0, 0))],
        out_specs=[pl.BlockSpec((tm, Q_LORA), lambda i: (i, 0)), pl.BlockSpec((tm, 2 * MLA_W), lambda i: (i, 0)),
                   pl.BlockSpec((tm, LANES), lambda i: (i, 0))],
        out_shape=[jax.ShapeDtypeStruct((T, Q_LORA), BF16), jax.ShapeDtypeStruct((T, 2 * MLA_W), BF16),
                   jax.ShapeDtypeStruct((T, LANES), BF16)],
        compiler_params=_cparams(("parallel",)),
    )(p, p, g, wuq, pos, invf)


def _d_proj_uq_rope(dqn, dqr, dkr, wuq, cq, p, g, pos, invf, *, tm, name):
    T = dqn.shape[0]
    ntile = MLA_W // LANES

    def body(dqn_ref, dqr_ref, dkr_ref, w_ref, cq_ref, x_ref, g_ref, pos_ref, invf_ref,
             dw_ref, dx_ref, dg_ref, dkr_o_ref, dqm_s):
        cos, sin, first = _rope_tables(pos_ref, invf_ref)
        dqm_s[:, :MLA_W] = dqn_ref[...]
        for t in range(ntile):
            sl = slice(t * LANES, (t + 1) * LANES)
            dqm_s[:, MLA_W + t * LANES:MLA_W + (t + 1) * LANES] = _rope_apply_t(dqr_ref[:, sl], cos, sin, first).astype(BF16)
        dkr_o_ref[...] = _rope_apply_t(dkr_ref[...], cos, sin, first).astype(BF16)
        dqm = dqm_s[...]
        dy = lax.dot_general(dqm, w_ref[...], (((1,), (1,)), ((), ())), preferred_element_type=F32)
        dx, part = _rms_grad(dy, x_ref[...], g_ref[...])
        dx_ref[...] = dx.astype(BF16)
        _accumulate(dg_ref, part)
        _accumulate(dw_ref, _dot_tn(dqm, cq_ref[...]))

    half = pl.BlockSpec((tm, MLA_W), lambda i: (i, 0))
    tile = pl.BlockSpec((tm, LANES), lambda i: (i, 0))
    lat = pl.BlockSpec((tm, Q_LORA), lambda i: (i, 0))
    return pl.pallas_call(
        body, name=name, grid=(T // tm,),
        in_specs=[half, half, tile, pl.BlockSpec((Q_LORA, 2 * MLA_W), lambda i: (0, 0)), lat,
                  pl.BlockSpec((tm, Q_LORA), lambda i: (i, P_CQ // Q_LORA)), pl.BlockSpec((1, Q_LORA), lambda i: (0, 0)),
                  pl.BlockSpec((tm, 1), lambda i: (i, 0)), pl.BlockSpec((1, LANES), lambda i: (0, 0))],
        out_specs=[pl.BlockSpec((2 * MLA_W, Q_LORA), lambda i: (0, 0)), lat,
                   pl.BlockSpec((1, Q_LORA), lambda i: (0, 0)), tile],
        out_shape=[jax.ShapeDtypeStruct((2 * MLA_W, Q_LORA), F32), jax.ShapeDtypeStruct((T, Q_LORA), BF16),
                   jax.ShapeDtypeStruct((1, Q_LORA), F32), jax.ShapeDtypeStruct((T, LANES), BF16)],
        scratch_shapes=[pltpu.VMEM((tm, 2 * MLA_W), BF16)],
        compiler_params=_cparams(("arbitrary",)),
    )(dqn, dqr, dkr, wuq, cq, p, g, pos, invf)


def _d_proj_cat(pieces, b, x, g, *, tm, name, residual=None, col_block=0, out_dtype=F32, b_transposed=False,
                wgrad_act=None):
    M = pieces[0].shape[0]
    widths = [pc.shape[1] for pc in pieces]
    K = sum(widths)
    C = b.shape[1] if b_transposed else b.shape[0]
    n = len(pieces)
    fuse_w = wgrad_act is not None
    contract = (((1,), (0,)), ((), ())) if b_transposed else (((1,), (1,)), ((), ()))
    in_specs = [pl.BlockSpec((tm, w), lambda i: (i, 0)) for w in widths]
    in_specs += [pl.BlockSpec(b.shape, lambda i: (0, 0)), pl.BlockSpec((tm, C), lambda i: (i, col_block)),
                 pl.BlockSpec((1, C), lambda i: (0, 0))]
    args = list(pieces) + [b, x, g]
    if residual is not None:
        in_specs.append(pl.BlockSpec((tm, C), lambda i: (i, 0)))
        args.append(residual)
    if fuse_w:
        in_specs.append(pl.BlockSpec((tm, C), lambda i: (i, 0)))
        args.append(wgrad_act)

    def body(*refs):
        b_ref, x_ref, g_ref = refs[n:n + 3]
        first_ref, dx_ref, dg_ref = refs[-4:-1] if fuse_w else refs[-3:]
        cat_ref = refs[-1] if fuse_w else first_ref
        off = 0
        for r, w in zip(refs[:n], widths):
            cat_ref[:, off:off + w] = r[...]
            off += w
        cat = cat_ref[...]
        dy = lax.dot_general(cat, b_ref[...], contract, preferred_element_type=F32)
        dx, part = _rms_grad(dy, x_ref[...], g_ref[...])
        if residual is not None:
            dx = dx + refs[n + 3][...]
        dx_ref[...] = dx.astype(out_dtype)
        _accumulate(dg_ref, part)
        if fuse_w:
            act_ref = refs[n + 3 + (residual is not None)]
            _accumulate(first_ref, _dot_tn(act_ref[...], cat))

    first_spec = pl.BlockSpec((C, K), lambda i: (0, 0)) if fuse_w else pl.BlockSpec((tm, K), lambda i: (i, 0))
    first_shape = jax.ShapeDtypeStruct((C, K), F32) if fuse_w else jax.ShapeDtypeStruct((M, K), BF16)
    return pl.pallas_call(
        body, name=name, grid=(M // tm,), in_specs=in_specs,
        out_specs=[first_spec, pl.BlockSpec((tm, C), lambda i: (i, 0)), pl.BlockSpec((1, C), lambda i: (0, 0))],
        out_shape=[first_shape, jax.ShapeDtypeStruct((M, C), out_dtype), jax.ShapeDtypeStruct((1, C), F32)],
        scratch_shapes=[pltpu.VMEM((tm, K), BF16)] if fuse_w else [],
        compiler_params=_cparams(("arbitrary",)),
    )(*args)


def _heads_out(xa, xb, ga, gb, w, resid, *, tm, name):
    T, C = xa.shape
    N = w.shape[1]

    def body(xa_ref, xb_ref, ga_ref, gb_ref, w_ref, r_ref, oc_ref, o_ref):
        oc_ref[:, :C] = _rms(xa_ref[...], ga_ref[...]).astype(BF16)
        oc_ref[:, C:] = _rms(xb_ref[...], gb_ref[...]).astype(BF16)
        o_ref[...] = r_ref[...] + jnp.dot(oc_ref[...], w_ref[...], preferred_element_type=F32)

    row = pl.BlockSpec((tm, C), lambda i: (i, 0))
    gsp = pl.BlockSpec((1, C), lambda i: (0, 0))
    full = pl.BlockSpec((tm, N), lambda i: (i, 0))
    return pl.pallas_call(
        body, name=name, grid=(T // tm,),
        in_specs=[row, row, gsp, gsp, pl.BlockSpec((2 * C, N), lambda i: (0, 0)), full],
        out_specs=[pl.BlockSpec((tm, 2 * C), lambda i: (i, 0)), full],
        out_shape=[jax.ShapeDtypeStruct((T, 2 * C), BF16), jax.ShapeDtypeStruct((T, N), F32)],
        compiler_params=_cparams(("parallel",)),
    )(xa, xb, ga, gb, w, resid)


def _heads_out_bwd(dout, w, ocat, xa, xb, ga, gb, *, tm, name):
    T, C = xa.shape
    N = w.shape[1]
    n_steps = T // tm

    def body(d_ref, w_ref, oc_ref, xa_ref, xb_ref, ga_ref, gb_ref, dxa_ref, dxb_ref, dga_ref, dgb_ref, dw_ref, dw_s):
        db = d_ref[...].astype(BF16)
        dy = lax.dot_general(db, w_ref[...], (((1,), (1,)), ((), ())), preferred_element_type=F32)
        dxa, pa = _rms_grad(dy[:, :C], xa_ref[...], ga_ref[...])
        dxb, pb = _rms_grad(dy[:, C:], xb_ref[...], gb_ref[...])
        dxa_ref[...] = dxa
        dxb_ref[...] = dxb
        _accumulate(dga_ref, pa)
        _accumulate(dgb_ref, pb)
        _accumulate(dw_s, _dot_tn(oc_ref[...], db))

        @pl.when(pl.program_id(0) == n_steps - 1)
        def _():
            dw_ref[...] = dw_s[...].astype(BF16)

    row = pl.BlockSpec((tm, C), lambda i: (i, 0))
    gsp = pl.BlockSpec((1, C), lambda i: (0, 0))
    wsp = pl.BlockSpec((2 * C, N), lambda i: (0, 0))
    return pl.pallas_call(
        body, name=name, grid=(n_steps,),
        in_specs=[pl.BlockSpec((tm, N), lambda i: (i, 0)), wsp, pl.BlockSpec((tm, 2 * C), lambda i: (i, 0)), row, row, gsp, gsp],
        out_specs=[row, row, gsp, gsp, wsp],
        out_shape=[jax.ShapeDtypeStruct((T, C), F32), jax.ShapeDtypeStruct((T, C), F32),
                   jax.ShapeDtypeStruct((1, C), F32), jax.ShapeDtypeStruct((1, C), F32),
                   jax.ShapeDtypeStruct((2 * C, N), BF16)],
        scratch_shapes=[pltpu.VMEM((2 * C, N), F32)],
        compiler_params=_cparams(("arbitrary",)),
    )(dout, w, ocat, xa, xb, ga, gb)


CONV_ROWS = 256
HALO = 8


def _conv_taps(w_ref):
    return w_ref[0:1, :], w_ref[1:2, :], w_ref[2:3, :]


def _conv_rows(cur, prev, w, bias):
    ext = jnp.concatenate([prev, cur], axis=0)
    u1 = pltpu.roll(ext, 1, 0)[HALO:]
    u2 = pltpu.roll(ext, 2, 0)[HALO:]
    return w[2] * cur + w[1] * u1 + w[0] * u2 + bias, u1, u2


def _conv_fwd(u, w, bias, *, seq, name):
    T = u.shape[0]
    B = T // seq
    W2 = 2 * FF_BLK

    def body(u_ref, w_ref, b_ref, a_ref):
        wv = _conv_taps(w_ref)
        bv = b_ref[...]
        for c in range(seq // CONV_ROWS):
            r0 = c * CONV_ROWS
            cur = u_ref[r0:r0 + CONV_ROWS, :]
            prev = u_ref[r0 - HALO:r0, :] if c > 0 else jnp.zeros((HALO, W2), F32)
            y, _, _ = _conv_rows(cur, prev, wv, bv)
            gc = y[:, :FF_BLK]
            a_ref[r0:r0 + CONV_ROWS, :] = (gc * (1.0 / (1.0 + jnp.exp(-gc))) * y[:, FF_BLK:]).astype(BF16)

    return pl.pallas_call(
        body, name=name, grid=(B, N_FF_BLK),
        in_specs=[pl.BlockSpec((seq, W2), lambda b, j: (b, j)), pl.BlockSpec((3, W2), lambda b, j: (0, j)),
                  pl.BlockSpec((1, W2), lambda b, j: (0, j))],
        out_specs=pl.BlockSpec((seq, FF_BLK), lambda b, j: (b, j)),
        out_shape=jax.ShapeDtypeStruct((T, D_FF), BF16),
        compiler_params=_cparams(("parallel", "parallel")),
    )(u, w, bias)


def _conv_bwd(u, dx2, wdn, w, bias, *, seq, name):
    T = u.shape[0]
    B = T // seq
    D = dx2.shape[1]
    W2 = 2 * FF_BLK
    nchunk = seq // CONV_ROWS

    def body(u_ref, dx_ref, wd_ref, w_ref, b_ref, du_ref, dw_ref, db_ref, dwd_ref, duc_s, dwd_s):
        wv = _conv_taps(w_ref)
        bv = b_ref[...]
        wd = wd_ref[...]
        zrow = jnp.zeros((1, W2), F32)
        dw0, dw1, dw2, dbs = zrow, zrow, zrow, zrow
        dwd = jnp.zeros((FF_BLK, D), F32)
        for c in range(nchunk):
            r0 = c * CONV_ROWS
            cur = u_ref[r0:r0 + CONV_ROWS, :]
            prev = u_ref[r0 - HALO:r0, :] if c > 0 else jnp.zeros((HALO, W2), F32)
            y, u1, u2 = _conv_rows(cur, prev, wv, bv)
            gc = y[:, :FF_BLK]
            vc = y[:, FF_BLK:]
            sg = 1.0 / (1.0 + jnp.exp(-gc))
            dxc = dx_ref[r0:r0 + CONV_ROWS, :].astype(BF16)
            dav = _dot_nt(dxc, wd)
            silu = gc * sg
            dwd = dwd + _dot_tn((silu * vc).astype(BF16), dxc)
            duc = jnp.concatenate([dav * vc * (sg * (1.0 + gc * (1.0 - sg))), dav * silu], axis=1)
            duc_s[r0:r0 + CONV_ROWS, :] = duc
            dw0 = dw0 + jnp.sum(duc * u2, axis=0, keepdims=True)
            dw1 = dw1 + jnp.sum(duc * u1, axis=0, keepdims=True)
            dw2 = dw2 + jnp.sum(duc * cur, axis=0, keepdims=True)
            dbs = dbs + jnp.sum(duc, axis=0, keepdims=True)
        duc_s[seq:seq + HALO, :] = jnp.zeros((HALO, W2), F32)
        n_ext = CONV_ROWS + HALO
        for c in range(nchunk):
            r0 = c * CONV_ROWS
            ext = duc_s[r0:r0 + n_ext, :]
            s1 = pltpu.roll(ext, n_ext - 1, 0)[:CONV_ROWS]
            s2 = pltpu.roll(ext, n_ext - 2, 0)[:CONV_ROWS]
            du_ref[r0:r0 + CONV_ROWS, :] = (wv[2] * ext[:CONV_ROWS] + wv[1] * s1 + wv[0] * s2).astype(BF16)

        b = pl.program_id(1)

        @pl.when(b == 0)
        def _():
            dw_ref[0:1, :] = dw0
            dw_ref[1:2, :] = dw1
            dw_ref[2:3, :] = dw2
            db_ref[...] = dbs
            dwd_s[...] = dwd

        @pl.when(b > 0)
        def _():
            dw_ref[0:1, :] += dw0
            dw_ref[1:2, :] += dw1
            dw_ref[2:3, :] += dw2
            db_ref[...] += dbs
            dwd_s[...] += dwd

        @pl.when(b == B - 1)
        def _():
            dwd_ref[...] = dwd_s[...].astype(BF16)

    return pl.pallas_call(
        body, name=name, grid=(N_FF_BLK, B),
        in_specs=[pl.BlockSpec((seq, W2), lambda j, b: (b, j)), pl.BlockSpec((seq, D), lambda j, b: (b, 0)),
                  pl.BlockSpec((FF_BLK, D), lambda j, b: (j, 0)),
                  pl.BlockSpec((3, W2), lambda j, b: (0, j)), pl.BlockSpec((1, W2), lambda j, b: (0, j))],
        out_specs=[pl.BlockSpec((seq, W2), lambda j, b: (b, j)), pl.BlockSpec((3, W2), lambda j, b: (0, j)),
                   pl.BlockSpec((1, W2), lambda j, b: (0, j)), pl.BlockSpec((FF_BLK, D), lambda j, b: (j, 0))],
        out_shape=[jax.ShapeDtypeStruct((T, 2 * D_FF), BF16), jax.ShapeDtypeStruct((3, 2 * D_FF), F32),
                   jax.ShapeDtypeStruct((1, 2 * D_FF), F32), jax.ShapeDtypeStruct((D_FF, D), BF16)],
        scratch_shapes=[pltpu.VMEM((seq + HALO, W2), F32), pltpu.VMEM((FF_BLK, D), F32)],
        compiler_params=_cparams(("parallel", "arbitrary")),
    )(u, dx2, wdn, w, bias)


def _place():
    return lax.axis_index("x"), lax.axis_index("y"), lax.axis_index("c")


def _other_chips(x, y):
    return [(1 - x, y), (x, 1 - y), (1 - x, 1 - y)]


def _all_gather(vs, *, name):
    n = len(vs)

    def body(*refs):
        v_refs, out_refs = refs[:n], refs[n:2 * n]
        send_sems, recv_sems, local_sems = refs[2 * n:]
        x, y, c = _place()
        me, sibling = (x, y, c), (x, y, 1 - c)
        chips = _other_chips(x, y)

        def slab(a, px, py, pc):
            return out_refs[a].at[4 * px + 2 * py + pc]

        def copy(a, k, block, to, src=None):
            return pltpu.make_async_remote_copy(
                src_ref=slab(a, *block) if src is None else src, dst_ref=slab(a, *block),
                send_sem=send_sems.at[7 * a + k], recv_sem=recv_sems.at[7 * a + k], device_id=to, device_id_type=MESH)

        mine = [pltpu.make_async_copy(v_refs[a], slab(a, *me), local_sems.at[a]) for a in range(n)]
        for cp in mine:
            cp.start()
        first = []
        for a in range(n):
            first.append(copy(a, 0, me, sibling, src=v_refs[a]))
            first += [copy(a, 1 + j, me, (*chip, c), src=v_refs[a]) for j, chip in enumerate(chips)]
        for cp in first:
            cp.start()
        passed = []
        for j, chip in enumerate(chips):
            for a in range(n):
                copy(a, 1 + j, (*chip, c), me).wait_recv()
                cp = copy(a, 4 + j, (*chip, c), sibling)
                cp.start()
                passed.append(cp)
        for a in range(n):
            copy(a, 0, sibling, me).wait_recv()
            for j, chip in enumerate(chips):
                copy(a, 4 + j, (*chip, 1 - c), me).wait_recv()
        for cp in first + passed:
            cp.wait_send()
        for cp in mine:
            cp.wait()

    return pl.pallas_call(
        body, name=name, in_specs=[ANY] * n, out_specs=[ANY] * n,
        out_shape=[jax.ShapeDtypeStruct((N_DEV,) + v.shape, v.dtype) for v in vs],
        scratch_shapes=[pltpu.SemaphoreType.DMA((7 * n,)), pltpu.SemaphoreType.DMA((7 * n,)), pltpu.SemaphoreType.DMA((n,))],
    )(*vs)


def _all_gather_async(vs, *, name, collective_id):
    n = len(vs)
    v_refs = [jax.new_ref(v, memory_space=pltpu.MemorySpace.HBM) for v in vs]
    out_refs = [jax.empty_ref(jax.ShapeDtypeStruct((N_DEV,) + v.shape, v.dtype), memory_space=pltpu.MemorySpace.HBM)
                for v in vs]

    @pl.kernel(mesh=plsc.ScalarSubcoreMesh(axis_name="seq", num_cores=1), name=name,
               scratch_types=(pltpu.SemaphoreType.DMA((7 * n,)), pltpu.SemaphoreType.DMA((7 * n,)),
                              pltpu.SemaphoreType.DMA((n,))),
               compiler_params=pltpu.CompilerParams(collective_id=collective_id))
    def launch(send_sems, recv_sems, local_sems):
        x, y, c = _place()
        me, sibling = (x, y, c), (x, y, 1 - c)
        chips = _other_chips(x, y)
        peers = [sibling] + [(*chip, c) for chip in chips]
        barrier = pltpu.get_barrier_semaphore()
        for peer in peers:
            pl.semaphore_signal(barrier, inc=1, device_id=peer, device_id_type=MESH)
        pl.semaphore_wait(barrier, len(peers))

        def slab(a, px, py, pc):
            return out_refs[a].at[4 * px + 2 * py + pc]

        def copy(a, k, block, to, src=None):
            return pltpu.make_async_remote_copy(
                src_ref=slab(a, *block) if src is None else src, dst_ref=slab(a, *block),
                send_sem=send_sems.at[7 * a + k], recv_sem=recv_sems.at[7 * a + k], device_id=to, device_id_type=MESH)

        mine = [pltpu.make_async_copy(v_refs[a], slab(a, *me), local_sems.at[a]) for a in range(n)]
        for cp in mine:
            cp.start()
        first = []
        for a in range(n):
            first.append(copy(a, 0, me, sibling, src=v_refs[a]))
            first += [copy(a, 1 + j, me, (*chip, c), src=v_refs[a]) for j, chip in enumerate(chips)]
        for cp in first:
            cp.start()
        passed = []
        for j, chip in enumerate(chips):
            for a in range(n):
                copy(a, 1 + j, (*chip, c), me).wait_recv()
                cp = copy(a, 4 + j, (*chip, c), sibling)
                cp.start()
                passed.append(cp)
        for a in range(n):
            copy(a, 0, sibling, me).wait_recv()
            for j, chip in enumerate(chips):
                copy(a, 4 + j, (*chip, 1 - c), me).wait_recv()
        for cp in first + passed:
            cp.wait_send()
        for cp in mine:
            cp.wait()

    launch()
    return [r[...] for r in out_refs]


def _handshake(peers):
    barrier = pltpu.get_barrier_semaphore()
    for peer in peers:
        pl.semaphore_signal(barrier, inc=1, device_id=peer, device_id_type=MESH)
    pl.semaphore_wait(barrier, len(peers))


def _hbm_refs(arrays, lead):
    src = [jax.new_ref(a, memory_space=pltpu.MemorySpace.HBM) for a in arrays]
    dst = [jax.empty_ref(jax.ShapeDtypeStruct((lead,) + a.shape[1:], a.dtype), memory_space=pltpu.MemorySpace.HBM)
           for a in arrays]
    return src, dst


def _peer(x, y, c, k):
    return ((1 - x) if k & 4 else x, (1 - y) if k & 2 else y, (1 - c) if k & 1 else c)


def _rs_direct_async(g8s, *, name, collective_id):
    n = len(g8s)
    g_refs, out_refs = _hbm_refs(g8s, N_DEV - 1)

    @pl.kernel(mesh=plsc.ScalarSubcoreMesh(axis_name="seq", num_cores=1), name=name,
               scratch_types=(pltpu.SemaphoreType.DMA((7 * n,)), pltpu.SemaphoreType.DMA((7 * n,))),
               compiler_params=pltpu.CompilerParams(collective_id=collective_id))
    def launch(send_sems, recv_sems):
        x, y, c = _place()
        peers = [_peer(x, y, c, k) for k in range(1, N_DEV)]
        _handshake(peers)
        copies = [
            pltpu.make_async_remote_copy(
                src_ref=g_refs[a].at[4 * px + 2 * py + pc], dst_ref=out_refs[a].at[k],
                send_sem=send_sems.at[7 * a + k], recv_sem=recv_sems.at[7 * a + k],
                device_id=(px, py, pc), device_id_type=MESH)
            for a in range(n) for k, (px, py, pc) in enumerate(peers)]
        for cp in copies:
            cp.start()
        for cp in copies:
            cp.wait()

    launch()
    return [r[...] for r in out_refs]


def _row_tile(rows):
    if rows <= 512:
        return rows
    return next(t for t in (512, 384, 352, 256, 128) if rows % t == 0)


def _split_moves(segments, chunk):
    moves = []
    for dst, src, length in segments:
        while length > 0:
            dev, off = divmod(src, chunk)
            take = min(length, chunk - off)
            moves.append((dst, dev, off, take))
            dst, src, length = dst + take, src + take, length - take
    return moves


def _assemble(stacked, segments, zero_spans, out_cols, *, name):
    _, R, c = stacked.shape
    tr = _row_tile(R)
    moves = _split_moves(segments, c)

    def body(x_ref, o_ref):
        for dst, dev, off, take in moves:
            o_ref[:, dst:dst + take] = x_ref[dev, :, off:off + take]
        for a, b in zero_spans:
            o_ref[:, a:b] = jnp.zeros((tr, b - a), o_ref.dtype)

    return pl.pallas_call(
        body, name=name, grid=(R // tr,),
        in_specs=[pl.BlockSpec((N_DEV, tr, c), lambda i: (0, i, 0))],
        out_specs=pl.BlockSpec((tr, out_cols), lambda i: (i, 0)),
        out_shape=jax.ShapeDtypeStruct((R, out_cols), stacked.dtype),
        compiler_params=_cparams(("parallel",)),
    )(stacked)


def _disassemble(full, segments, chunk, *, name, out_dtype=F32):
    R = full.shape[0]
    tr = _row_tile(R)
    moves = _split_moves(segments, chunk)

    def body(x_ref, o_ref):
        seen = set()
        for dst, dev, off, take in moves:
            piece = x_ref[:, dst:dst + take]
            if (dev, off) in seen:
                piece = piece + o_ref[dev, :, off:off + take]
            seen.add((dev, off))
            o_ref[dev, :, off:off + take] = piece.astype(out_dtype)

    return pl.pallas_call(
        body, name=name, grid=(R // tr,),
        in_specs=[pl.BlockSpec((tr, full.shape[1]), lambda i: (i, 0))],
        out_specs=pl.BlockSpec((N_DEV, tr, chunk), lambda i: (0, i, 0)),
        out_shape=jax.ShapeDtypeStruct((N_DEV, R, chunk), out_dtype),
        compiler_params=_cparams(("parallel",)),
    )(full)


def _assemble_rows(stacked, segments, zero_spans, out_rows, *, name):
    _, c, R = stacked.shape
    tc = next(t for t in (2 * LANES, LANES) if R % t == 0)
    moves = _split_moves(segments, c)

    def body(x_ref, o_ref):
        for dst, dev, off, take in moves:
            o_ref[dst:dst + take, :] = x_ref[dev, off:off + take, :]
        for a, b in zero_spans:
            o_ref[a:b, :] = jnp.zeros((b - a, tc), o_ref.dtype)

    return pl.pallas_call(
        body, name=name, grid=(R // tc,),
        in_specs=[pl.BlockSpec((N_DEV, c, tc), lambda i: (0, 0, i))],
        out_specs=pl.BlockSpec((out_rows, tc), lambda i: (0, i)),
        out_shape=jax.ShapeDtypeStruct((out_rows, R), stacked.dtype),
        compiler_params=_cparams(("parallel",)),
    )(stacked)


def _disassemble_rows(full_t, segments, chunk, *, name, out_dtype=F32):
    R = full_t.shape[1]
    tc = next(t for t in (2 * LANES, LANES) if R % t == 0)
    moves = _split_moves(segments, chunk)

    def body(x_ref, o_ref):
        seen = set()
        for dst, dev, off, take in moves:
            piece = x_ref[dst:dst + take, :]
            if (dev, off) in seen:
                piece = piece + o_ref[dev, off:off + take, :]
            seen.add((dev, off))
            o_ref[dev, off:off + take, :] = piece.astype(out_dtype)

    return pl.pallas_call(
        body, name=name, grid=(R // tc,),
        in_specs=[pl.BlockSpec((full_t.shape[0], tc), lambda i: (0, i))],
        out_specs=pl.BlockSpec((N_DEV, chunk, tc), lambda i: (0, 0, i)),
        out_shape=jax.ShapeDtypeStruct((N_DEV, chunk, R), out_dtype),
        compiler_params=_cparams(("parallel",)),
    )(full_t)


_O_CQ = 3 * SB_W
_O_CKV = _O_CQ + Q_LORA
_O_KR = _O_CKV + KV_LORA
SEG_W_IN = ((0, 0, 3 * SB_W), (P_CKV, _O_CKV, KV_LORA), (P_KRT, _O_KR, MLA_ROPE), (P_KRT + MLA_ROPE, _O_KR, MLA_ROPE),
            (P_CQ, _O_CQ, Q_LORA))
ZERO_W_IN = ((P_KRT + 2 * MLA_ROPE, P_CQ),)
SEG_W_UQ = tuple((MLA_NOPE * h, MLA_QK * h, MLA_NOPE) for h in range(MLA_HEADS)) + tuple(
    (MLA_W + LANES * (h // 2) + MLA_ROPE * (h % 2), MLA_QK * h + MLA_NOPE, MLA_ROPE) for h in range(MLA_HEADS))
ZERO_W_UQ = tuple((MLA_W + LANES * g + 2 * MLA_ROPE, MLA_W + LANES * (g + 1)) for g in range(MLA_HEADS // 2))
SEG_W_UKV = tuple((MLA_NOPE * h, (MLA_NOPE + MLA_V) * h, MLA_NOPE) for h in range(MLA_HEADS)) + tuple(
    (MLA_W + MLA_V * h, (MLA_NOPE + MLA_V) * h + MLA_NOPE, MLA_V) for h in range(MLA_HEADS))
SEG_W_UP = tuple((2 * FF_BLK * blk + FF_BLK * half, D_FF * half + FF_BLK * blk, FF_BLK)
                 for half in range(2) for blk in range(N_FF_BLK))


def _sum8(g, *, name):
    _, R, C = g.shape

    def body(g_ref, o_ref):
        acc = g_ref[0]
        for k in range(1, N_DEV):
            acc = acc + g_ref[k]
        o_ref[...] = acc

    return pl.pallas_call(
        body, name=name, out_shape=jax.ShapeDtypeStruct((R, C), F32),
    )(g)


def _adamw_math(w, gf, m, v):
    c1 = 1.0 / (1.0 - ADAM_B1 ** ADAM_STEP)
    c2 = 1.0 / (1.0 - ADAM_B2 ** ADAM_STEP)
    mn = ADAM_B1 * m + (1.0 - ADAM_B1) * gf
    vn = ADAM_B2 * v + (1.0 - ADAM_B2) * (gf * gf)
    return -ADAM_LR * ((mn * c1) / (jnp.sqrt(vn * c2) + ADAM_EPS) + ADAM_WD * w), mn, vn


def _adamw(w, g, m, v, *, name):
    R, C = w.shape
    tr = _row_tile(R)

    def body(w_ref, g_ref, m_ref, v_ref, d_ref, mo_ref, vo_ref):
        d_ref[...], mo_ref[...], vo_ref[...] = _adamw_math(w_ref[...], g_ref[...], m_ref[...], v_ref[...])

    blk = pl.BlockSpec((tr, C), lambda i: (i, 0))
    shp = jax.ShapeDtypeStruct((R, C), F32)
    return pl.pallas_call(
        body, name=name, grid=(R // tr,), in_specs=[blk] * 4, out_specs=[blk] * 3,
        out_shape=[shp, shp, shp], compiler_params=_cparams(("parallel",)),
    )(w, g, m, v)


def _adamw_rs8(g8, r7, me_idx, w, m, v, *, name):
    R, C = w.shape
    tr = _row_tile(R)

    def body(i_ref, f_ref, r_ref, w_ref, m_ref, v_ref, g_ref, d_ref, mo_ref, vo_ref):
        gf = f_ref[...].astype(F32)
        for k in range(N_DEV - 1):
            gf = gf + r_ref[k].astype(F32)
        g_ref[...] = gf
        d_ref[...], mo_ref[...], vo_ref[...] = _adamw_math(w_ref[...], gf, m_ref[...], v_ref[...])

    blk = pl.BlockSpec((tr, C), lambda i, i_ref: (i, 0))
    shp = jax.ShapeDtypeStruct((R, C), F32)
    return pl.pallas_call(
        body, name=name,
        grid_spec=pltpu.PrefetchScalarGridSpec(
            num_scalar_prefetch=1, grid=(R // tr,),
            in_specs=[pl.BlockSpec((None, tr, C), lambda i, i_ref: (i_ref[0], i, 0)),
                      pl.BlockSpec((N_DEV - 1, tr, C), lambda i, i_ref: (0, i, 0)), blk, blk, blk],
            out_specs=[blk] * 4),
        out_shape=[shp] * 4, compiler_params=_cparams(("parallel",)),
    )(me_idx, g8, r7, w, m, v)


def _ff_interleave(a):
    lead = a.shape[:-1]
    return a.reshape(*lead, 2, N_FF_BLK, FF_BLK).swapaxes(-3, -2).reshape(*lead, 2 * D_FF)


def _ff_deinterleave(a):
    lead = a.shape[:-1]
    return a.reshape(*lead, N_FF_BLK, 2, FF_BLK).swapaxes(-3, -2).reshape(*lead, 2 * D_FF)


SMALL =(("g_mix", D_MODEL), ("g_cq", Q_LORA), ("g_ckv", KV_LORA), ("g_sb_out", SB_W), ("g_mla_out", MLA_W),
         ("g_ffn", D_MODEL), ("conv_b", 2 * D_FF), ("g_final", D_MODEL))
SMALL_ROWS = 88


SMALL_USED = sum(size for _, size in SMALL)


def _pack_small(d, tail=None):
    parts = [d[n].reshape(-1) for n, _ in SMALL] + ([] if tail is None else [tail])
    flat = jnp.concatenate(parts)
    flat = jnp.pad(flat, (0, SMALL_ROWS * LANES - flat.shape[0]))
    return flat.reshape(SMALL_ROWS, LANES)


def _unpack_small(a):
    flat = a.reshape(-1)
    out, off = {}, 0
    for n, size in SMALL:
        out[n] = flat[off:off + size]
        off += size
    return out


def kernel(x, positions, g_mix, w_in, g_cq, w_uq, g_ckv, w_ukv, g_sb_out, g_mla_out, w_out, g_ffn, w_up, conv_w, conv_b, w_down, g_final, loss_target, m_g_mix, m_w_in, m_g_cq, m_w_uq, m_g_ckv, m_w_ukv, m_g_sb_out, m_g_mla_out, m_w_out, m_g_ffn, m_w_up, m_conv_w, m_conv_b, m_w_down, m_g_final, v_g_mix, v_w_in, v_g_cq, v_w_uq, v_g_ckv, v_w_ukv, v_g_sb_out, v_g_mla_out, v_w_out, v_g_ffn, v_w_up, v_conv_w, v_conv_b, v_w_down, v_g_final):
    B, S, D = x.shape
    T = B * S
    xf = x.reshape(T, D)
    tgt = loss_target.reshape(T, D)
    pos = positions.reshape(T, 1)
    half = MLA_ROPE // 2
    inv_freq = 1.0 / (ROPE_BASE ** (jnp.arange(half, dtype=F32) * (2.0 / MLA_ROPE)))
    invf = jnp.tile(inv_freq, LANES // half).reshape(1, LANES)
    me_idx = (4 * lax.axis_index("x") + 2 * lax.axis_index("y") + lax.axis_index("c")).astype(jnp.int32).reshape(1)

    names = ("w_in", "w_uq", "w_ukv", "w_out", "w_up", "w_down", "conv_w")
    shard = {"w_in": w_in[0], "w_uq": w_uq[0], "w_ukv": w_ukv[0], "w_out": w_out[0], "w_up": w_up[0],
             "w_down": w_down[0], "conv_w": conv_w[0]}
    sent = {n: shard[n] if n == "conv_w" else shard[n].astype(BF16) for n in names}
    later = names[1:]
    w_in_all = _all_gather([jnp.transpose(shard["w_in"]).astype(BF16)], name="ag_w_in")[0]
    w_in_all, rest = lax.optimization_barrier((w_in_all, [sent[n] for n in later]))
    got = {"w_in": w_in_all}
    got.update(zip(later, _all_gather_async(rest, name="ag_weights_async", collective_id=0)))
    wi_t = _assemble_rows(got["w_in"], SEG_W_IN, ZERO_W_IN, P_COLS, name="asm_w_in")
    wuq = _assemble(got["w_uq"], SEG_W_UQ, ZERO_W_UQ, 2 * MLA_W, name="asm_w_uq")
    wukv = _assemble(got["w_ukv"], SEG_W_UKV, (), 2 * MLA_W, name="asm_w_ukv")
    wup = _assemble(got["w_up"], SEG_W_UP, (), 2 * D_FF, name="asm_w_up")
    cwi = _assemble(got["conv_w"], SEG_W_UP, (), 2 * D_FF, name="asm_conv_w")
    wo = got["w_out"].reshape(D, D)
    wdn = got["w_down"].reshape(D_FF, D)
    cbi = _ff_interleave(conv_b)

    h, p = _rms_matmul_nn(xf, g_mix, wi_t, tm=512, name="proj_in", w_transposed=True)
    o_sb, ltot = _sb_fwd(p, seq=S, name="sb_fwd")
    cq, qm, krt = _proj_uq_rope(p, g_cq, wuq, pos, invf, tm=512, name="proj_uq")
    ckv, kvm = _rms_matmul_nn(p, g_ckv, wukv, tm=512, name="proj_ukv", col_block=P_CKV // KV_LORA, out_dtype=BF16)
    o_mla, lse = _mla_fwd(qm, kvm, krt, seq=S, name="mla_fwd")
    ocat, x1 = _heads_out(o_sb, o_mla, g_sb_out, g_mla_out, wo, xf, tm=512, name="proj_out")
    hf, u = _rms_matmul_nn(x1, g_ffn, wup, tm=256, name="ffn_up")
    a = _conv_fwd(u, cwi, cbi, seq=S, name="conv_fwd")
    dx2, dg_final, loss_row = _matmul_nn_loss(a, wdn, x1, g_final.reshape(1, D), tgt, tm=512, name="ffn_down_loss")

    du, dcw, dcb, dw_down = _conv_bwd(u, dx2, wdn, cwi, cbi, seq=S, name="conv_bwd")
    dw_up_t = _matmul_tn(du, hf, tm=D_FF, tn=1024, tk=1024, name="dw_up")
    dx1, dg_ffn = _matmul_nt_rms_bwd(du, wup, x1, g_ffn, tm=512, name="d_ffn_up", residual=dx2)
    do_sb, do_mla, dg_sb, dg_mla, dw_out = _heads_out_bwd(dx1, wo, ocat, o_sb, o_mla, g_sb_out, g_mla_out, tm=512,
                                                          name="d_proj_out")

    early = ("w_down", "w_up", "conv_w", "w_out")
    g8 = {"w_up": _disassemble_rows(dw_up_t, SEG_W_UP, shard["w_up"].shape[1], name="split_dw_up", out_dtype=BF16),
          "conv_w": _disassemble(dcw, SEG_W_UP, shard["conv_w"].shape[1], name="split_dconv_w", out_dtype=BF16),
          "w_out": dw_out.reshape((N_DEV,) + shard["w_out"].shape),
          "w_down": dw_down.reshape((N_DEV,) + shard["w_down"].shape)}
    r7 = dict(zip(early, _rs_direct_async([g8[n] for n in early], name="rs_direct_async", collective_id=1)))

    dq_sb, dk_sb, dv_sb = _sb_bwd(p, ltot, do_sb, seq=S, name="sb_bwd")
    dqn, dqr, dkn, dvm, dkr = _mla_bwd(qm, kvm, krt, o_mla, lse, do_mla, seq=S, name="mla_bwd")
    dw_uq_t, dcq, dg_cq, dkr_u = _d_proj_uq_rope(dqn, dqr, dkr, wuq, cq, p, g_cq, pos, invf, tm=512, name="d_proj_uq")
    dw_ukv, dckv, dg_ckv = _d_proj_cat([dkn, dvm], wukv, p, g_ckv, tm=512, name="d_proj_ukv",
                                       col_block=P_CKV // KV_LORA, out_dtype=BF16, wgrad_act=ckv)
    dp, dx, dg_mix = _d_proj_cat([dq_sb, dk_sb, dv_sb, dckv, dkr_u, dcq], wi_t, xf, g_mix, tm=512, name="d_proj_in",
                                 residual=dx1, b_transposed=True)
    dw_in_t = _matmul_tn(dp, h, tm=P_COLS, tn=1024, tk=1024, name="dw_in")

    late = ("w_in", "w_uq", "w_ukv")
    g8.update({"w_in": _disassemble_rows(dw_in_t, SEG_W_IN, shard["w_in"].shape[1], name="split_dw_in", out_dtype=BF16),
               "w_uq": _disassemble_rows(dw_uq_t, SEG_W_UQ, shard["w_uq"].shape[1], name="split_dw_uq", out_dtype=BF16),
               "w_ukv": _disassemble(dw_ukv, SEG_W_UKV, shard["w_ukv"].shape[1], name="split_dw_ukv", out_dtype=BF16)})
    r7.update(zip(late, _rs_direct_async([g8[n] for n in late], name="rs_direct_late", collective_id=3)))

    params = {"w_in": (w_in, m_w_in, v_w_in), "w_uq": (w_uq, m_w_uq, v_w_uq), "w_ukv": (w_ukv, m_w_ukv, v_w_ukv),
              "w_out": (w_out, m_w_out, v_w_out), "w_up": (w_up, m_w_up, v_w_up), "conv_w": (conv_w, m_conv_w, v_conv_w),
              "w_down": (w_down, m_w_down, v_w_down)}
    grad, delta, new_m, new_v = {}, {}, {}, {}

    transposed = ("w_in", "w_uq", "w_up")

    def adamw_group(group):
        for n in group:
            flip = jnp.transpose if n in transposed else (lambda t: t)
            w_, m_, v_ = [flip(t[0]) for t in params[n]]
            res = _adamw_rs8(g8[n], r7[n], me_idx, w_, m_, v_, name="adamw_" + n)
            grad[n], delta[n], new_m[n], new_v[n] = [flip(r)[None] for r in res]

    adamw_group(("w_down", "w_out", "conv_w"))
    small_part = {"g_mix": dg_mix, "g_cq": dg_cq, "g_ckv": dg_ckv, "g_sb_out": dg_sb, "g_mla_out": dg_mla,
                  "g_ffn": dg_ffn, "conv_b": _ff_deinterleave(dcb), "g_final": dg_final}
    small_all, = _all_gather_async([_pack_small(small_part, tail=loss_row[0, 0:1])], name="ag_small_async",
                                   collective_id=5)
    adamw_group(("w_up",))
    adamw_group(late)
    gsmall = _sum8(small_all, name="sum_small_grads")
    small_w = {"g_mix": g_mix, "g_cq": g_cq, "g_ckv": g_ckv, "g_sb_out": g_sb_out, "g_mla_out": g_mla_out,
               "g_ffn": g_ffn, "conv_b": conv_b, "g_final": g_final}
    small_m = {"g_mix": m_g_mix, "g_cq": m_g_cq, "g_ckv": m_g_ckv, "g_sb_out": m_g_sb_out, "g_mla_out": m_g_mla_out,
               "g_ffn": m_g_ffn, "conv_b": m_conv_b, "g_final": m_g_final}
    small_v = {"g_mix": v_g_mix, "g_cq": v_g_cq, "g_ckv": v_g_ckv, "g_sb_out": v_g_sb_out, "g_mla_out": v_g_mla_out,
               "g_ffn": v_g_ffn, "conv_b": v_conv_b, "g_final": v_g_final}
    ds_, ms_, vs_ = _adamw(_pack_small(small_w), gsmall, _pack_small(small_m), _pack_small(small_v), name="adamw_small")
    for src, dst in ((_unpack_small(gsmall), grad), (_unpack_small(ds_), delta), (_unpack_small(ms_), new_m), (_unpack_small(vs_), new_v)):
        for n, _ in SMALL:
            dst[n] = src[n].reshape(small_w[n].shape)

    loss = gsmall.reshape(-1)[SMALL_USED]
    order = ("g_mix", "w_in", "g_cq", "w_uq", "g_ckv", "w_ukv", "g_sb_out", "g_mla_out", "w_out", "g_ffn", "w_up",
             "conv_w", "conv_b", "w_down", "g_final")
    return (loss, dx.reshape(B, S, D), *[grad[n] for n in order], *[delta[n] for n in order],
            *[new_m[n] for n in order], *[new_v[n] for n in order])
```

```python
import jax
import jax.numpy as jnp
from jax import lax
from jax.experimental import pallas as pl
from jax.experimental.pallas import tpu as pltpu
from jax.experimental.pallas import tpu_sc as plsc

F32 = jnp.float32
BF16 = jnp.bfloat16

D_MODEL = 1024
SB_HEADS = 8
SB_HEAD_DIM = 64
MLA_HEADS = 8
MLA_NOPE = 64
MLA_ROPE = 32
MLA_V = 64
Q_LORA = 384
KV_LORA = 256
D_FF = 2816
ROPE_BASE = 10000.0
EPS = 1e-6
SB_W = SB_HEADS * SB_HEAD_DIM
MLA_W = MLA_HEADS * MLA_V
MLA_QK = MLA_NOPE + MLA_ROPE

ADAM_LR = 0.001
ADAM_B1 = 0.9
ADAM_B2 = 0.999
ADAM_EPS = 1e-08
ADAM_WD = 0.01
ADAM_STEP = 10

N_DEV = 8
LANES = 128
V7X_VMEM_LIMIT = 56 * 1024 * 1024
FF_BLK = 256
N_FF_BLK = D_FF // FF_BLK

P_Q, P_K, P_V = 0, SB_W, 2 * SB_W
P_CKV = 3 * SB_W
P_KRT = P_CKV + KV_LORA
P_CQ = P_KRT + LANES
P_COLS = P_CQ + Q_LORA

MESH = pl.DeviceIdType.MESH
ANY = pl.BlockSpec(memory_space=pl.ANY)


def _cparams(sem=None, vmem=V7X_VMEM_LIMIT):
    return pltpu.CompilerParams(dimension_semantics=sem, vmem_limit_bytes=vmem)


def _matmul_tn(a, b, *, tm, tn, tk, name, out_dtype=F32):
    K, M = a.shape
    N = b.shape[1]
    assert M % tm == 0 and N % tn == 0 and K % tk == 0, (name, a.shape, b.shape)
    n_k = K // tk
    narrow = out_dtype != F32

    def body(a_ref, b_ref, o_ref, *scratch):
        acc_ref = scratch[0] if narrow else o_ref
        k = pl.program_id(2)
        part = lax.dot_general(a_ref[...].astype(BF16), b_ref[...].astype(BF16), (((0,), (0,)), ((), ())),
                               preferred_element_type=F32)

        @pl.when(k == 0)
        def _():
            acc_ref[...] = part

        @pl.when(k > 0)
        def _():
            acc_ref[...] += part

        if narrow:
            @pl.when(k == n_k - 1)
            def _():
                o_ref[...] = acc_ref[...].astype(out_dtype)

    return pl.pallas_call(
        body, name=name, grid=(M // tm, N // tn, n_k),
        in_specs=[pl.BlockSpec((tk, tm), lambda i, j, k: (k, i)), pl.BlockSpec((tk, tn), lambda i, j, k: (k, j))],
        out_specs=pl.BlockSpec((tm, tn), lambda i, j, k: (i, j)),
        out_shape=jax.ShapeDtypeStruct((M, N), out_dtype),
        scratch_shapes=[pltpu.VMEM((tm, tn), F32)] if narrow else [],
        compiler_params=_cparams(("parallel", "parallel", "arbitrary")),
    )(a, b)


def _rms(xf, g):
    r = lax.rsqrt(jnp.mean(xf * xf, axis=1, keepdims=True) + EPS)
    return (xf * r) * g


def _rms_grad(dyf, xf, g):
    r = lax.rsqrt(jnp.mean(xf * xf, axis=1, keepdims=True) + EPS)
    xh = xf * r
    dyg = dyf * g
    dx = r * (dyg - xh * jnp.mean(dyg * xh, axis=1, keepdims=True))
    return dx, jnp.sum(dyf * xh, axis=0, keepdims=True)


def _accumulate(ref, part):
    @pl.when(pl.program_id(0) == 0)
    def _():
        ref[...] = part

    @pl.when(pl.program_id(0) > 0)
    def _():
        ref[...] += part


def _rms_matmul_nn(x, g, w, *, tm, name, col_block=0, out_dtype=F32, w_transposed=False):
    T = x.shape[0]
    C, N = w.shape[::-1] if w_transposed else w.shape
    assert T % tm == 0, (name, x.shape)
    contract = (((1,), (1,)), ((), ())) if w_transposed else (((1,), (0,)), ((), ()))

    def body(x_ref, g_ref, w_ref, h_ref, o_ref):
        hb = _rms(x_ref[...], g_ref[...]).astype(BF16)
        h_ref[...] = hb
        o_ref[...] = lax.dot_general(hb, w_ref[...], contract, preferred_element_type=F32).astype(out_dtype)

    return pl.pallas_call(
        body, name=name, grid=(T // tm,),
        in_specs=[pl.BlockSpec((tm, C), lambda i: (i, col_block)), pl.BlockSpec((1, C), lambda i: (0, 0)),
                  pl.BlockSpec(w.shape, lambda i: (0, 0))],
        out_specs=[pl.BlockSpec((tm, C), lambda i: (i, 0)), pl.BlockSpec((tm, N), lambda i: (i, 0))],
        out_shape=[jax.ShapeDtypeStruct((T, C), BF16), jax.ShapeDtypeStruct((T, N), out_dtype)],
        compiler_params=_cparams(("parallel",)),
    )(x, g, w)


def _matmul_nt_rms_bwd(a, b, x, g, *, tm, name, residual=None, col_block=0, out_dtype=F32):
    M, K = a.shape
    C = b.shape[0]
    assert M % tm == 0, (name, a.shape)
    in_specs = [pl.BlockSpec((tm, K), lambda i: (i, 0)), pl.BlockSpec((C, K), lambda i: (0, 0)),
                pl.BlockSpec((tm, C), lambda i: (i, col_block)), pl.BlockSpec((1, C), lambda i: (0, 0))]
    args = [a, b, x, g]
    if residual is not None:
        in_specs.append(pl.BlockSpec((tm, C), lambda i: (i, 0)))
        args.append(residual)

    def body(*refs):
        a_ref, b_ref, x_ref, g_ref = refs[:4]
        dx_ref, dg_ref = refs[-2:]
        dy = lax.dot_general(a_ref[...].astype(BF16), b_ref[...], (((1,), (1,)), ((), ())), preferred_element_type=F32)
        dx, part = _rms_grad(dy, x_ref[...], g_ref[...])
        if residual is not None:
            dx = dx + refs[4][...]
        dx_ref[...] = dx.astype(out_dtype)
        _accumulate(dg_ref, part)

    return pl.pallas_call(
        body, name=name, grid=(M // tm,), in_specs=in_specs,
        out_specs=[pl.BlockSpec((tm, C), lambda i: (i, 0)), pl.BlockSpec((1, C), lambda i: (0, 0))],
        out_shape=[jax.ShapeDtypeStruct((M, C), out_dtype), jax.ShapeDtypeStruct((1, C), F32)],
        compiler_params=_cparams(("arbitrary",)),
    )(*args)


def _matmul_nn_loss(a, w, x1, g, tgt, *, tm, name):
    M, K = a.shape
    C = w.shape[1]
    assert M % tm == 0, (name, a.shape)

    nsub = 4
    ts = tm // nsub

    def body(a_ref, w_ref, x_ref, g_ref, t_ref, dx_ref, dg_ref, loss_ref):
        gf = g_ref[...]
        wv = w_ref[...]
        rows = [slice(r * ts, (r + 1) * ts) for r in range(nsub)]
        xs = [x_ref[rw, :] + jnp.dot(a_ref[rw, :], wv, preferred_element_type=F32) for rw in rows]
        lpart, gpart = 0.0, 0.0
        for rw, xf in zip(rows, xs):
            err = _rms(xf, gf) - t_ref[rw, :]
            lpart = lpart + 0.5 * jnp.sum(jnp.mean(err * err, axis=1, keepdims=True), axis=0, keepdims=True)
            dx, gp = _rms_grad(err * (1.0 / C), xf, gf)
            dx_ref[rw, :] = dx
            gpart = gpart + gp
        _accumulate(dg_ref, gpart)
        _accumulate(loss_ref, jnp.broadcast_to(lpart, (1, LANES)))

    row = pl.BlockSpec((tm, C), lambda i: (i, 0))
    return pl.pallas_call(
        body, name=name, grid=(M // tm,),
        in_specs=[pl.BlockSpec((tm, K), lambda i: (i, 0)), pl.BlockSpec((K, C), lambda i: (0, 0)), row,
                  pl.BlockSpec((1, C), lambda i: (0, 0)), row],
        out_specs=[row, pl.BlockSpec((1, C), lambda i: (0, 0)), pl.BlockSpec((1, LANES), lambda i: (0, 0))],
        out_shape=[jax.ShapeDtypeStruct((M, C), F32), jax.ShapeDtypeStruct((1, C), F32),
                   jax.ShapeDtypeStruct((1, LANES), F32)],
        compiler_params=_cparams(("arbitrary",)),
    )(a, w, x1, g, tgt)


ATT_T = 256
ATT_PAIRS = 2
NEG_BIG = -1e30


def _lane_iota():
    return lax.broadcasted_iota(jnp.int32, (1, LANES), 1)


def _head_masks():
    first = _lane_iota() < SB_HEAD_DIM
    return first, jnp.logical_not(first)


def _pick(mask, x):
    return jnp.where(mask, x, jnp.zeros_like(x))


def _lane_value(t, lane):
    return jnp.sum(jnp.where(_lane_iota() == lane, t, 0.0), axis=1, keepdims=True)


def _split_hi_lo(x):
    hi = x.astype(BF16)
    lo = (x - hi.astype(F32)).astype(BF16)
    return jnp.concatenate([hi, lo], axis=1)


def _tri(n, kind):
    r = lax.broadcasted_iota(jnp.int32, (n, n), 0)
    c = lax.broadcasted_iota(jnp.int32, (n, n), 1)
    u = {"suffix_excl": r > c, "prefix_incl": r <= c, "prefix_excl": r < c}[kind].astype(BF16)
    return jnp.concatenate([u, u], axis=0)


def _dot_nt(a, b):
    return lax.dot_general(a, b, (((1,), (1,)), ((), ())), preferred_element_type=F32)


def _dot_tn(a, b):
    return lax.dot_general(a, b, (((0,), (0,)), ((), ())), preferred_element_type=F32)


def _dot(a, b):
    return jnp.dot(a, b, preferred_element_type=F32)


def _causal_mask(n, strict):
    r = lax.broadcasted_iota(jnp.int32, (n, n), 0)
    c = lax.broadcasted_iota(jnp.int32, (n, n), 1)
    return (c < r) if strict else (c <= r)


LOG2E = 1.4426950408889634


def _sb_logs(qh, kj, vis):
    z2 = _dot_nt(qh, kj) * LOG2E
    nk = jnp.maximum(z2, 0.0) + jnp.log2(1.0 + jnp.exp2(-jnp.abs(z2)))
    lb = z2 - nk
    if vis is not None:
        nk = jnp.where(vis, nk, 0.0)
    return lb, nk


def _sb_fwd(p, *, seq, name):
    T = p.shape[0]
    B = T // seq
    TQ = ATT_T
    nq = seq // TQ
    PP = ATT_PAIRS
    W = PP * LANES
    nstep = SB_W // W
    NH = 2 * PP

    def body(q_ref, k_ref, v_ref, o_ref, lt_ref, q_s, k_s, v_s):
        masks = _head_masks()
        q = q_ref[...] * (SB_HEAD_DIM ** -0.5)
        v = v_ref[...]
        k_s[...] = k_ref[...].astype(BF16)
        for h in range(NH):
            ps = slice((h // 2) * LANES, (h // 2 + 1) * LANES)
            hs = slice(h * LANES, (h + 1) * LANES)
            q_s[:, hs] = _pick(masks[h % 2], q[:, ps]).astype(BF16)
            v_s[:, hs] = _pick(masks[h % 2], v[:, ps]).astype(BF16)
        u_suf = _tri(TQ, "suffix_excl")
        vis = _causal_mask(TQ, True)

        def q_block(i, carry):
            q0 = pl.multiple_of(i * TQ, TQ)
            qs = [q_s[pl.ds(q0, TQ), h * LANES:(h + 1) * LANES] for h in range(NH)]

            def tile(k0, c, mask):
                rs, accs = list(c[:NH]), list(c[NH:])
                logs = [_sb_logs(qs[h], k_s[pl.ds(k0, TQ), (h // 2) * LANES:(h // 2 + 1) * LANES], mask) for h in range(NH)]
                sums = [_dot(_split_hi_lo(nk), u_suf) for _, nk in logs]
                for h in range(NH):
                    a = jnp.exp2(logs[h][0] - sums[h] - rs[h])
                    if mask is not None:
                        a = jnp.where(mask, a, 0.0)
                    accs[h // 2] = accs[h // 2] + _dot(a.astype(BF16), v_s[pl.ds(k0, TQ), h * LANES:(h + 1) * LANES])
                    rs[h] = rs[h] + jnp.sum(logs[h][1], axis=1, keepdims=True)
                return tuple(rs) + tuple(accs)

            zero = jnp.zeros((TQ, 1), F32)
            c = tile(q0, (zero,) * NH + (jnp.zeros((TQ, LANES), F32),) * PP, vis)

            def k_block(jj, c):
                return tile(pl.multiple_of((i - 1 - jj) * TQ, TQ), c, None)

            c = lax.fori_loop(0, i, k_block, c)
            for pr in range(PP):
                ps = slice(pr * LANES, (pr + 1) * LANES)
                o_ref[pl.ds(q0, TQ), ps] = c[NH + pr]
                lt_ref[pl.ds(q0, TQ), ps] = jnp.where(masks[0], c[2 * pr], c[2 * pr + 1])
            return carry

        lax.fori_loop(0, nq, q_block, 0)

    blk = lambda off: pl.BlockSpec((seq, W), lambda b, g: (b, off + g))
    out_blk = pl.BlockSpec((seq, W), lambda b, g: (b, g))
    return pl.pallas_call(
        body, name=name, grid=(B, nstep),
        in_specs=[blk(P_Q // W), blk(P_K // W), blk(P_V // W)],
        out_specs=[out_blk, out_blk],
        out_shape=[jax.ShapeDtypeStruct((T, SB_W), F32), jax.ShapeDtypeStruct((T, SB_W), F32)],
        scratch_shapes=[pltpu.VMEM((seq, NH * LANES), BF16), pltpu.VMEM((seq, W), BF16), pltpu.VMEM((seq, NH * LANES), BF16)],
        compiler_params=_cparams(("parallel", "parallel")),
    )(p, p, p)


def _sb_bwd(p, ltot, do, *, seq, name):
    T = p.shape[0]
    B = T // seq
    TQ = ATT_T
    nq = seq // TQ
    PP = ATT_PAIRS
    W = PP * LANES
    nstep = SB_W // W
    NH = 2 * PP
    scale = SB_HEAD_DIM ** -0.5

    def body(q_ref, k_ref, v_ref, lt_ref, do_ref, dq_ref, dk_ref, dv_ref, q_s, k_s, v_s, do_s, dk_s, dv_s):
        masks = _head_masks()
        q = q_ref[...] * scale
        dof = do_ref[...]
        k_s[...] = k_ref[...].astype(BF16)
        v_s[...] = v_ref[...].astype(BF16)
        for h in range(NH):
            ps = slice((h // 2) * LANES, (h // 2 + 1) * LANES)
            hs = slice(h * LANES, (h + 1) * LANES)
            q_s[:, hs] = _pick(masks[h % 2], q[:, ps]).astype(BF16)
            do_s[:, hs] = _pick(masks[h % 2], dof[:, ps]).astype(BF16)
        dk_s[...] = jnp.zeros_like(dk_s)
        dv_s[...] = jnp.zeros_like(dv_s)
        u_pin = _tri(TQ, "prefix_incl")
        u_pex = _tri(TQ, "prefix_excl")[:TQ]
        vis = _causal_mask(TQ, True)

        def q_block(i, carry):
            q0 = pl.multiple_of(i * TQ, TQ)
            qs = [q_s[pl.ds(q0, TQ), h * LANES:(h + 1) * LANES] for h in range(NH)]
            dos = [do_s[pl.ds(q0, TQ), h * LANES:(h + 1) * LANES] for h in range(NH)]
            lt = lt_ref[pl.ds(q0, TQ), :]
            lts = [_lane_value(lt[:, (h // 2) * LANES:(h // 2 + 1) * LANES], (h % 2) * SB_HEAD_DIM) for h in range(NH)]

            def tile(k0, c, mask):
                cs, gs, accs = list(c[:NH]), list(c[NH:2 * NH]), list(c[2 * NH:])
                kjs = [k_s[pl.ds(k0, TQ), pr * LANES:(pr + 1) * LANES] for pr in range(PP)]
                vjs = [v_s[pl.ds(k0, TQ), pr * LANES:(pr + 1) * LANES] for pr in range(PP)]
                logs = [_sb_logs(qs[h], kjs[h // 2], mask) for h in range(NH)]
                pins = [_dot(_split_hi_lo(nk), u_pin) for _, nk in logs]
                das = [_dot_nt(dos[h], vjs[h // 2]) for h in range(NH)]
                a_l, g_l = [], []
                for h in range(NH):
                    a = jnp.exp2(logs[h][0] - ((lts[h] - cs[h]) - pins[h]))
                    if mask is not None:
                        a = jnp.where(mask, a, 0.0)
                    a_l.append(a)
                    g_l.append(das[h] * a)
                pres = [_dot(g.astype(BF16), u_pex) for g in g_l]
                dz_l = []
                for h in range(NH):
                    dz = g_l[h] - jnp.exp2(logs[h][0]) * (g_l[h] + (pres[h] + gs[h]))
                    if mask is not None:
                        dz = jnp.where(mask, dz, 0.0)
                    dz_l.append(dz.astype(BF16))
                for h in range(NH):
                    accs[h] = accs[h] + _dot(dz_l[h], kjs[h // 2])
                for pr in range(PP):
                    ps = slice(pr * LANES, (pr + 1) * LANES)
                    ha, hb = 2 * pr, 2 * pr + 1
                    dk_s[ps, pl.ds(k0, TQ)] += _dot_tn(qs[ha], dz_l[ha]) + _dot_tn(qs[hb], dz_l[hb])
                    dv_s[ps, pl.ds(k0, TQ)] += _dot_tn(dos[ha], a_l[ha].astype(BF16)) + _dot_tn(dos[hb], a_l[hb].astype(BF16))
                for h in range(NH):
                    cs[h] = cs[h] + jnp.sum(logs[h][1], axis=1, keepdims=True)
                    gs[h] = gs[h] + jnp.sum(g_l[h], axis=1, keepdims=True)
                return tuple(cs) + tuple(gs) + tuple(accs)

            z1 = jnp.zeros((TQ, 1), F32)
            zl = jnp.zeros((TQ, LANES), F32)

            def k_block(j, c):
                return tile(pl.multiple_of(j * TQ, TQ), c, None)

            c = lax.fori_loop(0, i, k_block, (z1,) * (2 * NH) + (zl,) * NH)
            c = tile(q0, c, vis)
            for pr in range(PP):
                dq = jnp.where(masks[0], c[2 * NH + 2 * pr], c[2 * NH + 2 * pr + 1]) * scale
                dq_ref[pl.ds(q0, TQ), pr * LANES:(pr + 1) * LANES] = dq.astype(BF16)
            return carry

        lax.fori_loop(0, nq, q_block, 0)
        dk_ref[...] = dk_s[...].T.astype(BF16)
        dv_ref[...] = dv_s[...].T.astype(BF16)

    blk = lambda off: pl.BlockSpec((seq, W), lambda b, g: (b, off + g))
    out_blk = pl.BlockSpec((seq, W), lambda b, g: (b, g))
    return pl.pallas_call(
        body, name=name, grid=(B, nstep),
        in_specs=[blk(P_Q // W), blk(P_K // W), blk(P_V // W), out_blk, out_blk],
        out_specs=[out_blk, out_blk, out_blk],
        out_shape=[jax.ShapeDtypeStruct((T, SB_W), BF16) for _ in range(3)],
        scratch_shapes=[pltpu.VMEM((seq, NH * LANES), BF16), pltpu.VMEM((seq, W), BF16), pltpu.VMEM((seq, W), BF16),
                        pltpu.VMEM((seq, NH * LANES), BF16), pltpu.VMEM((W, seq), F32), pltpu.VMEM((W, seq), F32)],
        compiler_params=_cparams(("parallel", "parallel")),
    )(p, p, p, ltot, do)


def _mla_masks():
    lane = lax.broadcasted_iota(jnp.int32, (1, 2 * LANES), 1)
    ma = (lane < MLA_NOPE) | ((lane >= LANES) & (lane < LANES + MLA_ROPE))
    mb = ((lane >= MLA_NOPE) & (lane < LANES)) | ((lane >= LANES + MLA_ROPE) & (lane < LANES + 2 * MLA_ROPE))
    return ma, mb


def _mla_fwd(qm, kvm, krt, *, seq, name):
    T = qm.shape[0]
    B = T // seq
    TQ = ATT_T
    nq = seq // TQ
    PP = ATT_PAIRS
    W = PP * LANES
    nstep = MLA_W // W
    NH = 2 * PP
    CW = 2 * LANES
    scale = MLA_QK ** -0.5

    def body(qn_ref, qr_ref, kn_ref, v_ref, kr_ref, o_ref, lse_ref, q_s, kc_s, v_s):
        hm = _head_masks()
        mm = _mla_masks()
        v = v_ref[...]
        for pr in range(PP):
            ps = slice(pr * LANES, (pr + 1) * LANES)
            qc = jnp.concatenate([qn_ref[:, ps], qr_ref[:, ps]], axis=1)
            kc_s[:, pr * CW:(pr + 1) * CW] = jnp.concatenate([kn_ref[:, ps], kr_ref[...]], axis=1)
            for e in range(2):
                h = 2 * pr + e
                q_s[:, h * CW:(h + 1) * CW] = _pick(mm[e], qc)
                v_s[:, h * LANES:(h + 1) * LANES] = _pick(hm[e], v[:, ps])
        vis = _causal_mask(TQ, False)

        def q_block(i, carry):
            q0 = pl.multiple_of(i * TQ, TQ)
            qs = [q_s[pl.ds(q0, TQ), h * CW:(h + 1) * CW] for h in range(NH)]

            def tile(k0, c, mask):
                ms, ls, accs = list(c[:NH]), list(c[NH:2 * NH]), list(c[2 * NH:])
                ss = [_dot_nt(qs[h], kc_s[pl.ds(k0, TQ), (h // 2) * CW:(h // 2 + 1) * CW]) * scale for h in range(NH)]
                if mask is not None:
                    ss = [jnp.where(mask, s, NEG_BIG) for s in ss]
                m_new = [jnp.maximum(ms[h], jnp.max(ss[h], axis=1, keepdims=True)) for h in range(NH)]
                alphas = [jnp.exp(ms[h] - m_new[h]) for h in range(NH)]
                prs = [jnp.exp(ss[h] - m_new[h]) for h in range(NH)]
                outs = [_dot(prs[h].astype(BF16), v_s[pl.ds(k0, TQ), h * LANES:(h + 1) * LANES]) for h in range(NH)]
                ls = [alphas[h] * ls[h] + jnp.sum(prs[h], axis=1, keepdims=True) for h in range(NH)]
                for pr in range(PP):
                    accs[pr] = accs[pr] * jnp.where(hm[0], alphas[2 * pr], alphas[2 * pr + 1]) + outs[2 * pr] + outs[2 * pr + 1]
                return tuple(m_new) + tuple(ls) + tuple(accs)

            neg = jnp.full((TQ, 1), NEG_BIG, F32)
            z1 = jnp.zeros((TQ, 1), F32)

            def k_block(j, c):
                return tile(pl.multiple_of(j * TQ, TQ), c, None)

            c = lax.fori_loop(0, i, k_block, (neg,) * NH + (z1,) * NH + (jnp.zeros((TQ, LANES), F32),) * PP)
            c = tile(q0, c, vis)
            for pr in range(PP):
                ps = slice(pr * LANES, (pr + 1) * LANES)
                m_a, m_b, l_a, l_b = c[2 * pr], c[2 * pr + 1], c[NH + 2 * pr], c[NH + 2 * pr + 1]
                o_ref[pl.ds(q0, TQ), ps] = c[2 * NH + pr] / jnp.where(hm[0], l_a, l_b)
                lse_ref[pl.ds(q0, TQ), ps] = jnp.where(hm[0], m_a + jnp.log(l_a), m_b + jnp.log(l_b))
            return carry

        lax.fori_loop(0, nq, q_block, 0)

    blk = lambda off: pl.BlockSpec((seq, W), lambda b, g: (b, off + g))
    out_blk = pl.BlockSpec((seq, W), lambda b, g: (b, g))
    return pl.pallas_call(
        body, name=name, grid=(B, nstep),
        in_specs=[blk(0), blk(nstep), blk(0), blk(nstep), pl.BlockSpec((seq, LANES), lambda b, g: (b, 0))],
        out_specs=[out_blk, out_blk],
        out_shape=[jax.ShapeDtypeStruct((T, MLA_W), F32), jax.ShapeDtypeStruct((T, MLA_W), F32)],
        scratch_shapes=[pltpu.VMEM((seq, NH * CW), BF16), pltpu.VMEM((seq, PP * CW), BF16), pltpu.VMEM((seq, NH * LANES), BF16)],
        compiler_params=_cparams(("parallel", "parallel")),
    )(qm, qm, kvm, kvm, krt)


def _mla_bwd(qm, kvm, krt, o, lse, do, *, seq, name):
    T = qm.shape[0]
    B = T // seq
    TQ = ATT_T
    nq = seq // TQ
    PP = ATT_PAIRS
    W = PP * LANES
    nstep = MLA_W // W
    NH = 2 * PP
    CW = 2 * LANES
    scale = MLA_QK ** -0.5

    def body(qn_ref, qr_ref, kn_ref, v_ref, kr_ref, o_ref, lse_ref, do_ref,
             dqn_ref, dqr_ref, dkn_ref, dv_ref, dkr_ref, q_s, kc_s, do_s, dkc_s, dv_s):
        hm = _head_masks()
        mm = _mla_masks()
        dof = do_ref[...]
        for pr in range(PP):
            ps = slice(pr * LANES, (pr + 1) * LANES)
            qc = jnp.concatenate([qn_ref[:, ps], qr_ref[:, ps]], axis=1)
            kc_s[:, pr * CW:(pr + 1) * CW] = jnp.concatenate([kn_ref[:, ps], kr_ref[...]], axis=1)
            for e in range(2):
                h = 2 * pr + e
                q_s[:, h * CW:(h + 1) * CW] = _pick(mm[e], qc)
                do_s[:, h * LANES:(h + 1) * LANES] = _pick(hm[e], dof[:, ps]).astype(BF16)
        dkc_s[...] = jnp.zeros_like(dkc_s)
        dv_s[...] = jnp.zeros_like(dv_s)
        vis = _causal_mask(TQ, False)

        def q_block(i, carry):
            q0 = pl.multiple_of(i * TQ, TQ)
            qs = [q_s[pl.ds(q0, TQ), h * CW:(h + 1) * CW] for h in range(NH)]
            dos = [do_s[pl.ds(q0, TQ), h * LANES:(h + 1) * LANES] for h in range(NH)]
            lse_t = lse_ref[pl.ds(q0, TQ), :]
            dd = do_ref[pl.ds(q0, TQ), :] * o_ref[pl.ds(q0, TQ), :]
            lses, ds_ = [], []
            for h in range(NH):
                ps = slice((h // 2) * LANES, (h // 2 + 1) * LANES)
                lses.append(_lane_value(lse_t[:, ps], (h % 2) * MLA_V))
                ds_.append(jnp.sum(_pick(hm[h % 2], dd[:, ps]), axis=1, keepdims=True))

            def tile(k0, c, mask):
                accs = list(c)
                kcs = [kc_s[pl.ds(k0, TQ), pr * CW:(pr + 1) * CW] for pr in range(PP)]
                vjs = [v_ref[pl.ds(k0, TQ), pr * LANES:(pr + 1) * LANES] for pr in range(PP)]
                ss = [_dot_nt(qs[h], kcs[h // 2]) * scale for h in range(NH)]
                dps = [_dot_nt(dos[h], vjs[h // 2]) for h in range(NH)]
                p_l, ds_l = [], []
                for h in range(NH):
                    pr_ = jnp.exp(ss[h] - lses[h])
                    if mask is not None:
                        pr_ = jnp.where(mask, pr_, 0.0)
                    p_l.append(pr_.astype(BF16))
                    ds_l.append((pr_ * (dps[h] - ds_[h]) * scale).astype(BF16))
                for h in range(NH):
                    accs[h] = accs[h] + _dot(ds_l[h], kcs[h // 2])
                for pr in range(PP):
                    ha, hb = 2 * pr, 2 * pr + 1
                    dkc_s[pl.ds(k0, TQ), pr * CW:(pr + 1) * CW] += _dot_tn(ds_l[ha], qs[ha]) + _dot_tn(ds_l[hb], qs[hb])
                    dv_s[pr * LANES:(pr + 1) * LANES, pl.ds(k0, TQ)] += _dot_tn(dos[ha], p_l[ha]) + _dot_tn(dos[hb], p_l[hb])
                return tuple(accs)

            zc = jnp.zeros((TQ, CW), F32)

            def k_block(j, c):
                return tile(pl.multiple_of(j * TQ, TQ), c, None)

            c = lax.fori_loop(0, i, k_block, (zc,) * NH)
            c = tile(q0, c, vis)
            for pr in range(PP):
                ps = slice(pr * LANES, (pr + 1) * LANES)
                dq = _pick(mm[0], c[2 * pr]) + _pick(mm[1], c[2 * pr + 1])
                dqn_ref[pl.ds(q0, TQ), ps] = dq[:, :LANES].astype(BF16)
                dqr_ref[pl.ds(q0, TQ), ps] = dq[:, LANES:]
            return carry

        lax.fori_loop(0, nq, q_block, 0)
        dkr = dkc_s[:, LANES:CW]
        for pr in range(PP):
            dkn_ref[:, pr * LANES:(pr + 1) * LANES] = dkc_s[:, pr * CW:pr * CW + LANES].astype(BF16)
            if pr > 0:
                dkr = dkr + dkc_s[:, pr * CW + LANES:(pr + 1) * CW]
        dv_ref[...] = dv_s[...].T.astype(BF16)
        g = pl.program_id(1)

        @pl.when(g == 0)
        def _():
            dkr_ref[...] = dkr

        @pl.when(g > 0)
        def _():
            dkr_ref[...] += dkr

    blk = lambda off: pl.BlockSpec((seq, W), lambda b, g: (b, off + g))
    out_blk = pl.BlockSpec((seq, W), lambda b, g: (b, g))
    one_blk = pl.BlockSpec((seq, LANES), lambda b, g: (b, 0))
    return pl.pallas_call(
        body, name=name, grid=(B, nstep),
        in_specs=[blk(0), blk(nstep), blk(0), blk(nstep), one_blk, out_blk, out_blk, out_blk],
        out_specs=[out_blk, out_blk, out_blk, out_blk, one_blk],
        out_shape=[jax.ShapeDtypeStruct((T, MLA_W), BF16), jax.ShapeDtypeStruct((T, MLA_W), F32),
                   jax.ShapeDtypeStruct((T, MLA_W), BF16), jax.ShapeDtypeStruct((T, MLA_W), BF16),
                   jax.ShapeDtypeStruct((T, LANES), F32)],
        scratch_shapes=[pltpu.VMEM((seq, NH * CW), BF16), pltpu.VMEM((seq, PP * CW), BF16), pltpu.VMEM((seq, NH * LANES), BF16),
                        pltpu.VMEM((seq, PP * CW), F32), pltpu.VMEM((W, seq), F32)],
        compiler_params=_cparams(("parallel", "arbitrary")),
    )(qm, qm, kvm, kvm, krt, o, lse, do)


def _rope_tables(pos_ref, invf_ref):
    ang = pos_ref[...].astype(F32) * invf_ref[...]
    first = (_lane_iota() % MLA_ROPE) < (MLA_ROPE // 2)
    return jnp.cos(ang), jnp.sin(ang), first


def _rope_apply(x, cos, sin, first):
    rot = jnp.where(first, -pltpu.roll(x, LANES - MLA_ROPE // 2, 1), pltpu.roll(x, MLA_ROPE // 2, 1))
    return x * cos + rot * sin


def _rope_apply_t(dy, cos, sin, first):
    dys = dy * sin
    rot_t = jnp.where(first, pltpu.roll(dys, LANES - MLA_ROPE // 2, 1), -pltpu.roll(dys, MLA_ROPE // 2, 1))
    return dy * cos + rot_t


def _proj_uq_rope(p, g, wuq, pos, invf, *, tm, name):
    T = p.shape[0]
    ntile = MLA_W // LANES

    def body(x_ref, kr_ref, g_ref, w_ref, pos_ref, invf_ref, cq_ref, qm_ref, krt_ref):
        cos, sin, first = _rope_tables(pos_ref, invf_ref)
        hb = _rms(x_ref[...], g_ref[...]).astype(BF16)
        cq_ref[...] = hb
        q = jnp.dot(hb, w_ref[...], preferred_element_type=F32)
        qm_ref[:, :MLA_W] = q[:, :MLA_W].astype(BF16)
        for t in range(ntile):
            sl = slice(MLA_W + t * LANES, MLA_W + (t + 1) * LANES)
            qm_ref[:, sl] = _rope_apply(q[:, sl], cos, sin, first).astype(BF16)
        krt_ref[...] = _rope_apply(kr_ref[...], cos, sin, first).astype(BF16)

    return pl.pallas_call(
        body, name=name, grid=(T // tm,),
        in_specs=[pl.BlockSpec((tm, Q_LORA), lambda i: (i, P_CQ // Q_LORA)), pl.BlockSpec((tm, LANES), lambda i: (i, P_KRT // LANES)),
                  pl.BlockSpec((1, Q_LORA), lambda i: (0, 0)), pl.BlockSpec((Q_LORA, 2 * MLA_W), lambda i: (0, 0)),
                  pl.BlockSpec((tm, 1), lambda i: (i, 0)), pl.BlockSpec((1, LANES), lambda i: (0, 0))],
        out_specs=[pl.BlockSpec((tm, Q_LORA), lambda i: (i, 0)), pl.BlockSpec((tm, 2 * MLA_W), lambda i: (i, 0)),
                   pl.BlockSpec((tm, LANES), lambda i: (i, 0))],
        out_shape=[jax.ShapeDtypeStruct((T, Q_LORA), BF16), jax.ShapeDtypeStruct((T, 2 * MLA_W), BF16),
                   jax.ShapeDtypeStruct((T, LANES), BF16)],
        compiler_params=_cparams(("parallel",)),
    )(p, p, g, wuq, pos, invf)


def _d_proj_uq_rope(dqn, dqr, dkr, wuq, cq, p, g, pos, invf, *, tm, name):
    T = dqn.shape[0]
    ntile = MLA_W // LANES

    def body(dqn_ref, dqr_ref, dkr_ref, w_ref, cq_ref, x_ref, g_ref, pos_ref, invf_ref,
             dw_ref, dx_ref, dg_ref, dkr_o_ref, dqm_s):
        cos, sin, first = _rope_tables(pos_ref, invf_ref)
        dqm_s[:, :MLA_W] = dqn_ref[...]
        for t in range(ntile):
            sl = slice(t * LANES, (t + 1) * LANES)
            dqm_s[:, MLA_W + t * LANES:MLA_W + (t + 1) * LANES] = _rope_apply_t(dqr_ref[:, sl], cos, sin, first).astype(BF16)
        dkr_o_ref[...] = _rope_apply_t(dkr_ref[...], cos, sin, first).astype(BF16)
        dqm = dqm_s[...]
        dy = lax.dot_general(dqm, w_ref[...], (((1,), (1,)), ((), ())), preferred_element_type=F32)
        dx, part = _rms_grad(dy, x_ref[...], g_ref[...])
        dx_ref[...] = dx.astype(BF16)
        _accumulate(dg_ref, part)
        _accumulate(dw_ref, _dot_tn(dqm, cq_ref[...]))

    half = pl.BlockSpec((tm, MLA_W), lambda i: (i, 0))
    tile = pl.BlockSpec((tm, LANES), lambda i: (i, 0))
    lat = pl.BlockSpec((tm, Q_LORA), lambda i: (i, 0))
    return pl.pallas_call(
        body, name=name, grid=(T // tm,),
        in_specs=[half, half, tile, pl.BlockSpec((Q_LORA, 2 * MLA_W), lambda i: (0, 0)), lat,
                  pl.BlockSpec((tm, Q_LORA), lambda i: (i, P_CQ // Q_LORA)), pl.BlockSpec((1, Q_LORA), lambda i: (0, 0)),
                  pl.BlockSpec((tm, 1), lambda i: (i, 0)), pl.BlockSpec((1, LANES), lambda i: (0, 0))],
        out_specs=[pl.BlockSpec((2 * MLA_W, Q_LORA), lambda i: (0, 0)), lat,
                   pl.BlockSpec((1, Q_LORA), lambda i: (0, 0)), tile],
        out_shape=[jax.ShapeDtypeStruct((2 * MLA_W, Q_LORA), F32), jax.ShapeDtypeStruct((T, Q_LORA), BF16),
                   jax.ShapeDtypeStruct((1, Q_LORA), F32), jax.ShapeDtypeStruct((T, LANES), BF16)],
        scratch_shapes=[pltpu.VMEM((tm, 2 * MLA_W), BF16)],
        compiler_params=_cparams(("arbitrary",)),
    )(dqn, dqr, dkr, wuq, cq, p, g, pos, invf)


def _d_proj_cat(pieces, b, x, g, *, tm, name, residual=None, col_block=0, out_dtype=F32, b_transposed=False,
                wgrad_act=None):
    M = pieces[0].shape[0]
    widths = [pc.shape[1] for pc in pieces]
    K = sum(widths)
    C = b.shape[1] if b_transposed else b.shape[0]
    n = len(pieces)
    fuse_w = wgrad_act is not None
    contract = (((1,), (0,)), ((), ())) if b_transposed else (((1,), (1,)), ((), ()))
    in_specs = [pl.BlockSpec((tm, w), lambda i: (i, 0)) for w in widths]
    in_specs += [pl.BlockSpec(b.shape, lambda i: (0, 0)), pl.BlockSpec((tm, C), lambda i: (i, col_block)),
                 pl.BlockSpec((1, C), lambda i: (0, 0))]
    args = list(pieces) + [b, x, g]
    if residual is not None:
        in_specs.append(pl.BlockSpec((tm, C), lambda i: (i, 0)))
        args.append(residual)
    if fuse_w:
        in_specs.append(pl.BlockSpec((tm, C), lambda i: (i, 0)))
        args.append(wgrad_act)

    def body(*refs):
        b_ref, x_ref, g_ref = refs[n:n + 3]
        first_ref, dx_ref, dg_ref = refs[-4:-1] if fuse_w else refs[-3:]
        cat_ref = refs[-1] if fuse_w else first_ref
        off = 0
        for r, w in zip(refs[:n], widths):
            cat_ref[:, off:off + w] = r[...]
            off += w
        cat = cat_ref[...]
        dy = lax.dot_general(cat, b_ref[...], contract, preferred_element_type=F32)
        dx, part = _rms_grad(dy, x_ref[...], g_ref[...])
        if residual is not None:
            dx = dx + refs[n + 3][...]
        dx_ref[...] = dx.astype(out_dtype)
        _accumulate(dg_ref, part)
        if fuse_w:
            act_ref = refs[n + 3 + (residual is not None)]
            _accumulate(first_ref, _dot_tn(act_ref[...], cat))

    first_spec = pl.BlockSpec((C, K), lambda i: (0, 0)) if fuse_w else pl.BlockSpec((tm, K), lambda i: (i, 0))
    first_shape = jax.ShapeDtypeStruct((C, K), F32) if fuse_w else jax.ShapeDtypeStruct((M, K), BF16)
    return pl.pallas_call(
        body, name=name, grid=(M // tm,), in_specs=in_specs,
        out_specs=[first_spec, pl.BlockSpec((tm, C), lambda i: (i, 0)), pl.BlockSpec((1, C), lambda i: (0, 0))],
        out_shape=[first_shape, jax.ShapeDtypeStruct((M, C), out_dtype), jax.ShapeDtypeStruct((1, C), F32)],
        scratch_shapes=[pltpu.VMEM((tm, K), BF16)] if fuse_w else [],
        compiler_params=_cparams(("arbitrary",)),
    )(*args)


def _heads_out(xa, xb, ga, gb, w, resid, *, tm, name):
    T, C = xa.shape
    N = w.shape[1]

    def body(xa_ref, xb_ref, ga_ref, gb_ref, w_ref, r_ref, oc_ref, o_ref):
        oc_ref[:, :C] = _rms(xa_ref[...], ga_ref[...]).astype(BF16)
        oc_ref[:, C:] = _rms(xb_ref[...], gb_ref[...]).astype(BF16)
        o_ref[...] = r_ref[...] + jnp.dot(oc_ref[...], w_ref[...], preferred_element_type=F32)

    row = pl.BlockSpec((tm, C), lambda i: (i, 0))
    gsp = pl.BlockSpec((1, C), lambda i: (0, 0))
    full = pl.BlockSpec((tm, N), lambda i: (i, 0))
    return pl.pallas_call(
        body, name=name, grid=(T // tm,),
        in_specs=[row, row, gsp, gsp, pl.BlockSpec((2 * C, N), lambda i: (0, 0)), full],
        out_specs=[pl.BlockSpec((tm, 2 * C), lambda i: (i, 0)), full],
        out_shape=[jax.ShapeDtypeStruct((T, 2 * C), BF16), jax.ShapeDtypeStruct((T, N), F32)],
        compiler_params=_cparams(("parallel",)),
    )(xa, xb, ga, gb, w, resid)


def _heads_out_bwd(dout, w, ocat, xa, xb, ga, gb, *, tm, name):
    T, C = xa.shape
    N = w.shape[1]
    n_steps = T // tm

    def body(d_ref, w_ref, oc_ref, xa_ref, xb_ref, ga_ref, gb_ref, dxa_ref, dxb_ref, dga_ref, dgb_ref, dw_ref, dw_s):
        db = d_ref[...].astype(BF16)
        dy = lax.dot_general(db, w_ref[...], (((1,), (1,)), ((), ())), preferred_element_type=F32)
        dxa, pa = _rms_grad(dy[:, :C], xa_ref[...], ga_ref[...])
        dxb, pb = _rms_grad(dy[:, C:], xb_ref[...], gb_ref[...])
        dxa_ref[...] = dxa
        dxb_ref[...] = dxb
        _accumulate(dga_ref, pa)
        _accumulate(dgb_ref, pb)
        _accumulate(dw_s, _dot_tn(oc_ref[...], db))

        @pl.when(pl.program_id(0) == n_steps - 1)
        def _():
            dw_ref[...] = dw_s[...].astype(BF16)

    row = pl.BlockSpec((tm, C), lambda i: (i, 0))
    gsp = pl.BlockSpec((1, C), lambda i: (0, 0))
    wsp = pl.BlockSpec((2 * C, N), lambda i: (0, 0))
    return pl.pallas_call(
        body, name=name, grid=(n_steps,),
        in_specs=[pl.BlockSpec((tm, N), lambda i: (i, 0)), wsp, pl.BlockSpec((tm, 2 * C), lambda i: (i, 0)), row, row, gsp, gsp],
        out_specs=[row, row, gsp, gsp, wsp],
        out_shape=[jax.ShapeDtypeStruct((T, C), F32), jax.ShapeDtypeStruct((T, C), F32),
                   jax.ShapeDtypeStruct((1, C), F32), jax.ShapeDtypeStruct((1, C), F32),
                   jax.ShapeDtypeStruct((2 * C, N), BF16)],
        scratch_shapes=[pltpu.VMEM((2 * C, N), F32)],
        compiler_params=_cparams(("arbitrary",)),
    )(dout, w, ocat, xa, xb, ga, gb)


CONV_ROWS = 256
HALO = 8


def _conv_taps(w_ref):
    return w_ref[0:1, :], w_ref[1:2, :], w_ref[2:3, :]


def _conv_rows(cur, prev, w, bias):
    ext = jnp.concatenate([prev, cur], axis=0)
    u1 = pltpu.roll(ext, 1, 0)[HALO:]
    u2 = pltpu.roll(ext, 2, 0)[HALO:]
    return w[2] * cur + w[1] * u1 + w[0] * u2 + bias, u1, u2


def _conv_fwd(u, w, bias, *, seq, name):
    T = u.shape[0]
    B = T // seq
    W2 = 2 * FF_BLK

    def body(u_ref, w_ref, b_ref, a_ref):
        wv = _conv_taps(w_ref)
        bv = b_ref[...]
        for c in range(seq // CONV_ROWS):
            r0 = c * CONV_ROWS
            cur = u_ref[r0:r0 + CONV_ROWS, :]
            prev = u_ref[r0 - HALO:r0, :] if c > 0 else jnp.zeros((HALO, W2), F32)
            y, _, _ = _conv_rows(cur, prev, wv, bv)
            gc = y[:, :FF_BLK]
            a_ref[r0:r0 + CONV_ROWS, :] = (gc * (1.0 / (1.0 + jnp.exp(-gc))) * y[:, FF_BLK:]).astype(BF16)

    return pl.pallas_call(
        body, name=name, grid=(B, N_FF_BLK),
        in_specs=[pl.BlockSpec((seq, W2), lambda b, j: (b, j)), pl.BlockSpec((3, W2), lambda b, j: (0, j)),
                  pl.BlockSpec((1, W2), lambda b, j: (0, j))],
        out_specs=pl.BlockSpec((seq, FF_BLK), lambda b, j: (b, j)),
        out_shape=jax.ShapeDtypeStruct((T, D_FF), BF16),
        compiler_params=_cparams(("parallel", "parallel")),
    )(u, w, bias)


def _conv_bwd(u, dx2, wdn, w, bias, *, seq, name):
    T = u.shape[0]
    B = T // seq
    D = dx2.shape[1]
    W2 = 2 * FF_BLK
    nchunk = seq // CONV_ROWS

    def body(u_ref, dx_ref, wd_ref, w_ref, b_ref, du_ref, dw_ref, db_ref, dwd_ref, duc_s, dwd_s):
        wv = _conv_taps(w_ref)
        bv = b_ref[...]
        wd = wd_ref[...]
        zrow = jnp.zeros((1, W2), F32)
        dw0, dw1, dw2, dbs = zrow, zrow, zrow, zrow
        dwd = jnp.zeros((FF_BLK, D), F32)
        for c in range(nchunk):
            r0 = c * CONV_ROWS
            cur = u_ref[r0:r0 + CONV_ROWS, :]
            prev = u_ref[r0 - HALO:r0, :] if c > 0 else jnp.zeros((HALO, W2), F32)
            y, u1, u2 = _conv_rows(cur, prev, wv, bv)
            gc = y[:, :FF_BLK]
            vc = y[:, FF_BLK:]
            sg = 1.0 / (1.0 + jnp.exp(-gc))
            dxc = dx_ref[r0:r0 + CONV_ROWS, :].astype(BF16)
            dav = _dot_nt(dxc, wd)
            silu = gc * sg
            dwd = dwd + _dot_tn((silu * vc).astype(BF16), dxc)
            duc = jnp.concatenate([dav * vc * (sg * (1.0 + gc * (1.0 - sg))), dav * silu], axis=1)
            duc_s[r0:r0 + CONV_ROWS, :] = duc
            dw0 = dw0 + jnp.sum(duc * u2, axis=0, keepdims=True)
            dw1 = dw1 + jnp.sum(duc * u1, axis=0, keepdims=True)
            dw2 = dw2 + jnp.sum(duc * cur, axis=0, keepdims=True)
            dbs = dbs + jnp.sum(duc, axis=0, keepdims=True)
        duc_s[seq:seq + HALO, :] = jnp.zeros((HALO, W2), F32)
        n_ext = CONV_ROWS + HALO
        for c in range(nchunk):
            r0 = c * CONV_ROWS
            ext = duc_s[r0:r0 + n_ext, :]
            s1 = pltpu.roll(ext, n_ext - 1, 0)[:CONV_ROWS]
            s2 = pltpu.roll(ext, n_ext - 2, 0)[:CONV_ROWS]
            du_ref[r0:r0 + CONV_ROWS, :] = (wv[2] * ext[:CONV_ROWS] + wv[1] * s1 + wv[0] * s2).astype(BF16)

        b = pl.program_id(1)

        @pl.when(b == 0)
        def _():
            dw_ref[0:1, :] = dw0
            dw_ref[1:2, :] = dw1
            dw_ref[2:3, :] = dw2
            db_ref[...] = dbs
            dwd_s[...] = dwd

        @pl.when(b > 0)
        def _():
            dw_ref[0:1, :] += dw0
            dw_ref[1:2, :] += dw1
            dw_ref[2:3, :] += dw2
            db_ref[...] += dbs
            dwd_s[...] += dwd

        @pl.when(b == B - 1)
        def _():
            dwd_ref[...] = dwd_s[...].astype(BF16)

    return pl.pallas_call(
        body, name=name, grid=(N_FF_BLK, B),
        in_specs=[pl.BlockSpec((seq, W2), lambda j, b: (b, j)), pl.BlockSpec((seq, D), lambda j, b: (b, 0)),
                  pl.BlockSpec((FF_BLK, D), lambda j, b: (j, 0)),
                  pl.BlockSpec((3, W2), lambda j, b: (0, j)), pl.BlockSpec((1, W2), lambda j, b: (0, j))],
        out_specs=[pl.BlockSpec((seq, W2), lambda j, b: (b, j)), pl.BlockSpec((3, W2), lambda j, b: (0, j)),
                   pl.BlockSpec((1, W2), lambda j, b: (0, j)), pl.BlockSpec((FF_BLK, D), lambda j, b: (j, 0))],
        out_shape=[jax.ShapeDtypeStruct((T, 2 * D_FF), BF16), jax.ShapeDtypeStruct((3, 2 * D_FF), F32),
                   jax.ShapeDtypeStruct((1, 2 * D_FF), F32), jax.ShapeDtypeStruct((D_FF, D), BF16)],
        scratch_shapes=[pltpu.VMEM((seq + HALO, W2), F32), pltpu.VMEM((FF_BLK, D), F32)],
        compiler_params=_cparams(("parallel", "arbitrary")),
    )(u, dx2, wdn, w, bias)


def _place():
    return lax.axis_index("x"), lax.axis_index("y"), lax.axis_index("c")


def _other_chips(x, y):
    return [(1 - x, y), (x, 1 - y), (1 - x, 1 - y)]


def _all_gather(vs, *, name):
    n = len(vs)

    def body(*refs):
        v_refs, out_refs = refs[:n], refs[n:2 * n]
        send_sems, recv_sems, local_sems = refs[2 * n:]
        x, y, c = _place()
        me, sibling = (x, y, c), (x, y, 1 - c)
        chips = _other_chips(x, y)

        def slab(a, px, py, pc):
            return out_refs[a].at[4 * px + 2 * py + pc]

        def copy(a, k, block, to, src=None):
            return pltpu.make_async_remote_copy(
                src_ref=slab(a, *block) if src is None else src, dst_ref=slab(a, *block),
                send_sem=send_sems.at[7 * a + k], recv_sem=recv_sems.at[7 * a + k], device_id=to, device_id_type=MESH)

        mine = [pltpu.make_async_copy(v_refs[a], slab(a, *me), local_sems.at[a]) for a in range(n)]
        for cp in mine:
            cp.start()
        first = []
        for a in range(n):
            first.append(copy(a, 0, me, sibling, src=v_refs[a]))
            first += [copy(a, 1 + j, me, (*chip, c), src=v_refs[a]) for j, chip in enumerate(chips)]
        for cp in first:
            cp.start()
        passed = []
        for j, chip in enumerate(chips):
            for a in range(n):
                copy(a, 1 + j, (*chip, c), me).wait_recv()
                cp = copy(a, 4 + j, (*chip, c), sibling)
                cp.start()
                passed.append(cp)
        for a in range(n):
            copy(a, 0, sibling, me).wait_recv()
            for j, chip in enumerate(chips):
                copy(a, 4 + j, (*chip, 1 - c), me).wait_recv()
        for cp in first + passed:
            cp.wait_send()
        for cp in mine:
            cp.wait()

    return pl.pallas_call(
        body, name=name, in_specs=[ANY] * n, out_specs=[ANY] * n,
        out_shape=[jax.ShapeDtypeStruct((N_DEV,) + v.shape, v.dtype) for v in vs],
        scratch_shapes=[pltpu.SemaphoreType.DMA((7 * n,)), pltpu.SemaphoreType.DMA((7 * n,)), pltpu.SemaphoreType.DMA((n,))],
    )(*vs)


def _all_gather_async(vs, *, name, collective_id):
    n = len(vs)
    v_refs = [jax.new_ref(v, memory_space=pltpu.MemorySpace.HBM) for v in vs]
    out_refs = [jax.empty_ref(jax.ShapeDtypeStruct((N_DEV,) + v.shape, v.dtype), memory_space=pltpu.MemorySpace.HBM)
                for v in vs]

    @pl.kernel(mesh=plsc.ScalarSubcoreMesh(axis_name="seq", num_cores=1), name=name,
               scratch_types=(pltpu.SemaphoreType.DMA((7 * n,)), pltpu.SemaphoreType.DMA((7 * n,)),
                              pltpu.SemaphoreType.DMA((n,))),
               compiler_params=pltpu.CompilerParams(collective_id=collective_id))
    def launch(send_sems, recv_sems, local_sems):
        x, y, c = _place()
        me, sibling = (x, y, c), (x, y, 1 - c)
        chips = _other_chips(x, y)
        peers = [sibling] + [(*chip, c) for chip in chips]
        barrier = pltpu.get_barrier_semaphore()
        for peer in peers:
            pl.semaphore_signal(barrier, inc=1, device_id=peer, device_id_type=MESH)
        pl.semaphore_wait(barrier, len(peers))

        def slab(a, px, py, pc):
            return out_refs[a].at[4 * px + 2 * py + pc]

        def copy(a, k, block, to, src=None):
            return pltpu.make_async_remote_copy(
                src_ref=slab(a, *block) if src is None else src, dst_ref=slab(a, *block),
                send_sem=send_sems.at[7 * a + k], recv_sem=recv_sems.at[7 * a + k], device_id=to, device_id_type=MESH)

        mine = [pltpu.make_async_copy(v_refs[a], slab(a, *me), local_sems.at[a]) for a in range(n)]
        for cp in mine:
            cp.start()
        first = []
        for a in range(n):
            first.append(copy(a, 0, me, sibling, src=v_refs[a]))
            first += [copy(a, 1 + j, me, (*chip, c), src=v_refs[a]) for j, chip in enumerate(chips)]
        for cp in first:
            cp.start()
        passed = []
        for j, chip in enumerate(chips):
            for a in range(n):
                copy(a, 1 + j, (*chip, c), me).wait_recv()
                cp = copy(a, 4 + j, (*chip, c), sibling)
                cp.start()
                passed.append(cp)
        for a in range(n):
            copy(a, 0, sibling, me).wait_recv()
            for j, chip in enumerate(chips):
                copy(a, 4 + j, (*chip, 1 - c), me).wait_recv()
        for cp in first + passed:
            cp.wait_send()
        for cp in mine:
            cp.wait()

    launch()
    return [r[...] for r in out_refs]


def _handshake(peers):
    barrier = pltpu.get_barrier_semaphore()
    for peer in peers:
        pl.semaphore_signal(barrier, inc=1, device_id=peer, device_id_type=MESH)
    pl.semaphore_wait(barrier, len(peers))


def _hbm_refs(arrays, lead):
    src = [jax.new_ref(a, memory_space=pltpu.MemorySpace.HBM) for a in arrays]
    dst = [jax.empty_ref(jax.ShapeDtypeStruct((lead,) + a.shape[1:], a.dtype), memory_space=pltpu.MemorySpace.HBM)
           for a in arrays]
    return src, dst


def _peer(x, y, c, k):
    return ((1 - x) if k & 4 else x, (1 - y) if k & 2 else y, (1 - c) if k & 1 else c)


def _rs_direct_async(g8s, *, name, collective_id):
    n = len(g8s)
    g_refs, out_refs = _hbm_refs(g8s, N_DEV - 1)

    @pl.kernel(mesh=plsc.ScalarSubcoreMesh(axis_name="seq", num_cores=1), name=name,
               scratch_types=(pltpu.SemaphoreType.DMA((7 * n,)), pltpu.SemaphoreType.DMA((7 * n,))),
               compiler_params=pltpu.CompilerParams(collective_id=collective_id))
    def launch(send_sems, recv_sems):
        x, y, c = _place()
        peers = [_peer(x, y, c, k) for k in range(1, N_DEV)]
        _handshake(peers)
        copies = [
            pltpu.make_async_remote_copy(
                src_ref=g_refs[a].at[4 * px + 2 * py + pc], dst_ref=out_refs[a].at[k],
                send_sem=send_sems.at[7 * a + k], recv_sem=recv_sems.at[7 * a + k],
                device_id=(px, py, pc), device_id_type=MESH)
            for a in range(n) for k, (px, py, pc) in enumerate(peers)]
        for cp in copies:
            cp.start()
        for cp in copies:
            cp.wait()

    launch()
    return [r[...] for r in out_refs]


def _row_tile(rows):
    if rows <= 512:
        return rows
    return next(t for t in (512, 384, 352, 256, 128) if rows % t == 0)


def _split_moves(segments, chunk):
    moves = []
    for dst, src, length in segments:
        while length > 0:
            dev, off = divmod(src, chunk)
            take = min(length, chunk - off)
            moves.append((dst, dev, off, take))
            dst, src, length = dst + take, src + take, length - take
    return moves


def _assemble(stacked, segments, zero_spans, out_cols, *, name):
    _, R, c = stacked.shape
    tr = _row_tile(R)
    moves = _split_moves(segments, c)

    def body(x_ref, o_ref):
        for dst, dev, off, take in moves:
            o_ref[:, dst:dst + take] = x_ref[dev, :, off:off + take]
        for a, b in zero_spans:
            o_ref[:, a:b] = jnp.zeros((tr, b - a), o_ref.dtype)

    return pl.pallas_call(
        body, name=name, grid=(R // tr,),
        in_specs=[pl.BlockSpec((N_DEV, tr, c), lambda i: (0, i, 0))],
        out_specs=pl.BlockSpec((tr, out_cols), lambda i: (i, 0)),
        out_shape=jax.ShapeDtypeStruct((R, out_cols), stacked.dtype),
        compiler_params=_cparams(("parallel",)),
    )(stacked)


def _disassemble(full, segments, chunk, *, name, out_dtype=F32):
    R = full.shape[0]
    tr = _row_tile(R)
    moves = _split_moves(segments, chunk)

    def body(x_ref, o_ref):
        seen = set()
        for dst, dev, off, take in moves:
            piece = x_ref[:, dst:dst + take]
            if (dev, off) in seen:
                piece = piece + o_ref[dev, :, off:off + take]
            seen.add((dev, off))
            o_ref[dev, :, off:off + take] = piece.astype(out_dtype)

    return pl.pallas_call(
        body, name=name, grid=(R // tr,),
        in_specs=[pl.BlockSpec((tr, full.shape[1]), lambda i: (i, 0))],
        out_specs=pl.BlockSpec((N_DEV, tr, chunk), lambda i: (0, i, 0)),
        out_shape=jax.ShapeDtypeStruct((N_DEV, R, chunk), out_dtype),
        compiler_params=_cparams(("parallel",)),
    )(full)


def _assemble_rows(stacked, segments, zero_spans, out_rows, *, name):
    _, c, R = stacked.shape
    tc = next(t for t in (2 * LANES, LANES) if R % t == 0)
    moves = _split_moves(segments, c)

    def body(x_ref, o_ref):
        for dst, dev, off, take in moves:
            o_ref[dst:dst + take, :] = x_ref[dev, off:off + take, :]
        for a, b in zero_spans:
            o_ref[a:b, :] = jnp.zeros((b - a, tc), o_ref.dtype)

    return pl.pallas_call(
        body, name=name, grid=(R // tc,),
        in_specs=[pl.BlockSpec((N_DEV, c, tc), lambda i: (0, 0, i))],
        out_specs=pl.BlockSpec((out_rows, tc), lambda i: (0, i)),
        out_shape=jax.ShapeDtypeStruct((out_rows, R), stacked.dtype),
        compiler_params=_cparams(("parallel",)),
    )(stacked)


def _disassemble_rows(full_t, segments, chunk, *, name, out_dtype=F32):
    R = full_t.shape[1]
    tc = next(t for t in (2 * LANES, LANES) if R % t == 0)
    moves = _split_moves(segments, chunk)

    def body(x_ref, o_ref):
        seen = set()
        for dst, dev, off, take in moves:
            piece = x_ref[dst:dst + take, :]
            if (dev, off) in seen:
                piece = piece + o_ref[dev, off:off + take, :]
            seen.add((dev, off))
            o_ref[dev, off:off + take, :] = piece.astype(out_dtype)

    return pl.pallas_call(
        body, name=name, grid=(R // tc,),
        in_specs=[pl.BlockSpec((full_t.shape[0], tc), lambda i: (0, i))],
        out_specs=pl.BlockSpec((N_DEV, chunk, tc), lambda i: (0, 0, i)),
        out_shape=jax.ShapeDtypeStruct((N_DEV, chunk, R), out_dtype),
        compiler_params=_cparams(("parallel",)),
    )(full_t)


_O_CQ = 3 * SB_W
_O_CKV = _O_CQ + Q_LORA
_O_KR = _O_CKV + KV_LORA
SEG_W_IN = ((0, 0, 3 * SB_W), (P_CKV, _O_CKV, KV_LORA), (P_KRT, _O_KR, MLA_ROPE), (P_KRT + MLA_ROPE, _O_KR, MLA_ROPE),
            (P_CQ, _O_CQ, Q_LORA))
ZERO_W_IN = ((P_KRT + 2 * MLA_ROPE, P_CQ),)
SEG_W_UQ = tuple((MLA_NOPE * h, MLA_QK * h, MLA_NOPE) for h in range(MLA_HEADS)) + tuple(
    (MLA_W + LANES * (h // 2) + MLA_ROPE * (h % 2), MLA_QK * h + MLA_NOPE, MLA_ROPE) for h in range(MLA_HEADS))
ZERO_W_UQ = tuple((MLA_W + LANES * g + 2 * MLA_ROPE, MLA_W + LANES * (g + 1)) for g in range(MLA_HEADS // 2))
SEG_W_UKV = tuple((MLA_NOPE * h, (MLA_NOPE + MLA_V) * h, MLA_NOPE) for h in range(MLA_HEADS)) + tuple(
    (MLA_W + MLA_V * h, (MLA_NOPE + MLA_V) * h + MLA_NOPE, MLA_V) for h in range(MLA_HEADS))
SEG_W_UP = tuple((2 * FF_BLK * blk + FF_BLK * half, D_FF * half + FF_BLK * blk, FF_BLK)
                 for half in range(2) for blk in range(N_FF_BLK))


def _sum8(g, *, name):
    _, R, C = g.shape

    def body(g_ref, o_ref):
        acc = g_ref[0]
        for k in range(1, N_DEV):
            acc = acc + g_ref[k]
        o_ref[...] = acc

    return pl.pallas_call(
        body, name=name, out_shape=jax.ShapeDtypeStruct((R, C), F32),
    )(g)


def _adamw_math(w, gf, m, v):
    c1 = 1.0 / (1.0 - ADAM_B1 ** ADAM_STEP)
    c2 = 1.0 / (1.0 - ADAM_B2 ** ADAM_STEP)
    mn = ADAM_B1 * m + (1.0 - ADAM_B1) * gf
    vn = ADAM_B2 * v + (1.0 - ADAM_B2) * (gf * gf)
    return -ADAM_LR * ((mn * c1) / (jnp.sqrt(vn * c2) + ADAM_EPS) + ADAM_WD * w), mn, vn


def _adamw(w, g, m, v, *, name):
    R, C = w.shape
    tr = _row_tile(R)

    def body(w_ref, g_ref, m_ref, v_ref, d_ref, mo_ref, vo_ref):
        d_ref[...], mo_ref[...], vo_ref[...] = _adamw_math(w_ref[...], g_ref[...], m_ref[...], v_ref[...])

    blk = pl.BlockSpec((tr, C), lambda i: (i, 0))
    shp = jax.ShapeDtypeStruct((R, C), F32)
    return pl.pallas_call(
        body, name=name, grid=(R // tr,), in_specs=[blk] * 4, out_specs=[blk] * 3,
        out_shape=[shp, shp, shp], compiler_params=_cparams(("parallel",)),
    )(w, g, m, v)


def _adamw_rs8(g8, r7, me_idx, w, m, v, *, name):
    R, C = w.shape
    tr = _row_tile(R)

    def body(i_ref, f_ref, r_ref, w_ref, m_ref, v_ref, g_ref, d_ref, mo_ref, vo_ref):
        gf = f_ref[...].astype(F32)
        for k in range(N_DEV - 1):
            gf = gf + r_ref[k].astype(F32)
        g_ref[...] = gf
        d_ref[...], mo_ref[...], vo_ref[...] = _adamw_math(w_ref[...], gf, m_ref[...], v_ref[...])

    blk = pl.BlockSpec((tr, C), lambda i, i_ref: (i, 0))
    shp = jax.ShapeDtypeStruct((R, C), F32)
    return pl.pallas_call(
        body, name=name,
        grid_spec=pltpu.PrefetchScalarGridSpec(
            num_scalar_prefetch=1, grid=(R // tr,),
            in_specs=[pl.BlockSpec((None, tr, C), lambda i, i_ref: (i_ref[0], i, 0)),
                      pl.BlockSpec((N_DEV - 1, tr, C), lambda i, i_ref: (0, i, 0)), blk, blk, blk],
            out_specs=[blk] * 4),
        out_shape=[shp] * 4, compiler_params=_cparams(("parallel",)),
    )(me_idx, g8, r7, w, m, v)


def _ff_interleave(a):
    lead = a.shape[:-1]
    return a.reshape(*lead, 2, N_FF_BLK, FF_BLK).swapaxes(-3, -2).reshape(*lead, 2 * D_FF)


def _ff_deinterleave(a):
    lead = a.shape[:-1]
    return a.reshape(*lead, N_FF_BLK, 2, FF_BLK).swapaxes(-3, -2).reshape(*lead, 2 * D_FF)


SMALL =(("g_mix", D_MODEL), ("g_cq", Q_LORA), ("g_ckv", KV_LORA), ("g_sb_out", SB_W), ("g_mla_out", MLA_W),
         ("g_ffn", D_MODEL), ("conv_b", 2 * D_FF), ("g_final", D_MODEL))
SMALL_ROWS = 88


SMALL_USED = sum(size for _, size in SMALL)


def _pack_small(d, tail=None):
    parts = [d[n].reshape(-1) for n, _ in SMALL] + ([] if tail is None else [tail])
    flat = jnp.concatenate(parts)
    flat = jnp.pad(flat, (0, SMALL_ROWS * LANES - flat.shape[0]))
    return flat.reshape(SMALL_ROWS, LANES)


def _unpack_small(a):
    flat = a.reshape(-1)
    out, off = {}, 0
    for n, size in SMALL:
        out[n] = flat[off:off + size]
        off += size
    return out


def kernel(x, positions, g_mix, w_in, g_cq, w_uq, g_ckv, w_ukv, g_sb_out, g_mla_out, w_out, g_ffn, w_up, conv_w, conv_b, w_down, g_final, loss_target, m_g_mix, m_w_in, m_g_cq, m_w_uq, m_g_ckv, m_w_ukv, m_g_sb_out, m_g_mla_out, m_w_out, m_g_ffn, m_w_up, m_conv_w, m_conv_b, m_w_down, m_g_final, v_g_mix, v_w_in, v_g_cq, v_w_uq, v_g_ckv, v_w_ukv, v_g_sb_out, v_g_mla_out, v_w_out, v_g_ffn, v_w_up, v_conv_w, v_conv_b, v_w_down, v_g_final):
    B, S, D = x.shape
    T = B * S
    xf = x.reshape(T, D)
    tgt = loss_target.reshape(T, D)
    pos = positions.reshape(T, 1)
    half = MLA_ROPE // 2
    inv_freq = 1.0 / (ROPE_BASE ** (jnp.arange(half, dtype=F32) * (2.0 / MLA_ROPE)))
    invf = jnp.tile(inv_freq, LANES // half).reshape(1, LANES)
    me_idx = (4 * lax.axis_index("x") + 2 * lax.axis_index("y") + lax.axis_index("c")).astype(jnp.int32).reshape(1)

    names = ("w_in", "w_uq", "w_ukv", "w_out", "w_up", "w_down", "conv_w")
    shard = {"w_in": w_in[0], "w_uq": w_uq[0], "w_ukv": w_ukv[0], "w_out": w_out[0], "w_up": w_up[0],
             "w_down": w_down[0], "conv_w": conv_w[0]}
    sent = {n: shard[n] if n == "conv_w" else shard[n].astype(BF16) for n in names}
    later = names[1:]
    w_in_all = _all_gather([jnp.transpose(shard["w_in"]).astype(BF16)], name="ag_w_in")[0]
    w_in_all, rest = lax.optimization_barrier((w_in_all, [sent[n] for n in later]))
    got = {"w_in": w_in_all}
    got.update(zip(later, _all_gather_async(rest, name="ag_weights_async", collective_id=0)))
    wi_t = _assemble_rows(got["w_in"], SEG_W_IN, ZERO_W_IN, P_COLS, name="asm_w_in")
    wuq = _assemble(got["w_uq"], SEG_W_UQ, ZERO_W_UQ, 2 * MLA_W, name="asm_w_uq")
    wukv = _assemble(got["w_ukv"], SEG_W_UKV, (), 2 * MLA_W, name="asm_w_ukv")
    wup = _assemble(got["w_up"], SEG_W_UP, (), 2 * D_FF, name="asm_w_up")
    cwi = _assemble(got["conv_w"], SEG_W_UP, (), 2 * D_FF, name="asm_conv_w")
    wo = got["w_out"].reshape(D, D)
    wdn = got["w_down"].reshape(D_FF, D)
    cbi = _ff_interleave(conv_b)

    h, p = _rms_matmul_nn(xf, g_mix, wi_t, tm=512, name="proj_in", w_transposed=True)
    o_sb, ltot = _sb_fwd(p, seq=S, name="sb_fwd")
    cq, qm, krt = _proj_uq_rope(p, g_cq, wuq, pos, invf, tm=512, name="proj_uq")
    ckv, kvm = _rms_matmul_nn(p, g_ckv, wukv, tm=512, name="proj_ukv", col_block=P_CKV // KV_LORA, out_dtype=BF16)
    o_mla, lse = _mla_fwd(qm, kvm, krt, seq=S, name="mla_fwd")
    ocat, x1 = _heads_out(o_sb, o_mla, g_sb_out, g_mla_out, wo, xf, tm=512, name="proj_out")
    hf, u = _rms_matmul_nn(x1, g_ffn, wup, tm=256, name="ffn_up")
    a = _conv_fwd(u, cwi, cbi, seq=S, name="conv_fwd")
    dx2, dg_final, loss_row = _matmul_nn_loss(a, wdn, x1, g_final.reshape(1, D), tgt, tm=512, name="ffn_down_loss")

    du, dcw, dcb, dw_down = _conv_bwd(u, dx2, wdn, cwi, cbi, seq=S, name="conv_bwd")
    dw_up_t = _matmul_tn(du, hf, tm=D_FF, tn=1024, tk=1024, name="dw_up")
    dx1, dg_ffn = _matmul_nt_rms_bwd(du, wup, x1, g_ffn, tm=512, name="d_ffn_up", residual=dx2)
    do_sb, do_mla, dg_sb, dg_mla, dw_out = _heads_out_bwd(dx1, wo, ocat, o_sb, o_mla, g_sb_out, g_mla_out, tm=512,
                                                          name="d_proj_out")

    early = ("w_down", "w_up", "conv_w", "w_out")
    g8 = {"w_up": _disassemble_rows(dw_up_t, SEG_W_UP, shard["w_up"].shape[1], name="split_dw_up", out_dtype=BF16),
          "conv_w": _disassemble(dcw, SEG_W_UP, shard["conv_w"].shape[1], name="split_dconv_w", out_dtype=BF16),
          "w_out": dw_out.reshape((N_DEV,) + shard["w_out"].shape),
          "w_down": dw_down.reshape((N_DEV,) + shard["w_down"].shape)}
    r7 = dict(zip(early, _rs_direct_async([g8[n] for n in early], name="rs_direct_async", collective_id=1)))

    dq_sb, dk_sb, dv_sb = _sb_bwd(p, ltot, do_sb, seq=S, name="sb_bwd")
    dqn, dqr, dkn, dvm, dkr = _mla_bwd(qm, kvm, krt, o_mla, lse, do_mla, seq=S, name="mla_bwd")
    dw_uq_t, dcq, dg_cq, dkr_u = _d_proj_uq_rope(dqn, dqr, dkr, wuq, cq, p, g_cq, pos, invf, tm=512, name="d_proj_uq")
    dw_ukv, dckv, dg_ckv = _d_proj_cat([dkn, dvm], wukv, p, g_ckv, tm=512, name="d_proj_ukv",
                                       col_block=P_CKV // KV_LORA, out_dtype=BF16, wgrad_act=ckv)
    dp, dx, dg_mix = _d_proj_cat([dq_sb, dk_sb, dv_sb, dckv, dkr_u, dcq], wi_t, xf, g_mix, tm=512, name="d_proj_in",
                                 residual=dx1, b_transposed=True)
    dw_in_t = _matmul_tn(dp, h, tm=P_COLS, tn=1024, tk=1024, name="dw_in")

    late = ("w_in", "w_uq", "w_ukv")
    g8.update({"w_in": _disassemble_rows(dw_in_t, SEG_W_IN, shard["w_in"].shape[1], name="split_dw_in", out_dtype=BF16),
               "w_uq": _disassemble_rows(dw_uq_t, SEG_W_UQ, shard["w_uq"].shape[1], name="split_dw_uq", out_dtype=BF16),
               "w_ukv": _disassemble(dw_ukv, SEG_W_UKV, shard["w_ukv"].shape[1], name="split_dw_ukv", out_dtype=BF16)})
    tied, g8["w_in"] = lax.optimization_barrier(([r7[n] for n in early], g8["w_in"]))
    r7.update(zip(early, tied))
    r7.update(zip(late, _rs_direct_async([g8[n] for n in late], name="rs_direct_late", collective_id=3)))

    params = {"w_in": (w_in, m_w_in, v_w_in), "w_uq": (w_uq, m_w_uq, v_w_uq), "w_ukv": (w_ukv, m_w_ukv, v_w_ukv),
              "w_out": (w_out, m_w_out, v_w_out), "w_up": (w_up, m_w_up, v_w_up), "conv_w": (conv_w, m_conv_w, v_conv_w),
              "w_down": (w_down, m_w_down, v_w_down)}
    grad, delta, new_m, new_v = {}, {}, {}, {}

    transposed = ("w_in", "w_uq", "w_up")

    def adamw_group(group):
        for n in group:
            flip = jnp.transpose if n in transposed else (lambda t: t)
            w_, m_, v_ = [flip(t[0]) for t in params[n]]
            res = _adamw_rs8(g8[n], r7[n], me_idx, w_, m_, v_, name="adamw_" + n)
            grad[n], delta[n], new_m[n], new_v[n] = [flip(r)[None] for r in res]

    adamw_group(("w_down", "w_out", "conv_w"))
    small_part = {"g_mix": dg_mix, "g_cq": dg_cq, "g_ckv": dg_ckv, "g_sb_out": dg_sb, "g_mla_out": dg_mla,
                  "g_ffn": dg_ffn, "conv_b": _ff_deinterleave(dcb), "g_final": dg_final}
    small_all, = _all_gather_async([_pack_small(small_part, tail=loss_row[0, 0:1])], name="ag_small_async",
                                   collective_id=5)
    adamw_group(("w_up",))
    adamw_group(late)
    gsmall = _sum8(small_all, name="sum_small_grads")
    small_w = {"g_mix": g_mix, "g_cq": g_cq, "g_ckv": g_ckv, "g_sb_out": g_sb_out, "g_mla_out": g_mla_out,
               "g_ffn": g_ffn, "conv_b": conv_b, "g_final": g_final}
    small_m = {"g_mix": m_g_mix, "g_cq": m_g_cq, "g_ckv": m_g_ckv, "g_sb_out": m_g_sb_out, "g_mla_out": m_g_mla_out,
               "g_ffn": m_g_ffn, "conv_b": m_conv_b, "g_final": m_g_final}
    small_v = {"g_mix": v_g_mix, "g_cq": v_g_cq, "g_ckv": v_g_ckv, "g_sb_out": v_g_sb_out, "g_mla_out": v_g_mla_out,
               "g_ffn": v_g_ffn, "conv_b": v_conv_b, "g_final": v_g_final}
    ds_, ms_, vs_ = _adamw(_pack_small(small_w), gsmall, _pack_small(small_m), _pack_small(small_v), name="adamw_small")
    for src, dst in ((_unpack_small(gsmall), grad), (_unpack_small(ds_), delta), (_unpack_small(ms_), new_m), (_unpack_small(vs_), new_v)):
        for n, _ in SMALL:
            dst[n] = src[n].reshape(small_w[n].shape)

    loss = gsmall.reshape(-1)[SMALL_USED]
    order = ("g_mix", "w_in", "g_cq", "w_uq", "g_ckv", "w_ukv", "g_sb_out", "g_mla_out", "w_out", "g_ffn", "w_up",
             "conv_w", "conv_b", "w_down", "g_final")
    return (loss, dx.reshape(B, S, D), *[grad[n] for n in order], *[delta[n] for n in order],
            *[new_m[n] for n in order], *[new_v[n] for n in order])
```

```python
import jax
import jax.numpy as jnp
from jax import lax
from jax.experimental import pallas as pl
from jax.experimental.pallas import tpu as pltpu
from jax.experimental.pallas import tpu_sc as plsc

F32 = jnp.float32
BF16 = jnp.bfloat16

D_MODEL = 1024
SB_HEADS = 8
SB_HEAD_DIM = 64
MLA_HEADS = 8
MLA_NOPE = 64
MLA_ROPE = 32
MLA_V = 64
Q_LORA = 384
KV_LORA = 256
D_FF = 2816
ROPE_BASE = 10000.0
EPS = 1e-6
SB_W = SB_HEADS * SB_HEAD_DIM
MLA_W = MLA_HEADS * MLA_V
MLA_QK = MLA_NOPE + MLA_ROPE

ADAM_LR = 0.001
ADAM_B1 = 0.9
ADAM_B2 = 0.999
ADAM_EPS = 1e-08
ADAM_WD = 0.01
ADAM_STEP = 10

N_DEV = 8
LANES = 128
V7X_VMEM_LIMIT = 56 * 1024 * 1024
FF_BLK = 256
N_FF_BLK = D_FF // FF_BLK

P_Q, P_K, P_V = 0, SB_W, 2 * SB_W
P_CKV = 3 * SB_W
P_KRT = P_CKV + KV_LORA
P_CQ = P_KRT + LANES
P_COLS = P_CQ + Q_LORA

MESH = pl.DeviceIdType.MESH
ANY = pl.BlockSpec(memory_space=pl.ANY)


def _cparams(sem=None, vmem=V7X_VMEM_LIMIT):
    return pltpu.CompilerParams(dimension_semantics=sem, vmem_limit_bytes=vmem)


def _matmul_tn(a, b, *, tm, tn, tk, name, out_dtype=F32):
    K, M = a.shape
    N = b.shape[1]
    assert M % tm == 0 and N % tn == 0 and K % tk == 0, (name, a.shape, b.shape)
    n_k = K // tk
    narrow = out_dtype != F32

    def body(a_ref, b_ref, o_ref, *scratch):
        acc_ref = scratch[0] if narrow else o_ref
        k = pl.program_id(2)
        part = lax.dot_general(a_ref[...].astype(BF16), b_ref[...].astype(BF16), (((0,), (0,)), ((), ())),
                               preferred_element_type=F32)

        @pl.when(k == 0)
        def _():
            acc_ref[...] = part

        @pl.when(k > 0)
        def _():
            acc_ref[...] += part

        if narrow:
            @pl.when(k == n_k - 1)
            def _():
                o_ref[...] = acc_ref[...].astype(out_dtype)

    return pl.pallas_call(
        body, name=name, grid=(M // tm, N // tn, n_k),
        in_specs=[pl.BlockSpec((tk, tm), lambda i, j, k: (k, i)), pl.BlockSpec((tk, tn), lambda i, j, k: (k, j))],
        out_specs=pl.BlockSpec((tm, tn), lambda i, j, k: (i, j)),
        out_shape=jax.ShapeDtypeStruct((M, N), out_dtype),
        scratch_shapes=[pltpu.VMEM((tm, tn), F32)] if narrow else [],
        compiler_params=_cparams(("parallel", "parallel", "arbitrary")),
    )(a, b)


def _rms(xf, g):
    r = lax.rsqrt(jnp.mean(xf * xf, axis=1, keepdims=True) + EPS)
    return (xf * r) * g


def _rms_grad(dyf, xf, g):
    r = lax.rsqrt(jnp.mean(xf * xf, axis=1, keepdims=True) + EPS)
    xh = xf * r
    dyg = dyf * g
    dx = r * (dyg - xh * jnp.mean(dyg * xh, axis=1, keepdims=True))
    return dx, jnp.sum(dyf * xh, axis=0, keepdims=True)


def _accumulate(ref, part):
    @pl.when(pl.program_id(0) == 0)
    def _():
        ref[...] = part

    @pl.when(pl.program_id(0) > 0)
    def _():
        ref[...] += part


def _rms_matmul_nn(x, g, w, *, tm, name, col_block=0, out_dtype=F32, w_transposed=False):
    T = x.shape[0]
    C, N = w.shape[::-1] if w_transposed else w.shape
    assert T % tm == 0, (name, x.shape)
    contract = (((1,), (1,)), ((), ())) if w_transposed else (((1,), (0,)), ((), ()))

    def body(x_ref, g_ref, w_ref, h_ref, o_ref):
        hb = _rms(x_ref[...], g_ref[...]).astype(BF16)
        h_ref[...] = hb
        o_ref[...] = lax.dot_general(hb, w_ref[...], contract, preferred_element_type=F32).astype(out_dtype)

    return pl.pallas_call(
        body, name=name, grid=(T // tm,),
        in_specs=[pl.BlockSpec((tm, C), lambda i: (i, col_block)), pl.BlockSpec((1, C), lambda i: (0, 0)),
                  pl.BlockSpec(w.shape, lambda i: (0, 0))],
        out_specs=[pl.BlockSpec((tm, C), lambda i: (i, 0)), pl.BlockSpec((tm, N), lambda i: (i, 0))],
        out_shape=[jax.ShapeDtypeStruct((T, C), BF16), jax.ShapeDtypeStruct((T, N), out_dtype)],
        compiler_params=_cparams(("parallel",)),
    )(x, g, w)


def _matmul_nt_rms_bwd(a, b, x, g, *, tm, name, residual=None, col_block=0, out_dtype=F32):
    M, K = a.shape
    C = b.shape[0]
    assert M % tm == 0, (name, a.shape)
    in_specs = [pl.BlockSpec((tm, K), lambda i: (i, 0)), pl.BlockSpec((C, K), lambda i: (0, 0)),
                pl.BlockSpec((tm, C), lambda i: (i, col_block)), pl.BlockSpec((1, C), lambda i: (0, 0))]
    args = [a, b, x, g]
    if residual is not None:
        in_specs.append(pl.BlockSpec((tm, C), lambda i: (i, 0)))
        args.append(residual)

    def body(*refs):
        a_ref, b_ref, x_ref, g_ref = refs[:4]
        dx_ref, dg_ref = refs[-2:]
        dy = lax.dot_general(a_ref[...].astype(BF16), b_ref[...], (((1,), (1,)), ((), ())), preferred_element_type=F32)
        dx, part = _rms_grad(dy, x_ref[...], g_ref[...])
        if residual is not None:
            dx = dx + refs[4][...]
        dx_ref[...] = dx.astype(out_dtype)
        _accumulate(dg_ref, part)

    return pl.pallas_call(
        body, name=name, grid=(M // tm,), in_specs=in_specs,
        out_specs=[pl.BlockSpec((tm, C), lambda i: (i, 0)), pl.BlockSpec((1, C), lambda i: (0, 0))],
        out_shape=[jax.ShapeDtypeStruct((M, C), out_dtype), jax.ShapeDtypeStruct((1, C), F32)],
        compiler_params=_cparams(("arbitrary",)),
    )(*args)


def _matmul_nn_loss(a, w, x1, g, tgt, *, tm, name):
    M, K = a.shape
    C = w.shape[1]
    assert M % tm == 0, (name, a.shape)

    nsub = 4
    ts = tm // nsub

    def body(a_ref, w_ref, x_ref, g_ref, t_ref, dx_ref, dg_ref, loss_ref):
        gf = g_ref[...]
        wv = w_ref[...]
        rows = [slice(r * ts, (r + 1) * ts) for r in range(nsub)]
        xs = [x_ref[rw, :] + jnp.dot(a_ref[rw, :], wv, preferred_element_type=F32) for rw in rows]
        lpart, gpart = 0.0, 0.0
        for rw, xf in zip(rows, xs):
            err = _rms(xf, gf) - t_ref[rw, :]
            lpart = lpart + 0.5 * jnp.sum(jnp.mean(err * err, axis=1, keepdims=True), axis=0, keepdims=True)
            dx, gp = _rms_grad(err * (1.0 / C), xf, gf)
            dx_ref[rw, :] = dx
            gpart = gpart + gp
        _accumulate(dg_ref, gpart)
        _accumulate(loss_ref, jnp.broadcast_to(lpart, (1, LANES)))

    row = pl.BlockSpec((tm, C), lambda i: (i, 0))
    return pl.pallas_call(
        body, name=name, grid=(M // tm,),
        in_specs=[pl.BlockSpec((tm, K), lambda i: (i, 0)), pl.BlockSpec((K, C), lambda i: (0, 0)), row,
                  pl.BlockSpec((1, C), lambda i: (0, 0)), row],
        out_specs=[row, pl.BlockSpec((1, C), lambda i: (0, 0)), pl.BlockSpec((1, LANES), lambda i: (0, 0))],
        out_shape=[jax.ShapeDtypeStruct((M, C), F32), jax.ShapeDtypeStruct((1, C), F32),
                   jax.ShapeDtypeStruct((1, LANES), F32)],
        compiler_params=_cparams(("arbitrary",)),
    )(a, w, x1, g, tgt)


ATT_T = 256
ATT_PAIRS = 2
NEG_BIG = -1e30


def _lane_iota():
    return lax.broadcasted_iota(jnp.int32, (1, LANES), 1)


def _head_masks():
    first = _lane_iota() < SB_HEAD_DIM
    return first, jnp.logical_not(first)


def _pick(mask, x):
    return jnp.where(mask, x, jnp.zeros_like(x))


def _lane_value(t, lane):
    return jnp.sum(jnp.where(_lane_iota() == lane, t, 0.0), axis=1, keepdims=True)


def _split_hi_lo(x):
    hi = x.astype(BF16)
    lo = (x - hi.astype(F32)).astype(BF16)
    return jnp.concatenate([hi, lo], axis=1)


def _tri(n, kind):
    r = lax.broadcasted_iota(jnp.int32, (n, n), 0)
    c = lax.broadcasted_iota(jnp.int32, (n, n), 1)
    u = {"suffix_excl": r > c, "prefix_incl": r <= c, "prefix_excl": r < c}[kind].astype(BF16)
    return jnp.concatenate([u, u], axis=0)


def _dot_nt(a, b):
    return lax.dot_general(a, b, (((1,), (1,)), ((), ())), preferred_element_type=F32)


def _dot_tn(a, b):
    return lax.dot_general(a, b, (((0,), (0,)), ((), ())), preferred_element_type=F32)


def _dot(a, b):
    return jnp.dot(a, b, preferred_element_type=F32)


def _causal_mask(n, strict):
    r = lax.broadcasted_iota(jnp.int32, (n, n), 0)
    c = lax.broadcasted_iota(jnp.int32, (n, n), 1)
    return (c < r) if strict else (c <= r)


LOG2E = 1.4426950408889634


def _sb_logs(qh, kj, vis):
    z2 = _dot_nt(qh, kj) * LOG2E
    nk = jnp.maximum(z2, 0.0) + jnp.log2(1.0 + jnp.exp2(-jnp.abs(z2)))
    lb = z2 - nk
    if vis is not None:
        nk = jnp.where(vis, nk, 0.0)
    return lb, nk


def _sb_fwd(p, *, seq, name):
    T = p.shape[0]
    B = T // seq
    TQ = ATT_T
    nq = seq // TQ
    PP = ATT_PAIRS
    W = PP * LANES
    nstep = SB_W // W
    NH = 2 * PP

    def body(q_ref, k_ref, v_ref, o_ref, lt_ref, q_s, k_s, v_s):
        masks = _head_masks()
        q = q_ref[...] * (SB_HEAD_DIM ** -0.5)
        v = v_ref[...]
        k_s[...] = k_ref[...].astype(BF16)
        for h in range(NH):
            ps = slice((h // 2) * LANES, (h // 2 + 1) * LANES)
            hs = slice(h * LANES, (h + 1) * LANES)
            q_s[:, hs] = _pick(masks[h % 2], q[:, ps]).astype(BF16)
            v_s[:, hs] = _pick(masks[h % 2], v[:, ps]).astype(BF16)
        u_suf = _tri(TQ, "suffix_excl")
        vis = _causal_mask(TQ, True)

        def q_block(i, carry):
            q0 = pl.multiple_of(i * TQ, TQ)
            qs = [q_s[pl.ds(q0, TQ), h * LANES:(h + 1) * LANES] for h in range(NH)]

            def tile(k0, c, mask):
                rs, accs = list(c[:NH]), list(c[NH:])
                logs = [_sb_logs(qs[h], k_s[pl.ds(k0, TQ), (h // 2) * LANES:(h // 2 + 1) * LANES], mask) for h in range(NH)]
                sums = [_dot(_split_hi_lo(nk), u_suf) for _, nk in logs]
                for h in range(NH):
                    a = jnp.exp2(logs[h][0] - sums[h] - rs[h])
                    if mask is not None:
                        a = jnp.where(mask, a, 0.0)
                    accs[h // 2] = accs[h // 2] + _dot(a.astype(BF16), v_s[pl.ds(k0, TQ), h * LANES:(h + 1) * LANES])
                    rs[h] = rs[h] + jnp.sum(logs[h][1], axis=1, keepdims=True)
                return tuple(rs) + tuple(accs)

            zero = jnp.zeros((TQ, 1), F32)
            c = tile(q0, (zero,) * NH + (jnp.zeros((TQ, LANES), F32),) * PP, vis)

            def k_block(jj, c):
                return tile(pl.multiple_of((i - 1 - jj) * TQ, TQ), c, None)

            c = lax.fori_loop(0, i, k_block, c)
            for pr in range(PP):
                ps = slice(pr * LANES, (pr + 1) * LANES)
                o_ref[pl.ds(q0, TQ), ps] = c[NH + pr]
                lt_ref[pl.ds(q0, TQ), ps] = jnp.where(masks[0], c[2 * pr], c[2 * pr + 1])
            return carry

        lax.fori_loop(0, nq, q_block, 0)

    blk = lambda off: pl.BlockSpec((seq, W), lambda b, g: (b, off + g))
    out_blk = pl.BlockSpec((seq, W), lambda b, g: (b, g))
    return pl.pallas_call(
        body, name=name, grid=(B, nstep),
        in_specs=[blk(P_Q // W), blk(P_K // W), blk(P_V // W)],
        out_specs=[out_blk, out_blk],
        out_shape=[jax.ShapeDtypeStruct((T, SB_W), F32), jax.ShapeDtypeStruct((T, SB_W), F32)],
        scratch_shapes=[pltpu.VMEM((seq, NH * LANES), BF16), pltpu.VMEM((seq, W), BF16), pltpu.VMEM((seq, NH * LANES), BF16)],
        compiler_params=_cparams(("parallel", "parallel")),
    )(p, p, p)


def _sb_bwd(p, ltot, do, *, seq, name):
    T = p.shape[0]
    B = T // seq
    TQ = ATT_T
    nq = seq // TQ
    PP = ATT_PAIRS
    W = PP * LANES
    nstep = SB_W // W
    NH = 2 * PP
    scale = SB_HEAD_DIM ** -0.5

    def body(q_ref, k_ref, v_ref, lt_ref, do_ref, dq_ref, dk_ref, dv_ref, q_s, k_s, v_s, do_s, dk_s, dv_s):
        masks = _head_masks()
        q = q_ref[...] * scale
        dof = do_ref[...]
        k_s[...] = k_ref[...].astype(BF16)
        v_s[...] = v_ref[...].astype(BF16)
        for h in range(NH):
            ps = slice((h // 2) * LANES, (h // 2 + 1) * LANES)
            hs = slice(h * LANES, (h + 1) * LANES)
            q_s[:, hs] = _pick(masks[h % 2], q[:, ps]).astype(BF16)
            do_s[:, hs] = _pick(masks[h % 2], dof[:, ps]).astype(BF16)
        dk_s[...] = jnp.zeros_like(dk_s)
        dv_s[...] = jnp.zeros_like(dv_s)
        u_pin = _tri(TQ, "prefix_incl")
        u_pex = _tri(TQ, "prefix_excl")[:TQ]
        vis = _causal_mask(TQ, True)

        def q_block(i, carry):
            q0 = pl.multiple_of(i * TQ, TQ)
            qs = [q_s[pl.ds(q0, TQ), h * LANES:(h + 1) * LANES] for h in range(NH)]
            dos = [do_s[pl.ds(q0, TQ), h * LANES:(h + 1) * LANES] for h in range(NH)]
            lt = lt_ref[pl.ds(q0, TQ), :]
            lts = [_lane_value(lt[:, (h // 2) * LANES:(h // 2 + 1) * LANES], (h % 2) * SB_HEAD_DIM) for h in range(NH)]

            def tile(k0, c, mask):
                cs, gs, accs = list(c[:NH]), list(c[NH:2 * NH]), list(c[2 * NH:])
                kjs = [k_s[pl.ds(k0, TQ), pr * LANES:(pr + 1) * LANES] for pr in range(PP)]
                vjs = [v_s[pl.ds(k0, TQ), pr * LANES:(pr + 1) * LANES] for pr in range(PP)]
                logs = [_sb_logs(qs[h], kjs[h // 2], mask) for h in range(NH)]
                pins = [_dot(_split_hi_lo(nk), u_pin) for _, nk in logs]
                das = [_dot_nt(dos[h], vjs[h // 2]) for h in range(NH)]
                a_l, g_l = [], []
                for h in range(NH):
                    a = jnp.exp2(logs[h][0] - ((lts[h] - cs[h]) - pins[h]))
                    if mask is not None:
                        a = jnp.where(mask, a, 0.0)
                    a_l.append(a)
                    g_l.append(das[h] * a)
                pres = [_dot(g.astype(BF16), u_pex) for g in g_l]
                dz_l = []
                for h in range(NH):
                    dz = g_l[h] - jnp.exp2(logs[h][0]) * (g_l[h] + (pres[h] + gs[h]))
                    if mask is not None:
                        dz = jnp.where(mask, dz, 0.0)
                    dz_l.append(dz.astype(BF16))
                for h in range(NH):
                    accs[h] = accs[h] + _dot(dz_l[h], kjs[h // 2])
                for pr in range(PP):
                    ps = slice(pr * LANES, (pr + 1) * LANES)
                    ha, hb = 2 * pr, 2 * pr + 1
                    dk_s[ps, pl.ds(k0, TQ)] += _dot_tn(qs[ha], dz_l[ha]) + _dot_tn(qs[hb], dz_l[hb])
                    dv_s[ps, pl.ds(k0, TQ)] += _dot_tn(dos[ha], a_l[ha].astype(BF16)) + _dot_tn(dos[hb], a_l[hb].astype(BF16))
                for h in range(NH):
                    cs[h] = cs[h] + jnp.sum(logs[h][1], axis=1, keepdims=True)
                    gs[h] = gs[h] + jnp.sum(g_l[h], axis=1, keepdims=True)
                return tuple(cs) + tuple(gs) + tuple(accs)

            z1 = jnp.zeros((TQ, 1), F32)
            zl = jnp.zeros((TQ, LANES), F32)

            def k_block(j, c):
                return tile(pl.multiple_of(j * TQ, TQ), c, None)

            c = lax.fori_loop(0, i, k_block, (z1,) * (2 * NH) + (zl,) * NH)
            c = tile(q0, c, vis)
            for pr in range(PP):
                dq = jnp.where(masks[0], c[2 * NH + 2 * pr], c[2 * NH + 2 * pr + 1]) * scale
                dq_ref[pl.ds(q0, TQ), pr * LANES:(pr + 1) * LANES] = dq.astype(BF16)
            return carry

        lax.fori_loop(0, nq, q_block, 0)
        dk_ref[...] = dk_s[...].T.astype(BF16)
        dv_ref[...] = dv_s[...].T.astype(BF16)

    blk = lambda off: pl.BlockSpec((seq, W), lambda b, g: (b, off + g))
    out_blk = pl.BlockSpec((seq, W), lambda b, g: (b, g))
    return pl.pallas_call(
        body, name=name, grid=(B, nstep),
        in_specs=[blk(P_Q // W), blk(P_K // W), blk(P_V // W), out_blk, out_blk],
        out_specs=[out_blk, out_blk, out_blk],
        out_shape=[jax.ShapeDtypeStruct((T, SB_W), BF16) for _ in range(3)],
        scratch_shapes=[pltpu.VMEM((seq, NH * LANES), BF16), pltpu.VMEM((seq, W), BF16), pltpu.VMEM((seq, W), BF16),
                        pltpu.VMEM((seq, NH * LANES), BF16), pltpu.VMEM((W, seq), F32), pltpu.VMEM((W, seq), F32)],
        compiler_params=_cparams(("parallel", "parallel")),
    )(p, p, p, ltot, do)


def _mla_masks():
    lane = lax.broadcasted_iota(jnp.int32, (1, 2 * LANES), 1)
    ma = (lane < MLA_NOPE) | ((lane >= LANES) & (lane < LANES + MLA_ROPE))
    mb = ((lane >= MLA_NOPE) & (lane < LANES)) | ((lane >= LANES + MLA_ROPE) & (lane < LANES + 2 * MLA_ROPE))
    return ma, mb


def _mla_fwd(qm, kvm, krt, *, seq, name):
    T = qm.shape[0]
    B = T // seq
    TQ = ATT_T
    nq = seq // TQ
    PP = ATT_PAIRS
    W = PP * LANES
    nstep = MLA_W // W
    NH = 2 * PP
    CW = 2 * LANES
    scale = MLA_QK ** -0.5

    def body(qn_ref, qr_ref, kn_ref, v_ref, kr_ref, o_ref, lse_ref, q_s, kc_s, v_s):
        hm = _head_masks()
        mm = _mla_masks()
        v = v_ref[...]
        for pr in range(PP):
            ps = slice(pr * LANES, (pr + 1) * LANES)
            qc = jnp.concatenate([qn_ref[:, ps], qr_ref[:, ps]], axis=1)
            kc_s[:, pr * CW:(pr + 1) * CW] = jnp.concatenate([kn_ref[:, ps], kr_ref[...]], axis=1)
            for e in range(2):
                h = 2 * pr + e
                q_s[:, h * CW:(h + 1) * CW] = _pick(mm[e], qc)
                v_s[:, h * LANES:(h + 1) * LANES] = _pick(hm[e], v[:, ps])
        vis = _causal_mask(TQ, False)

        def q_block(i, carry):
            q0 = pl.multiple_of(i * TQ, TQ)
            qs = [q_s[pl.ds(q0, TQ), h * CW:(h + 1) * CW] for h in range(NH)]

            def tile(k0, c, mask):
                ms, ls, accs = list(c[:NH]), list(c[NH:2 * NH]), list(c[2 * NH:])
                ss = [_dot_nt(qs[h], kc_s[pl.ds(k0, TQ), (h // 2) * CW:(h // 2 + 1) * CW]) * scale for h in range(NH)]
                if mask is not None:
                    ss = [jnp.where(mask, s, NEG_BIG) for s in ss]
                m_new = [jnp.maximum(ms[h], jnp.max(ss[h], axis=1, keepdims=True)) for h in range(NH)]
                alphas = [jnp.exp(ms[h] - m_new[h]) for h in range(NH)]
                prs = [jnp.exp(ss[h] - m_new[h]) for h in range(NH)]
                outs = [_dot(prs[h].astype(BF16), v_s[pl.ds(k0, TQ), h * LANES:(h + 1) * LANES]) for h in range(NH)]
                ls = [alphas[h] * ls[h] + jnp.sum(prs[h], axis=1, keepdims=True) for h in range(NH)]
                for pr in range(PP):
                    accs[pr] = accs[pr] * jnp.where(hm[0], alphas[2 * pr], alphas[2 * pr + 1]) + outs[2 * pr] + outs[2 * pr + 1]
                return tuple(m_new) + tuple(ls) + tuple(accs)

            neg = jnp.full((TQ, 1), NEG_BIG, F32)
            z1 = jnp.zeros((TQ, 1), F32)

            def k_block(j, c):
                return tile(pl.multiple_of(j * TQ, TQ), c, None)

            c = lax.fori_loop(0, i, k_block, (neg,) * NH + (z1,) * NH + (jnp.zeros((TQ, LANES), F32),) * PP)
            c = tile(q0, c, vis)
            for pr in range(PP):
                ps = slice(pr * LANES, (pr + 1) * LANES)
                m_a, m_b, l_a, l_b = c[2 * pr], c[2 * pr + 1], c[NH + 2 * pr], c[NH + 2 * pr + 1]
                o_ref[pl.ds(q0, TQ), ps] = c[2 * NH + pr] / jnp.where(hm[0], l_a, l_b)
                lse_ref[pl.ds(q0, TQ), ps] = jnp.where(hm[0], m_a + jnp.log(l_a), m_b + jnp.log(l_b))
            return carry

        lax.fori_loop(0, nq, q_block, 0)

    blk = lambda off: pl.BlockSpec((seq, W), lambda b, g: (b, off + g))
    out_blk = pl.BlockSpec((seq, W), lambda b, g: (b, g))
    return pl.pallas_call(
        body, name=name, grid=(B, nstep),
        in_specs=[blk(0), blk(nstep), blk(0), blk(nstep), pl.BlockSpec((seq, LANES), lambda b, g: (b, 0))],
        out_specs=[out_blk, out_blk],
        out_shape=[jax.ShapeDtypeStruct((T, MLA_W), F32), jax.ShapeDtypeStruct((T, MLA_W), F32)],
        scratch_shapes=[pltpu.VMEM((seq, NH * CW), BF16), pltpu.VMEM((seq, PP * CW), BF16), pltpu.VMEM((seq, NH * LANES), BF16)],
        compiler_params=_cparams(("parallel", "parallel")),
    )(qm, qm, kvm, kvm, krt)


def _mla_bwd(qm, kvm, krt, o, lse, do, *, seq, name):
    T = qm.shape[0]
    B = T // seq
    TQ = ATT_T
    nq = seq // TQ
    PP = ATT_PAIRS
    W = PP * LANES
    nstep = MLA_W // W
    NH = 2 * PP
    CW = 2 * LANES
    scale = MLA_QK ** -0.5

    def body(qn_ref, qr_ref, kn_ref, v_ref, kr_ref, o_ref, lse_ref, do_ref,
             dqn_ref, dqr_ref, dkn_ref, dv_ref, dkr_ref, q_s, kc_s, do_s, dkc_s, dv_s):
        hm = _head_masks()
        mm = _mla_masks()
        dof = do_ref[...]
        for pr in range(PP):
            ps = slice(pr * LANES, (pr + 1) * LANES)
            qc = jnp.concatenate([qn_ref[:, ps], qr_ref[:, ps]], axis=1)
            kc_s[:, pr * CW:(pr + 1) * CW] = jnp.concatenate([kn_ref[:, ps], kr_ref[...]], axis=1)
            for e in range(2):
                h = 2 * pr + e
                q_s[:, h * CW:(h + 1) * CW] = _pick(mm[e], qc)
                do_s[:, h * LANES:(h + 1) * LANES] = _pick(hm[e], dof[:, ps]).astype(BF16)
        dkc_s[...] = jnp.zeros_like(dkc_s)
        dv_s[...] = jnp.zeros_like(dv_s)
        vis = _causal_mask(TQ, False)

        def q_block(i, carry):
            q0 = pl.multiple_of(i * TQ, TQ)
            qs = [q_s[pl.ds(q0, TQ), h * CW:(h + 1) * CW] for h in range(NH)]
            dos = [do_s[pl.ds(q0, TQ), h * LANES:(h + 1) * LANES] for h in range(NH)]
            lse_t = lse_ref[pl.ds(q0, TQ), :]
            dd = do_ref[pl.ds(q0, TQ), :] * o_ref[pl.ds(q0, TQ), :]
            lses, ds_ = [], []
            for h in range(NH):
                ps = slice((h // 2) * LANES, (h // 2 + 1) * LANES)
                lses.append(_lane_value(lse_t[:, ps], (h % 2) * MLA_V))
                ds_.append(jnp.sum(_pick(hm[h % 2], dd[:, ps]), axis=1, keepdims=True))

            def tile(k0, c, mask):
                accs = list(c)
                kcs = [kc_s[pl.ds(k0, TQ), pr * CW:(pr + 1) * CW] for pr in range(PP)]
                vjs = [v_ref[pl.ds(k0, TQ), pr * LANES:(pr + 1) * LANES] for pr in range(PP)]
                ss = [_dot_nt(qs[h], kcs[h // 2]) * scale for h in range(NH)]
                dps = [_dot_nt(dos[h], vjs[h // 2]) for h in range(NH)]
                p_l, ds_l = [], []
                for h in range(NH):
                    pr_ = jnp.exp(ss[h] - lses[h])
                    if mask is not None:
                        pr_ = jnp.where(mask, pr_, 0.0)
                    p_l.append(pr_.astype(BF16))
                    ds_l.append((pr_ * (dps[h] - ds_[h]) * scale).astype(BF16))
                for h in range(NH):
                    accs[h] = accs[h] + _dot(ds_l[h], kcs[h // 2])
                for pr in range(PP):
                    ha, hb = 2 * pr, 2 * pr + 1
                    dkc_s[pl.ds(k0, TQ), pr * CW:(pr + 1) * CW] += _dot_tn(ds_l[ha], qs[ha]) + _dot_tn(ds_l[hb], qs[hb])
                    dv_s[pr * LANES:(pr + 1) * LANES, pl.ds(k0, TQ)] += _dot_tn(dos[ha], p_l[ha]) + _dot_tn(dos[hb], p_l[hb])
                return tuple(accs)

            zc = jnp.zeros((TQ, CW), F32)

            def k_block(j, c):
                return tile(pl.multiple_of(j * TQ, TQ), c, None)

            c = lax.fori_loop(0, i, k_block, (zc,) * NH)
            c = tile(q0, c, vis)
            for pr in range(PP):
                ps = slice(pr * LANES, (pr + 1) * LANES)
                dq = _pick(mm[0], c[2 * pr]) + _pick(mm[1], c[2 * pr + 1])
                dqn_ref[pl.ds(q0, TQ), ps] = dq[:, :LANES].astype(BF16)
                dqr_ref[pl.ds(q0, TQ), ps] = dq[:, LANES:]
            return carry

        lax.fori_loop(0, nq, q_block, 0)
        dkr = dkc_s[:, LANES:CW]
        for pr in range(PP):
            dkn_ref[:, pr * LANES:(pr + 1) * LANES] = dkc_s[:, pr * CW:pr * CW + LANES].astype(BF16)
            if pr > 0:
                dkr = dkr + dkc_s[:, pr * CW + LANES:(pr + 1) * CW]
        dv_ref[...] = dv_s[...].T.astype(BF16)
        g = pl.program_id(1)

        @pl.when(g == 0)
        def _():
            dkr_ref[...] = dkr

        @pl.when(g > 0)
        def _():
            dkr_ref[...] += dkr

    blk = lambda off: pl.BlockSpec((seq, W), lambda b, g: (b, off + g))
    out_blk = pl.BlockSpec((seq, W), lambda b, g: (b, g))
    one_blk = pl.BlockSpec((seq, LANES), lambda b, g: (b, 0))
    return pl.pallas_call(
        body, name=name, grid=(B, nstep),
        in_specs=[blk(0), blk(nstep), blk(0), blk(nstep), one_blk, out_blk, out_blk, out_blk],
        out_specs=[out_blk, out_blk, out_blk, out_blk, one_blk],
        out_shape=[jax.ShapeDtypeStruct((T, MLA_W), BF16), jax.ShapeDtypeStruct((T, MLA_W), F32),
                   jax.ShapeDtypeStruct((T, MLA_W), BF16), jax.ShapeDtypeStruct((T, MLA_W), BF16),
                   jax.ShapeDtypeStruct((T, LANES), F32)],
        scratch_shapes=[pltpu.VMEM((seq, NH * CW), BF16), pltpu.VMEM((seq, PP * CW), BF16), pltpu.VMEM((seq, NH * LANES), BF16),
                        pltpu.VMEM((seq, PP * CW), F32), pltpu.VMEM((W, seq), F32)],
        compiler_params=_cparams(("parallel", "arbitrary")),
    )(qm, qm, kvm, kvm, krt, o, lse, do)


def _rope_tables(pos_ref, invf_ref):
    ang = pos_ref[...].astype(F32) * invf_ref[...]
    first = (_lane_iota() % MLA_ROPE) < (MLA_ROPE // 2)
    return jnp.cos(ang), jnp.sin(ang), first


def _rope_apply(x, cos, sin, first):
    rot = jnp.where(first, -pltpu.roll(x, LANES - MLA_ROPE // 2, 1), pltpu.roll(x, MLA_ROPE // 2, 1))
    return x * cos + rot * sin


def _rope_apply_t(dy, cos, sin, first):
    dys = dy * sin
    rot_t = jnp.where(first, pltpu.roll(dys, LANES - MLA_ROPE // 2, 1), -pltpu.roll(dys, MLA_ROPE // 2, 1))
    return dy * cos + rot_t


def _proj_uq_rope(p, g, wuq, pos, invf, *, tm, name):
    T = p.shape[0]
    ntile = MLA_W // LANES

    def body(x_ref, kr_ref, g_ref, w_ref, pos_ref, invf_ref, cq_ref, qm_ref, krt_ref):
        cos, sin, first = _rope_tables(pos_ref, invf_ref)
        hb = _rms(x_ref[...], g_ref[...]).astype(BF16)
        cq_ref[...] = hb
        q = jnp.dot(hb, w_ref[...], preferred_element_type=F32)
        qm_ref[:, :MLA_W] = q[:, :MLA_W].astype(BF16)
        for t in range(ntile):
            sl = slice(MLA_W + t * LANES, MLA_W + (t + 1) * LANES)
            qm_ref[:, sl] = _rope_apply(q[:, sl], cos, sin, first).astype(BF16)
        krt_ref[...] = _rope_apply(kr_ref[...], cos, sin, first).astype(BF16)

    return pl.pallas_call(
        body, name=name, grid=(T // tm,),
        in_specs=[pl.BlockSpec((tm, Q_LORA), lambda i: (i, P_CQ // Q_LORA)), pl.BlockSpec((tm, LANES), lambda i: (i, P_KRT // LANES)),
                  pl.BlockSpec((1, Q_LORA), lambda i: (0, 0)), pl.BlockSpec((Q_LORA, 2 * MLA_W), lambda i: (0, 0)),
                  pl.BlockSpec((tm, 1), lambda i: (i, 0)), pl.BlockSpec((1, LANES), lambda i: (0, 0))],
        out_specs=[pl.BlockSpec((tm, Q_LORA), lambda i: (i, 0)), pl.BlockSpec((tm, 2 * MLA_W), lambda i: (i, 0)),
                   pl.BlockSpec((tm, LANES), lambda i: (i, 0))],
        out_shape=[jax.ShapeDtypeStruct((T, Q_LORA), BF16), jax.ShapeDtypeStruct((T, 2 * MLA_W), BF16),
                   jax.ShapeDtypeStruct((T, LANES), BF16)],
        compiler_params=_cparams(("parallel",)),
    )(p, p, g, wuq, pos, invf)


def _d_proj_uq_rope(dqn, dqr, dkr, wuq, cq, p, g, pos, invf, *, tm, name):
    T = dqn.shape[0]
    ntile = MLA_W // LANES

    def body(dqn_ref, dqr_ref, dkr_ref, w_ref, cq_ref, x_ref, g_ref, pos_ref, invf_ref,
             dw_ref, dx_ref, dg_ref, dkr_o_ref, dqm_s):
        cos, sin, first = _rope_tables(pos_ref, invf_ref)
        dqm_s[:, :MLA_W] = dqn_ref[...]
        for t in range(ntile):
            sl = slice(t * LANES, (t + 1) * LANES)
            dqm_s[:, MLA_W + t * LANES:MLA_W + (t + 1) * LANES] = _rope_apply_t(dqr_ref[:, sl], cos, sin, first).astype(BF16)
        dkr_o_ref[...] = _rope_apply_t(dkr_ref[...], cos, sin, first).astype(BF16)
        dqm = dqm_s[...]
        dy = lax.dot_general(dqm, w_ref[...], (((1,), (1,)), ((), ())), preferred_element_type=F32)
        dx, part = _rms_grad(dy, x_ref[...], g_ref[...])
        dx_ref[...] = dx.astype(BF16)
        _accumulate(dg_ref, part)
        _accumulate(dw_ref, _dot_tn(dqm, cq_ref[...]))

    half = pl.BlockSpec((tm, MLA_W), lambda i: (i, 0))
    tile = pl.BlockSpec((tm, LANES), lambda i: (i, 0))
    lat = pl.BlockSpec((tm, Q_LORA), lambda i: (i, 0))
    return pl.pallas_call(
        body, name=name, grid=(T // tm,),
        in_specs=[half, half, tile, pl.BlockSpec((Q_LORA, 2 * MLA_W), lambda i: (0, 0)), lat,
                  pl.BlockSpec((tm, Q_LORA), lambda i: (i, P_CQ // Q_LORA)), pl.BlockSpec((1, Q_LORA), lambda i: (0, 0)),
                  pl.BlockSpec((tm, 1), lambda i: (i, 0)), pl.BlockSpec((1, LANES), lambda i: (0, 0))],
        out_specs=[pl.BlockSpec((2 * MLA_W, Q_LORA), lambda i: (0, 0)), lat,
                   pl.BlockSpec((1, Q_LORA), lambda i: (0, 0)), tile],
        out_shape=[jax.ShapeDtypeStruct((2 * MLA_W, Q_LORA), F32), jax.ShapeDtypeStruct((T, Q_LORA), BF16),
                   jax.ShapeDtypeStruct((1, Q_LORA), F32), jax.ShapeDtypeStruct((T, LANES), BF16)],
        scratch_shapes=[pltpu.VMEM((tm, 2 * MLA_W), BF16)],
        compiler_params=_cparams(("arbitrary",)),
    )(dqn, dqr, dkr, wuq, cq, p, g, pos, invf)


def _d_proj_cat(pieces, b, x, g, *, tm, name, residual=None, col_block=0, out_dtype=F32, b_transposed=False,
                wgrad_act=None):
    M = pieces[0].shape[0]
    widths = [pc.shape[1] for pc in pieces]
    K = sum(widths)
    C = b.shape[1] if b_transposed else b.shape[0]
    n = len(pieces)
    fuse_w = wgrad_act is not None
    contract = (((1,), (0,)), ((), ())) if b_transposed else (((1,), (1,)), ((), ()))
    in_specs = [pl.BlockSpec((tm, w), lambda i: (i, 0)) for w in widths]
    in_specs += [pl.BlockSpec(b.shape, lambda i: (0, 0)), pl.BlockSpec((tm, C), lambda i: (i, col_block)),
                 pl.BlockSpec((1, C), lambda i: (0, 0))]
    args = list(pieces) + [b, x, g]
    if residual is not None:
        in_specs.append(pl.BlockSpec((tm, C), lambda i: (i, 0)))
        args.append(residual)
    if fuse_w:
        in_specs.append(pl.BlockSpec((tm, C), lambda i: (i, 0)))
        args.append(wgrad_act)

    def body(*refs):
        b_ref, x_ref, g_ref = refs[n:n + 3]
        first_ref, dx_ref, dg_ref = refs[-4:-1] if fuse_w else refs[-3:]
        cat_ref = refs[-1] if fuse_w else first_ref
        off = 0
        for r, w in zip(refs[:n], widths):
            cat_ref[:, off:off + w] = r[...]
            off += w
        cat = cat_ref[...]
        dy = lax.dot_general(cat, b_ref[...], contract, preferred_element_type=F32)
        dx, part = _rms_grad(dy, x_ref[...], g_ref[...])
        if residual is not None:
            dx = dx + refs[n + 3][...]
        dx_ref[...] = dx.astype(out_dtype)
        _accumulate(dg_ref, part)
        if fuse_w:
            act_ref = refs[n + 3 + (residual is not None)]
            _accumulate(first_ref, _dot_tn(act_ref[...], cat))

    first_spec = pl.BlockSpec((C, K), lambda i: (0, 0)) if fuse_w else pl.BlockSpec((tm, K), lambda i: (i, 0))
    first_shape = jax.ShapeDtypeStruct((C, K), F32) if fuse_w else jax.ShapeDtypeStruct((M, K), BF16)
    return pl.pallas_call(
        body, name=name, grid=(M // tm,), in_specs=in_specs,
        out_specs=[first_spec, pl.BlockSpec((tm, C), lambda i: (i, 0)), pl.BlockSpec((1, C), lambda i: (0, 0))],
        out_shape=[first_shape, jax.ShapeDtypeStruct((M, C), out_dtype), jax.ShapeDtypeStruct((1, C), F32)],
        scratch_shapes=[pltpu.VMEM((tm, K), BF16)] if fuse_w else [],
        compiler_params=_cparams(("arbitrary",)),
    )(*args)


def _heads_out(xa, xb, ga, gb, w, resid, *, tm, name):
    T, C = xa.shape
    N = w.shape[1]

    def body(xa_ref, xb_ref, ga_ref, gb_ref, w_ref, r_ref, oc_ref, o_ref):
        oc_ref[:, :C] = _rms(xa_ref[...], ga_ref[...]).astype(BF16)
        oc_ref[:, C:] = _rms(xb_ref[...], gb_ref[...]).astype(BF16)
        o_ref[...] = r_ref[...] + jnp.dot(oc_ref[...], w_ref[...], preferred_element_type=F32)

    row = pl.BlockSpec((tm, C), lambda i: (i, 0))
    gsp = pl.BlockSpec((1, C), lambda i: (0, 0))
    full = pl.BlockSpec((tm, N), lambda i: (i, 0))
    return pl.pallas_call(
        body, name=name, grid=(T // tm,),
        in_specs=[row, row, gsp, gsp, pl.BlockSpec((2 * C, N), lambda i: (0, 0)), full],
        out_specs=[pl.BlockSpec((tm, 2 * C), lambda i: (i, 0)), full],
        out_shape=[jax.ShapeDtypeStruct((T, 2 * C), BF16), jax.ShapeDtypeStruct((T, N), F32)],
        compiler_params=_cparams(("parallel",)),
    )(xa, xb, ga, gb, w, resid)


def _heads_out_bwd(dout, w, ocat, xa, xb, ga, gb, *, tm, name):
    T, C = xa.shape
    N = w.shape[1]
    n_steps = T // tm

    def body(d_ref, w_ref, oc_ref, xa_ref, xb_ref, ga_ref, gb_ref, dxa_ref, dxb_ref, dga_ref, dgb_ref, dw_ref, dw_s):
        db = d_ref[...].astype(BF16)
        dy = lax.dot_general(db, w_ref[...], (((1,), (1,)), ((), ())), preferred_element_type=F32)
        dxa, pa = _rms_grad(dy[:, :C], xa_ref[...], ga_ref[...])
        dxb, pb = _rms_grad(dy[:, C:], xb_ref[...], gb_ref[...])
        dxa_ref[...] = dxa
        dxb_ref[...] = dxb
        _accumulate(dga_ref, pa)
        _accumulate(dgb_ref, pb)
        _accumulate(dw_s, _dot_tn(oc_ref[...], db))

        @pl.when(pl.program_id(0) == n_steps - 1)
        def _():
            dw_ref[...] = dw_s[...].astype(BF16)

    row = pl.BlockSpec((tm, C), lambda i: (i, 0))
    gsp = pl.BlockSpec((1, C), lambda i: (0, 0))
    wsp = pl.BlockSpec((2 * C, N), lambda i: (0, 0))
    return pl.pallas_call(
        body, name=name, grid=(n_steps,),
        in_specs=[pl.BlockSpec((tm, N), lambda i: (i, 0)), wsp, pl.BlockSpec((tm, 2 * C), lambda i: (i, 0)), row, row, gsp, gsp],
        out_specs=[row, row, gsp, gsp, wsp],
        out_shape=[jax.ShapeDtypeStruct((T, C), F32), jax.ShapeDtypeStruct((T, C), F32),
                   jax.ShapeDtypeStruct((1, C), F32), jax.ShapeDtypeStruct((1, C), F32),
                   jax.ShapeDtypeStruct((2 * C, N), BF16)],
        scratch_shapes=[pltpu.VMEM((2 * C, N), F32)],
        compiler_params=_cparams(("arbitrary",)),
    )(dout, w, ocat, xa, xb, ga, gb)


CONV_ROWS = 256
HALO = 8


def _conv_taps(w_ref):
    return w_ref[0:1, :], w_ref[1:2, :], w_ref[2:3, :]


def _conv_rows(cur, prev, w, bias):
    ext = jnp.concatenate([prev, cur], axis=0)
    u1 = pltpu.roll(ext, 1, 0)[HALO:]
    u2 = pltpu.roll(ext, 2, 0)[HALO:]
    return w[2] * cur + w[1] * u1 + w[0] * u2 + bias, u1, u2


def _conv_fwd(u, w, bias, *, seq, name):
    T = u.shape[0]
    B = T // seq
    W2 = 2 * FF_BLK

    def body(u_ref, w_ref, b_ref, a_ref):
        wv = _conv_taps(w_ref)
        bv = b_ref[...]
        for c in range(seq // CONV_ROWS):
            r0 = c * CONV_ROWS
            cur = u_ref[r0:r0 + CONV_ROWS, :]
            prev = u_ref[r0 - HALO:r0, :] if c > 0 else jnp.zeros((HALO, W2), F32)
            y, _, _ = _conv_rows(cur, prev, wv, bv)
            gc = y[:, :FF_BLK]
            a_ref[r0:r0 + CONV_ROWS, :] = (gc * (1.0 / (1.0 + jnp.exp(-gc))) * y[:, FF_BLK:]).astype(BF16)

    return pl.pallas_call(
        body, name=name, grid=(B, N_FF_BLK),
        in_specs=[pl.BlockSpec((seq, W2), lambda b, j: (b, j)), pl.BlockSpec((3, W2), lambda b, j: (0, j)),
                  pl.BlockSpec((1, W2), lambda b, j: (0, j))],
        out_specs=pl.BlockSpec((seq, FF_BLK), lambda b, j: (b, j)),
        out_shape=jax.ShapeDtypeStruct((T, D_FF), BF16),
        compiler_params=_cparams(("parallel", "parallel")),
    )(u, w, bias)


def _conv_bwd(u, dx2, wdn, w, bias, *, seq, name):
    T = u.shape[0]
    B = T // seq
    D = dx2.shape[1]
    W2 = 2 * FF_BLK
    nchunk = seq // CONV_ROWS

    def body(u_ref, dx_ref, wd_ref, w_ref, b_ref, du_ref, dw_ref, db_ref, dwd_ref, duc_s, dwd_s):
        wv = _conv_taps(w_ref)
        bv = b_ref[...]
        wd = wd_ref[...]
        zrow = jnp.zeros((1, W2), F32)
        dw0, dw1, dw2, dbs = zrow, zrow, zrow, zrow
        dwd = jnp.zeros((FF_BLK, D), F32)
        for c in range(nchunk):
            r0 = c * CONV_ROWS
            cur = u_ref[r0:r0 + CONV_ROWS, :]
            prev = u_ref[r0 - HALO:r0, :] if c > 0 else jnp.zeros((HALO, W2), F32)
            y, u1, u2 = _conv_rows(cur, prev, wv, bv)
            gc = y[:, :FF_BLK]
            vc = y[:, FF_BLK:]
            sg = 1.0 / (1.0 + jnp.exp(-gc))
            dxc = dx_ref[r0:r0 + CONV_ROWS, :].astype(BF16)
            dav = _dot_nt(dxc, wd)
            silu = gc * sg
            dwd = dwd + _dot_tn((silu * vc).astype(BF16), dxc)
            duc = jnp.concatenate([dav * vc * (sg * (1.0 + gc * (1.0 - sg))), dav * silu], axis=1)
            duc_s[r0:r0 + CONV_ROWS, :] = duc
            dw0 = dw0 + jnp.sum(duc * u2, axis=0, keepdims=True)
            dw1 = dw1 + jnp.sum(duc * u1, axis=0, keepdims=True)
            dw2 = dw2 + jnp.sum(duc * cur, axis=0, keepdims=True)
            dbs = dbs + jnp.sum(duc, axis=0, keepdims=True)
        duc_s[seq:seq + HALO, :] = jnp.zeros((HALO, W2), F32)
        n_ext = CONV_ROWS + HALO
        for c in range(nchunk):
            r0 = c * CONV_ROWS
            ext = duc_s[r0:r0 + n_ext, :]
            s1 = pltpu.roll(ext, n_ext - 1, 0)[:CONV_ROWS]
            s2 = pltpu.roll(ext, n_ext - 2, 0)[:CONV_ROWS]
            du_ref[r0:r0 + CONV_ROWS, :] = (wv[2] * ext[:CONV_ROWS] + wv[1] * s1 + wv[0] * s2).astype(BF16)

        b = pl.program_id(1)

        @pl.when(b == 0)
        def _():
            dw_ref[0:1, :] = dw0
            dw_ref[1:2, :] = dw1
            dw_ref[2:3, :] = dw2
            db_ref[...] = dbs
            dwd_s[...] = dwd

        @pl.when(b > 0)
        def _():
            dw_ref[0:1, :] += dw0
            dw_ref[1:2, :] += dw1
            dw_ref[2:3, :] += dw2
            db_ref[...] += dbs
            dwd_s[...] += dwd

        @pl.when(b == B - 1)
        def _():
            dwd_ref[...] = dwd_s[...].astype(BF16)

    return pl.pallas_call(
        body, name=name, grid=(N_FF_BLK, B),
        in_specs=[pl.BlockSpec((seq, W2), lambda j, b: (b, j)), pl.BlockSpec((seq, D), lambda j, b: (b, 0)),
                  pl.BlockSpec((FF_BLK, D), lambda j, b: (j, 0)),
                  pl.BlockSpec((3, W2), lambda j, b: (0, j)), pl.BlockSpec((1, W2), lambda j, b: (0, j))],
        out_specs=[pl.BlockSpec((seq, W2), lambda j, b: (b, j)), pl.BlockSpec((3, W2), lambda j, b: (0, j)),
                   pl.BlockSpec((1, W2), lambda j, b: (0, j)), pl.BlockSpec((FF_BLK, D), lambda j, b: (j, 0))],
        out_shape=[jax.ShapeDtypeStruct((T, 2 * D_FF), BF16), jax.ShapeDtypeStruct((3, 2 * D_FF), F32),
                   jax.ShapeDtypeStruct((1, 2 * D_FF), F32), jax.ShapeDtypeStruct((D_FF, D), BF16)],
        scratch_shapes=[pltpu.VMEM((seq + HALO, W2), F32), pltpu.VMEM((FF_BLK, D), F32)],
        compiler_params=_cparams(("parallel", "arbitrary")),
    )(u, dx2, wdn, w, bias)


def _place():
    return lax.axis_index("x"), lax.axis_index("y"), lax.axis_index("c")


def _other_chips(x, y):
    return [(1 - x, y), (x, 1 - y), (1 - x, 1 - y)]


def _all_gather(vs, *, name):
    n = len(vs)

    def body(*refs):
        v_refs, out_refs = refs[:n], refs[n:2 * n]
        send_sems, recv_sems, local_sems = refs[2 * n:]
        x, y, c = _place()
        me, sibling = (x, y, c), (x, y, 1 - c)
        chips = _other_chips(x, y)

        def slab(a, px, py, pc):
            return out_refs[a].at[4 * px + 2 * py + pc]

        def copy(a, k, block, to, src=None):
            return pltpu.make_async_remote_copy(
                src_ref=slab(a, *block) if src is None else src, dst_ref=slab(a, *block),
                send_sem=send_sems.at[7 * a + k], recv_sem=recv_sems.at[7 * a + k], device_id=to, device_id_type=MESH)

        mine = [pltpu.make_async_copy(v_refs[a], slab(a, *me), local_sems.at[a]) for a in range(n)]
        for cp in mine:
            cp.start()
        first = []
        for a in range(n):
            first.append(copy(a, 0, me, sibling, src=v_refs[a]))
            first += [copy(a, 1 + j, me, (*chip, c), src=v_refs[a]) for j, chip in enumerate(chips)]
        for cp in first:
            cp.start()
        passed = []
        for j, chip in enumerate(chips):
            for a in range(n):
                copy(a, 1 + j, (*chip, c), me).wait_recv()
                cp = copy(a, 4 + j, (*chip, c), sibling)
                cp.start()
                passed.append(cp)
        for a in range(n):
            copy(a, 0, sibling, me).wait_recv()
            for j, chip in enumerate(chips):
                copy(a, 4 + j, (*chip, 1 - c), me).wait_recv()
        for cp in first + passed:
            cp.wait_send()
        for cp in mine:
            cp.wait()

    return pl.pallas_call(
        body, name=name, in_specs=[ANY] * n, out_specs=[ANY] * n,
        out_shape=[jax.ShapeDtypeStruct((N_DEV,) + v.shape, v.dtype) for v in vs],
        scratch_shapes=[pltpu.SemaphoreType.DMA((7 * n,)), pltpu.SemaphoreType.DMA((7 * n,)), pltpu.SemaphoreType.DMA((n,))],
    )(*vs)


def _all_gather_async(vs, *, name, collective_id):
    n = len(vs)
    v_refs = [jax.new_ref(v, memory_space=pltpu.MemorySpace.HBM) for v in vs]
    out_refs = [jax.empty_ref(jax.ShapeDtypeStruct((N_DEV,) + v.shape, v.dtype), memory_space=pltpu.MemorySpace.HBM)
                for v in vs]

    @pl.kernel(mesh=plsc.ScalarSubcoreMesh(axis_name="seq", num_cores=1), name=name,
               scratch_types=(pltpu.SemaphoreType.DMA((7 * n,)), pltpu.SemaphoreType.DMA((7 * n,)),
                              pltpu.SemaphoreType.DMA((n,))),
               compiler_params=pltpu.CompilerParams(collective_id=collective_id))
    def launch(send_sems, recv_sems, local_sems):
        x, y, c = _place()
        me, sibling = (x, y, c), (x, y, 1 - c)
        chips = _other_chips(x, y)
        peers = [sibling] + [(*chip, c) for chip in chips]
        barrier = pltpu.get_barrier_semaphore()
        for peer in peers:
            pl.semaphore_signal(barrier, inc=1, device_id=peer, device_id_type=MESH)
        pl.semaphore_wait(barrier, len(peers))

        def slab(a, px, py, pc):
            return out_refs[a].at[4 * px + 2 * py + pc]

        def copy(a, k, block, to, src=None):
            return pltpu.make_async_remote_copy(
                src_ref=slab(a, *block) if src is None else src, dst_ref=slab(a, *block),
                send_sem=send_sems.at[7 * a + k], recv_sem=recv_sems.at[7 * a + k], device_id=to, device_id_type=MESH)

        mine = [pltpu.make_async_copy(v_refs[a], slab(a, *me), local_sems.at[a]) for a in range(n)]
        for cp in mine:
            cp.start()
        first = []
        for a in range(n):
            first.append(copy(a, 0, me, sibling, src=v_refs[a]))
            first += [copy(a, 1 + j, me, (*chip, c), src=v_refs[a]) for j, chip in enumerate(chips)]
        for cp in first:
            cp.start()
        passed = []
        for j, chip in enumerate(chips):
            for a in range(n):
                copy(a, 1 + j, (*chip, c), me).wait_recv()
                cp = copy(a, 4 + j, (*chip, c), sibling)
                cp.start()
                passed.append(cp)
        for a in range(n):
            copy(a, 0, sibling, me).wait_recv()
            for j, chip in enumerate(chips):
                copy(a, 4 + j, (*chip, 1 - c), me).wait_recv()
        for cp in first + passed:
            cp.wait_send()
        for cp in mine:
            cp.wait()

    launch()
    return [r[...] for r in out_refs]


def _handshake(peers):
    barrier = pltpu.get_barrier_semaphore()
    for peer in peers:
        pl.semaphore_signal(barrier, inc=1, device_id=peer, device_id_type=MESH)
    pl.semaphore_wait(barrier, len(peers))


def _hbm_refs(arrays, lead):
    src = [jax.new_ref(a, memory_space=pltpu.MemorySpace.HBM) for a in arrays]
    dst = [jax.empty_ref(jax.ShapeDtypeStruct((lead,) + a.shape[1:], a.dtype), memory_space=pltpu.MemorySpace.HBM)
           for a in arrays]
    return src, dst


def _peer(x, y, c, k):
    return ((1 - x) if k & 4 else x, (1 - y) if k & 2 else y, (1 - c) if k & 1 else c)


def _rs_direct_async(g8s, *, name, collective_id):
    n = len(g8s)
    g_refs, out_refs = _hbm_refs(g8s, N_DEV - 1)

    @pl.kernel(mesh=plsc.ScalarSubcoreMesh(axis_name="seq", num_cores=1), name=name,
               scratch_types=(pltpu.SemaphoreType.DMA((7 * n,)), pltpu.SemaphoreType.DMA((7 * n,))),
               compiler_params=pltpu.CompilerParams(collective_id=collective_id))
    def launch(send_sems, recv_sems):
        x, y, c = _place()
        peers = [_peer(x, y, c, k) for k in range(1, N_DEV)]
        _handshake(peers)
        copies = [
            pltpu.make_async_remote_copy(
                src_ref=g_refs[a].at[4 * px + 2 * py + pc], dst_ref=out_refs[a].at[k],
                send_sem=send_sems.at[7 * a + k], recv_sem=recv_sems.at[7 * a + k],
                device_id=(px, py, pc), device_id_type=MESH)
            for a in range(n) for k, (px, py, pc) in enumerate(peers)]
        for cp in copies:
            cp.start()
        for cp in copies:
            cp.wait()

    launch()
    return [r[...] for r in out_refs]


def _row_tile(rows):
    if rows <= 512:
        return rows
    return next(t for t in (512, 384, 352, 256, 128) if rows % t == 0)


def _split_moves(segments, chunk):
    moves = []
    for dst, src, length in segments:
        while length > 0:
            dev, off = divmod(src, chunk)
            take = min(length, chunk - off)
            moves.append((dst, dev, off, take))
            dst, src, length = dst + take, src + take, length - take
    return moves


def _assemble(stacked, segments, zero_spans, out_cols, *, name):
    _, R, c = stacked.shape
    tr = _row_tile(R)
    moves = _split_moves(segments, c)

    def body(x_ref, o_ref):
        for dst, dev, off, take in moves:
            o_ref[:, dst:dst + take] = x_ref[dev, :, off:off + take]
        for a, b in zero_spans:
            o_ref[:, a:b] = jnp.zeros((tr, b - a), o_ref.dtype)

    return pl.pallas_call(
        body, name=name, grid=(R // tr,),
        in_specs=[pl.BlockSpec((N_DEV, tr, c), lambda i: (0, i, 0))],
        out_specs=pl.BlockSpec((tr, out_cols), lambda i: (i, 0)),
        out_shape=jax.ShapeDtypeStruct((R, out_cols), stacked.dtype),
        compiler_params=_cparams(("parallel",)),
    )(stacked)


def _disassemble(full, segments, chunk, *, name, out_dtype=F32):
    R = full.shape[0]
    tr = _row_tile(R)
    moves = _split_moves(segments, chunk)

    def body(x_ref, o_ref):
        seen = set()
        for dst, dev, off, take in moves:
            piece = x_ref[:, dst:dst + take]
            if (dev, off) in seen:
                piece = piece + o_ref[dev, :, off:off + take]
            seen.add((dev, off))
            o_ref[dev, :, off:off + take] = piece.astype(out_dtype)

    return pl.pallas_call(
        body, name=name, grid=(R // tr,),
        in_specs=[pl.BlockSpec((tr, full.shape[1]), lambda i: (i, 0))],
        out_specs=pl.BlockSpec((N_DEV, tr, chunk), lambda i: (0, i, 0)),
        out_shape=jax.ShapeDtypeStruct((N_DEV, R, chunk), out_dtype),
        compiler_params=_cparams(("parallel",)),
    )(full)


def _assemble_rows(stacked, segments, zero_spans, out_rows, *, name):
    _, c, R = stacked.shape
    tc = next(t for t in (2 * LANES, LANES) if R % t == 0)
    moves = _split_moves(segments, c)

    def body(x_ref, o_ref):
        for dst, dev, off, take in moves:
            o_ref[dst:dst + take, :] = x_ref[dev, off:off + take, :]
        for a, b in zero_spans:
            o_ref[a:b, :] = jnp.zeros((b - a, tc), o_ref.dtype)

    return pl.pallas_call(
        body, name=name, grid=(R // tc,),
        in_specs=[pl.BlockSpec((N_DEV, c, tc), lambda i: (0, 0, i))],
        out_specs=pl.BlockSpec((out_rows, tc), lambda i: (0, i)),
        out_shape=jax.ShapeDtypeStruct((out_rows, R), stacked.dtype),
        compiler_params=_cparams(("parallel",)),
    )(stacked)


def _disassemble_rows(full_t, segments, chunk, *, name, out_dtype=F32):
    R = full_t.shape[1]
    tc = next(t for t in (2 * LANES, LANES) if R % t == 0)
    moves = _split_moves(segments, chunk)

    def body(x_ref, o_ref):
        seen = set()
        for dst, dev, off, take in moves:
            piece = x_ref[dst:dst + take, :]
            if (dev, off) in seen:
                piece = piece + o_ref[dev, off:off + take, :]
            seen.add((dev, off))
            o_ref[dev, off:off + take, :] = piece.astype(out_dtype)

    return pl.pallas_call(
        body, name=name, grid=(R // tc,),
        in_specs=[pl.BlockSpec((full_t.shape[0], tc), lambda i: (0, i))],
        out_specs=pl.BlockSpec((N_DEV, chunk, tc), lambda i: (0, 0, i)),
        out_shape=jax.ShapeDtypeStruct((N_DEV, chunk, R), out_dtype),
        compiler_params=_cparams(("parallel",)),
    )(full_t)


_O_CQ = 3 * SB_W
_O_CKV = _O_CQ + Q_LORA
_O_KR = _O_CKV + KV_LORA
SEG_W_IN = ((0, 0, 3 * SB_W), (P_CKV, _O_CKV, KV_LORA), (P_KRT, _O_KR, MLA_ROPE), (P_KRT + MLA_ROPE, _O_KR, MLA_ROPE),
            (P_CQ, _O_CQ, Q_LORA))
ZERO_W_IN = ((P_KRT + 2 * MLA_ROPE, P_CQ),)
SEG_W_UQ = tuple((MLA_NOPE * h, MLA_QK * h, MLA_NOPE) for h in range(MLA_HEADS)) + tuple(
    (MLA_W + LANES * (h // 2) + MLA_ROPE * (h % 2), MLA_QK * h + MLA_NOPE, MLA_ROPE) for h in range(MLA_HEADS))
ZERO_W_UQ = tuple((MLA_W + LANES * g + 2 * MLA_ROPE, MLA_W + LANES * (g + 1)) for g in range(MLA_HEADS // 2))
SEG_W_UKV = tuple((MLA_NOPE * h, (MLA_NOPE + MLA_V) * h, MLA_NOPE) for h in range(MLA_HEADS)) + tuple(
    (MLA_W + MLA_V * h, (MLA_NOPE + MLA_V) * h + MLA_NOPE, MLA_V) for h in range(MLA_HEADS))
SEG_W_UP = tuple((2 * FF_BLK * blk + FF_BLK * half, D_FF * half + FF_BLK * blk, FF_BLK)
                 for half in range(2) for blk in range(N_FF_BLK))


def _sum8(g, *, name):
    _, R, C = g.shape

    def body(g_ref, o_ref):
        acc = g_ref[0]
        for k in range(1, N_DEV):
            acc = acc + g_ref[k]
        o_ref[...] = acc

    return pl.pallas_call(
        body, name=name, out_shape=jax.ShapeDtypeStruct((R, C), F32),
    )(g)


def _adamw_math(w, gf, m, v):
    c1 = 1.0 / (1.0 - ADAM_B1 ** ADAM_STEP)
    c2 = 1.0 / (1.0 - ADAM_B2 ** ADAM_STEP)
    mn = ADAM_B1 * m + (1.0 - ADAM_B1) * gf
    vn = ADAM_B2 * v + (1.0 - ADAM_B2) * (gf * gf)
    return -ADAM_LR * ((mn * c1) / (jnp.sqrt(vn * c2) + ADAM_EPS) + ADAM_WD * w), mn, vn


def _adamw(w, g, m, v, *, name):
    R, C = w.shape
    tr = _row_tile(R)

    def body(w_ref, g_ref, m_ref, v_ref, d_ref, mo_ref, vo_ref):
        d_ref[...], mo_ref[...], vo_ref[...] = _adamw_math(w_ref[...], g_ref[...], m_ref[...], v_ref[...])

    blk = pl.BlockSpec((tr, C), lambda i: (i, 0))
    shp = jax.ShapeDtypeStruct((R, C), F32)
    return pl.pallas_call(
        body, name=name, grid=(R // tr,), in_specs=[blk] * 4, out_specs=[blk] * 3,
        out_shape=[shp, shp, shp], compiler_params=_cparams(("parallel",)),
    )(w, g, m, v)


def _adamw_rs8(g8, r7, me_idx, w, m, v, *, name):
    R, C = w.shape
    tr = _row_tile(R)

    def body(i_ref, f_ref, r_ref, w_ref, m_ref, v_ref, g_ref, d_ref, mo_ref, vo_ref):
        gf = f_ref[...].astype(F32)
        for k in range(N_DEV - 1):
            gf = gf + r_ref[k].astype(F32)
        g_ref[...] = gf
        d_ref[...], mo_ref[...], vo_ref[...] = _adamw_math(w_ref[...], gf, m_ref[...], v_ref[...])

    blk = pl.BlockSpec((tr, C), lambda i, i_ref: (i, 0))
    shp = jax.ShapeDtypeStruct((R, C), F32)
    return pl.pallas_call(
        body, name=name,
        grid_spec=pltpu.PrefetchScalarGridSpec(
            num_scalar_prefetch=1, grid=(R // tr,),
            in_specs=[pl.BlockSpec((None, tr, C), lambda i, i_ref: (i_ref[0], i, 0)),
                      pl.BlockSpec((N_DEV - 1, tr, C), lambda i, i_ref: (0, i, 0)), blk, blk, blk],
            out_specs=[blk] * 4),
        out_shape=[shp] * 4, compiler_params=_cparams(("parallel",)),
    )(me_idx, g8, r7, w, m, v)


def _ff_interleave(a):
    lead = a.shape[:-1]
    return a.reshape(*lead, 2, N_FF_BLK, FF_BLK).swapaxes(-3, -2).reshape(*lead, 2 * D_FF)


def _ff_deinterleave(a):
    lead = a.shape[:-1]
    return a.reshape(*lead, N_FF_BLK, 2, FF_BLK).swapaxes(-3, -2).reshape(*lead, 2 * D_FF)


SMALL =(("g_mix", D_MODEL), ("g_cq", Q_LORA), ("g_ckv", KV_LORA), ("g_sb_out", SB_W), ("g_mla_out", MLA_W),
         ("g_ffn", D_MODEL), ("conv_b", 2 * D_FF), ("g_final", D_MODEL))
SMALL_ROWS = 88


SMALL_USED = sum(size for _, size in SMALL)


def _pack_small(d, tail=None):
    parts = [d[n].reshape(-1) for n, _ in SMALL] + ([] if tail is None else [tail])
    flat = jnp.concatenate(parts)
    flat = jnp.pad(flat, (0, SMALL_ROWS * LANES - flat.shape[0]))
    return flat.reshape(SMALL_ROWS, LANES)


def _unpack_small(a):
    flat = a.reshape(-1)
    out, off = {}, 0
    for n, size in SMALL:
        out[n] = flat[off:off + size]
        off += size
    return out


def kernel(x, positions, g_mix, w_in, g_cq, w_uq, g_ckv, w_ukv, g_sb_out, g_mla_out, w_out, g_ffn, w_up, conv_w, conv_b, w_down, g_final, loss_target, m_g_mix, m_w_in, m_g_cq, m_w_uq, m_g_ckv, m_w_ukv, m_g_sb_out, m_g_mla_out, m_w_out, m_g_ffn, m_w_up, m_conv_w, m_conv_b, m_w_down, m_g_final, v_g_mix, v_w_in, v_g_cq, v_w_uq, v_g_ckv, v_w_ukv, v_g_sb_out, v_g_mla_out, v_w_out, v_g_ffn, v_w_up, v_conv_w, v_conv_b, v_w_down, v_g_final):
    B, S, D = x.shape
    T = B * S
    xf = x.reshape(T, D)
    tgt = loss_target.reshape(T, D)
    pos = positions.reshape(T, 1)
    half = MLA_ROPE // 2
    inv_freq = 1.0 / (ROPE_BASE ** (jnp.arange(half, dtype=F32) * (2.0 / MLA_ROPE)))
    invf = jnp.tile(inv_freq, LANES // half).reshape(1, LANES)
    me_idx = (4 * lax.axis_index("x") + 2 * lax.axis_index("y") + lax.axis_index("c")).astype(jnp.int32).reshape(1)

    names = ("w_in", "w_uq", "w_ukv", "w_out", "w_up", "w_down", "conv_w")
    shard = {"w_in": w_in[0], "w_uq": w_uq[0], "w_ukv": w_ukv[0], "w_out": w_out[0], "w_up": w_up[0],
             "w_down": w_down[0], "conv_w": conv_w[0]}
    sent = {n: shard[n] if n == "conv_w" else shard[n].astype(BF16) for n in names}
    later = names[1:]
    w_in_all = _all_gather([jnp.transpose(shard["w_in"]).astype(BF16)], name="ag_w_in")[0]
    w_in_all, rest = lax.optimization_barrier((w_in_all, [sent[n] for n in later]))
    got = {"w_in": w_in_all}
    got.update(zip(later, _all_gather_async(rest, name="ag_weights_async", collective_id=0)))
    wi_t = _assemble_rows(got["w_in"], SEG_W_IN, ZERO_W_IN, P_COLS, name="asm_w_in")
    wuq = _assemble(got["w_uq"], SEG_W_UQ, ZERO_W_UQ, 2 * MLA_W, name="asm_w_uq")
    wukv = _assemble(got["w_ukv"], SEG_W_UKV, (), 2 * MLA_W, name="asm_w_ukv")
    wup = _assemble(got["w_up"], SEG_W_UP, (), 2 * D_FF, name="asm_w_up")
    cwi = _assemble(got["conv_w"], SEG_W_UP, (), 2 * D_FF, name="asm_conv_w")
    wo = got["w_out"].reshape(D, D)
    wdn = got["w_down"].reshape(D_FF, D)
    cbi = _ff_interleave(conv_b)

    h, p = _rms_matmul_nn(xf, g_mix, wi_t, tm=512, name="proj_in", w_transposed=True)
    o_sb, ltot = _sb_fwd(p, seq=S, name="sb_fwd")
    cq, qm, krt = _proj_uq_rope(p, g_cq, wuq, pos, invf, tm=512, name="proj_uq")
    ckv, kvm = _rms_matmul_nn(p, g_ckv, wukv, tm=512, name="proj_ukv", col_block=P_CKV // KV_LORA, out_dtype=BF16)
    o_mla, lse = _mla_fwd(qm, kvm, krt, seq=S, name="mla_fwd")
    ocat, x1 = _heads_out(o_sb, o_mla, g_sb_out, g_mla_out, wo, xf, tm=512, name="proj_out")
    hf, u = _rms_matmul_nn(x1, g_ffn, wup, tm=256, name="ffn_up")
    a = _conv_fwd(u, cwi, cbi, seq=S, name="conv_fwd")
    dx2, dg_final, loss_row = _matmul_nn_loss(a, wdn, x1, g_final.reshape(1, D), tgt, tm=512, name="ffn_down_loss")

    du, dcw, dcb, dw_down = _conv_bwd(u, dx2, wdn, cwi, cbi, seq=S, name="conv_bwd")
    dw_up_t = _matmul_tn(du, hf, tm=D_FF, tn=1024, tk=1024, name="dw_up")
    dx1, dg_ffn = _matmul_nt_rms_bwd(du, wup, x1, g_ffn, tm=512, name="d_ffn_up", residual=dx2)
    do_sb, do_mla, dg_sb, dg_mla, dw_out = _heads_out_bwd(dx1, wo, ocat, o_sb, o_mla, g_sb_out, g_mla_out, tm=512,
                                                          name="d_proj_out")

    early = ("w_down", "w_up", "conv_w", "w_out")
    g8 = {"w_up": _disassemble_rows(dw_up_t, SEG_W_UP, shard["w_up"].shape[1], name="split_dw_up", out_dtype=BF16),
          "conv_w": _disassemble(dcw, SEG_W_UP, shard["conv_w"].shape[1], name="split_dconv_w", out_dtype=BF16),
          "w_out": dw_out.reshape((N_DEV,) + shard["w_out"].shape),
          "w_down": dw_down.reshape((N_DEV,) + shard["w_down"].shape)}
    r7 = dict(zip(early, _rs_direct_async([g8[n] for n in early], name="rs_direct_async", collective_id=1)))

    dq_sb, dk_sb, dv_sb = _sb_bwd(p, ltot, do_sb, seq=S, name="sb_bwd")
    dqn, dqr, dkn, dvm, dkr = _mla_bwd(qm, kvm, krt, o_mla, lse, do_mla, seq=S, name="mla_bwd")
    dw_uq_t, dcq, dg_cq, dkr_u = _d_proj_uq_rope(dqn, dqr, dkr, wuq, cq, p, g_cq, pos, invf, tm=512, name="d_proj_uq")
    dw_ukv, dckv, dg_ckv = _d_proj_cat([dkn, dvm], wukv, p, g_ckv, tm=512, name="d_proj_ukv",
                                       col_block=P_CKV // KV_LORA, out_dtype=BF16, wgrad_act=ckv)
    mid = ("w_uq", "w_ukv")
    g8.update({"w_uq": _disassemble_rows(dw_uq_t, SEG_W_UQ, shard["w_uq"].shape[1], name="split_dw_uq", out_dtype=BF16),
               "w_ukv": _disassemble(dw_ukv, SEG_W_UKV, shard["w_ukv"].shape[1], name="split_dw_ukv", out_dtype=BF16)})
    r7.update(zip(mid, _rs_direct_async([g8[n] for n in mid], name="rs_direct_mid", collective_id=4)))
    dp, dx, dg_mix = _d_proj_cat([dq_sb, dk_sb, dv_sb, dckv, dkr_u, dcq], wi_t, xf, g_mix, tm=512, name="d_proj_in",
                                 residual=dx1, b_transposed=True)
    dw_in_t = _matmul_tn(dp, h, tm=P_COLS, tn=1024, tk=1024, name="dw_in")

    late = ("w_in",)
    g8["w_in"] = _disassemble_rows(dw_in_t, SEG_W_IN, shard["w_in"].shape[1], name="split_dw_in", out_dtype=BF16)
    tied, g8["w_in"] = lax.optimization_barrier(([r7[n] for n in early], g8["w_in"]))
    r7.update(zip(early, tied))
    r7.update(zip(late, _rs_direct_async([g8[n] for n in late], name="rs_direct_late", collective_id=3)))

    params = {"w_in": (w_in, m_w_in, v_w_in), "w_uq": (w_uq, m_w_uq, v_w_uq), "w_ukv": (w_ukv, m_w_ukv, v_w_ukv),
              "w_out": (w_out, m_w_out, v_w_out), "w_up": (w_up, m_w_up, v_w_up), "conv_w": (conv_w, m_conv_w, v_conv_w),
              "w_down": (w_down, m_w_down, v_w_down)}
    grad, delta, new_m, new_v = {}, {}, {}, {}

    transposed = ("w_in", "w_uq", "w_up")

    def adamw_group(group):
        for n in group:
            flip = jnp.transpose if n in transposed else (lambda t: t)
            w_, m_, v_ = [flip(t[0]) for t in params[n]]
            res = _adamw_rs8(g8[n], r7[n], me_idx, w_, m_, v_, name="adamw_" + n)
            grad[n], delta[n], new_m[n], new_v[n] = [flip(r)[None] for r in res]

    adamw_group(("w_down", "w_out", "conv_w"))
    small_part = {"g_mix": dg_mix, "g_cq": dg_cq, "g_ckv": dg_ckv, "g_sb_out": dg_sb, "g_mla_out": dg_mla,
                  "g_ffn": dg_ffn, "conv_b": _ff_deinterleave(dcb), "g_final": dg_final}
    small_all, = _all_gather_async([_pack_small(small_part, tail=loss_row[0, 0:1])], name="ag_small_async",
                                   collective_id=5)
    adamw_group(("w_up",))
    adamw_group(mid + late)
    gsmall = _sum8(small_all, name="sum_small_grads")
    small_w = {"g_mix": g_mix, "g_cq": g_cq, "g_ckv": g_ckv, "g_sb_out": g_sb_out, "g_mla_out": g_mla_out,
               "g_ffn": g_ffn, "conv_b": conv_b, "g_final": g_final}
    small_m = {"g_mix": m_g_mix, "g_cq": m_g_cq, "g_ckv": m_g_ckv, "g_sb_out": m_g_sb_out, "g_mla_out": m_g_mla_out,
               "g_ffn": m_g_ffn, "conv_b": m_conv_b, "g_final": m_g_final}
    small_v = {"g_mix": v_g_mix, "g_cq": v_g_cq, "g_ckv": v_g_ckv, "g_sb_out": v_g_sb_out, "g_mla_out": v_g_mla_out,
               "g_ffn": v_g_ffn, "conv_b": v_conv_b, "g_final": v_g_final}
    ds_, ms_, vs_ = _adamw(_pack_small(small_w), gsmall, _pack_small(small_m), _pack_small(small_v), name="adamw_small")
    for src, dst in ((_unpack_small(gsmall), grad), (_unpack_small(ds_), delta), (_unpack_small(ms_), new_m), (_unpack_small(vs_), new_v)):
        for n, _ in SMALL:
            dst[n] = src[n].reshape(small_w[n].shape)

    loss = gsmall.reshape(-1)[SMALL_USED]
    order = ("g_mix", "w_in", "g_cq", "w_uq", "g_ckv", "w_ukv", "g_sb_out", "g_mla_out", "w_out", "g_ffn", "w_up",
             "conv_w", "conv_b", "w_down", "g_final")
    return (loss, dx.reshape(B, S, D), *[grad[n] for n in order], *[delta[n] for n in order],
            *[new_m[n] for n in order], *[new_v[n] for n in order])
```

```python
import jax
import jax.numpy as jnp
from jax import lax
from jax.experimental import pallas as pl
from jax.experimental.pallas import tpu as pltpu
from jax.experimental.pallas import tpu_sc as plsc

F32 = jnp.float32
BF16 = jnp.bfloat16

D_MODEL = 1024
SB_HEADS = 8
SB_HEAD_DIM = 64
MLA_HEADS = 8
MLA_NOPE = 64
MLA_ROPE = 32
MLA_V = 64
Q_LORA = 384
KV_LORA = 256
D_FF = 2816
ROPE_BASE = 10000.0
EPS = 1e-6
SB_W = SB_HEADS * SB_HEAD_DIM
MLA_W = MLA_HEADS * MLA_V
MLA_QK = MLA_NOPE + MLA_ROPE

ADAM_LR = 0.001
ADAM_B1 = 0.9
ADAM_B2 = 0.999
ADAM_EPS = 1e-08
ADAM_WD = 0.01
ADAM_STEP = 10

N_DEV = 8
LANES = 128
V7X_VMEM_LIMIT = 56 * 1024 * 1024
FF_BLK = 256
N_FF_BLK = D_FF // FF_BLK

P_Q, P_K, P_V = 0, SB_W, 2 * SB_W
P_CKV = 3 * SB_W
P_KRT = P_CKV + KV_LORA
P_CQ = P_KRT + LANES
P_COLS = P_CQ + Q_LORA

MESH = pl.DeviceIdType.MESH
ANY = pl.BlockSpec(memory_space=pl.ANY)


def _cparams(sem=None, vmem=V7X_VMEM_LIMIT):
    return pltpu.CompilerParams(dimension_semantics=sem, vmem_limit_bytes=vmem)


def _matmul_tn(a, b, *, tm, tn, tk, name, out_dtype=F32):
    K, M = a.shape
    N = b.shape[1]
    assert M % tm == 0 and N % tn == 0 and K % tk == 0, (name, a.shape, b.shape)
    n_k = K // tk
    narrow = out_dtype != F32

    def body(a_ref, b_ref, o_ref, *scratch):
        acc_ref = scratch[0] if narrow else o_ref
        k = pl.program_id(2)
        part = lax.dot_general(a_ref[...].astype(BF16), b_ref[...].astype(BF16), (((0,), (0,)), ((), ())),
                               preferred_element_type=F32)

        @pl.when(k == 0)
        def _():
            acc_ref[...] = part

        @pl.when(k > 0)
        def _():
            acc_ref[...] += part

        if narrow:
            @pl.when(k == n_k - 1)
            def _():
                o_ref[...] = acc_ref[...].astype(out_dtype)

    return pl.pallas_call(
        body, name=name, grid=(M // tm, N // tn, n_k),
        in_specs=[pl.BlockSpec((tk, tm), lambda i, j, k: (k, i)), pl.BlockSpec((tk, tn), lambda i, j, k: (k, j))],
        out_specs=pl.BlockSpec((tm, tn), lambda i, j, k: (i, j)),
        out_shape=jax.ShapeDtypeStruct((M, N), out_dtype),
        scratch_shapes=[pltpu.VMEM((tm, tn), F32)] if narrow else [],
        compiler_params=_cparams(("parallel", "parallel", "arbitrary")),
    )(a, b)


def _rms(xf, g):
    r = lax.rsqrt(jnp.mean(xf * xf, axis=1, keepdims=True) + EPS)
    return (xf * r) * g


def _rms_grad(dyf, xf, g):
    r = lax.rsqrt(jnp.mean(xf * xf, axis=1, keepdims=True) + EPS)
    xh = xf * r
    dyg = dyf * g
    dx = r * (dyg - xh * jnp.mean(dyg * xh, axis=1, keepdims=True))
    return dx, jnp.sum(dyf * xh, axis=0, keepdims=True)


def _accumulate(ref, part):
    @pl.when(pl.program_id(0) == 0)
    def _():
        ref[...] = part

    @pl.when(pl.program_id(0) > 0)
    def _():
        ref[...] += part


def _rms_matmul_nn(x, g, w, *, tm, name, col_block=0, out_dtype=F32, w_transposed=False):
    T = x.shape[0]
    C, N = w.shape[::-1] if w_transposed else w.shape
    assert T % tm == 0, (name, x.shape)
    contract = (((1,), (1,)), ((), ())) if w_transposed else (((1,), (0,)), ((), ()))

    def body(x_ref, g_ref, w_ref, h_ref, o_ref):
        hb = _rms(x_ref[...], g_ref[...]).astype(BF16)
        h_ref[...] = hb
        o_ref[...] = lax.dot_general(hb, w_ref[...], contract, preferred_element_type=F32).astype(out_dtype)

    return pl.pallas_call(
        body, name=name, grid=(T // tm,),
        in_specs=[pl.BlockSpec((tm, C), lambda i: (i, col_block)), pl.BlockSpec((1, C), lambda i: (0, 0)),
                  pl.BlockSpec(w.shape, lambda i: (0, 0))],
        out_specs=[pl.BlockSpec((tm, C), lambda i: (i, 0)), pl.BlockSpec((tm, N), lambda i: (i, 0))],
        out_shape=[jax.ShapeDtypeStruct((T, C), BF16), jax.ShapeDtypeStruct((T, N), out_dtype)],
        compiler_params=_cparams(("parallel",)),
    )(x, g, w)


def _matmul_nt_rms_bwd(a, b, x, g, *, tm, name, residual=None, col_block=0, out_dtype=F32):
    M, K = a.shape
    C = b.shape[0]
    assert M % tm == 0, (name, a.shape)
    in_specs = [pl.BlockSpec((tm, K), lambda i: (i, 0)), pl.BlockSpec((C, K), lambda i: (0, 0)),
                pl.BlockSpec((tm, C), lambda i: (i, col_block)), pl.BlockSpec((1, C), lambda i: (0, 0))]
    args = [a, b, x, g]
    if residual is not None:
        in_specs.append(pl.BlockSpec((tm, C), lambda i: (i, 0)))
        args.append(residual)

    def body(*refs):
        a_ref, b_ref, x_ref, g_ref = refs[:4]
        dx_ref, dg_ref = refs[-2:]
        dy = lax.dot_general(a_ref[...].astype(BF16), b_ref[...], (((1,), (1,)), ((), ())), preferred_element_type=F32)
        dx, part = _rms_grad(dy, x_ref[...], g_ref[...])
        if residual is not None:
            dx = dx + refs[4][...]
        dx_ref[...] = dx.astype(out_dtype)
        _accumulate(dg_ref, part)

    return pl.pallas_call(
        body, name=name, grid=(M // tm,), in_specs=in_specs,
        out_specs=[pl.BlockSpec((tm, C), lambda i: (i, 0)), pl.BlockSpec((1, C), lambda i: (0, 0))],
        out_shape=[jax.ShapeDtypeStruct((M, C), out_dtype), jax.ShapeDtypeStruct((1, C), F32)],
        compiler_params=_cparams(("arbitrary",)),
    )(*args)


def _matmul_nn_loss(a, w, x1, g, tgt, *, tm, name):
    M, K = a.shape
    C = w.shape[1]
    assert M % tm == 0, (name, a.shape)

    nsub = 4
    ts = tm // nsub

    def body(a_ref, w_ref, x_ref, g_ref, t_ref, dx_ref, dg_ref, loss_ref):
        gf = g_ref[...]
        wv = w_ref[...]
        rows = [slice(r * ts, (r + 1) * ts) for r in range(nsub)]
        xs = [x_ref[rw, :] + jnp.dot(a_ref[rw, :], wv, preferred_element_type=F32) for rw in rows]
        lpart, gpart = 0.0, 0.0
        for rw, xf in zip(rows, xs):
            err = _rms(xf, gf) - t_ref[rw, :]
            lpart = lpart + 0.5 * jnp.sum(jnp.mean(err * err, axis=1, keepdims=True), axis=0, keepdims=True)
            dx, gp = _rms_grad(err * (1.0 / C), xf, gf)
            dx_ref[rw, :] = dx
            gpart = gpart + gp
        _accumulate(dg_ref, gpart)
        _accumulate(loss_ref, jnp.broadcast_to(lpart, (1, LANES)))

    row = pl.BlockSpec((tm, C), lambda i: (i, 0))
    return pl.pallas_call(
        body, name=name, grid=(M // tm,),
        in_specs=[pl.BlockSpec((tm, K), lambda i: (i, 0)), pl.BlockSpec((K, C), lambda i: (0, 0)), row,
                  pl.BlockSpec((1, C), lambda i: (0, 0)), row],
        out_specs=[row, pl.BlockSpec((1, C), lambda i: (0, 0)), pl.BlockSpec((1, LANES), lambda i: (0, 0))],
        out_shape=[jax.ShapeDtypeStruct((M, C), F32), jax.ShapeDtypeStruct((1, C), F32),
                   jax.ShapeDtypeStruct((1, LANES), F32)],
        compiler_params=_cparams(("arbitrary",)),
    )(a, w, x1, g, tgt)


ATT_T = 256
ATT_PAIRS = 2
NEG_BIG = -1e30


def _lane_iota():
    return lax.broadcasted_iota(jnp.int32, (1, LANES), 1)


def _head_masks():
    first = _lane_iota() < SB_HEAD_DIM
    return first, jnp.logical_not(first)


def _pick(mask, x):
    return jnp.where(mask, x, jnp.zeros_like(x))


def _lane_value(t, lane):
    return jnp.sum(jnp.where(_lane_iota() == lane, t, 0.0), axis=1, keepdims=True)


def _split_hi_lo(x):
    hi = x.astype(BF16)
    lo = (x - hi.astype(F32)).astype(BF16)
    return jnp.concatenate([hi, lo], axis=1)


def _tri(n, kind):
    r = lax.broadcasted_iota(jnp.int32, (n, n), 0)
    c = lax.broadcasted_iota(jnp.int32, (n, n), 1)
    u = {"suffix_excl": r > c, "prefix_incl": r <= c, "prefix_excl": r < c}[kind].astype(BF16)
    return jnp.concatenate([u, u], axis=0)


def _dot_nt(a, b):
    return lax.dot_general(a, b, (((1,), (1,)), ((), ())), preferred_element_type=F32)


def _dot_tn(a, b):
    return lax.dot_general(a, b, (((0,), (0,)), ((), ())), preferred_element_type=F32)


def _dot(a, b):
    return jnp.dot(a, b, preferred_element_type=F32)


def _causal_mask(n, strict):
    r = lax.broadcasted_iota(jnp.int32, (n, n), 0)
    c = lax.broadcasted_iota(jnp.int32, (n, n), 1)
    return (c < r) if strict else (c <= r)


LOG2E = 1.4426950408889634


def _sb_logs(qh, kj, vis):
    z2 = _dot_nt(qh, kj) * LOG2E
    nk = jnp.maximum(z2, 0.0) + jnp.log2(1.0 + jnp.exp2(-jnp.abs(z2)))
    lb = z2 - nk
    if vis is not None:
        nk = jnp.where(vis, nk, 0.0)
    return lb, nk


def _sb_fwd(p, *, seq, name):
    T = p.shape[0]
    B = T // seq
    TQ = ATT_T
    nq = seq // TQ
    PP = ATT_PAIRS
    W = PP * LANES
    nstep = SB_W // W
    NH = 2 * PP

    def body(q_ref, k_ref, v_ref, o_ref, lt_ref, q_s, k_s, v_s):
        masks = _head_masks()
        q = q_ref[...] * (SB_HEAD_DIM ** -0.5)
        v = v_ref[...]
        k_s[...] = k_ref[...].astype(BF16)
        for h in range(NH):
            ps = slice((h // 2) * LANES, (h // 2 + 1) * LANES)
            hs = slice(h * LANES, (h + 1) * LANES)
            q_s[:, hs] = _pick(masks[h % 2], q[:, ps]).astype(BF16)
            v_s[:, hs] = _pick(masks[h % 2], v[:, ps]).astype(BF16)
        u_suf = _tri(TQ, "suffix_excl")
        vis = _causal_mask(TQ, True)

        def q_block(i, carry):
            q0 = pl.multiple_of(i * TQ, TQ)
            qs = [q_s[pl.ds(q0, TQ), h * LANES:(h + 1) * LANES] for h in range(NH)]

            def tile(k0, c, mask):
                rs, accs = list(c[:NH]), list(c[NH:])
                logs = [_sb_logs(qs[h], k_s[pl.ds(k0, TQ), (h // 2) * LANES:(h // 2 + 1) * LANES], mask) for h in range(NH)]
                sums = [_dot(_split_hi_lo(nk), u_suf) for _, nk in logs]
                for h in range(NH):
                    a = jnp.exp2(logs[h][0] - sums[h] - rs[h])
                    if mask is not None:
                        a = jnp.where(mask, a, 0.0)
                    accs[h // 2] = accs[h // 2] + _dot(a.astype(BF16), v_s[pl.ds(k0, TQ), h * LANES:(h + 1) * LANES])
                    rs[h] = rs[h] + jnp.sum(logs[h][1], axis=1, keepdims=True)
                return tuple(rs) + tuple(accs)

            zero = jnp.zeros((TQ, 1), F32)
            c = tile(q0, (zero,) * NH + (jnp.zeros((TQ, LANES), F32),) * PP, vis)

            def k_block(jj, c):
                return tile(pl.multiple_of((i - 1 - jj) * TQ, TQ), c, None)

            c = lax.fori_loop(0, i, k_block, c)
            for pr in range(PP):
                ps = slice(pr * LANES, (pr + 1) * LANES)
                o_ref[pl.ds(q0, TQ), ps] = c[NH + pr]
                lt_ref[pl.ds(q0, TQ), ps] = jnp.where(masks[0], c[2 * pr], c[2 * pr + 1])
            return carry

        lax.fori_loop(0, nq, q_block, 0)

    blk = lambda off: pl.BlockSpec((seq, W), lambda b, g: (b, off + g))
    out_blk = pl.BlockSpec((seq, W), lambda b, g: (b, g))
    return pl.pallas_call(
        body, name=name, grid=(B, nstep),
        in_specs=[blk(P_Q // W), blk(P_K // W), blk(P_V // W)],
        out_specs=[out_blk, out_blk],
        out_shape=[jax.ShapeDtypeStruct((T, SB_W), F32), jax.ShapeDtypeStruct((T, SB_W), F32)],
        scratch_shapes=[pltpu.VMEM((seq, NH * LANES), BF16), pltpu.VMEM((seq, W), BF16), pltpu.VMEM((seq, NH * LANES), BF16)],
        compiler_params=_cparams(("parallel", "parallel")),
    )(p, p, p)


def _sb_bwd(p, ltot, do, *, seq, name):
    T = p.shape[0]
    B = T // seq
    TQ = ATT_T
    nq = seq // TQ
    PP = ATT_PAIRS
    W = PP * LANES
    nstep = SB_W // W
    NH = 2 * PP
    scale = SB_HEAD_DIM ** -0.5

    def body(q_ref, k_ref, v_ref, lt_ref, do_ref, dq_ref, dk_ref, dv_ref, q_s, k_s, v_s, do_s, dk_s, dv_s):
        masks = _head_masks()
        q = q_ref[...] * scale
        dof = do_ref[...]
        k_s[...] = k_ref[...].astype(BF16)
        v_s[...] = v_ref[...].astype(BF16)
        for h in range(NH):
            ps = slice((h // 2) * LANES, (h // 2 + 1) * LANES)
            hs = slice(h * LANES, (h + 1) * LANES)
            q_s[:, hs] = _pick(masks[h % 2], q[:, ps]).astype(BF16)
            do_s[:, hs] = _pick(masks[h % 2], dof[:, ps]).astype(BF16)
        dk_s[...] = jnp.zeros_like(dk_s)
        dv_s[...] = jnp.zeros_like(dv_s)
        u_pin = _tri(TQ, "prefix_incl")
        u_pex = _tri(TQ, "prefix_excl")[:TQ]
        vis = _causal_mask(TQ, True)

        def q_block(i, carry):
            q0 = pl.multiple_of(i * TQ, TQ)
            qs = [q_s[pl.ds(q0, TQ), h * LANES:(h + 1) * LANES] for h in range(NH)]
            dos = [do_s[pl.ds(q0, TQ), h * LANES:(h + 1) * LANES] for h in range(NH)]
            lt = lt_ref[pl.ds(q0, TQ), :]
            lts = [_lane_value(lt[:, (h // 2) * LANES:(h // 2 + 1) * LANES], (h % 2) * SB_HEAD_DIM) for h in range(NH)]

            def tile(k0, c, mask):
                cs, gs, accs = list(c[:NH]), list(c[NH:2 * NH]), list(c[2 * NH:])
                kjs = [k_s[pl.ds(k0, TQ), pr * LANES:(pr + 1) * LANES] for pr in range(PP)]
                vjs = [v_s[pl.ds(k0, TQ), pr * LANES:(pr + 1) * LANES] for pr in range(PP)]
                logs = [_sb_logs(qs[h], kjs[h // 2], mask) for h in range(NH)]
                pins = [_dot(_split_hi_lo(nk), u_pin) for _, nk in logs]
                das = [_dot_nt(dos[h], vjs[h // 2]) for h in range(NH)]
                a_l, g_l = [], []
                for h in range(NH):
                    a = jnp.exp2(logs[h][0] - ((lts[h] - cs[h]) - pins[h]))
                    if mask is not None:
                        a = jnp.where(mask, a, 0.0)
                    a_l.append(a)
                    g_l.append(das[h] * a)
                pres = [_dot(g.astype(BF16), u_pex) for g in g_l]
                dz_l = []
                for h in range(NH):
                    dz = g_l[h] - jnp.exp2(logs[h][0]) * (g_l[h] + (pres[h] + gs[h]))
                    if mask is not None:
                        dz = jnp.where(mask, dz, 0.0)
                    dz_l.append(dz.astype(BF16))
                for h in range(NH):
                    accs[h] = accs[h] + _dot(dz_l[h], kjs[h // 2])
                for pr in range(PP):
                    ps = slice(pr * LANES, (pr + 1) * LANES)
                    ha, hb = 2 * pr, 2 * pr + 1
                    dk_s[ps, pl.ds(k0, TQ)] += _dot_tn(qs[ha], dz_l[ha]) + _dot_tn(qs[hb], dz_l[hb])
                    dv_s[ps, pl.ds(k0, TQ)] += _dot_tn(dos[ha], a_l[ha].astype(BF16)) + _dot_tn(dos[hb], a_l[hb].astype(BF16))
                for h in range(NH):
                    cs[h] = cs[h] + jnp.sum(logs[h][1], axis=1, keepdims=True)
                    gs[h] = gs[h] + jnp.sum(g_l[h], axis=1, keepdims=True)
                return tuple(cs) + tuple(gs) + tuple(accs)

            z1 = jnp.zeros((TQ, 1), F32)
            zl = jnp.zeros((TQ, LANES), F32)

            def k_block(j, c):
                return tile(pl.multiple_of(j * TQ, TQ), c, None)

            c = lax.fori_loop(0, i, k_block, (z1,) * (2 * NH) + (zl,) * NH)
            c = tile(q0, c, vis)
            for pr in range(PP):
                dq = jnp.where(masks[0], c[2 * NH + 2 * pr], c[2 * NH + 2 * pr + 1]) * scale
                dq_ref[pl.ds(q0, TQ), pr * LANES:(pr + 1) * LANES] = dq.astype(BF16)
            return carry

        lax.fori_loop(0, nq, q_block, 0)
        dk_ref[...] = dk_s[...].T.astype(BF16)
        dv_ref[...] = dv_s[...].T.astype(BF16)

    blk = lambda off: pl.BlockSpec((seq, W), lambda b, g: (b, off + g))
    out_blk = pl.BlockSpec((seq, W), lambda b, g: (b, g))
    return pl.pallas_call(
        body, name=name, grid=(B, nstep),
        in_specs=[blk(P_Q // W), blk(P_K // W), blk(P_V // W), out_blk, out_blk],
        out_specs=[out_blk, out_blk, out_blk],
        out_shape=[jax.ShapeDtypeStruct((T, SB_W), BF16) for _ in range(3)],
        scratch_shapes=[pltpu.VMEM((seq, NH * LANES), BF16), pltpu.VMEM((seq, W), BF16), pltpu.VMEM((seq, W), BF16),
                        pltpu.VMEM((seq, NH * LANES), BF16), pltpu.VMEM((W, seq), F32), pltpu.VMEM((W, seq), F32)],
        compiler_params=_cparams(("parallel", "parallel")),
    )(p, p, p, ltot, do)


def _mla_masks():
    lane = lax.broadcasted_iota(jnp.int32, (1, 2 * LANES), 1)
    ma = (lane < MLA_NOPE) | ((lane >= LANES) & (lane < LANES + MLA_ROPE))
    mb = ((lane >= MLA_NOPE) & (lane < LANES)) | ((lane >= LANES + MLA_ROPE) & (lane < LANES + 2 * MLA_ROPE))
    return ma, mb


def _mla_fwd(qm, kvm, krt, *, seq, name):
    T = qm.shape[0]
    B = T // seq
    TQ = ATT_T
    nq = seq // TQ
    PP = ATT_PAIRS
    W = PP * LANES
    nstep = MLA_W // W
    NH = 2 * PP
    CW = 2 * LANES
    scale = MLA_QK ** -0.5

    def body(qn_ref, qr_ref, kn_ref, v_ref, kr_ref, o_ref, lse_ref, q_s, kc_s, v_s):
        hm = _head_masks()
        mm = _mla_masks()
        v = v_ref[...]
        for pr in range(PP):
            ps = slice(pr * LANES, (pr + 1) * LANES)
            qc = jnp.concatenate([qn_ref[:, ps], qr_ref[:, ps]], axis=1)
            kc_s[:, pr * CW:(pr + 1) * CW] = jnp.concatenate([kn_ref[:, ps], kr_ref[...]], axis=1)
            for e in range(2):
                h = 2 * pr + e
                q_s[:, h * CW:(h + 1) * CW] = _pick(mm[e], qc)
                v_s[:, h * LANES:(h + 1) * LANES] = _pick(hm[e], v[:, ps])
        vis = _causal_mask(TQ, False)

        def q_block(i, carry):
            q0 = pl.multiple_of(i * TQ, TQ)
            qs = [q_s[pl.ds(q0, TQ), h * CW:(h + 1) * CW] for h in range(NH)]

            def tile(k0, c, mask):
                ms, ls, accs = list(c[:NH]), list(c[NH:2 * NH]), list(c[2 * NH:])
                ss = [_dot_nt(qs[h], kc_s[pl.ds(k0, TQ), (h // 2) * CW:(h // 2 + 1) * CW]) * scale for h in range(NH)]
                if mask is not None:
                    ss = [jnp.where(mask, s, NEG_BIG) for s in ss]
                m_new = [jnp.maximum(ms[h], jnp.max(ss[h], axis=1, keepdims=True)) for h in range(NH)]
                alphas = [jnp.exp(ms[h] - m_new[h]) for h in range(NH)]
                prs = [jnp.exp(ss[h] - m_new[h]) for h in range(NH)]
                outs = [_dot(prs[h].astype(BF16), v_s[pl.ds(k0, TQ), h * LANES:(h + 1) * LANES]) for h in range(NH)]
                ls = [alphas[h] * ls[h] + jnp.sum(prs[h], axis=1, keepdims=True) for h in range(NH)]
                for pr in range(PP):
                    accs[pr] = accs[pr] * jnp.where(hm[0], alphas[2 * pr], alphas[2 * pr + 1]) + outs[2 * pr] + outs[2 * pr + 1]
                return tuple(m_new) + tuple(ls) + tuple(accs)

            neg = jnp.full((TQ, 1), NEG_BIG, F32)
            z1 = jnp.zeros((TQ, 1), F32)

            def k_block(j, c):
                return tile(pl.multiple_of(j * TQ, TQ), c, None)

            c = lax.fori_loop(0, i, k_block, (neg,) * NH + (z1,) * NH + (jnp.zeros((TQ, LANES), F32),) * PP)
            c = tile(q0, c, vis)
            for pr in range(PP):
                ps = slice(pr * LANES, (pr + 1) * LANES)
                m_a, m_b, l_a, l_b = c[2 * pr], c[2 * pr + 1], c[NH + 2 * pr], c[NH + 2 * pr + 1]
                o_ref[pl.ds(q0, TQ), ps] = c[2 * NH + pr] / jnp.where(hm[0], l_a, l_b)
                lse_ref[pl.ds(q0, TQ), ps] = jnp.where(hm[0], m_a + jnp.log(l_a), m_b + jnp.log(l_b))
            return carry

        lax.fori_loop(0, nq, q_block, 0)

    blk = lambda off: pl.BlockSpec((seq, W), lambda b, g: (b, off + g))
    out_blk = pl.BlockSpec((seq, W), lambda b, g: (b, g))
    return pl.pallas_call(
        body, name=name, grid=(B, nstep),
        in_specs=[blk(0), blk(nstep), blk(0), blk(nstep), pl.BlockSpec((seq, LANES), lambda b, g: (b, 0))],
        out_specs=[out_blk, out_blk],
        out_shape=[jax.ShapeDtypeStruct((T, MLA_W), F32), jax.ShapeDtypeStruct((T, MLA_W), F32)],
        scratch_shapes=[pltpu.VMEM((seq, NH * CW), BF16), pltpu.VMEM((seq, PP * CW), BF16), pltpu.VMEM((seq, NH * LANES), BF16)],
        compiler_params=_cparams(("parallel", "parallel")),
    )(qm, qm, kvm, kvm, krt)


def _mla_bwd(qm, kvm, krt, o, lse, do, *, seq, name):
    T = qm.shape[0]
    B = T // seq
    TQ = ATT_T
    nq = seq // TQ
    PP = ATT_PAIRS
    W = PP * LANES
    nstep = MLA_W // W
    NH = 2 * PP
    CW = 2 * LANES
    scale = MLA_QK ** -0.5

    def body(qn_ref, qr_ref, kn_ref, v_ref, kr_ref, o_ref, lse_ref, do_ref,
             dqn_ref, dqr_ref, dkn_ref, dv_ref, dkr_ref, q_s, kc_s, do_s, dkc_s, dv_s):
        hm = _head_masks()
        mm = _mla_masks()
        dof = do_ref[...]
        for pr in range(PP):
            ps = slice(pr * LANES, (pr + 1) * LANES)
            qc = jnp.concatenate([qn_ref[:, ps], qr_ref[:, ps]], axis=1)
            kc_s[:, pr * CW:(pr + 1) * CW] = jnp.concatenate([kn_ref[:, ps], kr_ref[...]], axis=1)
            for e in range(2):
                h = 2 * pr + e
                q_s[:, h * CW:(h + 1) * CW] = _pick(mm[e], qc)
                do_s[:, h * LANES:(h + 1) * LANES] = _pick(hm[e], dof[:, ps]).astype(BF16)
        dkc_s[...] = jnp.zeros_like(dkc_s)
        dv_s[...] = jnp.zeros_like(dv_s)
        vis = _causal_mask(TQ, False)

        def q_block(i, carry):
            q0 = pl.multiple_of(i * TQ, TQ)
            qs = [q_s[pl.ds(q0, TQ), h * CW:(h + 1) * CW] for h in range(NH)]
            dos = [do_s[pl.ds(q0, TQ), h * LANES:(h + 1) * LANES] for h in range(NH)]
            lse_t = lse_ref[pl.ds(q0, TQ), :]
            dd = do_ref[pl.ds(q0, TQ), :] * o_ref[pl.ds(q0, TQ), :]
            lses, ds_ = [], []
            for h in range(NH):
                ps = slice((h // 2) * LANES, (h // 2 + 1) * LANES)
                lses.append(_lane_value(lse_t[:, ps], (h % 2) * MLA_V))
                ds_.append(jnp.sum(_pick(hm[h % 2], dd[:, ps]), axis=1, keepdims=True))

            def tile(k0, c, mask):
                accs = list(c)
                kcs = [kc_s[pl.ds(k0, TQ), pr * CW:(pr + 1) * CW] for pr in range(PP)]
                vjs = [v_ref[pl.ds(k0, TQ), pr * LANES:(pr + 1) * LANES] for pr in range(PP)]
                ss = [_dot_nt(qs[h], kcs[h // 2]) * scale for h in range(NH)]
                dps = [_dot_nt(dos[h], vjs[h // 2]) for h in range(NH)]
                p_l, ds_l = [], []
                for h in range(NH):
                    pr_ = jnp.exp(ss[h] - lses[h])
                    if mask is not None:
                        pr_ = jnp.where(mask, pr_, 0.0)
                    p_l.append(pr_.astype(BF16))
                    ds_l.append((pr_ * (dps[h] - ds_[h]) * scale).astype(BF16))
                for h in range(NH):
                    accs[h] = accs[h] + _dot(ds_l[h], kcs[h // 2])
                for pr in range(PP):
                    ha, hb = 2 * pr, 2 * pr + 1
                    dkc_s[pl.ds(k0, TQ), pr * CW:(pr + 1) * CW] += _dot_tn(ds_l[ha], qs[ha]) + _dot_tn(ds_l[hb], qs[hb])
                    dv_s[pr * LANES:(pr + 1) * LANES, pl.ds(k0, TQ)] += _dot_tn(dos[ha], p_l[ha]) + _dot_tn(dos[hb], p_l[hb])
                return tuple(accs)

            zc = jnp.zeros((TQ, CW), F32)

            def k_block(j, c):
                return tile(pl.multiple_of(j * TQ, TQ), c, None)

            c = lax.fori_loop(0, i, k_block, (zc,) * NH)
            c = tile(q0, c, vis)
            for pr in range(PP):
                ps = slice(pr * LANES, (pr + 1) * LANES)
                dq = _pick(mm[0], c[2 * pr]) + _pick(mm[1], c[2 * pr + 1])
                dqn_ref[pl.ds(q0, TQ), ps] = dq[:, :LANES].astype(BF16)
                dqr_ref[pl.ds(q0, TQ), ps] = dq[:, LANES:]
            return carry

        lax.fori_loop(0, nq, q_block, 0)
        dkr = dkc_s[:, LANES:CW]
        for pr in range(PP):
            dkn_ref[:, pr * LANES:(pr + 1) * LANES] = dkc_s[:, pr * CW:pr * CW + LANES].astype(BF16)
            if pr > 0:
                dkr = dkr + dkc_s[:, pr * CW + LANES:(pr + 1) * CW]
        dv_ref[...] = dv_s[...].T.astype(BF16)
        g = pl.program_id(1)

        @pl.when(g == 0)
        def _():
            dkr_ref[...] = dkr

        @pl.when(g > 0)
        def _():
            dkr_ref[...] += dkr

    blk = lambda off: pl.BlockSpec((seq, W), lambda b, g: (b, off + g))
    out_blk = pl.BlockSpec((seq, W), lambda b, g: (b, g))
    one_blk = pl.BlockSpec((seq, LANES), lambda b, g: (b, 0))
    return pl.pallas_call(
        body, name=name, grid=(B, nstep),
        in_specs=[blk(0), blk(nstep), blk(0), blk(nstep), one_blk, out_blk, out_blk, out_blk],
        out_specs=[out_blk, out_blk, out_blk, out_blk, one_blk],
        out_shape=[jax.ShapeDtypeStruct((T, MLA_W), BF16), jax.ShapeDtypeStruct((T, MLA_W), F32),
                   jax.ShapeDtypeStruct((T, MLA_W), BF16), jax.ShapeDtypeStruct((T, MLA_W), BF16),
                   jax.ShapeDtypeStruct((T, LANES), F32)],
        scratch_shapes=[pltpu.VMEM((seq, NH * CW), BF16), pltpu.VMEM((seq, PP * CW), BF16), pltpu.VMEM((seq, NH * LANES), BF16),
                        pltpu.VMEM((seq, PP * CW), F32), pltpu.VMEM((W, seq), F32)],
        compiler_params=_cparams(("parallel", "arbitrary")),
    )(qm, qm, kvm, kvm, krt, o, lse, do)


def _rope_tables(pos_ref, invf_ref):
    ang = pos_ref[...].astype(F32) * invf_ref[...]
    first = (_lane_iota() % MLA_ROPE) < (MLA_ROPE // 2)
    return jnp.cos(ang), jnp.sin(ang), first


def _rope_apply(x, cos, sin, first):
    rot = jnp.where(first, -pltpu.roll(x, LANES - MLA_ROPE // 2, 1), pltpu.roll(x, MLA_ROPE // 2, 1))
    return x * cos + rot * sin


def _rope_apply_t(dy, cos, sin, first):
    dys = dy * sin
    rot_t = jnp.where(first, pltpu.roll(dys, LANES - MLA_ROPE // 2, 1), -pltpu.roll(dys, MLA_ROPE // 2, 1))
    return dy * cos + rot_t


def _proj_mla(p, g, wuq, gkv, wukv, pos, invf, *, tm, name):
    T = p.shape[0]
    ntile = MLA_W // LANES

    def body(x_ref, xkv_ref, kr_ref, g_ref, w_ref, gkv_ref, wkv_ref, pos_ref, invf_ref,
             cq_ref, qm_ref, krt_ref, ckv_ref, kvm_ref):
        cos, sin, first = _rope_tables(pos_ref, invf_ref)
        hb = _rms(x_ref[...], g_ref[...]).astype(BF16)
        cq_ref[...] = hb
        q = jnp.dot(hb, w_ref[...], preferred_element_type=F32)
        hkv = _rms(xkv_ref[...], gkv_ref[...]).astype(BF16)
        ckv_ref[...] = hkv
        kvm_ref[...] = jnp.dot(hkv, wkv_ref[...], preferred_element_type=F32).astype(BF16)
        qm_ref[:, :MLA_W] = q[:, :MLA_W].astype(BF16)
        for t in range(ntile):
            sl = slice(MLA_W + t * LANES, MLA_W + (t + 1) * LANES)
            qm_ref[:, sl] = _rope_apply(q[:, sl], cos, sin, first).astype(BF16)
        krt_ref[...] = _rope_apply(kr_ref[...], cos, sin, first).astype(BF16)

    wide = pl.BlockSpec((tm, 2 * MLA_W), lambda i: (i, 0))
    return pl.pallas_call(
        body, name=name, grid=(T // tm,),
        in_specs=[pl.BlockSpec((tm, Q_LORA), lambda i: (i, P_CQ // Q_LORA)),
                  pl.BlockSpec((tm, KV_LORA), lambda i: (i, P_CKV // KV_LORA)),
                  pl.BlockSpec((tm, LANES), lambda i: (i, P_KRT // LANES)),
                  pl.BlockSpec((1, Q_LORA), lambda i: (0, 0)), pl.BlockSpec((Q_LORA, 2 * MLA_W), lambda i: (0, 0)),
                  pl.BlockSpec((1, KV_LORA), lambda i: (0, 0)), pl.BlockSpec((KV_LORA, 2 * MLA_W), lambda i: (0, 0)),
                  pl.BlockSpec((tm, 1), lambda i: (i, 0)), pl.BlockSpec((1, LANES), lambda i: (0, 0))],
        out_specs=[pl.BlockSpec((tm, Q_LORA), lambda i: (i, 0)), wide, pl.BlockSpec((tm, LANES), lambda i: (i, 0)),
                   pl.BlockSpec((tm, KV_LORA), lambda i: (i, 0)), wide],
        out_shape=[jax.ShapeDtypeStruct((T, Q_LORA), BF16), jax.ShapeDtypeStruct((T, 2 * MLA_W), BF16),
                   jax.ShapeDtypeStruct((T, LANES), BF16), jax.ShapeDtypeStruct((T, KV_LORA), BF16),
                   jax.ShapeDtypeStruct((T, 2 * MLA_W), BF16)],
        compiler_params=_cparams(("parallel",)),
    )(p, p, p, g, wuq, gkv, wukv, pos, invf)


def _d_proj_uq_rope(dqn, dqr, dkr, wuq, cq, p, g, pos, invf, *, tm, name):
    T = dqn.shape[0]
    ntile = MLA_W // LANES

    def body(dqn_ref, dqr_ref, dkr_ref, w_ref, cq_ref, x_ref, g_ref, pos_ref, invf_ref,
             dw_ref, dx_ref, dg_ref, dkr_o_ref, dqm_s):
        cos, sin, first = _rope_tables(pos_ref, invf_ref)
        dqm_s[:, :MLA_W] = dqn_ref[...]
        for t in range(ntile):
            sl = slice(t * LANES, (t + 1) * LANES)
            dqm_s[:, MLA_W + t * LANES:MLA_W + (t + 1) * LANES] = _rope_apply_t(dqr_ref[:, sl], cos, sin, first).astype(BF16)
        dkr_o_ref[...] = _rope_apply_t(dkr_ref[...], cos, sin, first).astype(BF16)
        dqm = dqm_s[...]
        dy = lax.dot_general(dqm, w_ref[...], (((1,), (1,)), ((), ())), preferred_element_type=F32)
        dx, part = _rms_grad(dy, x_ref[...], g_ref[...])
        dx_ref[...] = dx.astype(BF16)
        _accumulate(dg_ref, part)
        _accumulate(dw_ref, _dot_tn(dqm, cq_ref[...]))

    half = pl.BlockSpec((tm, MLA_W), lambda i: (i, 0))
    tile = pl.BlockSpec((tm, LANES), lambda i: (i, 0))
    lat = pl.BlockSpec((tm, Q_LORA), lambda i: (i, 0))
    return pl.pallas_call(
        body, name=name, grid=(T // tm,),
        in_specs=[half, half, tile, pl.BlockSpec((Q_LORA, 2 * MLA_W), lambda i: (0, 0)), lat,
                  pl.BlockSpec((tm, Q_LORA), lambda i: (i, P_CQ // Q_LORA)), pl.BlockSpec((1, Q_LORA), lambda i: (0, 0)),
                  pl.BlockSpec((tm, 1), lambda i: (i, 0)), pl.BlockSpec((1, LANES), lambda i: (0, 0))],
        out_specs=[pl.BlockSpec((2 * MLA_W, Q_LORA), lambda i: (0, 0)), lat,
                   pl.BlockSpec((1, Q_LORA), lambda i: (0, 0)), tile],
        out_shape=[jax.ShapeDtypeStruct((2 * MLA_W, Q_LORA), F32), jax.ShapeDtypeStruct((T, Q_LORA), BF16),
                   jax.ShapeDtypeStruct((1, Q_LORA), F32), jax.ShapeDtypeStruct((T, LANES), BF16)],
        scratch_shapes=[pltpu.VMEM((tm, 2 * MLA_W), BF16)],
        compiler_params=_cparams(("arbitrary",)),
    )(dqn, dqr, dkr, wuq, cq, p, g, pos, invf)


def _d_proj_cat(pieces, b, x, g, *, tm, name, residual=None, col_block=0, out_dtype=F32, b_transposed=False,
                wgrad_act=None):
    M = pieces[0].shape[0]
    widths = [pc.shape[1] for pc in pieces]
    K = sum(widths)
    C = b.shape[1] if b_transposed else b.shape[0]
    n = len(pieces)
    fuse_w = wgrad_act is not None
    contract = (((1,), (0,)), ((), ())) if b_transposed else (((1,), (1,)), ((), ()))
    in_specs = [pl.BlockSpec((tm, w), lambda i: (i, 0)) for w in widths]
    in_specs += [pl.BlockSpec(b.shape, lambda i: (0, 0)), pl.BlockSpec((tm, C), lambda i: (i, col_block)),
                 pl.BlockSpec((1, C), lambda i: (0, 0))]
    args = list(pieces) + [b, x, g]
    if residual is not None:
        in_specs.append(pl.BlockSpec((tm, C), lambda i: (i, 0)))
        args.append(residual)
    if fuse_w:
        in_specs.append(pl.BlockSpec((tm, C), lambda i: (i, 0)))
        args.append(wgrad_act)

    def body(*refs):
        b_ref, x_ref, g_ref = refs[n:n + 3]
        first_ref, dx_ref, dg_ref = refs[-4:-1] if fuse_w else refs[-3:]
        cat_ref = refs[-1] if fuse_w else first_ref
        off = 0
        for r, w in zip(refs[:n], widths):
            cat_ref[:, off:off + w] = r[...]
            off += w
        cat = cat_ref[...]
        dy = lax.dot_general(cat, b_ref[...], contract, preferred_element_type=F32)
        dx, part = _rms_grad(dy, x_ref[...], g_ref[...])
        if residual is not None:
            dx = dx + refs[n + 3][...]
        dx_ref[...] = dx.astype(out_dtype)
        _accumulate(dg_ref, part)
        if fuse_w:
            act_ref = refs[n + 3 + (residual is not None)]
            _accumulate(first_ref, _dot_tn(act_ref[...], cat))

    first_spec = pl.BlockSpec((C, K), lambda i: (0, 0)) if fuse_w else pl.BlockSpec((tm, K), lambda i: (i, 0))
    first_shape = jax.ShapeDtypeStruct((C, K), F32) if fuse_w else jax.ShapeDtypeStruct((M, K), BF16)
    return pl.pallas_call(
        body, name=name, grid=(M // tm,), in_specs=in_specs,
        out_specs=[first_spec, pl.BlockSpec((tm, C), lambda i: (i, 0)), pl.BlockSpec((1, C), lambda i: (0, 0))],
        out_shape=[first_shape, jax.ShapeDtypeStruct((M, C), out_dtype), jax.ShapeDtypeStruct((1, C), F32)],
        scratch_shapes=[pltpu.VMEM((tm, K), BF16)] if fuse_w else [],
        compiler_params=_cparams(("arbitrary",)),
    )(*args)


def _heads_out(xa, xb, ga, gb, w, resid, *, tm, name):
    T, C = xa.shape
    N = w.shape[1]

    def body(xa_ref, xb_ref, ga_ref, gb_ref, w_ref, r_ref, oc_ref, o_ref):
        oc_ref[:, :C] = _rms(xa_ref[...], ga_ref[...]).astype(BF16)
        oc_ref[:, C:] = _rms(xb_ref[...], gb_ref[...]).astype(BF16)
        o_ref[...] = r_ref[...] + jnp.dot(oc_ref[...], w_ref[...], preferred_element_type=F32)

    row = pl.BlockSpec((tm, C), lambda i: (i, 0))
    gsp = pl.BlockSpec((1, C), lambda i: (0, 0))
    full = pl.BlockSpec((tm, N), lambda i: (i, 0))
    return pl.pallas_call(
        body, name=name, grid=(T // tm,),
        in_specs=[row, row, gsp, gsp, pl.BlockSpec((2 * C, N), lambda i: (0, 0)), full],
        out_specs=[pl.BlockSpec((tm, 2 * C), lambda i: (i, 0)), full],
        out_shape=[jax.ShapeDtypeStruct((T, 2 * C), BF16), jax.ShapeDtypeStruct((T, N), F32)],
        compiler_params=_cparams(("parallel",)),
    )(xa, xb, ga, gb, w, resid)


def _heads_out_bwd(dout, w, ocat, xa, xb, ga, gb, *, tm, name):
    T, C = xa.shape
    N = w.shape[1]
    n_steps = T // tm

    def body(d_ref, w_ref, oc_ref, xa_ref, xb_ref, ga_ref, gb_ref, dxa_ref, dxb_ref, dga_ref, dgb_ref, dw_ref, dw_s):
        db = d_ref[...].astype(BF16)
        dy = lax.dot_general(db, w_ref[...], (((1,), (1,)), ((), ())), preferred_element_type=F32)
        dxa, pa = _rms_grad(dy[:, :C], xa_ref[...], ga_ref[...])
        dxb, pb = _rms_grad(dy[:, C:], xb_ref[...], gb_ref[...])
        dxa_ref[...] = dxa
        dxb_ref[...] = dxb
        _accumulate(dga_ref, pa)
        _accumulate(dgb_ref, pb)
        _accumulate(dw_s, _dot_tn(oc_ref[...], db))

        @pl.when(pl.program_id(0) == n_steps - 1)
        def _():
            dw_ref[...] = dw_s[...].astype(BF16)

    row = pl.BlockSpec((tm, C), lambda i: (i, 0))
    gsp = pl.BlockSpec((1, C), lambda i: (0, 0))
    wsp = pl.BlockSpec((2 * C, N), lambda i: (0, 0))
    return pl.pallas_call(
        body, name=name, grid=(n_steps,),
        in_specs=[pl.BlockSpec((tm, N), lambda i: (i, 0)), wsp, pl.BlockSpec((tm, 2 * C), lambda i: (i, 0)), row, row, gsp, gsp],
        out_specs=[row, row, gsp, gsp, wsp],
        out_shape=[jax.ShapeDtypeStruct((T, C), F32), jax.ShapeDtypeStruct((T, C), F32),
                   jax.ShapeDtypeStruct((1, C), F32), jax.ShapeDtypeStruct((1, C), F32),
                   jax.ShapeDtypeStruct((2 * C, N), BF16)],
        scratch_shapes=[pltpu.VMEM((2 * C, N), F32)],
        compiler_params=_cparams(("arbitrary",)),
    )(dout, w, ocat, xa, xb, ga, gb)


CONV_ROWS = 256
HALO = 8


def _conv_taps(w_ref):
    return w_ref[0:1, :], w_ref[1:2, :], w_ref[2:3, :]


def _conv_rows(cur, prev, w, bias):
    ext = jnp.concatenate([prev, cur], axis=0)
    u1 = pltpu.roll(ext, 1, 0)[HALO:]
    u2 = pltpu.roll(ext, 2, 0)[HALO:]
    return w[2] * cur + w[1] * u1 + w[0] * u2 + bias, u1, u2


def _conv_fwd(u, w, bias, *, seq, name):
    T = u.shape[0]
    B = T // seq
    W2 = 2 * FF_BLK

    def body(u_ref, w_ref, b_ref, a_ref):
        wv = _conv_taps(w_ref)
        bv = b_ref[...]
        for c in range(seq // CONV_ROWS):
            r0 = c * CONV_ROWS
            cur = u_ref[r0:r0 + CONV_ROWS, :]
            prev = u_ref[r0 - HALO:r0, :] if c > 0 else jnp.zeros((HALO, W2), F32)
            y, _, _ = _conv_rows(cur, prev, wv, bv)
            gc = y[:, :FF_BLK]
            a_ref[r0:r0 + CONV_ROWS, :] = (gc * (1.0 / (1.0 + jnp.exp(-gc))) * y[:, FF_BLK:]).astype(BF16)

    return pl.pallas_call(
        body, name=name, grid=(B, N_FF_BLK),
        in_specs=[pl.BlockSpec((seq, W2), lambda b, j: (b, j)), pl.BlockSpec((3, W2), lambda b, j: (0, j)),
                  pl.BlockSpec((1, W2), lambda b, j: (0, j))],
        out_specs=pl.BlockSpec((seq, FF_BLK), lambda b, j: (b, j)),
        out_shape=jax.ShapeDtypeStruct((T, D_FF), BF16),
        compiler_params=_cparams(("parallel", "parallel")),
    )(u, w, bias)


def _conv_bwd(u, dx2, wdn, w, bias, *, seq, name):
    T = u.shape[0]
    B = T // seq
    D = dx2.shape[1]
    W2 = 2 * FF_BLK
    nchunk = seq // CONV_ROWS

    def body(u_ref, dx_ref, wd_ref, w_ref, b_ref, du_ref, dw_ref, db_ref, dwd_ref, duc_s, dwd_s):
        wv = _conv_taps(w_ref)
        bv = b_ref[...]
        wd = wd_ref[...]
        zrow = jnp.zeros((1, W2), F32)
        dw0, dw1, dw2, dbs = zrow, zrow, zrow, zrow
        dwd = jnp.zeros((FF_BLK, D), F32)
        for c in range(nchunk):
            r0 = c * CONV_ROWS
            cur = u_ref[r0:r0 + CONV_ROWS, :]
            prev = u_ref[r0 - HALO:r0, :] if c > 0 else jnp.zeros((HALO, W2), F32)
            y, u1, u2 = _conv_rows(cur, prev, wv, bv)
            gc = y[:, :FF_BLK]
            vc = y[:, FF_BLK:]
            sg = 1.0 / (1.0 + jnp.exp(-gc))
            dxc = dx_ref[r0:r0 + CONV_ROWS, :].astype(BF16)
            dav = _dot_nt(dxc, wd)
            silu = gc * sg
            dwd = dwd + _dot_tn((silu * vc).astype(BF16), dxc)
            duc = jnp.concatenate([dav * vc * (sg * (1.0 + gc * (1.0 - sg))), dav * silu], axis=1)
            duc_s[r0:r0 + CONV_ROWS, :] = duc
            dw0 = dw0 + jnp.sum(duc * u2, axis=0, keepdims=True)
            dw1 = dw1 + jnp.sum(duc * u1, axis=0, keepdims=True)
            dw2 = dw2 + jnp.sum(duc * cur, axis=0, keepdims=True)
            dbs = dbs + jnp.sum(duc, axis=0, keepdims=True)
        duc_s[seq:seq + HALO, :] = jnp.zeros((HALO, W2), F32)
        n_ext = CONV_ROWS + HALO
        for c in range(nchunk):
            r0 = c * CONV_ROWS
            ext = duc_s[r0:r0 + n_ext, :]
            s1 = pltpu.roll(ext, n_ext - 1, 0)[:CONV_ROWS]
            s2 = pltpu.roll(ext, n_ext - 2, 0)[:CONV_ROWS]
            du_ref[r0:r0 + CONV_ROWS, :] = (wv[2] * ext[:CONV_ROWS] + wv[1] * s1 + wv[0] * s2).astype(BF16)

        b = pl.program_id(1)

        @pl.when(b == 0)
        def _():
            dw_ref[0:1, :] = dw0
            dw_ref[1:2, :] = dw1
            dw_ref[2:3, :] = dw2
            db_ref[...] = dbs
            dwd_s[...] = dwd

        @pl.when(b > 0)
        def _():
            dw_ref[0:1, :] += dw0
            dw_ref[1:2, :] += dw1
            dw_ref[2:3, :] += dw2
            db_ref[...] += dbs
            dwd_s[...] += dwd

        @pl.when(b == B - 1)
        def _():
            dwd_ref[...] = dwd_s[...].astype(BF16)

    return pl.pallas_call(
        body, name=name, grid=(N_FF_BLK, B),
        in_specs=[pl.BlockSpec((seq, W2), lambda j, b: (b, j)), pl.BlockSpec((seq, D), lambda j, b: (b, 0)),
                  pl.BlockSpec((FF_BLK, D), lambda j, b: (j, 0)),
                  pl.BlockSpec((3, W2), lambda j, b: (0, j)), pl.BlockSpec((1, W2), lambda j, b: (0, j))],
        out_specs=[pl.BlockSpec((seq, W2), lambda j, b: (b, j)), pl.BlockSpec((3, W2), lambda j, b: (0, j)),
                   pl.BlockSpec((1, W2), lambda j, b: (0, j)), pl.BlockSpec((FF_BLK, D), lambda j, b: (j, 0))],
        out_shape=[jax.ShapeDtypeStruct((T, 2 * D_FF), BF16), jax.ShapeDtypeStruct((3, 2 * D_FF), F32),
                   jax.ShapeDtypeStruct((1, 2 * D_FF), F32), jax.ShapeDtypeStruct((D_FF, D), BF16)],
        scratch_shapes=[pltpu.VMEM((seq + HALO, W2), F32), pltpu.VMEM((FF_BLK, D), F32)],
        compiler_params=_cparams(("parallel", "arbitrary")),
    )(u, dx2, wdn, w, bias)


def _place():
    return lax.axis_index("x"), lax.axis_index("y"), lax.axis_index("c")


def _other_chips(x, y):
    return [(1 - x, y), (x, 1 - y), (1 - x, 1 - y)]


def _all_gather(vs, *, name):
    n = len(vs)

    def body(*refs):
        v_refs, out_refs = refs[:n], refs[n:2 * n]
        send_sems, recv_sems, local_sems = refs[2 * n:]
        x, y, c = _place()
        me, sibling = (x, y, c), (x, y, 1 - c)
        chips = _other_chips(x, y)

        def slab(a, px, py, pc):
            return out_refs[a].at[4 * px + 2 * py + pc]

        def copy(a, k, block, to, src=None):
            return pltpu.make_async_remote_copy(
                src_ref=slab(a, *block) if src is None else src, dst_ref=slab(a, *block),
                send_sem=send_sems.at[7 * a + k], recv_sem=recv_sems.at[7 * a + k], device_id=to, device_id_type=MESH)

        mine = [pltpu.make_async_copy(v_refs[a], slab(a, *me), local_sems.at[a]) for a in range(n)]
        for cp in mine:
            cp.start()
        first = []
        for a in range(n):
            first.append(copy(a, 0, me, sibling, src=v_refs[a]))
            first += [copy(a, 1 + j, me, (*chip, c), src=v_refs[a]) for j, chip in enumerate(chips)]
        for cp in first:
            cp.start()
        passed = []
        for j, chip in enumerate(chips):
            for a in range(n):
                copy(a, 1 + j, (*chip, c), me).wait_recv()
                cp = copy(a, 4 + j, (*chip, c), sibling)
                cp.start()
                passed.append(cp)
        for a in range(n):
            copy(a, 0, sibling, me).wait_recv()
            for j, chip in enumerate(chips):
                copy(a, 4 + j, (*chip, 1 - c), me).wait_recv()
        for cp in first + passed:
            cp.wait_send()
        for cp in mine:
            cp.wait()

    return pl.pallas_call(
        body, name=name, in_specs=[ANY] * n, out_specs=[ANY] * n,
        out_shape=[jax.ShapeDtypeStruct((N_DEV,) + v.shape, v.dtype) for v in vs],
        scratch_shapes=[pltpu.SemaphoreType.DMA((7 * n,)), pltpu.SemaphoreType.DMA((7 * n,)), pltpu.SemaphoreType.DMA((n,))],
    )(*vs)


def _all_gather_async(vs, *, name, collective_id):
    n = len(vs)
    v_refs = [jax.new_ref(v, memory_space=pltpu.MemorySpace.HBM) for v in vs]
    out_refs = [jax.empty_ref(jax.ShapeDtypeStruct((N_DEV,) + v.shape, v.dtype), memory_space=pltpu.MemorySpace.HBM)
                for v in vs]

    @pl.kernel(mesh=plsc.ScalarSubcoreMesh(axis_name="seq", num_cores=1), name=name,
               scratch_types=(pltpu.SemaphoreType.DMA((7 * n,)), pltpu.SemaphoreType.DMA((7 * n,)),
                              pltpu.SemaphoreType.DMA((n,))),
               compiler_params=pltpu.CompilerParams(collective_id=collective_id))
    def launch(send_sems, recv_sems, local_sems):
        x, y, c = _place()
        me, sibling = (x, y, c), (x, y, 1 - c)
        chips = _other_chips(x, y)
        peers = [sibling] + [(*chip, c) for chip in chips]
        barrier = pltpu.get_barrier_semaphore()
        for peer in peers:
            pl.semaphore_signal(barrier, inc=1, device_id=peer, device_id_type=MESH)
        pl.semaphore_wait(barrier, len(peers))

        def slab(a, px, py, pc):
            return out_refs[a].at[4 * px + 2 * py + pc]

        def copy(a, k, block, to, src=None):
            return pltpu.make_async_remote_copy(
                src_ref=slab(a, *block) if src is None else src, dst_ref=slab(a, *block),
                send_sem=send_sems.at[7 * a + k], recv_sem=recv_sems.at[7 * a + k], device_id=to, device_id_type=MESH)

        mine = [pltpu.make_async_copy(v_refs[a], slab(a, *me), local_sems.at[a]) for a in range(n)]
        for cp in mine:
            cp.start()
        first = []
        for a in range(n):
            first.append(copy(a, 0, me, sibling, src=v_refs[a]))
            first += [copy(a, 1 + j, me, (*chip, c), src=v_refs[a]) for j, chip in enumerate(chips)]
        for cp in first:
            cp.start()
        passed = []
        for j, chip in enumerate(chips):
            for a in range(n):
                copy(a, 1 + j, (*chip, c), me).wait_recv()
                cp = copy(a, 4 + j, (*chip, c), sibling)
                cp.start()
                passed.append(cp)
        for a in range(n):
            copy(a, 0, sibling, me).wait_recv()
            for j, chip in enumerate(chips):
                copy(a, 4 + j, (*chip, 1 - c), me).wait_recv()
        for cp in first + passed:
            cp.wait_send()
        for cp in mine:
            cp.wait()

    launch()
    return [r[...] for r in out_refs]


def _handshake(peers):
    barrier = pltpu.get_barrier_semaphore()
    for peer in peers:
        pl.semaphore_signal(barrier, inc=1, device_id=peer, device_id_type=MESH)
    pl.semaphore_wait(barrier, len(peers))


def _hbm_refs(arrays, lead):
    src = [jax.new_ref(a, memory_space=pltpu.MemorySpace.HBM) for a in arrays]
    dst = [jax.empty_ref(jax.ShapeDtypeStruct((lead,) + a.shape[1:], a.dtype), memory_space=pltpu.MemorySpace.HBM)
           for a in arrays]
    return src, dst


def _peer(x, y, c, k):
    return ((1 - x) if k & 4 else x, (1 - y) if k & 2 else y, (1 - c) if k & 1 else c)


def _rs_direct_async(g8s, *, name, collective_id):
    n = len(g8s)
    g_refs, out_refs = _hbm_refs(g8s, N_DEV - 1)

    @pl.kernel(mesh=plsc.ScalarSubcoreMesh(axis_name="seq", num_cores=1), name=name,
               scratch_types=(pltpu.SemaphoreType.DMA((7 * n,)), pltpu.SemaphoreType.DMA((7 * n,))),
               compiler_params=pltpu.CompilerParams(collective_id=collective_id))
    def launch(send_sems, recv_sems):
        x, y, c = _place()
        peers = [_peer(x, y, c, k) for k in range(1, N_DEV)]
        _handshake(peers)
        copies = [
            pltpu.make_async_remote_copy(
                src_ref=g_refs[a].at[4 * px + 2 * py + pc], dst_ref=out_refs[a].at[k],
                send_sem=send_sems.at[7 * a + k], recv_sem=recv_sems.at[7 * a + k],
                device_id=(px, py, pc), device_id_type=MESH)
            for a in range(n) for k, (px, py, pc) in enumerate(peers)]
        for cp in copies:
            cp.start()
        for cp in copies:
            cp.wait()

    launch()
    return [r[...] for r in out_refs]


def _row_tile(rows):
    if rows <= 512:
        return rows
    return next(t for t in (512, 384, 352, 256, 128) if rows % t == 0)


def _split_moves(segments, chunk):
    moves = []
    for dst, src, length in segments:
        while length > 0:
            dev, off = divmod(src, chunk)
            take = min(length, chunk - off)
            moves.append((dst, dev, off, take))
            dst, src, length = dst + take, src + take, length - take
    return moves


def _assemble(stacked, segments, zero_spans, out_cols, *, name):
    _, R, c = stacked.shape
    tr = _row_tile(R)
    moves = _split_moves(segments, c)

    def body(x_ref, o_ref):
        for dst, dev, off, take in moves:
            o_ref[:, dst:dst + take] = x_ref[dev, :, off:off + take]
        for a, b in zero_spans:
            o_ref[:, a:b] = jnp.zeros((tr, b - a), o_ref.dtype)

    return pl.pallas_call(
        body, name=name, grid=(R // tr,),
        in_specs=[pl.BlockSpec((N_DEV, tr, c), lambda i: (0, i, 0))],
        out_specs=pl.BlockSpec((tr, out_cols), lambda i: (i, 0)),
        out_shape=jax.ShapeDtypeStruct((R, out_cols), stacked.dtype),
        compiler_params=_cparams(("parallel",)),
    )(stacked)


def _disassemble(full, segments, chunk, *, name, out_dtype=F32):
    R = full.shape[0]
    tr = _row_tile(R)
    moves = _split_moves(segments, chunk)

    def body(x_ref, o_ref):
        seen = set()
        for dst, dev, off, take in moves:
            piece = x_ref[:, dst:dst + take]
            if (dev, off) in seen:
                piece = piece + o_ref[dev, :, off:off + take]
            seen.add((dev, off))
            o_ref[dev, :, off:off + take] = piece.astype(out_dtype)

    return pl.pallas_call(
        body, name=name, grid=(R // tr,),
        in_specs=[pl.BlockSpec((tr, full.shape[1]), lambda i: (i, 0))],
        out_specs=pl.BlockSpec((N_DEV, tr, chunk), lambda i: (0, i, 0)),
        out_shape=jax.ShapeDtypeStruct((N_DEV, R, chunk), out_dtype),
        compiler_params=_cparams(("parallel",)),
    )(full)


def _assemble_rows(stacked, segments, zero_spans, out_rows, *, name):
    _, c, R = stacked.shape
    tc = next(t for t in (2 * LANES, LANES) if R % t == 0)
    moves = _split_moves(segments, c)

    def body(x_ref, o_ref):
        for dst, dev, off, take in moves:
            o_ref[dst:dst + take, :] = x_ref[dev, off:off + take, :]
        for a, b in zero_spans:
            o_ref[a:b, :] = jnp.zeros((b - a, tc), o_ref.dtype)

    return pl.pallas_call(
        body, name=name, grid=(R // tc,),
        in_specs=[pl.BlockSpec((N_DEV, c, tc), lambda i: (0, 0, i))],
        out_specs=pl.BlockSpec((out_rows, tc), lambda i: (0, i)),
        out_shape=jax.ShapeDtypeStruct((out_rows, R), stacked.dtype),
        compiler_params=_cparams(("parallel",)),
    )(stacked)


def _disassemble_rows(full_t, segments, chunk, *, name, out_dtype=F32):
    R = full_t.shape[1]
    tc = next(t for t in (2 * LANES, LANES) if R % t == 0)
    moves = _split_moves(segments, chunk)

    def body(x_ref, o_ref):
        seen = set()
        for dst, dev, off, take in moves:
            piece = x_ref[dst:dst + take, :]
            if (dev, off) in seen:
                piece = piece + o_ref[dev, off:off + take, :]
            seen.add((dev, off))
            o_ref[dev, off:off + take, :] = piece.astype(out_dtype)

    return pl.pallas_call(
        body, name=name, grid=(R // tc,),
        in_specs=[pl.BlockSpec((full_t.shape[0], tc), lambda i: (0, i))],
        out_specs=pl.BlockSpec((N_DEV, chunk, tc), lambda i: (0, 0, i)),
        out_shape=jax.ShapeDtypeStruct((N_DEV, chunk, R), out_dtype),
        compiler_params=_cparams(("parallel",)),
    )(full_t)


_O_CQ = 3 * SB_W
_O_CKV = _O_CQ + Q_LORA
_O_KR = _O_CKV + KV_LORA
SEG_W_IN = ((0, 0, 3 * SB_W), (P_CKV, _O_CKV, KV_LORA), (P_KRT, _O_KR, MLA_ROPE), (P_KRT + MLA_ROPE, _O_KR, MLA_ROPE),
            (P_CQ, _O_CQ, Q_LORA))
ZERO_W_IN = ((P_KRT + 2 * MLA_ROPE, P_CQ),)
SEG_W_UQ = tuple((MLA_NOPE * h, MLA_QK * h, MLA_NOPE) for h in range(MLA_HEADS)) + tuple(
    (MLA_W + LANES * (h // 2) + MLA_ROPE * (h % 2), MLA_QK * h + MLA_NOPE, MLA_ROPE) for h in range(MLA_HEADS))
ZERO_W_UQ = tuple((MLA_W + LANES * g + 2 * MLA_ROPE, MLA_W + LANES * (g + 1)) for g in range(MLA_HEADS // 2))
SEG_W_UKV = tuple((MLA_NOPE * h, (MLA_NOPE + MLA_V) * h, MLA_NOPE) for h in range(MLA_HEADS)) + tuple(
    (MLA_W + MLA_V * h, (MLA_NOPE + MLA_V) * h + MLA_NOPE, MLA_V) for h in range(MLA_HEADS))
SEG_W_UP = tuple((2 * FF_BLK * blk + FF_BLK * half, D_FF * half + FF_BLK * blk, FF_BLK)
                 for half in range(2) for blk in range(N_FF_BLK))


def _sum8(g, *, name):
    _, R, C = g.shape

    def body(g_ref, o_ref):
        acc = g_ref[0]
        for k in range(1, N_DEV):
            acc = acc + g_ref[k]
        o_ref[...] = acc

    return pl.pallas_call(
        body, name=name, out_shape=jax.ShapeDtypeStruct((R, C), F32),
    )(g)


def _adamw_math(w, gf, m, v):
    c1 = 1.0 / (1.0 - ADAM_B1 ** ADAM_STEP)
    c2 = 1.0 / (1.0 - ADAM_B2 ** ADAM_STEP)
    mn = ADAM_B1 * m + (1.0 - ADAM_B1) * gf
    vn = ADAM_B2 * v + (1.0 - ADAM_B2) * (gf * gf)
    return -ADAM_LR * ((mn * c1) / (jnp.sqrt(vn * c2) + ADAM_EPS) + ADAM_WD * w), mn, vn


def _adamw(w, g, m, v, *, name):
    R, C = w.shape
    tr = _row_tile(R)

    def body(w_ref, g_ref, m_ref, v_ref, d_ref, mo_ref, vo_ref):
        d_ref[...], mo_ref[...], vo_ref[...] = _adamw_math(w_ref[...], g_ref[...], m_ref[...], v_ref[...])

    blk = pl.BlockSpec((tr, C), lambda i: (i, 0))
    shp = jax.ShapeDtypeStruct((R, C), F32)
    return pl.pallas_call(
        body, name=name, grid=(R // tr,), in_specs=[blk] * 4, out_specs=[blk] * 3,
        out_shape=[shp, shp, shp], compiler_params=_cparams(("parallel",)),
    )(w, g, m, v)


def _adamw_rs8(g8, r7, me_idx, w, m, v, *, name):
    R, C = w.shape
    tr = _row_tile(R)

    def body(i_ref, f_ref, r_ref, w_ref, m_ref, v_ref, g_ref, d_ref, mo_ref, vo_ref):
        gf = f_ref[...].astype(F32)
        for k in range(N_DEV - 1):
            gf = gf + r_ref[k].astype(F32)
        g_ref[...] = gf
        d_ref[...], mo_ref[...], vo_ref[...] = _adamw_math(w_ref[...], gf, m_ref[...], v_ref[...])

    blk = pl.BlockSpec((tr, C), lambda i, i_ref: (i, 0))
    shp = jax.ShapeDtypeStruct((R, C), F32)
    return pl.pallas_call(
        body, name=name,
        grid_spec=pltpu.PrefetchScalarGridSpec(
            num_scalar_prefetch=1, grid=(R // tr,),
            in_specs=[pl.BlockSpec((None, tr, C), lambda i, i_ref: (i_ref[0], i, 0)),
                      pl.BlockSpec((N_DEV - 1, tr, C), lambda i, i_ref: (0, i, 0)), blk, blk, blk],
            out_specs=[blk] * 4),
        out_shape=[shp] * 4, compiler_params=_cparams(("parallel",)),
    )(me_idx, g8, r7, w, m, v)


def _ff_interleave(a):
    lead = a.shape[:-1]
    return a.reshape(*lead, 2, N_FF_BLK, FF_BLK).swapaxes(-3, -2).reshape(*lead, 2 * D_FF)


def _ff_deinterleave(a):
    lead = a.shape[:-1]
    return a.reshape(*lead, N_FF_BLK, 2, FF_BLK).swapaxes(-3, -2).reshape(*lead, 2 * D_FF)


SMALL =(("g_mix", D_MODEL), ("g_cq", Q_LORA), ("g_ckv", KV_LORA), ("g_sb_out", SB_W), ("g_mla_out", MLA_W),
         ("g_ffn", D_MODEL), ("conv_b", 2 * D_FF), ("g_final", D_MODEL))
SMALL_ROWS = 88


SMALL_USED = sum(size for _, size in SMALL)


def _pack_small(d, tail=None):
    parts = [d[n].reshape(-1) for n, _ in SMALL] + ([] if tail is None else [tail])
    flat = jnp.concatenate(parts)
    flat = jnp.pad(flat, (0, SMALL_ROWS * LANES - flat.shape[0]))
    return flat.reshape(SMALL_ROWS, LANES)


def _unpack_small(a):
    flat = a.reshape(-1)
    out, off = {}, 0
    for n, size in SMALL:
        out[n] = flat[off:off + size]
        off += size
    return out


def kernel(x, positions, g_mix, w_in, g_cq, w_uq, g_ckv, w_ukv, g_sb_out, g_mla_out, w_out, g_ffn, w_up, conv_w, conv_b, w_down, g_final, loss_target, m_g_mix, m_w_in, m_g_cq, m_w_uq, m_g_ckv, m_w_ukv, m_g_sb_out, m_g_mla_out, m_w_out, m_g_ffn, m_w_up, m_conv_w, m_conv_b, m_w_down, m_g_final, v_g_mix, v_w_in, v_g_cq, v_w_uq, v_g_ckv, v_w_ukv, v_g_sb_out, v_g_mla_out, v_w_out, v_g_ffn, v_w_up, v_conv_w, v_conv_b, v_w_down, v_g_final):
    B, S, D = x.shape
    T = B * S
    xf = x.reshape(T, D)
    tgt = loss_target.reshape(T, D)
    pos = positions.reshape(T, 1)
    half = MLA_ROPE // 2
    inv_freq = 1.0 / (ROPE_BASE ** (jnp.arange(half, dtype=F32) * (2.0 / MLA_ROPE)))
    invf = jnp.tile(inv_freq, LANES // half).reshape(1, LANES)
    me_idx = (4 * lax.axis_index("x") + 2 * lax.axis_index("y") + lax.axis_index("c")).astype(jnp.int32).reshape(1)

    names = ("w_in", "w_uq", "w_ukv", "w_out", "w_up", "w_down", "conv_w")
    shard = {"w_in": w_in[0], "w_uq": w_uq[0], "w_ukv": w_ukv[0], "w_out": w_out[0], "w_up": w_up[0],
             "w_down": w_down[0], "conv_w": conv_w[0]}
    sent = {n: shard[n] if n == "conv_w" else shard[n].astype(BF16) for n in names}
    later = names[1:]
    w_in_all = _all_gather([jnp.transpose(shard["w_in"]).astype(BF16)], name="ag_w_in")[0]
    w_in_all, rest = lax.optimization_barrier((w_in_all, [sent[n] for n in later]))
    got = {"w_in": w_in_all}
    got.update(zip(later, _all_gather_async(rest, name="ag_weights_async", collective_id=0)))
    wi_t = _assemble_rows(got["w_in"], SEG_W_IN, ZERO_W_IN, P_COLS, name="asm_w_in")
    wuq = _assemble(got["w_uq"], SEG_W_UQ, ZERO_W_UQ, 2 * MLA_W, name="asm_w_uq")
    wukv = _assemble(got["w_ukv"], SEG_W_UKV, (), 2 * MLA_W, name="asm_w_ukv")
    wup = _assemble(got["w_up"], SEG_W_UP, (), 2 * D_FF, name="asm_w_up")
    cwi = _assemble(got["conv_w"], SEG_W_UP, (), 2 * D_FF, name="asm_conv_w")
    wo = got["w_out"].reshape(D, D)
    wdn = got["w_down"].reshape(D_FF, D)
    cbi = _ff_interleave(conv_b)

    h, p = _rms_matmul_nn(xf, g_mix, wi_t, tm=512, name="proj_in", w_transposed=True)
    o_sb, ltot = _sb_fwd(p, seq=S, name="sb_fwd")
    cq, qm, krt, ckv, kvm = _proj_mla(p, g_cq, wuq, g_ckv, wukv, pos, invf, tm=512, name="proj_mla")
    o_mla, lse = _mla_fwd(qm, kvm, krt, seq=S, name="mla_fwd")
    ocat, x1 = _heads_out(o_sb, o_mla, g_sb_out, g_mla_out, wo, xf, tm=512, name="proj_out")
    hf, u = _rms_matmul_nn(x1, g_ffn, wup, tm=256, name="ffn_up")
    a = _conv_fwd(u, cwi, cbi, seq=S, name="conv_fwd")
    dx2, dg_final, loss_row = _matmul_nn_loss(a, wdn, x1, g_final.reshape(1, D), tgt, tm=512, name="ffn_down_loss")

    du, dcw, dcb, dw_down = _conv_bwd(u, dx2, wdn, cwi, cbi, seq=S, name="conv_bwd")
    dw_up_t = _matmul_tn(du, hf, tm=D_FF, tn=1024, tk=1024, name="dw_up")
    dx1, dg_ffn = _matmul_nt_rms_bwd(du, wup, x1, g_ffn, tm=512, name="d_ffn_up", residual=dx2)
    do_sb, do_mla, dg_sb, dg_mla, dw_out = _heads_out_bwd(dx1, wo, ocat, o_sb, o_mla, g_sb_out, g_mla_out, tm=512,
                                                          name="d_proj_out")

    early = ("w_down", "w_up", "conv_w", "w_out")
    g8 = {"w_up": _disassemble_rows(dw_up_t, SEG_W_UP, shard["w_up"].shape[1], name="split_dw_up", out_dtype=BF16),
          "conv_w": _disassemble(dcw, SEG_W_UP, shard["conv_w"].shape[1], name="split_dconv_w", out_dtype=BF16),
          "w_out": dw_out.reshape((N_DEV,) + shard["w_out"].shape),
          "w_down": dw_down.reshape((N_DEV,) + shard["w_down"].shape)}
    r7 = dict(zip(early, _rs_direct_async([g8[n] for n in early], name="rs_direct_async", collective_id=1)))

    dq_sb, dk_sb, dv_sb = _sb_bwd(p, ltot, do_sb, seq=S, name="sb_bwd")
    dqn, dqr, dkn, dvm, dkr = _mla_bwd(qm, kvm, krt, o_mla, lse, do_mla, seq=S, name="mla_bwd")
    dw_uq_t, dcq, dg_cq, dkr_u = _d_proj_uq_rope(dqn, dqr, dkr, wuq, cq, p, g_cq, pos, invf, tm=512, name="d_proj_uq")
    dw_ukv, dckv, dg_ckv = _d_proj_cat([dkn, dvm], wukv, p, g_ckv, tm=512, name="d_proj_ukv",
                                       col_block=P_CKV // KV_LORA, out_dtype=BF16, wgrad_act=ckv)
    mid = ("w_uq", "w_ukv")
    g8.update({"w_uq": _disassemble_rows(dw_uq_t, SEG_W_UQ, shard["w_uq"].shape[1], name="split_dw_uq", out_dtype=BF16),
               "w_ukv": _disassemble(dw_ukv, SEG_W_UKV, shard["w_ukv"].shape[1], name="split_dw_ukv", out_dtype=BF16)})
    r7.update(zip(mid, _rs_direct_async([g8[n] for n in mid], name="rs_direct_mid", collective_id=4)))
    dp, dx, dg_mix = _d_proj_cat([dq_sb, dk_sb, dv_sb, dckv, dkr_u, dcq], wi_t, xf, g_mix, tm=512, name="d_proj_in",
                                 residual=dx1, b_transposed=True)
    dw_in_t = _matmul_tn(dp, h, tm=P_COLS, tn=1024, tk=1024, name="dw_in")

    late = ("w_in",)
    g8["w_in"] = _disassemble_rows(dw_in_t, SEG_W_IN, shard["w_in"].shape[1], name="split_dw_in", out_dtype=BF16)
    tied, g8["w_in"] = lax.optimization_barrier(([r7[n] for n in early], g8["w_in"]))
    r7.update(zip(early, tied))
    r7.update(zip(late, _rs_direct_async([g8[n] for n in late], name="rs_direct_late", collective_id=3)))

    params = {"w_in": (w_in, m_w_in, v_w_in), "w_uq": (w_uq, m_w_uq, v_w_uq), "w_ukv": (w_ukv, m_w_ukv, v_w_ukv),
              "w_out": (w_out, m_w_out, v_w_out), "w_up": (w_up, m_w_up, v_w_up), "conv_w": (conv_w, m_conv_w, v_conv_w),
              "w_down": (w_down, m_w_down, v_w_down)}
    grad, delta, new_m, new_v = {}, {}, {}, {}

    transposed = ("w_in", "w_uq", "w_up")

    def adamw_group(group):
        for n in group:
            flip = jnp.transpose if n in transposed else (lambda t: t)
            w_, m_, v_ = [flip(t[0]) for t in params[n]]
            res = _adamw_rs8(g8[n], r7[n], me_idx, w_, m_, v_, name="adamw_" + n)
            grad[n], delta[n], new_m[n], new_v[n] = [flip(r)[None] for r in res]

    adamw_group(("w_down", "w_out", "conv_w"))
    small_part = {"g_mix": dg_mix, "g_cq": dg_cq, "g_ckv": dg_ckv, "g_sb_out": dg_sb, "g_mla_out": dg_mla,
                  "g_ffn": dg_ffn, "conv_b": _ff_deinterleave(dcb), "g_final": dg_final}
    small_all, = _all_gather_async([_pack_small(small_part, tail=loss_row[0, 0:1])], name="ag_small_async",
                                   collective_id=5)
    adamw_group(("w_up",))
    adamw_group(mid + late)
    gsmall = _sum8(small_all, name="sum_small_grads")
    small_w = {"g_mix": g_mix, "g_cq": g_cq, "g_ckv": g_ckv, "g_sb_out": g_sb_out, "g_mla_out": g_mla_out,
               "g_ffn": g_ffn, "conv_b": conv_b, "g_final": g_final}
    small_m = {"g_mix": m_g_mix, "g_cq": m_g_cq, "g_ckv": m_g_ckv, "g_sb_out": m_g_sb_out, "g_mla_out": m_g_mla_out,
               "g_ffn": m_g_ffn, "conv_b": m_conv_b, "g_final": m_g_final}
    small_v = {"g_mix": v_g_mix, "g_cq": v_g_cq, "g_ckv": v_g_ckv, "g_sb_out": v_g_sb_out, "g_mla_out": v_g_mla_out,
               "g_ffn": v_g_ffn, "conv_b": v_conv_b, "g_final": v_g_final}
    ds_, ms_, vs_ = _adamw(_pack_small(small_w), gsmall, _pack_small(small_m), _pack_small(small_v), name="adamw_small")
    for src, dst in ((_unpack_small(gsmall), grad), (_unpack_small(ds_), delta), (_unpack_small(ms_), new_m), (_unpack_small(vs_), new_v)):
        for n, _ in SMALL:
            dst[n] = src[n].reshape(small_w[n].shape)

    loss = gsmall.reshape(-1)[SMALL_USED]
    order = ("g_mix", "w_in", "g_cq", "w_uq", "g_ckv", "w_ukv", "g_sb_out", "g_mla_out", "w_out", "g_ffn", "w_up",
             "conv_w", "conv_b", "w_down", "g_final")
    return (loss, dx.reshape(B, S, D), *[grad[n] for n in order], *[delta[n] for n in order],
            *[new_m[n] for n in order], *[new_v[n] for n in order])
```
